```python
import jax, jax.numpy as jnp
from jax import lax
import numpy as np

D_MODEL = 1024
BATCH = 8
SEQ = 2048
DEPTH = 1

CHUNK = 64
MIX_WIDTH = D_MODEL
RET_WIDTH = MIX_WIDTH // 2
RET_HEADS = 4
RET_DV = RET_WIDTH // RET_HEADS
RET_DK = RET_DV // 2
RET_QK = RET_HEADS * RET_DK
RWKV_WIDTH = MIX_WIDTH - RET_WIDTH
RWKV_HEAD = 64
RWKV_HEADS = RWKV_WIDTH // RWKV_HEAD
LORA_W = 64
LORA_A = 64
RET_COLS = 2 * RET_QK + 2 * RET_WIDTH
RWKV_COLS = 4 * RWKV_WIDTH + LORA_W + LORA_A
IN_COLS = RET_COLS + RWKV_COLS
ROPE_BASE = 10000.0
RMS_EPS = 1e-6
RET_GN_EPS = 1e-5
RWKV_GN_EPS = 64e-5

kernel_name = "retention_rwkv7_parallel_hybrid"


def rmsnorm(x, g):
    xf = x.astype(jnp.float32)
    y = xf * lax.rsqrt(jnp.mean(xf * xf, axis=-1, keepdims=True) + RMS_EPS)
    return (y * g.astype(jnp.float32)).astype(x.dtype)


def head_norm(x, eps):
    mu = jnp.mean(x, axis=-1, keepdims=True)
    var = jnp.mean(jnp.square(x - mu), axis=-1, keepdims=True)
    return (x - mu) * lax.rsqrt(var + eps)


def rope(x, pos):
    half = x.shape[-1] // 2
    expo = -jnp.arange(half, dtype=jnp.float32) / jnp.float32(half)
    freqs = jnp.exp(expo * jnp.float32(np.log(ROPE_BASE)))
    ang = pos.astype(jnp.float32)[:, None] * freqs[None, :]
    cos = jnp.cos(ang)[None, :, None, :]
    sin = jnp.sin(ang)[None, :, None, :]
    x1 = x[..., :half]
    x2 = x[..., half:]
    return jnp.concatenate([x1 * cos - x2 * sin, x1 * sin + x2 * cos], axis=-1)


def chunk_retention(q, k, v):
    q = q.astype(jnp.float32)
    k = k.astype(jnp.float32)
    v = v.astype(jnp.float32)
    B, T, H, dk = q.shape
    dv = v.shape[-1]
    nc = T // CHUNK
    hidx = jnp.arange(H, dtype=jnp.float32)
    lg = jnp.log(1.0 - jnp.exp2(-5.0 - hidx))
    idx = jnp.arange(CHUNK, dtype=jnp.float32)
    intra_decay = jnp.exp(lg[:, None, None] * jnp.abs(idx[:, None] - idx[None, :]))
    q_dec = jnp.transpose(jnp.exp(lg[:, None] * (idx[None, :] + 1.0)))
    k_dec = jnp.transpose(jnp.exp(lg[:, None] * (CHUNK - 1.0 - idx[None, :])))
    chunk_dec = jnp.exp(lg * CHUNK).reshape(1, H, 1, 1)

    qc = q.reshape(B, nc, CHUNK, H, dk)
    kc = k.reshape(B, nc, CHUNK, H, dk)
    vc = v.reshape(B, nc, CHUNK, H, dv)

    scores = jnp.einsum('bnqhd,bnkhd->bnhqk', qc, kc) * intra_decay[None, None]
    intra = jnp.einsum('bnhqk,bnkhe->bnqhe', scores, vc)

    kv = jnp.einsum('bnkhd,bnkhe->nbhde', kc * k_dec[None, None, :, :, None], vc)

    def step(S, kv_c):
        S_new = (S * chunk_dec + kv_c).astype(S.dtype)
        return S_new, S

    S0 = jnp.zeros(kv.shape[1:], dtype=kv.dtype)
    _, s_in = lax.scan(step, S0, kv)
    inter = jnp.einsum('bnqhd,nbhde->bnqhe', qc * q_dec[None, None, :, :, None], s_in)
    return (intra + inter).reshape(B, T, H, dv)


def wkv7_scan(r, w, k, v, kk, a):
    B, T, H, N = r.shape
    xs = tuple(jnp.moveaxis(t.astype(jnp.float32), 1, 0) for t in (r, w, k, v, kk, a))

    def step(S, inp):
        r_t, w_t, k_t, v_t, kk_t, a_t = inp
        sa = jnp.einsum('bhij,bhj->bhi', S, -kk_t)
        S_new = (S * w_t[:, :, None, :]
                 + sa[..., None] * (kk_t * a_t)[:, :, None, :]
                 + v_t[..., None] * k_t[:, :, None, :]).astype(S.dtype)
        return S_new, jnp.einsum('bhij,bhj->bhi', S_new, r_t)

    S0 = jnp.zeros((B, H, N, N), dtype=jnp.float32)
    _, o = lax.scan(step, S0, xs)
    return jnp.moveaxis(o, 0, 1)


def _fwd_setup_inputs(seed: int = 0) -> dict:
    key = jax.random.key(seed)
    ks = jax.random.split(key, 20)
    f32 = jnp.float32
    L = DEPTH
    x = jax.random.normal(ks[0], (BATCH, SEQ, D_MODEL), f32)
    norm_g = 1.0 + 0.01 * jax.random.normal(ks[1], (L, D_MODEL), f32)
    w_in = jax.random.normal(ks[2], (L, D_MODEL, IN_COLS), f32) * D_MODEL ** -0.5
    ret_gn_g = 1.0 + 0.01 * jax.random.normal(ks[3], (L, RET_WIDTH), f32)
    rwkv_mu = jax.random.uniform(ks[4], (L, RWKV_COLS), f32)
    w_lora_up = jax.random.normal(ks[5], (L, LORA_W, RWKV_WIDTH), f32) * 0.1
    w0 = jax.random.uniform(ks[6], (L, RWKV_WIDTH), f32, minval=-3.0, maxval=1.0)
    a_lora_up = jax.random.normal(ks[7], (L, LORA_A, RWKV_WIDTH), f32) * 0.1
    a0 = 0.1 * jax.random.normal(ks[8], (L, RWKV_WIDTH), f32)
    k_k = 0.85 + 0.05 * jax.random.normal(ks[9], (L, RWKV_WIDTH), f32)
    k_a = 1.0 + 0.05 * jax.random.normal(ks[10], (L, RWKV_WIDTH), f32)
    r_k = 0.1 * jax.random.normal(ks[11], (L, RWKV_HEADS, RWKV_HEAD), f32)
    rwkv_gn_g = 1.0 + 0.01 * jax.random.normal(ks[12], (L, RWKV_WIDTH), f32)
    rwkv_gn_b = 0.01 * jax.random.normal(ks[13], (L, RWKV_WIDTH), f32)
    w_out = jax.random.normal(ks[14], (L, MIX_WIDTH, D_MODEL), f32) * MIX_WIDTH ** -0.5
    final_norm_g = 1.0 + 0.01 * jax.random.normal(ks[15], (D_MODEL,), f32)
    return {"x": x, "norm_g": norm_g, "w_in": w_in, "ret_gn_g": ret_gn_g,
            "rwkv_mu": rwkv_mu, "w_lora_up": w_lora_up, "w0": w0,
            "a_lora_up": a_lora_up, "a0": a0, "k_k": k_k, "k_a": k_a, "r_k": r_k,
            "rwkv_gn_g": rwkv_gn_g, "rwkv_gn_b": rwkv_gn_b, "w_out": w_out,
            "final_norm_g": final_norm_g}


def _fwd_reference(x, norm_g, w_in, ret_gn_g, rwkv_mu, w_lora_up, w0, a_lora_up, a0,
              k_k, k_a, r_k, rwkv_gn_g, rwkv_gn_b, w_out, final_norm_g):
    B, T, _ = x.shape
    f32 = jnp.float32
    pos = jnp.arange(T, dtype=jnp.int32)
    W = RWKV_WIDTH
    h = x
    for l in range(DEPTH):
        u = rmsnorm(h, norm_g[l])
        p = jnp.einsum('btd,dc->btc', u, w_in[l]).astype(f32)
        p_ret = p[..., :RET_COLS]
        p_rwkv = p[..., RET_COLS:]

        q = p_ret[..., :RET_QK]
        k = p_ret[..., RET_QK:2 * RET_QK]
        v = p_ret[..., 2 * RET_QK:2 * RET_QK + RET_WIDTH]
        g_ret = p_ret[..., 2 * RET_QK + RET_WIDTH:]
        q = rope(q.reshape(B, T, RET_HEADS, RET_DK), pos)
        k = rope(k.reshape(B, T, RET_HEADS, RET_DK), pos) * (RET_DK ** -0.5)
        v = v.reshape(B, T, RET_HEADS, RET_DV)
        ret = head_norm(chunk_retention(q, k, v), RET_GN_EPS).reshape(B, T, RET_WIDTH)
        y_ret = jax.nn.silu(g_ret) * (ret * ret_gn_g[l].astype(f32))

        prev = jnp.pad(p_rwkv, ((0, 0), (1, 0), (0, 0)))[:, :-1]
        ps = p_rwkv + rwkv_mu[l].astype(f32) * (prev - p_rwkv)
        r = ps[..., :W]
        kr = ps[..., W:2 * W]
        vr = ps[..., 2 * W:3 * W]
        g_rw = ps[..., 3 * W:4 * W]
        xw = ps[..., 4 * W:4 * W + LORA_W]
        xa = ps[..., 4 * W + LORA_W:]
        w_log = -jax.nn.softplus(-(w0[l].astype(f32) + jnp.tanh(xw) @ w_lora_up[l].astype(f32))) - 0.5
        decay = jnp.exp(-jnp.exp(w_log))
        a = jax.nn.sigmoid(a0[l].astype(f32) + xa @ a_lora_up[l].astype(f32))
        kk = (kr * k_k[l].astype(f32)).reshape(B, T, RWKV_HEADS, RWKV_HEAD)
        kk = kk / jnp.maximum(jnp.sqrt(jnp.sum(kk * kk, axis=-1, keepdims=True)), 1e-12)
        kr = kr * (1.0 + (a - 1.0) * k_a[l].astype(f32))
        r4 = r.reshape(B, T, RWKV_HEADS, RWKV_HEAD)
        k4 = kr.reshape(B, T, RWKV_HEADS, RWKV_HEAD)
        v4 = vr.reshape(B, T, RWKV_HEADS, RWKV_HEAD)
        w4 = decay.reshape(B, T, RWKV_HEADS, RWKV_HEAD)
        a4 = a.reshape(B, T, RWKV_HEADS, RWKV_HEAD)
        o = wkv7_scan(r4, w4, k4, v4, kk, a4)
        gn_g = rwkv_gn_g[l].astype(f32).reshape(RWKV_HEADS, RWKV_HEAD)
        gn_b = rwkv_gn_b[l].astype(f32).reshape(RWKV_HEADS, RWKV_HEAD)
        o = head_norm(o, RWKV_GN_EPS) * gn_g + gn_b
        bonus = jnp.sum(r4 * k4 * r_k[l].astype(f32), axis=-1, keepdims=True) * v4
        y_rwkv = jax.nn.silu(g_rw) * (o + bonus).reshape(B, T, RWKV_WIDTH)

        y = jnp.concatenate([y_ret, y_rwkv], axis=-1).astype(x.dtype)
        h = h + jnp.einsum('btc,cd->btd', y, w_out[l]).astype(h.dtype)
    return rmsnorm(h, final_norm_g)


import jax as _jax
import jax.numpy as _jnp

TWIN_FORMAT = 'train_step'
FWD_PARAMS = ['x', 'norm_g', 'w_in', 'ret_gn_g', 'rwkv_mu', 'w_lora_up', 'w0', 'a_lora_up', 'a0', 'k_k', 'k_a', 'r_k', 'rwkv_gn_g', 'rwkv_gn_b', 'w_out', 'final_norm_g']
TWIN_WEIGHTS = ['norm_g', 'w_in', 'ret_gn_g', 'rwkv_mu', 'w_lora_up', 'w0', 'a_lora_up', 'a0', 'k_k', 'k_a', 'r_k', 'rwkv_gn_g', 'rwkv_gn_b', 'w_out', 'final_norm_g']
TWIN_DIFF_INPUT = 'x'
TWIN_INPUTS = ['x', 'norm_g', 'w_in', 'ret_gn_g', 'rwkv_mu', 'w_lora_up', 'w0', 'a_lora_up', 'a0', 'k_k', 'k_a', 'r_k', 'rwkv_gn_g', 'rwkv_gn_b', 'w_out', 'final_norm_g', 'loss_target', 'm_norm_g', 'm_w_in', 'm_ret_gn_g', 'm_rwkv_mu', 'm_w_lora_up', 'm_w0', 'm_a_lora_up', 'm_a0', 'm_k_k', 'm_k_a', 'm_r_k', 'm_rwkv_gn_g', 'm_rwkv_gn_b', 'm_w_out', 'm_final_norm_g', 'v_norm_g', 'v_w_in', 'v_ret_gn_g', 'v_rwkv_mu', 'v_w_lora_up', 'v_w0', 'v_a_lora_up', 'v_a0', 'v_k_k', 'v_k_a', 'v_r_k', 'v_rwkv_gn_g', 'v_rwkv_gn_b', 'v_w_out', 'v_final_norm_g']
TWIN_OUTPUTS = ['loss', 'grad_x', 'grad_norm_g', 'grad_w_in', 'grad_ret_gn_g', 'grad_rwkv_mu', 'grad_w_lora_up', 'grad_w0', 'grad_a_lora_up', 'grad_a0', 'grad_k_k', 'grad_k_a', 'grad_r_k', 'grad_rwkv_gn_g', 'grad_rwkv_gn_b', 'grad_w_out', 'grad_final_norm_g', 'delta_norm_g', 'delta_w_in', 'delta_ret_gn_g', 'delta_rwkv_mu', 'delta_w_lora_up', 'delta_w0', 'delta_a_lora_up', 'delta_a0', 'delta_k_k', 'delta_k_a', 'delta_r_k', 'delta_rwkv_gn_g', 'delta_rwkv_gn_b', 'delta_w_out', 'delta_final_norm_g', 'new_m_norm_g', 'new_m_w_in', 'new_m_ret_gn_g', 'new_m_rwkv_mu', 'new_m_w_lora_up', 'new_m_w0', 'new_m_a_lora_up', 'new_m_a0', 'new_m_k_k', 'new_m_k_a', 'new_m_r_k', 'new_m_rwkv_gn_g', 'new_m_rwkv_gn_b', 'new_m_w_out', 'new_m_final_norm_g', 'new_v_norm_g', 'new_v_w_in', 'new_v_ret_gn_g', 'new_v_rwkv_mu', 'new_v_w_lora_up', 'new_v_w0', 'new_v_a_lora_up', 'new_v_a0', 'new_v_k_k', 'new_v_k_a', 'new_v_r_k', 'new_v_rwkv_gn_g', 'new_v_rwkv_gn_b', 'new_v_w_out', 'new_v_final_norm_g']
TWIN_LEAF_KINDS = {'loss': 'loss', 'grad_x': 'grad_x', 'grad_norm_g': 'grad_w', 'grad_w_in': 'grad_w', 'grad_ret_gn_g': 'grad_w', 'grad_rwkv_mu': 'grad_w', 'grad_w_lora_up': 'grad_w', 'grad_w0': 'grad_w', 'grad_a_lora_up': 'grad_w', 'grad_a0': 'grad_w', 'grad_k_k': 'grad_w', 'grad_k_a': 'grad_w', 'grad_r_k': 'grad_w', 'grad_rwkv_gn_g': 'grad_w', 'grad_rwkv_gn_b': 'grad_w', 'grad_w_out': 'grad_w', 'grad_final_norm_g': 'grad_w', 'delta_norm_g': 'delta_w', 'delta_w_in': 'delta_w', 'delta_ret_gn_g': 'delta_w', 'delta_rwkv_mu': 'delta_w', 'delta_w_lora_up': 'delta_w', 'delta_w0': 'delta_w', 'delta_a_lora_up': 'delta_w', 'delta_a0': 'delta_w', 'delta_k_k': 'delta_w', 'delta_k_a': 'delta_w', 'delta_r_k': 'delta_w', 'delta_rwkv_gn_g': 'delta_w', 'delta_rwkv_gn_b': 'delta_w', 'delta_w_out': 'delta_w', 'delta_final_norm_g': 'delta_w', 'new_m_norm_g': 'new_m', 'new_m_w_in': 'new_m', 'new_m_ret_gn_g': 'new_m', 'new_m_rwkv_mu': 'new_m', 'new_m_w_lora_up': 'new_m', 'new_m_w0': 'new_m', 'new_m_a_lora_up': 'new_m', 'new_m_a0': 'new_m', 'new_m_k_k': 'new_m', 'new_m_k_a': 'new_m', 'new_m_r_k': 'new_m', 'new_m_rwkv_gn_g': 'new_m', 'new_m_rwkv_gn_b': 'new_m', 'new_m_w_out': 'new_m', 'new_m_final_norm_g': 'new_m', 'new_v_norm_g': 'new_v', 'new_v_w_in': 'new_v', 'new_v_ret_gn_g': 'new_v', 'new_v_rwkv_mu': 'new_v', 'new_v_w_lora_up': 'new_v', 'new_v_w0': 'new_v', 'new_v_a_lora_up': 'new_v', 'new_v_a0': 'new_v', 'new_v_k_k': 'new_v', 'new_v_k_a': 'new_v', 'new_v_r_k': 'new_v', 'new_v_rwkv_gn_g': 'new_v', 'new_v_rwkv_gn_b': 'new_v', 'new_v_w_out': 'new_v', 'new_v_final_norm_g': 'new_v'}


def _forward(args):
    return _fwd_reference(*[args[k] for k in FWD_PARAMS])


def _output_shape():
    out = _jax.eval_shape(lambda: _forward(_fwd_setup_inputs(0)))
    return out.shape, out.dtype

N_MICROBATCH = 1
ADAM_LR = 0.001
ADAM_B1 = 0.9
ADAM_B2 = 0.999
ADAM_EPS = 1e-08
ADAM_WD = 0.01
ADAM_STEP = 10
PER_EXAMPLE_BATCH_AXIS = {'x': 0, 'loss_target': 0}
SHARED_INPUTS = []
_WEIGHT_DTYPES = {'norm_g': _jnp.float32, 'w_in': _jnp.float32, 'ret_gn_g': _jnp.float32, 'rwkv_mu': _jnp.float32, 'w_lora_up': _jnp.float32, 'w0': _jnp.float32, 'a_lora_up': _jnp.float32, 'a0': _jnp.float32, 'k_k': _jnp.float32, 'k_a': _jnp.float32, 'r_k': _jnp.float32, 'rwkv_gn_g': _jnp.float32, 'rwkv_gn_b': _jnp.float32, 'w_out': _jnp.float32, 'final_norm_g': _jnp.float32}
MOMENT_SCALE = {'norm_g': 1.274277e-01, 'w_in': 6.505102e-02, 'ret_gn_g': 6.445974e-02, 'rwkv_mu': 9.402951e-02, 'w_lora_up': 5.417197e-03, 'w0': 2.777159e-02, 'a_lora_up': 1.928463e-02, 'a0': 2.470824e-02, 'k_k': 1.718688e-02, 'k_a': 6.329464e-02, 'r_k': 1.167866e-01, 'rwkv_gn_g': 5.382600e-02, 'rwkv_gn_b': 5.085582e-02, 'w_out': 5.978422e-02, 'final_norm_g': 1.598776e+01}


def _to_microbatches(a, axis):
    t = _jnp.moveaxis(a, axis, 0)
    t = t.reshape((N_MICROBATCH, t.shape[0] // N_MICROBATCH) + t.shape[1:])
    return _jnp.moveaxis(t, 1, axis + 1)


def setup_inputs(seed: int = 0) -> dict:
    inp = _fwd_setup_inputs(seed)
    key = _jax.random.fold_in(_jax.random.key(seed), 7919)
    shape, _ = _output_shape()
    out = dict(inp)
    out["loss_target"] = _jax.random.normal(_jax.random.fold_in(key, 0), shape, _jnp.float32)
    for i, name in enumerate(TWIN_WEIGHTS):
        w = inp[name].astype(_jnp.float32)
        if MOMENT_SCALE is None:
            s = _jnp.sqrt(_jnp.mean(_jnp.square(w)) + 1e-30)
        else:
            s = MOMENT_SCALE[name]
        km, kv = _jax.random.split(_jax.random.fold_in(key, i + 1))
        out[name] = w
        out["m_" + name] = s * _jax.random.normal(km, w.shape, _jnp.float32)
        out["v_" + name] = (s * s) * _jax.random.uniform(kv, w.shape, _jnp.float32, 0.5, 1.5)
    if N_MICROBATCH > 1:
        for name, axis in PER_EXAMPLE_BATCH_AXIS.items():
            out[name] = _to_microbatches(out[name], axis)
    return {'x': out['x'], 'norm_g': out['norm_g'], 'w_in': out['w_in'], 'ret_gn_g': out['ret_gn_g'], 'rwkv_mu': out['rwkv_mu'], 'w_lora_up': out['w_lora_up'], 'w0': out['w0'], 'a_lora_up': out['a_lora_up'], 'a0': out['a0'], 'k_k': out['k_k'], 'k_a': out['k_a'], 'r_k': out['r_k'], 'rwkv_gn_g': out['rwkv_gn_g'], 'rwkv_gn_b': out['rwkv_gn_b'], 'w_out': out['w_out'], 'final_norm_g': out['final_norm_g'], 'loss_target': out['loss_target'], 'm_norm_g': out['m_norm_g'], 'm_w_in': out['m_w_in'], 'm_ret_gn_g': out['m_ret_gn_g'], 'm_rwkv_mu': out['m_rwkv_mu'], 'm_w_lora_up': out['m_w_lora_up'], 'm_w0': out['m_w0'], 'm_a_lora_up': out['m_a_lora_up'], 'm_a0': out['m_a0'], 'm_k_k': out['m_k_k'], 'm_k_a': out['m_k_a'], 'm_r_k': out['m_r_k'], 'm_rwkv_gn_g': out['m_rwkv_gn_g'], 'm_rwkv_gn_b': out['m_rwkv_gn_b'], 'm_w_out': out['m_w_out'], 'm_final_norm_g': out['m_final_norm_g'], 'v_norm_g': out['v_norm_g'], 'v_w_in': out['v_w_in'], 'v_ret_gn_g': out['v_ret_gn_g'], 'v_rwkv_mu': out['v_rwkv_mu'], 'v_w_lora_up': out['v_w_lora_up'], 'v_w0': out['v_w0'], 'v_a_lora_up': out['v_a_lora_up'], 'v_a0': out['v_a0'], 'v_k_k': out['v_k_k'], 'v_k_a': out['v_k_a'], 'v_r_k': out['v_r_k'], 'v_rwkv_gn_g': out['v_rwkv_gn_g'], 'v_rwkv_gn_b': out['v_rwkv_gn_b'], 'v_w_out': out['v_w_out'], 'v_final_norm_g': out['v_final_norm_g']}


def _loss(weights, diff, rest, loss_target):
    with _jax.named_scope("forward"):
        args = {**rest, TWIN_DIFF_INPUT: diff, **{k: w.astype(_WEIGHT_DTYPES[k]) for k, w in weights.items()}}
        y = _forward(args)
    with _jax.named_scope("loss_head"):
        err = _jnp.square(y.astype(_jnp.float32) - loss_target)
        return 0.5 * _jnp.sum(_jnp.mean(err, axis=-1)) if err.ndim else 0.5 * err


def _adamw(w, g, m, v):
    m = ADAM_B1 * m + (1.0 - ADAM_B1) * g
    v = ADAM_B2 * v + (1.0 - ADAM_B2) * _jnp.square(g)
    m_hat = m / (1.0 - ADAM_B1 ** ADAM_STEP)
    v_hat = v / (1.0 - ADAM_B2 ** ADAM_STEP)
    delta = -ADAM_LR * (m_hat / (_jnp.sqrt(v_hat) + ADAM_EPS) + ADAM_WD * w)
    return delta, m, v


def reference(x, norm_g, w_in, ret_gn_g, rwkv_mu, w_lora_up, w0, a_lora_up, a0, k_k, k_a, r_k, rwkv_gn_g, rwkv_gn_b, w_out, final_norm_g, loss_target, m_norm_g, m_w_in, m_ret_gn_g, m_rwkv_mu, m_w_lora_up, m_w0, m_a_lora_up, m_a0, m_k_k, m_k_a, m_r_k, m_rwkv_gn_g, m_rwkv_gn_b, m_w_out, m_final_norm_g, v_norm_g, v_w_in, v_ret_gn_g, v_rwkv_mu, v_w_lora_up, v_w0, v_a_lora_up, v_a0, v_k_k, v_k_a, v_r_k, v_rwkv_gn_g, v_rwkv_gn_b, v_w_out, v_final_norm_g):
    given = dict(x=x, norm_g=norm_g, w_in=w_in, ret_gn_g=ret_gn_g, rwkv_mu=rwkv_mu, w_lora_up=w_lora_up, w0=w0, a_lora_up=a_lora_up, a0=a0, k_k=k_k, k_a=k_a, r_k=r_k, rwkv_gn_g=rwkv_gn_g, rwkv_gn_b=rwkv_gn_b, w_out=w_out, final_norm_g=final_norm_g, loss_target=loss_target, m_norm_g=m_norm_g, m_w_in=m_w_in, m_ret_gn_g=m_ret_gn_g, m_rwkv_mu=m_rwkv_mu, m_w_lora_up=m_w_lora_up, m_w0=m_w0, m_a_lora_up=m_a_lora_up, m_a0=m_a0, m_k_k=m_k_k, m_k_a=m_k_a, m_r_k=m_r_k, m_rwkv_gn_g=m_rwkv_gn_g, m_rwkv_gn_b=m_rwkv_gn_b, m_w_out=m_w_out, m_final_norm_g=m_final_norm_g, v_norm_g=v_norm_g, v_w_in=v_w_in, v_ret_gn_g=v_ret_gn_g, v_rwkv_mu=v_rwkv_mu, v_w_lora_up=v_w_lora_up, v_w0=v_w0, v_a_lora_up=v_a_lora_up, v_a0=v_a0, v_k_k=v_k_k, v_k_a=v_k_a, v_r_k=v_r_k, v_rwkv_gn_g=v_rwkv_gn_g, v_rwkv_gn_b=v_rwkv_gn_b, v_w_out=v_w_out, v_final_norm_g=v_final_norm_g)
    weights = {n: given[n] for n in TWIN_WEIGHTS}
    shared = {n: given[n] for n in SHARED_INPUTS}
    per_example = {n: given[n] for n in ['x']}
    grad_fn = _jax.value_and_grad(_loss, argnums=(0, 1))

    def one_microbatch(ex, loss_target):
        ex = dict(ex)
        diff = ex.pop(TWIN_DIFF_INPUT)
        return grad_fn(weights, diff, {**shared, **ex}, loss_target)

    if N_MICROBATCH == 1:
        loss, (grad_w, grad_x) = one_microbatch(per_example, given["loss_target"])
    else:
        def body(carry, xs):
            loss_sum, grad_sum = carry
            l_k, (gw_k, gx_k) = one_microbatch(xs[0], xs[1])
            with _jax.named_scope("update"):
                return (loss_sum + l_k, _jax.tree.map(_jnp.add, grad_sum, gw_k)), gx_k

        init = (_jnp.zeros((), _jnp.float32), _jax.tree.map(_jnp.zeros_like, weights))
        (loss, grad_w), grad_x = _jax.lax.scan(body, init, (per_example, given["loss_target"]))
    with _jax.named_scope("update"):
        delta_w, new_m, new_v = {}, {}, {}
        for n in TWIN_WEIGHTS:
            delta_w[n], new_m[n], new_v[n] = _adamw(weights[n], grad_w[n], given["m_" + n], given["v_" + n])
    return (loss, grad_x, *[grad_w[n] for n in TWIN_WEIGHTS], *[delta_w[n] for n in TWIN_WEIGHTS],
            *[new_m[n] for n in TWIN_WEIGHTS], *[new_v[n] for n in TWIN_WEIGHTS])
```

```python
import functools

import numpy as np
import jax
import jax.numpy as jnp
from jax import lax
from jax.experimental import pallas as pl
from jax.experimental.pallas import tpu as pltpu

F32 = jnp.float32
BF16 = jnp.bfloat16
HIGHEST = lax.Precision.HIGHEST
MESH = pl.DeviceIdType.MESH

D_MODEL = 1024
N_CHIPS = 4
RET_HEADS = 4
RET_DK = 64
RET_DV = 128
RET_QK = RET_HEADS * RET_DK
RET_WIDTH = RET_HEADS * RET_DV
RET_COLS = 2 * RET_QK + 2 * RET_WIDTH
RET_CHUNK = 64
RW_WIDTH = 512
RW_HEAD = 64
RW_HEADS = 8
LORA = 64
RW_COLS = 4 * RW_WIDTH + 2 * LORA
IN_COLS = RET_COLS + RW_COLS
IN_SHARD = IN_COLS // N_CHIPS
IN_COLS_PAD = 3840
DW_TILE = 768
OUT_SHARD = D_MODEL // N_CHIPS
ROPE_BASE = 10000.0
RMS_EPS = 1e-6
RET_GN_EPS = 1e-5
RW_GN_EPS = 64e-5
WKV_CHUNK = 16
N_VEC = 5

ADAM_LR = 0.001
ADAM_B1 = 0.9
ADAM_B2 = 0.999
ADAM_EPS = 1e-08
ADAM_WD = 0.01
ADAM_STEP = 10

VMEM_LIMIT = 56 * 1024 * 1024


def _cparams(**kw):
    return pltpu.CompilerParams(vmem_limit_bytes=VMEM_LIMIT, **kw)


def _dot(a, b, precision=None):
    return jnp.dot(a, b, precision=precision, preferred_element_type=F32)


def _dot_nt(a, b, precision=None):
    return lax.dot_general(a, b, (((1,), (1,)), ((), ())), precision=precision, preferred_element_type=F32)


def _dot_tn(a, b, precision=None):
    return lax.dot_general(a, b, (((0,), (0,)), ((), ())), precision=precision, preferred_element_type=F32)


def _split(x):
    hi = x.astype(BF16)
    lo = (x - hi.astype(F32)).astype(BF16)
    return hi, lo


@jax.custom_vjp
def _segsum(x, seg):
    hi, lo = _split(x)
    return _dot(hi, seg) + _dot(lo, seg)


def _segsum_fwd(x, seg):
    return _segsum(x, seg), seg


def _segsum_bwd(seg, ct):
    return _segsum(ct, seg), jnp.zeros_like(seg)


_segsum.defvjp(_segsum_fwd, _segsum_bwd)


def _softplus(z):
    return jnp.maximum(z, 0.0) + jnp.log(1.0 + jnp.exp(-jnp.abs(z)))


def _full(shape):
    nd = len(shape)
    return pl.BlockSpec(shape, lambda *_: (0,) * nd)


def _rope_tables(T):
    half = RET_DK // 2
    expo = -jnp.arange(half, dtype=F32) / jnp.float32(half)
    freqs = jnp.exp(expo * jnp.float32(np.log(ROPE_BASE)))
    ang = jnp.arange(T, dtype=jnp.int32).astype(F32)[:, None] * freqs[None, :]
    cos = jnp.tile(jnp.cos(ang), (1, 2 * RET_HEADS))
    sin = jnp.tile(jnp.sin(ang), (1, 2 * RET_HEADS))
    return cos, sin


def _ret_tables():
    H, C = RET_HEADS, RET_CHUNK
    hidx = jnp.arange(H, dtype=F32)
    lg = jnp.log(1.0 - jnp.exp2(-5.0 - hidx))
    idx = jnp.arange(C, dtype=F32)
    intra = jnp.exp(lg[:, None, None] * jnp.abs(idx[:, None] - idx[None, :]))
    q_dec = jnp.transpose(jnp.exp(lg[:, None] * (idx[None, :] + 1.0)))
    k_dec = jnp.transpose(jnp.exp(lg[:, None] * (C - 1.0 - idx[None, :])))
    chunk_dec = jnp.exp(lg * C)
    qd = jnp.repeat(q_dec, RET_DK, axis=1)
    kd = jnp.repeat(k_dec, RET_DK, axis=1)
    row_h = np.arange(RET_QK) // RET_DK
    col_h = np.arange(RET_WIDTH) // RET_DV
    bm = jnp.asarray((row_h[:, None] == col_h[None, :]).astype(np.float32))
    cd = bm * jnp.repeat(chunk_dec, RET_DK)[:, None]
    return intra, qd, kd, cd, bm


def _seg_matrix(width, head):
    h = np.arange(width) // head
    return jnp.asarray((h[:, None] == h[None, :]).astype(np.float32), dtype=BF16)


def _wkv_expand_table():
    Tc = WKV_CHUNK
    k = np.arange(2 * RW_HEADS * Tc)
    kh, kt = (k % (RW_HEADS * Tc)) // Tc, k % Tc
    nh = np.arange(RW_WIDTH) // RW_HEAD
    e = (kh[None, :, None] == nh[None, None, :]) & (kt[None, :, None] == np.arange(Tc)[:, None, None])
    return jnp.asarray(e.astype(np.float32), dtype=BF16)


def _wkv_reduce_table():
    Tc = WKV_CHUNK
    kh = (np.arange(2 * RW_WIDTH) % RW_WIDTH) // RW_HEAD
    n = np.arange(RW_HEADS * Tc)
    nh, nt = n // Tc, n % Tc
    r = (kh[None, :, None] == nh[None, None, :]) & (nt[None, None, :] == np.arange(Tc)[:, None, None])
    return jnp.asarray(r.astype(np.float32), dtype=BF16)


def _inproj_fwd(x, norm_g, w_b):
    T = x.shape[0]
    tm = min(T, 256)

    def body(x_ref, g_ref, w_ref, pret_ref, prw_ref, u_ref):
        xf = x_ref[...]
        rstd = lax.rsqrt(jnp.mean(xf * xf, axis=-1, keepdims=True) + RMS_EPS)
        ub = ((xf * rstd) * g_ref[...]).astype(BF16)
        u_ref[...] = ub
        pret_ref[...] = _dot(ub, w_ref[:, :RET_COLS])
        prw_ref[...] = _dot(ub, w_ref[:, RET_COLS:])

    return pl.pallas_call(
        body, name="inproj_fwd", grid=(T // tm,),
        in_specs=[pl.BlockSpec((tm, D_MODEL), lambda i: (i, 0)), _full((1, D_MODEL)), _full((D_MODEL, IN_COLS))],
        out_specs=[pl.BlockSpec((tm, RET_COLS), lambda i: (i, 0)), pl.BlockSpec((tm, RW_COLS), lambda i: (i, 0)),
                   pl.BlockSpec((tm, D_MODEL), lambda i: (i, 0))],
        out_shape=[jax.ShapeDtypeStruct((T, RET_COLS), F32), jax.ShapeDtypeStruct((T, RW_COLS), F32),
                   jax.ShapeDtypeStruct((T, D_MODEL), BF16)],
        compiler_params=_cparams(dimension_semantics=("arbitrary",)),
    )(x, norm_g, w_b)


def _rot_half(x):
    n = x.shape[1]
    lane = lax.broadcasted_iota(jnp.int32, x.shape, 1)
    first = (lane % RET_DK) < (RET_DK // 2)
    return jnp.where(first, -pltpu.roll(x, n - RET_DK // 2, 1), pltpu.roll(x, RET_DK // 2, 1))


def _rope(x, cos, sin):
    return x * cos + _rot_half(x) * sin


def _rope_bwd(d, cos, sin):
    return d * cos - _rot_half(d * sin)


def _ret_post(ret, g, gn_g, seg):
    mu = _segsum(ret, seg) * (1.0 / RET_DV)
    xc = ret - mu
    var = _segsum(xc * xc, seg) * (1.0 / RET_DV)
    n = xc * lax.rsqrt(var + RET_GN_EPS)
    return (g * jax.nn.sigmoid(g)) * (n * gn_g)


def _ret_scores(qt, kt, d_ref, h):
    lane = lax.broadcasted_iota(jnp.int32, qt.shape, 1)
    qh = jnp.where(lane // RET_DK == h, qt, 0.0)
    return qh, _dot_nt(qh, kt, HIGHEST) * d_ref[h]


def _ret_fwd(p_ret, cos, sin, tabs, gn_g, seg128):
    T = p_ret.shape[0]
    C = RET_CHUNK
    nch = T // C
    intra_d, qd, kd, cd, bm = tabs

    def body(q_ref, k_ref, v_ref, g_ref, cos_ref, sin_ref, qd_ref, kd_ref, d_ref, cd_ref, bm_ref, gn_ref, seg_ref,
             y_ref, ret_ref, sin_out_ref, s_ref):
        @pl.when(pl.program_id(0) == 0)
        def _():
            s_ref[...] = jnp.zeros_like(s_ref)

        cosv, sinv = cos_ref[...], sin_ref[...]
        qt = _rope(q_ref[...], cosv, sinv)
        kt = _rope(k_ref[...], cosv, sinv) * (RET_DK ** -0.5)
        v = v_ref[...]
        s_in = s_ref[...]
        sin_out_ref[0] = s_in
        inter = _dot(qt * qd_ref[...], s_in, HIGHEST)
        intra = []
        for h in range(RET_HEADS):
            _, a = _ret_scores(qt, kt, d_ref, h)
            intra.append(_dot(a, v[:, h * RET_DV:(h + 1) * RET_DV], HIGHEST))
        ret = jnp.concatenate(intra, axis=1) + inter
        kv = _dot_tn(kt * kd_ref[...], v, HIGHEST)
        s_ref[...] = s_in * cd_ref[...] + kv * bm_ref[...]
        ret_ref[...] = ret
        y_ref[...] = _ret_post(ret, g_ref[...], gn_ref[...], seg_ref[...]).astype(BF16)

    return pl.pallas_call(
        body, name="ret_fwd", grid=(nch,),
        in_specs=[pl.BlockSpec((C, RET_QK), lambda c: (c, 0)), pl.BlockSpec((C, RET_QK), lambda c: (c, 1)),
                  pl.BlockSpec((C, RET_WIDTH), lambda c: (c, 1)), pl.BlockSpec((C, RET_WIDTH), lambda c: (c, 2)),
                  pl.BlockSpec((C, RET_QK), lambda c: (c, 0)), pl.BlockSpec((C, RET_QK), lambda c: (c, 0)),
                  _full((C, RET_QK)), _full((C, RET_QK)), _full((RET_HEADS, C, C)),
                  _full((RET_QK, RET_WIDTH)), _full((RET_QK, RET_WIDTH)), _full((1, RET_WIDTH)),
                  _full((RET_WIDTH, RET_WIDTH))],
        out_specs=[pl.BlockSpec((C, RET_WIDTH), lambda c: (c, 0)), pl.BlockSpec((C, RET_WIDTH), lambda c: (c, 0)),
                   pl.BlockSpec((1, RET_QK, RET_WIDTH), lambda c: (c, 0, 0))],
        out_shape=[jax.ShapeDtypeStruct((T, RET_WIDTH), BF16), jax.ShapeDtypeStruct((T, RET_WIDTH), F32),
                   jax.ShapeDtypeStruct((nch, RET_QK, RET_WIDTH), F32)],
        scratch_shapes=[pltpu.VMEM((RET_QK, RET_WIDTH), F32)],
        compiler_params=_cparams(dimension_semantics=("arbitrary",)),
    )(p_ret, p_ret, p_ret, p_ret, cos, sin, qd, kd, intra_d, cd, bm, gn_g, seg128)


def _ret_bwd(p_ret, cos, sin, tabs, gn_g, seg128, ret, s_in_all, dy):
    T = p_ret.shape[0]
    C = RET_CHUNK
    nch = T // C
    intra_d, qd, kd, cd, bm = tabs

    def rev(j):
        return lambda c: (nch - 1 - c, j)

    def body(q_ref, k_ref, v_ref, g_ref, cos_ref, sin_ref, qd_ref, kd_ref, d_ref, cd_ref, bm_ref, gn_ref, seg_ref,
             ret_ref, sin_ref_, dy_ref, dp_ref, dgn_ref, ds_ref):
        @pl.when(pl.program_id(0) == 0)
        def _():
            ds_ref[...] = jnp.zeros_like(ds_ref)
            dgn_ref[...] = jnp.zeros_like(dgn_ref)

        cosv, sinv = cos_ref[...], sin_ref[...]
        qt = _rope(q_ref[...], cosv, sinv)
        kt = _rope(k_ref[...], cosv, sinv) * (RET_DK ** -0.5)
        v = v_ref[...]
        s_in = sin_ref_[0]
        ds_out = ds_ref[...]
        seg = seg_ref[...]

        _, post_vjp = jax.vjp(lambda r_, g_, gn_: _ret_post(r_, g_, gn_, seg), ret_ref[...], g_ref[...], gn_ref[...])
        dret, dg, dgn = post_vjp(dy_ref[...])
        dgn_ref[...] += dgn

        qdv, kdv = qd_ref[...], kd_ref[...]
        dqt = qdv * _dot_nt(dret, s_in, HIGHEST)
        dkt = kdv * _dot_nt(v, ds_out, HIGHEST)
        dv_all = _dot(kt * kdv, ds_out, HIGHEST)
        dvs = []
        for h in range(RET_HEADS):
            sl = slice(h * RET_DV, (h + 1) * RET_DV)
            qh, a = _ret_scores(qt, kt, d_ref, h)
            lane = lax.broadcasted_iota(jnp.int32, kt.shape, 1)
            kh = jnp.where(lane // RET_DK == h, kt, 0.0)
            da = _dot_nt(dret[:, sl], v[:, sl], HIGHEST) * d_ref[h]
            dvs.append(_dot_tn(a, dret[:, sl], HIGHEST))
            dqt = dqt + _dot(da, kh, HIGHEST)
            dkt = dkt + _dot_tn(da, qh, HIGHEST)
        dv = dv_all + jnp.concatenate(dvs, axis=1)
        ds_ref[...] = ds_out * cd_ref[...] + _dot_tn(qt * qdv, dret, HIGHEST) * bm_ref[...]
        dq = _rope_bwd(dqt, cosv, sinv)
        dk = _rope_bwd(dkt * (RET_DK ** -0.5), cosv, sinv)
        dp_ref[...] = jnp.concatenate([dq, dk, dv, dg], axis=1)

    return pl.pallas_call(
        body, name="ret_bwd", grid=(nch,),
        in_specs=[pl.BlockSpec((C, RET_QK), rev(0)), pl.BlockSpec((C, RET_QK), rev(1)),
                  pl.BlockSpec((C, RET_WIDTH), rev(1)), pl.BlockSpec((C, RET_WIDTH), rev(2)),
                  pl.BlockSpec((C, RET_QK), rev(0)), pl.BlockSpec((C, RET_QK), rev(0)),
                  _full((C, RET_QK)), _full((C, RET_QK)), _full((RET_HEADS, C, C)),
                  _full((RET_QK, RET_WIDTH)), _full((RET_QK, RET_WIDTH)), _full((1, RET_WIDTH)),
                  _full((RET_WIDTH, RET_WIDTH)),
                  pl.BlockSpec((C, RET_WIDTH), rev(0)),
                  pl.BlockSpec((1, RET_QK, RET_WIDTH), lambda c: (nch - 1 - c, 0, 0)),
                  pl.BlockSpec((C, RET_WIDTH), rev(0))],
        out_specs=[pl.BlockSpec((C, RET_COLS), rev(0)), _full((1, RET_WIDTH))],
        out_shape=[jax.ShapeDtypeStruct((T, RET_COLS), F32), jax.ShapeDtypeStruct((1, RET_WIDTH), F32)],
        scratch_shapes=[pltpu.VMEM((RET_QK, RET_WIDTH), F32)],
        compiler_params=_cparams(dimension_semantics=("arbitrary",)),
    )(p_ret, p_ret, p_ret, p_ret, cos, sin, qd, kd, intra_d, cd, bm, gn_g, seg128, ret, s_in_all, dy)


def _prep_fn(p, prev, mu, w0, a0, k_k, k_a, lora, seg):
    W = RW_WIDTH
    ps = p + mu * (prev - p)
    r, kr, vr, g = ps[:, 0:W], ps[:, W:2 * W], ps[:, 2 * W:3 * W], ps[:, 3 * W:4 * W]
    z = ps[:, 4 * W:]
    lane = lax.broadcasted_iota(jnp.int32, z.shape, 1)
    z = jnp.where(lane < LORA, jnp.tanh(z), z)
    lo = _dot(z, lora, HIGHEST)
    w_log = -_softplus(-(w0 + lo[:, :W])) - 0.5
    decay = jnp.exp(-jnp.exp(w_log))
    a = jax.nn.sigmoid(a0 + lo[:, W:])
    kk = kr * k_k
    kk = kk / jnp.maximum(jnp.sqrt(_segsum(kk * kk, seg)), 1e-12)
    k = kr * (1.0 + (a - 1.0) * k_a)
    return kk, decay, kk * a, k, r, vr, g


def _post_fn(o, r, k, v, g, gn_g, gn_b, r_k, seg):
    mu = _segsum(o, seg) * (1.0 / RW_HEAD)
    oc = o - mu
    var = _segsum(oc * oc, seg) * (1.0 / RW_HEAD)
    on = oc * lax.rsqrt(var + RW_GN_EPS) * gn_g + gn_b
    bonus = _segsum(r * k * r_k, seg) * v
    return (g * jax.nn.sigmoid(g)) * (on + bonus)


def _shift_down(p, first_row):
    row = lax.broadcasted_iota(jnp.int32, p.shape, 0)
    return jnp.where(row == 0, first_row, pltpu.roll(p, 1, 0))


def _shift_up(p, last_row):
    n = p.shape[0]
    row = lax.broadcasted_iota(jnp.int32, p.shape, 0)
    return jnp.where(row == n - 1, last_row, pltpu.roll(p, n - 1, 0))


def _row_tile(T):
    return min(T, 256)


def _prep_fwd(p_rw, bnd, mu, w0, a0, k_k, k_a, lora, seg64):
    T = p_rw.shape[0]
    tm = _row_tile(T)
    W = RW_WIDTH

    def body(p_ref, bnd_ref, mu_ref, w0_ref, a0_ref, kk_ref, ka_ref, lora_ref, seg_ref, *outs):
        p = p_ref[...]
        prev = _shift_down(p, bnd_ref[0])
        res = _prep_fn(p, prev, mu_ref[...], w0_ref[...], a0_ref[...], kk_ref[...], ka_ref[...], lora_ref[...],
                       seg_ref[...])
        for o_ref, val in zip(outs, res):
            o_ref[...] = val

    small = _full((1, W))
    return pl.pallas_call(
        body, name="rwkv_prep_fwd", grid=(T // tm,),
        in_specs=[pl.BlockSpec((tm, RW_COLS), lambda i: (i, 0)), pl.BlockSpec((1, 1, RW_COLS), lambda i: (i, 0, 0)),
                  _full((1, RW_COLS)), small, small, small, small, _full((2 * LORA, 2 * W)), _full((W, W))],
        out_specs=[pl.BlockSpec((tm, W), lambda i: (i, 0))] * 7,
        out_shape=[jax.ShapeDtypeStruct((T, W), F32)] * 7,
        compiler_params=_cparams(dimension_semantics=("arbitrary",)),
    )(p_rw, bnd, mu, w0, a0, k_k, k_a, lora, seg64)


def _prep_bwd(p_rw, bnd, mu, w0, a0, k_k, k_a, lora, seg64, cts):
    T = p_rw.shape[0]
    tm = _row_tile(T)
    W = RW_WIDTH

    def body(p_ref, bnd_ref, mu_ref, w0_ref, a0_ref, kk_ref, ka_ref, lora_ref, seg_ref,
             dkk_ref, dw_ref, db_ref, dk1_ref, dk2_ref, dr1_ref, dr2_ref, dv1_ref, dv2_ref, dg_ref,
             dp_ref, dprev_ref, dmu_ref, dw0_ref, da0_ref, dkk_p_ref, dka_ref, dlora_ref):
        accs = (dmu_ref, dw0_ref, da0_ref, dkk_p_ref, dka_ref, dlora_ref)

        @pl.when(pl.program_id(0) == 0)
        def _():
            for a_ref in accs:
                a_ref[...] = jnp.zeros_like(a_ref)

        p = p_ref[...]
        prev = _shift_down(p, bnd_ref[0])
        seg = seg_ref[...]
        _, vjp = jax.vjp(lambda *a: _prep_fn(*a, seg), p, prev, mu_ref[...], w0_ref[...], a0_ref[...], kk_ref[...],
                         ka_ref[...], lora_ref[...])
        ct = (dkk_ref[...], dw_ref[...], db_ref[...], dk1_ref[...] + dk2_ref[...], dr1_ref[...] + dr2_ref[...],
              dv1_ref[...] + dv2_ref[...], dg_ref[...])
        grads = vjp(ct)
        dp_ref[...] = grads[0]
        dprev_ref[...] = grads[1]
        for a_ref, gval in zip(accs, grads[2:]):
            a_ref[...] += gval

    small = _full((1, W))
    row = pl.BlockSpec((tm, W), lambda i: (i, 0))
    return pl.pallas_call(
        body, name="rwkv_prep_bwd", grid=(T // tm,),
        in_specs=[pl.BlockSpec((tm, RW_COLS), lambda i: (i, 0)), pl.BlockSpec((1, 1, RW_COLS), lambda i: (i, 0, 0)),
                  _full((1, RW_COLS)), small, small, small, small, _full((2 * LORA, 2 * W)), _full((W, W))]
                 + [row] * 10,
        out_specs=[pl.BlockSpec((tm, RW_COLS), lambda i: (i, 0)), pl.BlockSpec((tm, RW_COLS), lambda i: (i, 0)),
                   _full((1, RW_COLS)), small, small, small, small, _full((2 * LORA, 2 * W))],
        out_shape=[jax.ShapeDtypeStruct((T, RW_COLS), F32), jax.ShapeDtypeStruct((T, RW_COLS), F32),
                   jax.ShapeDtypeStruct((1, RW_COLS), F32)] + [jax.ShapeDtypeStruct((1, W), F32)] * 4
                  + [jax.ShapeDtypeStruct((2 * LORA, 2 * W), F32)],
        compiler_params=_cparams(dimension_semantics=("arbitrary",)),
    )(p_rw, bnd, mu, w0, a0, k_k, k_a, lora, seg64, *cts)


def _post_fwd(o, r, k, v, g, gn_g, gn_b, r_k, seg64):
    T = o.shape[0]
    tm = _row_tile(T)
    W = RW_WIDTH

    def body(o_ref, r_ref, k_ref, v_ref, g_ref, gg_ref, gb_ref, rk_ref, seg_ref, y_ref):
        y_ref[...] = _post_fn(o_ref[...], r_ref[...], k_ref[...], v_ref[...], g_ref[...], gg_ref[...], gb_ref[...],
                              rk_ref[...], seg_ref[...]).astype(BF16)

    row = pl.BlockSpec((tm, W), lambda i: (i, 0))
    small = _full((1, W))
    return pl.pallas_call(
        body, name="rwkv_post_fwd", grid=(T // tm,),
        in_specs=[row] * 5 + [small] * 3 + [_full((W, W))],
        out_specs=row, out_shape=jax.ShapeDtypeStruct((T, W), BF16),
        compiler_params=_cparams(dimension_semantics=("arbitrary",)),
    )(o, r, k, v, g, gn_g, gn_b, r_k, seg64)


def _post_bwd(o, r, k, v, g, gn_g, gn_b, r_k, seg64, dy):
    T = o.shape[0]
    tm = _row_tile(T)
    W = RW_WIDTH

    def body(o_ref, r_ref, k_ref, v_ref, g_ref, gg_ref, gb_ref, rk_ref, seg_ref, dy_ref,
             do_ref, dr_ref, dk_ref, dv_ref, dg_ref, dgg_ref, dgb_ref, drk_ref):
        accs = (dgg_ref, dgb_ref, drk_ref)

        @pl.when(pl.program_id(0) == 0)
        def _():
            for a_ref in accs:
                a_ref[...] = jnp.zeros_like(a_ref)

        seg = seg_ref[...]
        _, vjp = jax.vjp(lambda *a: _post_fn(*a, seg), o_ref[...], r_ref[...], k_ref[...], v_ref[...], g_ref[...],
                         gg_ref[...], gb_ref[...], rk_ref[...])
        grads = vjp(dy_ref[...])
        for o_, gval in zip((do_ref, dr_ref, dk_ref, dv_ref, dg_ref), grads[:5]):
            o_[...] = gval
        for a_ref, gval in zip(accs, grads[5:]):
            a_ref[...] += gval

    row = pl.BlockSpec((tm, W), lambda i: (i, 0))
    small = _full((1, W))
    return pl.pallas_call(
        body, name="rwkv_post_bwd", grid=(T // tm,),
        in_specs=[row] * 5 + [small] * 3 + [_full((W, W)), pl.BlockSpec((tm, W), lambda i: (i, 1))],
        out_specs=[row] * 5 + [small] * 3,
        out_shape=[jax.ShapeDtypeStruct((T, W), F32)] * 5 + [jax.ShapeDtypeStruct((1, W), F32)] * 3,
        compiler_params=_cparams(dimension_semantics=("arbitrary",)),
    )(o, r, k, v, g, gn_g, gn_b, r_k, seg64, dy)


def _to_cols(vecs, T):
    Tc = WKV_CHUNK
    x = jnp.stack(vecs, axis=0).reshape(N_VEC, T // Tc, Tc, RW_HEADS, RW_HEAD)
    x = jnp.transpose(x, (1, 0, 4, 3, 2))
    return x.reshape(T // Tc, N_VEC * RW_HEAD, RW_HEADS * Tc)


def _from_cols(cols, T):
    Tc = WKV_CHUNK
    x = cols.reshape(T // Tc, N_VEC, RW_HEAD, RW_HEADS, Tc)
    x = jnp.transpose(x, (1, 0, 4, 3, 2)).reshape(N_VEC, T, RW_WIDTH)
    return [x[i] for i in range(N_VEC)]


def _wkv_lhs(cols_ref):
    hi, lo = _split(cols_ref[0])
    return jnp.concatenate([hi, lo], axis=1)


def _wkv_step(st, kk_e, w_e, b_e, k_e, v_row):
    sa = -jnp.sum(st * kk_e, axis=0, keepdims=True)
    return st * w_e + b_e * sa + k_e * v_row, sa


def _wkv_fwd(cols, v, e_tab):
    T = v.shape[0]
    Tc = WKV_CHUNK
    nch = T // Tc
    J, W = RW_HEAD, RW_WIDTH

    def body(cols_ref, v_ref, e_ref, o_ref, ckpt_ref, s_ref):
        @pl.when(pl.program_id(0) == 0)
        def _():
            s_ref[...] = jnp.zeros_like(s_ref)

        lhs = _wkv_lhs(cols_ref)
        st = s_ref[...]
        ckpt_ref[0] = st
        for t in range(Tc):
            ex = _dot(lhs, e_ref[t])
            st, _ = _wkv_step(st, ex[0:J], ex[J:2 * J], ex[2 * J:3 * J], ex[3 * J:4 * J], v_ref[t:t + 1, :])
            o_ref[t:t + 1, :] = jnp.sum(st * ex[4 * J:5 * J], axis=0, keepdims=True)
        s_ref[...] = st

    return pl.pallas_call(
        body, name="wkv_fwd", grid=(nch,),
        in_specs=[pl.BlockSpec((1, N_VEC * J, 128), lambda c: (c, 0, 0)), pl.BlockSpec((Tc, W), lambda c: (c, 0)),
                  _full((Tc, 2 * 128, W))],
        out_specs=[pl.BlockSpec((Tc, W), lambda c: (c, 0)), pl.BlockSpec((1, J, W), lambda c: (c, 0, 0))],
        out_shape=[jax.ShapeDtypeStruct((T, W), F32), jax.ShapeDtypeStruct((nch, J, W), F32)],
        scratch_shapes=[pltpu.VMEM((J, W), F32)],
        compiler_params=_cparams(dimension_semantics=("arbitrary",)),
    )(cols, v, e_tab)


def _wkv_bwd(cols, v, do, ckpt, e_tab, r_tab):
    T = v.shape[0]
    Tc = WKV_CHUNK
    nch = T // Tc
    J, W = RW_HEAD, RW_WIDTH

    def body(cols_ref, v_ref, do_ref, ckpt_ref, e_ref, r_ref, dv_ref, dcols_ref, ds_ref, sbuf, sabuf):
        @pl.when(pl.program_id(0) == 0)
        def _():
            ds_ref[...] = jnp.zeros_like(ds_ref)

        lhs = _wkv_lhs(cols_ref)
        st = ckpt_ref[0]
        for t in range(Tc):
            ex = _dot(lhs[0:4 * J], e_ref[t])
            sbuf[t] = st
            st, sa = _wkv_step(st, ex[0:J], ex[J:2 * J], ex[2 * J:3 * J], ex[3 * J:4 * J], v_ref[t:t + 1, :])
            sabuf[t] = sa
        sbuf[Tc] = st

        dst = ds_ref[...]
        acc = jnp.zeros((N_VEC * J, 128), F32)
        for t in reversed(range(Tc)):
            ex = _dot(lhs, e_ref[t])
            kk_e, w_e, b_e, k_e, r_e = (ex[i * J:(i + 1) * J] for i in range(N_VEC))
            s_new, s_old = sbuf[t + 1], sbuf[t]
            do_row, v_row, sa_row = do_ref[t:t + 1, :], v_ref[t:t + 1, :], sabuf[t]
            dsn = dst + r_e * do_row
            dsa = jnp.sum(dsn * b_e, axis=0, keepdims=True)
            dv_ref[t:t + 1, :] = jnp.sum(dsn * k_e, axis=0, keepdims=True)
            prods = jnp.concatenate([-(s_old * dsa), dsn * s_old, dsn * sa_row, dsn * v_row, s_new * do_row], axis=0)
            hi, lo = _split(prods)
            acc = acc + _dot(jnp.concatenate([hi, lo], axis=1), r_ref[t])
            dst = dsn * w_e - kk_e * dsa
        ds_ref[...] = dst
        dcols_ref[0] = acc

    rev2 = lambda c: (nch - 1 - c, 0)
    rev3 = lambda c: (nch - 1 - c, 0, 0)
    return pl.pallas_call(
        body, name="wkv_bwd", grid=(nch,),
        in_specs=[pl.BlockSpec((1, N_VEC * J, 128), rev3), pl.BlockSpec((Tc, W), rev2), pl.BlockSpec((Tc, W), rev2),
                  pl.BlockSpec((1, J, W), rev3), _full((Tc, 2 * 128, W)), _full((Tc, 2 * W, 128))],
        out_specs=[pl.BlockSpec((Tc, W), rev2), pl.BlockSpec((1, N_VEC * J, 128), rev3)],
        out_shape=[jax.ShapeDtypeStruct((T, W), F32), jax.ShapeDtypeStruct((nch, N_VEC * J, 128), F32)],
        scratch_shapes=[pltpu.VMEM((J, W), F32), pltpu.VMEM((Tc + 1, J, W), F32), pltpu.VMEM((Tc, 1, W), F32)],
        compiler_params=_cparams(dimension_semantics=("arbitrary",)),
    )(cols, v, do, ckpt, e_tab, r_tab)


def _outproj(x, y_ret, y_rw, w_out_b, target, gf):
    T = x.shape[0]
    tm = _row_tile(T)
    W = RW_WIDTH

    def body(x_ref, yr_ref, yw_ref, w_ref, t_ref, gf_ref, loss_ref, dh_ref, dy_ref, dw_ref, dgf_ref):
        @pl.when(pl.program_id(0) == 0)
        def _():
            loss_ref[...] = jnp.zeros_like(loss_ref)
            dw_ref[...] = jnp.zeros_like(dw_ref)
            dgf_ref[...] = jnp.zeros_like(dgf_ref)

        y = jnp.concatenate([yr_ref[...], yw_ref[...]], axis=1)
        w = w_ref[...]
        h = x_ref[...] + _dot(y, w)
        rstd = lax.rsqrt(jnp.mean(h * h, axis=-1, keepdims=True) + RMS_EPS)
        hn = h * rstd
        gfv = gf_ref[...]
        err = hn * gfv - t_ref[...]
        loss_ref[...] += 0.5 * jnp.sum(jnp.mean(err * err, axis=-1))
        dout = err * (1.0 / D_MODEL)
        dgf_ref[...] += jnp.sum(dout * hn, axis=0, keepdims=True)
        dhn = dout * gfv
        dh = rstd * (dhn - hn * jnp.mean(dhn * hn, axis=-1, keepdims=True))
        dh_ref[...] = dh
        dhb = dh.astype(BF16)
        dy_ref[...] = _dot_nt(dhb, w)
        dw_ref[...] += _dot_tn(y, dhb)

    return pl.pallas_call(
        body, name="outproj_loss", grid=(T // tm,),
        in_specs=[pl.BlockSpec((tm, D_MODEL), lambda i: (i, 0)), pl.BlockSpec((tm, W), lambda i: (i, 0)),
                  pl.BlockSpec((tm, W), lambda i: (i, 0)), _full((D_MODEL, D_MODEL)),
                  pl.BlockSpec((tm, D_MODEL), lambda i: (i, 0)), _full((1, D_MODEL))],
        out_specs=[_full((8, 128)), pl.BlockSpec((tm, D_MODEL), lambda i: (i, 0)),
                   pl.BlockSpec((tm, D_MODEL), lambda i: (i, 0)), _full((D_MODEL, D_MODEL)), _full((1, D_MODEL))],
        out_shape=[jax.ShapeDtypeStruct((8, 128), F32), jax.ShapeDtypeStruct((T, D_MODEL), F32),
                   jax.ShapeDtypeStruct((T, D_MODEL), F32), jax.ShapeDtypeStruct((D_MODEL, D_MODEL), F32),
                   jax.ShapeDtypeStruct((1, D_MODEL), F32)],
        compiler_params=_cparams(dimension_semantics=("arbitrary",)),
    )(x, y_ret, y_rw, w_out_b, target, gf)


def _inproj_bwd_x(dp_ret, dp_rw, dprev, dbnd, w_b, x, norm_g, dh):
    T = x.shape[0]
    tm = _row_tile(T)

    def body(dpr_ref, dpw_ref, dprev_ref, dbnd_ref, w_ref, x_ref, g_ref, dh_ref, gx_ref, dg_ref, dpb_ref):
        @pl.when(pl.program_id(0) == 0)
        def _():
            dg_ref[...] = jnp.zeros_like(dg_ref)

        d_rw = dpw_ref[...] + _shift_up(dprev_ref[...], dbnd_ref[0])
        dpb = jnp.concatenate([dpr_ref[...].astype(BF16), d_rw.astype(BF16)], axis=1)
        dpb_ref[:, :IN_COLS] = dpb
        dpb_ref[:, IN_COLS:] = jnp.zeros((tm, IN_COLS_PAD - IN_COLS), BF16)
        du = _dot_nt(dpb, w_ref[...])
        xf = x_ref[...]
        rstd = lax.rsqrt(jnp.mean(xf * xf, axis=-1, keepdims=True) + RMS_EPS)
        xn = xf * rstd
        dg_ref[...] += jnp.sum(du * xn, axis=0, keepdims=True)
        dxn = du * g_ref[...]
        gx_ref[...] = dh_ref[...] + rstd * (dxn - xn * jnp.mean(dxn * xn, axis=-1, keepdims=True))

    return pl.pallas_call(
        body, name="inproj_bwd_x", grid=(T // tm,),
        in_specs=[pl.BlockSpec((tm, RET_COLS), lambda i: (i, 0)), pl.BlockSpec((tm, RW_COLS), lambda i: (i, 0)),
                  pl.BlockSpec((tm, RW_COLS), lambda i: (i, 0)), pl.BlockSpec((1, 1, RW_COLS), lambda i: (i, 0, 0)),
                  _full((D_MODEL, IN_COLS)), pl.BlockSpec((tm, D_MODEL), lambda i: (i, 0)), _full((1, D_MODEL)),
                  pl.BlockSpec((tm, D_MODEL), lambda i: (i, 0))],
        out_specs=[pl.BlockSpec((tm, D_MODEL), lambda i: (i, 0)), _full((1, D_MODEL)),
                   pl.BlockSpec((tm, IN_COLS_PAD), lambda i: (i, 0))],
        out_shape=[jax.ShapeDtypeStruct((T, D_MODEL), F32), jax.ShapeDtypeStruct((1, D_MODEL), F32),
                   jax.ShapeDtypeStruct((T, IN_COLS_PAD), BF16)],
        compiler_params=_cparams(dimension_semantics=("arbitrary",)),
    )(dp_ret, dp_rw, dprev, dbnd, w_b, x, norm_g, dh)


def _inproj_bwd_w(u_t, dpb):
    T = u_t.shape[1]

    def body(u_ref, d_ref, o_ref):
        o_ref[...] = _dot(u_ref[...], d_ref[...])

    return pl.pallas_call(
        body, name="inproj_bwd_w", grid=(IN_COLS_PAD // DW_TILE,),
        in_specs=[_full((D_MODEL, T)), pl.BlockSpec((T, DW_TILE), lambda j: (0, j))],
        out_specs=pl.BlockSpec((D_MODEL, DW_TILE), lambda j: (0, j)),
        out_shape=jax.ShapeDtypeStruct((D_MODEL, IN_COLS_PAD), F32),
        compiler_params=_cparams(dimension_semantics=("arbitrary",)),
    )(u_t, dpb)


def _tile_boundaries(a, tm, first):
    T, n = a.shape
    zero = jnp.zeros((1, n), a.dtype)
    if first:
        rows = jnp.concatenate([zero, a[tm - 1:T - 1:tm]], axis=0)
    else:
        rows = jnp.concatenate([a[tm:T:tm], zero], axis=0)
    return rows.reshape(T // tm, 1, n)


def _local_step(x, target, w_in_b, w_out_b, lora, small):
    T = x.shape[0]
    tm = _row_tile(T)
    cos, sin = _rope_tables(T)
    tabs = _ret_tables()
    seg128 = _seg_matrix(RET_WIDTH, RET_DV)
    seg64 = _seg_matrix(RW_WIDTH, RW_HEAD)
    e_tab = _wkv_expand_table()
    r_tab = _wkv_reduce_table()
    prep_w = (small["rwkv_mu"], small["w0"], small["a0"], small["k_k"], small["k_a"], lora, seg64)
    post_w = (small["rwkv_gn_g"], small["rwkv_gn_b"], small["r_k"], seg64)

    p_ret, p_rw, u = _inproj_fwd(x, small["norm_g"], w_in_b)
    y_ret, ret, s_in_all = _ret_fwd(p_ret, cos, sin, tabs, small["ret_gn_g"], seg128)
    bnd = _tile_boundaries(p_rw, tm, True)
    kk, w, b, k, r, v, g = _prep_fwd(p_rw, bnd, *prep_w)
    cols = _to_cols([kk, w, b, k, r], T)
    o, ckpt = _wkv_fwd(cols, v, e_tab)
    y_rw = _post_fwd(o, r, k, v, g, *post_w)
    loss, dh, dy, d_w_out, d_gf = _outproj(x, y_ret, y_rw, w_out_b, target, small["final_norm_g"])

    do, dr2, dk2, dv2, dg, d_gn_g, d_gn_b, d_r_k = _post_bwd(o, r, k, v, g, *post_w, dy)
    dv1, dcols = _wkv_bwd(cols, v, do, ckpt, e_tab, r_tab)
    dkk, dw, db, dk1, dr1 = _from_cols(dcols, T)
    dp_rw, dprev, d_mu, d_w0, d_a0, d_k_k, d_k_a, d_lora = _prep_bwd(
        p_rw, bnd, *prep_w, (dkk, dw, db, dk1, dk2, dr1, dr2, dv1, dv2, dg))
    dp_ret, d_ret_gn = _ret_bwd(p_ret, cos, sin, tabs, small["ret_gn_g"], seg128, ret, s_in_all, dy)
    dbnd = _tile_boundaries(dprev, tm, False)
    grad_x, d_norm_g, dpb = _inproj_bwd_x(dp_ret, dp_rw, dprev, dbnd, w_in_b, x, small["norm_g"], dh)
    d_w_in = _inproj_bwd_w(jnp.transpose(u), dpb)

    d_small = {"norm_g": d_norm_g, "ret_gn_g": d_ret_gn, "rwkv_mu": d_mu, "w0": d_w0, "a0": d_a0, "k_k": d_k_k,
               "k_a": d_k_a, "r_k": d_r_k, "rwkv_gn_g": d_gn_g, "rwkv_gn_b": d_gn_b, "final_norm_g": d_gf}
    return loss, grad_x, d_w_in, d_w_out, d_lora, d_small


ANY = pl.BlockSpec(memory_space=pl.ANY)
CHIP_FLIPS = ((0, 1), (1, 0), (1, 1))
N_FLIPS = len(CHIP_FLIPS)
SMALL_NAMES = ("norm_g", "ret_gn_g", "rwkv_mu", "w0", "a0", "k_k", "k_a", "r_k", "rwkv_gn_g", "rwkv_gn_b",
               "final_norm_g")
SMALL_SIZES = (1024, 512, 2176, 512, 512, 512, 512, 512, 512, 512, 1024)
SMALL_ROWS = sum(SMALL_SIZES) // 128
LORA_ROWS = LORA * RW_WIDTH // 128
RED_ROWS = 584
LORA_SHARD = RW_WIDTH // N_CHIPS
ADAM_ROWS = 200
HALF_IN = D_MODEL // 2
HALF_OUT = OUT_SHARD // 2


def _position():
    return lax.axis_index("x"), lax.axis_index("y"), lax.axis_index("c")


def _flip(v, f):
    return 1 - v if f else v


def _finish(local, remote, landed):
    for cp in landed:
        cp.wait_recv()
    for cp in remote:
        cp.wait_send()
    for cp in local:
        cp.wait()


def _gather_chips(arrs):
    n = len(arrs)

    def body(*refs):
        ins, outs = refs[:n], refs[n:2 * n]
        send, recv, loc = refs[2 * n:]
        x, y, c = _position()
        s = 2 * x + y
        local, remote, landed = [], [], []
        for a in range(n):
            local.append(pltpu.make_async_copy(ins[a], outs[a].at[s], loc.at[a]))
            for j, (fx, fy) in enumerate(CHIP_FLIPS):
                px, py = _flip(x, fx), _flip(y, fy)
                k = a * N_FLIPS + j
                remote.append(pltpu.make_async_remote_copy(
                    src_ref=ins[a], dst_ref=outs[a].at[s], send_sem=send.at[k], recv_sem=recv.at[k],
                    device_id=(px, py, c), device_id_type=MESH))
                landed.append(pltpu.make_async_remote_copy(
                    src_ref=ins[a], dst_ref=outs[a].at[2 * px + py], send_sem=send.at[k], recv_sem=recv.at[k],
                    device_id=(px, py, c), device_id_type=MESH))
        for cp in local + remote:
            cp.start()
        _finish(local, remote, landed)

    return pl.pallas_call(
        body, name="gather_weights",
        in_specs=[ANY] * n, out_specs=[ANY] * n,
        out_shape=[jax.ShapeDtypeStruct((N_CHIPS,) + a.shape, a.dtype) for a in arrs],
        scratch_shapes=[pltpu.SemaphoreType.DMA((n * N_FLIPS,)), pltpu.SemaphoreType.DMA((n * N_FLIPS,)),
                        pltpu.SemaphoreType.DMA((n,))],
    )(*arrs)


def _pair_exchange(g_in, g_out, g_small):
    def body(gi_ref, go_ref, gs_ref, li_ref, lo_ref, ls_ref, send, recv):
        x, y, c = _position()
        peer = (x, y, 1 - c)
        srcs = (gi_ref.at[:, pl.ds((1 - c) * HALF_IN, HALF_IN), :], go_ref.at[:, pl.ds((1 - c) * HALF_OUT, HALF_OUT), :],
                gs_ref)
        remote = [pltpu.make_async_remote_copy(src_ref=src, dst_ref=dst, send_sem=send.at[k], recv_sem=recv.at[k],
                                               device_id=peer, device_id_type=MESH)
                  for k, (src, dst) in enumerate(zip(srcs, (li_ref, lo_ref, ls_ref)))]
        for cp in remote:
            cp.start()
        _finish([], remote, remote)

    return pl.pallas_call(
        body, name="pair_exchange",
        in_specs=[ANY] * 3, out_specs=[ANY] * 3,
        out_shape=[jax.ShapeDtypeStruct((N_CHIPS, HALF_IN, IN_SHARD), F32),
                   jax.ShapeDtypeStruct((N_CHIPS, HALF_OUT, D_MODEL), F32),
                   jax.ShapeDtypeStruct(g_small.shape, F32)],
        scratch_shapes=[pltpu.SemaphoreType.DMA((3,)), pltpu.SemaphoreType.DMA((3,))],
    )(g_in, g_out, g_small)


def _pair_sum(g_in, g_out, g_small, l_in, l_out, l_small, c_arr):
    tr = HALF_IN // 2

    def body(c_ref, gi_ref, go_ref, gs_ref, li_ref, lo_ref, ls_ref, ci_ref, co_ref, cs_ref):
        ci_ref[...] = gi_ref[...] + li_ref[...]

        @pl.when(pl.program_id(1) == 0)
        def _():
            co_ref[...] = go_ref[...] + lo_ref[...]

        @pl.when((pl.program_id(0) == 0) & (pl.program_id(1) == 0))
        def _():
            cs_ref[...] = gs_ref[...] + ls_ref[...]

    nd = g_small.shape
    return pl.pallas_call(
        body, name="pair_sum",
        grid_spec=pltpu.PrefetchScalarGridSpec(
            num_scalar_prefetch=1, grid=(N_CHIPS, 2),
            in_specs=[pl.BlockSpec((1, tr, IN_SHARD), lambda s, i, c: (s, 2 * c[0] + i, 0)),
                      pl.BlockSpec((1, HALF_OUT, D_MODEL), lambda s, i, c: (s, c[0], 0)),
                      pl.BlockSpec(nd, lambda s, i, c: (0, 0)),
                      pl.BlockSpec((1, tr, IN_SHARD), lambda s, i, c: (s, i, 0)),
                      pl.BlockSpec((1, HALF_OUT, D_MODEL), lambda s, i, c: (s, 0, 0)),
                      pl.BlockSpec(nd, lambda s, i, c: (0, 0))],
            out_specs=[pl.BlockSpec((1, tr, IN_SHARD), lambda s, i, c: (s, i, 0)),
                       pl.BlockSpec((1, HALF_OUT, D_MODEL), lambda s, i, c: (s, 0, 0)),
                       pl.BlockSpec(nd, lambda s, i, c: (0, 0))]),
        out_shape=[jax.ShapeDtypeStruct((N_CHIPS, HALF_IN, IN_SHARD), F32),
                   jax.ShapeDtypeStruct((N_CHIPS, HALF_OUT, D_MODEL), F32), jax.ShapeDtypeStruct(nd, F32)],
        compiler_params=_cparams(dimension_semantics=("arbitrary", "arbitrary")),
    )(c_arr, g_in, g_out, g_small, l_in, l_out, l_small)


def _chip_exchange(c_in, c_out, c_small):
    def body(ci_ref, co_ref, cs_ref, li_ref, lo_ref, ls_ref, send, recv):
        x, y, c = _position()
        s = 2 * x + y
        remote = []
        for j, (fx, fy) in enumerate(CHIP_FLIPS):
            px, py = _flip(x, fx), _flip(y, fy)
            ps = 2 * px + py
            for a, (src, dst) in enumerate(((ci_ref.at[ps], li_ref.at[j]), (co_ref.at[ps], lo_ref.at[j]),
                                            (cs_ref, ls_ref.at[j]))):
                k = 3 * j + a
                remote.append(pltpu.make_async_remote_copy(src_ref=src, dst_ref=dst, send_sem=send.at[k],
                                                           recv_sem=recv.at[k], device_id=(px, py, c),
                                                           device_id_type=MESH))
        for cp in remote:
            cp.start()
        _finish([], remote, remote)

    return pl.pallas_call(
        body, name="chip_exchange",
        in_specs=[ANY] * 3, out_specs=[ANY] * 3,
        out_shape=[jax.ShapeDtypeStruct((N_FLIPS, HALF_IN, IN_SHARD), F32),
                   jax.ShapeDtypeStruct((N_FLIPS, HALF_OUT, D_MODEL), F32),
                   jax.ShapeDtypeStruct((N_FLIPS,) + c_small.shape, F32)],
        scratch_shapes=[pltpu.SemaphoreType.DMA((3 * N_FLIPS,)), pltpu.SemaphoreType.DMA((3 * N_FLIPS,))],
    )(c_in, c_out, c_small)


def _chip_sum(c_in, c_out, c_small, l_in, l_out, l_small, s_arr):
    tr = HALF_IN // 2
    nd = c_small.shape

    def body(s_ref, ci_ref, co_ref, cs_ref, li0, li1, li2, lo0, lo1, lo2, ls_ref, ri_ref, ro_ref, rs_ref):
        ri_ref[...] = ((ci_ref[0] + li0[0]) + li1[0]) + li2[0]

        @pl.when(pl.program_id(0) == 0)
        def _():
            ro_ref[...] = ((co_ref[0] + lo0[0]) + lo1[0]) + lo2[0]
            me = s_ref[0]
            parts = (cs_ref[...], ls_ref[0], ls_ref[1], ls_ref[2])

            def of_chip(s):
                m = jnp.bitwise_xor(me, s)
                return jnp.where(m == 0, parts[0], jnp.where(m == 1, parts[1], jnp.where(m == 2, parts[2], parts[3])))

            rs_ref[...] = ((of_chip(0) + of_chip(1)) + of_chip(2)) + of_chip(3)

    def flip_in(j):
        return pl.BlockSpec((1, tr, IN_SHARD), lambda i, s: (j, i, 0))

    def flip_out(j):
        return pl.BlockSpec((1, HALF_OUT, D_MODEL), lambda i, s: (j, 0, 0))

    return pl.pallas_call(
        body, name="chip_sum",
        grid_spec=pltpu.PrefetchScalarGridSpec(
            num_scalar_prefetch=1, grid=(2,),
            in_specs=[pl.BlockSpec((1, tr, IN_SHARD), lambda i, s: (s[0], i, 0)),
                      pl.BlockSpec((1, HALF_OUT, D_MODEL), lambda i, s: (s[0], 0, 0)),
                      pl.BlockSpec(nd, lambda i, s: (0, 0)),
                      flip_in(0), flip_in(1), flip_in(2), flip_out(0), flip_out(1), flip_out(2),
                      pl.BlockSpec((N_FLIPS,) + nd, lambda i, s: (0, 0, 0))],
            out_specs=[pl.BlockSpec((tr, IN_SHARD), lambda i, s: (i, 0)),
                       pl.BlockSpec((HALF_OUT, D_MODEL), lambda i, s: (0, 0)),
                       pl.BlockSpec(nd, lambda i, s: (0, 0))]),
        out_shape=[jax.ShapeDtypeStruct((HALF_IN, IN_SHARD), F32), jax.ShapeDtypeStruct((HALF_OUT, D_MODEL), F32),
                   jax.ShapeDtypeStruct(nd, F32)],
        compiler_params=_cparams(dimension_semantics=("arbitrary",)),
    )(s_arr, c_in, c_out, c_small, l_in, l_in, l_in, l_out, l_out, l_out, l_small)


def _pair_share(r_in, r_out):
    def body(ri_ref, ro_ref, fi_ref, fo_ref, send, recv, loc):
        x, y, c = _position()
        peer = (x, y, 1 - c)
        local, remote, landed = [], [], []
        for k, (src, dst, rows) in enumerate(((ri_ref, fi_ref, HALF_IN), (ro_ref, fo_ref, HALF_OUT))):
            local.append(pltpu.make_async_copy(src, dst.at[pl.ds(c * rows, rows), :], loc.at[k]))
            remote.append(pltpu.make_async_remote_copy(
                src_ref=src, dst_ref=dst.at[pl.ds(c * rows, rows), :], send_sem=send.at[k], recv_sem=recv.at[k],
                device_id=peer, device_id_type=MESH))
            landed.append(pltpu.make_async_remote_copy(
                src_ref=src, dst_ref=dst.at[pl.ds((1 - c) * rows, rows), :], send_sem=send.at[k], recv_sem=recv.at[k],
                device_id=peer, device_id_type=MESH))
        for cp in local + remote:
            cp.start()
        _finish(local, remote, landed)

    return pl.pallas_call(
        body, name="pair_share",
        in_specs=[ANY] * 2, out_specs=[ANY] * 2,
        out_shape=[jax.ShapeDtypeStruct((D_MODEL, IN_SHARD), F32), jax.ShapeDtypeStruct((OUT_SHARD, D_MODEL), F32)],
        scratch_shapes=[pltpu.SemaphoreType.DMA((2,)), pltpu.SemaphoreType.DMA((2,)), pltpu.SemaphoreType.DMA((2,))],
    )(r_in, r_out)


def _adamw(name, w, g, m, v, tr):
    rows, cols = w.shape

    def body(w_ref, g_ref, m_ref, v_ref, d_ref, nm_ref, nv_ref):
        gv = g_ref[...]
        mn = ADAM_B1 * m_ref[...] + (1.0 - ADAM_B1) * gv
        vn = ADAM_B2 * v_ref[...] + (1.0 - ADAM_B2) * jnp.square(gv)
        m_hat = mn / (1.0 - ADAM_B1 ** ADAM_STEP)
        v_hat = vn / (1.0 - ADAM_B2 ** ADAM_STEP)
        d_ref[...] = -ADAM_LR * (m_hat / (jnp.sqrt(v_hat) + ADAM_EPS) + ADAM_WD * w_ref[...])
        nm_ref[...] = mn
        nv_ref[...] = vn

    spec = pl.BlockSpec((tr, cols), lambda i: (i, 0))
    return pl.pallas_call(
        body, name=name, grid=(rows // tr,), in_specs=[spec] * 4, out_specs=[spec] * 3,
        out_shape=[jax.ShapeDtypeStruct((rows, cols), F32)] * 3,
        compiler_params=_cparams(dimension_semantics=("arbitrary",)),
    )(w, g, m, v)


def _pack_rows(parts, rows):
    flat = jnp.concatenate([p.reshape(-1) for p in parts])
    return jnp.pad(flat, (0, rows * 128 - flat.shape[0])).reshape(rows, 128)


def _unpack_small(packed):
    out, at = {}, 0
    for name, n in zip(SMALL_NAMES, SMALL_SIZES):
        out[name] = packed[at // 128:(at + n) // 128].reshape(n)
        at += n
    return out


def kernel(x, norm_g, w_in, ret_gn_g, rwkv_mu, w_lora_up, w0, a_lora_up, a0, k_k, k_a, r_k, rwkv_gn_g, rwkv_gn_b, w_out, final_norm_g, loss_target, m_norm_g, m_w_in, m_ret_gn_g, m_rwkv_mu, m_w_lora_up, m_w0, m_a_lora_up, m_a0, m_k_k, m_k_a, m_r_k, m_rwkv_gn_g, m_rwkv_gn_b, m_w_out, m_final_norm_g, v_norm_g, v_w_in, v_ret_gn_g, v_rwkv_mu, v_w_lora_up, v_w0, v_a_lora_up, v_a0, v_k_k, v_k_a, v_r_k, v_rwkv_gn_g, v_rwkv_gn_b, v_w_out, v_final_norm_g):
    W = RW_WIDTH
    params = dict(norm_g=norm_g, ret_gn_g=ret_gn_g, rwkv_mu=rwkv_mu, w0=w0, a0=a0, k_k=k_k, k_a=k_a, r_k=r_k,
                  rwkv_gn_g=rwkv_gn_g, rwkv_gn_b=rwkv_gn_b, final_norm_g=final_norm_g)
    moments_m = dict(norm_g=m_norm_g, ret_gn_g=m_ret_gn_g, rwkv_mu=m_rwkv_mu, w0=m_w0, a0=m_a0, k_k=m_k_k, k_a=m_k_a,
                     r_k=m_r_k, rwkv_gn_g=m_rwkv_gn_g, rwkv_gn_b=m_rwkv_gn_b, final_norm_g=m_final_norm_g)
    moments_v = dict(norm_g=v_norm_g, ret_gn_g=v_ret_gn_g, rwkv_mu=v_rwkv_mu, w0=v_w0, a0=v_a0, k_k=v_k_k, k_a=v_k_a,
                     r_k=v_r_k, rwkv_gn_g=v_rwkv_gn_g, rwkv_gn_b=v_rwkv_gn_b, final_norm_g=v_final_norm_g)
    xi, yi, ci = _position()
    chip = (2 * xi + yi).astype(jnp.int32)

    g_in, g_out, g_lw, g_la = _gather_chips([w_in[0].astype(BF16), w_out[0].astype(BF16), w_lora_up[0], a_lora_up[0]])
    w_in_b = jnp.transpose(g_in, (1, 0, 2)).reshape(D_MODEL, IN_COLS)
    w_out_b = g_out.reshape(D_MODEL, D_MODEL)
    lw = jnp.transpose(g_lw, (1, 0, 2)).reshape(LORA, W)
    la = jnp.transpose(g_la, (1, 0, 2)).reshape(LORA, W)
    zero = jnp.zeros((LORA, W), F32)
    lora = jnp.concatenate([jnp.concatenate([lw, zero], axis=1), jnp.concatenate([zero, la], axis=1)], axis=0)
    small = {n: params[n].reshape(1, -1) for n in SMALL_NAMES}

    loss, grad_x, d_w_in, d_w_out, d_lora, d_small = _local_step(x[0], loss_target[0], w_in_b, w_out_b, lora, small)

    gi = jnp.transpose(d_w_in[:, :IN_COLS].reshape(D_MODEL, N_CHIPS, IN_SHARD), (1, 0, 2))
    go = d_w_out.reshape(N_CHIPS, OUT_SHARD, D_MODEL)
    gs = _pack_rows([d_small[n] for n in SMALL_NAMES] + [d_lora[:LORA, :W], d_lora[LORA:, W:]], RED_ROWS)
    li, lo, ls = _pair_exchange(gi, go, gs)
    c_in, c_out, c_small = _pair_sum(gi, go, gs, li, lo, ls, ci.astype(jnp.int32).reshape(1))
    li2, lo2, ls2 = _chip_exchange(c_in, c_out, c_small)
    r_in, r_out, tot_small = _chip_sum(c_in, c_out, c_small, li2, lo2, ls2, chip.reshape(1))
    grad_w_in, grad_w_out = _pair_share(r_in, r_out)

    grad_small = _unpack_small(tot_small)
    lora_at = SMALL_ROWS
    grad_lw_full = tot_small[lora_at:lora_at + LORA_ROWS].reshape(LORA, W)
    grad_la_full = tot_small[lora_at + LORA_ROWS:lora_at + 2 * LORA_ROWS].reshape(LORA, W)
    grad_lw = lax.dynamic_slice(grad_lw_full, (0, chip * LORA_SHARD), (LORA, LORA_SHARD))
    grad_la = lax.dynamic_slice(grad_la_full, (0, chip * LORA_SHARD), (LORA, LORA_SHARD))

    d_in, nm_in, nv_in = _adamw("adamw_w_in", w_in[0], grad_w_in, m_w_in[0], v_w_in[0], 256)
    d_out, nm_out, nv_out = _adamw("adamw_w_out", w_out[0], grad_w_out, m_w_out[0], v_w_out[0], OUT_SHARD)

    def packed(tree, lw_, la_):
        return _pack_rows([tree[n] for n in SMALL_NAMES] + [lw_, la_], ADAM_ROWS)

    d_p, nm_p, nv_p = _adamw(
        "adamw_small", packed(params, w_lora_up[0], a_lora_up[0]), packed(grad_small, grad_lw, grad_la),
        packed(moments_m, m_w_lora_up[0], m_a_lora_up[0]), packed(moments_v, v_w_lora_up[0], v_a_lora_up[0]), ADAM_ROWS)

    def unpack_all(p):
        out = _unpack_small(p)
        at = SMALL_ROWS
        out["w_lora_up"] = p[at:at + LORA].reshape(LORA, LORA_SHARD)
        out["a_lora_up"] = p[at + LORA:at + 2 * LORA].reshape(LORA, LORA_SHARD)
        return out

    names = ("norm_g", "w_in", "ret_gn_g", "rwkv_mu", "w_lora_up", "w0", "a_lora_up", "a0", "k_k", "k_a", "r_k",
             "rwkv_gn_g", "rwkv_gn_b", "w_out", "final_norm_g")
    shapes = dict(w_in=w_in.shape, w_out=w_out.shape, w_lora_up=w_lora_up.shape, a_lora_up=a_lora_up.shape,
                  **{n: params[n].shape for n in SMALL_NAMES})

    def leaves(smalls, big_in, big_out):
        tree = dict(smalls, w_in=big_in, w_out=big_out)
        return [tree[n].reshape(shapes[n]) for n in names]

    grads = leaves(dict(grad_small, w_lora_up=grad_lw, a_lora_up=grad_la), grad_w_in, grad_w_out)
    deltas = leaves(unpack_all(d_p), d_in, d_out)
    new_m = leaves(unpack_all(nm_p), nm_in, nm_out)
    new_v = leaves(unpack_all(nv_p), nv_in, nv_out)
    total_loss = lax.psum(loss[0, 0], ("x", "y", "c"))
    return (total_loss, grad_x.reshape(x.shape), *grads, *deltas, *new_m, *new_v)
```

```python
import functools

import numpy as np
import jax
import jax.numpy as jnp
from jax import lax
from jax.experimental import pallas as pl
from jax.experimental.pallas import tpu as pltpu

F32 = jnp.float32
BF16 = jnp.bfloat16
HIGHEST = lax.Precision.HIGHEST
MESH = pl.DeviceIdType.MESH

D_MODEL = 1024
N_CHIPS = 4
RET_HEADS = 4
RET_DK = 64
RET_DV = 128
RET_QK = RET_HEADS * RET_DK
RET_WIDTH = RET_HEADS * RET_DV
RET_COLS = 2 * RET_QK + 2 * RET_WIDTH
RET_CHUNK = 64
RW_WIDTH = 512
RW_HEAD = 64
RW_HEADS = 8
LORA = 64
RW_COLS = 4 * RW_WIDTH + 2 * LORA
IN_COLS = RET_COLS + RW_COLS
IN_SHARD = IN_COLS // N_CHIPS
OUT_SHARD = D_MODEL // N_CHIPS
ROPE_BASE = 10000.0
RMS_EPS = 1e-6
RET_GN_EPS = 1e-5
RW_GN_EPS = 64e-5
WKV_CHUNK = 16
N_VEC = 5

ADAM_LR = 0.001
ADAM_B1 = 0.9
ADAM_B2 = 0.999
ADAM_EPS = 1e-08
ADAM_WD = 0.01
ADAM_STEP = 10

VMEM_LIMIT = 56 * 1024 * 1024

PACK_W = 512
SMALL_NAMES = ("norm_g", "ret_gn_g", "rwkv_mu", "w0", "a0", "k_k", "k_a", "r_k", "rwkv_gn_g", "rwkv_gn_b",
               "final_norm_g")
SMALL_SIZES = (1024, 512, 2176, 512, 512, 512, 512, 512, 512, 512, 1024)
PACK_LORA_W = 0
PACK_LORA_A = LORA
PACK_SMALL = 2 * LORA


def _pack_layout():
    rows, at = {}, PACK_SMALL
    for name, n in zip(SMALL_NAMES, SMALL_SIZES):
        rows[name] = at
        at += -(-n // PACK_W)
    return rows, at


PACK_AT, PACK_LOSS = _pack_layout()
PACK_ROWS = -(-(PACK_LOSS + 1) // 8) * 8


def _cparams(**kw):
    return pltpu.CompilerParams(vmem_limit_bytes=VMEM_LIMIT, **kw)


def _dot(a, b, precision=None):
    return jnp.dot(a, b, precision=precision, preferred_element_type=F32)


def _dot_nt(a, b, precision=None):
    return lax.dot_general(a, b, (((1,), (1,)), ((), ())), precision=precision, preferred_element_type=F32)


def _dot_tn(a, b, precision=None):
    return lax.dot_general(a, b, (((0,), (0,)), ((), ())), precision=precision, preferred_element_type=F32)


def _split(x):
    hi = x.astype(BF16)
    lo = (x - hi.astype(F32)).astype(BF16)
    return hi, lo


@jax.custom_vjp
def _segsum(x, seg):
    hi, lo = _split(x)
    return _dot(hi, seg) + _dot(lo, seg)


def _segsum_fwd(x, seg):
    return _segsum(x, seg), seg


def _segsum_bwd(seg, ct):
    return _segsum(ct, seg), jnp.zeros_like(seg)


_segsum.defvjp(_segsum_fwd, _segsum_bwd)


def _softplus(z):
    return jnp.maximum(z, 0.0) + jnp.log(1.0 + jnp.exp(-jnp.abs(z)))


def _full(shape):
    nd = len(shape)
    return pl.BlockSpec(shape, lambda *_: (0,) * nd)


def _rope_tables(T):
    half = RET_DK // 2
    expo = -jnp.arange(half, dtype=F32) / jnp.float32(half)
    freqs = jnp.exp(expo * jnp.float32(np.log(ROPE_BASE)))
    ang = jnp.arange(T, dtype=jnp.int32).astype(F32)[:, None] * freqs[None, :]
    cos = jnp.tile(jnp.cos(ang), (1, 2 * RET_HEADS))
    sin = jnp.tile(jnp.sin(ang), (1, 2 * RET_HEADS))
    return cos, sin


def _ret_tables():
    H, C = RET_HEADS, RET_CHUNK
    hidx = jnp.arange(H, dtype=F32)
    lg = jnp.log(1.0 - jnp.exp2(-5.0 - hidx))
    idx = jnp.arange(C, dtype=F32)
    intra = jnp.exp(lg[:, None, None] * jnp.abs(idx[:, None] - idx[None, :]))
    q_dec = jnp.transpose(jnp.exp(lg[:, None] * (idx[None, :] + 1.0)))
    k_dec = jnp.transpose(jnp.exp(lg[:, None] * (C - 1.0 - idx[None, :])))
    chunk_dec = jnp.exp(lg * C)
    qd = jnp.repeat(q_dec, RET_DK, axis=1)
    kd = jnp.repeat(k_dec, RET_DK, axis=1)
    row_h = np.arange(RET_QK) // RET_DK
    col_h = np.arange(RET_WIDTH) // RET_DV
    bm = jnp.asarray((row_h[:, None] == col_h[None, :]).astype(np.float32))
    cd = bm * jnp.repeat(chunk_dec, RET_DK)[:, None]
    return intra, qd, kd, cd, bm


def _seg_matrix(width, head):
    h = np.arange(width) // head
    return jnp.asarray((h[:, None] == h[None, :]).astype(np.float32), dtype=BF16)


def _wkv_expand_table():
    Tc = WKV_CHUNK
    k = np.arange(2 * RW_HEADS * Tc)
    kh, kt = (k % (RW_HEADS * Tc)) // Tc, k % Tc
    nh = np.arange(RW_WIDTH) // RW_HEAD
    e = (kh[None, :, None] == nh[None, None, :]) & (kt[None, :, None] == np.arange(Tc)[:, None, None])
    return jnp.asarray(e.astype(np.float32), dtype=BF16)


def _wkv_reduce_table():
    Tc = WKV_CHUNK
    kh = np.arange(RW_WIDTH) // RW_HEAD
    n = np.arange(RW_HEADS * Tc)
    nh, nt = n // Tc, n % Tc
    r = (kh[None, :, None] == nh[None, None, :]) & (nt[None, None, :] == np.arange(Tc)[:, None, None])
    return jnp.asarray(r.astype(np.float32), dtype=BF16)


def _inproj_fwd(x, norm_g, w_b):
    T = x.shape[0]
    tm = min(T, 256)

    def body(x_ref, g_ref, w_ref, pret_ref, prw_ref, u_ref):
        xf = x_ref[...]
        rstd = lax.rsqrt(jnp.mean(xf * xf, axis=-1, keepdims=True) + RMS_EPS)
        ub = ((xf * rstd) * g_ref[...]).astype(BF16)
        u_ref[...] = ub
        pret_ref[...] = _dot(ub, w_ref[:, :RET_COLS])
        prw_ref[...] = _dot(ub, w_ref[:, RET_COLS:])

    return pl.pallas_call(
        body, name="inproj_fwd", grid=(T // tm,),
        in_specs=[pl.BlockSpec((tm, D_MODEL), lambda i: (i, 0)), _full((1, D_MODEL)), _full((D_MODEL, IN_COLS))],
        out_specs=[pl.BlockSpec((tm, RET_COLS), lambda i: (i, 0)), pl.BlockSpec((tm, RW_COLS), lambda i: (i, 0)),
                   pl.BlockSpec((tm, D_MODEL), lambda i: (i, 0))],
        out_shape=[jax.ShapeDtypeStruct((T, RET_COLS), F32), jax.ShapeDtypeStruct((T, RW_COLS), F32),
                   jax.ShapeDtypeStruct((T, D_MODEL), BF16)],
        compiler_params=_cparams(dimension_semantics=("arbitrary",)),
    )(x, norm_g, w_b)


def _rot_half(x):
    n = x.shape[1]
    lane = lax.broadcasted_iota(jnp.int32, x.shape, 1)
    first = (lane % RET_DK) < (RET_DK // 2)
    return jnp.where(first, -pltpu.roll(x, n - RET_DK // 2, 1), pltpu.roll(x, RET_DK // 2, 1))


def _rope(x, cos, sin):
    return x * cos + _rot_half(x) * sin


def _rope_bwd(d, cos, sin):
    return d * cos - _rot_half(d * sin)


def _ret_post(ret, g, gn_g, seg):
    mu = _segsum(ret, seg) * (1.0 / RET_DV)
    xc = ret - mu
    var = _segsum(xc * xc, seg) * (1.0 / RET_DV)
    n = xc * lax.rsqrt(var + RET_GN_EPS)
    return (g * jax.nn.sigmoid(g)) * (n * gn_g)


def _ret_scores(qt, kt, d_ref, h):
    lane = lax.broadcasted_iota(jnp.int32, qt.shape, 1)
    qh = jnp.where(lane // RET_DK == h, qt, 0.0)
    return qh, _dot_nt(qh, kt, HIGHEST) * d_ref[h]


def _ret_fwd(p_ret, cos, sin, tabs, gn_g, seg128):
    T = p_ret.shape[0]
    C = RET_CHUNK
    nch = T // C
    intra_d, qd, kd, cd, bm = tabs

    def body(q_ref, k_ref, v_ref, g_ref, cos_ref, sin_ref, qd_ref, kd_ref, d_ref, cd_ref, bm_ref, gn_ref, seg_ref,
             y_ref, ret_ref, sin_out_ref, s_ref):
        @pl.when(pl.program_id(0) == 0)
        def _():
            s_ref[...] = jnp.zeros_like(s_ref)

        cosv, sinv = cos_ref[...], sin_ref[...]
        qt = _rope(q_ref[...], cosv, sinv)
        kt = _rope(k_ref[...], cosv, sinv) * (RET_DK ** -0.5)
        v = v_ref[...]
        s_in = s_ref[...]
        sin_out_ref[0] = s_in
        inter = _dot(qt * qd_ref[...], s_in, HIGHEST)
        intra = []
        for h in range(RET_HEADS):
            _, a = _ret_scores(qt, kt, d_ref, h)
            intra.append(_dot(a, v[:, h * RET_DV:(h + 1) * RET_DV], HIGHEST))
        ret = jnp.concatenate(intra, axis=1) + inter
        kv = _dot_tn(kt * kd_ref[...], v, HIGHEST)
        s_ref[...] = s_in * cd_ref[...] + kv * bm_ref[...]
        ret_ref[...] = ret
        y_ref[...] = _ret_post(ret, g_ref[...], gn_ref[...], seg_ref[...]).astype(BF16)

    return pl.pallas_call(
        body, name="ret_fwd", grid=(nch,),
        in_specs=[pl.BlockSpec((C, RET_QK), lambda c: (c, 0)), pl.BlockSpec((C, RET_QK), lambda c: (c, 1)),
                  pl.BlockSpec((C, RET_WIDTH), lambda c: (c, 1)), pl.BlockSpec((C, RET_WIDTH), lambda c: (c, 2)),
                  pl.BlockSpec((C, RET_QK), lambda c: (c, 0)), pl.BlockSpec((C, RET_QK), lambda c: (c, 0)),
                  _full((C, RET_QK)), _full((C, RET_QK)), _full((RET_HEADS, C, C)),
                  _full((RET_QK, RET_WIDTH)), _full((RET_QK, RET_WIDTH)), _full((1, RET_WIDTH)),
                  _full((RET_WIDTH, RET_WIDTH))],
        out_specs=[pl.BlockSpec((C, RET_WIDTH), lambda c: (c, 0)), pl.BlockSpec((C, RET_WIDTH), lambda c: (c, 0)),
                   pl.BlockSpec((1, RET_QK, RET_WIDTH), lambda c: (c, 0, 0))],
        out_shape=[jax.ShapeDtypeStruct((T, RET_WIDTH), BF16), jax.ShapeDtypeStruct((T, RET_WIDTH), F32),
                   jax.ShapeDtypeStruct((nch, RET_QK, RET_WIDTH), F32)],
        scratch_shapes=[pltpu.VMEM((RET_QK, RET_WIDTH), F32)],
        compiler_params=_cparams(dimension_semantics=("arbitrary",)),
    )(p_ret, p_ret, p_ret, p_ret, cos, sin, qd, kd, intra_d, cd, bm, gn_g, seg128)


def _ret_bwd(p_ret, cos, sin, tabs, gn_g, seg128, ret, s_in_all, dy):
    T = p_ret.shape[0]
    C = RET_CHUNK
    nch = T // C
    intra_d, qd, kd, cd, bm = tabs

    def rev(j):
        return lambda c: (nch - 1 - c, j)

    def body(q_ref, k_ref, v_ref, g_ref, cos_ref, sin_ref, qd_ref, kd_ref, d_ref, cd_ref, bm_ref, gn_ref, seg_ref,
             ret_ref, sin_ref_, dy_ref, dp_ref, dgn_ref, ds_ref):
        @pl.when(pl.program_id(0) == 0)
        def _():
            ds_ref[...] = jnp.zeros_like(ds_ref)
            dgn_ref[...] = jnp.zeros_like(dgn_ref)

        cosv, sinv = cos_ref[...], sin_ref[...]
        qt = _rope(q_ref[...], cosv, sinv)
        kt = _rope(k_ref[...], cosv, sinv) * (RET_DK ** -0.5)
        v = v_ref[...]
        s_in = sin_ref_[0]
        ds_out = ds_ref[...]
        seg = seg_ref[...]

        _, post_vjp = jax.vjp(lambda r_, g_, gn_: _ret_post(r_, g_, gn_, seg), ret_ref[...], g_ref[...], gn_ref[...])
        dret, dg, dgn = post_vjp(dy_ref[...])
        dgn_ref[...] += dgn

        qdv, kdv = qd_ref[...], kd_ref[...]
        dqt = qdv * _dot_nt(dret, s_in, HIGHEST)
        dkt = kdv * _dot_nt(v, ds_out, HIGHEST)
        dv_all = _dot(kt * kdv, ds_out, HIGHEST)
        dvs = []
        for h in range(RET_HEADS):
            sl = slice(h * RET_DV, (h + 1) * RET_DV)
            qh, a = _ret_scores(qt, kt, d_ref, h)
            lane = lax.broadcasted_iota(jnp.int32, kt.shape, 1)
            kh = jnp.where(lane // RET_DK == h, kt, 0.0)
            da = _dot_nt(dret[:, sl], v[:, sl], HIGHEST) * d_ref[h]
            dvs.append(_dot_tn(a, dret[:, sl], HIGHEST))
            dqt = dqt + _dot(da, kh, HIGHEST)
            dkt = dkt + _dot_tn(da, qh, HIGHEST)
        dv = dv_all + jnp.concatenate(dvs, axis=1)
        ds_ref[...] = ds_out * cd_ref[...] + _dot_tn(qt * qdv, dret, HIGHEST) * bm_ref[...]
        dq = _rope_bwd(dqt, cosv, sinv)
        dk = _rope_bwd(dkt * (RET_DK ** -0.5), cosv, sinv)
        dp_ref[...] = jnp.concatenate([dq, dk, dv, dg], axis=1)

    return pl.pallas_call(
        body, name="ret_bwd", grid=(nch,),
        in_specs=[pl.BlockSpec((C, RET_QK), rev(0)), pl.BlockSpec((C, RET_QK), rev(1)),
                  pl.BlockSpec((C, RET_WIDTH), rev(1)), pl.BlockSpec((C, RET_WIDTH), rev(2)),
                  pl.BlockSpec((C, RET_QK), rev(0)), pl.BlockSpec((C, RET_QK), rev(0)),
                  _full((C, RET_QK)), _full((C, RET_QK)), _full((RET_HEADS, C, C)),
                  _full((RET_QK, RET_WIDTH)), _full((RET_QK, RET_WIDTH)), _full((1, RET_WIDTH)),
                  _full((RET_WIDTH, RET_WIDTH)),
                  pl.BlockSpec((C, RET_WIDTH), rev(0)),
                  pl.BlockSpec((1, RET_QK, RET_WIDTH), lambda c: (nch - 1 - c, 0, 0)),
                  pl.BlockSpec((C, RET_WIDTH), rev(0))],
        out_specs=[pl.BlockSpec((C, RET_COLS), rev(0)), _full((1, RET_WIDTH))],
        out_shape=[jax.ShapeDtypeStruct((T, RET_COLS), F32), jax.ShapeDtypeStruct((1, RET_WIDTH), F32)],
        scratch_shapes=[pltpu.VMEM((RET_QK, RET_WIDTH), F32)],
        compiler_params=_cparams(dimension_semantics=("arbitrary",)),
    )(p_ret, p_ret, p_ret, p_ret, cos, sin, qd, kd, intra_d, cd, bm, gn_g, seg128, ret, s_in_all, dy)


def _prep_fn(p, prev, mu, w0, a0, k_k, k_a, lora, seg):
    W = RW_WIDTH
    ps = p + mu * (prev - p)
    r, kr, vr, g = ps[:, 0:W], ps[:, W:2 * W], ps[:, 2 * W:3 * W], ps[:, 3 * W:4 * W]
    z = ps[:, 4 * W:]
    lane = lax.broadcasted_iota(jnp.int32, z.shape, 1)
    z = jnp.where(lane < LORA, jnp.tanh(z), z)
    lo = _dot(z, lora, HIGHEST)
    w_log = -_softplus(-(w0 + lo[:, :W])) - 0.5
    decay = jnp.exp(-jnp.exp(w_log))
    a = jax.nn.sigmoid(a0 + lo[:, W:])
    kk = kr * k_k
    kk = kk / jnp.maximum(jnp.sqrt(_segsum(kk * kk, seg)), 1e-12)
    k = kr * (1.0 + (a - 1.0) * k_a)
    return kk, decay, kk * a, k, r, vr, g


def _post_fn(o, r, k, v, g, gn_g, gn_b, r_k, seg):
    mu = _segsum(o, seg) * (1.0 / RW_HEAD)
    oc = o - mu
    var = _segsum(oc * oc, seg) * (1.0 / RW_HEAD)
    on = oc * lax.rsqrt(var + RW_GN_EPS) * gn_g + gn_b
    bonus = _segsum(r * k * r_k, seg) * v
    return (g * jax.nn.sigmoid(g)) * (on + bonus)


def _shift_down(p, first_row):
    row = lax.broadcasted_iota(jnp.int32, p.shape, 0)
    return jnp.where(row == 0, first_row, pltpu.roll(p, 1, 0))


def _shift_up(p, last_row):
    n = p.shape[0]
    row = lax.broadcasted_iota(jnp.int32, p.shape, 0)
    return jnp.where(row == n - 1, last_row, pltpu.roll(p, n - 1, 0))


def _row_tile(T):
    return min(T, 256)


def _prep_fwd(p_rw, bnd, mu, w0, a0, k_k, k_a, lora, seg64):
    T = p_rw.shape[0]
    tm = _row_tile(T)
    W = RW_WIDTH

    def body(p_ref, bnd_ref, mu_ref, w0_ref, a0_ref, kk_ref, ka_ref, lora_ref, seg_ref, vecs_ref, v_ref, g_ref):
        p = p_ref[...]
        prev = _shift_down(p, bnd_ref[0])
        res = _prep_fn(p, prev, mu_ref[...], w0_ref[...], a0_ref[...], kk_ref[...], ka_ref[...], lora_ref[...],
                       seg_ref[...])
        for j in range(N_VEC):
            vecs_ref[j] = res[j]
        v_ref[...] = res[N_VEC]
        g_ref[...] = res[N_VEC + 1]

    small = _full((1, W))
    row = pl.BlockSpec((tm, W), lambda i: (i, 0))
    return pl.pallas_call(
        body, name="rwkv_prep_fwd", grid=(T // tm,),
        in_specs=[pl.BlockSpec((tm, RW_COLS), lambda i: (i, 0)), pl.BlockSpec((1, 1, RW_COLS), lambda i: (i, 0, 0)),
                  _full((1, RW_COLS)), small, small, small, small, _full((2 * LORA, 2 * W)), _full((W, W))],
        out_specs=[pl.BlockSpec((N_VEC, tm, W), lambda i: (0, i, 0)), row, row],
        out_shape=[jax.ShapeDtypeStruct((N_VEC, T, W), F32), jax.ShapeDtypeStruct((T, W), F32),
                   jax.ShapeDtypeStruct((T, W), F32)],
        compiler_params=_cparams(dimension_semantics=("arbitrary",)),
    )(p_rw, bnd, mu, w0, a0, k_k, k_a, lora, seg64)


def _prep_bwd(p_rw, bnd, mu, w0, a0, k_k, k_a, lora, seg64, cts):
    T = p_rw.shape[0]
    tm = _row_tile(T)
    W = RW_WIDTH

    def body(p_ref, bnd_ref, mu_ref, w0_ref, a0_ref, kk_ref, ka_ref, lora_ref, seg_ref,
             dvecs_ref, dk2_ref, dr2_ref, dv1_ref, dv2_ref, dg_ref,
             dp_ref, dprev_ref, dmu_ref, dw0_ref, da0_ref, dkk_p_ref, dka_ref, dlora_ref):
        accs = (dmu_ref, dw0_ref, da0_ref, dkk_p_ref, dka_ref, dlora_ref)

        @pl.when(pl.program_id(0) == 0)
        def _():
            for a_ref in accs:
                a_ref[...] = jnp.zeros_like(a_ref)

        p = p_ref[...]
        prev = _shift_down(p, bnd_ref[0])
        seg = seg_ref[...]
        _, vjp = jax.vjp(lambda *a: _prep_fn(*a, seg), p, prev, mu_ref[...], w0_ref[...], a0_ref[...], kk_ref[...],
                         ka_ref[...], lora_ref[...])
        ct = (dvecs_ref[0], dvecs_ref[1], dvecs_ref[2], dvecs_ref[3] + dk2_ref[...], dvecs_ref[4] + dr2_ref[...],
              dv1_ref[...] + dv2_ref[...], dg_ref[...])
        grads = vjp(ct)
        dp_ref[...] = grads[0]
        dprev_ref[...] = grads[1]
        for a_ref, gval in zip(accs, grads[2:]):
            a_ref[...] += gval

    small = _full((1, W))
    row = pl.BlockSpec((tm, W), lambda i: (i, 0))
    return pl.pallas_call(
        body, name="rwkv_prep_bwd", grid=(T // tm,),
        in_specs=[pl.BlockSpec((tm, RW_COLS), lambda i: (i, 0)), pl.BlockSpec((1, 1, RW_COLS), lambda i: (i, 0, 0)),
                  _full((1, RW_COLS)), small, small, small, small, _full((2 * LORA, 2 * W)), _full((W, W)),
                  pl.BlockSpec((N_VEC, tm, W), lambda i: (0, i, 0))] + [row] * 5,
        out_specs=[pl.BlockSpec((tm, RW_COLS), lambda i: (i, 0)), pl.BlockSpec((tm, RW_COLS), lambda i: (i, 0)),
                   _full((1, RW_COLS)), small, small, small, small, _full((2 * LORA, 2 * W))],
        out_shape=[jax.ShapeDtypeStruct((T, RW_COLS), F32), jax.ShapeDtypeStruct((T, RW_COLS), F32),
                   jax.ShapeDtypeStruct((1, RW_COLS), F32)] + [jax.ShapeDtypeStruct((1, W), F32)] * 4
                  + [jax.ShapeDtypeStruct((2 * LORA, 2 * W), F32)],
        compiler_params=_cparams(dimension_semantics=("arbitrary",)),
    )(p_rw, bnd, mu, w0, a0, k_k, k_a, lora, seg64, *cts)


def _vec_spec(j, tm):
    return pl.BlockSpec((None, tm, RW_WIDTH), lambda i: (j, i, 0))


def _post_fwd(o, vecs, v, g, gn_g, gn_b, r_k, seg64):
    T = o.shape[0]
    tm = _row_tile(T)
    W = RW_WIDTH

    def body(o_ref, r_ref, k_ref, v_ref, g_ref, gg_ref, gb_ref, rk_ref, seg_ref, y_ref):
        y_ref[...] = _post_fn(o_ref[...], r_ref[...], k_ref[...], v_ref[...], g_ref[...], gg_ref[...], gb_ref[...],
                              rk_ref[...], seg_ref[...]).astype(BF16)

    row = pl.BlockSpec((tm, W), lambda i: (i, 0))
    small = _full((1, W))
    return pl.pallas_call(
        body, name="rwkv_post_fwd", grid=(T // tm,),
        in_specs=[row, _vec_spec(4, tm), _vec_spec(3, tm), row, row] + [small] * 3 + [_full((W, W))],
        out_specs=row, out_shape=jax.ShapeDtypeStruct((T, W), BF16),
        compiler_params=_cparams(dimension_semantics=("arbitrary",)),
    )(o, vecs, vecs, v, g, gn_g, gn_b, r_k, seg64)


def _post_bwd(o, vecs, v, g, gn_g, gn_b, r_k, seg64, dy):
    T = o.shape[0]
    tm = _row_tile(T)
    W = RW_WIDTH

    def body(o_ref, r_ref, k_ref, v_ref, g_ref, gg_ref, gb_ref, rk_ref, seg_ref, dy_ref,
             do_ref, dr_ref, dk_ref, dv_ref, dg_ref, dgg_ref, dgb_ref, drk_ref):
        accs = (dgg_ref, dgb_ref, drk_ref)

        @pl.when(pl.program_id(0) == 0)
        def _():
            for a_ref in accs:
                a_ref[...] = jnp.zeros_like(a_ref)

        seg = seg_ref[...]
        _, vjp = jax.vjp(lambda *a: _post_fn(*a, seg), o_ref[...], r_ref[...], k_ref[...], v_ref[...], g_ref[...],
                         gg_ref[...], gb_ref[...], rk_ref[...])
        grads = vjp(dy_ref[...])
        for o_, gval in zip((do_ref, dr_ref, dk_ref, dv_ref, dg_ref), grads[:5]):
            o_[...] = gval
        for a_ref, gval in zip(accs, grads[5:]):
            a_ref[...] += gval

    row = pl.BlockSpec((tm, W), lambda i: (i, 0))
    small = _full((1, W))
    return pl.pallas_call(
        body, name="rwkv_post_bwd", grid=(T // tm,),
        in_specs=[row, _vec_spec(4, tm), _vec_spec(3, tm), row, row] + [small] * 3
                 + [_full((W, W)), pl.BlockSpec((tm, W), lambda i: (i, 1))],
        out_specs=[row] * 5 + [small] * 3,
        out_shape=[jax.ShapeDtypeStruct((T, W), F32)] * 5 + [jax.ShapeDtypeStruct((1, W), F32)] * 3,
        compiler_params=_cparams(dimension_semantics=("arbitrary",)),
    )(o, vecs, vecs, v, g, gn_g, gn_b, r_k, seg64, dy)


def _to_cols(vecs, T):
    Tc = WKV_CHUNK
    x = vecs.reshape(N_VEC, T // Tc, Tc, RW_HEADS, RW_HEAD)
    x = jnp.transpose(x, (1, 0, 4, 3, 2))
    return x.reshape(T // Tc, N_VEC * RW_HEAD, RW_HEADS * Tc)


def _from_cols(cols, T):
    Tc = WKV_CHUNK
    x = cols.reshape(T // Tc, N_VEC, RW_HEAD, RW_HEADS, Tc)
    return jnp.transpose(x, (1, 0, 4, 3, 2)).reshape(N_VEC, T, RW_WIDTH)


def _wkv_lhs(cols_ref):
    hi, lo = _split(cols_ref[0])
    return jnp.concatenate([hi, lo], axis=1)


def _wkv_step(st, kk_e, w_e, b_e, k_e, v_row):
    sa = -jnp.sum(st * kk_e, axis=0, keepdims=True)
    return st * w_e + b_e * sa + k_e * v_row, sa


def _wkv_fwd(cols, v, e_tab):
    T = v.shape[0]
    Tc = WKV_CHUNK
    nch = T // Tc
    J, W = RW_HEAD, RW_WIDTH

    def body(cols_ref, v_ref, e_ref, o_ref, ckpt_ref, s_ref):
        @pl.when(pl.program_id(0) == 0)
        def _():
            s_ref[...] = jnp.zeros_like(s_ref)

        lhs = _wkv_lhs(cols_ref)
        st = s_ref[...]
        ckpt_ref[0] = st
        for t in range(Tc):
            ex = _dot(lhs, e_ref[t])
            st, _ = _wkv_step(st, ex[0:J], ex[J:2 * J], ex[2 * J:3 * J], ex[3 * J:4 * J], v_ref[t:t + 1, :])
            o_ref[t:t + 1, :] = jnp.sum(st * ex[4 * J:5 * J], axis=0, keepdims=True)
        s_ref[...] = st

    return pl.pallas_call(
        body, name="wkv_fwd", grid=(nch,),
        in_specs=[pl.BlockSpec((1, N_VEC * J, 128), lambda c: (c, 0, 0)), pl.BlockSpec((Tc, W), lambda c: (c, 0)),
                  _full((Tc, 2 * 128, W))],
        out_specs=[pl.BlockSpec((Tc, W), lambda c: (c, 0)), pl.BlockSpec((1, J, W), lambda c: (c, 0, 0))],
        out_shape=[jax.ShapeDtypeStruct((T, W), F32), jax.ShapeDtypeStruct((nch, J, W), F32)],
        scratch_shapes=[pltpu.VMEM((J, W), F32)],
        compiler_params=_cparams(dimension_semantics=("arbitrary",)),
    )(cols, v, e_tab)


def _wkv_bwd(cols, v, do, ckpt, e_tab, r_tab):
    T = v.shape[0]
    Tc = WKV_CHUNK
    nch = T // Tc
    J, W = RW_HEAD, RW_WIDTH

    def body(cols_ref, v_ref, do_ref, ckpt_ref, e_ref, r_ref, dv_ref, dcols_ref, ds_ref, sbuf, sabuf, ebuf):
        @pl.when(pl.program_id(0) == 0)
        def _():
            ds_ref[...] = jnp.zeros_like(ds_ref)

        lhs = _wkv_lhs(cols_ref)
        st = ckpt_ref[0]
        for t in range(Tc):
            ex = _dot(lhs, e_ref[t])
            ebuf[t] = ex
            sbuf[t] = st
            st, sa = _wkv_step(st, ex[0:J], ex[J:2 * J], ex[2 * J:3 * J], ex[3 * J:4 * J], v_ref[t:t + 1, :])
            sabuf[t] = sa
        sbuf[Tc] = st

        dst = ds_ref[...]
        acc = jnp.zeros((N_VEC * J, 128), F32)
        for t in reversed(range(Tc)):
            kk_e, w_e, b_e, k_e, r_e = (ebuf[t, i * J:(i + 1) * J] for i in range(N_VEC))
            s_new, s_old = sbuf[t + 1], sbuf[t]
            do_row, v_row, sa_row = do_ref[t:t + 1, :], v_ref[t:t + 1, :], sabuf[t]
            dsn = dst + r_e * do_row
            dsa = jnp.sum(dsn * b_e, axis=0, keepdims=True)
            dv_ref[t:t + 1, :] = jnp.sum(dsn * k_e, axis=0, keepdims=True)
            prods = jnp.concatenate([-(s_old * dsa), dsn * s_old, dsn * sa_row, dsn * v_row, s_new * do_row], axis=0)
            acc = acc + _dot(prods.astype(BF16), r_ref[t])
            dst = dsn * w_e - kk_e * dsa
        ds_ref[...] = dst
        dcols_ref[0] = acc

    rev2 = lambda c: (nch - 1 - c, 0)
    rev3 = lambda c: (nch - 1 - c, 0, 0)
    return pl.pallas_call(
        body, name="wkv_bwd", grid=(nch,),
        in_specs=[pl.BlockSpec((1, N_VEC * J, 128), rev3), pl.BlockSpec((Tc, W), rev2), pl.BlockSpec((Tc, W), rev2),
                  pl.BlockSpec((1, J, W), rev3), _full((Tc, 2 * 128, W)), _full((Tc, W, 128))],
        out_specs=[pl.BlockSpec((Tc, W), rev2), pl.BlockSpec((1, N_VEC * J, 128), rev3)],
        out_shape=[jax.ShapeDtypeStruct((T, W), F32), jax.ShapeDtypeStruct((nch, N_VEC * J, 128), F32)],
        scratch_shapes=[pltpu.VMEM((J, W), F32), pltpu.VMEM((Tc + 1, J, W), F32), pltpu.VMEM((Tc, 1, W), F32),
                        pltpu.VMEM((Tc, N_VEC * J, W), F32)],
        compiler_params=_cparams(dimension_semantics=("arbitrary",)),
    )(cols, v, do, ckpt, e_tab, r_tab)


def _outproj(x, y_ret, y_rw, w_out_b, target, gf):
    T = x.shape[0]
    tm = _row_tile(T)
    W = RW_WIDTH

    def body(x_ref, yr_ref, yw_ref, w_ref, t_ref, gf_ref, loss_ref, dh_ref, dy_ref, dw_ref, dgf_ref):
        @pl.when(pl.program_id(0) == 0)
        def _():
            loss_ref[...] = jnp.zeros_like(loss_ref)
            dw_ref[...] = jnp.zeros_like(dw_ref)
            dgf_ref[...] = jnp.zeros_like(dgf_ref)

        y = jnp.concatenate([yr_ref[...], yw_ref[...]], axis=1)
        w = w_ref[...]
        h = x_ref[...] + _dot(y, w)
        rstd = lax.rsqrt(jnp.mean(h * h, axis=-1, keepdims=True) + RMS_EPS)
        hn = h * rstd
        gfv = gf_ref[...]
        err = hn * gfv - t_ref[...]
        loss_ref[...] += 0.5 * jnp.sum(jnp.mean(err * err, axis=-1))
        dout = err * (1.0 / D_MODEL)
        dgf_ref[...] += jnp.sum(dout * hn, axis=0, keepdims=True)
        dhn = dout * gfv
        dh = rstd * (dhn - hn * jnp.mean(dhn * hn, axis=-1, keepdims=True))
        dh_ref[...] = dh
        dhb = dh.astype(BF16)
        dy_ref[...] = _dot_nt(dhb, w)
        dw_ref[...] += _dot_tn(y, dhb)

    return pl.pallas_call(
        body, name="outproj_loss", grid=(T // tm,),
        in_specs=[pl.BlockSpec((tm, D_MODEL), lambda i: (i, 0)), pl.BlockSpec((tm, W), lambda i: (i, 0)),
                  pl.BlockSpec((tm, W), lambda i: (i, 0)), _full((D_MODEL, D_MODEL)),
                  pl.BlockSpec((tm, D_MODEL), lambda i: (i, 0)), _full((1, D_MODEL))],
        out_specs=[_full((1, PACK_W)), pl.BlockSpec((tm, D_MODEL), lambda i: (i, 0)),
                   pl.BlockSpec((tm, D_MODEL), lambda i: (i, 0)), _full((D_MODEL, D_MODEL)), _full((1, D_MODEL))],
        out_shape=[jax.ShapeDtypeStruct((1, PACK_W), F32), jax.ShapeDtypeStruct((T, D_MODEL), F32),
                   jax.ShapeDtypeStruct((T, D_MODEL), F32), jax.ShapeDtypeStruct((D_MODEL, D_MODEL), F32),
                   jax.ShapeDtypeStruct((1, D_MODEL), F32)],
        compiler_params=_cparams(dimension_semantics=("arbitrary",)),
    )(x, y_ret, y_rw, w_out_b, target, gf)


def _inproj_bwd_x(dp_ret, dp_rw, dprev, dbnd, w_b, x, norm_g, dh):
    T = x.shape[0]
    tm = _row_tile(T)

    def body(dpr_ref, dpw_ref, dprev_ref, dbnd_ref, w_ref, x_ref, g_ref, dh_ref, gx_ref, dg_ref, dpb_ref):
        @pl.when(pl.program_id(0) == 0)
        def _():
            dg_ref[...] = jnp.zeros_like(dg_ref)

        d_rw = dpw_ref[...] + _shift_up(dprev_ref[...], dbnd_ref[0])
        dpb = jnp.concatenate([dpr_ref[...].astype(BF16), d_rw.astype(BF16)], axis=1)
        dpb_ref[...] = dpb
        du = _dot_nt(dpb, w_ref[...])
        xf = x_ref[...]
        rstd = lax.rsqrt(jnp.mean(xf * xf, axis=-1, keepdims=True) + RMS_EPS)
        xn = xf * rstd
        dg_ref[...] += jnp.sum(du * xn, axis=0, keepdims=True)
        dxn = du * g_ref[...]
        gx_ref[...] = dh_ref[...] + rstd * (dxn - xn * jnp.mean(dxn * xn, axis=-1, keepdims=True))

    return pl.pallas_call(
        body, name="inproj_bwd_x", grid=(T // tm,),
        in_specs=[pl.BlockSpec((tm, RET_COLS), lambda i: (i, 0)), pl.BlockSpec((tm, RW_COLS), lambda i: (i, 0)),
                  pl.BlockSpec((tm, RW_COLS), lambda i: (i, 0)), pl.BlockSpec((1, 1, RW_COLS), lambda i: (i, 0, 0)),
                  _full((D_MODEL, IN_COLS)), pl.BlockSpec((tm, D_MODEL), lambda i: (i, 0)), _full((1, D_MODEL)),
                  pl.BlockSpec((tm, D_MODEL), lambda i: (i, 0))],
        out_specs=[pl.BlockSpec((tm, D_MODEL), lambda i: (i, 0)), _full((1, D_MODEL)),
                   pl.BlockSpec((tm, IN_COLS), lambda i: (i, 0))],
        out_shape=[jax.ShapeDtypeStruct((T, D_MODEL), F32), jax.ShapeDtypeStruct((1, D_MODEL), F32),
                   jax.ShapeDtypeStruct((T, IN_COLS), BF16)],
        compiler_params=_cparams(dimension_semantics=("arbitrary",)),
    )(dp_ret, dp_rw, dprev, dbnd, w_b, x, norm_g, dh)


def _inproj_bwd_w(u_t, dpb):
    T = u_t.shape[1]
    tr = 256

    def body(u_ref, d_ref, o_ref):
        o_ref[...] = _dot(u_ref[...], d_ref[...])

    return pl.pallas_call(
        body, name="inproj_bwd_w", grid=(D_MODEL // tr,),
        in_specs=[pl.BlockSpec((tr, T), lambda i: (i, 0)), _full((T, IN_COLS))],
        out_specs=pl.BlockSpec((tr, IN_COLS), lambda i: (i, 0)),
        out_shape=jax.ShapeDtypeStruct((D_MODEL, IN_COLS), F32),
        compiler_params=_cparams(dimension_semantics=("arbitrary",)),
    )(u_t, dpb)


def _tile_boundaries(a, tm, first):
    T, n = a.shape
    zero = jnp.zeros((1, n), a.dtype)
    if first:
        rows = jnp.concatenate([zero, a[tm - 1:T - 1:tm]], axis=0)
    else:
        rows = jnp.concatenate([a[tm:T:tm], zero], axis=0)
    return rows.reshape(T // tm, 1, n)


def _local_step(x, target, w_in_b, w_out_b, lora, small):
    T = x.shape[0]
    tm = _row_tile(T)
    cos, sin = _rope_tables(T)
    tabs = _ret_tables()
    seg128 = _seg_matrix(RET_WIDTH, RET_DV)
    seg64 = _seg_matrix(RW_WIDTH, RW_HEAD)
    e_tab = _wkv_expand_table()
    r_tab = _wkv_reduce_table()
    prep_w = (small["rwkv_mu"], small["w0"], small["a0"], small["k_k"], small["k_a"], lora, seg64)
    post_w = (small["rwkv_gn_g"], small["rwkv_gn_b"], small["r_k"], seg64)

    p_ret, p_rw, u = _inproj_fwd(x, small["norm_g"], w_in_b)
    y_ret, ret, s_in_all = _ret_fwd(p_ret, cos, sin, tabs, small["ret_gn_g"], seg128)
    bnd = _tile_boundaries(p_rw, tm, True)
    vecs, v, g = _prep_fwd(p_rw, bnd, *prep_w)
    cols = _to_cols(vecs, T)
    o, ckpt = _wkv_fwd(cols, v, e_tab)
    y_rw = _post_fwd(o, vecs, v, g, *post_w)
    loss, dh, dy, d_w_out, d_gf = _outproj(x, y_ret, y_rw, w_out_b, target, small["final_norm_g"])

    do, dr2, dk2, dv2, dg, d_gn_g, d_gn_b, d_r_k = _post_bwd(o, vecs, v, g, *post_w, dy)
    dv1, dcols = _wkv_bwd(cols, v, do, ckpt, e_tab, r_tab)
    dp_rw, dprev, d_mu, d_w0, d_a0, d_k_k, d_k_a, d_lora = _prep_bwd(
        p_rw, bnd, *prep_w, (_from_cols(dcols, T), dk2, dr2, dv1, dv2, dg))
    dp_ret, d_ret_gn = _ret_bwd(p_ret, cos, sin, tabs, small["ret_gn_g"], seg128, ret, s_in_all, dy)
    dbnd = _tile_boundaries(dprev, tm, False)
    grad_x, d_norm_g, dpb = _inproj_bwd_x(dp_ret, dp_rw, dprev, dbnd, w_in_b, x, small["norm_g"], dh)
    d_w_in = _inproj_bwd_w(jnp.transpose(u), dpb)

    d_small = {"norm_g": d_norm_g, "ret_gn_g": d_ret_gn, "rwkv_mu": d_mu, "w0": d_w0, "a0": d_a0, "k_k": d_k_k,
               "k_a": d_k_a, "r_k": d_r_k, "rwkv_gn_g": d_gn_g, "rwkv_gn_b": d_gn_b, "final_norm_g": d_gf}
    return loss, grad_x, d_w_in, d_w_out, d_lora, d_small


ANY = pl.BlockSpec(memory_space=pl.ANY)
CHIP_FLIPS = ((0, 1), (1, 0), (1, 1))
N_FLIPS = len(CHIP_FLIPS)
LORA_SHARD = RW_WIDTH // N_CHIPS
HALF_IN = D_MODEL // 2
HALF_OUT = OUT_SHARD // 2


def _position():
    return lax.axis_index("x"), lax.axis_index("y"), lax.axis_index("c")


def _flip(v, f):
    return 1 - v if f else v


def _finish(local, remote, landed):
    for cp in landed:
        cp.wait_recv()
    for cp in remote:
        cp.wait_send()
    for cp in local:
        cp.wait()


def _gather_chips(arrs):
    n = len(arrs)
    halves = [a.shape[0] // 2 for a in arrs]

    def body(*refs):
        ins, outs = refs[:n], refs[n:2 * n]
        send, recv, pass_send, pass_recv, loc = refs[2 * n:]
        x, y, c = _position()
        s = 2 * x + y
        sibling = (x, y, 1 - c)

        def copy(src, dst, sems, k, to):
            return pltpu.make_async_remote_copy(src_ref=src, dst_ref=dst, send_sem=sems[0].at[k], recv_sem=sems[1].at[k],
                                                device_id=to, device_id_type=MESH)

        local, remote, landed, passed, passed_in = [], [], [], [], []
        for a in range(n):
            mine = pl.ds(c * halves[a], halves[a])
            theirs = pl.ds((1 - c) * halves[a], halves[a])
            local.append(pltpu.make_async_copy(ins[a], outs[a].at[s], loc.at[a]))
            for j, (fx, fy) in enumerate(CHIP_FLIPS):
                px, py = _flip(x, fx), _flip(y, fy)
                ps = 2 * px + py
                k = a * N_FLIPS + j
                remote.append(copy(ins[a].at[mine], outs[a].at[s, mine], (send, recv), k, (px, py, c)))
                landed.append(copy(ins[a].at[mine], outs[a].at[ps, mine], (send, recv), k, (px, py, c)))
                passed.append(copy(outs[a].at[ps, mine], outs[a].at[ps, mine], (pass_send, pass_recv), k, sibling))
                passed_in.append(copy(outs[a].at[ps, theirs], outs[a].at[ps, theirs], (pass_send, pass_recv), k, sibling))
        for cp in local + remote:
            cp.start()
        for arrived, onward in zip(landed, passed):
            arrived.wait_recv()
            onward.start()
        _finish(local, remote + passed, passed_in)

    sems = pltpu.SemaphoreType.DMA((n * N_FLIPS,))
    return pl.pallas_call(
        body, name="gather_weights",
        in_specs=[ANY] * n, out_specs=[ANY] * n,
        out_shape=[jax.ShapeDtypeStruct((N_CHIPS,) + a.shape, a.dtype) for a in arrs],
        scratch_shapes=[sems, sems, sems, sems, pltpu.SemaphoreType.DMA((n,))],
    )(*arrs)


def _pair_exchange(g_in, g_out, g_small):
    def body(gi_ref, go_ref, gs_ref, li_ref, lo_ref, ls_ref, send, recv):
        x, y, c = _position()
        peer = (x, y, 1 - c)
        srcs = (gi_ref.at[:, pl.ds((1 - c) * HALF_IN, HALF_IN), :], go_ref.at[:, pl.ds((1 - c) * HALF_OUT, HALF_OUT), :],
                gs_ref)
        remote = [pltpu.make_async_remote_copy(src_ref=src, dst_ref=dst, send_sem=send.at[k], recv_sem=recv.at[k],
                                               device_id=peer, device_id_type=MESH)
                  for k, (src, dst) in enumerate(zip(srcs, (li_ref, lo_ref, ls_ref)))]
        for cp in remote:
            cp.start()
        _finish([], remote, remote)

    return pl.pallas_call(
        body, name="pair_exchange",
        in_specs=[ANY] * 3, out_specs=[ANY] * 3,
        out_shape=[jax.ShapeDtypeStruct((N_CHIPS, HALF_IN, IN_SHARD), F32),
                   jax.ShapeDtypeStruct((N_CHIPS, HALF_OUT, D_MODEL), F32),
                   jax.ShapeDtypeStruct(g_small.shape, F32)],
        scratch_shapes=[pltpu.SemaphoreType.DMA((3,)), pltpu.SemaphoreType.DMA((3,))],
    )(g_in, g_out, g_small)


def _pair_sum(g_in, g_out, g_small, l_in, l_out, l_small, c_arr):
    tr = HALF_IN // 2

    def body(c_ref, gi_ref, go_ref, gs_ref, li_ref, lo_ref, ls_ref, ci_ref, co_ref, cs_ref):
        ci_ref[...] = (gi_ref[...] + li_ref[...]).astype(BF16)

        @pl.when(pl.program_id(1) == 0)
        def _():
            co_ref[...] = (go_ref[...] + lo_ref[...]).astype(BF16)

        @pl.when((pl.program_id(0) == 0) & (pl.program_id(1) == 0))
        def _():
            cs_ref[...] = gs_ref[...] + ls_ref[...]

    nd = g_small.shape
    return pl.pallas_call(
        body, name="pair_sum",
        grid_spec=pltpu.PrefetchScalarGridSpec(
            num_scalar_prefetch=1, grid=(N_CHIPS, 2),
            in_specs=[pl.BlockSpec((1, tr, IN_SHARD), lambda s, i, c: (s, 2 * c[0] + i, 0)),
                      pl.BlockSpec((1, HALF_OUT, D_MODEL), lambda s, i, c: (s, c[0], 0)),
                      pl.BlockSpec(nd, lambda s, i, c: (0, 0)),
                      pl.BlockSpec((1, tr, IN_SHARD), lambda s, i, c: (s, i, 0)),
                      pl.BlockSpec((1, HALF_OUT, D_MODEL), lambda s, i, c: (s, 0, 0)),
                      pl.BlockSpec(nd, lambda s, i, c: (0, 0))],
            out_specs=[pl.BlockSpec((1, tr, IN_SHARD), lambda s, i, c: (s, i, 0)),
                       pl.BlockSpec((1, HALF_OUT, D_MODEL), lambda s, i, c: (s, 0, 0)),
                       pl.BlockSpec(nd, lambda s, i, c: (0, 0))]),
        out_shape=[jax.ShapeDtypeStruct((N_CHIPS, HALF_IN, IN_SHARD), BF16),
                   jax.ShapeDtypeStruct((N_CHIPS, HALF_OUT, D_MODEL), BF16), jax.ShapeDtypeStruct(nd, F32)],
        compiler_params=_cparams(dimension_semantics=("arbitrary", "arbitrary")),
    )(c_arr, g_in, g_out, g_small, l_in, l_out, l_small)


def _chip_exchange(c_in, c_out, c_small):
    def body(ci_ref, co_ref, cs_ref, li_ref, lo_ref, ls_ref, send, recv):
        x, y, c = _position()
        s = 2 * x + y
        remote = []
        for j, (fx, fy) in enumerate(CHIP_FLIPS):
            px, py = _flip(x, fx), _flip(y, fy)
            ps = 2 * px + py
            for a, (src, dst) in enumerate(((ci_ref.at[ps], li_ref.at[j]), (co_ref.at[ps], lo_ref.at[j]),
                                            (cs_ref, ls_ref.at[j]))):
                k = 3 * j + a
                remote.append(pltpu.make_async_remote_copy(src_ref=src, dst_ref=dst, send_sem=send.at[k],
                                                           recv_sem=recv.at[k], device_id=(px, py, c),
                                                           device_id_type=MESH))
        for cp in remote:
            cp.start()
        _finish([], remote, remote)

    return pl.pallas_call(
        body, name="chip_exchange",
        in_specs=[ANY] * 3, out_specs=[ANY] * 3,
        out_shape=[jax.ShapeDtypeStruct((N_FLIPS, HALF_IN, IN_SHARD), c_in.dtype),
                   jax.ShapeDtypeStruct((N_FLIPS, HALF_OUT, D_MODEL), c_out.dtype),
                   jax.ShapeDtypeStruct((N_FLIPS,) + c_small.shape, F32)],
        scratch_shapes=[pltpu.SemaphoreType.DMA((3 * N_FLIPS,)), pltpu.SemaphoreType.DMA((3 * N_FLIPS,))],
    )(c_in, c_out, c_small)


def _chip_sum(g_in, g_out, p_in, p_out, c_small, l_in, l_out, l_small, sc_arr):
    tr = HALF_IN // 2
    nd = c_small.shape

    def body(s_ref, gi_ref, go_ref, pi_ref, po_ref, cs_ref, li0, li1, li2, lo0, lo1, lo2, ls_ref,
             ri_ref, ro_ref, rs_ref):
        ri_ref[...] = (((gi_ref[0] + pi_ref[0]) + li0[0].astype(F32)) + li1[0].astype(F32)) + li2[0].astype(F32)

        @pl.when(pl.program_id(0) == 0)
        def _():
            ro_ref[...] = (((go_ref[0] + po_ref[0]) + lo0[0].astype(F32)) + lo1[0].astype(F32)) + lo2[0].astype(F32)
            me = s_ref[0]
            parts = (cs_ref[...], ls_ref[0], ls_ref[1], ls_ref[2])

            def of_chip(s):
                m = jnp.bitwise_xor(me, s)
                return jnp.where(m == 0, parts[0], jnp.where(m == 1, parts[1], jnp.where(m == 2, parts[2], parts[3])))

            rs_ref[...] = ((of_chip(0) + of_chip(1)) + of_chip(2)) + of_chip(3)

    def flip_in(j):
        return pl.BlockSpec((1, tr, IN_SHARD), lambda i, s: (j, i, 0))

    def flip_out(j):
        return pl.BlockSpec((1, HALF_OUT, D_MODEL), lambda i, s: (j, 0, 0))

    return pl.pallas_call(
        body, name="chip_sum",
        grid_spec=pltpu.PrefetchScalarGridSpec(
            num_scalar_prefetch=1, grid=(2,),
            in_specs=[pl.BlockSpec((1, tr, IN_SHARD), lambda i, s: (s[0], 2 * s[1] + i, 0)),
                      pl.BlockSpec((1, HALF_OUT, D_MODEL), lambda i, s: (s[0], s[1], 0)),
                      pl.BlockSpec((1, tr, IN_SHARD), lambda i, s: (s[0], i, 0)),
                      pl.BlockSpec((1, HALF_OUT, D_MODEL), lambda i, s: (s[0], 0, 0)),
                      pl.BlockSpec(nd, lambda i, s: (0, 0)),
                      flip_in(0), flip_in(1), flip_in(2), flip_out(0), flip_out(1), flip_out(2),
                      pl.BlockSpec((N_FLIPS,) + nd, lambda i, s: (0, 0, 0))],
            out_specs=[pl.BlockSpec((tr, IN_SHARD), lambda i, s: (i, 0)),
                       pl.BlockSpec((HALF_OUT, D_MODEL), lambda i, s: (0, 0)),
                       pl.BlockSpec(nd, lambda i, s: (0, 0))]),
        out_shape=[jax.ShapeDtypeStruct((HALF_IN, IN_SHARD), F32), jax.ShapeDtypeStruct((HALF_OUT, D_MODEL), F32),
                   jax.ShapeDtypeStruct(nd, F32)],
        compiler_params=_cparams(dimension_semantics=("arbitrary",)),
    )(sc_arr, g_in, g_out, p_in, p_out, c_small, l_in, l_in, l_in, l_out, l_out, l_out, l_small)


def _pair_share(r_in, r_out):
    def body(ri_ref, ro_ref, fi_ref, fo_ref, send, recv, loc):
        x, y, c = _position()
        peer = (x, y, 1 - c)
        local, remote, landed = [], [], []
        for k, (src, dst, rows) in enumerate(((ri_ref, fi_ref, HALF_IN), (ro_ref, fo_ref, HALF_OUT))):
            local.append(pltpu.make_async_copy(src, dst.at[pl.ds(c * rows, rows), :], loc.at[k]))
            remote.append(pltpu.make_async_remote_copy(
                src_ref=src, dst_ref=dst.at[pl.ds(c * rows, rows), :], send_sem=send.at[k], recv_sem=recv.at[k],
                device_id=peer, device_id_type=MESH))
            landed.append(pltpu.make_async_remote_copy(
                src_ref=src, dst_ref=dst.at[pl.ds((1 - c) * rows, rows), :], send_sem=send.at[k], recv_sem=recv.at[k],
                device_id=peer, device_id_type=MESH))
        for cp in local + remote:
            cp.start()
        _finish(local, remote, landed)

    return pl.pallas_call(
        body, name="pair_share",
        in_specs=[ANY] * 2, out_specs=[ANY] * 2,
        out_shape=[jax.ShapeDtypeStruct((D_MODEL, IN_SHARD), F32), jax.ShapeDtypeStruct((OUT_SHARD, D_MODEL), F32)],
        scratch_shapes=[pltpu.SemaphoreType.DMA((2,)), pltpu.SemaphoreType.DMA((2,)), pltpu.SemaphoreType.DMA((2,))],
    )(r_in, r_out)


def _adamw(name, w, g, m, v, tr):
    rows, cols = w.shape

    def body(w_ref, g_ref, m_ref, v_ref, d_ref, nm_ref, nv_ref):
        gv = g_ref[...]
        mn = ADAM_B1 * m_ref[...] + (1.0 - ADAM_B1) * gv
        vn = ADAM_B2 * v_ref[...] + (1.0 - ADAM_B2) * jnp.square(gv)
        m_hat = mn / (1.0 - ADAM_B1 ** ADAM_STEP)
        v_hat = vn / (1.0 - ADAM_B2 ** ADAM_STEP)
        d_ref[...] = -ADAM_LR * (m_hat / (jnp.sqrt(v_hat) + ADAM_EPS) + ADAM_WD * w_ref[...])
        nm_ref[...] = mn
        nv_ref[...] = vn

    spec = pl.BlockSpec((tr, cols), lambda i: (i, 0))
    return pl.pallas_call(
        body, name=name, grid=(rows // tr,), in_specs=[spec] * 4, out_specs=[spec] * 3,
        out_shape=[jax.ShapeDtypeStruct((rows, cols), F32)] * 3,
        compiler_params=_cparams(dimension_semantics=("arbitrary",)),
    )(w, g, m, v)


def _row_pieces(n):
    return [(k, k * PACK_W, min(PACK_W, n - k * PACK_W)) for k in range(-(-n // PACK_W))]


def _pack_small(d_small, loss, d_lora):
    ns = len(SMALL_NAMES)

    def body(*refs):
        small_refs, (loss_ref, lora_ref, out_ref) = refs[:ns], refs[ns:]
        out_ref[...] = jnp.zeros_like(out_ref)
        out_ref[PACK_LORA_W:PACK_LORA_W + LORA, :] = lora_ref[:LORA, :RW_WIDTH]
        out_ref[PACK_LORA_A:PACK_LORA_A + LORA, :] = lora_ref[LORA:, RW_WIDTH:]
        for name, n, ref in zip(SMALL_NAMES, SMALL_SIZES, small_refs):
            for k, at, w in _row_pieces(n):
                out_ref[PACK_AT[name] + k:PACK_AT[name] + k + 1, 0:w] = ref[:, at:at + w]
        out_ref[PACK_LOSS:PACK_LOSS + 1, :] = loss_ref[...]

    return pl.pallas_call(body, name="pack_small", out_shape=jax.ShapeDtypeStruct((PACK_ROWS, PACK_W), F32),
                          compiler_params=_cparams())(*d_small, loss, d_lora)


def _adam_update(w, g, m, v):
    mn = ADAM_B1 * m + (1.0 - ADAM_B1) * g
    vn = ADAM_B2 * v + (1.0 - ADAM_B2) * jnp.square(g)
    m_hat = mn / (1.0 - ADAM_B1 ** ADAM_STEP)
    v_hat = vn / (1.0 - ADAM_B2 ** ADAM_STEP)
    return -ADAM_LR * (m_hat / (jnp.sqrt(v_hat) + ADAM_EPS) + ADAM_WD * w), mn, vn


def _adamw_small(tot, chip_arr, ws, ms, vs):
    ns = len(SMALL_NAMES)
    n_par = ns + 2

    def body(s_ref, tot_ref, glw_ref, gla_ref, *refs):
        w_refs, m_refs, v_refs = refs[:n_par], refs[n_par:2 * n_par], refs[2 * n_par:3 * n_par]
        outs = refs[3 * n_par:]
        g_refs, d_refs, nm_refs, nv_refs = (outs[i * n_par:(i + 1) * n_par] for i in range(4))
        grads = [jnp.concatenate([tot_ref[PACK_AT[name] + k:PACK_AT[name] + k + 1, 0:w] for k, _, w in _row_pieces(n)],
                                 axis=1) for name, n in zip(SMALL_NAMES, SMALL_SIZES)]
        grads += [glw_ref[...], gla_ref[...]]
        for i, g in enumerate(grads):
            d, mn, vn = _adam_update(w_refs[i][...], g, m_refs[i][...], v_refs[i][...])
            g_refs[i][...] = g
            d_refs[i][...] = d
            nm_refs[i][...] = mn
            nv_refs[i][...] = vn

    def whole(a):
        nd = a.ndim
        return pl.BlockSpec(a.shape, lambda i, s: (0,) * nd)

    shard = (LORA, LORA_SHARD)
    par_specs = [whole(a) for a in ws]
    res = pl.pallas_call(
        body, name="adamw_small",
        grid_spec=pltpu.PrefetchScalarGridSpec(
            num_scalar_prefetch=1, grid=(1,),
            in_specs=[whole(tot), pl.BlockSpec(shard, lambda i, s: (PACK_LORA_W // LORA, s[0])),
                      pl.BlockSpec(shard, lambda i, s: (PACK_LORA_A // LORA, s[0]))] + par_specs * 3,
            out_specs=par_specs * 4),
        out_shape=[jax.ShapeDtypeStruct(a.shape, F32) for a in ws] * 4,
        compiler_params=_cparams(dimension_semantics=("arbitrary",)),
    )(chip_arr, tot, tot, tot, *ws, *ms, *vs)
    return [res[i * n_par:(i + 1) * n_par] for i in range(4)]


def kernel(x, norm_g, w_in, ret_gn_g, rwkv_mu, w_lora_up, w0, a_lora_up, a0, k_k, k_a, r_k, rwkv_gn_g, rwkv_gn_b, w_out, final_norm_g, loss_target, m_norm_g, m_w_in, m_ret_gn_g, m_rwkv_mu, m_w_lora_up, m_w0, m_a_lora_up, m_a0, m_k_k, m_k_a, m_r_k, m_rwkv_gn_g, m_rwkv_gn_b, m_w_out, m_final_norm_g, v_norm_g, v_w_in, v_ret_gn_g, v_rwkv_mu, v_w_lora_up, v_w0, v_a_lora_up, v_a0, v_k_k, v_k_a, v_r_k, v_rwkv_gn_g, v_rwkv_gn_b, v_w_out, v_final_norm_g):
    W = RW_WIDTH
    params = dict(norm_g=norm_g, ret_gn_g=ret_gn_g, rwkv_mu=rwkv_mu, w0=w0, a0=a0, k_k=k_k, k_a=k_a, r_k=r_k,
                  rwkv_gn_g=rwkv_gn_g, rwkv_gn_b=rwkv_gn_b, final_norm_g=final_norm_g)
    moments_m = dict(norm_g=m_norm_g, ret_gn_g=m_ret_gn_g, rwkv_mu=m_rwkv_mu, w0=m_w0, a0=m_a0, k_k=m_k_k, k_a=m_k_a,
                     r_k=m_r_k, rwkv_gn_g=m_rwkv_gn_g, rwkv_gn_b=m_rwkv_gn_b, final_norm_g=m_final_norm_g)
    moments_v = dict(norm_g=v_norm_g, ret_gn_g=v_ret_gn_g, rwkv_mu=v_rwkv_mu, w0=v_w0, a0=v_a0, k_k=v_k_k, k_a=v_k_a,
                     r_k=v_r_k, rwkv_gn_g=v_rwkv_gn_g, rwkv_gn_b=v_rwkv_gn_b, final_norm_g=v_final_norm_g)
    xi, yi, ci = _position()
    chip = (2 * xi + yi).astype(jnp.int32)

    g_in, g_out, g_lw, g_la = _gather_chips([w_in[0].astype(BF16), w_out[0].astype(BF16), w_lora_up[0], a_lora_up[0]])
    w_in_b = jnp.transpose(g_in, (1, 0, 2)).reshape(D_MODEL, IN_COLS)
    w_out_b = g_out.reshape(D_MODEL, D_MODEL)
    lw = jnp.transpose(g_lw, (1, 0, 2)).reshape(LORA, W)
    la = jnp.transpose(g_la, (1, 0, 2)).reshape(LORA, W)
    zero = jnp.zeros((LORA, W), F32)
    lora = jnp.concatenate([jnp.concatenate([lw, zero], axis=1), jnp.concatenate([zero, la], axis=1)], axis=0)
    small = {n: params[n].reshape(1, -1) for n in SMALL_NAMES}

    loss, grad_x, d_w_in, d_w_out, d_lora, d_small = _local_step(x[0], loss_target[0], w_in_b, w_out_b, lora, small)

    core = ci.astype(jnp.int32)
    gi = jnp.transpose(d_w_in.reshape(D_MODEL, N_CHIPS, IN_SHARD), (1, 0, 2))
    go = d_w_out.reshape(N_CHIPS, OUT_SHARD, D_MODEL)
    gs = _pack_small([d_small[n] for n in SMALL_NAMES], loss, d_lora)
    p_in, p_out, p_small = _pair_exchange(gi, go, gs)
    c_in, c_out, c_small = _pair_sum(gi, go, gs, p_in, p_out, p_small, core.reshape(1))
    l_in, l_out, l_small = _chip_exchange(c_in, c_out, c_small)
    r_in, r_out, tot = _chip_sum(gi, go, p_in, p_out, c_small, l_in, l_out, l_small, jnp.stack([chip, core]))
    grad_w_in, grad_w_out = _pair_share(r_in, r_out)

    d_in, nm_in, nv_in = _adamw("adamw_w_in", w_in[0], grad_w_in, m_w_in[0], v_w_in[0], 256)
    d_out, nm_out, nv_out = _adamw("adamw_w_out", w_out[0], grad_w_out, m_w_out[0], v_w_out[0], OUT_SHARD)
    par_names = SMALL_NAMES + ("w_lora_up", "a_lora_up")

    def operands(tree, lw_, la_):
        return [tree[n].reshape(1, -1) for n in SMALL_NAMES] + [lw_[0], la_[0]]

    res = _adamw_small(tot, chip.reshape(1), operands(params, w_lora_up, a_lora_up),
                       operands(moments_m, m_w_lora_up, m_a_lora_up), operands(moments_v, v_w_lora_up, v_a_lora_up))

    names = ("norm_g", "w_in", "ret_gn_g", "rwkv_mu", "w_lora_up", "w0", "a_lora_up", "a0", "k_k", "k_a", "r_k",
             "rwkv_gn_g", "rwkv_gn_b", "w_out", "final_norm_g")
    shapes = dict(w_in=w_in.shape, w_out=w_out.shape, w_lora_up=w_lora_up.shape, a_lora_up=a_lora_up.shape,
                  **{n: params[n].shape for n in SMALL_NAMES})

    def leaves(pars, big_in, big_out):
        tree = dict(zip(par_names, pars), w_in=big_in, w_out=big_out)
        return [tree[n].reshape(shapes[n]) for n in names]

    grads = leaves(res[0], grad_w_in, grad_w_out)
    deltas = leaves(res[1], d_in, d_out)
    new_m = leaves(res[2], nm_in, nm_out)
    new_v = leaves(res[3], nv_in, nv_out)
    return (tot[PACK_LOSS, 0], grad_x.reshape(x.shape), *grads, *deltas, *new_m, *new_v)
```

```python
import functools

import numpy as np
import jax
import jax.numpy as jnp
from jax import lax
from jax.experimental import pallas as pl
from jax.experimental.pallas import tpu as pltpu

F32 = jnp.float32
BF16 = jnp.bfloat16
HIGHEST = lax.Precision.HIGHEST
MESH = pl.DeviceIdType.MESH

D_MODEL = 1024
N_CHIPS = 4
RET_HEADS = 4
RET_DK = 64
RET_DV = 128
RET_QK = RET_HEADS * RET_DK
RET_WIDTH = RET_HEADS * RET_DV
RET_COLS = 2 * RET_QK + 2 * RET_WIDTH
RET_CHUNK = 64
RW_WIDTH = 512
RW_HEAD = 64
RW_HEADS = 8
LORA = 64
RW_COLS = 4 * RW_WIDTH + 2 * LORA
IN_COLS = RET_COLS + RW_COLS
IN_SHARD = IN_COLS // N_CHIPS
OUT_SHARD = D_MODEL // N_CHIPS
ROPE_BASE = 10000.0
RMS_EPS = 1e-6
RET_GN_EPS = 1e-5
RW_GN_EPS = 64e-5
WKV_CHUNK = 16
N_VEC = 5

ADAM_LR = 0.001
ADAM_B1 = 0.9
ADAM_B2 = 0.999
ADAM_EPS = 1e-08
ADAM_WD = 0.01
ADAM_STEP = 10

VMEM_LIMIT = 56 * 1024 * 1024

PACK_W = 512
SMALL_NAMES = ("norm_g", "ret_gn_g", "rwkv_mu", "w0", "a0", "k_k", "k_a", "r_k", "rwkv_gn_g", "rwkv_gn_b",
               "final_norm_g")
SMALL_SIZES = (1024, 512, 2176, 512, 512, 512, 512, 512, 512, 512, 1024)
PACK_LORA_W = 0
PACK_LORA_A = LORA
PACK_SMALL = 2 * LORA


def _pack_layout():
    rows, at = {}, PACK_SMALL
    for name, n in zip(SMALL_NAMES, SMALL_SIZES):
        rows[name] = at
        at += -(-n // PACK_W)
    return rows, at


PACK_AT, PACK_LOSS = _pack_layout()
PACK_ROWS = -(-(PACK_LOSS + 1) // 8) * 8


def _cparams(**kw):
    return pltpu.CompilerParams(vmem_limit_bytes=VMEM_LIMIT, **kw)


def _dot(a, b, precision=None):
    return jnp.dot(a, b, precision=precision, preferred_element_type=F32)


def _dot_nt(a, b, precision=None):
    return lax.dot_general(a, b, (((1,), (1,)), ((), ())), precision=precision, preferred_element_type=F32)


def _dot_tn(a, b, precision=None):
    return lax.dot_general(a, b, (((0,), (0,)), ((), ())), precision=precision, preferred_element_type=F32)


def _split(x):
    hi = x.astype(BF16)
    lo = (x - hi.astype(F32)).astype(BF16)
    return hi, lo


@jax.custom_vjp
def _segsum(x, seg):
    hi, lo = _split(x)
    return _dot(hi, seg) + _dot(lo, seg)


def _segsum_fwd(x, seg):
    return _segsum(x, seg), seg


def _segsum_bwd(seg, ct):
    return _segsum(ct, seg), jnp.zeros_like(seg)


_segsum.defvjp(_segsum_fwd, _segsum_bwd)


def _softplus(z):
    return jnp.maximum(z, 0.0) + jnp.log(1.0 + jnp.exp(-jnp.abs(z)))


def _full(shape):
    nd = len(shape)
    return pl.BlockSpec(shape, lambda *_: (0,) * nd)


def _rope_tables(T):
    half = RET_DK // 2
    expo = -jnp.arange(half, dtype=F32) / jnp.float32(half)
    freqs = jnp.exp(expo * jnp.float32(np.log(ROPE_BASE)))
    ang = jnp.arange(T, dtype=jnp.int32).astype(F32)[:, None] * freqs[None, :]
    cos = jnp.tile(jnp.cos(ang), (1, 2 * RET_HEADS))
    sin = jnp.tile(jnp.sin(ang), (1, 2 * RET_HEADS))
    return cos, sin


def _ret_tables():
    H, C = RET_HEADS, RET_CHUNK
    hidx = jnp.arange(H, dtype=F32)
    lg = jnp.log(1.0 - jnp.exp2(-5.0 - hidx))
    idx = jnp.arange(C, dtype=F32)
    intra = jnp.exp(lg[:, None, None] * jnp.abs(idx[:, None] - idx[None, :]))
    q_dec = jnp.transpose(jnp.exp(lg[:, None] * (idx[None, :] + 1.0)))
    k_dec = jnp.transpose(jnp.exp(lg[:, None] * (C - 1.0 - idx[None, :])))
    chunk_dec = jnp.exp(lg * C)
    qd = jnp.repeat(q_dec, RET_DK, axis=1)
    kd = jnp.repeat(k_dec, RET_DK, axis=1)
    row_h = np.arange(RET_QK) // RET_DK
    col_h = np.arange(RET_WIDTH) // RET_DV
    bm = jnp.asarray((row_h[:, None] == col_h[None, :]).astype(np.float32))
    cd = bm * jnp.repeat(chunk_dec, RET_DK)[:, None]
    return intra, qd, kd, cd, bm


def _seg_matrix(width, head):
    h = np.arange(width) // head
    return jnp.asarray((h[:, None] == h[None, :]).astype(np.float32), dtype=BF16)


def _wkv_expand_table():
    Tc = WKV_CHUNK
    k = np.arange(2 * RW_HEADS * Tc)
    kh, kt = (k % (RW_HEADS * Tc)) // Tc, k % Tc
    nh = np.arange(RW_WIDTH) // RW_HEAD
    e = (kh[None, :, None] == nh[None, None, :]) & (kt[None, :, None] == np.arange(Tc)[:, None, None])
    return jnp.asarray(e.astype(np.float32), dtype=BF16)


def _wkv_reduce_table():
    Tc = WKV_CHUNK
    kh = np.arange(RW_WIDTH) // RW_HEAD
    n = np.arange(RW_HEADS * Tc)
    nh, nt = n // Tc, n % Tc
    r = (kh[None, :, None] == nh[None, None, :]) & (nt[None, None, :] == np.arange(Tc)[:, None, None])
    return jnp.asarray(r.astype(np.float32), dtype=BF16)


def _inproj_fwd(x, norm_g, w_b):
    T = x.shape[0]
    tm = min(T, 256)

    def body(x_ref, g_ref, w_ref, pret_ref, prw_ref, u_ref):
        xf = x_ref[...]
        rstd = lax.rsqrt(jnp.mean(xf * xf, axis=-1, keepdims=True) + RMS_EPS)
        ub = ((xf * rstd) * g_ref[...]).astype(BF16)
        u_ref[...] = ub
        pret_ref[...] = _dot(ub, w_ref[:, :RET_COLS])
        prw_ref[...] = _dot(ub, w_ref[:, RET_COLS:])

    return pl.pallas_call(
        body, name="inproj_fwd", grid=(T // tm,),
        in_specs=[pl.BlockSpec((tm, D_MODEL), lambda i: (i, 0)), _full((1, D_MODEL)), _full((D_MODEL, IN_COLS))],
        out_specs=[pl.BlockSpec((tm, RET_COLS), lambda i: (i, 0)), pl.BlockSpec((tm, RW_COLS), lambda i: (i, 0)),
                   pl.BlockSpec((tm, D_MODEL), lambda i: (i, 0))],
        out_shape=[jax.ShapeDtypeStruct((T, RET_COLS), F32), jax.ShapeDtypeStruct((T, RW_COLS), F32),
                   jax.ShapeDtypeStruct((T, D_MODEL), BF16)],
        compiler_params=_cparams(dimension_semantics=("arbitrary",)),
    )(x, norm_g, w_b)


def _rot_half(x):
    n = x.shape[1]
    lane = lax.broadcasted_iota(jnp.int32, x.shape, 1)
    first = (lane % RET_DK) < (RET_DK // 2)
    return jnp.where(first, -pltpu.roll(x, n - RET_DK // 2, 1), pltpu.roll(x, RET_DK // 2, 1))


def _rope(x, cos, sin):
    return x * cos + _rot_half(x) * sin


def _rope_bwd(d, cos, sin):
    return d * cos - _rot_half(d * sin)


def _ret_post(ret, g, gn_g, seg):
    mu = _segsum(ret, seg) * (1.0 / RET_DV)
    xc = ret - mu
    var = _segsum(xc * xc, seg) * (1.0 / RET_DV)
    n = xc * lax.rsqrt(var + RET_GN_EPS)
    return (g * jax.nn.sigmoid(g)) * (n * gn_g)


def _ret_scores(qt, kt, d_ref, h):
    lane = lax.broadcasted_iota(jnp.int32, qt.shape, 1)
    qh = jnp.where(lane // RET_DK == h, qt, 0.0)
    return qh, _dot_nt(qh, kt, HIGHEST) * d_ref[h]


def _ret_fwd(p_ret, cos, sin, tabs, gn_g, seg128):
    T = p_ret.shape[0]
    C = RET_CHUNK
    nch = T // C
    intra_d, qd, kd, cd, bm = tabs

    def body(q_ref, k_ref, v_ref, g_ref, cos_ref, sin_ref, qd_ref, kd_ref, d_ref, cd_ref, bm_ref, gn_ref, seg_ref,
             y_ref, ret_ref, sin_out_ref, s_ref):
        @pl.when(pl.program_id(0) == 0)
        def _():
            s_ref[...] = jnp.zeros_like(s_ref)

        cosv, sinv = cos_ref[...], sin_ref[...]
        qt = _rope(q_ref[...], cosv, sinv)
        kt = _rope(k_ref[...], cosv, sinv) * (RET_DK ** -0.5)
        v = v_ref[...]
        s_in = s_ref[...]
        sin_out_ref[0] = s_in
        inter = _dot(qt * qd_ref[...], s_in, HIGHEST)
        intra = []
        for h in range(RET_HEADS):
            _, a = _ret_scores(qt, kt, d_ref, h)
            intra.append(_dot(a, v[:, h * RET_DV:(h + 1) * RET_DV], HIGHEST))
        ret = jnp.concatenate(intra, axis=1) + inter
        kv = _dot_tn(kt * kd_ref[...], v, HIGHEST)
        s_ref[...] = s_in * cd_ref[...] + kv * bm_ref[...]
        ret_ref[...] = ret
        y_ref[...] = _ret_post(ret, g_ref[...], gn_ref[...], seg_ref[...]).astype(BF16)

    return pl.pallas_call(
        body, name="ret_fwd", grid=(nch,),
        in_specs=[pl.BlockSpec((C, RET_QK), lambda c: (c, 0)), pl.BlockSpec((C, RET_QK), lambda c: (c, 1)),
                  pl.BlockSpec((C, RET_WIDTH), lambda c: (c, 1)), pl.BlockSpec((C, RET_WIDTH), lambda c: (c, 2)),
                  pl.BlockSpec((C, RET_QK), lambda c: (c, 0)), pl.BlockSpec((C, RET_QK), lambda c: (c, 0)),
                  _full((C, RET_QK)), _full((C, RET_QK)), _full((RET_HEADS, C, C)),
                  _full((RET_QK, RET_WIDTH)), _full((RET_QK, RET_WIDTH)), _full((1, RET_WIDTH)),
                  _full((RET_WIDTH, RET_WIDTH))],
        out_specs=[pl.BlockSpec((C, RET_WIDTH), lambda c: (c, 0)), pl.BlockSpec((C, RET_WIDTH), lambda c: (c, 0)),
                   pl.BlockSpec((1, RET_QK, RET_WIDTH), lambda c: (c, 0, 0))],
        out_shape=[jax.ShapeDtypeStruct((T, RET_WIDTH), BF16), jax.ShapeDtypeStruct((T, RET_WIDTH), F32),
                   jax.ShapeDtypeStruct((nch, RET_QK, RET_WIDTH), F32)],
        scratch_shapes=[pltpu.VMEM((RET_QK, RET_WIDTH), F32)],
        compiler_params=_cparams(dimension_semantics=("arbitrary",)),
    )(p_ret, p_ret, p_ret, p_ret, cos, sin, qd, kd, intra_d, cd, bm, gn_g, seg128)


def _ret_bwd(p_ret, cos, sin, tabs, gn_g, seg128, ret, s_in_all, dy):
    T = p_ret.shape[0]
    C = RET_CHUNK
    nch = T // C
    intra_d, qd, kd, cd, bm = tabs

    def rev(j):
        return lambda c: (nch - 1 - c, j)

    def body(q_ref, k_ref, v_ref, g_ref, cos_ref, sin_ref, qd_ref, kd_ref, d_ref, cd_ref, bm_ref, gn_ref, seg_ref,
             ret_ref, sin_ref_, dy_ref, dp_ref, dgn_ref, ds_ref):
        @pl.when(pl.program_id(0) == 0)
        def _():
            ds_ref[...] = jnp.zeros_like(ds_ref)
            dgn_ref[...] = jnp.zeros_like(dgn_ref)

        cosv, sinv = cos_ref[...], sin_ref[...]
        qt = _rope(q_ref[...], cosv, sinv)
        kt = _rope(k_ref[...], cosv, sinv) * (RET_DK ** -0.5)
        v = v_ref[...]
        s_in = sin_ref_[0]
        ds_out = ds_ref[...]
        seg = seg_ref[...]

        _, post_vjp = jax.vjp(lambda r_, g_, gn_: _ret_post(r_, g_, gn_, seg), ret_ref[...], g_ref[...], gn_ref[...])
        dret, dg, dgn = post_vjp(dy_ref[...])
        dgn_ref[...] += dgn

        qdv, kdv = qd_ref[...], kd_ref[...]
        dqt = qdv * _dot_nt(dret, s_in, HIGHEST)
        dkt = kdv * _dot_nt(v, ds_out, HIGHEST)
        dv_all = _dot(kt * kdv, ds_out, HIGHEST)
        dvs = []
        for h in range(RET_HEADS):
            sl = slice(h * RET_DV, (h + 1) * RET_DV)
            qh, a = _ret_scores(qt, kt, d_ref, h)
            lane = lax.broadcasted_iota(jnp.int32, kt.shape, 1)
            kh = jnp.where(lane // RET_DK == h, kt, 0.0)
            da = _dot_nt(dret[:, sl], v[:, sl], HIGHEST) * d_ref[h]
            dvs.append(_dot_tn(a, dret[:, sl], HIGHEST))
            dqt = dqt + _dot(da, kh, HIGHEST)
            dkt = dkt + _dot_tn(da, qh, HIGHEST)
        dv = dv_all + jnp.concatenate(dvs, axis=1)
        ds_ref[...] = ds_out * cd_ref[...] + _dot_tn(qt * qdv, dret, HIGHEST) * bm_ref[...]
        dq = _rope_bwd(dqt, cosv, sinv)
        dk = _rope_bwd(dkt * (RET_DK ** -0.5), cosv, sinv)
        dp_ref[...] = jnp.concatenate([dq, dk, dv, dg], axis=1)

    return pl.pallas_call(
        body, name="ret_bwd", grid=(nch,),
        in_specs=[pl.BlockSpec((C, RET_QK), rev(0)), pl.BlockSpec((C, RET_QK), rev(1)),
                  pl.BlockSpec((C, RET_WIDTH), rev(1)), pl.BlockSpec((C, RET_WIDTH), rev(2)),
                  pl.BlockSpec((C, RET_QK), rev(0)), pl.BlockSpec((C, RET_QK), rev(0)),
                  _full((C, RET_QK)), _full((C, RET_QK)), _full((RET_HEADS, C, C)),
                  _full((RET_QK, RET_WIDTH)), _full((RET_QK, RET_WIDTH)), _full((1, RET_WIDTH)),
                  _full((RET_WIDTH, RET_WIDTH)),
                  pl.BlockSpec((C, RET_WIDTH), rev(0)),
                  pl.BlockSpec((1, RET_QK, RET_WIDTH), lambda c: (nch - 1 - c, 0, 0)),
                  pl.BlockSpec((C, RET_WIDTH), rev(0))],
        out_specs=[pl.BlockSpec((C, RET_COLS), rev(0)), _full((1, RET_WIDTH))],
        out_shape=[jax.ShapeDtypeStruct((T, RET_COLS), F32), jax.ShapeDtypeStruct((1, RET_WIDTH), F32)],
        scratch_shapes=[pltpu.VMEM((RET_QK, RET_WIDTH), F32)],
        compiler_params=_cparams(dimension_semantics=("arbitrary",)),
    )(p_ret, p_ret, p_ret, p_ret, cos, sin, qd, kd, intra_d, cd, bm, gn_g, seg128, ret, s_in_all, dy)


def _prep_fn(p, prev, mu, w0, a0, k_k, k_a, lora, seg):
    W = RW_WIDTH
    ps = p + mu * (prev - p)
    r, kr, vr, g = ps[:, 0:W], ps[:, W:2 * W], ps[:, 2 * W:3 * W], ps[:, 3 * W:4 * W]
    z = ps[:, 4 * W:]
    lane = lax.broadcasted_iota(jnp.int32, z.shape, 1)
    z = jnp.where(lane < LORA, jnp.tanh(z), z)
    lo = _dot(z, lora, HIGHEST)
    w_log = -_softplus(-(w0 + lo[:, :W])) - 0.5
    decay = jnp.exp(-jnp.exp(w_log))
    a = jax.nn.sigmoid(a0 + lo[:, W:])
    kk = kr * k_k
    kk = kk / jnp.maximum(jnp.sqrt(_segsum(kk * kk, seg)), 1e-12)
    k = kr * (1.0 + (a - 1.0) * k_a)
    return kk, decay, kk * a, k, r, vr, g


def _post_fn(o, r, k, v, g, gn_g, gn_b, r_k, seg):
    mu = _segsum(o, seg) * (1.0 / RW_HEAD)
    oc = o - mu
    var = _segsum(oc * oc, seg) * (1.0 / RW_HEAD)
    on = oc * lax.rsqrt(var + RW_GN_EPS) * gn_g + gn_b
    bonus = _segsum(r * k * r_k, seg) * v
    return (g * jax.nn.sigmoid(g)) * (on + bonus)


def _shift_down(p, first_row):
    row = lax.broadcasted_iota(jnp.int32, p.shape, 0)
    return jnp.where(row == 0, first_row, pltpu.roll(p, 1, 0))


def _shift_up(p, last_row):
    n = p.shape[0]
    row = lax.broadcasted_iota(jnp.int32, p.shape, 0)
    return jnp.where(row == n - 1, last_row, pltpu.roll(p, n - 1, 0))


def _row_tile(T):
    return min(T, 256)


def _prep_fwd(p_rw, bnd, mu, w0, a0, k_k, k_a, lora, seg64):
    T = p_rw.shape[0]
    tm = _row_tile(T)
    W = RW_WIDTH

    def body(p_ref, bnd_ref, mu_ref, w0_ref, a0_ref, kk_ref, ka_ref, lora_ref, seg_ref, vecs_ref, v_ref, g_ref):
        p = p_ref[...]
        prev = _shift_down(p, bnd_ref[0])
        res = _prep_fn(p, prev, mu_ref[...], w0_ref[...], a0_ref[...], kk_ref[...], ka_ref[...], lora_ref[...],
                       seg_ref[...])
        for j in range(N_VEC):
            vecs_ref[j] = res[j]
        v_ref[...] = res[N_VEC]
        g_ref[...] = res[N_VEC + 1]

    small = _full((1, W))
    row = pl.BlockSpec((tm, W), lambda i: (i, 0))
    return pl.pallas_call(
        body, name="rwkv_prep_fwd", grid=(T // tm,),
        in_specs=[pl.BlockSpec((tm, RW_COLS), lambda i: (i, 0)), pl.BlockSpec((1, 1, RW_COLS), lambda i: (i, 0, 0)),
                  _full((1, RW_COLS)), small, small, small, small, _full((2 * LORA, 2 * W)), _full((W, W))],
        out_specs=[pl.BlockSpec((N_VEC, tm, W), lambda i: (0, i, 0)), row, row],
        out_shape=[jax.ShapeDtypeStruct((N_VEC, T, W), F32), jax.ShapeDtypeStruct((T, W), F32),
                   jax.ShapeDtypeStruct((T, W), F32)],
        compiler_params=_cparams(dimension_semantics=("arbitrary",)),
    )(p_rw, bnd, mu, w0, a0, k_k, k_a, lora, seg64)


def _prep_bwd(p_rw, bnd, mu, w0, a0, k_k, k_a, lora, seg64, cts):
    T = p_rw.shape[0]
    tm = _row_tile(T)
    W = RW_WIDTH

    def body(p_ref, bnd_ref, mu_ref, w0_ref, a0_ref, kk_ref, ka_ref, lora_ref, seg_ref,
             dvecs_ref, dk2_ref, dr2_ref, dv1_ref, dv2_ref, dg_ref,
             dp_ref, dprev_ref, dmu_ref, dw0_ref, da0_ref, dkk_p_ref, dka_ref, dlora_ref):
        accs = (dmu_ref, dw0_ref, da0_ref, dkk_p_ref, dka_ref, dlora_ref)

        @pl.when(pl.program_id(0) == 0)
        def _():
            for a_ref in accs:
                a_ref[...] = jnp.zeros_like(a_ref)

        p = p_ref[...]
        prev = _shift_down(p, bnd_ref[0])
        seg = seg_ref[...]
        _, vjp = jax.vjp(lambda *a: _prep_fn(*a, seg), p, prev, mu_ref[...], w0_ref[...], a0_ref[...], kk_ref[...],
                         ka_ref[...], lora_ref[...])
        ct = (dvecs_ref[0], dvecs_ref[1], dvecs_ref[2], dvecs_ref[3] + dk2_ref[...], dvecs_ref[4] + dr2_ref[...],
              dv1_ref[...] + dv2_ref[...], dg_ref[...])
        grads = vjp(ct)
        dp_ref[...] = grads[0]
        dprev_ref[...] = grads[1]
        for a_ref, gval in zip(accs, grads[2:]):
            a_ref[...] += gval

    small = _full((1, W))
    row = pl.BlockSpec((tm, W), lambda i: (i, 0))
    return pl.pallas_call(
        body, name="rwkv_prep_bwd", grid=(T // tm,),
        in_specs=[pl.BlockSpec((tm, RW_COLS), lambda i: (i, 0)), pl.BlockSpec((1, 1, RW_COLS), lambda i: (i, 0, 0)),
                  _full((1, RW_COLS)), small, small, small, small, _full((2 * LORA, 2 * W)), _full((W, W)),
                  pl.BlockSpec((N_VEC, tm, W), lambda i: (0, i, 0))] + [row] * 5,
        out_specs=[pl.BlockSpec((tm, RW_COLS), lambda i: (i, 0)), pl.BlockSpec((tm, RW_COLS), lambda i: (i, 0)),
                   _full((1, RW_COLS)), small, small, small, small, _full((2 * LORA, 2 * W))],
        out_shape=[jax.ShapeDtypeStruct((T, RW_COLS), F32), jax.ShapeDtypeStruct((T, RW_COLS), F32),
                   jax.ShapeDtypeStruct((1, RW_COLS), F32)] + [jax.ShapeDtypeStruct((1, W), F32)] * 4
                  + [jax.ShapeDtypeStruct((2 * LORA, 2 * W), F32)],
        compiler_params=_cparams(dimension_semantics=("arbitrary",)),
    )(p_rw, bnd, mu, w0, a0, k_k, k_a, lora, seg64, *cts)


def _vec_spec(j, tm):
    return pl.BlockSpec((None, tm, RW_WIDTH), lambda i: (j, i, 0))


def _post_fwd(o, vecs, v, g, gn_g, gn_b, r_k, seg64):
    T = o.shape[0]
    tm = _row_tile(T)
    W = RW_WIDTH

    def body(o_ref, r_ref, k_ref, v_ref, g_ref, gg_ref, gb_ref, rk_ref, seg_ref, y_ref):
        y_ref[...] = _post_fn(o_ref[...], r_ref[...], k_ref[...], v_ref[...], g_ref[...], gg_ref[...], gb_ref[...],
                              rk_ref[...], seg_ref[...]).astype(BF16)

    row = pl.BlockSpec((tm, W), lambda i: (i, 0))
    small = _full((1, W))
    return pl.pallas_call(
        body, name="rwkv_post_fwd", grid=(T // tm,),
        in_specs=[row, _vec_spec(4, tm), _vec_spec(3, tm), row, row] + [small] * 3 + [_full((W, W))],
        out_specs=row, out_shape=jax.ShapeDtypeStruct((T, W), BF16),
        compiler_params=_cparams(dimension_semantics=("arbitrary",)),
    )(o, vecs, vecs, v, g, gn_g, gn_b, r_k, seg64)


def _post_bwd(o, vecs, v, g, gn_g, gn_b, r_k, seg64, dy):
    T = o.shape[0]
    tm = _row_tile(T)
    W = RW_WIDTH

    def body(o_ref, r_ref, k_ref, v_ref, g_ref, gg_ref, gb_ref, rk_ref, seg_ref, dy_ref,
             do_ref, dr_ref, dk_ref, dv_ref, dg_ref, dgg_ref, dgb_ref, drk_ref):
        accs = (dgg_ref, dgb_ref, drk_ref)

        @pl.when(pl.program_id(0) == 0)
        def _():
            for a_ref in accs:
                a_ref[...] = jnp.zeros_like(a_ref)

        seg = seg_ref[...]
        _, vjp = jax.vjp(lambda *a: _post_fn(*a, seg), o_ref[...], r_ref[...], k_ref[...], v_ref[...], g_ref[...],
                         gg_ref[...], gb_ref[...], rk_ref[...])
        grads = vjp(dy_ref[...])
        for o_, gval in zip((do_ref, dr_ref, dk_ref, dv_ref, dg_ref), grads[:5]):
            o_[...] = gval
        for a_ref, gval in zip(accs, grads[5:]):
            a_ref[...] += gval

    row = pl.BlockSpec((tm, W), lambda i: (i, 0))
    small = _full((1, W))
    return pl.pallas_call(
        body, name="rwkv_post_bwd", grid=(T // tm,),
        in_specs=[row, _vec_spec(4, tm), _vec_spec(3, tm), row, row] + [small] * 3
                 + [_full((W, W)), pl.BlockSpec((tm, W), lambda i: (i, 1))],
        out_specs=[row] * 5 + [small] * 3,
        out_shape=[jax.ShapeDtypeStruct((T, W), F32)] * 5 + [jax.ShapeDtypeStruct((1, W), F32)] * 3,
        compiler_params=_cparams(dimension_semantics=("arbitrary",)),
    )(o, vecs, vecs, v, g, gn_g, gn_b, r_k, seg64, dy)


def _to_cols(vecs, T):
    Tc = WKV_CHUNK
    x = vecs.reshape(N_VEC, T // Tc, Tc, RW_HEADS, RW_HEAD)
    x = jnp.transpose(x, (1, 0, 4, 3, 2))
    return x.reshape(T // Tc, N_VEC * RW_HEAD, RW_HEADS * Tc)


def _from_cols(cols, T):
    Tc = WKV_CHUNK
    x = cols.reshape(T // Tc, N_VEC, RW_HEAD, RW_HEADS, Tc)
    return jnp.transpose(x, (1, 0, 4, 3, 2)).reshape(N_VEC, T, RW_WIDTH)


def _wkv_lhs(cols_ref):
    hi, lo = _split(cols_ref[0])
    return jnp.concatenate([hi, lo], axis=1)


def _wkv_step(st, kk_e, w_e, b_e, k_e, v_row):
    sa = -jnp.sum(st * kk_e, axis=0, keepdims=True)
    return st * w_e + b_e * sa + k_e * v_row, sa


def _wkv_fwd(cols, v, e_tab):
    T = v.shape[0]
    Tc = WKV_CHUNK
    nch = T // Tc
    J, W = RW_HEAD, RW_WIDTH

    def body(cols_ref, v_ref, e_ref, o_ref, states_ref, sa_ref, last_ref, s_ref):
        @pl.when(pl.program_id(0) == 0)
        def _():
            s_ref[...] = jnp.zeros_like(s_ref)

        lhs = _wkv_lhs(cols_ref)
        st = s_ref[...]
        for t in range(Tc):
            ex = _dot(lhs, e_ref[t])
            states_ref[t] = st
            st, sa = _wkv_step(st, ex[0:J], ex[J:2 * J], ex[2 * J:3 * J], ex[3 * J:4 * J], v_ref[t:t + 1, :])
            sa_ref[t:t + 1, :] = sa
            o_ref[t:t + 1, :] = jnp.sum(st * ex[4 * J:5 * J], axis=0, keepdims=True)
        s_ref[...] = st
        last_ref[...] = st

    return pl.pallas_call(
        body, name="wkv_fwd", grid=(nch,),
        in_specs=[pl.BlockSpec((1, N_VEC * J, 128), lambda c: (c, 0, 0)), pl.BlockSpec((Tc, W), lambda c: (c, 0)),
                  _full((Tc, 2 * 128, W))],
        out_specs=[pl.BlockSpec((Tc, W), lambda c: (c, 0)), pl.BlockSpec((Tc, J, W), lambda c: (c, 0, 0)),
                   pl.BlockSpec((Tc, W), lambda c: (c, 0)), _full((J, W))],
        out_shape=[jax.ShapeDtypeStruct((T, W), F32), jax.ShapeDtypeStruct((T, J, W), F32),
                   jax.ShapeDtypeStruct((T, W), F32), jax.ShapeDtypeStruct((J, W), F32)],
        scratch_shapes=[pltpu.VMEM((J, W), F32)],
        compiler_params=_cparams(dimension_semantics=("arbitrary",)),
    )(cols, v, e_tab)


def _wkv_bwd(cols, v, do, states, sa, s_last, e_tab, r_tab):
    T = v.shape[0]
    Tc = WKV_CHUNK
    nch = T // Tc
    J, W = RW_HEAD, RW_WIDTH
    blocks = [slice(b * 128, (b + 1) * 128) for b in range(W // 128)]

    def body(cols_ref, v_ref, do_ref, states_ref, sa_ref, last_ref, e_ref, r_ref, dv_ref, dcols_ref, ds_ref, next_ref):
        @pl.when(pl.program_id(0) == 0)
        def _():
            ds_ref[...] = jnp.zeros_like(ds_ref)
            next_ref[...] = last_ref[...]

        lhs = _wkv_lhs(cols_ref)
        dst = [ds_ref[:, b] for b in blocks]
        acc = jnp.zeros((N_VEC * J, 128), F32)
        for t in reversed(range(Tc)):
            ex = _dot(lhs, e_ref[t])
            dvs, prods = [], []
            for i, b in enumerate(blocks):
                kk_e, w_e, b_e, k_e, r_e = (ex[n * J:(n + 1) * J, b] for n in range(N_VEC))
                s_old = states_ref[t, :, b]
                s_new = states_ref[t + 1, :, b] if t + 1 < Tc else next_ref[:, b]
                do_row, v_row, sa_row = do_ref[t:t + 1, b], v_ref[t:t + 1, b], sa_ref[t:t + 1, b]
                dsn = dst[i] + r_e * do_row
                dsa = jnp.sum(dsn * b_e, axis=0, keepdims=True)
                dvs.append(jnp.sum(dsn * k_e, axis=0, keepdims=True))
                prods.append(jnp.concatenate(
                    [s_old * (-dsa), dsn * s_old, dsn * sa_row, dsn * v_row, s_new * do_row], axis=0).astype(BF16))
                dst[i] = dsn * w_e - kk_e * dsa
            dv_ref[t:t + 1, :] = jnp.concatenate(dvs, axis=1)
            acc = acc + _dot(jnp.concatenate(prods, axis=1), r_ref[t])
        for i, b in enumerate(blocks):
            ds_ref[:, b] = dst[i]
        next_ref[...] = states_ref[0]
        dcols_ref[0] = acc

    rev2 = lambda c: (nch - 1 - c, 0)
    rev3 = lambda c: (nch - 1 - c, 0, 0)
    return pl.pallas_call(
        body, name="wkv_bwd", grid=(nch,),
        in_specs=[pl.BlockSpec((1, N_VEC * J, 128), rev3), pl.BlockSpec((Tc, W), rev2), pl.BlockSpec((Tc, W), rev2),
                  pl.BlockSpec((Tc, J, W), rev3), pl.BlockSpec((Tc, W), rev2), _full((J, W)),
                  _full((Tc, 2 * 128, W)), _full((Tc, W, 128))],
        out_specs=[pl.BlockSpec((Tc, W), rev2), pl.BlockSpec((1, N_VEC * J, 128), rev3)],
        out_shape=[jax.ShapeDtypeStruct((T, W), F32), jax.ShapeDtypeStruct((nch, N_VEC * J, 128), F32)],
        scratch_shapes=[pltpu.VMEM((J, W), F32), pltpu.VMEM((J, W), F32)],
        compiler_params=_cparams(dimension_semantics=("arbitrary",)),
    )(cols, v, do, states, sa, s_last, e_tab, r_tab)


def _outproj(x, y_ret, y_rw, w_out_b, target, gf):
    T = x.shape[0]
    tm = _row_tile(T)
    W = RW_WIDTH

    def body(x_ref, yr_ref, yw_ref, w_ref, t_ref, gf_ref, loss_ref, dh_ref, dy_ref, dw_ref, dgf_ref):
        @pl.when(pl.program_id(0) == 0)
        def _():
            loss_ref[...] = jnp.zeros_like(loss_ref)
            dw_ref[...] = jnp.zeros_like(dw_ref)
            dgf_ref[...] = jnp.zeros_like(dgf_ref)

        y = jnp.concatenate([yr_ref[...], yw_ref[...]], axis=1)
        w = w_ref[...]
        h = x_ref[...] + _dot(y, w)
        rstd = lax.rsqrt(jnp.mean(h * h, axis=-1, keepdims=True) + RMS_EPS)
        hn = h * rstd
        gfv = gf_ref[...]
        err = hn * gfv - t_ref[...]
        loss_ref[...] += 0.5 * jnp.sum(jnp.mean(err * err, axis=-1))
        dout = err * (1.0 / D_MODEL)
        dgf_ref[...] += jnp.sum(dout * hn, axis=0, keepdims=True)
        dhn = dout * gfv
        dh = rstd * (dhn - hn * jnp.mean(dhn * hn, axis=-1, keepdims=True))
        dh_ref[...] = dh
        dhb = dh.astype(BF16)
        dy_ref[...] = _dot_nt(dhb, w)
        dw_ref[...] += _dot_tn(y, dhb)

    return pl.pallas_call(
        body, name="outproj_loss", grid=(T // tm,),
        in_specs=[pl.BlockSpec((tm, D_MODEL), lambda i: (i, 0)), pl.BlockSpec((tm, W), lambda i: (i, 0)),
                  pl.BlockSpec((tm, W), lambda i: (i, 0)), _full((D_MODEL, D_MODEL)),
                  pl.BlockSpec((tm, D_MODEL), lambda i: (i, 0)), _full((1, D_MODEL))],
        out_specs=[_full((1, PACK_W)), pl.BlockSpec((tm, D_MODEL), lambda i: (i, 0)),
                   pl.BlockSpec((tm, D_MODEL), lambda i: (i, 0)), _full((D_MODEL, D_MODEL)), _full((1, D_MODEL))],
        out_shape=[jax.ShapeDtypeStruct((1, PACK_W), F32), jax.ShapeDtypeStruct((T, D_MODEL), F32),
                   jax.ShapeDtypeStruct((T, D_MODEL), F32), jax.ShapeDtypeStruct((D_MODEL, D_MODEL), F32),
                   jax.ShapeDtypeStruct((1, D_MODEL), F32)],
        compiler_params=_cparams(dimension_semantics=("arbitrary",)),
    )(x, y_ret, y_rw, w_out_b, target, gf)


def _inproj_bwd_x(dp_ret, dp_rw, dprev, dbnd, w_b, x, norm_g, dh):
    T = x.shape[0]
    tm = _row_tile(T)

    def body(dpr_ref, dpw_ref, dprev_ref, dbnd_ref, w_ref, x_ref, g_ref, dh_ref, gx_ref, dg_ref, dpb_ref):
        @pl.when(pl.program_id(0) == 0)
        def _():
            dg_ref[...] = jnp.zeros_like(dg_ref)

        d_rw = dpw_ref[...] + _shift_up(dprev_ref[...], dbnd_ref[0])
        dpb = jnp.concatenate([dpr_ref[...].astype(BF16), d_rw.astype(BF16)], axis=1)
        dpb_ref[...] = dpb
        du = _dot_nt(dpb, w_ref[...])
        xf = x_ref[...]
        rstd = lax.rsqrt(jnp.mean(xf * xf, axis=-1, keepdims=True) + RMS_EPS)
        xn = xf * rstd
        dg_ref[...] += jnp.sum(du * xn, axis=0, keepdims=True)
        dxn = du * g_ref[...]
        gx_ref[...] = dh_ref[...] + rstd * (dxn - xn * jnp.mean(dxn * xn, axis=-1, keepdims=True))

    return pl.pallas_call(
        body, name="inproj_bwd_x", grid=(T // tm,),
        in_specs=[pl.BlockSpec((tm, RET_COLS), lambda i: (i, 0)), pl.BlockSpec((tm, RW_COLS), lambda i: (i, 0)),
                  pl.BlockSpec((tm, RW_COLS), lambda i: (i, 0)), pl.BlockSpec((1, 1, RW_COLS), lambda i: (i, 0, 0)),
                  _full((D_MODEL, IN_COLS)), pl.BlockSpec((tm, D_MODEL), lambda i: (i, 0)), _full((1, D_MODEL)),
                  pl.BlockSpec((tm, D_MODEL), lambda i: (i, 0))],
        out_specs=[pl.BlockSpec((tm, D_MODEL), lambda i: (i, 0)), _full((1, D_MODEL)),
                   pl.BlockSpec((tm, IN_COLS), lambda i: (i, 0))],
        out_shape=[jax.ShapeDtypeStruct((T, D_MODEL), F32), jax.ShapeDtypeStruct((1, D_MODEL), F32),
                   jax.ShapeDtypeStruct((T, IN_COLS), BF16)],
        compiler_params=_cparams(dimension_semantics=("arbitrary",)),
    )(dp_ret, dp_rw, dprev, dbnd, w_b, x, norm_g, dh)


def _inproj_bwd_w(u_t, dpb):
    T = u_t.shape[1]
    tr = 256

    def body(u_ref, d_ref, o_ref):
        o_ref[...] = _dot(u_ref[...], d_ref[...])

    return pl.pallas_call(
        body, name="inproj_bwd_w", grid=(D_MODEL // tr,),
        in_specs=[pl.BlockSpec((tr, T), lambda i: (i, 0)), _full((T, IN_COLS))],
        out_specs=pl.BlockSpec((tr, IN_COLS), lambda i: (i, 0)),
        out_shape=jax.ShapeDtypeStruct((D_MODEL, IN_COLS), F32),
        compiler_params=_cparams(dimension_semantics=("arbitrary",)),
    )(u_t, dpb)


def _tile_boundaries(a, tm, first):
    T, n = a.shape
    zero = jnp.zeros((1, n), a.dtype)
    if first:
        rows = jnp.concatenate([zero, a[tm - 1:T - 1:tm]], axis=0)
    else:
        rows = jnp.concatenate([a[tm:T:tm], zero], axis=0)
    return rows.reshape(T // tm, 1, n)


def _local_step(x, target, w_in_b, w_out_b, lora, small):
    T = x.shape[0]
    tm = _row_tile(T)
    cos, sin = _rope_tables(T)
    tabs = _ret_tables()
    seg128 = _seg_matrix(RET_WIDTH, RET_DV)
    seg64 = _seg_matrix(RW_WIDTH, RW_HEAD)
    e_tab = _wkv_expand_table()
    r_tab = _wkv_reduce_table()
    prep_w = (small["rwkv_mu"], small["w0"], small["a0"], small["k_k"], small["k_a"], lora, seg64)
    post_w = (small["rwkv_gn_g"], small["rwkv_gn_b"], small["r_k"], seg64)

    p_ret, p_rw, u = _inproj_fwd(x, small["norm_g"], w_in_b)
    y_ret, ret, s_in_all = _ret_fwd(p_ret, cos, sin, tabs, small["ret_gn_g"], seg128)
    bnd = _tile_boundaries(p_rw, tm, True)
    vecs, v, g = _prep_fwd(p_rw, bnd, *prep_w)
    cols = _to_cols(vecs, T)
    o, states, sa, s_last = _wkv_fwd(cols, v, e_tab)
    y_rw = _post_fwd(o, vecs, v, g, *post_w)
    loss, dh, dy, d_w_out, d_gf = _outproj(x, y_ret, y_rw, w_out_b, target, small["final_norm_g"])

    do, dr2, dk2, dv2, dg, d_gn_g, d_gn_b, d_r_k = _post_bwd(o, vecs, v, g, *post_w, dy)
    dv1, dcols = _wkv_bwd(cols, v, do, states, sa, s_last, e_tab, r_tab)
    dp_rw, dprev, d_mu, d_w0, d_a0, d_k_k, d_k_a, d_lora = _prep_bwd(
        p_rw, bnd, *prep_w, (_from_cols(dcols, T), dk2, dr2, dv1, dv2, dg))
    dp_ret, d_ret_gn = _ret_bwd(p_ret, cos, sin, tabs, small["ret_gn_g"], seg128, ret, s_in_all, dy)
    dbnd = _tile_boundaries(dprev, tm, False)
    grad_x, d_norm_g, dpb = _inproj_bwd_x(dp_ret, dp_rw, dprev, dbnd, w_in_b, x, small["norm_g"], dh)
    d_w_in = _inproj_bwd_w(jnp.transpose(u), dpb)

    d_small = {"norm_g": d_norm_g, "ret_gn_g": d_ret_gn, "rwkv_mu": d_mu, "w0": d_w0, "a0": d_a0, "k_k": d_k_k,
               "k_a": d_k_a, "r_k": d_r_k, "rwkv_gn_g": d_gn_g, "rwkv_gn_b": d_gn_b, "final_norm_g": d_gf}
    return loss, grad_x, d_w_in, d_w_out, d_lora, d_small


ANY = pl.BlockSpec(memory_space=pl.ANY)
CHIP_FLIPS = ((0, 1), (1, 0), (1, 1))
N_FLIPS = len(CHIP_FLIPS)
LORA_SHARD = RW_WIDTH // N_CHIPS
HALF_IN = D_MODEL // 2
HALF_OUT = OUT_SHARD // 2


def _position():
    return lax.axis_index("x"), lax.axis_index("y"), lax.axis_index("c")


def _flip(v, f):
    return 1 - v if f else v


def _finish(local, remote, landed):
    for cp in landed:
        cp.wait_recv()
    for cp in remote:
        cp.wait_send()
    for cp in local:
        cp.wait()


def _gather_chips(arrs):
    n = len(arrs)
    halves = [a.shape[0] // 2 for a in arrs]

    def body(*refs):
        ins, outs = refs[:n], refs[n:2 * n]
        send, recv, pass_send, pass_recv, loc = refs[2 * n:]
        x, y, c = _position()
        s = 2 * x + y
        sibling = (x, y, 1 - c)

        def copy(src, dst, sems, k, to):
            return pltpu.make_async_remote_copy(src_ref=src, dst_ref=dst, send_sem=sems[0].at[k], recv_sem=sems[1].at[k],
                                                device_id=to, device_id_type=MESH)

        local, remote, landed, passed, passed_in = [], [], [], [], []
        for a in range(n):
            mine = pl.ds(c * halves[a], halves[a])
            theirs = pl.ds((1 - c) * halves[a], halves[a])
            local.append(pltpu.make_async_copy(ins[a], outs[a].at[s], loc.at[a]))
            for j, (fx, fy) in enumerate(CHIP_FLIPS):
                px, py = _flip(x, fx), _flip(y, fy)
                ps = 2 * px + py
                k = a * N_FLIPS + j
                remote.append(copy(ins[a].at[mine], outs[a].at[s, mine], (send, recv), k, (px, py, c)))
                landed.append(copy(ins[a].at[mine], outs[a].at[ps, mine], (send, recv), k, (px, py, c)))
                passed.append(copy(outs[a].at[ps, mine], outs[a].at[ps, mine], (pass_send, pass_recv), k, sibling))
                passed_in.append(copy(outs[a].at[ps, theirs], outs[a].at[ps, theirs], (pass_send, pass_recv), k, sibling))
        for cp in local + remote:
            cp.start()
        for arrived, onward in zip(landed, passed):
            arrived.wait_recv()
            onward.start()
        _finish(local, remote + passed, passed_in)

    sems = pltpu.SemaphoreType.DMA((n * N_FLIPS,))
    return pl.pallas_call(
        body, name="gather_weights",
        in_specs=[ANY] * n, out_specs=[ANY] * n,
        out_shape=[jax.ShapeDtypeStruct((N_CHIPS,) + a.shape, a.dtype) for a in arrs],
        scratch_shapes=[sems, sems, sems, sems, pltpu.SemaphoreType.DMA((n,))],
    )(*arrs)


def _pair_exchange(g_in, g_out, g_small):
    def body(gi_ref, go_ref, gs_ref, li_ref, lo_ref, ls_ref, send, recv):
        x, y, c = _position()
        peer = (x, y, 1 - c)
        srcs = (gi_ref.at[:, pl.ds((1 - c) * HALF_IN, HALF_IN), :], go_ref.at[:, pl.ds((1 - c) * HALF_OUT, HALF_OUT), :],
                gs_ref)
        remote = [pltpu.make_async_remote_copy(src_ref=src, dst_ref=dst, send_sem=send.at[k], recv_sem=recv.at[k],
                                               device_id=peer, device_id_type=MESH)
                  for k, (src, dst) in enumerate(zip(srcs, (li_ref, lo_ref, ls_ref)))]
        for cp in remote:
            cp.start()
        _finish([], remote, remote)

    return pl.pallas_call(
        body, name="pair_exchange",
        in_specs=[ANY] * 3, out_specs=[ANY] * 3,
        out_shape=[jax.ShapeDtypeStruct((N_CHIPS, HALF_IN, IN_SHARD), F32),
                   jax.ShapeDtypeStruct((N_CHIPS, HALF_OUT, D_MODEL), F32),
                   jax.ShapeDtypeStruct(g_small.shape, F32)],
        scratch_shapes=[pltpu.SemaphoreType.DMA((3,)), pltpu.SemaphoreType.DMA((3,))],
    )(g_in, g_out, g_small)


def _pair_sum(g_in, g_out, g_small, l_in, l_out, l_small, c_arr):
    tr = HALF_IN // 2

    def body(c_ref, gi_ref, go_ref, gs_ref, li_ref, lo_ref, ls_ref, ci_ref, co_ref, cs_ref):
        ci_ref[...] = (gi_ref[...] + li_ref[...]).astype(BF16)

        @pl.when(pl.program_id(1) == 0)
        def _():
            co_ref[...] = (go_ref[...] + lo_ref[...]).astype(BF16)

        @pl.when((pl.program_id(0) == 0) & (pl.program_id(1) == 0))
        def _():
            cs_ref[...] = gs_ref[...] + ls_ref[...]

    nd = g_small.shape
    return pl.pallas_call(
        body, name="pair_sum",
        grid_spec=pltpu.PrefetchScalarGridSpec(
            num_scalar_prefetch=1, grid=(N_CHIPS, 2),
            in_specs=[pl.BlockSpec((1, tr, IN_SHARD), lambda s, i, c: (s, 2 * c[0] + i, 0)),
                      pl.BlockSpec((1, HALF_OUT, D_MODEL), lambda s, i, c: (s, c[0], 0)),
                      pl.BlockSpec(nd, lambda s, i, c: (0, 0)),
                      pl.BlockSpec((1, tr, IN_SHARD), lambda s, i, c: (s, i, 0)),
                      pl.BlockSpec((1, HALF_OUT, D_MODEL), lambda s, i, c: (s, 0, 0)),
                      pl.BlockSpec(nd, lambda s, i, c: (0, 0))],
            out_specs=[pl.BlockSpec((1, tr, IN_SHARD), lambda s, i, c: (s, i, 0)),
                       pl.BlockSpec((1, HALF_OUT, D_MODEL), lambda s, i, c: (s, 0, 0)),
                       pl.BlockSpec(nd, lambda s, i, c: (0, 0))]),
        out_shape=[jax.ShapeDtypeStruct((N_CHIPS, HALF_IN, IN_SHARD), BF16),
                   jax.ShapeDtypeStruct((N_CHIPS, HALF_OUT, D_MODEL), BF16), jax.ShapeDtypeStruct(nd, F32)],
        compiler_params=_cparams(dimension_semantics=("arbitrary", "arbitrary")),
    )(c_arr, g_in, g_out, g_small, l_in, l_out, l_small)


def _chip_exchange(c_in, c_out, c_small):
    def body(ci_ref, co_ref, cs_ref, li_ref, lo_ref, ls_ref, send, recv):
        x, y, c = _position()
        s = 2 * x + y
        remote = []
        for j, (fx, fy) in enumerate(CHIP_FLIPS):
            px, py = _flip(x, fx), _flip(y, fy)
            ps = 2 * px + py
            for a, (src, dst) in enumerate(((ci_ref.at[ps], li_ref.at[j]), (co_ref.at[ps], lo_ref.at[j]),
                                            (cs_ref, ls_ref.at[j]))):
                k = 3 * j + a
                remote.append(pltpu.make_async_remote_copy(src_ref=src, dst_ref=dst, send_sem=send.at[k],
                                                           recv_sem=recv.at[k], device_id=(px, py, c),
                                                           device_id_type=MESH))
        for cp in remote:
            cp.start()
        _finish([], remote, remote)

    return pl.pallas_call(
        body, name="chip_exchange",
        in_specs=[ANY] * 3, out_specs=[ANY] * 3,
        out_shape=[jax.ShapeDtypeStruct((N_FLIPS, HALF_IN, IN_SHARD), c_in.dtype),
                   jax.ShapeDtypeStruct((N_FLIPS, HALF_OUT, D_MODEL), c_out.dtype),
                   jax.ShapeDtypeStruct((N_FLIPS,) + c_small.shape, F32)],
        scratch_shapes=[pltpu.SemaphoreType.DMA((3 * N_FLIPS,)), pltpu.SemaphoreType.DMA((3 * N_FLIPS,))],
    )(c_in, c_out, c_small)


def _chip_sum(g_in, g_out, p_in, p_out, c_small, l_in, l_out, l_small, sc_arr):
    tr = HALF_IN // 2
    nd = c_small.shape

    def body(s_ref, gi_ref, go_ref, pi_ref, po_ref, cs_ref, li0, li1, li2, lo0, lo1, lo2, ls_ref,
             ri_ref, ro_ref, rs_ref):
        ri_ref[...] = (((gi_ref[0] + pi_ref[0]) + li0[0].astype(F32)) + li1[0].astype(F32)) + li2[0].astype(F32)

        @pl.when(pl.program_id(0) == 0)
        def _():
            ro_ref[...] = (((go_ref[0] + po_ref[0]) + lo0[0].astype(F32)) + lo1[0].astype(F32)) + lo2[0].astype(F32)
            me = s_ref[0]
            parts = (cs_ref[...], ls_ref[0], ls_ref[1], ls_ref[2])

            def of_chip(s):
                m = jnp.bitwise_xor(me, s)
                return jnp.where(m == 0, parts[0], jnp.where(m == 1, parts[1], jnp.where(m == 2, parts[2], parts[3])))

            rs_ref[...] = ((of_chip(0) + of_chip(1)) + of_chip(2)) + of_chip(3)

    def flip_in(j):
        return pl.BlockSpec((1, tr, IN_SHARD), lambda i, s: (j, i, 0))

    def flip_out(j):
        return pl.BlockSpec((1, HALF_OUT, D_MODEL), lambda i, s: (j, 0, 0))

    return pl.pallas_call(
        body, name="chip_sum",
        grid_spec=pltpu.PrefetchScalarGridSpec(
            num_scalar_prefetch=1, grid=(2,),
            in_specs=[pl.BlockSpec((1, tr, IN_SHARD), lambda i, s: (s[0], 2 * s[1] + i, 0)),
                      pl.BlockSpec((1, HALF_OUT, D_MODEL), lambda i, s: (s[0], s[1], 0)),
                      pl.BlockSpec((1, tr, IN_SHARD), lambda i, s: (s[0], i, 0)),
                      pl.BlockSpec((1, HALF_OUT, D_MODEL), lambda i, s: (s[0], 0, 0)),
                      pl.BlockSpec(nd, lambda i, s: (0, 0)),
                      flip_in(0), flip_in(1), flip_in(2), flip_out(0), flip_out(1), flip_out(2),
                      pl.BlockSpec((N_FLIPS,) + nd, lambda i, s: (0, 0, 0))],
            out_specs=[pl.BlockSpec((tr, IN_SHARD), lambda i, s: (i, 0)),
                       pl.BlockSpec((HALF_OUT, D_MODEL), lambda i, s: (0, 0)),
                       pl.BlockSpec(nd, lambda i, s: (0, 0))]),
        out_shape=[jax.ShapeDtypeStruct((HALF_IN, IN_SHARD), F32), jax.ShapeDtypeStruct((HALF_OUT, D_MODEL), F32),
                   jax.ShapeDtypeStruct(nd, F32)],
        compiler_params=_cparams(dimension_semantics=("arbitrary",)),
    )(sc_arr, g_in, g_out, p_in, p_out, c_small, l_in, l_in, l_in, l_out, l_out, l_out, l_small)


def _pair_share(r_in, r_out):
    def body(ri_ref, ro_ref, fi_ref, fo_ref, send, recv, loc):
        x, y, c = _position()
        peer = (x, y, 1 - c)
        local, remote, landed = [], [], []
        for k, (src, dst, rows) in enumerate(((ri_ref, fi_ref, HALF_IN), (ro_ref, fo_ref, HALF_OUT))):
            local.append(pltpu.make_async_copy(src, dst.at[pl.ds(c * rows, rows), :], loc.at[k]))
            remote.append(pltpu.make_async_remote_copy(
                src_ref=src, dst_ref=dst.at[pl.ds(c * rows, rows), :], send_sem=send.at[k], recv_sem=recv.at[k],
                device_id=peer, device_id_type=MESH))
            landed.append(pltpu.make_async_remote_copy(
                src_ref=src, dst_ref=dst.at[pl.ds((1 - c) * rows, rows), :], send_sem=send.at[k], recv_sem=recv.at[k],
                device_id=peer, device_id_type=MESH))
        for cp in local + remote:
            cp.start()
        _finish(local, remote, landed)

    return pl.pallas_call(
        body, name="pair_share",
        in_specs=[ANY] * 2, out_specs=[ANY] * 2,
        out_shape=[jax.ShapeDtypeStruct((D_MODEL, IN_SHARD), F32), jax.ShapeDtypeStruct((OUT_SHARD, D_MODEL), F32)],
        scratch_shapes=[pltpu.SemaphoreType.DMA((2,)), pltpu.SemaphoreType.DMA((2,)), pltpu.SemaphoreType.DMA((2,))],
    )(r_in, r_out)


def _adamw(name, w, g, m, v, tr):
    rows, cols = w.shape

    def body(w_ref, g_ref, m_ref, v_ref, d_ref, nm_ref, nv_ref):
        gv = g_ref[...]
        mn = ADAM_B1 * m_ref[...] + (1.0 - ADAM_B1) * gv
        vn = ADAM_B2 * v_ref[...] + (1.0 - ADAM_B2) * jnp.square(gv)
        m_hat = mn / (1.0 - ADAM_B1 ** ADAM_STEP)
        v_hat = vn / (1.0 - ADAM_B2 ** ADAM_STEP)
        d_ref[...] = -ADAM_LR * (m_hat / (jnp.sqrt(v_hat) + ADAM_EPS) + ADAM_WD * w_ref[...])
        nm_ref[...] = mn
        nv_ref[...] = vn

    spec = pl.BlockSpec((tr, cols), lambda i: (i, 0))
    return pl.pallas_call(
        body, name=name, grid=(rows // tr,), in_specs=[spec] * 4, out_specs=[spec] * 3,
        out_shape=[jax.ShapeDtypeStruct((rows, cols), F32)] * 3,
        compiler_params=_cparams(dimension_semantics=("arbitrary",)),
    )(w, g, m, v)


def _row_pieces(n):
    return [(k, k * PACK_W, min(PACK_W, n - k * PACK_W)) for k in range(-(-n // PACK_W))]


def _pack_small(d_small, loss, d_lora):
    ns = len(SMALL_NAMES)

    def body(*refs):
        small_refs, (loss_ref, lora_ref, out_ref) = refs[:ns], refs[ns:]
        out_ref[...] = jnp.zeros_like(out_ref)
        out_ref[PACK_LORA_W:PACK_LORA_W + LORA, :] = lora_ref[:LORA, :RW_WIDTH]
        out_ref[PACK_LORA_A:PACK_LORA_A + LORA, :] = lora_ref[LORA:, RW_WIDTH:]
        for name, n, ref in zip(SMALL_NAMES, SMALL_SIZES, small_refs):
            for k, at, w in _row_pieces(n):
                out_ref[PACK_AT[name] + k:PACK_AT[name] + k + 1, 0:w] = ref[:, at:at + w]
        out_ref[PACK_LOSS:PACK_LOSS + 1, :] = loss_ref[...]

    return pl.pallas_call(body, name="pack_small", out_shape=jax.ShapeDtypeStruct((PACK_ROWS, PACK_W), F32),
                          compiler_params=_cparams())(*d_small, loss, d_lora)


def _adam_update(w, g, m, v):
    mn = ADAM_B1 * m + (1.0 - ADAM_B1) * g
    vn = ADAM_B2 * v + (1.0 - ADAM_B2) * jnp.square(g)
    m_hat = mn / (1.0 - ADAM_B1 ** ADAM_STEP)
    v_hat = vn / (1.0 - ADAM_B2 ** ADAM_STEP)
    return -ADAM_LR * (m_hat / (jnp.sqrt(v_hat) + ADAM_EPS) + ADAM_WD * w), mn, vn


def _adamw_small(tot, chip_arr, ws, ms, vs):
    ns = len(SMALL_NAMES)
    n_par = ns + 2

    def body(s_ref, tot_ref, glw_ref, gla_ref, *refs):
        w_refs, m_refs, v_refs = refs[:n_par], refs[n_par:2 * n_par], refs[2 * n_par:3 * n_par]
        outs = refs[3 * n_par:]
        g_refs, d_refs, nm_refs, nv_refs = (outs[i * n_par:(i + 1) * n_par] for i in range(4))
        grads = [jnp.concatenate([tot_ref[PACK_AT[name] + k:PACK_AT[name] + k + 1, 0:w] for k, _, w in _row_pieces(n)],
                                 axis=1) for name, n in zip(SMALL_NAMES, SMALL_SIZES)]
        grads += [glw_ref[...], gla_ref[...]]
        for i, g in enumerate(grads):
            d, mn, vn = _adam_update(w_refs[i][...], g, m_refs[i][...], v_refs[i][...])
            g_refs[i][...] = g
            d_refs[i][...] = d
            nm_refs[i][...] = mn
            nv_refs[i][...] = vn

    def whole(a):
        nd = a.ndim
        return pl.BlockSpec(a.shape, lambda i, s: (0,) * nd)

    shard = (LORA, LORA_SHARD)
    par_specs = [whole(a) for a in ws]
    res = pl.pallas_call(
        body, name="adamw_small",
        grid_spec=pltpu.PrefetchScalarGridSpec(
            num_scalar_prefetch=1, grid=(1,),
            in_specs=[whole(tot), pl.BlockSpec(shard, lambda i, s: (PACK_LORA_W // LORA, s[0])),
                      pl.BlockSpec(shard, lambda i, s: (PACK_LORA_A // LORA, s[0]))] + par_specs * 3,
            out_specs=par_specs * 4),
        out_shape=[jax.ShapeDtypeStruct(a.shape, F32) for a in ws] * 4,
        compiler_params=_cparams(dimension_semantics=("arbitrary",)),
    )(chip_arr, tot, tot, tot, *ws, *ms, *vs)
    return [res[i * n_par:(i + 1) * n_par] for i in range(4)]


def kernel(x, norm_g, w_in, ret_gn_g, rwkv_mu, w_lora_up, w0, a_lora_up, a0, k_k, k_a, r_k, rwkv_gn_g, rwkv_gn_b, w_out, final_norm_g, loss_target, m_norm_g, m_w_in, m_ret_gn_g, m_rwkv_mu, m_w_lora_up, m_w0, m_a_lora_up, m_a0, m_k_k, m_k_a, m_r_k, m_rwkv_gn_g, m_rwkv_gn_b, m_w_out, m_final_norm_g, v_norm_g, v_w_in, v_ret_gn_g, v_rwkv_mu, v_w_lora_up, v_w0, v_a_lora_up, v_a0, v_k_k, v_k_a, v_r_k, v_rwkv_gn_g, v_rwkv_gn_b, v_w_out, v_final_norm_g):
    W = RW_WIDTH
    params = dict(norm_g=norm_g, ret_gn_g=ret_gn_g, rwkv_mu=rwkv_mu, w0=w0, a0=a0, k_k=k_k, k_a=k_a, r_k=r_k,
                  rwkv_gn_g=rwkv_gn_g, rwkv_gn_b=rwkv_gn_b, final_norm_g=final_norm_g)
    moments_m = dict(norm_g=m_norm_g, ret_gn_g=m_ret_gn_g, rwkv_mu=m_rwkv_mu, w0=m_w0, a0=m_a0, k_k=m_k_k, k_a=m_k_a,
                     r_k=m_r_k, rwkv_gn_g=m_rwkv_gn_g, rwkv_gn_b=m_rwkv_gn_b, final_norm_g=m_final_norm_g)
    moments_v = dict(norm_g=v_norm_g, ret_gn_g=v_ret_gn_g, rwkv_mu=v_rwkv_mu, w0=v_w0, a0=v_a0, k_k=v_k_k, k_a=v_k_a,
                     r_k=v_r_k, rwkv_gn_g=v_rwkv_gn_g, rwkv_gn_b=v_rwkv_gn_b, final_norm_g=v_final_norm_g)
    xi, yi, ci = _position()
    chip = (2 * xi + yi).astype(jnp.int32)

    g_in, g_out, g_lw, g_la = _gather_chips([w_in[0].astype(BF16), w_out[0].astype(BF16), w_lora_up[0], a_lora_up[0]])
    w_in_b = jnp.transpose(g_in, (1, 0, 2)).reshape(D_MODEL, IN_COLS)
    w_out_b = g_out.reshape(D_MODEL, D_MODEL)
    lw = jnp.transpose(g_lw, (1, 0, 2)).reshape(LORA, W)
    la = jnp.transpose(g_la, (1, 0, 2)).reshape(LORA, W)
    zero = jnp.zeros((LORA, W), F32)
    lora = jnp.concatenate([jnp.concatenate([lw, zero], axis=1), jnp.concatenate([zero, la], axis=1)], axis=0)
    small = {n: params[n].reshape(1, -1) for n in SMALL_NAMES}

    loss, grad_x, d_w_in, d_w_out, d_lora, d_small = _local_step(x[0], loss_target[0], w_in_b, w_out_b, lora, small)

    core = ci.astype(jnp.int32)
    gi = jnp.transpose(d_w_in.reshape(D_MODEL, N_CHIPS, IN_SHARD), (1, 0, 2))
    go = d_w_out.reshape(N_CHIPS, OUT_SHARD, D_MODEL)
    gs = _pack_small([d_small[n] for n in SMALL_NAMES], loss, d_lora)
    p_in, p_out, p_small = _pair_exchange(gi, go, gs)
    c_in, c_out, c_small = _pair_sum(gi, go, gs, p_in, p_out, p_small, core.reshape(1))
    l_in, l_out, l_small = _chip_exchange(c_in, c_out, c_small)
    r_in, r_out, tot = _chip_sum(gi, go, p_in, p_out, c_small, l_in, l_out, l_small, jnp.stack([chip, core]))
    grad_w_in, grad_w_out = _pair_share(r_in, r_out)

    d_in, nm_in, nv_in = _adamw("adamw_w_in", w_in[0], grad_w_in, m_w_in[0], v_w_in[0], 256)
    d_out, nm_out, nv_out = _adamw("adamw_w_out", w_out[0], grad_w_out, m_w_out[0], v_w_out[0], OUT_SHARD)
    par_names = SMALL_NAMES + ("w_lora_up", "a_lora_up")

    def operands(tree, lw_, la_):
        return [tree[n].reshape(1, -1) for n in SMALL_NAMES] + [lw_[0], la_[0]]

    res = _adamw_small(tot, chip.reshape(1), operands(params, w_lora_up, a_lora_up),
                       operands(moments_m, m_w_lora_up, m_a_lora_up), operands(moments_v, v_w_lora_up, v_a_lora_up))

    names = ("norm_g", "w_in", "ret_gn_g", "rwkv_mu", "w_lora_up", "w0", "a_lora_up", "a0", "k_k", "k_a", "r_k",
             "rwkv_gn_g", "rwkv_gn_b", "w_out", "final_norm_g")
    shapes = dict(w_in=w_in.shape, w_out=w_out.shape, w_lora_up=w_lora_up.shape, a_lora_up=a_lora_up.shape,
                  **{n: params[n].shape for n in SMALL_NAMES})

    def leaves(pars, big_in, big_out):
        tree = dict(zip(par_names, pars), w_in=big_in, w_out=big_out)
        return [tree[n].reshape(shapes[n]) for n in names]

    grads = leaves(res[0], grad_w_in, grad_w_out)
    deltas = leaves(res[1], d_in, d_out)
    new_m = leaves(res[2], nm_in, nm_out)
    new_v = leaves(res[3], nv_in, nv_out)
    return (tot[PACK_LOSS, 0], grad_x.reshape(x.shape), *grads, *deltas, *new_m, *new_v)
```

```python
import functools

import numpy as np
import jax
import jax.numpy as jnp
from jax import lax
from jax.experimental import pallas as pl
from jax.experimental.pallas import tpu as pltpu

F32 = jnp.float32
BF16 = jnp.bfloat16
HIGHEST = lax.Precision.HIGHEST
MESH = pl.DeviceIdType.MESH

D_MODEL = 1024
N_CHIPS = 4
RET_HEADS = 4
RET_DK = 64
RET_DV = 128
RET_QK = RET_HEADS * RET_DK
RET_WIDTH = RET_HEADS * RET_DV
RET_COLS = 2 * RET_QK + 2 * RET_WIDTH
RET_CHUNK = 64
RW_WIDTH = 512
RW_HEAD = 64
RW_HEADS = 8
LORA = 64
RW_COLS = 4 * RW_WIDTH + 2 * LORA
IN_COLS = RET_COLS + RW_COLS
IN_SHARD = IN_COLS // N_CHIPS
OUT_SHARD = D_MODEL // N_CHIPS
ROPE_BASE = 10000.0
RMS_EPS = 1e-6
RET_GN_EPS = 1e-5
RW_GN_EPS = 64e-5
WKV_CHUNK = 16
N_VEC = 5

ADAM_LR = 0.001
ADAM_B1 = 0.9
ADAM_B2 = 0.999
ADAM_EPS = 1e-08
ADAM_WD = 0.01
ADAM_STEP = 10

VMEM_LIMIT = 56 * 1024 * 1024

PACK_W = 512
SMALL_NAMES = ("norm_g", "ret_gn_g", "rwkv_mu", "w0", "a0", "k_k", "k_a", "r_k", "rwkv_gn_g", "rwkv_gn_b",
               "final_norm_g")
SMALL_SIZES = (1024, 512, 2176, 512, 512, 512, 512, 512, 512, 512, 1024)
PACK_LORA_W = 0
PACK_LORA_A = LORA
PACK_SMALL = 2 * LORA


def _pack_layout():
    rows, at = {}, PACK_SMALL
    for name, n in zip(SMALL_NAMES, SMALL_SIZES):
        rows[name] = at
        at += -(-n // PACK_W)
    return rows, at


PACK_AT, PACK_LOSS = _pack_layout()
PACK_ROWS = -(-(PACK_LOSS + 1) // 8) * 8


def _cparams(**kw):
    return pltpu.CompilerParams(vmem_limit_bytes=VMEM_LIMIT, **kw)


def _dot(a, b, precision=None):
    return jnp.dot(a, b, precision=precision, preferred_element_type=F32)


def _dot_nt(a, b, precision=None):
    return lax.dot_general(a, b, (((1,), (1,)), ((), ())), precision=precision, preferred_element_type=F32)


def _dot_tn(a, b, precision=None):
    return lax.dot_general(a, b, (((0,), (0,)), ((), ())), precision=precision, preferred_element_type=F32)


def _split(x):
    hi = x.astype(BF16)
    lo = (x - hi.astype(F32)).astype(BF16)
    return hi, lo


@jax.custom_vjp
def _segsum(x, seg):
    hi, lo = _split(x)
    return _dot(hi, seg) + _dot(lo, seg)


def _segsum_fwd(x, seg):
    return _segsum(x, seg), seg


def _segsum_bwd(seg, ct):
    return _segsum(ct, seg), jnp.zeros_like(seg)


_segsum.defvjp(_segsum_fwd, _segsum_bwd)


def _softplus(z):
    return jnp.maximum(z, 0.0) + jnp.log(1.0 + jnp.exp(-jnp.abs(z)))


def _full(shape):
    nd = len(shape)
    return pl.BlockSpec(shape, lambda *_: (0,) * nd)


def _rope_tables(T):
    half = RET_DK // 2
    expo = -jnp.arange(half, dtype=F32) / jnp.float32(half)
    freqs = jnp.exp(expo * jnp.float32(np.log(ROPE_BASE)))
    ang = jnp.arange(T, dtype=jnp.int32).astype(F32)[:, None] * freqs[None, :]
    cos = jnp.tile(jnp.cos(ang), (1, 2 * RET_HEADS))
    sin = jnp.tile(jnp.sin(ang), (1, 2 * RET_HEADS))
    return cos, sin


def _ret_tables():
    H, C = RET_HEADS, RET_CHUNK
    hidx = jnp.arange(H, dtype=F32)
    lg = jnp.log(1.0 - jnp.exp2(-5.0 - hidx))
    idx = jnp.arange(C, dtype=F32)
    intra = jnp.exp(lg[:, None, None] * jnp.abs(idx[:, None] - idx[None, :]))
    q_dec = jnp.transpose(jnp.exp(lg[:, None] * (idx[None, :] + 1.0)))
    k_dec = jnp.transpose(jnp.exp(lg[:, None] * (C - 1.0 - idx[None, :])))
    chunk_dec = jnp.exp(lg * C)
    qd = jnp.repeat(q_dec, RET_DK, axis=1)
    kd = jnp.repeat(k_dec, RET_DK, axis=1)
    row_h = np.arange(RET_QK) // RET_DK
    col_h = np.arange(RET_WIDTH) // RET_DV
    bm = jnp.asarray((row_h[:, None] == col_h[None, :]).astype(np.float32))
    cd = bm * jnp.repeat(chunk_dec, RET_DK)[:, None]
    return intra, qd, kd, cd, bm


def _seg_matrix(width, head):
    h = np.arange(width) // head
    return jnp.asarray((h[:, None] == h[None, :]).astype(np.float32), dtype=BF16)


def _wkv_expand_table():
    Tc = WKV_CHUNK
    k = np.arange(2 * RW_HEADS * Tc)
    kh, kt = (k % (RW_HEADS * Tc)) // Tc, k % Tc
    nh = np.arange(RW_WIDTH) // RW_HEAD
    e = (kh[None, :, None] == nh[None, None, :]) & (kt[None, :, None] == np.arange(Tc)[:, None, None])
    return jnp.asarray(e.astype(np.float32), dtype=BF16)


def _wkv_reduce_table():
    Tc = WKV_CHUNK
    kh = np.arange(RW_WIDTH) // RW_HEAD
    n = np.arange(RW_HEADS * Tc)
    nh, nt = n // Tc, n % Tc
    r = (kh[None, :, None] == nh[None, None, :]) & (nt[None, None, :] == np.arange(Tc)[:, None, None])
    return jnp.asarray(r.astype(np.float32), dtype=BF16)


def _inproj_fwd(x, norm_g, w_b):
    T = x.shape[0]
    tm = min(T, 256)

    def body(x_ref, g_ref, w_ref, pret_ref, prw_ref, u_ref):
        xf = x_ref[...]
        rstd = lax.rsqrt(jnp.mean(xf * xf, axis=-1, keepdims=True) + RMS_EPS)
        ub = ((xf * rstd) * g_ref[...]).astype(BF16)
        u_ref[...] = ub
        pret_ref[...] = _dot(ub, w_ref[:, :RET_COLS])
        prw_ref[...] = _dot(ub, w_ref[:, RET_COLS:])

    return pl.pallas_call(
        body, name="inproj_fwd", grid=(T // tm,),
        in_specs=[pl.BlockSpec((tm, D_MODEL), lambda i: (i, 0)), _full((1, D_MODEL)), _full((D_MODEL, IN_COLS))],
        out_specs=[pl.BlockSpec((tm, RET_COLS), lambda i: (i, 0)), pl.BlockSpec((tm, RW_COLS), lambda i: (i, 0)),
                   pl.BlockSpec((tm, D_MODEL), lambda i: (i, 0))],
        out_shape=[jax.ShapeDtypeStruct((T, RET_COLS), F32), jax.ShapeDtypeStruct((T, RW_COLS), F32),
                   jax.ShapeDtypeStruct((T, D_MODEL), BF16)],
        compiler_params=_cparams(dimension_semantics=("arbitrary",)),
    )(x, norm_g, w_b)


def _rot_half(x):
    n = x.shape[1]
    lane = lax.broadcasted_iota(jnp.int32, x.shape, 1)
    first = (lane % RET_DK) < (RET_DK // 2)
    return jnp.where(first, -pltpu.roll(x, n - RET_DK // 2, 1), pltpu.roll(x, RET_DK // 2, 1))


def _rope(x, cos, sin):
    return x * cos + _rot_half(x) * sin


def _rope_bwd(d, cos, sin):
    return d * cos - _rot_half(d * sin)


def _ret_post(ret, g, gn_g, seg):
    mu = _segsum(ret, seg) * (1.0 / RET_DV)
    xc = ret - mu
    var = _segsum(xc * xc, seg) * (1.0 / RET_DV)
    n = xc * lax.rsqrt(var + RET_GN_EPS)
    return (g * jax.nn.sigmoid(g)) * (n * gn_g)


def _ret_scores(qt, kt, d_ref, h):
    lane = lax.broadcasted_iota(jnp.int32, qt.shape, 1)
    qh = jnp.where(lane // RET_DK == h, qt, 0.0)
    return qh, _dot_nt(qh, kt, HIGHEST) * d_ref[h]


def _ret_fwd(p_ret, cos, sin, tabs, gn_g, seg128):
    T = p_ret.shape[0]
    C = RET_CHUNK
    nch = T // C
    intra_d, qd, kd, cd, bm = tabs

    def body(q_ref, k_ref, v_ref, g_ref, cos_ref, sin_ref, qd_ref, kd_ref, d_ref, cd_ref, bm_ref, gn_ref, seg_ref,
             y_ref, ret_ref, sin_out_ref, s_ref):
        @pl.when(pl.program_id(0) == 0)
        def _():
            s_ref[...] = jnp.zeros_like(s_ref)

        cosv, sinv = cos_ref[...], sin_ref[...]
        qt = _rope(q_ref[...], cosv, sinv)
        kt = _rope(k_ref[...], cosv, sinv) * (RET_DK ** -0.5)
        v = v_ref[...]
        s_in = s_ref[...]
        sin_out_ref[0] = s_in
        inter = _dot(qt * qd_ref[...], s_in, HIGHEST)
        intra = []
        for h in range(RET_HEADS):
            _, a = _ret_scores(qt, kt, d_ref, h)
            intra.append(_dot(a, v[:, h * RET_DV:(h + 1) * RET_DV], HIGHEST))
        ret = jnp.concatenate(intra, axis=1) + inter
        kv = _dot_tn(kt * kd_ref[...], v, HIGHEST)
        s_ref[...] = s_in * cd_ref[...] + kv * bm_ref[...]
        ret_ref[...] = ret
        y_ref[...] = _ret_post(ret, g_ref[...], gn_ref[...], seg_ref[...]).astype(BF16)

    return pl.pallas_call(
        body, name="ret_fwd", grid=(nch,),
        in_specs=[pl.BlockSpec((C, RET_QK), lambda c: (c, 0)), pl.BlockSpec((C, RET_QK), lambda c: (c, 1)),
                  pl.BlockSpec((C, RET_WIDTH), lambda c: (c, 1)), pl.BlockSpec((C, RET_WIDTH), lambda c: (c, 2)),
                  pl.BlockSpec((C, RET_QK), lambda c: (c, 0)), pl.BlockSpec((C, RET_QK), lambda c: (c, 0)),
                  _full((C, RET_QK)), _full((C, RET_QK)), _full((RET_HEADS, C, C)),
                  _full((RET_QK, RET_WIDTH)), _full((RET_QK, RET_WIDTH)), _full((1, RET_WIDTH)),
                  _full((RET_WIDTH, RET_WIDTH))],
        out_specs=[pl.BlockSpec((C, RET_WIDTH), lambda c: (c, 0)), pl.BlockSpec((C, RET_WIDTH), lambda c: (c, 0)),
                   pl.BlockSpec((1, RET_QK, RET_WIDTH), lambda c: (c, 0, 0))],
        out_shape=[jax.ShapeDtypeStruct((T, RET_WIDTH), BF16), jax.ShapeDtypeStruct((T, RET_WIDTH), F32),
                   jax.ShapeDtypeStruct((nch, RET_QK, RET_WIDTH), F32)],
        scratch_shapes=[pltpu.VMEM((RET_QK, RET_WIDTH), F32)],
        compiler_params=_cparams(dimension_semantics=("arbitrary",)),
    )(p_ret, p_ret, p_ret, p_ret, cos, sin, qd, kd, intra_d, cd, bm, gn_g, seg128)


def _ret_bwd(p_ret, cos, sin, tabs, gn_g, seg128, ret, s_in_all, dy):
    T = p_ret.shape[0]
    C = RET_CHUNK
    nch = T // C
    intra_d, qd, kd, cd, bm = tabs

    def rev(j):
        return lambda c: (nch - 1 - c, j)

    def body(q_ref, k_ref, v_ref, g_ref, cos_ref, sin_ref, qd_ref, kd_ref, d_ref, cd_ref, bm_ref, gn_ref, seg_ref,
             ret_ref, sin_ref_, dy_ref, dp_ref, dgn_ref, ds_ref):
        @pl.when(pl.program_id(0) == 0)
        def _():
            ds_ref[...] = jnp.zeros_like(ds_ref)
            dgn_ref[...] = jnp.zeros_like(dgn_ref)

        cosv, sinv = cos_ref[...], sin_ref[...]
        qt = _rope(q_ref[...], cosv, sinv)
        kt = _rope(k_ref[...], cosv, sinv) * (RET_DK ** -0.5)
        v = v_ref[...]
        s_in = sin_ref_[0]
        ds_out = ds_ref[...]
        seg = seg_ref[...]

        _, post_vjp = jax.vjp(lambda r_, g_, gn_: _ret_post(r_, g_, gn_, seg), ret_ref[...], g_ref[...], gn_ref[...])
        dret, dg, dgn = post_vjp(dy_ref[...])
        dgn_ref[...] += dgn

        qdv, kdv = qd_ref[...], kd_ref[...]
        dqt = qdv * _dot_nt(dret, s_in, HIGHEST)
        dkt = kdv * _dot_nt(v, ds_out, HIGHEST)
        dv_all = _dot(kt * kdv, ds_out, HIGHEST)
        dvs = []
        for h in range(RET_HEADS):
            sl = slice(h * RET_DV, (h + 1) * RET_DV)
            qh, a = _ret_scores(qt, kt, d_ref, h)
            lane = lax.broadcasted_iota(jnp.int32, kt.shape, 1)
            kh = jnp.where(lane // RET_DK == h, kt, 0.0)
            da = _dot_nt(dret[:, sl], v[:, sl], HIGHEST) * d_ref[h]
            dvs.append(_dot_tn(a, dret[:, sl], HIGHEST))
            dqt = dqt + _dot(da, kh, HIGHEST)
            dkt = dkt + _dot_tn(da, qh, HIGHEST)
        dv = dv_all + jnp.concatenate(dvs, axis=1)
        ds_ref[...] = ds_out * cd_ref[...] + _dot_tn(qt * qdv, dret, HIGHEST) * bm_ref[...]
        dq = _rope_bwd(dqt, cosv, sinv)
        dk = _rope_bwd(dkt * (RET_DK ** -0.5), cosv, sinv)
        dp_ref[...] = jnp.concatenate([dq, dk, dv, dg], axis=1)

    return pl.pallas_call(
        body, name="ret_bwd", grid=(nch,),
        in_specs=[pl.BlockSpec((C, RET_QK), rev(0)), pl.BlockSpec((C, RET_QK), rev(1)),
                  pl.BlockSpec((C, RET_WIDTH), rev(1)), pl.BlockSpec((C, RET_WIDTH), rev(2)),
                  pl.BlockSpec((C, RET_QK), rev(0)), pl.BlockSpec((C, RET_QK), rev(0)),
                  _full((C, RET_QK)), _full((C, RET_QK)), _full((RET_HEADS, C, C)),
                  _full((RET_QK, RET_WIDTH)), _full((RET_QK, RET_WIDTH)), _full((1, RET_WIDTH)),
                  _full((RET_WIDTH, RET_WIDTH)),
                  pl.BlockSpec((C, RET_WIDTH), rev(0)),
                  pl.BlockSpec((1, RET_QK, RET_WIDTH), lambda c: (nch - 1 - c, 0, 0)),
                  pl.BlockSpec((C, RET_WIDTH), rev(0))],
        out_specs=[pl.BlockSpec((C, RET_COLS), rev(0)), _full((1, RET_WIDTH))],
        out_shape=[jax.ShapeDtypeStruct((T, RET_COLS), F32), jax.ShapeDtypeStruct((1, RET_WIDTH), F32)],
        scratch_shapes=[pltpu.VMEM((RET_QK, RET_WIDTH), F32)],
        compiler_params=_cparams(dimension_semantics=("arbitrary",)),
    )(p_ret, p_ret, p_ret, p_ret, cos, sin, qd, kd, intra_d, cd, bm, gn_g, seg128, ret, s_in_all, dy)


def _prep_fn(p, prev, mu, w0, a0, k_k, k_a, lora, seg):
    W = RW_WIDTH
    ps = p + mu * (prev - p)
    r, kr, vr, g = ps[:, 0:W], ps[:, W:2 * W], ps[:, 2 * W:3 * W], ps[:, 3 * W:4 * W]
    z = ps[:, 4 * W:]
    lane = lax.broadcasted_iota(jnp.int32, z.shape, 1)
    z = jnp.where(lane < LORA, jnp.tanh(z), z)
    lo = _dot(z, lora, HIGHEST)
    w_log = -_softplus(-(w0 + lo[:, :W])) - 0.5
    decay = jnp.exp(-jnp.exp(w_log))
    a = jax.nn.sigmoid(a0 + lo[:, W:])
    kk = kr * k_k
    kk = kk / jnp.maximum(jnp.sqrt(_segsum(kk * kk, seg)), 1e-12)
    k = kr * (1.0 + (a - 1.0) * k_a)
    return kk, decay, kk * a, k, r, vr, g


def _post_fn(o, r, k, v, g, gn_g, gn_b, r_k, seg):
    mu = _segsum(o, seg) * (1.0 / RW_HEAD)
    oc = o - mu
    var = _segsum(oc * oc, seg) * (1.0 / RW_HEAD)
    on = oc * lax.rsqrt(var + RW_GN_EPS) * gn_g + gn_b
    bonus = _segsum(r * k * r_k, seg) * v
    return (g * jax.nn.sigmoid(g)) * (on + bonus)


def _shift_down(p, first_row):
    row = lax.broadcasted_iota(jnp.int32, p.shape, 0)
    return jnp.where(row == 0, first_row, pltpu.roll(p, 1, 0))


def _shift_up(p, last_row):
    n = p.shape[0]
    row = lax.broadcasted_iota(jnp.int32, p.shape, 0)
    return jnp.where(row == n - 1, last_row, pltpu.roll(p, n - 1, 0))


def _row_tile(T):
    return min(T, 256)


def _prep_fwd(p_rw, bnd, mu, w0, a0, k_k, k_a, lora, seg64):
    T = p_rw.shape[0]
    tm = _row_tile(T)
    W = RW_WIDTH

    def body(p_ref, bnd_ref, mu_ref, w0_ref, a0_ref, kk_ref, ka_ref, lora_ref, seg_ref, vecs_ref, v_ref, g_ref):
        p = p_ref[...]
        prev = _shift_down(p, bnd_ref[0])
        res = _prep_fn(p, prev, mu_ref[...], w0_ref[...], a0_ref[...], kk_ref[...], ka_ref[...], lora_ref[...],
                       seg_ref[...])
        for j in range(N_VEC):
            vecs_ref[j] = res[j]
        v_ref[...] = res[N_VEC]
        g_ref[...] = res[N_VEC + 1]

    small = _full((1, W))
    row = pl.BlockSpec((tm, W), lambda i: (i, 0))
    return pl.pallas_call(
        body, name="rwkv_prep_fwd", grid=(T // tm,),
        in_specs=[pl.BlockSpec((tm, RW_COLS), lambda i: (i, 0)), pl.BlockSpec((1, 1, RW_COLS), lambda i: (i, 0, 0)),
                  _full((1, RW_COLS)), small, small, small, small, _full((2 * LORA, 2 * W)), _full((W, W))],
        out_specs=[pl.BlockSpec((N_VEC, tm, W), lambda i: (0, i, 0)), row, row],
        out_shape=[jax.ShapeDtypeStruct((N_VEC, T, W), F32), jax.ShapeDtypeStruct((T, W), F32),
                   jax.ShapeDtypeStruct((T, W), F32)],
        compiler_params=_cparams(dimension_semantics=("arbitrary",)),
    )(p_rw, bnd, mu, w0, a0, k_k, k_a, lora, seg64)


def _prep_bwd(p_rw, bnd, mu, w0, a0, k_k, k_a, lora, seg64, cts):
    T = p_rw.shape[0]
    tm = _row_tile(T)
    W = RW_WIDTH

    def body(p_ref, bnd_ref, mu_ref, w0_ref, a0_ref, kk_ref, ka_ref, lora_ref, seg_ref,
             dvecs_ref, dk2_ref, dr2_ref, dv1_ref, dv2_ref, dg_ref,
             dp_ref, dprev_ref, dmu_ref, dw0_ref, da0_ref, dkk_p_ref, dka_ref, dlora_ref):
        accs = (dmu_ref, dw0_ref, da0_ref, dkk_p_ref, dka_ref, dlora_ref)

        @pl.when(pl.program_id(0) == 0)
        def _():
            for a_ref in accs:
                a_ref[...] = jnp.zeros_like(a_ref)

        p = p_ref[...]
        prev = _shift_down(p, bnd_ref[0])
        seg = seg_ref[...]
        _, vjp = jax.vjp(lambda *a: _prep_fn(*a, seg), p, prev, mu_ref[...], w0_ref[...], a0_ref[...], kk_ref[...],
                         ka_ref[...], lora_ref[...])
        ct = (dvecs_ref[0], dvecs_ref[1], dvecs_ref[2], dvecs_ref[3] + dk2_ref[...], dvecs_ref[4] + dr2_ref[...],
              dv1_ref[...] + dv2_ref[...], dg_ref[...])
        grads = vjp(ct)
        dp_ref[...] = grads[0]
        dprev_ref[...] = grads[1]
        for a_ref, gval in zip(accs, grads[2:]):
            a_ref[...] += gval

    small = _full((1, W))
    row = pl.BlockSpec((tm, W), lambda i: (i, 0))
    return pl.pallas_call(
        body, name="rwkv_prep_bwd", grid=(T // tm,),
        in_specs=[pl.BlockSpec((tm, RW_COLS), lambda i: (i, 0)), pl.BlockSpec((1, 1, RW_COLS), lambda i: (i, 0, 0)),
                  _full((1, RW_COLS)), small, small, small, small, _full((2 * LORA, 2 * W)), _full((W, W)),
                  pl.BlockSpec((N_VEC, tm, W), lambda i: (0, i, 0))] + [row] * 5,
        out_specs=[pl.BlockSpec((tm, RW_COLS), lambda i: (i, 0)), pl.BlockSpec((tm, RW_COLS), lambda i: (i, 0)),
                   _full((1, RW_COLS)), small, small, small, small, _full((2 * LORA, 2 * W))],
        out_shape=[jax.ShapeDtypeStruct((T, RW_COLS), F32), jax.ShapeDtypeStruct((T, RW_COLS), F32),
                   jax.ShapeDtypeStruct((1, RW_COLS), F32)] + [jax.ShapeDtypeStruct((1, W), F32)] * 4
                  + [jax.ShapeDtypeStruct((2 * LORA, 2 * W), F32)],
        compiler_params=_cparams(dimension_semantics=("arbitrary",)),
    )(p_rw, bnd, mu, w0, a0, k_k, k_a, lora, seg64, *cts)


def _vec_spec(j, tm):
    return pl.BlockSpec((None, tm, RW_WIDTH), lambda i: (j, i, 0))


def _post_fwd(o, vecs, v, g, gn_g, gn_b, r_k, seg64):
    T = o.shape[0]
    tm = _row_tile(T)
    W = RW_WIDTH

    def body(o_ref, r_ref, k_ref, v_ref, g_ref, gg_ref, gb_ref, rk_ref, seg_ref, y_ref):
        y_ref[...] = _post_fn(o_ref[...], r_ref[...], k_ref[...], v_ref[...], g_ref[...], gg_ref[...], gb_ref[...],
                              rk_ref[...], seg_ref[...]).astype(BF16)

    row = pl.BlockSpec((tm, W), lambda i: (i, 0))
    small = _full((1, W))
    return pl.pallas_call(
        body, name="rwkv_post_fwd", grid=(T // tm,),
        in_specs=[row, _vec_spec(4, tm), _vec_spec(3, tm), row, row] + [small] * 3 + [_full((W, W))],
        out_specs=row, out_shape=jax.ShapeDtypeStruct((T, W), BF16),
        compiler_params=_cparams(dimension_semantics=("arbitrary",)),
    )(o, vecs, vecs, v, g, gn_g, gn_b, r_k, seg64)


def _post_bwd(o, vecs, v, g, gn_g, gn_b, r_k, seg64, dy):
    T = o.shape[0]
    tm = _row_tile(T)
    W = RW_WIDTH

    def body(o_ref, r_ref, k_ref, v_ref, g_ref, gg_ref, gb_ref, rk_ref, seg_ref, dy_ref,
             do_ref, dr_ref, dk_ref, dv_ref, dg_ref, dgg_ref, dgb_ref, drk_ref):
        accs = (dgg_ref, dgb_ref, drk_ref)

        @pl.when(pl.program_id(0) == 0)
        def _():
            for a_ref in accs:
                a_ref[...] = jnp.zeros_like(a_ref)

        seg = seg_ref[...]
        _, vjp = jax.vjp(lambda *a: _post_fn(*a, seg), o_ref[...], r_ref[...], k_ref[...], v_ref[...], g_ref[...],
                         gg_ref[...], gb_ref[...], rk_ref[...])
        grads = vjp(dy_ref[...])
        for o_, gval in zip((do_ref, dr_ref, dk_ref, dv_ref, dg_ref), grads[:5]):
            o_[...] = gval
        for a_ref, gval in zip(accs, grads[5:]):
            a_ref[...] += gval

    row = pl.BlockSpec((tm, W), lambda i: (i, 0))
    small = _full((1, W))
    return pl.pallas_call(
        body, name="rwkv_post_bwd", grid=(T // tm,),
        in_specs=[row, _vec_spec(4, tm), _vec_spec(3, tm), row, row] + [small] * 3
                 + [_full((W, W)), pl.BlockSpec((tm, W), lambda i: (i, 1))],
        out_specs=[row] * 5 + [small] * 3,
        out_shape=[jax.ShapeDtypeStruct((T, W), F32)] * 5 + [jax.ShapeDtypeStruct((1, W), F32)] * 3,
        compiler_params=_cparams(dimension_semantics=("arbitrary",)),
    )(o, vecs, vecs, v, g, gn_g, gn_b, r_k, seg64, dy)


def _to_cols(vecs, T):
    Tc = WKV_CHUNK
    x = vecs.reshape(N_VEC, T // Tc, Tc, RW_HEADS, RW_HEAD)
    x = jnp.transpose(x, (1, 0, 4, 3, 2))
    return x.reshape(T // Tc, N_VEC * RW_HEAD, RW_HEADS * Tc)


def _from_cols(cols, T):
    Tc = WKV_CHUNK
    x = cols.reshape(T // Tc, N_VEC, RW_HEAD, RW_HEADS, Tc)
    return jnp.transpose(x, (1, 0, 4, 3, 2)).reshape(N_VEC, T, RW_WIDTH)


def _wkv_lhs(cols_ref):
    hi, lo = _split(cols_ref[0])
    return jnp.concatenate([hi, lo], axis=1)


def _wkv_step(st, kk_e, w_e, b_e, k_e, v_row):
    sa = -jnp.sum(st * kk_e, axis=0, keepdims=True)
    return st * w_e + b_e * sa + k_e * v_row, sa


def _wkv_fwd(cols, v, e_tab):
    T = v.shape[0]
    Tc = WKV_CHUNK
    nch = T // Tc
    J, W = RW_HEAD, RW_WIDTH

    def body(cols_ref, v_ref, e_ref, o_ref, states_ref, sa_ref, last_ref, s_ref):
        @pl.when(pl.program_id(0) == 0)
        def _():
            s_ref[...] = jnp.zeros_like(s_ref)

        lhs = _wkv_lhs(cols_ref)
        st = s_ref[...]
        for t in range(Tc):
            ex = _dot(lhs, e_ref[t])
            states_ref[t] = st
            st, sa = _wkv_step(st, ex[0:J], ex[J:2 * J], ex[2 * J:3 * J], ex[3 * J:4 * J], v_ref[t:t + 1, :])
            sa_ref[t:t + 1, :] = sa
            o_ref[t:t + 1, :] = jnp.sum(st * ex[4 * J:5 * J], axis=0, keepdims=True)
        s_ref[...] = st
        last_ref[...] = st

    return pl.pallas_call(
        body, name="wkv_fwd", grid=(nch,),
        in_specs=[pl.BlockSpec((1, N_VEC * J, 128), lambda c: (c, 0, 0)), pl.BlockSpec((Tc, W), lambda c: (c, 0)),
                  _full((Tc, 2 * 128, W))],
        out_specs=[pl.BlockSpec((Tc, W), lambda c: (c, 0)), pl.BlockSpec((Tc, J, W), lambda c: (c, 0, 0)),
                   pl.BlockSpec((Tc, W), lambda c: (c, 0)), _full((J, W))],
        out_shape=[jax.ShapeDtypeStruct((T, W), F32), jax.ShapeDtypeStruct((T, J, W), F32),
                   jax.ShapeDtypeStruct((T, W), F32), jax.ShapeDtypeStruct((J, W), F32)],
        scratch_shapes=[pltpu.VMEM((J, W), F32)],
        compiler_params=_cparams(dimension_semantics=("arbitrary",)),
    )(cols, v, e_tab)


def _wkv_bwd(cols, v, do, states, sa, s_last, e_tab, r_tab):
    T = v.shape[0]
    Tc = WKV_CHUNK
    nch = T // Tc
    J, W = RW_HEAD, RW_WIDTH
    blocks = [slice(b * 128, (b + 1) * 128) for b in range(W // 128)]

    def body(cols_ref, v_ref, do_ref, states_ref, sa_ref, last_ref, e_ref, r_ref, dv_ref, dcols_ref, ds_ref, next_ref):
        @pl.when(pl.program_id(0) == 0)
        def _():
            ds_ref[...] = jnp.zeros_like(ds_ref)
            next_ref[...] = last_ref[...]

        lhs = _wkv_lhs(cols_ref)
        dst = [ds_ref[:, b] for b in blocks]
        acc = jnp.zeros((N_VEC * J, 128), F32)
        for t in reversed(range(Tc)):
            ex = _dot(lhs, e_ref[t])
            dvs, prods = [], []
            for i, b in enumerate(blocks):
                kk_e, w_e, b_e, k_e, r_e = (ex[n * J:(n + 1) * J, b] for n in range(N_VEC))
                s_old = states_ref[t, :, b]
                s_new = states_ref[t + 1, :, b] if t + 1 < Tc else next_ref[:, b]
                do_row, v_row, sa_row = do_ref[t:t + 1, b], v_ref[t:t + 1, b], sa_ref[t:t + 1, b]
                dsn = dst[i] + r_e * do_row
                dsa = jnp.sum(dsn * b_e, axis=0, keepdims=True)
                dvs.append(jnp.sum(dsn * k_e, axis=0, keepdims=True))
                prods.append(jnp.concatenate(
                    [s_old * (-dsa), dsn * s_old, dsn * sa_row, dsn * v_row, s_new * do_row], axis=0).astype(BF16))
                dst[i] = dsn * w_e - kk_e * dsa
            dv_ref[t:t + 1, :] = jnp.concatenate(dvs, axis=1)
            acc = acc + _dot(jnp.concatenate(prods, axis=1), r_ref[t])
        for i, b in enumerate(blocks):
            ds_ref[:, b] = dst[i]
        next_ref[...] = states_ref[0]
        dcols_ref[0] = acc

    rev2 = lambda c: (nch - 1 - c, 0)
    rev3 = lambda c: (nch - 1 - c, 0, 0)
    return pl.pallas_call(
        body, name="wkv_bwd", grid=(nch,),
        in_specs=[pl.BlockSpec((1, N_VEC * J, 128), rev3), pl.BlockSpec((Tc, W), rev2), pl.BlockSpec((Tc, W), rev2),
                  pl.BlockSpec((Tc, J, W), rev3), pl.BlockSpec((Tc, W), rev2), _full((J, W)),
                  _full((Tc, 2 * 128, W)), _full((Tc, W, 128))],
        out_specs=[pl.BlockSpec((Tc, W), rev2), pl.BlockSpec((1, N_VEC * J, 128), rev3)],
        out_shape=[jax.ShapeDtypeStruct((T, W), F32), jax.ShapeDtypeStruct((nch, N_VEC * J, 128), F32)],
        scratch_shapes=[pltpu.VMEM((J, W), F32), pltpu.VMEM((J, W), F32)],
        compiler_params=_cparams(dimension_semantics=("arbitrary",)),
    )(cols, v, do, states, sa, s_last, e_tab, r_tab)


def _outproj(x, y_ret, y_rw, w_out_b, target, gf):
    T = x.shape[0]
    tm = _row_tile(T)
    W = RW_WIDTH

    def body(x_ref, yr_ref, yw_ref, w_ref, t_ref, gf_ref, loss_ref, dh_ref, dy_ref, dw_ref, dgf_ref):
        @pl.when(pl.program_id(0) == 0)
        def _():
            loss_ref[...] = jnp.zeros_like(loss_ref)
            dw_ref[...] = jnp.zeros_like(dw_ref)
            dgf_ref[...] = jnp.zeros_like(dgf_ref)

        y = jnp.concatenate([yr_ref[...], yw_ref[...]], axis=1)
        w = w_ref[...]
        h = x_ref[...] + _dot(y, w)
        rstd = lax.rsqrt(jnp.mean(h * h, axis=-1, keepdims=True) + RMS_EPS)
        hn = h * rstd
        gfv = gf_ref[...]
        err = hn * gfv - t_ref[...]
        loss_ref[...] += 0.5 * jnp.sum(jnp.mean(err * err, axis=-1))
        dout = err * (1.0 / D_MODEL)
        dgf_ref[...] += jnp.sum(dout * hn, axis=0, keepdims=True)
        dhn = dout * gfv
        dh = rstd * (dhn - hn * jnp.mean(dhn * hn, axis=-1, keepdims=True))
        dh_ref[...] = dh
        dhb = dh.astype(BF16)
        dy_ref[...] = _dot_nt(dhb, w)
        dw_ref[...] += _dot_tn(y, dhb)

    return pl.pallas_call(
        body, name="outproj_loss", grid=(T // tm,),
        in_specs=[pl.BlockSpec((tm, D_MODEL), lambda i: (i, 0)), pl.BlockSpec((tm, W), lambda i: (i, 0)),
                  pl.BlockSpec((tm, W), lambda i: (i, 0)), _full((D_MODEL, D_MODEL)),
                  pl.BlockSpec((tm, D_MODEL), lambda i: (i, 0)), _full((1, D_MODEL))],
        out_specs=[_full((1, PACK_W)), pl.BlockSpec((tm, D_MODEL), lambda i: (i, 0)),
                   pl.BlockSpec((tm, D_MODEL), lambda i: (i, 0)), _full((D_MODEL, D_MODEL)), _full((1, D_MODEL))],
        out_shape=[jax.ShapeDtypeStruct((1, PACK_W), F32), jax.ShapeDtypeStruct((T, D_MODEL), F32),
                   jax.ShapeDtypeStruct((T, D_MODEL), F32), jax.ShapeDtypeStruct((D_MODEL, D_MODEL), F32),
                   jax.ShapeDtypeStruct((1, D_MODEL), F32)],
        compiler_params=_cparams(dimension_semantics=("arbitrary",)),
    )(x, y_ret, y_rw, w_out_b, target, gf)


def _inproj_bwd_x(dp_ret, dp_rw, dprev, dbnd, w_b, x, norm_g, dh):
    T = x.shape[0]
    tm = _row_tile(T)

    def body(dpr_ref, dpw_ref, dprev_ref, dbnd_ref, w_ref, x_ref, g_ref, dh_ref, gx_ref, dg_ref, dpb_ref):
        @pl.when(pl.program_id(0) == 0)
        def _():
            dg_ref[...] = jnp.zeros_like(dg_ref)

        d_rw = dpw_ref[...] + _shift_up(dprev_ref[...], dbnd_ref[0])
        dpb = jnp.concatenate([dpr_ref[...].astype(BF16), d_rw.astype(BF16)], axis=1)
        dpb_ref[...] = dpb
        du = _dot_nt(dpb, w_ref[...])
        xf = x_ref[...]
        rstd = lax.rsqrt(jnp.mean(xf * xf, axis=-1, keepdims=True) + RMS_EPS)
        xn = xf * rstd
        dg_ref[...] += jnp.sum(du * xn, axis=0, keepdims=True)
        dxn = du * g_ref[...]
        gx_ref[...] = dh_ref[...] + rstd * (dxn - xn * jnp.mean(dxn * xn, axis=-1, keepdims=True))

    return pl.pallas_call(
        body, name="inproj_bwd_x", grid=(T // tm,),
        in_specs=[pl.BlockSpec((tm, RET_COLS), lambda i: (i, 0)), pl.BlockSpec((tm, RW_COLS), lambda i: (i, 0)),
                  pl.BlockSpec((tm, RW_COLS), lambda i: (i, 0)), pl.BlockSpec((1, 1, RW_COLS), lambda i: (i, 0, 0)),
                  _full((D_MODEL, IN_COLS)), pl.BlockSpec((tm, D_MODEL), lambda i: (i, 0)), _full((1, D_MODEL)),
                  pl.BlockSpec((tm, D_MODEL), lambda i: (i, 0))],
        out_specs=[pl.BlockSpec((tm, D_MODEL), lambda i: (i, 0)), _full((1, D_MODEL)),
                   pl.BlockSpec((tm, IN_COLS), lambda i: (i, 0))],
        out_shape=[jax.ShapeDtypeStruct((T, D_MODEL), F32), jax.ShapeDtypeStruct((1, D_MODEL), F32),
                   jax.ShapeDtypeStruct((T, IN_COLS), BF16)],
        compiler_params=_cparams(dimension_semantics=("arbitrary",)),
    )(dp_ret, dp_rw, dprev, dbnd, w_b, x, norm_g, dh)


def _inproj_bwd_w(u_t, dpb):
    T = u_t.shape[1]
    tr = 256

    def body(u_ref, d_ref, o_ref):
        o_ref[...] = _dot(u_ref[...], d_ref[...])

    return pl.pallas_call(
        body, name="inproj_bwd_w", grid=(D_MODEL // tr,),
        in_specs=[pl.BlockSpec((tr, T), lambda i: (i, 0)), _full((T, IN_COLS))],
        out_specs=pl.BlockSpec((tr, IN_COLS), lambda i: (i, 0)),
        out_shape=jax.ShapeDtypeStruct((D_MODEL, IN_COLS), F32),
        compiler_params=_cparams(dimension_semantics=("arbitrary",)),
    )(u_t, dpb)


def _tile_boundaries(a, tm, first):
    T, n = a.shape
    zero = jnp.zeros((1, n), a.dtype)
    if first:
        rows = jnp.concatenate([zero, a[tm - 1:T - 1:tm]], axis=0)
    else:
        rows = jnp.concatenate([a[tm:T:tm], zero], axis=0)
    return rows.reshape(T // tm, 1, n)


def _local_step(x, target, w_in_b, w_out_b, lora, small):
    T = x.shape[0]
    tm = _row_tile(T)
    cos, sin = _rope_tables(T)
    tabs = _ret_tables()
    seg128 = _seg_matrix(RET_WIDTH, RET_DV)
    seg64 = _seg_matrix(RW_WIDTH, RW_HEAD)
    e_tab = _wkv_expand_table()
    r_tab = _wkv_reduce_table()
    prep_w = (small["rwkv_mu"], small["w0"], small["a0"], small["k_k"], small["k_a"], lora, seg64)
    post_w = (small["rwkv_gn_g"], small["rwkv_gn_b"], small["r_k"], seg64)

    p_ret, p_rw, u = _inproj_fwd(x, small["norm_g"], w_in_b)
    y_ret, ret, s_in_all = _ret_fwd(p_ret, cos, sin, tabs, small["ret_gn_g"], seg128)
    bnd = _tile_boundaries(p_rw, tm, True)
    vecs, v, g = _prep_fwd(p_rw, bnd, *prep_w)
    cols = _to_cols(vecs, T)
    o, states, sa, s_last = _wkv_fwd(cols, v, e_tab)
    y_rw = _post_fwd(o, vecs, v, g, *post_w)
    loss, dh, dy, d_w_out, d_gf = _outproj(x, y_ret, y_rw, w_out_b, target, small["final_norm_g"])

    do, dr2, dk2, dv2, dg, d_gn_g, d_gn_b, d_r_k = _post_bwd(o, vecs, v, g, *post_w, dy)
    dv1, dcols = _wkv_bwd(cols, v, do, states, sa, s_last, e_tab, r_tab)
    dp_rw, dprev, d_mu, d_w0, d_a0, d_k_k, d_k_a, d_lora = _prep_bwd(
        p_rw, bnd, *prep_w, (_from_cols(dcols, T), dk2, dr2, dv1, dv2, dg))
    dp_ret, d_ret_gn = _ret_bwd(p_ret, cos, sin, tabs, small["ret_gn_g"], seg128, ret, s_in_all, dy)
    dbnd = _tile_boundaries(dprev, tm, False)
    grad_x, d_norm_g, dpb = _inproj_bwd_x(dp_ret, dp_rw, dprev, dbnd, w_in_b, x, small["norm_g"], dh)
    d_w_in = _inproj_bwd_w(jnp.transpose(u), dpb)

    d_small = {"norm_g": d_norm_g, "ret_gn_g": d_ret_gn, "rwkv_mu": d_mu, "w0": d_w0, "a0": d_a0, "k_k": d_k_k,
               "k_a": d_k_a, "r_k": d_r_k, "rwkv_gn_g": d_gn_g, "rwkv_gn_b": d_gn_b, "final_norm_g": d_gf}
    return loss, grad_x, d_w_in, d_w_out, d_lora, d_small


ANY = pl.BlockSpec(memory_space=pl.ANY)
CHIP_FLIPS = ((0, 1), (1, 0), (1, 1))
N_FLIPS = len(CHIP_FLIPS)
LORA_SHARD = RW_WIDTH // N_CHIPS
HALF_IN = D_MODEL // 2
HALF_OUT = OUT_SHARD // 2


def _position():
    return lax.axis_index("x"), lax.axis_index("y"), lax.axis_index("c")


def _flip(v, f):
    return 1 - v if f else v


def _finish(local, remote, landed):
    for cp in landed:
        cp.wait_recv()
    for cp in remote:
        cp.wait_send()
    for cp in local:
        cp.wait()


def _gather_chips(arrs):
    n = len(arrs)

    def body(*refs):
        ins, outs = refs[:n], refs[n:2 * n]
        send, recv, pass_send, pass_recv, loc = refs[2 * n:]
        x, y, c = _position()
        s = 2 * x + y
        sibling = (x, y, 1 - c)

        def copy(src, dst, sems, k, to):
            return pltpu.make_async_remote_copy(src_ref=src, dst_ref=dst, send_sem=sems[0].at[k], recv_sem=sems[1].at[k],
                                                device_id=to, device_id_type=MESH)

        local, remote, landed, passed, passed_in = [], [], [], [], []
        for a in range(n):
            local.append(pltpu.make_async_copy(ins[a], outs[a].at[s], loc.at[a]))
            for j, (fx, fy) in enumerate(CHIP_FLIPS):
                px, py = _flip(x, fx), _flip(y, fy)
                ps = 2 * px + py
                k = a * N_FLIPS + j
                remote.append(copy(ins[a].at[c], outs[a].at[s, c], (send, recv), k, (px, py, c)))
                landed.append(copy(ins[a].at[c], outs[a].at[ps, c], (send, recv), k, (px, py, c)))
                passed.append(copy(outs[a].at[ps, c], outs[a].at[ps, c], (pass_send, pass_recv), k, sibling))
                passed_in.append(copy(outs[a].at[ps, 1 - c], outs[a].at[ps, 1 - c], (pass_send, pass_recv), k, sibling))
        for cp in local + remote:
            cp.start()
        for arrived, onward in zip(landed, passed):
            arrived.wait_recv()
            onward.start()
        _finish(local, remote + passed, passed_in)

    sems = pltpu.SemaphoreType.DMA((n * N_FLIPS,))
    return pl.pallas_call(
        body, name="gather_weights",
        in_specs=[ANY] * n, out_specs=[ANY] * n,
        out_shape=[jax.ShapeDtypeStruct((N_CHIPS,) + a.shape, a.dtype) for a in arrs],
        scratch_shapes=[sems, sems, sems, sems, pltpu.SemaphoreType.DMA((n,))],
    )(*arrs)


def _pair_exchange(g_in, g_out, g_small):
    def body(gi_ref, go_ref, gs_ref, li_ref, lo_ref, ls_ref, send, recv):
        x, y, c = _position()
        peer = (x, y, 1 - c)
        srcs = (gi_ref.at[:, pl.ds((1 - c) * HALF_IN, HALF_IN), :], go_ref.at[:, pl.ds((1 - c) * HALF_OUT, HALF_OUT), :],
                gs_ref)
        remote = [pltpu.make_async_remote_copy(src_ref=src, dst_ref=dst, send_sem=send.at[k], recv_sem=recv.at[k],
                                               device_id=peer, device_id_type=MESH)
                  for k, (src, dst) in enumerate(zip(srcs, (li_ref, lo_ref, ls_ref)))]
        for cp in remote:
            cp.start()
        _finish([], remote, remote)

    return pl.pallas_call(
        body, name="pair_exchange",
        in_specs=[ANY] * 3, out_specs=[ANY] * 3,
        out_shape=[jax.ShapeDtypeStruct((N_CHIPS, HALF_IN, IN_SHARD), F32),
                   jax.ShapeDtypeStruct((N_CHIPS, HALF_OUT, D_MODEL), F32),
                   jax.ShapeDtypeStruct(g_small.shape, F32)],
        scratch_shapes=[pltpu.SemaphoreType.DMA((3,)), pltpu.SemaphoreType.DMA((3,))],
    )(g_in, g_out, g_small)


def _pair_sum(g_in, g_out, g_small, l_in, l_out, l_small, c_arr):
    tr = HALF_IN // 2

    def body(c_ref, gi_ref, go_ref, gs_ref, li_ref, lo_ref, ls_ref, ci_ref, co_ref, cs_ref):
        ci_ref[...] = (gi_ref[...] + li_ref[...]).astype(BF16)

        @pl.when(pl.program_id(1) == 0)
        def _():
            co_ref[...] = (go_ref[...] + lo_ref[...]).astype(BF16)

        @pl.when((pl.program_id(0) == 0) & (pl.program_id(1) == 0))
        def _():
            cs_ref[...] = gs_ref[...] + ls_ref[...]

    nd = g_small.shape
    return pl.pallas_call(
        body, name="pair_sum",
        grid_spec=pltpu.PrefetchScalarGridSpec(
            num_scalar_prefetch=1, grid=(N_CHIPS, 2),
            in_specs=[pl.BlockSpec((1, tr, IN_SHARD), lambda s, i, c: (s, 2 * c[0] + i, 0)),
                      pl.BlockSpec((1, HALF_OUT, D_MODEL), lambda s, i, c: (s, c[0], 0)),
                      pl.BlockSpec(nd, lambda s, i, c: (0, 0)),
                      pl.BlockSpec((1, tr, IN_SHARD), lambda s, i, c: (s, i, 0)),
                      pl.BlockSpec((1, HALF_OUT, D_MODEL), lambda s, i, c: (s, 0, 0)),
                      pl.BlockSpec(nd, lambda s, i, c: (0, 0))],
            out_specs=[pl.BlockSpec((1, tr, IN_SHARD), lambda s, i, c: (s, i, 0)),
                       pl.BlockSpec((1, HALF_OUT, D_MODEL), lambda s, i, c: (s, 0, 0)),
                       pl.BlockSpec(nd, lambda s, i, c: (0, 0))]),
        out_shape=[jax.ShapeDtypeStruct((N_CHIPS, HALF_IN, IN_SHARD), BF16),
                   jax.ShapeDtypeStruct((N_CHIPS, HALF_OUT, D_MODEL), BF16), jax.ShapeDtypeStruct(nd, F32)],
        compiler_params=_cparams(dimension_semantics=("arbitrary", "arbitrary")),
    )(c_arr, g_in, g_out, g_small, l_in, l_out, l_small)


def _chip_exchange(c_in, c_out, c_small):
    def body(ci_ref, co_ref, cs_ref, li_ref, lo_ref, ls_ref, send, recv):
        x, y, c = _position()
        s = 2 * x + y
        remote = []
        for j, (fx, fy) in enumerate(CHIP_FLIPS):
            px, py = _flip(x, fx), _flip(y, fy)
            ps = 2 * px + py
            for a, (src, dst) in enumerate(((ci_ref.at[ps], li_ref.at[j]), (co_ref.at[ps], lo_ref.at[j]),
                                            (cs_ref, ls_ref.at[j]))):
                k = 3 * j + a
                remote.append(pltpu.make_async_remote_copy(src_ref=src, dst_ref=dst, send_sem=send.at[k],
                                                           recv_sem=recv.at[k], device_id=(px, py, c),
                                                           device_id_type=MESH))
        for cp in remote:
            cp.start()
        _finish([], remote, remote)

    return pl.pallas_call(
        body, name="chip_exchange",
        in_specs=[ANY] * 3, out_specs=[ANY] * 3,
        out_shape=[jax.ShapeDtypeStruct((N_FLIPS, HALF_IN, IN_SHARD), c_in.dtype),
                   jax.ShapeDtypeStruct((N_FLIPS, HALF_OUT, D_MODEL), c_out.dtype),
                   jax.ShapeDtypeStruct((N_FLIPS,) + c_small.shape, F32)],
        scratch_shapes=[pltpu.SemaphoreType.DMA((3 * N_FLIPS,)), pltpu.SemaphoreType.DMA((3 * N_FLIPS,))],
    )(c_in, c_out, c_small)


def _chip_sum(g_in, g_out, p_in, p_out, c_small, l_in, l_out, l_small, sc_arr):
    tr = HALF_IN // 2
    nd = c_small.shape

    def body(s_ref, gi_ref, go_ref, pi_ref, po_ref, cs_ref, li0, li1, li2, lo0, lo1, lo2, ls_ref,
             ri_ref, ro_ref, rs_ref):
        ri_ref[...] = (((gi_ref[0] + pi_ref[0]) + li0[0].astype(F32)) + li1[0].astype(F32)) + li2[0].astype(F32)

        @pl.when(pl.program_id(0) == 0)
        def _():
            ro_ref[...] = (((go_ref[0] + po_ref[0]) + lo0[0].astype(F32)) + lo1[0].astype(F32)) + lo2[0].astype(F32)
            me = s_ref[0]
            parts = (cs_ref[...], ls_ref[0], ls_ref[1], ls_ref[2])

            def of_chip(s):
                m = jnp.bitwise_xor(me, s)
                return jnp.where(m == 0, parts[0], jnp.where(m == 1, parts[1], jnp.where(m == 2, parts[2], parts[3])))

            rs_ref[...] = ((of_chip(0) + of_chip(1)) + of_chip(2)) + of_chip(3)

    def flip_in(j):
        return pl.BlockSpec((1, tr, IN_SHARD), lambda i, s: (j, i, 0))

    def flip_out(j):
        return pl.BlockSpec((1, HALF_OUT, D_MODEL), lambda i, s: (j, 0, 0))

    return pl.pallas_call(
        body, name="chip_sum",
        grid_spec=pltpu.PrefetchScalarGridSpec(
            num_scalar_prefetch=1, grid=(2,),
            in_specs=[pl.BlockSpec((1, tr, IN_SHARD), lambda i, s: (s[0], 2 * s[1] + i, 0)),
                      pl.BlockSpec((1, HALF_OUT, D_MODEL), lambda i, s: (s[0], s[1], 0)),
                      pl.BlockSpec((1, tr, IN_SHARD), lambda i, s: (s[0], i, 0)),
                      pl.BlockSpec((1, HALF_OUT, D_MODEL), lambda i, s: (s[0], 0, 0)),
                      pl.BlockSpec(nd, lambda i, s: (0, 0)),
                      flip_in(0), flip_in(1), flip_in(2), flip_out(0), flip_out(1), flip_out(2),
                      pl.BlockSpec((N_FLIPS,) + nd, lambda i, s: (0, 0, 0))],
            out_specs=[pl.BlockSpec((tr, IN_SHARD), lambda i, s: (i, 0)),
                       pl.BlockSpec((HALF_OUT, D_MODEL), lambda i, s: (0, 0)),
                       pl.BlockSpec(nd, lambda i, s: (0, 0))]),
        out_shape=[jax.ShapeDtypeStruct((HALF_IN, IN_SHARD), F32), jax.ShapeDtypeStruct((HALF_OUT, D_MODEL), F32),
                   jax.ShapeDtypeStruct(nd, F32)],
        compiler_params=_cparams(dimension_semantics=("arbitrary",)),
    )(sc_arr, g_in, g_out, p_in, p_out, c_small, l_in, l_in, l_in, l_out, l_out, l_out, l_small)


def _pair_share(r_in, r_out):
    def body(ri_ref, ro_ref, li_ref, lo_ref, send, recv):
        x, y, c = _position()
        remote = [pltpu.make_async_remote_copy(src_ref=src, dst_ref=dst, send_sem=send.at[k], recv_sem=recv.at[k],
                                               device_id=(x, y, 1 - c), device_id_type=MESH)
                  for k, (src, dst) in enumerate(((ri_ref, li_ref), (ro_ref, lo_ref)))]
        for cp in remote:
            cp.start()
        _finish([], remote, remote)

    return pl.pallas_call(
        body, name="pair_share",
        in_specs=[ANY] * 2, out_specs=[ANY] * 2,
        out_shape=[jax.ShapeDtypeStruct(r_in.shape, F32), jax.ShapeDtypeStruct(r_out.shape, F32)],
        scratch_shapes=[pltpu.SemaphoreType.DMA((2,)), pltpu.SemaphoreType.DMA((2,))],
    )(r_in, r_out)


def _adam_update(w, g, m, v):
    mn = ADAM_B1 * m + (1.0 - ADAM_B1) * g
    vn = ADAM_B2 * v + (1.0 - ADAM_B2) * jnp.square(g)
    m_hat = mn / (1.0 - ADAM_B1 ** ADAM_STEP)
    v_hat = vn / (1.0 - ADAM_B2 ** ADAM_STEP)
    return -ADAM_LR * (m_hat / (jnp.sqrt(v_hat) + ADAM_EPS) + ADAM_WD * w), mn, vn


def _adamw(name, w, g_mine, g_theirs, m, v, core_arr, tr):
    rows, cols = w.shape
    per_half = rows // 2 // tr

    def body(c_ref, w_ref, gm_ref, gt_ref, m_ref, v_ref, g_ref, d_ref, nm_ref, nv_ref):
        mine = (pl.program_id(0) // per_half) == c_ref[0]
        g = jnp.where(mine, gm_ref[...], gt_ref[...])
        d, mn, vn = _adam_update(w_ref[...], g, m_ref[...], v_ref[...])
        g_ref[...] = g
        d_ref[...] = d
        nm_ref[...] = mn
        nv_ref[...] = vn

    spec = pl.BlockSpec((tr, cols), lambda i, c: (i, 0))
    half = pl.BlockSpec((tr, cols), lambda i, c: (i % per_half, 0))
    return pl.pallas_call(
        body, name=name,
        grid_spec=pltpu.PrefetchScalarGridSpec(
            num_scalar_prefetch=1, grid=(rows // tr,),
            in_specs=[spec, half, half, spec, spec], out_specs=[spec] * 4),
        out_shape=[jax.ShapeDtypeStruct((rows, cols), F32)] * 4,
        compiler_params=_cparams(dimension_semantics=("arbitrary",)),
    )(core_arr, w, g_mine, g_theirs, m, v)


def _row_pieces(n):
    return [(k, k * PACK_W, min(PACK_W, n - k * PACK_W)) for k in range(-(-n // PACK_W))]


def _pack_small(d_small, loss, d_lora):
    ns = len(SMALL_NAMES)

    def body(*refs):
        small_refs, (loss_ref, lora_ref, out_ref) = refs[:ns], refs[ns:]
        out_ref[...] = jnp.zeros_like(out_ref)
        out_ref[PACK_LORA_W:PACK_LORA_W + LORA, :] = lora_ref[:LORA, :RW_WIDTH]
        out_ref[PACK_LORA_A:PACK_LORA_A + LORA, :] = lora_ref[LORA:, RW_WIDTH:]
        for name, n, ref in zip(SMALL_NAMES, SMALL_SIZES, small_refs):
            for k, at, w in _row_pieces(n):
                out_ref[PACK_AT[name] + k:PACK_AT[name] + k + 1, 0:w] = ref[:, at:at + w]
        out_ref[PACK_LOSS:PACK_LOSS + 1, :] = loss_ref[...]

    return pl.pallas_call(body, name="pack_small", out_shape=jax.ShapeDtypeStruct((PACK_ROWS, PACK_W), F32),
                          compiler_params=_cparams())(*d_small, loss, d_lora)


def _adamw_small(tot, chip_arr, ws, ms, vs):
    ns = len(SMALL_NAMES)
    n_par = ns + 2

    def body(s_ref, tot_ref, glw_ref, gla_ref, *refs):
        w_refs, m_refs, v_refs = refs[:n_par], refs[n_par:2 * n_par], refs[2 * n_par:3 * n_par]
        outs = refs[3 * n_par:]
        g_refs, d_refs, nm_refs, nv_refs = (outs[i * n_par:(i + 1) * n_par] for i in range(4))
        grads = [jnp.concatenate([tot_ref[PACK_AT[name] + k:PACK_AT[name] + k + 1, 0:w] for k, _, w in _row_pieces(n)],
                                 axis=1) for name, n in zip(SMALL_NAMES, SMALL_SIZES)]
        grads += [glw_ref[...], gla_ref[...]]
        for i, g in enumerate(grads):
            d, mn, vn = _adam_update(w_refs[i][...], g, m_refs[i][...], v_refs[i][...])
            g_refs[i][...] = g
            d_refs[i][...] = d
            nm_refs[i][...] = mn
            nv_refs[i][...] = vn

    def whole(a):
        nd = a.ndim
        return pl.BlockSpec(a.shape, lambda i, s: (0,) * nd)

    shard = (LORA, LORA_SHARD)
    par_specs = [whole(a) for a in ws]
    res = pl.pallas_call(
        body, name="adamw_small",
        grid_spec=pltpu.PrefetchScalarGridSpec(
            num_scalar_prefetch=1, grid=(1,),
            in_specs=[whole(tot), pl.BlockSpec(shard, lambda i, s: (PACK_LORA_W // LORA, s[0])),
                      pl.BlockSpec(shard, lambda i, s: (PACK_LORA_A // LORA, s[0]))] + par_specs * 3,
            out_specs=par_specs * 4),
        out_shape=[jax.ShapeDtypeStruct(a.shape, F32) for a in ws] * 4,
        compiler_params=_cparams(dimension_semantics=("arbitrary",)),
    )(chip_arr, tot, tot, tot, *ws, *ms, *vs)
    return [res[i * n_par:(i + 1) * n_par] for i in range(4)]


def kernel(x, norm_g, w_in, ret_gn_g, rwkv_mu, w_lora_up, w0, a_lora_up, a0, k_k, k_a, r_k, rwkv_gn_g, rwkv_gn_b, w_out, final_norm_g, loss_target, m_norm_g, m_w_in, m_ret_gn_g, m_rwkv_mu, m_w_lora_up, m_w0, m_a_lora_up, m_a0, m_k_k, m_k_a, m_r_k, m_rwkv_gn_g, m_rwkv_gn_b, m_w_out, m_final_norm_g, v_norm_g, v_w_in, v_ret_gn_g, v_rwkv_mu, v_w_lora_up, v_w0, v_a_lora_up, v_a0, v_k_k, v_k_a, v_r_k, v_rwkv_gn_g, v_rwkv_gn_b, v_w_out, v_final_norm_g):
    W = RW_WIDTH
    params = dict(norm_g=norm_g, ret_gn_g=ret_gn_g, rwkv_mu=rwkv_mu, w0=w0, a0=a0, k_k=k_k, k_a=k_a, r_k=r_k,
                  rwkv_gn_g=rwkv_gn_g, rwkv_gn_b=rwkv_gn_b, final_norm_g=final_norm_g)
    moments_m = dict(norm_g=m_norm_g, ret_gn_g=m_ret_gn_g, rwkv_mu=m_rwkv_mu, w0=m_w0, a0=m_a0, k_k=m_k_k, k_a=m_k_a,
                     r_k=m_r_k, rwkv_gn_g=m_rwkv_gn_g, rwkv_gn_b=m_rwkv_gn_b, final_norm_g=m_final_norm_g)
    moments_v = dict(norm_g=v_norm_g, ret_gn_g=v_ret_gn_g, rwkv_mu=v_rwkv_mu, w0=v_w0, a0=v_a0, k_k=v_k_k, k_a=v_k_a,
                     r_k=v_r_k, rwkv_gn_g=v_rwkv_gn_g, rwkv_gn_b=v_rwkv_gn_b, final_norm_g=v_final_norm_g)
    xi, yi, ci = _position()
    chip = (2 * xi + yi).astype(jnp.int32)

    def halves(a):
        return a.reshape(2, a.shape[0] // 2, a.shape[1])

    g_in, g_out, g_lw, g_la = _gather_chips([halves(w_in[0].astype(BF16)), halves(w_out[0].astype(BF16)),
                                             halves(w_lora_up[0]), halves(a_lora_up[0])])
    w_in_b = jnp.transpose(g_in.reshape(N_CHIPS, D_MODEL, IN_SHARD), (1, 0, 2)).reshape(D_MODEL, IN_COLS)
    w_out_b = g_out.reshape(D_MODEL, D_MODEL)
    lw = jnp.transpose(g_lw.reshape(N_CHIPS, LORA, LORA_SHARD), (1, 0, 2)).reshape(LORA, W)
    la = jnp.transpose(g_la.reshape(N_CHIPS, LORA, LORA_SHARD), (1, 0, 2)).reshape(LORA, W)
    zero = jnp.zeros((LORA, W), F32)
    lora = jnp.concatenate([jnp.concatenate([lw, zero], axis=1), jnp.concatenate([zero, la], axis=1)], axis=0)
    small = {n: params[n].reshape(1, -1) for n in SMALL_NAMES}

    loss, grad_x, d_w_in, d_w_out, d_lora, d_small = _local_step(x[0], loss_target[0], w_in_b, w_out_b, lora, small)

    core = ci.astype(jnp.int32)
    gi = jnp.transpose(d_w_in.reshape(D_MODEL, N_CHIPS, IN_SHARD), (1, 0, 2))
    go = d_w_out.reshape(N_CHIPS, OUT_SHARD, D_MODEL)
    gs = _pack_small([d_small[n] for n in SMALL_NAMES], loss, d_lora)
    p_in, p_out, p_small = _pair_exchange(gi, go, gs)
    c_in, c_out, c_small = _pair_sum(gi, go, gs, p_in, p_out, p_small, core.reshape(1))
    l_in, l_out, l_small = _chip_exchange(c_in, c_out, c_small)
    r_in, r_out, tot = _chip_sum(gi, go, p_in, p_out, c_small, l_in, l_out, l_small, jnp.stack([chip, core]))
    t_in, t_out = _pair_share(r_in, r_out)

    grad_w_in, d_in, nm_in, nv_in = _adamw("adamw_w_in", w_in[0], r_in, t_in, m_w_in[0], v_w_in[0], core.reshape(1), 256)
    grad_w_out, d_out, nm_out, nv_out = _adamw("adamw_w_out", w_out[0], r_out, t_out, m_w_out[0], v_w_out[0],
                                               core.reshape(1), HALF_OUT)
    par_names = SMALL_NAMES + ("w_lora_up", "a_lora_up")

    def operands(tree, lw_, la_):
        return [tree[n].reshape(1, -1) for n in SMALL_NAMES] + [lw_[0], la_[0]]

    res = _adamw_small(tot, chip.reshape(1), operands(params, w_lora_up, a_lora_up),
                       operands(moments_m, m_w_lora_up, m_a_lora_up), operands(moments_v, v_w_lora_up, v_a_lora_up))

    names = ("norm_g", "w_in", "ret_gn_g", "rwkv_mu", "w_lora_up", "w0", "a_lora_up", "a0", "k_k", "k_a", "r_k",
             "rwkv_gn_g", "rwkv_gn_b", "w_out", "final_norm_g")
    shapes = dict(w_in=w_in.shape, w_out=w_out.shape, w_lora_up=w_lora_up.shape, a_lora_up=a_lora_up.shape,
                  **{n: params[n].shape for n in SMALL_NAMES})

    def leaves(pars, big_in, big_out):
        tree = dict(zip(par_names, pars), w_in=big_in, w_out=big_out)
        return [tree[n].reshape(shapes[n]) for n in names]

    grads = leaves(res[0], grad_w_in, grad_w_out)
    deltas = leaves(res[1], d_in, d_out)
    new_m = leaves(res[2], nm_in, nm_out)
    new_v = leaves(res[3], nv_in, nv_out)
    return (tot[PACK_LOSS, 0], grad_x.reshape(x.shape), *grads, *deltas, *new_m, *new_v)
```

```python
import functools

import numpy as np
import jax
import jax.numpy as jnp
from jax import lax
from jax.experimental import pallas as pl
from jax.experimental.pallas import tpu as pltpu

F32 = jnp.float32
BF16 = jnp.bfloat16
HIGHEST = lax.Precision.HIGHEST
MESH = pl.DeviceIdType.MESH

D_MODEL = 1024
N_CHIPS = 4
RET_HEADS = 4
RET_DK = 64
RET_DV = 128
RET_QK = RET_HEADS * RET_DK
RET_WIDTH = RET_HEADS * RET_DV
RET_COLS = 2 * RET_QK + 2 * RET_WIDTH
RET_CHUNK = 64
RET_GROUP = 4
RW_WIDTH = 512
RW_HEAD = 64
RW_HEADS = 8
LORA = 64
RW_COLS = 4 * RW_WIDTH + 2 * LORA
IN_COLS = RET_COLS + RW_COLS
IN_SHARD = IN_COLS // N_CHIPS
OUT_SHARD = D_MODEL // N_CHIPS
ROPE_BASE = 10000.0
RMS_EPS = 1e-6
RET_GN_EPS = 1e-5
RW_GN_EPS = 64e-5
WKV_CHUNK = 16
N_VEC = 5

ADAM_LR = 0.001
ADAM_B1 = 0.9
ADAM_B2 = 0.999
ADAM_EPS = 1e-08
ADAM_WD = 0.01
ADAM_STEP = 10

VMEM_LIMIT = 56 * 1024 * 1024

PACK_W = 512
SMALL_NAMES = ("norm_g", "ret_gn_g", "rwkv_mu", "w0", "a0", "k_k", "k_a", "r_k", "rwkv_gn_g", "rwkv_gn_b",
               "final_norm_g")
SMALL_SIZES = (1024, 512, 2176, 512, 512, 512, 512, 512, 512, 512, 1024)
PACK_LORA_W = 0
PACK_LORA_A = LORA
PACK_SMALL = 2 * LORA


def _pack_layout():
    rows, at = {}, PACK_SMALL
    for name, n in zip(SMALL_NAMES, SMALL_SIZES):
        rows[name] = at
        at += -(-n // PACK_W)
    return rows, at


PACK_AT, PACK_LOSS = _pack_layout()
PACK_ROWS = -(-(PACK_LOSS + 1) // 8) * 8


def _cparams(**kw):
    return pltpu.CompilerParams(vmem_limit_bytes=VMEM_LIMIT, **kw)


def _dot(a, b, precision=None):
    return jnp.dot(a, b, precision=precision, preferred_element_type=F32)


def _dot_nt(a, b, precision=None):
    return lax.dot_general(a, b, (((1,), (1,)), ((), ())), precision=precision, preferred_element_type=F32)


def _dot_tn(a, b, precision=None):
    return lax.dot_general(a, b, (((0,), (0,)), ((), ())), precision=precision, preferred_element_type=F32)


def _split(x):
    hi = x.astype(BF16)
    lo = (x - hi.astype(F32)).astype(BF16)
    return hi, lo


@jax.custom_vjp
def _segsum(x, seg):
    hi, lo = _split(x)
    return _dot(hi, seg) + _dot(lo, seg)


def _segsum_fwd(x, seg):
    return _segsum(x, seg), seg


def _segsum_bwd(seg, ct):
    return _segsum(ct, seg), jnp.zeros_like(seg)


_segsum.defvjp(_segsum_fwd, _segsum_bwd)


def _softplus(z):
    return jnp.maximum(z, 0.0) + jnp.log(1.0 + jnp.exp(-jnp.abs(z)))


def _full(shape):
    nd = len(shape)
    return pl.BlockSpec(shape, lambda *_: (0,) * nd)


def _rope_tables(T):
    half = RET_DK // 2
    expo = -jnp.arange(half, dtype=F32) / jnp.float32(half)
    freqs = jnp.exp(expo * jnp.float32(np.log(ROPE_BASE)))
    ang = jnp.arange(T, dtype=jnp.int32).astype(F32)[:, None] * freqs[None, :]
    cos = jnp.tile(jnp.cos(ang), (1, 2 * RET_HEADS))
    sin = jnp.tile(jnp.sin(ang), (1, 2 * RET_HEADS))
    return cos, sin


def _ret_tables():
    H, C = RET_HEADS, RET_CHUNK
    hidx = jnp.arange(H, dtype=F32)
    lg = jnp.log(1.0 - jnp.exp2(-5.0 - hidx))
    idx = jnp.arange(C, dtype=F32)
    intra = jnp.exp(lg[:, None, None] * jnp.abs(idx[:, None] - idx[None, :]))
    q_dec = jnp.transpose(jnp.exp(lg[:, None] * (idx[None, :] + 1.0)))
    k_dec = jnp.transpose(jnp.exp(lg[:, None] * (C - 1.0 - idx[None, :])))
    chunk_dec = jnp.exp(lg * C)
    qd = jnp.repeat(q_dec, RET_DK, axis=1)
    kd = jnp.repeat(k_dec, RET_DK, axis=1)
    row_h = np.arange(RET_QK) // RET_DK
    col_h = np.arange(RET_WIDTH) // RET_DV
    bm = jnp.asarray((row_h[:, None] == col_h[None, :]).astype(np.float32))
    cd = bm * jnp.repeat(chunk_dec, RET_DK)[:, None]
    return intra, qd, kd, cd, bm


def _seg_matrix(width, head):
    h = np.arange(width) // head
    return jnp.asarray((h[:, None] == h[None, :]).astype(np.float32), dtype=BF16)


def _wkv_expand_table():
    Tc = WKV_CHUNK
    k = np.arange(2 * RW_HEADS * Tc)
    kh, kt = (k % (RW_HEADS * Tc)) // Tc, k % Tc
    nh = np.arange(RW_WIDTH) // RW_HEAD
    e = (kh[None, :, None] == nh[None, None, :]) & (kt[None, :, None] == np.arange(Tc)[:, None, None])
    return jnp.asarray(e.astype(np.float32), dtype=BF16)


def _wkv_reduce_table():
    Tc = WKV_CHUNK
    kh = np.arange(RW_WIDTH) // RW_HEAD
    n = np.arange(RW_HEADS * Tc)
    nh, nt = n // Tc, n % Tc
    r = (kh[None, :, None] == nh[None, None, :]) & (nt[None, None, :] == np.arange(Tc)[:, None, None])
    return jnp.asarray(r.astype(np.float32), dtype=BF16)


def _inproj_fwd(x, norm_g, w_b):
    T = x.shape[0]
    tm = min(T, 256)

    def body(x_ref, g_ref, w_ref, pret_ref, prw_ref, u_ref):
        xf = x_ref[...]
        rstd = lax.rsqrt(jnp.mean(xf * xf, axis=-1, keepdims=True) + RMS_EPS)
        ub = ((xf * rstd) * g_ref[...]).astype(BF16)
        u_ref[...] = ub
        pret_ref[...] = _dot(ub, w_ref[:, :RET_COLS])
        prw_ref[...] = _dot(ub, w_ref[:, RET_COLS:])

    return pl.pallas_call(
        body, name="inproj_fwd", grid=(T // tm,),
        in_specs=[pl.BlockSpec((tm, D_MODEL), lambda i: (i, 0)), _full((1, D_MODEL)), _full((D_MODEL, IN_COLS))],
        out_specs=[pl.BlockSpec((tm, RET_COLS), lambda i: (i, 0)), pl.BlockSpec((tm, RW_COLS), lambda i: (i, 0)),
                   pl.BlockSpec((tm, D_MODEL), lambda i: (i, 0))],
        out_shape=[jax.ShapeDtypeStruct((T, RET_COLS), F32), jax.ShapeDtypeStruct((T, RW_COLS), F32),
                   jax.ShapeDtypeStruct((T, D_MODEL), BF16)],
        compiler_params=_cparams(dimension_semantics=("arbitrary",)),
    )(x, norm_g, w_b)


def _rot_half(x):
    n = x.shape[1]
    lane = lax.broadcasted_iota(jnp.int32, x.shape, 1)
    first = (lane % RET_DK) < (RET_DK // 2)
    return jnp.where(first, -pltpu.roll(x, n - RET_DK // 2, 1), pltpu.roll(x, RET_DK // 2, 1))


def _rope(x, cos, sin):
    return x * cos + _rot_half(x) * sin


def _rope_bwd(d, cos, sin):
    return d * cos - _rot_half(d * sin)


def _ret_post(ret, g, gn_g, seg):
    mu = _segsum(ret, seg) * (1.0 / RET_DV)
    xc = ret - mu
    var = _segsum(xc * xc, seg) * (1.0 / RET_DV)
    n = xc * lax.rsqrt(var + RET_GN_EPS)
    return (g * jax.nn.sigmoid(g)) * (n * gn_g)


def _ret_scores(qt, kt, d_ref, h):
    lane = lax.broadcasted_iota(jnp.int32, qt.shape, 1)
    qh = jnp.where(lane // RET_DK == h, qt, 0.0)
    return qh, _dot_nt(qh, kt, HIGHEST) * d_ref[h]


def _ret_group(nch):
    return min(RET_GROUP, nch)


def _ret_fwd(p_ret, cos, sin, tabs, gn_g, seg128):
    T = p_ret.shape[0]
    C = RET_CHUNK
    nch = T // C
    G = _ret_group(nch)
    intra_d, qd, kd, cd, bm = tabs

    def body(q_ref, k_ref, v_ref, g_ref, cos_ref, sin_ref, qd_ref, kd_ref, d_ref, cd_ref, bm_ref, gn_ref, seg_ref,
             y_ref, ret_ref, sin_out_ref, s_ref):
        @pl.when(pl.program_id(0) == 0)
        def _():
            s_ref[...] = jnp.zeros_like(s_ref)

        s_in = s_ref[...]
        for i in range(G):
            rows = slice(i * C, (i + 1) * C)
            cosv, sinv = cos_ref[rows, :], sin_ref[rows, :]
            qt = _rope(q_ref[rows, :], cosv, sinv)
            kt = _rope(k_ref[rows, :], cosv, sinv) * (RET_DK ** -0.5)
            v = v_ref[rows, :]
            sin_out_ref[i] = s_in
            inter = _dot(qt * qd_ref[...], s_in, HIGHEST)
            intra = []
            for h in range(RET_HEADS):
                _, a = _ret_scores(qt, kt, d_ref, h)
                intra.append(_dot(a, v[:, h * RET_DV:(h + 1) * RET_DV], HIGHEST))
            ret_ref[rows, :] = jnp.concatenate(intra, axis=1) + inter
            kv = _dot_tn(kt * kd_ref[...], v, HIGHEST)
            s_in = s_in * cd_ref[...] + kv * bm_ref[...]
        s_ref[...] = s_in
        y_ref[...] = _ret_post(ret_ref[...], g_ref[...], gn_ref[...], seg_ref[...]).astype(BF16)

    GC = G * C
    return pl.pallas_call(
        body, name="ret_fwd", grid=(nch // G,),
        in_specs=[pl.BlockSpec((GC, RET_QK), lambda c: (c, 0)), pl.BlockSpec((GC, RET_QK), lambda c: (c, 1)),
                  pl.BlockSpec((GC, RET_WIDTH), lambda c: (c, 1)), pl.BlockSpec((GC, RET_WIDTH), lambda c: (c, 2)),
                  pl.BlockSpec((GC, RET_QK), lambda c: (c, 0)), pl.BlockSpec((GC, RET_QK), lambda c: (c, 0)),
                  _full((C, RET_QK)), _full((C, RET_QK)), _full((RET_HEADS, C, C)),
                  _full((RET_QK, RET_WIDTH)), _full((RET_QK, RET_WIDTH)), _full((1, RET_WIDTH)),
                  _full((RET_WIDTH, RET_WIDTH))],
        out_specs=[pl.BlockSpec((GC, RET_WIDTH), lambda c: (c, 0)), pl.BlockSpec((GC, RET_WIDTH), lambda c: (c, 0)),
                   pl.BlockSpec((G, RET_QK, RET_WIDTH), lambda c: (c, 0, 0))],
        out_shape=[jax.ShapeDtypeStruct((T, RET_WIDTH), BF16), jax.ShapeDtypeStruct((T, RET_WIDTH), F32),
                   jax.ShapeDtypeStruct((nch, RET_QK, RET_WIDTH), F32)],
        scratch_shapes=[pltpu.VMEM((RET_QK, RET_WIDTH), F32)],
        compiler_params=_cparams(dimension_semantics=("arbitrary",)),
    )(p_ret, p_ret, p_ret, p_ret, cos, sin, qd, kd, intra_d, cd, bm, gn_g, seg128)


def _ret_bwd(p_ret, cos, sin, tabs, gn_g, seg128, ret, s_in_all, dy):
    T = p_ret.shape[0]
    C = RET_CHUNK
    nch = T // C
    G = _ret_group(nch)
    ngr = nch // G
    intra_d, qd, kd, cd, bm = tabs

    def rev(j):
        return lambda c: (ngr - 1 - c, j)

    def body(q_ref, k_ref, v_ref, g_ref, cos_ref, sin_ref, qd_ref, kd_ref, d_ref, cd_ref, bm_ref, gn_ref, seg_ref,
             ret_ref, sin_ref_, dy_ref, dp_ref, dgn_ref, ds_ref):
        @pl.when(pl.program_id(0) == 0)
        def _():
            ds_ref[...] = jnp.zeros_like(ds_ref)
            dgn_ref[...] = jnp.zeros_like(dgn_ref)

        seg = seg_ref[...]
        _, post_vjp = jax.vjp(lambda r_, g_, gn_: _ret_post(r_, g_, gn_, seg), ret_ref[...], g_ref[...], gn_ref[...])
        dret_all, dg_all, dgn = post_vjp(dy_ref[...])
        dgn_ref[...] += dgn
        dp_ref[:, 2 * RET_QK + RET_WIDTH:] = dg_all

        qdv, kdv = qd_ref[...], kd_ref[...]
        ds_out = ds_ref[...]
        for i in reversed(range(G)):
            rows = slice(i * C, (i + 1) * C)
            cosv, sinv = cos_ref[rows, :], sin_ref[rows, :]
            qt = _rope(q_ref[rows, :], cosv, sinv)
            kt = _rope(k_ref[rows, :], cosv, sinv) * (RET_DK ** -0.5)
            v = v_ref[rows, :]
            s_in = sin_ref_[i]
            dret = dret_all[rows, :]
            dqt = qdv * _dot_nt(dret, s_in, HIGHEST)
            dkt = kdv * _dot_nt(v, ds_out, HIGHEST)
            dv_all = _dot(kt * kdv, ds_out, HIGHEST)
            dvs = []
            for h in range(RET_HEADS):
                sl = slice(h * RET_DV, (h + 1) * RET_DV)
                qh, a = _ret_scores(qt, kt, d_ref, h)
                lane = lax.broadcasted_iota(jnp.int32, kt.shape, 1)
                kh = jnp.where(lane // RET_DK == h, kt, 0.0)
                da = _dot_nt(dret[:, sl], v[:, sl], HIGHEST) * d_ref[h]
                dvs.append(_dot_tn(a, dret[:, sl], HIGHEST))
                dqt = dqt + _dot(da, kh, HIGHEST)
                dkt = dkt + _dot_tn(da, qh, HIGHEST)
            ds_out = ds_out * cd_ref[...] + _dot_tn(qt * qdv, dret, HIGHEST) * bm_ref[...]
            dp_ref[rows, :RET_QK] = _rope_bwd(dqt, cosv, sinv)
            dp_ref[rows, RET_QK:2 * RET_QK] = _rope_bwd(dkt * (RET_DK ** -0.5), cosv, sinv)
            dp_ref[rows, 2 * RET_QK:2 * RET_QK + RET_WIDTH] = dv_all + jnp.concatenate(dvs, axis=1)
        ds_ref[...] = ds_out

    GC = G * C
    return pl.pallas_call(
        body, name="ret_bwd", grid=(ngr,),
        in_specs=[pl.BlockSpec((GC, RET_QK), rev(0)), pl.BlockSpec((GC, RET_QK), rev(1)),
                  pl.BlockSpec((GC, RET_WIDTH), rev(1)), pl.BlockSpec((GC, RET_WIDTH), rev(2)),
                  pl.BlockSpec((GC, RET_QK), rev(0)), pl.BlockSpec((GC, RET_QK), rev(0)),
                  _full((C, RET_QK)), _full((C, RET_QK)), _full((RET_HEADS, C, C)),
                  _full((RET_QK, RET_WIDTH)), _full((RET_QK, RET_WIDTH)), _full((1, RET_WIDTH)),
                  _full((RET_WIDTH, RET_WIDTH)),
                  pl.BlockSpec((GC, RET_WIDTH), rev(0)),
                  pl.BlockSpec((G, RET_QK, RET_WIDTH), lambda c: (ngr - 1 - c, 0, 0)),
                  pl.BlockSpec((GC, RET_WIDTH), rev(0))],
        out_specs=[pl.BlockSpec((GC, RET_COLS), rev(0)), _full((1, RET_WIDTH))],
        out_shape=[jax.ShapeDtypeStruct((T, RET_COLS), F32), jax.ShapeDtypeStruct((1, RET_WIDTH), F32)],
        scratch_shapes=[pltpu.VMEM((RET_QK, RET_WIDTH), F32)],
        compiler_params=_cparams(dimension_semantics=("arbitrary",)),
    )(p_ret, p_ret, p_ret, p_ret, cos, sin, qd, kd, intra_d, cd, bm, gn_g, seg128, ret, s_in_all, dy)


def _prep_fn(p, prev, mu, w0, a0, k_k, k_a, lora, seg):
    W = RW_WIDTH
    ps = p + mu * (prev - p)
    r, kr, vr, g = ps[:, 0:W], ps[:, W:2 * W], ps[:, 2 * W:3 * W], ps[:, 3 * W:4 * W]
    z = ps[:, 4 * W:]
    lane = lax.broadcasted_iota(jnp.int32, z.shape, 1)
    z = jnp.where(lane < LORA, jnp.tanh(z), z)
    lo = _dot(z, lora, HIGHEST)
    w_log = -_softplus(-(w0 + lo[:, :W])) - 0.5
    decay = jnp.exp(-jnp.exp(w_log))
    a = jax.nn.sigmoid(a0 + lo[:, W:])
    kk = kr * k_k
    kk = kk / jnp.maximum(jnp.sqrt(_segsum(kk * kk, seg)), 1e-12)
    k = kr * (1.0 + (a - 1.0) * k_a)
    return kk, decay, kk * a, k, r, vr, g


def _post_fn(o, r, k, v, g, gn_g, gn_b, r_k, seg):
    mu = _segsum(o, seg) * (1.0 / RW_HEAD)
    oc = o - mu
    var = _segsum(oc * oc, seg) * (1.0 / RW_HEAD)
    on = oc * lax.rsqrt(var + RW_GN_EPS) * gn_g + gn_b
    bonus = _segsum(r * k * r_k, seg) * v
    return (g * jax.nn.sigmoid(g)) * (on + bonus)


def _shift_down(p, first_row):
    row = lax.broadcasted_iota(jnp.int32, p.shape, 0)
    return jnp.where(row == 0, first_row, pltpu.roll(p, 1, 0))


def _shift_up(p, last_row):
    n = p.shape[0]
    row = lax.broadcasted_iota(jnp.int32, p.shape, 0)
    return jnp.where(row == n - 1, last_row, pltpu.roll(p, n - 1, 0))


def _row_tile(T):
    return min(T, 256)


def _prep_fwd(p_rw, bnd, mu, w0, a0, k_k, k_a, lora, seg64):
    T = p_rw.shape[0]
    tm = _row_tile(T)
    W = RW_WIDTH

    def body(p_ref, bnd_ref, mu_ref, w0_ref, a0_ref, kk_ref, ka_ref, lora_ref, seg_ref, vecs_ref, v_ref, g_ref):
        p = p_ref[...]
        prev = _shift_down(p, bnd_ref[0])
        res = _prep_fn(p, prev, mu_ref[...], w0_ref[...], a0_ref[...], kk_ref[...], ka_ref[...], lora_ref[...],
                       seg_ref[...])
        for j in range(N_VEC):
            vecs_ref[j] = res[j]
        v_ref[...] = res[N_VEC]
        g_ref[...] = res[N_VEC + 1]

    small = _full((1, W))
    row = pl.BlockSpec((tm, W), lambda i: (i, 0))
    return pl.pallas_call(
        body, name="rwkv_prep_fwd", grid=(T // tm,),
        in_specs=[pl.BlockSpec((tm, RW_COLS), lambda i: (i, 0)), pl.BlockSpec((1, 1, RW_COLS), lambda i: (i, 0, 0)),
                  _full((1, RW_COLS)), small, small, small, small, _full((2 * LORA, 2 * W)), _full((W, W))],
        out_specs=[pl.BlockSpec((N_VEC, tm, W), lambda i: (0, i, 0)), row, row],
        out_shape=[jax.ShapeDtypeStruct((N_VEC, T, W), F32), jax.ShapeDtypeStruct((T, W), F32),
                   jax.ShapeDtypeStruct((T, W), F32)],
        compiler_params=_cparams(dimension_semantics=("arbitrary",)),
    )(p_rw, bnd, mu, w0, a0, k_k, k_a, lora, seg64)


def _prep_bwd(p_rw, bnd, mu, w0, a0, k_k, k_a, lora, seg64, cts):
    T = p_rw.shape[0]
    tm = _row_tile(T)
    W = RW_WIDTH

    def body(p_ref, bnd_ref, mu_ref, w0_ref, a0_ref, kk_ref, ka_ref, lora_ref, seg_ref,
             dvecs_ref, dk2_ref, dr2_ref, dv1_ref, dv2_ref, dg_ref,
             dp_ref, dprev_ref, dmu_ref, dw0_ref, da0_ref, dkk_p_ref, dka_ref, dlora_ref):
        accs = (dmu_ref, dw0_ref, da0_ref, dkk_p_ref, dka_ref, dlora_ref)

        @pl.when(pl.program_id(0) == 0)
        def _():
            for a_ref in accs:
                a_ref[...] = jnp.zeros_like(a_ref)

        p = p_ref[...]
        prev = _shift_down(p, bnd_ref[0])
        seg = seg_ref[...]
        _, vjp = jax.vjp(lambda *a: _prep_fn(*a, seg), p, prev, mu_ref[...], w0_ref[...], a0_ref[...], kk_ref[...],
                         ka_ref[...], lora_ref[...])
        ct = (dvecs_ref[0], dvecs_ref[1], dvecs_ref[2], dvecs_ref[3] + dk2_ref[...], dvecs_ref[4] + dr2_ref[...],
              dv1_ref[...] + dv2_ref[...], dg_ref[...])
        grads = vjp(ct)
        dp_ref[...] = grads[0]
        dprev_ref[...] = grads[1]
        for a_ref, gval in zip(accs, grads[2:]):
            a_ref[...] += gval

    small = _full((1, W))
    row = pl.BlockSpec((tm, W), lambda i: (i, 0))
    return pl.pallas_call(
        body, name="rwkv_prep_bwd", grid=(T // tm,),
        in_specs=[pl.BlockSpec((tm, RW_COLS), lambda i: (i, 0)), pl.BlockSpec((1, 1, RW_COLS), lambda i: (i, 0, 0)),
                  _full((1, RW_COLS)), small, small, small, small, _full((2 * LORA, 2 * W)), _full((W, W)),
                  pl.BlockSpec((N_VEC, tm, W), lambda i: (0, i, 0))] + [row] * 5,
        out_specs=[pl.BlockSpec((tm, RW_COLS), lambda i: (i, 0)), pl.BlockSpec((tm, RW_COLS), lambda i: (i, 0)),
                   _full((1, RW_COLS)), small, small, small, small, _full((2 * LORA, 2 * W))],
        out_shape=[jax.ShapeDtypeStruct((T, RW_COLS), F32), jax.ShapeDtypeStruct((T, RW_COLS), F32),
                   jax.ShapeDtypeStruct((1, RW_COLS), F32)] + [jax.ShapeDtypeStruct((1, W), F32)] * 4
                  + [jax.ShapeDtypeStruct((2 * LORA, 2 * W), F32)],
        compiler_params=_cparams(dimension_semantics=("arbitrary",)),
    )(p_rw, bnd, mu, w0, a0, k_k, k_a, lora, seg64, *cts)


def _vec_spec(j, tm):
    return pl.BlockSpec((None, tm, RW_WIDTH), lambda i: (j, i, 0))


def _post_fwd(o, vecs, v, g, gn_g, gn_b, r_k, seg64):
    T = o.shape[0]
    tm = _row_tile(T)
    W = RW_WIDTH

    def body(o_ref, r_ref, k_ref, v_ref, g_ref, gg_ref, gb_ref, rk_ref, seg_ref, y_ref):
        y_ref[...] = _post_fn(o_ref[...], r_ref[...], k_ref[...], v_ref[...], g_ref[...], gg_ref[...], gb_ref[...],
                              rk_ref[...], seg_ref[...]).astype(BF16)

    row = pl.BlockSpec((tm, W), lambda i: (i, 0))
    small = _full((1, W))
    return pl.pallas_call(
        body, name="rwkv_post_fwd", grid=(T // tm,),
        in_specs=[row, _vec_spec(4, tm), _vec_spec(3, tm), row, row] + [small] * 3 + [_full((W, W))],
        out_specs=row, out_shape=jax.ShapeDtypeStruct((T, W), BF16),
        compiler_params=_cparams(dimension_semantics=("arbitrary",)),
    )(o, vecs, vecs, v, g, gn_g, gn_b, r_k, seg64)


def _post_bwd(o, vecs, v, g, gn_g, gn_b, r_k, seg64, dy):
    T = o.shape[0]
    tm = _row_tile(T)
    W = RW_WIDTH

    def body(o_ref, r_ref, k_ref, v_ref, g_ref, gg_ref, gb_ref, rk_ref, seg_ref, dy_ref,
             do_ref, dr_ref, dk_ref, dv_ref, dg_ref, dgg_ref, dgb_ref, drk_ref):
        accs = (dgg_ref, dgb_ref, drk_ref)

        @pl.when(pl.program_id(0) == 0)
        def _():
            for a_ref in accs:
                a_ref[...] = jnp.zeros_like(a_ref)

        seg = seg_ref[...]
        _, vjp = jax.vjp(lambda *a: _post_fn(*a, seg), o_ref[...], r_ref[...], k_ref[...], v_ref[...], g_ref[...],
                         gg_ref[...], gb_ref[...], rk_ref[...])
        grads = vjp(dy_ref[...])
        for o_, gval in zip((do_ref, dr_ref, dk_ref, dv_ref, dg_ref), grads[:5]):
            o_[...] = gval
        for a_ref, gval in zip(accs, grads[5:]):
            a_ref[...] += gval

    row = pl.BlockSpec((tm, W), lambda i: (i, 0))
    small = _full((1, W))
    return pl.pallas_call(
        body, name="rwkv_post_bwd", grid=(T // tm,),
        in_specs=[row, _vec_spec(4, tm), _vec_spec(3, tm), row, row] + [small] * 3
                 + [_full((W, W)), pl.BlockSpec((tm, W), lambda i: (i, 1))],
        out_specs=[row] * 5 + [small] * 3,
        out_shape=[jax.ShapeDtypeStruct((T, W), F32)] * 5 + [jax.ShapeDtypeStruct((1, W), F32)] * 3,
        compiler_params=_cparams(dimension_semantics=("arbitrary",)),
    )(o, vecs, vecs, v, g, gn_g, gn_b, r_k, seg64, dy)


def _to_cols(vecs, T):
    Tc = WKV_CHUNK
    x = vecs.reshape(N_VEC, T // Tc, Tc, RW_HEADS, RW_HEAD)
    x = jnp.transpose(x, (1, 0, 4, 3, 2))
    return x.reshape(T // Tc, N_VEC * RW_HEAD, RW_HEADS * Tc)


def _from_cols(cols, T):
    Tc = WKV_CHUNK
    x = cols.reshape(T // Tc, N_VEC, RW_HEAD, RW_HEADS, Tc)
    return jnp.transpose(x, (1, 0, 4, 3, 2)).reshape(N_VEC, T, RW_WIDTH)


def _wkv_lhs(cols_ref):
    hi, lo = _split(cols_ref[0])
    return jnp.concatenate([hi, lo], axis=1)


def _wkv_step(st, kk_e, w_e, b_e, k_e, v_row):
    sa = -jnp.sum(st * kk_e, axis=0, keepdims=True)
    return st * w_e + b_e * sa + k_e * v_row, sa


def _wkv_fwd(cols, v, e_tab):
    T = v.shape[0]
    Tc = WKV_CHUNK
    nch = T // Tc
    J, W = RW_HEAD, RW_WIDTH

    def body(cols_ref, v_ref, e_ref, o_ref, states_ref, sa_ref, last_ref, s_ref):
        @pl.when(pl.program_id(0) == 0)
        def _():
            s_ref[...] = jnp.zeros_like(s_ref)

        lhs = _wkv_lhs(cols_ref)
        st = s_ref[...]
        for t in range(Tc):
            ex = _dot(lhs, e_ref[t])
            states_ref[t] = st
            st, sa = _wkv_step(st, ex[0:J], ex[J:2 * J], ex[2 * J:3 * J], ex[3 * J:4 * J], v_ref[t:t + 1, :])
            sa_ref[t:t + 1, :] = sa
            o_ref[t:t + 1, :] = jnp.sum(st * ex[4 * J:5 * J], axis=0, keepdims=True)
        s_ref[...] = st
        last_ref[...] = st

    return pl.pallas_call(
        body, name="wkv_fwd", grid=(nch,),
        in_specs=[pl.BlockSpec((1, N_VEC * J, 128), lambda c: (c, 0, 0)), pl.BlockSpec((Tc, W), lambda c: (c, 0)),
                  _full((Tc, 2 * 128, W))],
        out_specs=[pl.BlockSpec((Tc, W), lambda c: (c, 0)), pl.BlockSpec((Tc, J, W), lambda c: (c, 0, 0)),
                   pl.BlockSpec((Tc, W), lambda c: (c, 0)), _full((J, W))],
        out_shape=[jax.ShapeDtypeStruct((T, W), F32), jax.ShapeDtypeStruct((T, J, W), F32),
                   jax.ShapeDtypeStruct((T, W), F32), jax.ShapeDtypeStruct((J, W), F32)],
        scratch_shapes=[pltpu.VMEM((J, W), F32)],
        compiler_params=_cparams(dimension_semantics=("arbitrary",)),
    )(cols, v, e_tab)


def _wkv_bwd(cols, v, do, states, sa, s_last, e_tab, r_tab):
    T = v.shape[0]
    Tc = WKV_CHUNK
    nch = T // Tc
    J, W = RW_HEAD, RW_WIDTH
    blocks = [slice(b * 128, (b + 1) * 128) for b in range(W // 128)]

    def body(cols_ref, v_ref, do_ref, states_ref, sa_ref, last_ref, e_ref, r_ref, dv_ref, dcols_ref, ds_ref, next_ref):
        @pl.when(pl.program_id(0) == 0)
        def _():
            ds_ref[...] = jnp.zeros_like(ds_ref)
            next_ref[...] = last_ref[...]

        lhs = _wkv_lhs(cols_ref)
        dst = [ds_ref[:, b] for b in blocks]
        acc = jnp.zeros((N_VEC * J, 128), F32)
        for t in reversed(range(Tc)):
            ex = _dot(lhs, e_ref[t])
            dvs, prods = [], []
            for i, b in enumerate(blocks):
                kk_e, w_e, b_e, k_e, r_e = (ex[n * J:(n + 1) * J, b] for n in range(N_VEC))
                s_old = states_ref[t, :, b]
                s_new = states_ref[t + 1, :, b] if t + 1 < Tc else next_ref[:, b]
                do_row, v_row, sa_row = do_ref[t:t + 1, b], v_ref[t:t + 1, b], sa_ref[t:t + 1, b]
                dsn = dst[i] + r_e * do_row
                dsa = jnp.sum(dsn * b_e, axis=0, keepdims=True)
                dvs.append(jnp.sum(dsn * k_e, axis=0, keepdims=True))
                prods.append(jnp.concatenate(
                    [s_old * (-dsa), dsn * s_old, dsn * sa_row, dsn * v_row, s_new * do_row], axis=0).astype(BF16))
                dst[i] = dsn * w_e - kk_e * dsa
            dv_ref[t:t + 1, :] = jnp.concatenate(dvs, axis=1)
            acc = acc + _dot(jnp.concatenate(prods, axis=1), r_ref[t])
        for i, b in enumerate(blocks):
            ds_ref[:, b] = dst[i]
        next_ref[...] = states_ref[0]
        dcols_ref[0] = acc

    rev2 = lambda c: (nch - 1 - c, 0)
    rev3 = lambda c: (nch - 1 - c, 0, 0)
    return pl.pallas_call(
        body, name="wkv_bwd", grid=(nch,),
        in_specs=[pl.BlockSpec((1, N_VEC * J, 128), rev3), pl.BlockSpec((Tc, W), rev2), pl.BlockSpec((Tc, W), rev2),
                  pl.BlockSpec((Tc, J, W), rev3), pl.BlockSpec((Tc, W), rev2), _full((J, W)),
                  _full((Tc, 2 * 128, W)), _full((Tc, W, 128))],
        out_specs=[pl.BlockSpec((Tc, W), rev2), pl.BlockSpec((1, N_VEC * J, 128), rev3)],
        out_shape=[jax.ShapeDtypeStruct((T, W), F32), jax.ShapeDtypeStruct((nch, N_VEC * J, 128), F32)],
        scratch_shapes=[pltpu.VMEM((J, W), F32), pltpu.VMEM((J, W), F32)],
        compiler_params=_cparams(dimension_semantics=("arbitrary",)),
    )(cols, v, do, states, sa, s_last, e_tab, r_tab)


def _outproj(x, y_ret, y_rw, w_out_b, target, gf):
    T = x.shape[0]
    tm = _row_tile(T)
    W = RW_WIDTH

    def body(x_ref, yr_ref, yw_ref, w_ref, t_ref, gf_ref, loss_ref, dh_ref, dy_ref, dw_ref, dgf_ref):
        @pl.when(pl.program_id(0) == 0)
        def _():
            loss_ref[...] = jnp.zeros_like(loss_ref)
            dw_ref[...] = jnp.zeros_like(dw_ref)
            dgf_ref[...] = jnp.zeros_like(dgf_ref)

        y = jnp.concatenate([yr_ref[...], yw_ref[...]], axis=1)
        w = w_ref[...]
        h = x_ref[...] + _dot(y, w)
        rstd = lax.rsqrt(jnp.mean(h * h, axis=-1, keepdims=True) + RMS_EPS)
        hn = h * rstd
        gfv = gf_ref[...]
        err = hn * gfv - t_ref[...]
        loss_ref[...] += 0.5 * jnp.sum(jnp.mean(err * err, axis=-1))
        dout = err * (1.0 / D_MODEL)
        dgf_ref[...] += jnp.sum(dout * hn, axis=0, keepdims=True)
        dhn = dout * gfv
        dh = rstd * (dhn - hn * jnp.mean(dhn * hn, axis=-1, keepdims=True))
        dh_ref[...] = dh
        dhb = dh.astype(BF16)
        dy_ref[...] = _dot_nt(dhb, w)
        dw_ref[...] += _dot_tn(y, dhb)

    return pl.pallas_call(
        body, name="outproj_loss", grid=(T // tm,),
        in_specs=[pl.BlockSpec((tm, D_MODEL), lambda i: (i, 0)), pl.BlockSpec((tm, W), lambda i: (i, 0)),
                  pl.BlockSpec((tm, W), lambda i: (i, 0)), _full((D_MODEL, D_MODEL)),
                  pl.BlockSpec((tm, D_MODEL), lambda i: (i, 0)), _full((1, D_MODEL))],
        out_specs=[_full((1, PACK_W)), pl.BlockSpec((tm, D_MODEL), lambda i: (i, 0)),
                   pl.BlockSpec((tm, D_MODEL), lambda i: (i, 0)), _full((D_MODEL, D_MODEL)), _full((1, D_MODEL))],
        out_shape=[jax.ShapeDtypeStruct((1, PACK_W), F32), jax.ShapeDtypeStruct((T, D_MODEL), F32),
                   jax.ShapeDtypeStruct((T, D_MODEL), F32), jax.ShapeDtypeStruct((D_MODEL, D_MODEL), F32),
                   jax.ShapeDtypeStruct((1, D_MODEL), F32)],
        compiler_params=_cparams(dimension_semantics=("arbitrary",)),
    )(x, y_ret, y_rw, w_out_b, target, gf)


def _inproj_bwd_x(dp_ret, dp_rw, dprev, dbnd, w_b, x, norm_g, dh):
    T = x.shape[0]
    tm = _row_tile(T)

    def body(dpr_ref, dpw_ref, dprev_ref, dbnd_ref, w_ref, x_ref, g_ref, dh_ref, gx_ref, dg_ref, dpb_ref):
        @pl.when(pl.program_id(0) == 0)
        def _():
            dg_ref[...] = jnp.zeros_like(dg_ref)

        d_rw = dpw_ref[...] + _shift_up(dprev_ref[...], dbnd_ref[0])
        dpb = jnp.concatenate([dpr_ref[...].astype(BF16), d_rw.astype(BF16)], axis=1)
        dpb_ref[...] = dpb
        du = _dot_nt(dpb, w_ref[...])
        xf = x_ref[...]
        rstd = lax.rsqrt(jnp.mean(xf * xf, axis=-1, keepdims=True) + RMS_EPS)
        xn = xf * rstd
        dg_ref[...] += jnp.sum(du * xn, axis=0, keepdims=True)
        dxn = du * g_ref[...]
        gx_ref[...] = dh_ref[...] + rstd * (dxn - xn * jnp.mean(dxn * xn, axis=-1, keepdims=True))

    return pl.pallas_call(
        body, name="inproj_bwd_x", grid=(T // tm,),
        in_specs=[pl.BlockSpec((tm, RET_COLS), lambda i: (i, 0)), pl.BlockSpec((tm, RW_COLS), lambda i: (i, 0)),
                  pl.BlockSpec((tm, RW_COLS), lambda i: (i, 0)), pl.BlockSpec((1, 1, RW_COLS), lambda i: (i, 0, 0)),
                  _full((D_MODEL, IN_COLS)), pl.BlockSpec((tm, D_MODEL), lambda i: (i, 0)), _full((1, D_MODEL)),
                  pl.BlockSpec((tm, D_MODEL), lambda i: (i, 0))],
        out_specs=[pl.BlockSpec((tm, D_MODEL), lambda i: (i, 0)), _full((1, D_MODEL)),
                   pl.BlockSpec((tm, IN_COLS), lambda i: (i, 0))],
        out_shape=[jax.ShapeDtypeStruct((T, D_MODEL), F32), jax.ShapeDtypeStruct((1, D_MODEL), F32),
                   jax.ShapeDtypeStruct((T, IN_COLS), BF16)],
        compiler_params=_cparams(dimension_semantics=("arbitrary",)),
    )(dp_ret, dp_rw, dprev, dbnd, w_b, x, norm_g, dh)


def _inproj_bwd_w(u_t, dpb):
    T = u_t.shape[1]
    tr = 256

    def body(u_ref, d_ref, o_ref):
        o_ref[...] = _dot(u_ref[...], d_ref[...])

    return pl.pallas_call(
        body, name="inproj_bwd_w", grid=(D_MODEL // tr,),
        in_specs=[pl.BlockSpec((tr, T), lambda i: (i, 0)), _full((T, IN_COLS))],
        out_specs=pl.BlockSpec((tr, IN_COLS), lambda i: (i, 0)),
        out_shape=jax.ShapeDtypeStruct((D_MODEL, IN_COLS), F32),
        compiler_params=_cparams(dimension_semantics=("arbitrary",)),
    )(u_t, dpb)


def _tile_boundaries(a, tm, first):
    T, n = a.shape
    zero = jnp.zeros((1, n), a.dtype)
    if first:
        rows = jnp.concatenate([zero, a[tm - 1:T - 1:tm]], axis=0)
    else:
        rows = jnp.concatenate([a[tm:T:tm], zero], axis=0)
    return rows.reshape(T // tm, 1, n)


def _local_step(x, target, w_in_b, w_out_b, lora, small):
    T = x.shape[0]
    tm = _row_tile(T)
    cos, sin = _rope_tables(T)
    tabs = _ret_tables()
    seg128 = _seg_matrix(RET_WIDTH, RET_DV)
    seg64 = _seg_matrix(RW_WIDTH, RW_HEAD)
    e_tab = _wkv_expand_table()
    r_tab = _wkv_reduce_table()
    prep_w = (small["rwkv_mu"], small["w0"], small["a0"], small["k_k"], small["k_a"], lora, seg64)
    post_w = (small["rwkv_gn_g"], small["rwkv_gn_b"], small["r_k"], seg64)

    p_ret, p_rw, u = _inproj_fwd(x, small["norm_g"], w_in_b)
    y_ret, ret, s_in_all = _ret_fwd(p_ret, cos, sin, tabs, small["ret_gn_g"], seg128)
    bnd = _tile_boundaries(p_rw, tm, True)
    vecs, v, g = _prep_fwd(p_rw, bnd, *prep_w)
    cols = _to_cols(vecs, T)
    o, states, sa, s_last = _wkv_fwd(cols, v, e_tab)
    y_rw = _post_fwd(o, vecs, v, g, *post_w)
    loss, dh, dy, d_w_out, d_gf = _outproj(x, y_ret, y_rw, w_out_b, target, small["final_norm_g"])

    do, dr2, dk2, dv2, dg, d_gn_g, d_gn_b, d_r_k = _post_bwd(o, vecs, v, g, *post_w, dy)
    dv1, dcols = _wkv_bwd(cols, v, do, states, sa, s_last, e_tab, r_tab)
    dp_rw, dprev, d_mu, d_w0, d_a0, d_k_k, d_k_a, d_lora = _prep_bwd(
        p_rw, bnd, *prep_w, (_from_cols(dcols, T), dk2, dr2, dv1, dv2, dg))
    dp_ret, d_ret_gn = _ret_bwd(p_ret, cos, sin, tabs, small["ret_gn_g"], seg128, ret, s_in_all, dy)
    dbnd = _tile_boundaries(dprev, tm, False)
    grad_x, d_norm_g, dpb = _inproj_bwd_x(dp_ret, dp_rw, dprev, dbnd, w_in_b, x, small["norm_g"], dh)
    d_w_in = _inproj_bwd_w(jnp.transpose(u), dpb)

    d_small = {"norm_g": d_norm_g, "ret_gn_g": d_ret_gn, "rwkv_mu": d_mu, "w0": d_w0, "a0": d_a0, "k_k": d_k_k,
               "k_a": d_k_a, "r_k": d_r_k, "rwkv_gn_g": d_gn_g, "rwkv_gn_b": d_gn_b, "final_norm_g": d_gf}
    return loss, grad_x, d_w_in, d_w_out, d_lora, d_small


ANY = pl.BlockSpec(memory_space=pl.ANY)
CHIP_FLIPS = ((0, 1), (1, 0), (1, 1))
N_FLIPS = len(CHIP_FLIPS)
LORA_SHARD = RW_WIDTH // N_CHIPS
HALF_IN = D_MODEL // 2
HALF_OUT = OUT_SHARD // 2


def _position():
    return lax.axis_index("x"), lax.axis_index("y"), lax.axis_index("c")


def _flip(v, f):
    return 1 - v if f else v


def _finish(local, remote, landed):
    for cp in landed:
        cp.wait_recv()
    for cp in remote:
        cp.wait_send()
    for cp in local:
        cp.wait()


def _gather_chips(arrs):
    n = len(arrs)

    def body(*refs):
        ins, outs = refs[:n], refs[n:2 * n]
        send, recv, pass_send, pass_recv = refs[2 * n:]
        x, y, c = _position()
        s = 2 * x + y
        sibling = (x, y, 1 - c)

        def copy(src, dst, sems, k, to):
            return pltpu.make_async_remote_copy(src_ref=src, dst_ref=dst, send_sem=sems[0].at[k], recv_sem=sems[1].at[k],
                                                device_id=to, device_id_type=MESH)

        remote, landed, passed, passed_in = [], [], [], []
        for a in range(n):
            for j, (fx, fy) in enumerate(CHIP_FLIPS):
                px, py = _flip(x, fx), _flip(y, fy)
                ps = 2 * px + py
                k = a * N_FLIPS + j
                remote.append(copy(ins[a].at[c], outs[a].at[s, c], (send, recv), k, (px, py, c)))
                landed.append(copy(ins[a].at[c], outs[a].at[ps, c], (send, recv), k, (px, py, c)))
                passed.append(copy(outs[a].at[ps, c], outs[a].at[ps, c], (pass_send, pass_recv), k, sibling))
                passed_in.append(copy(outs[a].at[ps, 1 - c], outs[a].at[ps, 1 - c], (pass_send, pass_recv), k, sibling))
        for cp in remote:
            cp.start()
        for arrived, onward in zip(landed, passed):
            arrived.wait_recv()
            onward.start()
        _finish([], remote + passed, passed_in)

    sems = pltpu.SemaphoreType.DMA((n * N_FLIPS,))
    return pl.pallas_call(
        body, name="gather_weights",
        in_specs=[ANY] * n, out_specs=[ANY] * n,
        out_shape=[jax.ShapeDtypeStruct((N_CHIPS,) + a.shape, a.dtype) for a in arrs],
        scratch_shapes=[sems, sems, sems, sems],
    )(*arrs)


def _pair_exchange(g_in, g_out, g_small):
    def body(gi_ref, go_ref, gs_ref, li_ref, lo_ref, ls_ref, send, recv):
        x, y, c = _position()
        peer = (x, y, 1 - c)
        srcs = (gi_ref.at[:, pl.ds((1 - c) * HALF_IN, HALF_IN), :], go_ref.at[:, pl.ds((1 - c) * HALF_OUT, HALF_OUT), :],
                gs_ref)
        remote = [pltpu.make_async_remote_copy(src_ref=src, dst_ref=dst, send_sem=send.at[k], recv_sem=recv.at[k],
                                               device_id=peer, device_id_type=MESH)
                  for k, (src, dst) in enumerate(zip(srcs, (li_ref, lo_ref, ls_ref)))]
        for cp in remote:
            cp.start()
        _finish([], remote, remote)

    return pl.pallas_call(
        body, name="pair_exchange",
        in_specs=[ANY] * 3, out_specs=[ANY] * 3,
        out_shape=[jax.ShapeDtypeStruct((N_CHIPS, HALF_IN, IN_SHARD), F32),
                   jax.ShapeDtypeStruct((N_CHIPS, HALF_OUT, D_MODEL), F32),
                   jax.ShapeDtypeStruct(g_small.shape, F32)],
        scratch_shapes=[pltpu.SemaphoreType.DMA((3,)), pltpu.SemaphoreType.DMA((3,))],
    )(g_in, g_out, g_small)


def _pair_sum(g_in, g_out, g_small, l_in, l_out, l_small, c_arr):
    tr = HALF_IN // 2

    def body(c_ref, gi_ref, go_ref, gs_ref, li_ref, lo_ref, ls_ref, ci_ref, co_ref, cs_ref):
        ci_ref[...] = (gi_ref[...] + li_ref[...]).astype(BF16)

        @pl.when(pl.program_id(1) == 0)
        def _():
            co_ref[...] = (go_ref[...] + lo_ref[...]).astype(BF16)

        @pl.when((pl.program_id(0) == 0) & (pl.program_id(1) == 0))
        def _():
            cs_ref[...] = gs_ref[...] + ls_ref[...]

    nd = g_small.shape
    return pl.pallas_call(
        body, name="pair_sum",
        grid_spec=pltpu.PrefetchScalarGridSpec(
            num_scalar_prefetch=1, grid=(N_CHIPS, 2),
            in_specs=[pl.BlockSpec((1, tr, IN_SHARD), lambda s, i, c: (s, 2 * c[0] + i, 0)),
                      pl.BlockSpec((1, HALF_OUT, D_MODEL), lambda s, i, c: (s, c[0], 0)),
                      pl.BlockSpec(nd, lambda s, i, c: (0, 0)),
                      pl.BlockSpec((1, tr, IN_SHARD), lambda s, i, c: (s, i, 0)),
                      pl.BlockSpec((1, HALF_OUT, D_MODEL), lambda s, i, c: (s, 0, 0)),
                      pl.BlockSpec(nd, lambda s, i, c: (0, 0))],
            out_specs=[pl.BlockSpec((1, tr, IN_SHARD), lambda s, i, c: (s, i, 0)),
                       pl.BlockSpec((1, HALF_OUT, D_MODEL), lambda s, i, c: (s, 0, 0)),
                       pl.BlockSpec(nd, lambda s, i, c: (0, 0))]),
        out_shape=[jax.ShapeDtypeStruct((N_CHIPS, HALF_IN, IN_SHARD), BF16),
                   jax.ShapeDtypeStruct((N_CHIPS, HALF_OUT, D_MODEL), BF16), jax.ShapeDtypeStruct(nd, F32)],
        compiler_params=_cparams(dimension_semantics=("arbitrary", "arbitrary")),
    )(c_arr, g_in, g_out, g_small, l_in, l_out, l_small)


def _chip_exchange(c_in, c_out, c_small):
    def body(ci_ref, co_ref, cs_ref, li_ref, lo_ref, ls_ref, send, recv):
        x, y, c = _position()
        s = 2 * x + y
        remote = []
        for j, (fx, fy) in enumerate(CHIP_FLIPS):
            px, py = _flip(x, fx), _flip(y, fy)
            ps = 2 * px + py
            for a, (src, dst) in enumerate(((ci_ref.at[ps], li_ref.at[j]), (co_ref.at[ps], lo_ref.at[j]),
                                            (cs_ref, ls_ref.at[j]))):
                k = 3 * j + a
                remote.append(pltpu.make_async_remote_copy(src_ref=src, dst_ref=dst, send_sem=send.at[k],
                                                           recv_sem=recv.at[k], device_id=(px, py, c),
                                                           device_id_type=MESH))
        for cp in remote:
            cp.start()
        _finish([], remote, remote)

    return pl.pallas_call(
        body, name="chip_exchange",
        in_specs=[ANY] * 3, out_specs=[ANY] * 3,
        out_shape=[jax.ShapeDtypeStruct((N_FLIPS, HALF_IN, IN_SHARD), c_in.dtype),
                   jax.ShapeDtypeStruct((N_FLIPS, HALF_OUT, D_MODEL), c_out.dtype),
                   jax.ShapeDtypeStruct((N_FLIPS,) + c_small.shape, F32)],
        scratch_shapes=[pltpu.SemaphoreType.DMA((3 * N_FLIPS,)), pltpu.SemaphoreType.DMA((3 * N_FLIPS,))],
    )(c_in, c_out, c_small)


def _chip_sum(g_in, g_out, p_in, p_out, c_small, l_in, l_out, l_small, sc_arr):
    tr = HALF_IN // 2
    nd = c_small.shape

    def body(s_ref, gi_ref, go_ref, pi_ref, po_ref, cs_ref, li0, li1, li2, lo0, lo1, lo2, ls_ref,
             ri_ref, ro_ref, rs_ref):
        ri_ref[...] = (((gi_ref[0] + pi_ref[0]) + li0[0].astype(F32)) + li1[0].astype(F32)) + li2[0].astype(F32)

        @pl.when(pl.program_id(0) == 0)
        def _():
            ro_ref[...] = (((go_ref[0] + po_ref[0]) + lo0[0].astype(F32)) + lo1[0].astype(F32)) + lo2[0].astype(F32)
            me = s_ref[0]
            parts = (cs_ref[...], ls_ref[0], ls_ref[1], ls_ref[2])

            def of_chip(s):
                m = jnp.bitwise_xor(me, s)
                return jnp.where(m == 0, parts[0], jnp.where(m == 1, parts[1], jnp.where(m == 2, parts[2], parts[3])))

            rs_ref[...] = ((of_chip(0) + of_chip(1)) + of_chip(2)) + of_chip(3)

    def flip_in(j):
        return pl.BlockSpec((1, tr, IN_SHARD), lambda i, s: (j, i, 0))

    def flip_out(j):
        return pl.BlockSpec((1, HALF_OUT, D_MODEL), lambda i, s: (j, 0, 0))

    return pl.pallas_call(
        body, name="chip_sum",
        grid_spec=pltpu.PrefetchScalarGridSpec(
            num_scalar_prefetch=1, grid=(2,),
            in_specs=[pl.BlockSpec((1, tr, IN_SHARD), lambda i, s: (s[0], 2 * s[1] + i, 0)),
                      pl.BlockSpec((1, HALF_OUT, D_MODEL), lambda i, s: (s[0], s[1], 0)),
                      pl.BlockSpec((1, tr, IN_SHARD), lambda i, s: (s[0], i, 0)),
                      pl.BlockSpec((1, HALF_OUT, D_MODEL), lambda i, s: (s[0], 0, 0)),
                      pl.BlockSpec(nd, lambda i, s: (0, 0)),
                      flip_in(0), flip_in(1), flip_in(2), flip_out(0), flip_out(1), flip_out(2),
                      pl.BlockSpec((N_FLIPS,) + nd, lambda i, s: (0, 0, 0))],
            out_specs=[pl.BlockSpec((tr, IN_SHARD), lambda i, s: (i, 0)),
                       pl.BlockSpec((HALF_OUT, D_MODEL), lambda i, s: (0, 0)),
                       pl.BlockSpec(nd, lambda i, s: (0, 0))]),
        out_shape=[jax.ShapeDtypeStruct((HALF_IN, IN_SHARD), F32), jax.ShapeDtypeStruct((HALF_OUT, D_MODEL), F32),
                   jax.ShapeDtypeStruct(nd, F32)],
        compiler_params=_cparams(dimension_semantics=("arbitrary",)),
    )(sc_arr, g_in, g_out, p_in, p_out, c_small, l_in, l_in, l_in, l_out, l_out, l_out, l_small)


def _pair_share(r_in, r_out):
    def body(ri_ref, ro_ref, li_ref, lo_ref, send, recv):
        x, y, c = _position()
        remote = [pltpu.make_async_remote_copy(src_ref=src, dst_ref=dst, send_sem=send.at[k], recv_sem=recv.at[k],
                                               device_id=(x, y, 1 - c), device_id_type=MESH)
                  for k, (src, dst) in enumerate(((ri_ref, li_ref), (ro_ref, lo_ref)))]
        for cp in remote:
            cp.start()
        _finish([], remote, remote)

    return pl.pallas_call(
        body, name="pair_share",
        in_specs=[ANY] * 2, out_specs=[ANY] * 2,
        out_shape=[jax.ShapeDtypeStruct(r_in.shape, F32), jax.ShapeDtypeStruct(r_out.shape, F32)],
        scratch_shapes=[pltpu.SemaphoreType.DMA((2,)), pltpu.SemaphoreType.DMA((2,))],
    )(r_in, r_out)


def _adam_update(w, g, m, v):
    mn = ADAM_B1 * m + (1.0 - ADAM_B1) * g
    vn = ADAM_B2 * v + (1.0 - ADAM_B2) * jnp.square(g)
    m_hat = mn / (1.0 - ADAM_B1 ** ADAM_STEP)
    v_hat = vn / (1.0 - ADAM_B2 ** ADAM_STEP)
    return -ADAM_LR * (m_hat / (jnp.sqrt(v_hat) + ADAM_EPS) + ADAM_WD * w), mn, vn


def _adamw(name, w, g_mine, g_theirs, m, v, core_arr, tr):
    rows, cols = w.shape
    per_half = rows // 2 // tr

    def body(c_ref, w_ref, gm_ref, gt_ref, m_ref, v_ref, g_ref, d_ref, nm_ref, nv_ref):
        mine = (pl.program_id(0) // per_half) == c_ref[0]
        g = jnp.where(mine, gm_ref[...], gt_ref[...])
        d, mn, vn = _adam_update(w_ref[...], g, m_ref[...], v_ref[...])
        g_ref[...] = g
        d_ref[...] = d
        nm_ref[...] = mn
        nv_ref[...] = vn

    spec = pl.BlockSpec((tr, cols), lambda i, c: (i, 0))
    half = pl.BlockSpec((tr, cols), lambda i, c: (i % per_half, 0))
    return pl.pallas_call(
        body, name=name,
        grid_spec=pltpu.PrefetchScalarGridSpec(
            num_scalar_prefetch=1, grid=(rows // tr,),
            in_specs=[spec, half, half, spec, spec], out_specs=[spec] * 4),
        out_shape=[jax.ShapeDtypeStruct((rows, cols), F32)] * 4,
        compiler_params=_cparams(dimension_semantics=("arbitrary",)),
    )(core_arr, w, g_mine, g_theirs, m, v)


def _row_pieces(n):
    return [(k, k * PACK_W, min(PACK_W, n - k * PACK_W)) for k in range(-(-n // PACK_W))]


def _pack_small(d_small, loss, d_lora):
    ns = len(SMALL_NAMES)

    def body(*refs):
        small_refs, (loss_ref, lora_ref, out_ref) = refs[:ns], refs[ns:]
        out_ref[...] = jnp.zeros_like(out_ref)
        out_ref[PACK_LORA_W:PACK_LORA_W + LORA, :] = lora_ref[:LORA, :RW_WIDTH]
        out_ref[PACK_LORA_A:PACK_LORA_A + LORA, :] = lora_ref[LORA:, RW_WIDTH:]
        for name, n, ref in zip(SMALL_NAMES, SMALL_SIZES, small_refs):
            for k, at, w in _row_pieces(n):
                out_ref[PACK_AT[name] + k:PACK_AT[name] + k + 1, 0:w] = ref[:, at:at + w]
        out_ref[PACK_LOSS:PACK_LOSS + 1, :] = loss_ref[...]

    return pl.pallas_call(body, name="pack_small", out_shape=jax.ShapeDtypeStruct((PACK_ROWS, PACK_W), F32),
                          compiler_params=_cparams())(*d_small, loss, d_lora)


def _adamw_small(tot, chip_arr, ws, ms, vs):
    ns = len(SMALL_NAMES)
    n_par = ns + 2

    def body(s_ref, tot_ref, glw_ref, gla_ref, *refs):
        w_refs, m_refs, v_refs = refs[:n_par], refs[n_par:2 * n_par], refs[2 * n_par:3 * n_par]
        outs = refs[3 * n_par:]
        g_refs, d_refs, nm_refs, nv_refs = (outs[i * n_par:(i + 1) * n_par] for i in range(4))
        grads = [jnp.concatenate([tot_ref[PACK_AT[name] + k:PACK_AT[name] + k + 1, 0:w] for k, _, w in _row_pieces(n)],
                                 axis=1) for name, n in zip(SMALL_NAMES, SMALL_SIZES)]
        grads += [glw_ref[...], gla_ref[...]]
        for i, g in enumerate(grads):
            d, mn, vn = _adam_update(w_refs[i][...], g, m_refs[i][...], v_refs[i][...])
            g_refs[i][...] = g
            d_refs[i][...] = d
            nm_refs[i][...] = mn
            nv_refs[i][...] = vn

    def whole(a):
        nd = a.ndim
        return pl.BlockSpec(a.shape, lambda i, s: (0,) * nd)

    shard = (LORA, LORA_SHARD)
    par_specs = [whole(a) for a in ws]
    res = pl.pallas_call(
        body, name="adamw_small",
        grid_spec=pltpu.PrefetchScalarGridSpec(
            num_scalar_prefetch=1, grid=(1,),
            in_specs=[whole(tot), pl.BlockSpec(shard, lambda i, s: (PACK_LORA_W // LORA, s[0])),
                      pl.BlockSpec(shard, lambda i, s: (PACK_LORA_A // LORA, s[0]))] + par_specs * 3,
            out_specs=par_specs * 4),
        out_shape=[jax.ShapeDtypeStruct(a.shape, F32) for a in ws] * 4,
        compiler_params=_cparams(dimension_semantics=("arbitrary",)),
    )(chip_arr, tot, tot, tot, *ws, *ms, *vs)
    return [res[i * n_par:(i + 1) * n_par] for i in range(4)]


def kernel(x, norm_g, w_in, ret_gn_g, rwkv_mu, w_lora_up, w0, a_lora_up, a0, k_k, k_a, r_k, rwkv_gn_g, rwkv_gn_b, w_out, final_norm_g, loss_target, m_norm_g, m_w_in, m_ret_gn_g, m_rwkv_mu, m_w_lora_up, m_w0, m_a_lora_up, m_a0, m_k_k, m_k_a, m_r_k, m_rwkv_gn_g, m_rwkv_gn_b, m_w_out, m_final_norm_g, v_norm_g, v_w_in, v_ret_gn_g, v_rwkv_mu, v_w_lora_up, v_w0, v_a_lora_up, v_a0, v_k_k, v_k_a, v_r_k, v_rwkv_gn_g, v_rwkv_gn_b, v_w_out, v_final_norm_g):
    W = RW_WIDTH
    params = dict(norm_g=norm_g, ret_gn_g=ret_gn_g, rwkv_mu=rwkv_mu, w0=w0, a0=a0, k_k=k_k, k_a=k_a, r_k=r_k,
                  rwkv_gn_g=rwkv_gn_g, rwkv_gn_b=rwkv_gn_b, final_norm_g=final_norm_g)
    moments_m = dict(norm_g=m_norm_g, ret_gn_g=m_ret_gn_g, rwkv_mu=m_rwkv_mu, w0=m_w0, a0=m_a0, k_k=m_k_k, k_a=m_k_a,
                     r_k=m_r_k, rwkv_gn_g=m_rwkv_gn_g, rwkv_gn_b=m_rwkv_gn_b, final_norm_g=m_final_norm_g)
    moments_v = dict(norm_g=v_norm_g, ret_gn_g=v_ret_gn_g, rwkv_mu=v_rwkv_mu, w0=v_w0, a0=v_a0, k_k=v_k_k, k_a=v_k_a,
                     r_k=v_r_k, rwkv_gn_g=v_rwkv_gn_g, rwkv_gn_b=v_rwkv_gn_b, final_norm_g=v_final_norm_g)
    xi, yi, ci = _position()
    chip = (2 * xi + yi).astype(jnp.int32)

    def halves(a):
        return a.reshape(2, a.shape[0] // 2, a.shape[1])

    mine = [halves(w_in[0].astype(BF16)), halves(w_out[0].astype(BF16)), halves(w_lora_up[0]), halves(a_lora_up[0])]
    g_in, g_out, g_lw, g_la = [lax.dynamic_update_slice(g, own[None], (chip, 0, 0, 0))
                               for g, own in zip(_gather_chips(mine), mine)]
    w_in_b = jnp.transpose(g_in.reshape(N_CHIPS, D_MODEL, IN_SHARD), (1, 0, 2)).reshape(D_MODEL, IN_COLS)
    w_out_b = g_out.reshape(D_MODEL, D_MODEL)
    lw = jnp.transpose(g_lw.reshape(N_CHIPS, LORA, LORA_SHARD), (1, 0, 2)).reshape(LORA, W)
    la = jnp.transpose(g_la.reshape(N_CHIPS, LORA, LORA_SHARD), (1, 0, 2)).reshape(LORA, W)
    zero = jnp.zeros((LORA, W), F32)
    lora = jnp.concatenate([jnp.concatenate([lw, zero], axis=1), jnp.concatenate([zero, la], axis=1)], axis=0)
    small = {n: params[n].reshape(1, -1) for n in SMALL_NAMES}

    loss, grad_x, d_w_in, d_w_out, d_lora, d_small = _local_step(x[0], loss_target[0], w_in_b, w_out_b, lora, small)

    core = ci.astype(jnp.int32)
    gi = jnp.transpose(d_w_in.reshape(D_MODEL, N_CHIPS, IN_SHARD), (1, 0, 2))
    go = d_w_out.reshape(N_CHIPS, OUT_SHARD, D_MODEL)
    gs = _pack_small([d_small[n] for n in SMALL_NAMES], loss, d_lora)
    p_in, p_out, p_small = _pair_exchange(gi, go, gs)
    c_in, c_out, c_small = _pair_sum(gi, go, gs, p_in, p_out, p_small, core.reshape(1))
    l_in, l_out, l_small = _chip_exchange(c_in, c_out, c_small)
    r_in, r_out, tot = _chip_sum(gi, go, p_in, p_out, c_small, l_in, l_out, l_small, jnp.stack([chip, core]))
    t_in, t_out = _pair_share(r_in, r_out)

    grad_w_in, d_in, nm_in, nv_in = _adamw("adamw_w_in", w_in[0], r_in, t_in, m_w_in[0], v_w_in[0], core.reshape(1), 256)
    grad_w_out, d_out, nm_out, nv_out = _adamw("adamw_w_out", w_out[0], r_out, t_out, m_w_out[0], v_w_out[0],
                                               core.reshape(1), HALF_OUT)
    par_names = SMALL_NAMES + ("w_lora_up", "a_lora_up")

    def operands(tree, lw_, la_):
        return [tree[n].reshape(1, -1) for n in SMALL_NAMES] + [lw_[0], la_[0]]

    res = _adamw_small(tot, chip.reshape(1), operands(params, w_lora_up, a_lora_up),
                       operands(moments_m, m_w_lora_up, m_a_lora_up), operands(moments_v, v_w_lora_up, v_a_lora_up))

    names = ("norm_g", "w_in", "ret_gn_g", "rwkv_mu", "w_lora_up", "w0", "a_lora_up", "a0", "k_k", "k_a", "r_k",
             "rwkv_gn_g", "rwkv_gn_b", "w_out", "final_norm_g")
    shapes = dict(w_in=w_in.shape, w_out=w_out.shape, w_lora_up=w_lora_up.shape, a_lora_up=a_lora_up.shape,
                  **{n: params[n].shape for n in SMALL_NAMES})

    def leaves(pars, big_in, big_out):
        tree = dict(zip(par_names, pars), w_in=big_in, w_out=big_out)
        return [tree[n].reshape(shapes[n]) for n in names]

    grads = leaves(res[0], grad_w_in, grad_w_out)
    deltas = leaves(res[1], d_in, d_out)
    new_m = leaves(res[2], nm_in, nm_out)
    new_v = leaves(res[3], nv_in, nv_out)
    return (tot[PACK_LOSS, 0], grad_x.reshape(x.shape), *grads, *deltas, *new_m, *new_v)
```

```python
import functools

import numpy as np
import jax
import jax.numpy as jnp
from jax import lax
from jax.experimental import pallas as pl
from jax.experimental.pallas import tpu as pltpu

F32 = jnp.float32
BF16 = jnp.bfloat16
HIGHEST = lax.Precision.HIGHEST
MESH = pl.DeviceIdType.MESH

D_MODEL = 1024
N_CHIPS = 4
RET_HEADS = 4
RET_DK = 64
RET_DV = 128
RET_QK = RET_HEADS * RET_DK
RET_WIDTH = RET_HEADS * RET_DV
RET_COLS = 2 * RET_QK + 2 * RET_WIDTH
RET_CHUNK = 64
RET_GROUP = 4
RW_WIDTH = 512
RW_HEAD = 64
RW_HEADS = 8
LORA = 64
RW_COLS = 4 * RW_WIDTH + 2 * LORA
IN_COLS = RET_COLS + RW_COLS
IN_SHARD = IN_COLS // N_CHIPS
OUT_SHARD = D_MODEL // N_CHIPS
ROPE_BASE = 10000.0
RMS_EPS = 1e-6
RET_GN_EPS = 1e-5
RW_GN_EPS = 64e-5
WKV_CHUNK = 16
N_VEC = 5

ADAM_LR = 0.001
ADAM_B1 = 0.9
ADAM_B2 = 0.999
ADAM_EPS = 1e-08
ADAM_WD = 0.01
ADAM_STEP = 10

VMEM_LIMIT = 56 * 1024 * 1024

PACK_W = 512
SMALL_NAMES = ("norm_g", "ret_gn_g", "rwkv_mu", "w0", "a0", "k_k", "k_a", "r_k", "rwkv_gn_g", "rwkv_gn_b",
               "final_norm_g")
SMALL_SIZES = (1024, 512, 2176, 512, 512, 512, 512, 512, 512, 512, 1024)
PACK_LORA_W = 0
PACK_LORA_A = LORA
PACK_SMALL = 2 * LORA


def _pack_layout():
    rows, at = {}, PACK_SMALL
    for name, n in zip(SMALL_NAMES, SMALL_SIZES):
        rows[name] = at
        at += -(-n // PACK_W)
    return rows, at


PACK_AT, PACK_LOSS = _pack_layout()
PACK_ROWS = -(-(PACK_LOSS + 1) // 8) * 8


def _cparams(**kw):
    return pltpu.CompilerParams(vmem_limit_bytes=VMEM_LIMIT, **kw)


def _dot(a, b, precision=None):
    return jnp.dot(a, b, precision=precision, preferred_element_type=F32)


def _dot_nt(a, b, precision=None):
    return lax.dot_general(a, b, (((1,), (1,)), ((), ())), precision=precision, preferred_element_type=F32)


def _dot_tn(a, b, precision=None):
    return lax.dot_general(a, b, (((0,), (0,)), ((), ())), precision=precision, preferred_element_type=F32)


def _split(x):
    hi = x.astype(BF16)
    lo = (x - hi.astype(F32)).astype(BF16)
    return hi, lo


@jax.custom_vjp
def _segsum(x, seg):
    hi, lo = _split(x)
    return _dot(hi, seg) + _dot(lo, seg)


def _segsum_fwd(x, seg):
    return _segsum(x, seg), seg


def _segsum_bwd(seg, ct):
    return _segsum(ct, seg), jnp.zeros_like(seg)


_segsum.defvjp(_segsum_fwd, _segsum_bwd)


def _softplus(z):
    return jnp.maximum(z, 0.0) + jnp.log(1.0 + jnp.exp(-jnp.abs(z)))


def _full(shape):
    nd = len(shape)
    return pl.BlockSpec(shape, lambda *_: (0,) * nd)


def _rope_tables(T):
    half = RET_DK // 2
    expo = -jnp.arange(half, dtype=F32) / jnp.float32(half)
    freqs = jnp.exp(expo * jnp.float32(np.log(ROPE_BASE)))
    ang = jnp.arange(T, dtype=jnp.int32).astype(F32)[:, None] * freqs[None, :]
    cos = jnp.tile(jnp.cos(ang), (1, 2 * RET_HEADS))
    sin = jnp.tile(jnp.sin(ang), (1, 2 * RET_HEADS))
    return cos, sin


def _ret_tables():
    H, C = RET_HEADS, RET_CHUNK
    hidx = jnp.arange(H, dtype=F32)
    lg = jnp.log(1.0 - jnp.exp2(-5.0 - hidx))
    idx = jnp.arange(C, dtype=F32)
    intra = jnp.exp(lg[:, None, None] * jnp.abs(idx[:, None] - idx[None, :]))
    q_dec = jnp.transpose(jnp.exp(lg[:, None] * (idx[None, :] + 1.0)))
    k_dec = jnp.transpose(jnp.exp(lg[:, None] * (C - 1.0 - idx[None, :])))
    chunk_dec = jnp.exp(lg * C)
    qd = jnp.repeat(q_dec, RET_DK, axis=1)
    kd = jnp.repeat(k_dec, RET_DK, axis=1)
    row_h = np.arange(RET_QK) // RET_DK
    col_h = np.arange(RET_WIDTH) // RET_DV
    bm = jnp.asarray((row_h[:, None] == col_h[None, :]).astype(np.float32))
    cd = bm * jnp.repeat(chunk_dec, RET_DK)[:, None]
    return intra, qd, kd, cd, bm


def _seg_matrix(width, head):
    h = np.arange(width) // head
    return jnp.asarray((h[:, None] == h[None, :]).astype(np.float32), dtype=BF16)


def _wkv_expand_table():
    Tc = WKV_CHUNK
    k = np.arange(2 * RW_HEADS * Tc)
    kh, kt = (k % (RW_HEADS * Tc)) // Tc, k % Tc
    nh = np.arange(RW_WIDTH) // RW_HEAD
    e = (kh[None, :, None] == nh[None, None, :]) & (kt[None, :, None] == np.arange(Tc)[:, None, None])
    return jnp.asarray(e.astype(np.float32), dtype=BF16)


def _wkv_reduce_table():
    Tc = WKV_CHUNK
    kh = np.arange(RW_WIDTH) // RW_HEAD
    n = np.arange(RW_HEADS * Tc)
    nh, nt = n // Tc, n % Tc
    r = (kh[None, :, None] == nh[None, None, :]) & (nt[None, None, :] == np.arange(Tc)[:, None, None])
    return jnp.asarray(r.astype(np.float32), dtype=BF16)


def _inproj_fwd(x, norm_g, w_b):
    T = x.shape[0]
    tm = min(T, 256)

    def body(x_ref, g_ref, w_ref, pret_ref, prw_ref, u_ref):
        xf = x_ref[...]
        rstd = lax.rsqrt(jnp.mean(xf * xf, axis=-1, keepdims=True) + RMS_EPS)
        ub = ((xf * rstd) * g_ref[...]).astype(BF16)
        u_ref[...] = ub
        pret_ref[...] = _dot(ub, w_ref[:, :RET_COLS])
        prw_ref[...] = _dot(ub, w_ref[:, RET_COLS:])

    return pl.pallas_call(
        body, name="inproj_fwd", grid=(T // tm,),
        in_specs=[pl.BlockSpec((tm, D_MODEL), lambda i: (i, 0)), _full((1, D_MODEL)), _full((D_MODEL, IN_COLS))],
        out_specs=[pl.BlockSpec((tm, RET_COLS), lambda i: (i, 0)), pl.BlockSpec((tm, RW_COLS), lambda i: (i, 0)),
                   pl.BlockSpec((tm, D_MODEL), lambda i: (i, 0))],
        out_shape=[jax.ShapeDtypeStruct((T, RET_COLS), F32), jax.ShapeDtypeStruct((T, RW_COLS), F32),
                   jax.ShapeDtypeStruct((T, D_MODEL), BF16)],
        compiler_params=_cparams(dimension_semantics=("arbitrary",)),
    )(x, norm_g, w_b)


def _rot_half(x):
    n = x.shape[1]
    lane = lax.broadcasted_iota(jnp.int32, x.shape, 1)
    first = (lane % RET_DK) < (RET_DK // 2)
    return jnp.where(first, -pltpu.roll(x, n - RET_DK // 2, 1), pltpu.roll(x, RET_DK // 2, 1))


def _rope(x, cos, sin):
    return x * cos + _rot_half(x) * sin


def _rope_bwd(d, cos, sin):
    return d * cos - _rot_half(d * sin)


def _ret_post(ret, g, gn_g, seg):
    mu = _segsum(ret, seg) * (1.0 / RET_DV)
    xc = ret - mu
    var = _segsum(xc * xc, seg) * (1.0 / RET_DV)
    n = xc * lax.rsqrt(var + RET_GN_EPS)
    return (g * jax.nn.sigmoid(g)) * (n * gn_g)


def _ret_scores(qt, kt, d_ref, h):
    lane = lax.broadcasted_iota(jnp.int32, qt.shape, 1)
    qh = jnp.where(lane // RET_DK == h, qt, 0.0)
    return qh, _dot_nt(qh, kt, HIGHEST) * d_ref[h]


def _ret_group(nch):
    return min(RET_GROUP, nch)


def _ret_fwd(p_ret, cos, sin, tabs, gn_g, seg128):
    T = p_ret.shape[0]
    C = RET_CHUNK
    nch = T // C
    G = _ret_group(nch)
    intra_d, qd, kd, cd, bm = tabs

    def body(q_ref, k_ref, v_ref, g_ref, cos_ref, sin_ref, qd_ref, kd_ref, d_ref, cd_ref, bm_ref, gn_ref, seg_ref,
             y_ref, ret_ref, sin_out_ref, s_ref):
        @pl.when(pl.program_id(0) == 0)
        def _():
            s_ref[...] = jnp.zeros_like(s_ref)

        s_in = s_ref[...]
        for i in range(G):
            rows = slice(i * C, (i + 1) * C)
            cosv, sinv = cos_ref[rows, :], sin_ref[rows, :]
            qt = _rope(q_ref[rows, :], cosv, sinv)
            kt = _rope(k_ref[rows, :], cosv, sinv) * (RET_DK ** -0.5)
            v = v_ref[rows, :]
            sin_out_ref[i] = s_in
            inter = _dot(qt * qd_ref[...], s_in, HIGHEST)
            intra = []
            for h in range(RET_HEADS):
                _, a = _ret_scores(qt, kt, d_ref, h)
                intra.append(_dot(a, v[:, h * RET_DV:(h + 1) * RET_DV], HIGHEST))
            ret_ref[rows, :] = jnp.concatenate(intra, axis=1) + inter
            kv = _dot_tn(kt * kd_ref[...], v, HIGHEST)
            s_in = s_in * cd_ref[...] + kv * bm_ref[...]
        s_ref[...] = s_in
        y_ref[...] = _ret_post(ret_ref[...], g_ref[...], gn_ref[...], seg_ref[...]).astype(BF16)

    GC = G * C
    return pl.pallas_call(
        body, name="ret_fwd", grid=(nch // G,),
        in_specs=[pl.BlockSpec((GC, RET_QK), lambda c: (c, 0)), pl.BlockSpec((GC, RET_QK), lambda c: (c, 1)),
                  pl.BlockSpec((GC, RET_WIDTH), lambda c: (c, 1)), pl.BlockSpec((GC, RET_WIDTH), lambda c: (c, 2)),
                  pl.BlockSpec((GC, RET_QK), lambda c: (c, 0)), pl.BlockSpec((GC, RET_QK), lambda c: (c, 0)),
                  _full((C, RET_QK)), _full((C, RET_QK)), _full((RET_HEADS, C, C)),
                  _full((RET_QK, RET_WIDTH)), _full((RET_QK, RET_WIDTH)), _full((1, RET_WIDTH)),
                  _full((RET_WIDTH, RET_WIDTH))],
        out_specs=[pl.BlockSpec((GC, RET_WIDTH), lambda c: (c, 0)), pl.BlockSpec((GC, RET_WIDTH), lambda c: (c, 0)),
                   pl.BlockSpec((G, RET_QK, RET_WIDTH), lambda c: (c, 0, 0))],
        out_shape=[jax.ShapeDtypeStruct((T, RET_WIDTH), BF16), jax.ShapeDtypeStruct((T, RET_WIDTH), F32),
                   jax.ShapeDtypeStruct((nch, RET_QK, RET_WIDTH), F32)],
        scratch_shapes=[pltpu.VMEM((RET_QK, RET_WIDTH), F32)],
        compiler_params=_cparams(dimension_semantics=("arbitrary",)),
    )(p_ret, p_ret, p_ret, p_ret, cos, sin, qd, kd, intra_d, cd, bm, gn_g, seg128)


def _ret_bwd(p_ret, cos, sin, tabs, gn_g, seg128, ret, s_in_all, dy):
    T = p_ret.shape[0]
    C = RET_CHUNK
    nch = T // C
    G = _ret_group(nch)
    ngr = nch // G
    intra_d, qd, kd, cd, bm = tabs

    def rev(j):
        return lambda c: (ngr - 1 - c, j)

    def body(q_ref, k_ref, v_ref, g_ref, cos_ref, sin_ref, qd_ref, kd_ref, d_ref, cd_ref, bm_ref, gn_ref, seg_ref,
             ret_ref, sin_ref_, dy_ref, dp_ref, dgn_ref, ds_ref):
        @pl.when(pl.program_id(0) == 0)
        def _():
            ds_ref[...] = jnp.zeros_like(ds_ref)
            dgn_ref[...] = jnp.zeros_like(dgn_ref)

        seg = seg_ref[...]
        _, post_vjp = jax.vjp(lambda r_, g_, gn_: _ret_post(r_, g_, gn_, seg), ret_ref[...], g_ref[...], gn_ref[...])
        dret_all, dg_all, dgn = post_vjp(dy_ref[...])
        dgn_ref[...] += dgn
        dp_ref[:, 2 * RET_QK + RET_WIDTH:] = dg_all

        qdv, kdv = qd_ref[...], kd_ref[...]
        ds_out = ds_ref[...]
        for i in reversed(range(G)):
            rows = slice(i * C, (i + 1) * C)
            cosv, sinv = cos_ref[rows, :], sin_ref[rows, :]
            qt = _rope(q_ref[rows, :], cosv, sinv)
            kt = _rope(k_ref[rows, :], cosv, sinv) * (RET_DK ** -0.5)
            v = v_ref[rows, :]
            s_in = sin_ref_[i]
            dret = dret_all[rows, :]
            dqt = qdv * _dot_nt(dret, s_in, HIGHEST)
            dkt = kdv * _dot_nt(v, ds_out, HIGHEST)
            dv_all = _dot(kt * kdv, ds_out, HIGHEST)
            dvs = []
            for h in range(RET_HEADS):
                sl = slice(h * RET_DV, (h + 1) * RET_DV)
                qh, a = _ret_scores(qt, kt, d_ref, h)
                lane = lax.broadcasted_iota(jnp.int32, kt.shape, 1)
                kh = jnp.where(lane // RET_DK == h, kt, 0.0)
                da = _dot_nt(dret[:, sl], v[:, sl], HIGHEST) * d_ref[h]
                dvs.append(_dot_tn(a, dret[:, sl], HIGHEST))
                dqt = dqt + _dot(da, kh, HIGHEST)
                dkt = dkt + _dot_tn(da, qh, HIGHEST)
            ds_out = ds_out * cd_ref[...] + _dot_tn(qt * qdv, dret, HIGHEST) * bm_ref[...]
            dp_ref[rows, :RET_QK] = _rope_bwd(dqt, cosv, sinv)
            dp_ref[rows, RET_QK:2 * RET_QK] = _rope_bwd(dkt * (RET_DK ** -0.5), cosv, sinv)
            dp_ref[rows, 2 * RET_QK:2 * RET_QK + RET_WIDTH] = dv_all + jnp.concatenate(dvs, axis=1)
        ds_ref[...] = ds_out

    GC = G * C
    return pl.pallas_call(
        body, name="ret_bwd", grid=(ngr,),
        in_specs=[pl.BlockSpec((GC, RET_QK), rev(0)), pl.BlockSpec((GC, RET_QK), rev(1)),
                  pl.BlockSpec((GC, RET_WIDTH), rev(1)), pl.BlockSpec((GC, RET_WIDTH), rev(2)),
                  pl.BlockSpec((GC, RET_QK), rev(0)), pl.BlockSpec((GC, RET_QK), rev(0)),
                  _full((C, RET_QK)), _full((C, RET_QK)), _full((RET_HEADS, C, C)),
                  _full((RET_QK, RET_WIDTH)), _full((RET_QK, RET_WIDTH)), _full((1, RET_WIDTH)),
                  _full((RET_WIDTH, RET_WIDTH)),
                  pl.BlockSpec((GC, RET_WIDTH), rev(0)),
                  pl.BlockSpec((G, RET_QK, RET_WIDTH), lambda c: (ngr - 1 - c, 0, 0)),
                  pl.BlockSpec((GC, RET_WIDTH), rev(0))],
        out_specs=[pl.BlockSpec((GC, RET_COLS), rev(0)), _full((1, RET_WIDTH))],
        out_shape=[jax.ShapeDtypeStruct((T, RET_COLS), F32), jax.ShapeDtypeStruct((1, RET_WIDTH), F32)],
        scratch_shapes=[pltpu.VMEM((RET_QK, RET_WIDTH), F32)],
        compiler_params=_cparams(dimension_semantics=("arbitrary",)),
    )(p_ret, p_ret, p_ret, p_ret, cos, sin, qd, kd, intra_d, cd, bm, gn_g, seg128, ret, s_in_all, dy)


@jax.custom_vjp
def _chunk_sums(x, tri):
    hi, lo = _split(x)
    return _dot(tri, hi) + _dot(tri, lo)


def _chunk_sums_fwd(x, tri):
    return _chunk_sums(x, tri), tri


def _chunk_sums_bwd(tri, ct):
    hi, lo = _split(ct)
    return _dot_tn(tri, hi) + _dot_tn(tri, lo), jnp.zeros_like(tri)


_chunk_sums.defvjp(_chunk_sums_fwd, _chunk_sums_bwd)


def _chunk_tables(tm):
    t = np.arange(tm)
    same = (t[:, None] // WKV_CHUNK) == (t[None, :] // WKV_CHUNK)
    return jnp.asarray(np.stack([same & (t[None, :] <= t[:, None]), same]).astype(np.float32), dtype=BF16)


def _prep_fn(p, prev, mu, w0, a0, k_k, k_a, lora, seg, tri):
    W = RW_WIDTH
    ps = p + mu * (prev - p)
    r, kr, vr, g = ps[:, 0:W], ps[:, W:2 * W], ps[:, 2 * W:3 * W], ps[:, 3 * W:4 * W]
    z = ps[:, 4 * W:]
    lane = lax.broadcasted_iota(jnp.int32, z.shape, 1)
    z = jnp.where(lane < LORA, jnp.tanh(z), z)
    lo = _dot(z, lora, HIGHEST)
    w_log = -_softplus(-(w0 + lo[:, :W])) - 0.5
    log_decay = -jnp.exp(w_log)
    cum = _chunk_sums(log_decay, tri[0])
    total = _chunk_sums(log_decay, tri[1])
    a = jax.nn.sigmoid(a0 + lo[:, W:])
    kk = kr * k_k
    kk = kk / jnp.maximum(jnp.sqrt(_segsum(kk * kk, seg)), 1e-12)
    k = kr * (1.0 + (a - 1.0) * k_a)
    grow = jnp.exp(-cum)
    return kk * jnp.exp(cum - log_decay), (kk * a) * grow, k * grow, r * jnp.exp(cum), jnp.exp(total), vr, g


def _post_fn(o, r, k, v, g, gn_g, gn_b, r_k, seg):
    mu = _segsum(o, seg) * (1.0 / RW_HEAD)
    oc = o - mu
    var = _segsum(oc * oc, seg) * (1.0 / RW_HEAD)
    on = oc * lax.rsqrt(var + RW_GN_EPS) * gn_g + gn_b
    bonus = _segsum(r * k * r_k, seg) * v
    return (g * jax.nn.sigmoid(g)) * (on + bonus)


def _shift_down(p, first_row):
    row = lax.broadcasted_iota(jnp.int32, p.shape, 0)
    return jnp.where(row == 0, first_row, pltpu.roll(p, 1, 0))


def _shift_up(p, last_row):
    n = p.shape[0]
    row = lax.broadcasted_iota(jnp.int32, p.shape, 0)
    return jnp.where(row == n - 1, last_row, pltpu.roll(p, n - 1, 0))


def _row_tile(T):
    return min(T, 256)


def _prep_fwd(p_rw, bnd, mu, w0, a0, k_k, k_a, lora, seg64, tri):
    T = p_rw.shape[0]
    tm = _row_tile(T)
    W = RW_WIDTH

    def body(p_ref, bnd_ref, mu_ref, w0_ref, a0_ref, kk_ref, ka_ref, lora_ref, seg_ref, tri_ref,
             vecs_ref, v_ref, g_ref):
        p = p_ref[...]
        prev = _shift_down(p, bnd_ref[0])
        res = _prep_fn(p, prev, mu_ref[...], w0_ref[...], a0_ref[...], kk_ref[...], ka_ref[...], lora_ref[...],
                       seg_ref[...], (tri_ref[0], tri_ref[1]))
        for j in range(N_VEC):
            vecs_ref[j] = res[j]
        v_ref[...] = res[N_VEC]
        g_ref[...] = res[N_VEC + 1]

    small = _full((1, W))
    row = pl.BlockSpec((tm, W), lambda i: (i, 0))
    return pl.pallas_call(
        body, name="rwkv_prep_fwd", grid=(T // tm,),
        in_specs=[pl.BlockSpec((tm, RW_COLS), lambda i: (i, 0)), pl.BlockSpec((1, 1, RW_COLS), lambda i: (i, 0, 0)),
                  _full((1, RW_COLS)), small, small, small, small, _full((2 * LORA, 2 * W)), _full((W, W)),
                  _full((2, tm, tm))],
        out_specs=[pl.BlockSpec((N_VEC, tm, W), lambda i: (0, i, 0)), row, row],
        out_shape=[jax.ShapeDtypeStruct((N_VEC, T, W), F32), jax.ShapeDtypeStruct((T, W), F32),
                   jax.ShapeDtypeStruct((T, W), F32)],
        compiler_params=_cparams(dimension_semantics=("arbitrary",)),
    )(p_rw, bnd, mu, w0, a0, k_k, k_a, lora, seg64, tri)


def _prep_bwd(p_rw, bnd, mu, w0, a0, k_k, k_a, lora, seg64, tri, cts):
    T = p_rw.shape[0]
    tm = _row_tile(T)
    W = RW_WIDTH

    def body(p_ref, bnd_ref, mu_ref, w0_ref, a0_ref, kk_ref, ka_ref, lora_ref, seg_ref, tri_ref,
             dvecs_ref, dk2_ref, dr2_ref, dv1_ref, dv2_ref, dg_ref,
             dp_ref, dprev_ref, dmu_ref, dw0_ref, da0_ref, dkk_p_ref, dka_ref, dlora_ref):
        accs = (dmu_ref, dw0_ref, da0_ref, dkk_p_ref, dka_ref, dlora_ref)

        @pl.when(pl.program_id(0) == 0)
        def _():
            for a_ref in accs:
                a_ref[...] = jnp.zeros_like(a_ref)

        p = p_ref[...]
        prev = _shift_down(p, bnd_ref[0])
        seg, tri = seg_ref[...], (tri_ref[0], tri_ref[1])
        _, vjp = jax.vjp(lambda *a: _prep_fn(*a, seg, tri), p, prev, mu_ref[...], w0_ref[...], a0_ref[...],
                         kk_ref[...], ka_ref[...], lora_ref[...])
        ct = (dvecs_ref[0], dvecs_ref[1], dvecs_ref[2] + dk2_ref[...], dvecs_ref[3] + dr2_ref[...], dvecs_ref[4],
              dv1_ref[...] + dv2_ref[...], dg_ref[...])
        grads = vjp(ct)
        dp_ref[...] = grads[0]
        dprev_ref[...] = grads[1]
        for a_ref, gval in zip(accs, grads[2:]):
            a_ref[...] += gval

    small = _full((1, W))
    row = pl.BlockSpec((tm, W), lambda i: (i, 0))
    return pl.pallas_call(
        body, name="rwkv_prep_bwd", grid=(T // tm,),
        in_specs=[pl.BlockSpec((tm, RW_COLS), lambda i: (i, 0)), pl.BlockSpec((1, 1, RW_COLS), lambda i: (i, 0, 0)),
                  _full((1, RW_COLS)), small, small, small, small, _full((2 * LORA, 2 * W)), _full((W, W)),
                  _full((2, tm, tm)), pl.BlockSpec((N_VEC, tm, W), lambda i: (0, i, 0))] + [row] * 5,
        out_specs=[pl.BlockSpec((tm, RW_COLS), lambda i: (i, 0)), pl.BlockSpec((tm, RW_COLS), lambda i: (i, 0)),
                   _full((1, RW_COLS)), small, small, small, small, _full((2 * LORA, 2 * W))],
        out_shape=[jax.ShapeDtypeStruct((T, RW_COLS), F32), jax.ShapeDtypeStruct((T, RW_COLS), F32),
                   jax.ShapeDtypeStruct((1, RW_COLS), F32)] + [jax.ShapeDtypeStruct((1, W), F32)] * 4
                  + [jax.ShapeDtypeStruct((2 * LORA, 2 * W), F32)],
        compiler_params=_cparams(dimension_semantics=("arbitrary",)),
    )(p_rw, bnd, mu, w0, a0, k_k, k_a, lora, seg64, tri, *cts)


def _vec_spec(j, tm):
    return pl.BlockSpec((None, tm, RW_WIDTH), lambda i: (j, i, 0))


def _post_fwd(o, vecs, v, g, gn_g, gn_b, r_k, seg64):
    T = o.shape[0]
    tm = _row_tile(T)
    W = RW_WIDTH

    def body(o_ref, r_ref, k_ref, v_ref, g_ref, gg_ref, gb_ref, rk_ref, seg_ref, y_ref):
        y_ref[...] = _post_fn(o_ref[...], r_ref[...], k_ref[...], v_ref[...], g_ref[...], gg_ref[...], gb_ref[...],
                              rk_ref[...], seg_ref[...]).astype(BF16)

    row = pl.BlockSpec((tm, W), lambda i: (i, 0))
    small = _full((1, W))
    return pl.pallas_call(
        body, name="rwkv_post_fwd", grid=(T // tm,),
        in_specs=[row, _vec_spec(3, tm), _vec_spec(2, tm), row, row] + [small] * 3 + [_full((W, W))],
        out_specs=row, out_shape=jax.ShapeDtypeStruct((T, W), BF16),
        compiler_params=_cparams(dimension_semantics=("arbitrary",)),
    )(o, vecs, vecs, v, g, gn_g, gn_b, r_k, seg64)


def _post_bwd(o, vecs, v, g, gn_g, gn_b, r_k, seg64, dy):
    T = o.shape[0]
    tm = _row_tile(T)
    W = RW_WIDTH

    def body(o_ref, r_ref, k_ref, v_ref, g_ref, gg_ref, gb_ref, rk_ref, seg_ref, dy_ref,
             do_ref, dr_ref, dk_ref, dv_ref, dg_ref, dgg_ref, dgb_ref, drk_ref):
        accs = (dgg_ref, dgb_ref, drk_ref)

        @pl.when(pl.program_id(0) == 0)
        def _():
            for a_ref in accs:
                a_ref[...] = jnp.zeros_like(a_ref)

        seg = seg_ref[...]
        _, vjp = jax.vjp(lambda *a: _post_fn(*a, seg), o_ref[...], r_ref[...], k_ref[...], v_ref[...], g_ref[...],
                         gg_ref[...], gb_ref[...], rk_ref[...])
        grads = vjp(dy_ref[...])
        for o_, gval in zip((do_ref, dr_ref, dk_ref, dv_ref, dg_ref), grads[:5]):
            o_[...] = gval
        for a_ref, gval in zip(accs, grads[5:]):
            a_ref[...] += gval

    row = pl.BlockSpec((tm, W), lambda i: (i, 0))
    small = _full((1, W))
    return pl.pallas_call(
        body, name="rwkv_post_bwd", grid=(T // tm,),
        in_specs=[row, _vec_spec(3, tm), _vec_spec(2, tm), row, row] + [small] * 3
                 + [_full((W, W)), pl.BlockSpec((tm, W), lambda i: (i, 1))],
        out_specs=[row] * 5 + [small] * 3,
        out_shape=[jax.ShapeDtypeStruct((T, W), F32)] * 5 + [jax.ShapeDtypeStruct((1, W), F32)] * 3,
        compiler_params=_cparams(dimension_semantics=("arbitrary",)),
    )(o, vecs, vecs, v, g, gn_g, gn_b, r_k, seg64, dy)


def _to_cols(vecs, T):
    Tc = WKV_CHUNK
    x = vecs.reshape(N_VEC, T // Tc, Tc, RW_HEADS, RW_HEAD)
    x = jnp.transpose(x, (1, 0, 4, 3, 2))
    return x.reshape(T // Tc, N_VEC * RW_HEAD, RW_HEADS * Tc)


def _from_cols(cols, T):
    Tc = WKV_CHUNK
    x = cols.reshape(T // Tc, N_VEC, RW_HEAD, RW_HEADS, Tc)
    return jnp.transpose(x, (1, 0, 4, 3, 2)).reshape(N_VEC, T, RW_WIDTH)


def _wkv_lhs(cols_ref):
    hi, lo = _split(cols_ref[0])
    return jnp.concatenate([hi, lo], axis=1)


N_STEP_VEC = N_VEC - 1


def _wkv_fwd(cols, v, e_tab):
    T = v.shape[0]
    Tc = WKV_CHUNK
    nch = T // Tc
    J, W = RW_HEAD, RW_WIDTH
    JS = N_STEP_VEC * J

    def body(cols_ref, v_ref, e_ref, o_ref, states_ref, sa_ref, s_ref):
        @pl.when(pl.program_id(0) == 0)
        def _():
            s_ref[...] = jnp.zeros_like(s_ref)

        lhs = _wkv_lhs(cols_ref)
        st = s_ref[...]
        for t in range(Tc):
            ex = _dot(lhs[:JS], e_ref[t])
            states_ref[t] = st
            sa = -jnp.sum(st * ex[0:J], axis=0, keepdims=True)
            st = st + ex[J:2 * J] * sa + ex[2 * J:3 * J] * v_ref[t:t + 1, :]
            sa_ref[t:t + 1, :] = sa
            o_ref[t:t + 1, :] = jnp.sum(st * ex[3 * J:4 * J], axis=0, keepdims=True)
        s_ref[...] = st * _dot(lhs[JS:], e_ref[Tc - 1])

    return pl.pallas_call(
        body, name="wkv_fwd", grid=(nch,),
        in_specs=[pl.BlockSpec((1, N_VEC * J, 128), lambda c: (c, 0, 0)), pl.BlockSpec((Tc, W), lambda c: (c, 0)),
                  _full((Tc, 2 * 128, W))],
        out_specs=[pl.BlockSpec((Tc, W), lambda c: (c, 0)), pl.BlockSpec((Tc, J, W), lambda c: (c, 0, 0)),
                   pl.BlockSpec((Tc, W), lambda c: (c, 0))],
        out_shape=[jax.ShapeDtypeStruct((T, W), F32), jax.ShapeDtypeStruct((T, J, W), F32),
                   jax.ShapeDtypeStruct((T, W), F32)],
        scratch_shapes=[pltpu.VMEM((J, W), F32)],
        compiler_params=_cparams(dimension_semantics=("arbitrary",)),
    )(cols, v, e_tab)


def _wkv_bwd(cols, v, do, states, sa, e_tab, r_tab):
    T = v.shape[0]
    Tc = WKV_CHUNK
    nch = T // Tc
    J, W = RW_HEAD, RW_WIDTH
    JS = N_STEP_VEC * J
    blocks = [slice(b * 128, (b + 1) * 128) for b in range(W // 128)]

    def body(cols_ref, v_ref, do_ref, states_ref, sa_ref, e_ref, r_ref, dv_ref, dcols_ref, ds_ref):
        @pl.when(pl.program_id(0) == 0)
        def _():
            ds_ref[...] = jnp.zeros_like(ds_ref)

        lhs = _wkv_lhs(cols_ref)
        last = Tc - 1
        ex = _dot(lhs, e_ref[last])
        dst, ends = [], []
        for b in blocks:
            s_end = (states_ref[last, :, b] + ex[J:2 * J, b] * sa_ref[last:last + 1, b]
                     + ex[2 * J:3 * J, b] * v_ref[last:last + 1, b])
            d_end = ds_ref[:, b]
            ends.append((d_end * s_end).astype(BF16))
            dst.append(d_end * ex[JS:, b])
        dcols_ref[0, JS:, :] = _dot(jnp.concatenate(ends, axis=1), r_ref[last])
        acc = jnp.zeros((JS, 128), F32)
        for t in reversed(range(Tc)):
            if t != last:
                ex = _dot(lhs[:JS], e_ref[t])
            dvs, prods = [], []
            for i, b in enumerate(blocks):
                kk_e, b_e, k_e, r_e = (ex[n * J:(n + 1) * J, b] for n in range(N_STEP_VEC))
                do_row, v_row, sa_row = do_ref[t:t + 1, b], v_ref[t:t + 1, b], sa_ref[t:t + 1, b]
                s_old = states_ref[t, :, b]
                s_new = states_ref[t + 1, :, b] if t != last else s_old + b_e * sa_row + k_e * v_row
                dsn = dst[i] + r_e * do_row
                dsa = jnp.sum(dsn * b_e, axis=0, keepdims=True)
                dvs.append(jnp.sum(dsn * k_e, axis=0, keepdims=True))
                prods.append(jnp.concatenate(
                    [s_old * (-dsa), dsn * sa_row, dsn * v_row, s_new * do_row], axis=0).astype(BF16))
                dst[i] = dsn - kk_e * dsa
            dv_ref[t:t + 1, :] = jnp.concatenate(dvs, axis=1)
            acc = acc + _dot(jnp.concatenate(prods, axis=1), r_ref[t])
        for i, b in enumerate(blocks):
            ds_ref[:, b] = dst[i]
        dcols_ref[0, :JS, :] = acc

    rev2 = lambda c: (nch - 1 - c, 0)
    rev3 = lambda c: (nch - 1 - c, 0, 0)
    return pl.pallas_call(
        body, name="wkv_bwd", grid=(nch,),
        in_specs=[pl.BlockSpec((1, N_VEC * J, 128), rev3), pl.BlockSpec((Tc, W), rev2), pl.BlockSpec((Tc, W), rev2),
                  pl.BlockSpec((Tc, J, W), rev3), pl.BlockSpec((Tc, W), rev2),
                  _full((Tc, 2 * 128, W)), _full((Tc, W, 128))],
        out_specs=[pl.BlockSpec((Tc, W), rev2), pl.BlockSpec((1, N_VEC * J, 128), rev3)],
        out_shape=[jax.ShapeDtypeStruct((T, W), F32), jax.ShapeDtypeStruct((nch, N_VEC * J, 128), F32)],
        scratch_shapes=[pltpu.VMEM((J, W), F32)],
        compiler_params=_cparams(dimension_semantics=("arbitrary",)),
    )(cols, v, do, states, sa, e_tab, r_tab)


def _outproj(x, y_ret, y_rw, w_out_b, target, gf):
    T = x.shape[0]
    tm = _row_tile(T)
    W = RW_WIDTH

    def body(x_ref, yr_ref, yw_ref, w_ref, t_ref, gf_ref, loss_ref, dh_ref, dy_ref, dw_ref, dgf_ref):
        @pl.when(pl.program_id(0) == 0)
        def _():
            loss_ref[...] = jnp.zeros_like(loss_ref)
            dw_ref[...] = jnp.zeros_like(dw_ref)
            dgf_ref[...] = jnp.zeros_like(dgf_ref)

        y = jnp.concatenate([yr_ref[...], yw_ref[...]], axis=1)
        w = w_ref[...]
        h = x_ref[...] + _dot(y, w)
        rstd = lax.rsqrt(jnp.mean(h * h, axis=-1, keepdims=True) + RMS_EPS)
        hn = h * rstd
        gfv = gf_ref[...]
        err = hn * gfv - t_ref[...]
        loss_ref[...] += 0.5 * jnp.sum(jnp.mean(err * err, axis=-1))
        dout = err * (1.0 / D_MODEL)
        dgf_ref[...] += jnp.sum(dout * hn, axis=0, keepdims=True)
        dhn = dout * gfv
        dh = rstd * (dhn - hn * jnp.mean(dhn * hn, axis=-1, keepdims=True))
        dh_ref[...] = dh
        dhb = dh.astype(BF16)
        dy_ref[...] = _dot_nt(dhb, w)
        dw_ref[...] += _dot_tn(y, dhb)

    return pl.pallas_call(
        body, name="outproj_loss", grid=(T // tm,),
        in_specs=[pl.BlockSpec((tm, D_MODEL), lambda i: (i, 0)), pl.BlockSpec((tm, W), lambda i: (i, 0)),
                  pl.BlockSpec((tm, W), lambda i: (i, 0)), _full((D_MODEL, D_MODEL)),
                  pl.BlockSpec((tm, D_MODEL), lambda i: (i, 0)), _full((1, D_MODEL))],
        out_specs=[_full((1, PACK_W)), pl.BlockSpec((tm, D_MODEL), lambda i: (i, 0)),
                   pl.BlockSpec((tm, D_MODEL), lambda i: (i, 0)), _full((D_MODEL, D_MODEL)), _full((1, D_MODEL))],
        out_shape=[jax.ShapeDtypeStruct((1, PACK_W), F32), jax.ShapeDtypeStruct((T, D_MODEL), F32),
                   jax.ShapeDtypeStruct((T, D_MODEL), F32), jax.ShapeDtypeStruct((D_MODEL, D_MODEL), F32),
                   jax.ShapeDtypeStruct((1, D_MODEL), F32)],
        compiler_params=_cparams(dimension_semantics=("arbitrary",)),
    )(x, y_ret, y_rw, w_out_b, target, gf)


def _inproj_bwd_x(dp_ret, dp_rw, dprev, dbnd, w_b, x, norm_g, dh):
    T = x.shape[0]
    tm = _row_tile(T)

    def body(dpr_ref, dpw_ref, dprev_ref, dbnd_ref, w_ref, x_ref, g_ref, dh_ref, gx_ref, dg_ref, dpb_ref):
        @pl.when(pl.program_id(0) == 0)
        def _():
            dg_ref[...] = jnp.zeros_like(dg_ref)

        d_rw = dpw_ref[...] + _shift_up(dprev_ref[...], dbnd_ref[0])
        dpb = jnp.concatenate([dpr_ref[...].astype(BF16), d_rw.astype(BF16)], axis=1)
        dpb_ref[...] = dpb
        du = _dot_nt(dpb, w_ref[...])
        xf = x_ref[...]
        rstd = lax.rsqrt(jnp.mean(xf * xf, axis=-1, keepdims=True) + RMS_EPS)
        xn = xf * rstd
        dg_ref[...] += jnp.sum(du * xn, axis=0, keepdims=True)
        dxn = du * g_ref[...]
        gx_ref[...] = dh_ref[...] + rstd * (dxn - xn * jnp.mean(dxn * xn, axis=-1, keepdims=True))

    return pl.pallas_call(
        body, name="inproj_bwd_x", grid=(T // tm,),
        in_specs=[pl.BlockSpec((tm, RET_COLS), lambda i: (i, 0)), pl.BlockSpec((tm, RW_COLS), lambda i: (i, 0)),
                  pl.BlockSpec((tm, RW_COLS), lambda i: (i, 0)), pl.BlockSpec((1, 1, RW_COLS), lambda i: (i, 0, 0)),
                  _full((D_MODEL, IN_COLS)), pl.BlockSpec((tm, D_MODEL), lambda i: (i, 0)), _full((1, D_MODEL)),
                  pl.BlockSpec((tm, D_MODEL), lambda i: (i, 0))],
        out_specs=[pl.BlockSpec((tm, D_MODEL), lambda i: (i, 0)), _full((1, D_MODEL)),
                   pl.BlockSpec((tm, IN_COLS), lambda i: (i, 0))],
        out_shape=[jax.ShapeDtypeStruct((T, D_MODEL), F32), jax.ShapeDtypeStruct((1, D_MODEL), F32),
                   jax.ShapeDtypeStruct((T, IN_COLS), BF16)],
        compiler_params=_cparams(dimension_semantics=("arbitrary",)),
    )(dp_ret, dp_rw, dprev, dbnd, w_b, x, norm_g, dh)


def _inproj_bwd_w(u_t, dpb):
    T = u_t.shape[1]
    tr = 256

    def body(u_ref, d_ref, o_ref):
        o_ref[...] = _dot(u_ref[...], d_ref[...])

    return pl.pallas_call(
        body, name="inproj_bwd_w", grid=(D_MODEL // tr,),
        in_specs=[pl.BlockSpec((tr, T), lambda i: (i, 0)), _full((T, IN_COLS))],
        out_specs=pl.BlockSpec((tr, IN_COLS), lambda i: (i, 0)),
        out_shape=jax.ShapeDtypeStruct((D_MODEL, IN_COLS), F32),
        compiler_params=_cparams(dimension_semantics=("arbitrary",)),
    )(u_t, dpb)


def _tile_boundaries(a, tm, first):
    T, n = a.shape
    zero = jnp.zeros((1, n), a.dtype)
    if first:
        rows = jnp.concatenate([zero, a[tm - 1:T - 1:tm]], axis=0)
    else:
        rows = jnp.concatenate([a[tm:T:tm], zero], axis=0)
    return rows.reshape(T // tm, 1, n)


def _local_step(x, target, w_in_b, w_out_b, lora, small):
    T = x.shape[0]
    tm = _row_tile(T)
    cos, sin = _rope_tables(T)
    tabs = _ret_tables()
    seg128 = _seg_matrix(RET_WIDTH, RET_DV)
    seg64 = _seg_matrix(RW_WIDTH, RW_HEAD)
    e_tab = _wkv_expand_table()
    r_tab = _wkv_reduce_table()
    prep_w = (small["rwkv_mu"], small["w0"], small["a0"], small["k_k"], small["k_a"], lora, seg64, _chunk_tables(tm))
    post_w = (small["rwkv_gn_g"], small["rwkv_gn_b"], small["r_k"], seg64)

    p_ret, p_rw, u = _inproj_fwd(x, small["norm_g"], w_in_b)
    y_ret, ret, s_in_all = _ret_fwd(p_ret, cos, sin, tabs, small["ret_gn_g"], seg128)
    bnd = _tile_boundaries(p_rw, tm, True)
    vecs, v, g = _prep_fwd(p_rw, bnd, *prep_w)
    cols = _to_cols(vecs, T)
    o, states, sa = _wkv_fwd(cols, v, e_tab)
    y_rw = _post_fwd(o, vecs, v, g, *post_w)
    loss, dh, dy, d_w_out, d_gf = _outproj(x, y_ret, y_rw, w_out_b, target, small["final_norm_g"])

    do, dr2, dk2, dv2, dg, d_gn_g, d_gn_b, d_r_k = _post_bwd(o, vecs, v, g, *post_w, dy)
    dv1, dcols = _wkv_bwd(cols, v, do, states, sa, e_tab, r_tab)
    dp_rw, dprev, d_mu, d_w0, d_a0, d_k_k, d_k_a, d_lora = _prep_bwd(
        p_rw, bnd, *prep_w, (_from_cols(dcols, T), dk2, dr2, dv1, dv2, dg))
    dp_ret, d_ret_gn = _ret_bwd(p_ret, cos, sin, tabs, small["ret_gn_g"], seg128, ret, s_in_all, dy)
    dbnd = _tile_boundaries(dprev, tm, False)
    grad_x, d_norm_g, dpb = _inproj_bwd_x(dp_ret, dp_rw, dprev, dbnd, w_in_b, x, small["norm_g"], dh)
    d_w_in = _inproj_bwd_w(jnp.transpose(u), dpb)

    d_small = {"norm_g": d_norm_g, "ret_gn_g": d_ret_gn, "rwkv_mu": d_mu, "w0": d_w0, "a0": d_a0, "k_k": d_k_k,
               "k_a": d_k_a, "r_k": d_r_k, "rwkv_gn_g": d_gn_g, "rwkv_gn_b": d_gn_b, "final_norm_g": d_gf}
    return loss, grad_x, d_w_in, d_w_out, d_lora, d_small


ANY = pl.BlockSpec(memory_space=pl.ANY)
CHIP_FLIPS = ((0, 1), (1, 0), (1, 1))
N_FLIPS = len(CHIP_FLIPS)
LORA_SHARD = RW_WIDTH // N_CHIPS
HALF_IN = D_MODEL // 2
HALF_OUT = OUT_SHARD // 2


def _position():
    return lax.axis_index("x"), lax.axis_index("y"), lax.axis_index("c")


def _flip(v, f):
    return 1 - v if f else v


def _finish(local, remote, landed):
    for cp in landed:
        cp.wait_recv()
    for cp in remote:
        cp.wait_send()
    for cp in local:
        cp.wait()


def _gather_chips(arrs):
    n = len(arrs)

    def body(*refs):
        ins, outs = refs[:n], refs[n:2 * n]
        send, recv, pass_send, pass_recv = refs[2 * n:]
        x, y, c = _position()
        s = 2 * x + y
        sibling = (x, y, 1 - c)

        def copy(src, dst, sems, k, to):
            return pltpu.make_async_remote_copy(src_ref=src, dst_ref=dst, send_sem=sems[0].at[k], recv_sem=sems[1].at[k],
                                                device_id=to, device_id_type=MESH)

        remote, landed, passed, passed_in = [], [], [], []
        for a in range(n):
            for j, (fx, fy) in enumerate(CHIP_FLIPS):
                px, py = _flip(x, fx), _flip(y, fy)
                ps = 2 * px + py
                k = a * N_FLIPS + j
                remote.append(copy(ins[a].at[c], outs[a].at[s, c], (send, recv), k, (px, py, c)))
                landed.append(copy(ins[a].at[c], outs[a].at[ps, c], (send, recv), k, (px, py, c)))
                passed.append(copy(outs[a].at[ps, c], outs[a].at[ps, c], (pass_send, pass_recv), k, sibling))
                passed_in.append(copy(outs[a].at[ps, 1 - c], outs[a].at[ps, 1 - c], (pass_send, pass_recv), k, sibling))
        for cp in remote:
            cp.start()
        for arrived, onward in zip(landed, passed):
            arrived.wait_recv()
            onward.start()
        _finish([], remote + passed, passed_in)

    sems = pltpu.SemaphoreType.DMA((n * N_FLIPS,))
    return pl.pallas_call(
        body, name="gather_weights",
        in_specs=[ANY] * n, out_specs=[ANY] * n,
        out_shape=[jax.ShapeDtypeStruct((N_CHIPS,) + a.shape, a.dtype) for a in arrs],
        scratch_shapes=[sems, sems, sems, sems],
    )(*arrs)


def _pair_exchange(g_in, g_out, g_small):
    def body(gi_ref, go_ref, gs_ref, li_ref, lo_ref, ls_ref, send, recv):
        x, y, c = _position()
        peer = (x, y, 1 - c)
        srcs = (gi_ref.at[:, pl.ds((1 - c) * HALF_IN, HALF_IN), :], go_ref.at[:, pl.ds((1 - c) * HALF_OUT, HALF_OUT), :],
                gs_ref)
        remote = [pltpu.make_async_remote_copy(src_ref=src, dst_ref=dst, send_sem=send.at[k], recv_sem=recv.at[k],
                                               device_id=peer, device_id_type=MESH)
                  for k, (src, dst) in enumerate(zip(srcs, (li_ref, lo_ref, ls_ref)))]
        for cp in remote:
            cp.start()
        _finish([], remote, remote)

    return pl.pallas_call(
        body, name="pair_exchange",
        in_specs=[ANY] * 3, out_specs=[ANY] * 3,
        out_shape=[jax.ShapeDtypeStruct((N_CHIPS, HALF_IN, IN_SHARD), F32),
                   jax.ShapeDtypeStruct((N_CHIPS, HALF_OUT, D_MODEL), F32),
                   jax.ShapeDtypeStruct(g_small.shape, F32)],
        scratch_shapes=[pltpu.SemaphoreType.DMA((3,)), pltpu.SemaphoreType.DMA((3,))],
    )(g_in, g_out, g_small)


def _pair_sum(g_in, g_out, g_small, l_in, l_out, l_small, c_arr):
    tr = HALF_IN // 2

    def body(c_ref, gi_ref, go_ref, gs_ref, li_ref, lo_ref, ls_ref, ci_ref, co_ref, cs_ref):
        ci_ref[...] = (gi_ref[...] + li_ref[...]).astype(BF16)

        @pl.when(pl.program_id(1) == 0)
        def _():
            co_ref[...] = (go_ref[...] + lo_ref[...]).astype(BF16)

        @pl.when((pl.program_id(0) == 0) & (pl.program_id(1) == 0))
        def _():
            cs_ref[...] = gs_ref[...] + ls_ref[...]

    nd = g_small.shape
    return pl.pallas_call(
        body, name="pair_sum",
        grid_spec=pltpu.PrefetchScalarGridSpec(
            num_scalar_prefetch=1, grid=(N_CHIPS, 2),
            in_specs=[pl.BlockSpec((1, tr, IN_SHARD), lambda s, i, c: (s, 2 * c[0] + i, 0)),
                      pl.BlockSpec((1, HALF_OUT, D_MODEL), lambda s, i, c: (s, c[0], 0)),
                      pl.BlockSpec(nd, lambda s, i, c: (0, 0)),
                      pl.BlockSpec((1, tr, IN_SHARD), lambda s, i, c: (s, i, 0)),
                      pl.BlockSpec((1, HALF_OUT, D_MODEL), lambda s, i, c: (s, 0, 0)),
                      pl.BlockSpec(nd, lambda s, i, c: (0, 0))],
            out_specs=[pl.BlockSpec((1, tr, IN_SHARD), lambda s, i, c: (s, i, 0)),
                       pl.BlockSpec((1, HALF_OUT, D_MODEL), lambda s, i, c: (s, 0, 0)),
                       pl.BlockSpec(nd, lambda s, i, c: (0, 0))]),
        out_shape=[jax.ShapeDtypeStruct((N_CHIPS, HALF_IN, IN_SHARD), BF16),
                   jax.ShapeDtypeStruct((N_CHIPS, HALF_OUT, D_MODEL), BF16), jax.ShapeDtypeStruct(nd, F32)],
        compiler_params=_cparams(dimension_semantics=("arbitrary", "arbitrary")),
    )(c_arr, g_in, g_out, g_small, l_in, l_out, l_small)


def _chip_exchange(c_in, c_out, c_small):
    def body(ci_ref, co_ref, cs_ref, li_ref, lo_ref, ls_ref, send, recv):
        x, y, c = _position()
        s = 2 * x + y
        remote = []
        for j, (fx, fy) in enumerate(CHIP_FLIPS):
            px, py = _flip(x, fx), _flip(y, fy)
            ps = 2 * px + py
            for a, (src, dst) in enumerate(((ci_ref.at[ps], li_ref.at[j]), (co_ref.at[ps], lo_ref.at[j]),
                                            (cs_ref, ls_ref.at[j]))):
                k = 3 * j + a
                remote.append(pltpu.make_async_remote_copy(src_ref=src, dst_ref=dst, send_sem=send.at[k],
                                                           recv_sem=recv.at[k], device_id=(px, py, c),
                                                           device_id_type=MESH))
        for cp in remote:
            cp.start()
        _finish([], remote, remote)

    return pl.pallas_call(
        body, name="chip_exchange",
        in_specs=[ANY] * 3, out_specs=[ANY] * 3,
        out_shape=[jax.ShapeDtypeStruct((N_FLIPS, HALF_IN, IN_SHARD), c_in.dtype),
                   jax.ShapeDtypeStruct((N_FLIPS, HALF_OUT, D_MODEL), c_out.dtype),
                   jax.ShapeDtypeStruct((N_FLIPS,) + c_small.shape, F32)],
        scratch_shapes=[pltpu.SemaphoreType.DMA((3 * N_FLIPS,)), pltpu.SemaphoreType.DMA((3 * N_FLIPS,))],
    )(c_in, c_out, c_small)


def _chip_sum(g_in, g_out, p_in, p_out, c_small, l_in, l_out, l_small, sc_arr):
    tr = HALF_IN // 2
    nd = c_small.shape

    def body(s_ref, gi_ref, go_ref, pi_ref, po_ref, cs_ref, li0, li1, li2, lo0, lo1, lo2, ls_ref,
             ri_ref, ro_ref, rs_ref):
        ri_ref[...] = (((gi_ref[0] + pi_ref[0]) + li0[0].astype(F32)) + li1[0].astype(F32)) + li2[0].astype(F32)

        @pl.when(pl.program_id(0) == 0)
        def _():
            ro_ref[...] = (((go_ref[0] + po_ref[0]) + lo0[0].astype(F32)) + lo1[0].astype(F32)) + lo2[0].astype(F32)
            me = s_ref[0]
            parts = (cs_ref[...], ls_ref[0], ls_ref[1], ls_ref[2])

            def of_chip(s):
                m = jnp.bitwise_xor(me, s)
                return jnp.where(m == 0, parts[0], jnp.where(m == 1, parts[1], jnp.where(m == 2, parts[2], parts[3])))

            rs_ref[...] = ((of_chip(0) + of_chip(1)) + of_chip(2)) + of_chip(3)

    def flip_in(j):
        return pl.BlockSpec((1, tr, IN_SHARD), lambda i, s: (j, i, 0))

    def flip_out(j):
        return pl.BlockSpec((1, HALF_OUT, D_MODEL), lambda i, s: (j, 0, 0))

    return pl.pallas_call(
        body, name="chip_sum",
        grid_spec=pltpu.PrefetchScalarGridSpec(
            num_scalar_prefetch=1, grid=(2,),
            in_specs=[pl.BlockSpec((1, tr, IN_SHARD), lambda i, s: (s[0], 2 * s[1] + i, 0)),
                      pl.BlockSpec((1, HALF_OUT, D_MODEL), lambda i, s: (s[0], s[1], 0)),
                      pl.BlockSpec((1, tr, IN_SHARD), lambda i, s: (s[0], i, 0)),
                      pl.BlockSpec((1, HALF_OUT, D_MODEL), lambda i, s: (s[0], 0, 0)),
                      pl.BlockSpec(nd, lambda i, s: (0, 0)),
                      flip_in(0), flip_in(1), flip_in(2), flip_out(0), flip_out(1), flip_out(2),
                      pl.BlockSpec((N_FLIPS,) + nd, lambda i, s: (0, 0, 0))],
            out_specs=[pl.BlockSpec((tr, IN_SHARD), lambda i, s: (i, 0)),
                       pl.BlockSpec((HALF_OUT, D_MODEL), lambda i, s: (0, 0)),
                       pl.BlockSpec(nd, lambda i, s: (0, 0))]),
        out_shape=[jax.ShapeDtypeStruct((HALF_IN, IN_SHARD), F32), jax.ShapeDtypeStruct((HALF_OUT, D_MODEL), F32),
                   jax.ShapeDtypeStruct(nd, F32)],
        compiler_params=_cparams(dimension_semantics=("arbitrary",)),
    )(sc_arr, g_in, g_out, p_in, p_out, c_small, l_in, l_in, l_in, l_out, l_out, l_out, l_small)


def _pair_share(r_in, r_out):
    def body(ri_ref, ro_ref, li_ref, lo_ref, send, recv):
        x, y, c = _position()
        remote = [pltpu.make_async_remote_copy(src_ref=src, dst_ref=dst, send_sem=send.at[k], recv_sem=recv.at[k],
                                               device_id=(x, y, 1 - c), device_id_type=MESH)
                  for k, (src, dst) in enumerate(((ri_ref, li_ref), (ro_ref, lo_ref)))]
        for cp in remote:
            cp.start()
        _finish([], remote, remote)

    return pl.pallas_call(
        body, name="pair_share",
        in_specs=[ANY] * 2, out_specs=[ANY] * 2,
        out_shape=[jax.ShapeDtypeStruct(r_in.shape, F32), jax.ShapeDtypeStruct(r_out.shape, F32)],
        scratch_shapes=[pltpu.SemaphoreType.DMA((2,)), pltpu.SemaphoreType.DMA((2,))],
    )(r_in, r_out)


def _adam_update(w, g, m, v):
    mn = ADAM_B1 * m + (1.0 - ADAM_B1) * g
    vn = ADAM_B2 * v + (1.0 - ADAM_B2) * jnp.square(g)
    m_hat = mn / (1.0 - ADAM_B1 ** ADAM_STEP)
    v_hat = vn / (1.0 - ADAM_B2 ** ADAM_STEP)
    return -ADAM_LR * (m_hat / (jnp.sqrt(v_hat) + ADAM_EPS) + ADAM_WD * w), mn, vn


def _adamw(name, w, g_mine, g_theirs, m, v, core_arr, tr):
    rows, cols = w.shape
    per_half = rows // 2 // tr

    def body(c_ref, w_ref, gm_ref, gt_ref, m_ref, v_ref, g_ref, d_ref, nm_ref, nv_ref):
        mine = (pl.program_id(0) // per_half) == c_ref[0]
        g = jnp.where(mine, gm_ref[...], gt_ref[...])
        d, mn, vn = _adam_update(w_ref[...], g, m_ref[...], v_ref[...])
        g_ref[...] = g
        d_ref[...] = d
        nm_ref[...] = mn
        nv_ref[...] = vn

    spec = pl.BlockSpec((tr, cols), lambda i, c: (i, 0))
    half = pl.BlockSpec((tr, cols), lambda i, c: (i % per_half, 0))
    return pl.pallas_call(
        body, name=name,
        grid_spec=pltpu.PrefetchScalarGridSpec(
            num_scalar_prefetch=1, grid=(rows // tr,),
            in_specs=[spec, half, half, spec, spec], out_specs=[spec] * 4),
        out_shape=[jax.ShapeDtypeStruct((rows, cols), F32)] * 4,
        compiler_params=_cparams(dimension_semantics=("arbitrary",)),
    )(core_arr, w, g_mine, g_theirs, m, v)


def _row_pieces(n):
    return [(k, k * PACK_W, min(PACK_W, n - k * PACK_W)) for k in range(-(-n // PACK_W))]


def _pack_small(d_small, loss, d_lora):
    ns = len(SMALL_NAMES)

    def body(*refs):
        small_refs, (loss_ref, lora_ref, out_ref) = refs[:ns], refs[ns:]
        out_ref[...] = jnp.zeros_like(out_ref)
        out_ref[PACK_LORA_W:PACK_LORA_W + LORA, :] = lora_ref[:LORA, :RW_WIDTH]
        out_ref[PACK_LORA_A:PACK_LORA_A + LORA, :] = lora_ref[LORA:, RW_WIDTH:]
        for name, n, ref in zip(SMALL_NAMES, SMALL_SIZES, small_refs):
            for k, at, w in _row_pieces(n):
                out_ref[PACK_AT[name] + k:PACK_AT[name] + k + 1, 0:w] = ref[:, at:at + w]
        out_ref[PACK_LOSS:PACK_LOSS + 1, :] = loss_ref[...]

    return pl.pallas_call(body, name="pack_small", out_shape=jax.ShapeDtypeStruct((PACK_ROWS, PACK_W), F32),
                          compiler_params=_cparams())(*d_small, loss, d_lora)


def _adamw_small(tot, chip_arr, ws, ms, vs):
    ns = len(SMALL_NAMES)
    n_par = ns + 2

    def body(s_ref, tot_ref, glw_ref, gla_ref, *refs):
        w_refs, m_refs, v_refs = refs[:n_par], refs[n_par:2 * n_par], refs[2 * n_par:3 * n_par]
        outs = refs[3 * n_par:]
        g_refs, d_refs, nm_refs, nv_refs = (outs[i * n_par:(i + 1) * n_par] for i in range(4))
        grads = [jnp.concatenate([tot_ref[PACK_AT[name] + k:PACK_AT[name] + k + 1, 0:w] for k, _, w in _row_pieces(n)],
                                 axis=1) for name, n in zip(SMALL_NAMES, SMALL_SIZES)]
        grads += [glw_ref[...], gla_ref[...]]
        for i, g in enumerate(grads):
            d, mn, vn = _adam_update(w_refs[i][...], g, m_refs[i][...], v_refs[i][...])
            g_refs[i][...] = g
            d_refs[i][...] = d
            nm_refs[i][...] = mn
            nv_refs[i][...] = vn

    def whole(a):
        nd = a.ndim
        return pl.BlockSpec(a.shape, lambda i, s: (0,) * nd)

    shard = (LORA, LORA_SHARD)
    par_specs = [whole(a) for a in ws]
    res = pl.pallas_call(
        body, name="adamw_small",
        grid_spec=pltpu.PrefetchScalarGridSpec(
            num_scalar_prefetch=1, grid=(1,),
            in_specs=[whole(tot), pl.BlockSpec(shard, lambda i, s: (PACK_LORA_W // LORA, s[0])),
                      pl.BlockSpec(shard, lambda i, s: (PACK_LORA_A // LORA, s[0]))] + par_specs * 3,
            out_specs=par_specs * 4),
        out_shape=[jax.ShapeDtypeStruct(a.shape, F32) for a in ws] * 4,
        compiler_params=_cparams(dimension_semantics=("arbitrary",)),
    )(chip_arr, tot, tot, tot, *ws, *ms, *vs)
    return [res[i * n_par:(i + 1) * n_par] for i in range(4)]


def kernel(x, norm_g, w_in, ret_gn_g, rwkv_mu, w_lora_up, w0, a_lora_up, a0, k_k, k_a, r_k, rwkv_gn_g, rwkv_gn_b, w_out, final_norm_g, loss_target, m_norm_g, m_w_in, m_ret_gn_g, m_rwkv_mu, m_w_lora_up, m_w0, m_a_lora_up, m_a0, m_k_k, m_k_a, m_r_k, m_rwkv_gn_g, m_rwkv_gn_b, m_w_out, m_final_norm_g, v_norm_g, v_w_in, v_ret_gn_g, v_rwkv_mu, v_w_lora_up, v_w0, v_a_lora_up, v_a0, v_k_k, v_k_a, v_r_k, v_rwkv_gn_g, v_rwkv_gn_b, v_w_out, v_final_norm_g):
    W = RW_WIDTH
    params = dict(norm_g=norm_g, ret_gn_g=ret_gn_g, rwkv_mu=rwkv_mu, w0=w0, a0=a0, k_k=k_k, k_a=k_a, r_k=r_k,
                  rwkv_gn_g=rwkv_gn_g, rwkv_gn_b=rwkv_gn_b, final_norm_g=final_norm_g)
    moments_m = dict(norm_g=m_norm_g, ret_gn_g=m_ret_gn_g, rwkv_mu=m_rwkv_mu, w0=m_w0, a0=m_a0, k_k=m_k_k, k_a=m_k_a,
                     r_k=m_r_k, rwkv_gn_g=m_rwkv_gn_g, rwkv_gn_b=m_rwkv_gn_b, final_norm_g=m_final_norm_g)
    moments_v = dict(norm_g=v_norm_g, ret_gn_g=v_ret_gn_g, rwkv_mu=v_rwkv_mu, w0=v_w0, a0=v_a0, k_k=v_k_k, k_a=v_k_a,
                     r_k=v_r_k, rwkv_gn_g=v_rwkv_gn_g, rwkv_gn_b=v_rwkv_gn_b, final_norm_g=v_final_norm_g)
    xi, yi, ci = _position()
    chip = (2 * xi + yi).astype(jnp.int32)

    def halves(a):
        return a.reshape(2, a.shape[0] // 2, a.shape[1])

    mine = [halves(w_in[0].astype(BF16)), halves(w_out[0].astype(BF16)), halves(w_lora_up[0]), halves(a_lora_up[0])]
    g_in, g_out, g_lw, g_la = [lax.dynamic_update_slice(g, own[None], (chip, 0, 0, 0))
                               for g, own in zip(_gather_chips(mine), mine)]
    w_in_b = jnp.transpose(g_in.reshape(N_CHIPS, D_MODEL, IN_SHARD), (1, 0, 2)).reshape(D_MODEL, IN_COLS)
    w_out_b = g_out.reshape(D_MODEL, D_MODEL)
    lw = jnp.transpose(g_lw.reshape(N_CHIPS, LORA, LORA_SHARD), (1, 0, 2)).reshape(LORA, W)
    la = jnp.transpose(g_la.reshape(N_CHIPS, LORA, LORA_SHARD), (1, 0, 2)).reshape(LORA, W)
    zero = jnp.zeros((LORA, W), F32)
    lora = jnp.concatenate([jnp.concatenate([lw, zero], axis=1), jnp.concatenate([zero, la], axis=1)], axis=0)
    small = {n: params[n].reshape(1, -1) for n in SMALL_NAMES}

    loss, grad_x, d_w_in, d_w_out, d_lora, d_small = _local_step(x[0], loss_target[0], w_in_b, w_out_b, lora, small)

    core = ci.astype(jnp.int32)
    gi = jnp.transpose(d_w_in.reshape(D_MODEL, N_CHIPS, IN_SHARD), (1, 0, 2))
    go = d_w_out.reshape(N_CHIPS, OUT_SHARD, D_MODEL)
    gs = _pack_small([d_small[n] for n in SMALL_NAMES], loss, d_lora)
    p_in, p_out, p_small = _pair_exchange(gi, go, gs)
    c_in, c_out, c_small = _pair_sum(gi, go, gs, p_in, p_out, p_small, core.reshape(1))
    l_in, l_out, l_small = _chip_exchange(c_in, c_out, c_small)
    r_in, r_out, tot = _chip_sum(gi, go, p_in, p_out, c_small, l_in, l_out, l_small, jnp.stack([chip, core]))
    t_in, t_out = _pair_share(r_in, r_out)

    grad_w_in, d_in, nm_in, nv_in = _adamw("adamw_w_in", w_in[0], r_in, t_in, m_w_in[0], v_w_in[0], core.reshape(1), 256)
    grad_w_out, d_out, nm_out, nv_out = _adamw("adamw_w_out", w_out[0], r_out, t_out, m_w_out[0], v_w_out[0],
                                               core.reshape(1), HALF_OUT)
    par_names = SMALL_NAMES + ("w_lora_up", "a_lora_up")

    def operands(tree, lw_, la_):
        return [tree[n].reshape(1, -1) for n in SMALL_NAMES] + [lw_[0], la_[0]]

    res = _adamw_small(tot, chip.reshape(1), operands(params, w_lora_up, a_lora_up),
                       operands(moments_m, m_w_lora_up, m_a_lora_up), operands(moments_v, v_w_lora_up, v_a_lora_up))

    names = ("norm_g", "w_in", "ret_gn_g", "rwkv_mu", "w_lora_up", "w0", "a_lora_up", "a0", "k_k", "k_a", "r_k",
             "rwkv_gn_g", "rwkv_gn_b", "w_out", "final_norm_g")
    shapes = dict(w_in=w_in.shape, w_out=w_out.shape, w_lora_up=w_lora_up.shape, a_lora_up=a_lora_up.shape,
                  **{n: params[n].shape for n in SMALL_NAMES})

    def leaves(pars, big_in, big_out):
        tree = dict(zip(par_names, pars), w_in=big_in, w_out=big_out)
        return [tree[n].reshape(shapes[n]) for n in names]

    grads = leaves(res[0], grad_w_in, grad_w_out)
    deltas = leaves(res[1], d_in, d_out)
    new_m = leaves(res[2], nm_in, nm_out)
    new_v = leaves(res[3], nv_in, nv_out)
    return (tot[PACK_LOSS, 0], grad_x.reshape(x.shape), *grads, *deltas, *new_m, *new_v)
```

```python
import functools

import numpy as np
import jax
import jax.numpy as jnp
from jax import lax
from jax.experimental import pallas as pl
from jax.experimental.pallas import tpu as pltpu

F32 = jnp.float32
BF16 = jnp.bfloat16
HIGHEST = lax.Precision.HIGHEST
MESH = pl.DeviceIdType.MESH

D_MODEL = 1024
N_CHIPS = 4
RET_HEADS = 4
RET_DK = 64
RET_DV = 128
RET_QK = RET_HEADS * RET_DK
RET_WIDTH = RET_HEADS * RET_DV
RET_COLS = 2 * RET_QK + 2 * RET_WIDTH
RET_CHUNK = 64
RET_GROUP = 4
RW_WIDTH = 512
RW_HEAD = 64
RW_HEADS = 8
LORA = 64
RW_COLS = 4 * RW_WIDTH + 2 * LORA
IN_COLS = RET_COLS + RW_COLS
IN_SHARD = IN_COLS // N_CHIPS
OUT_SHARD = D_MODEL // N_CHIPS
ROPE_BASE = 10000.0
RMS_EPS = 1e-6
RET_GN_EPS = 1e-5
RW_GN_EPS = 64e-5
WKV_CHUNK = 16
N_VEC = 5

ADAM_LR = 0.001
ADAM_B1 = 0.9
ADAM_B2 = 0.999
ADAM_EPS = 1e-08
ADAM_WD = 0.01
ADAM_STEP = 10

VMEM_LIMIT = 56 * 1024 * 1024

PACK_W = 512
SMALL_NAMES = ("norm_g", "ret_gn_g", "rwkv_mu", "w0", "a0", "k_k", "k_a", "r_k", "rwkv_gn_g", "rwkv_gn_b",
               "final_norm_g")
SMALL_SIZES = (1024, 512, 2176, 512, 512, 512, 512, 512, 512, 512, 1024)
PACK_LORA_W = 0
PACK_LORA_A = LORA
PACK_SMALL = 2 * LORA


def _pack_layout():
    rows, at = {}, PACK_SMALL
    for name, n in zip(SMALL_NAMES, SMALL_SIZES):
        rows[name] = at
        at += -(-n // PACK_W)
    return rows, at


PACK_AT, PACK_LOSS = _pack_layout()
PACK_ROWS = -(-(PACK_LOSS + 1) // 8) * 8


def _cparams(**kw):
    return pltpu.CompilerParams(vmem_limit_bytes=VMEM_LIMIT, **kw)


def _dot(a, b, precision=None):
    return jnp.dot(a, b, precision=precision, preferred_element_type=F32)


def _dot_nt(a, b, precision=None):
    return lax.dot_general(a, b, (((1,), (1,)), ((), ())), precision=precision, preferred_element_type=F32)


def _dot_tn(a, b, precision=None):
    return lax.dot_general(a, b, (((0,), (0,)), ((), ())), precision=precision, preferred_element_type=F32)


def _split(x):
    hi = x.astype(BF16)
    lo = (x - hi.astype(F32)).astype(BF16)
    return hi, lo


@jax.custom_vjp
def _segsum(x, seg):
    hi, lo = _split(x)
    return _dot(hi, seg) + _dot(lo, seg)


def _segsum_fwd(x, seg):
    return _segsum(x, seg), seg


def _segsum_bwd(seg, ct):
    return _segsum(ct, seg), jnp.zeros_like(seg)


_segsum.defvjp(_segsum_fwd, _segsum_bwd)


def _softplus(z):
    return jnp.maximum(z, 0.0) + jnp.log(1.0 + jnp.exp(-jnp.abs(z)))


def _full(shape):
    nd = len(shape)
    return pl.BlockSpec(shape, lambda *_: (0,) * nd)


def _rope_tables(T):
    half = RET_DK // 2
    expo = -jnp.arange(half, dtype=F32) / jnp.float32(half)
    freqs = jnp.exp(expo * jnp.float32(np.log(ROPE_BASE)))
    ang = jnp.arange(T, dtype=jnp.int32).astype(F32)[:, None] * freqs[None, :]
    cos = jnp.tile(jnp.cos(ang), (1, 2 * RET_HEADS))
    sin = jnp.tile(jnp.sin(ang), (1, 2 * RET_HEADS))
    return cos, sin


def _ret_tables():
    H, C = RET_HEADS, RET_CHUNK
    hidx = jnp.arange(H, dtype=F32)
    lg = jnp.log(1.0 - jnp.exp2(-5.0 - hidx))
    idx = jnp.arange(C, dtype=F32)
    intra = jnp.exp(lg[:, None, None] * jnp.abs(idx[:, None] - idx[None, :]))
    q_dec = jnp.transpose(jnp.exp(lg[:, None] * (idx[None, :] + 1.0)))
    k_dec = jnp.transpose(jnp.exp(lg[:, None] * (C - 1.0 - idx[None, :])))
    chunk_dec = jnp.exp(lg * C)
    qd = jnp.repeat(q_dec, RET_DK, axis=1)
    kd = jnp.repeat(k_dec, RET_DK, axis=1)
    row_h = np.arange(RET_QK) // RET_DK
    col_h = np.arange(RET_WIDTH) // RET_DV
    bm = jnp.asarray((row_h[:, None] == col_h[None, :]).astype(np.float32))
    cd = bm * jnp.repeat(chunk_dec, RET_DK)[:, None]
    return intra, qd, kd, cd, bm


def _seg_matrix(width, head):
    h = np.arange(width) // head
    return jnp.asarray((h[:, None] == h[None, :]).astype(np.float32), dtype=BF16)


def _wkv_expand_table():
    Tc = WKV_CHUNK
    k = np.arange(2 * RW_HEADS * Tc)
    kh, kt = (k % (RW_HEADS * Tc)) // Tc, k % Tc
    nh = np.arange(RW_WIDTH) // RW_HEAD
    e = (kh[None, :, None] == nh[None, None, :]) & (kt[None, :, None] == np.arange(Tc)[:, None, None])
    return jnp.asarray(e.astype(np.float32), dtype=BF16)


def _wkv_reduce_table():
    Tc = WKV_CHUNK
    kh = np.arange(RW_WIDTH) // RW_HEAD
    n = np.arange(RW_HEADS * Tc)
    nh, nt = n // Tc, n % Tc
    r = (kh[None, :, None] == nh[None, None, :]) & (nt[None, None, :] == np.arange(Tc)[:, None, None])
    return jnp.asarray(r.astype(np.float32), dtype=BF16)


def _inproj_fwd(x, norm_g, w_b):
    T = x.shape[0]
    tm = min(T, 256)

    def body(x_ref, g_ref, w_ref, pret_ref, prw_ref, u_ref):
        xf = x_ref[...]
        rstd = lax.rsqrt(jnp.mean(xf * xf, axis=-1, keepdims=True) + RMS_EPS)
        ub = ((xf * rstd) * g_ref[...]).astype(BF16)
        u_ref[...] = ub
        pret_ref[...] = _dot(ub, w_ref[:, :RET_COLS])
        prw_ref[...] = _dot(ub, w_ref[:, RET_COLS:])

    return pl.pallas_call(
        body, name="inproj_fwd", grid=(T // tm,),
        in_specs=[pl.BlockSpec((tm, D_MODEL), lambda i: (i, 0)), _full((1, D_MODEL)), _full((D_MODEL, IN_COLS))],
        out_specs=[pl.BlockSpec((tm, RET_COLS), lambda i: (i, 0)), pl.BlockSpec((tm, RW_COLS), lambda i: (i, 0)),
                   pl.BlockSpec((tm, D_MODEL), lambda i: (i, 0))],
        out_shape=[jax.ShapeDtypeStruct((T, RET_COLS), F32), jax.ShapeDtypeStruct((T, RW_COLS), F32),
                   jax.ShapeDtypeStruct((T, D_MODEL), BF16)],
        compiler_params=_cparams(dimension_semantics=("arbitrary",)),
    )(x, norm_g, w_b)


def _rot_half(x):
    n = x.shape[1]
    lane = lax.broadcasted_iota(jnp.int32, x.shape, 1)
    first = (lane % RET_DK) < (RET_DK // 2)
    return jnp.where(first, -pltpu.roll(x, n - RET_DK // 2, 1), pltpu.roll(x, RET_DK // 2, 1))


def _rope(x, cos, sin):
    return x * cos + _rot_half(x) * sin


def _rope_bwd(d, cos, sin):
    return d * cos - _rot_half(d * sin)


def _ret_post(ret, g, gn_g, seg):
    mu = _segsum(ret, seg) * (1.0 / RET_DV)
    xc = ret - mu
    var = _segsum(xc * xc, seg) * (1.0 / RET_DV)
    n = xc * lax.rsqrt(var + RET_GN_EPS)
    return (g * jax.nn.sigmoid(g)) * (n * gn_g)


def _ret_scores(qt, kt, d_ref, h):
    lane = lax.broadcasted_iota(jnp.int32, qt.shape, 1)
    qh = jnp.where(lane // RET_DK == h, qt, 0.0)
    return qh, _dot_nt(qh, kt, HIGHEST) * d_ref[h]


def _ret_group(nch):
    return min(RET_GROUP, nch)


def _ret_fwd(p_ret, cos, sin, tabs, gn_g, seg128):
    T = p_ret.shape[0]
    C = RET_CHUNK
    nch = T // C
    G = _ret_group(nch)
    intra_d, qd, kd, cd, bm = tabs

    def body(q_ref, k_ref, v_ref, g_ref, cos_ref, sin_ref, qd_ref, kd_ref, d_ref, cd_ref, bm_ref, gn_ref, seg_ref,
             y_ref, ret_ref, sin_out_ref, s_ref):
        @pl.when(pl.program_id(0) == 0)
        def _():
            s_ref[...] = jnp.zeros_like(s_ref)

        s_in = s_ref[...]
        for i in range(G):
            rows = slice(i * C, (i + 1) * C)
            cosv, sinv = cos_ref[rows, :], sin_ref[rows, :]
            qt = _rope(q_ref[rows, :], cosv, sinv)
            kt = _rope(k_ref[rows, :], cosv, sinv) * (RET_DK ** -0.5)
            v = v_ref[rows, :]
            sin_out_ref[i] = s_in
            inter = _dot(qt * qd_ref[...], s_in, HIGHEST)
            intra = []
            for h in range(RET_HEADS):
                _, a = _ret_scores(qt, kt, d_ref, h)
                intra.append(_dot(a, v[:, h * RET_DV:(h + 1) * RET_DV], HIGHEST))
            ret_ref[rows, :] = jnp.concatenate(intra, axis=1) + inter
            kv = _dot_tn(kt * kd_ref[...], v, HIGHEST)
            s_in = s_in * cd_ref[...] + kv * bm_ref[...]
        s_ref[...] = s_in
        y_ref[...] = _ret_post(ret_ref[...], g_ref[...], gn_ref[...], seg_ref[...]).astype(BF16)

    GC = G * C
    return pl.pallas_call(
        body, name="ret_fwd", grid=(nch // G,),
        in_specs=[pl.BlockSpec((GC, RET_QK), lambda c: (c, 0)), pl.BlockSpec((GC, RET_QK), lambda c: (c, 1)),
                  pl.BlockSpec((GC, RET_WIDTH), lambda c: (c, 1)), pl.BlockSpec((GC, RET_WIDTH), lambda c: (c, 2)),
                  pl.BlockSpec((GC, RET_QK), lambda c: (c, 0)), pl.BlockSpec((GC, RET_QK), lambda c: (c, 0)),
                  _full((C, RET_QK)), _full((C, RET_QK)), _full((RET_HEADS, C, C)),
                  _full((RET_QK, RET_WIDTH)), _full((RET_QK, RET_WIDTH)), _full((1, RET_WIDTH)),
                  _full((RET_WIDTH, RET_WIDTH))],
        out_specs=[pl.BlockSpec((GC, RET_WIDTH), lambda c: (c, 0)), pl.BlockSpec((GC, RET_WIDTH), lambda c: (c, 0)),
                   pl.BlockSpec((G, RET_QK, RET_WIDTH), lambda c: (c, 0, 0))],
        out_shape=[jax.ShapeDtypeStruct((T, RET_WIDTH), BF16), jax.ShapeDtypeStruct((T, RET_WIDTH), F32),
                   jax.ShapeDtypeStruct((nch, RET_QK, RET_WIDTH), F32)],
        scratch_shapes=[pltpu.VMEM((RET_QK, RET_WIDTH), F32)],
        compiler_params=_cparams(dimension_semantics=("arbitrary",)),
    )(p_ret, p_ret, p_ret, p_ret, cos, sin, qd, kd, intra_d, cd, bm, gn_g, seg128)


def _ret_bwd(p_ret, cos, sin, tabs, gn_g, seg128, ret, s_in_all, dy):
    T = p_ret.shape[0]
    C = RET_CHUNK
    nch = T // C
    G = _ret_group(nch)
    ngr = nch // G
    intra_d, qd, kd, cd, bm = tabs

    def rev(j):
        return lambda c: (ngr - 1 - c, j)

    def body(q_ref, k_ref, v_ref, g_ref, cos_ref, sin_ref, qd_ref, kd_ref, d_ref, cd_ref, bm_ref, gn_ref, seg_ref,
             ret_ref, sin_ref_, dy_ref, dp_ref, dgn_ref, ds_ref):
        @pl.when(pl.program_id(0) == 0)
        def _():
            ds_ref[...] = jnp.zeros_like(ds_ref)
            dgn_ref[...] = jnp.zeros_like(dgn_ref)

        seg = seg_ref[...]
        _, post_vjp = jax.vjp(lambda r_, g_, gn_: _ret_post(r_, g_, gn_, seg), ret_ref[...], g_ref[...], gn_ref[...])
        dret_all, dg_all, dgn = post_vjp(dy_ref[...])
        dgn_ref[...] += dgn
        dp_ref[:, 2 * RET_QK + RET_WIDTH:] = dg_all

        qdv, kdv = qd_ref[...], kd_ref[...]
        ds_out = ds_ref[...]
        for i in reversed(range(G)):
            rows = slice(i * C, (i + 1) * C)
            cosv, sinv = cos_ref[rows, :], sin_ref[rows, :]
            qt = _rope(q_ref[rows, :], cosv, sinv)
            kt = _rope(k_ref[rows, :], cosv, sinv) * (RET_DK ** -0.5)
            v = v_ref[rows, :]
            s_in = sin_ref_[i]
            dret = dret_all[rows, :]
            dqt = qdv * _dot_nt(dret, s_in, HIGHEST)
            dkt = kdv * _dot_nt(v, ds_out, HIGHEST)
            dv_all = _dot(kt * kdv, ds_out, HIGHEST)
            dvs = []
            for h in range(RET_HEADS):
                sl = slice(h * RET_DV, (h + 1) * RET_DV)
                qh, a = _ret_scores(qt, kt, d_ref, h)
                lane = lax.broadcasted_iota(jnp.int32, kt.shape, 1)
                kh = jnp.where(lane // RET_DK == h, kt, 0.0)
                da = _dot_nt(dret[:, sl], v[:, sl], HIGHEST) * d_ref[h]
                dvs.append(_dot_tn(a, dret[:, sl], HIGHEST))
                dqt = dqt + _dot(da, kh, HIGHEST)
                dkt = dkt + _dot_tn(da, qh, HIGHEST)
            ds_out = ds_out * cd_ref[...] + _dot_tn(qt * qdv, dret, HIGHEST) * bm_ref[...]
            dp_ref[rows, :RET_QK] = _rope_bwd(dqt, cosv, sinv)
            dp_ref[rows, RET_QK:2 * RET_QK] = _rope_bwd(dkt * (RET_DK ** -0.5), cosv, sinv)
            dp_ref[rows, 2 * RET_QK:2 * RET_QK + RET_WIDTH] = dv_all + jnp.concatenate(dvs, axis=1)
        ds_ref[...] = ds_out

    GC = G * C
    return pl.pallas_call(
        body, name="ret_bwd", grid=(ngr,),
        in_specs=[pl.BlockSpec((GC, RET_QK), rev(0)), pl.BlockSpec((GC, RET_QK), rev(1)),
                  pl.BlockSpec((GC, RET_WIDTH), rev(1)), pl.BlockSpec((GC, RET_WIDTH), rev(2)),
                  pl.BlockSpec((GC, RET_QK), rev(0)), pl.BlockSpec((GC, RET_QK), rev(0)),
                  _full((C, RET_QK)), _full((C, RET_QK)), _full((RET_HEADS, C, C)),
                  _full((RET_QK, RET_WIDTH)), _full((RET_QK, RET_WIDTH)), _full((1, RET_WIDTH)),
                  _full((RET_WIDTH, RET_WIDTH)),
                  pl.BlockSpec((GC, RET_WIDTH), rev(0)),
                  pl.BlockSpec((G, RET_QK, RET_WIDTH), lambda c: (ngr - 1 - c, 0, 0)),
                  pl.BlockSpec((GC, RET_WIDTH), rev(0))],
        out_specs=[pl.BlockSpec((GC, RET_COLS), rev(0)), _full((1, RET_WIDTH))],
        out_shape=[jax.ShapeDtypeStruct((T, RET_COLS), F32), jax.ShapeDtypeStruct((1, RET_WIDTH), F32)],
        scratch_shapes=[pltpu.VMEM((RET_QK, RET_WIDTH), F32)],
        compiler_params=_cparams(dimension_semantics=("arbitrary",)),
    )(p_ret, p_ret, p_ret, p_ret, cos, sin, qd, kd, intra_d, cd, bm, gn_g, seg128, ret, s_in_all, dy)


@jax.custom_vjp
def _chunk_sums(x, tri):
    hi, lo = _split(x)
    return _dot(tri, hi) + _dot(tri, lo)


def _chunk_sums_fwd(x, tri):
    return _chunk_sums(x, tri), tri


def _chunk_sums_bwd(tri, ct):
    hi, lo = _split(ct)
    return _dot_tn(tri, hi) + _dot_tn(tri, lo), jnp.zeros_like(tri)


_chunk_sums.defvjp(_chunk_sums_fwd, _chunk_sums_bwd)


def _chunk_tables(tm):
    t = np.arange(tm)
    same = (t[:, None] // WKV_CHUNK) == (t[None, :] // WKV_CHUNK)
    return jnp.asarray(np.stack([same & (t[None, :] <= t[:, None]), same]).astype(np.float32), dtype=BF16)


def _prep_fn(p, prev, mu, w0, a0, k_k, k_a, lora, seg, tri):
    W = RW_WIDTH
    ps = p + mu * (prev - p)
    r, kr, vr, g = ps[:, 0:W], ps[:, W:2 * W], ps[:, 2 * W:3 * W], ps[:, 3 * W:4 * W]
    z = ps[:, 4 * W:]
    lane = lax.broadcasted_iota(jnp.int32, z.shape, 1)
    z = jnp.where(lane < LORA, jnp.tanh(z), z)
    lo = _dot(z, lora, HIGHEST)
    w_log = -_softplus(-(w0 + lo[:, :W])) - 0.5
    log_decay = -jnp.exp(w_log)
    cum = _chunk_sums(log_decay, tri[0])
    total = _chunk_sums(log_decay, tri[1])
    a = jax.nn.sigmoid(a0 + lo[:, W:])
    kk = kr * k_k
    kk = kk / jnp.maximum(jnp.sqrt(_segsum(kk * kk, seg)), 1e-12)
    k = kr * (1.0 + (a - 1.0) * k_a)
    grow = jnp.exp(-cum)
    return kk * jnp.exp(cum - log_decay), (kk * a) * grow, k * grow, r * jnp.exp(cum), jnp.exp(total), vr, g


def _post_fn(o, r, k, v, g, gn_g, gn_b, r_k, seg):
    mu = _segsum(o, seg) * (1.0 / RW_HEAD)
    oc = o - mu
    var = _segsum(oc * oc, seg) * (1.0 / RW_HEAD)
    on = oc * lax.rsqrt(var + RW_GN_EPS) * gn_g + gn_b
    bonus = _segsum(r * k * r_k, seg) * v
    return (g * jax.nn.sigmoid(g)) * (on + bonus)


def _shift_down(p, first_row):
    row = lax.broadcasted_iota(jnp.int32, p.shape, 0)
    return jnp.where(row == 0, first_row, pltpu.roll(p, 1, 0))


def _shift_up(p, last_row):
    n = p.shape[0]
    row = lax.broadcasted_iota(jnp.int32, p.shape, 0)
    return jnp.where(row == n - 1, last_row, pltpu.roll(p, n - 1, 0))


def _row_tile(T):
    return min(T, 256)


def _prep_fwd(p_rw, bnd, mu, w0, a0, k_k, k_a, lora, seg64, tri):
    T = p_rw.shape[0]
    tm = _row_tile(T)
    W = RW_WIDTH

    def body(p_ref, bnd_ref, mu_ref, w0_ref, a0_ref, kk_ref, ka_ref, lora_ref, seg_ref, tri_ref,
             vecs_ref, v_ref, g_ref):
        p = p_ref[...]
        prev = _shift_down(p, bnd_ref[0])
        res = _prep_fn(p, prev, mu_ref[...], w0_ref[...], a0_ref[...], kk_ref[...], ka_ref[...], lora_ref[...],
                       seg_ref[...], (tri_ref[0], tri_ref[1]))
        for j in range(N_VEC):
            vecs_ref[j] = res[j]
        v_ref[...] = res[N_VEC]
        g_ref[...] = res[N_VEC + 1]

    small = _full((1, W))
    row = pl.BlockSpec((tm, W), lambda i: (i, 0))
    return pl.pallas_call(
        body, name="rwkv_prep_fwd", grid=(T // tm,),
        in_specs=[pl.BlockSpec((tm, RW_COLS), lambda i: (i, 0)), pl.BlockSpec((1, 1, RW_COLS), lambda i: (i, 0, 0)),
                  _full((1, RW_COLS)), small, small, small, small, _full((2 * LORA, 2 * W)), _full((W, W)),
                  _full((2, tm, tm))],
        out_specs=[pl.BlockSpec((N_VEC, tm, W), lambda i: (0, i, 0)), row, row],
        out_shape=[jax.ShapeDtypeStruct((N_VEC, T, W), F32), jax.ShapeDtypeStruct((T, W), F32),
                   jax.ShapeDtypeStruct((T, W), F32)],
        compiler_params=_cparams(dimension_semantics=("arbitrary",)),
    )(p_rw, bnd, mu, w0, a0, k_k, k_a, lora, seg64, tri)


def _prep_bwd(p_rw, bnd, mu, w0, a0, k_k, k_a, lora, seg64, tri, cts):
    T = p_rw.shape[0]
    tm = _row_tile(T)
    W = RW_WIDTH

    def body(p_ref, bnd_ref, mu_ref, w0_ref, a0_ref, kk_ref, ka_ref, lora_ref, seg_ref, tri_ref,
             dvecs_ref, dk2_ref, dr2_ref, dv1_ref, dv2_ref, dg_ref,
             dp_ref, dprev_ref, dmu_ref, dw0_ref, da0_ref, dkk_p_ref, dka_ref, dlora_ref):
        accs = (dmu_ref, dw0_ref, da0_ref, dkk_p_ref, dka_ref, dlora_ref)

        @pl.when(pl.program_id(0) == 0)
        def _():
            for a_ref in accs:
                a_ref[...] = jnp.zeros_like(a_ref)

        p = p_ref[...]
        prev = _shift_down(p, bnd_ref[0])
        seg, tri = seg_ref[...], (tri_ref[0], tri_ref[1])
        _, vjp = jax.vjp(lambda *a: _prep_fn(*a, seg, tri), p, prev, mu_ref[...], w0_ref[...], a0_ref[...],
                         kk_ref[...], ka_ref[...], lora_ref[...])
        ct = (dvecs_ref[0], dvecs_ref[1], dvecs_ref[2] + dk2_ref[...], dvecs_ref[3] + dr2_ref[...], dvecs_ref[4],
              dv1_ref[...] + dv2_ref[...], dg_ref[...])
        grads = vjp(ct)
        dp_ref[...] = grads[0]
        dprev_ref[...] = grads[1]
        for a_ref, gval in zip(accs, grads[2:]):
            a_ref[...] += gval

    small = _full((1, W))
    row = pl.BlockSpec((tm, W), lambda i: (i, 0))
    return pl.pallas_call(
        body, name="rwkv_prep_bwd", grid=(T // tm,),
        in_specs=[pl.BlockSpec((tm, RW_COLS), lambda i: (i, 0)), pl.BlockSpec((1, 1, RW_COLS), lambda i: (i, 0, 0)),
                  _full((1, RW_COLS)), small, small, small, small, _full((2 * LORA, 2 * W)), _full((W, W)),
                  _full((2, tm, tm)), pl.BlockSpec((N_VEC, tm, W), lambda i: (0, i, 0))] + [row] * 5,
        out_specs=[pl.BlockSpec((tm, RW_COLS), lambda i: (i, 0)), pl.BlockSpec((tm, RW_COLS), lambda i: (i, 0)),
                   _full((1, RW_COLS)), small, small, small, small, _full((2 * LORA, 2 * W))],
        out_shape=[jax.ShapeDtypeStruct((T, RW_COLS), F32), jax.ShapeDtypeStruct((T, RW_COLS), F32),
                   jax.ShapeDtypeStruct((1, RW_COLS), F32)] + [jax.ShapeDtypeStruct((1, W), F32)] * 4
                  + [jax.ShapeDtypeStruct((2 * LORA, 2 * W), F32)],
        compiler_params=_cparams(dimension_semantics=("arbitrary",)),
    )(p_rw, bnd, mu, w0, a0, k_k, k_a, lora, seg64, tri, *cts)


def _vec_spec(j, tm):
    return pl.BlockSpec((None, tm, RW_WIDTH), lambda i: (j, i, 0))


def _post_fwd(o, vecs, v, g, gn_g, gn_b, r_k, seg64):
    T = o.shape[0]
    tm = _row_tile(T)
    W = RW_WIDTH

    def body(o_ref, r_ref, k_ref, v_ref, g_ref, gg_ref, gb_ref, rk_ref, seg_ref, y_ref):
        y_ref[...] = _post_fn(o_ref[...], r_ref[...], k_ref[...], v_ref[...], g_ref[...], gg_ref[...], gb_ref[...],
                              rk_ref[...], seg_ref[...]).astype(BF16)

    row = pl.BlockSpec((tm, W), lambda i: (i, 0))
    small = _full((1, W))
    return pl.pallas_call(
        body, name="rwkv_post_fwd", grid=(T // tm,),
        in_specs=[row, _vec_spec(3, tm), _vec_spec(2, tm), row, row] + [small] * 3 + [_full((W, W))],
        out_specs=row, out_shape=jax.ShapeDtypeStruct((T, W), BF16),
        compiler_params=_cparams(dimension_semantics=("arbitrary",)),
    )(o, vecs, vecs, v, g, gn_g, gn_b, r_k, seg64)


def _post_bwd(o, vecs, v, g, gn_g, gn_b, r_k, seg64, dy):
    T = o.shape[0]
    tm = _row_tile(T)
    W = RW_WIDTH

    def body(o_ref, r_ref, k_ref, v_ref, g_ref, gg_ref, gb_ref, rk_ref, seg_ref, dy_ref,
             do_ref, dr_ref, dk_ref, dv_ref, dg_ref, dgg_ref, dgb_ref, drk_ref):
        accs = (dgg_ref, dgb_ref, drk_ref)

        @pl.when(pl.program_id(0) == 0)
        def _():
            for a_ref in accs:
                a_ref[...] = jnp.zeros_like(a_ref)

        seg = seg_ref[...]
        _, vjp = jax.vjp(lambda *a: _post_fn(*a, seg), o_ref[...], r_ref[...], k_ref[...], v_ref[...], g_ref[...],
                         gg_ref[...], gb_ref[...], rk_ref[...])
        grads = vjp(dy_ref[...])
        for o_, gval in zip((do_ref, dr_ref, dk_ref, dv_ref, dg_ref), grads[:5]):
            o_[...] = gval
        for a_ref, gval in zip(accs, grads[5:]):
            a_ref[...] += gval

    row = pl.BlockSpec((tm, W), lambda i: (i, 0))
    small = _full((1, W))
    return pl.pallas_call(
        body, name="rwkv_post_bwd", grid=(T // tm,),
        in_specs=[row, _vec_spec(3, tm), _vec_spec(2, tm), row, row] + [small] * 3
                 + [_full((W, W)), pl.BlockSpec((tm, W), lambda i: (i, 1))],
        out_specs=[row] * 5 + [small] * 3,
        out_shape=[jax.ShapeDtypeStruct((T, W), F32)] * 5 + [jax.ShapeDtypeStruct((1, W), F32)] * 3,
        compiler_params=_cparams(dimension_semantics=("arbitrary",)),
    )(o, vecs, vecs, v, g, gn_g, gn_b, r_k, seg64, dy)


def _to_cols(vecs, T):
    Tc = WKV_CHUNK
    x = vecs.reshape(N_VEC, T // Tc, Tc, RW_HEADS, RW_HEAD)
    x = jnp.transpose(x, (1, 0, 4, 3, 2))
    return x.reshape(T // Tc, N_VEC * RW_HEAD, RW_HEADS * Tc)


def _from_cols(cols, T):
    Tc = WKV_CHUNK
    x = cols.reshape(T // Tc, N_VEC, RW_HEAD, RW_HEADS, Tc)
    return jnp.transpose(x, (1, 0, 4, 3, 2)).reshape(N_VEC, T, RW_WIDTH)


def _wkv_lhs(cols_ref):
    hi, lo = _split(cols_ref[0])
    return jnp.concatenate([hi, lo], axis=1)


N_STEP_VEC = N_VEC - 1
PAD_ROWS = 16


def _wkv_fwd(cols, v, e_tab):
    T = v.shape[0]
    Tc = WKV_CHUNK
    nch = T // Tc
    J, W = RW_HEAD, RW_WIDTH
    JS = N_STEP_VEC * J

    def body(cols_ref, v_ref, e_ref, o_ref, states_ref, sa_ref, s_ref):
        @pl.when(pl.program_id(0) == 0)
        def _():
            s_ref[...] = jnp.zeros_like(s_ref)

        lhs = _wkv_lhs(cols_ref)
        st = s_ref[...]
        for t in range(Tc):
            ex = _dot(lhs[:JS], e_ref[t])
            states_ref[t] = st
            sa = -jnp.sum(st * ex[0:J], axis=0, keepdims=True)
            st = st + ex[J:2 * J] * sa + ex[2 * J:3 * J] * v_ref[t:t + 1, :]
            sa_ref[t:t + 1, :] = sa
            o_ref[t:t + 1, :] = jnp.sum(st * ex[3 * J:4 * J], axis=0, keepdims=True)
        s_ref[...] = st * _dot(lhs[JS:], e_ref[Tc - 1])

    return pl.pallas_call(
        body, name="wkv_fwd", grid=(nch,),
        in_specs=[pl.BlockSpec((1, N_VEC * J, 128), lambda c: (c, 0, 0)), pl.BlockSpec((Tc, W), lambda c: (c, 0)),
                  _full((Tc, 2 * 128, W))],
        out_specs=[pl.BlockSpec((Tc, W), lambda c: (c, 0)), pl.BlockSpec((Tc, J, W), lambda c: (c, 0, 0)),
                   pl.BlockSpec((Tc, W), lambda c: (c, 0))],
        out_shape=[jax.ShapeDtypeStruct((T, W), F32), jax.ShapeDtypeStruct((T, J, W), F32),
                   jax.ShapeDtypeStruct((T, W), F32)],
        scratch_shapes=[pltpu.VMEM((J, W), F32)],
        compiler_params=_cparams(dimension_semantics=("arbitrary",)),
    )(cols, v, e_tab)


def _wkv_bwd(cols, v, do, states, sa, e_tab, r_tab):
    T = v.shape[0]
    Tc = WKV_CHUNK
    nch = T // Tc
    J, W = RW_HEAD, RW_WIDTH
    JS = N_STEP_VEC * J
    blocks = [slice(b * 128, (b + 1) * 128) for b in range(W // 128)]

    def body(cols_ref, v_ref, do_ref, states_ref, sa_ref, e_ref, r_ref, dv_ref, dcols_ref, ds_ref):
        @pl.when(pl.program_id(0) == 0)
        def _():
            ds_ref[...] = jnp.zeros_like(ds_ref)

        lhs = _wkv_lhs(cols_ref)
        last = Tc - 1
        ex = _dot(lhs, e_ref[last])
        dst, ends = [], []
        for b in blocks:
            s_end = (states_ref[last, :, b] + ex[J:2 * J, b] * sa_ref[last:last + 1, b]
                     + ex[2 * J:3 * J, b] * v_ref[last:last + 1, b])
            d_end = ds_ref[:, b]
            ends.append((d_end * s_end).astype(BF16))
            dst.append(d_end * ex[JS:, b])
        dcols_ref[0, JS:, :] = _dot(jnp.concatenate(ends, axis=1), r_ref[last])
        acc = jnp.zeros((JS + PAD_ROWS, 128), F32)
        for t in reversed(range(Tc)):
            if t != last:
                ex = _dot(lhs[:JS], e_ref[t])
            dvs, prods = [], []
            for i, b in enumerate(blocks):
                kk_e, b_e, k_e, r_e = (ex[n * J:(n + 1) * J, b] for n in range(N_STEP_VEC))
                do_row, v_row, sa_row = do_ref[t:t + 1, b], v_ref[t:t + 1, b], sa_ref[t:t + 1, b]
                s_old = states_ref[t, :, b]
                s_new = states_ref[t + 1, :, b] if t != last else s_old + b_e * sa_row + k_e * v_row
                dsn = dst[i] + r_e * do_row
                dsa = jnp.sum(dsn * b_e, axis=0, keepdims=True)
                dvs.append(jnp.sum(dsn * k_e, axis=0, keepdims=True))
                prods.append(jnp.concatenate(
                    [s_old * (-dsa), dsn * sa_row, dsn * v_row, s_new * do_row, jnp.zeros((PAD_ROWS, 128), F32)],
                    axis=0).astype(BF16))
                dst[i] = dsn - kk_e * dsa
            dv_ref[t:t + 1, :] = jnp.concatenate(dvs, axis=1)
            acc = acc + _dot(jnp.concatenate(prods, axis=1), r_ref[t])
        for i, b in enumerate(blocks):
            ds_ref[:, b] = dst[i]
        dcols_ref[0, :JS, :] = acc[:JS]

    rev2 = lambda c: (nch - 1 - c, 0)
    rev3 = lambda c: (nch - 1 - c, 0, 0)
    return pl.pallas_call(
        body, name="wkv_bwd", grid=(nch,),
        in_specs=[pl.BlockSpec((1, N_VEC * J, 128), rev3), pl.BlockSpec((Tc, W), rev2), pl.BlockSpec((Tc, W), rev2),
                  pl.BlockSpec((Tc, J, W), rev3), pl.BlockSpec((Tc, W), rev2),
                  _full((Tc, 2 * 128, W)), _full((Tc, W, 128))],
        out_specs=[pl.BlockSpec((Tc, W), rev2), pl.BlockSpec((1, N_VEC * J, 128), rev3)],
        out_shape=[jax.ShapeDtypeStruct((T, W), F32), jax.ShapeDtypeStruct((nch, N_VEC * J, 128), F32)],
        scratch_shapes=[pltpu.VMEM((J, W), F32)],
        compiler_params=_cparams(dimension_semantics=("arbitrary",)),
    )(cols, v, do, states, sa, e_tab, r_tab)


def _outproj(x, y_ret, y_rw, w_out_b, target, gf):
    T = x.shape[0]
    tm = _row_tile(T)
    W = RW_WIDTH

    def body(x_ref, yr_ref, yw_ref, w_ref, t_ref, gf_ref, loss_ref, dh_ref, dy_ref, dw_ref, dgf_ref):
        @pl.when(pl.program_id(0) == 0)
        def _():
            loss_ref[...] = jnp.zeros_like(loss_ref)
            dw_ref[...] = jnp.zeros_like(dw_ref)
            dgf_ref[...] = jnp.zeros_like(dgf_ref)

        y = jnp.concatenate([yr_ref[...], yw_ref[...]], axis=1)
        w = w_ref[...]
        h = x_ref[...] + _dot(y, w)
        rstd = lax.rsqrt(jnp.mean(h * h, axis=-1, keepdims=True) + RMS_EPS)
        hn = h * rstd
        gfv = gf_ref[...]
        err = hn * gfv - t_ref[...]
        loss_ref[...] += 0.5 * jnp.sum(jnp.mean(err * err, axis=-1))
        dout = err * (1.0 / D_MODEL)
        dgf_ref[...] += jnp.sum(dout * hn, axis=0, keepdims=True)
        dhn = dout * gfv
        dh = rstd * (dhn - hn * jnp.mean(dhn * hn, axis=-1, keepdims=True))
        dh_ref[...] = dh
        dhb = dh.astype(BF16)
        dy_ref[...] = _dot_nt(dhb, w)
        dw_ref[...] += _dot_tn(y, dhb)

    return pl.pallas_call(
        body, name="outproj_loss", grid=(T // tm,),
        in_specs=[pl.BlockSpec((tm, D_MODEL), lambda i: (i, 0)), pl.BlockSpec((tm, W), lambda i: (i, 0)),
                  pl.BlockSpec((tm, W), lambda i: (i, 0)), _full((D_MODEL, D_MODEL)),
                  pl.BlockSpec((tm, D_MODEL), lambda i: (i, 0)), _full((1, D_MODEL))],
        out_specs=[_full((1, PACK_W)), pl.BlockSpec((tm, D_MODEL), lambda i: (i, 0)),
                   pl.BlockSpec((tm, D_MODEL), lambda i: (i, 0)), _full((D_MODEL, D_MODEL)), _full((1, D_MODEL))],
        out_shape=[jax.ShapeDtypeStruct((1, PACK_W), F32), jax.ShapeDtypeStruct((T, D_MODEL), F32),
                   jax.ShapeDtypeStruct((T, D_MODEL), F32), jax.ShapeDtypeStruct((D_MODEL, D_MODEL), F32),
                   jax.ShapeDtypeStruct((1, D_MODEL), F32)],
        compiler_params=_cparams(dimension_semantics=("arbitrary",)),
    )(x, y_ret, y_rw, w_out_b, target, gf)


def _inproj_bwd_x(dp_ret, dp_rw, dprev, dbnd, w_b, x, norm_g, dh):
    T = x.shape[0]
    tm = _row_tile(T)

    def body(dpr_ref, dpw_ref, dprev_ref, dbnd_ref, w_ref, x_ref, g_ref, dh_ref, gx_ref, dg_ref, dpb_ref):
        @pl.when(pl.program_id(0) == 0)
        def _():
            dg_ref[...] = jnp.zeros_like(dg_ref)

        d_rw = dpw_ref[...] + _shift_up(dprev_ref[...], dbnd_ref[0])
        dpb = jnp.concatenate([dpr_ref[...].astype(BF16), d_rw.astype(BF16)], axis=1)
        dpb_ref[...] = dpb
        du = _dot_nt(dpb, w_ref[...])
        xf = x_ref[...]
        rstd = lax.rsqrt(jnp.mean(xf * xf, axis=-1, keepdims=True) + RMS_EPS)
        xn = xf * rstd
        dg_ref[...] += jnp.sum(du * xn, axis=0, keepdims=True)
        dxn = du * g_ref[...]
        gx_ref[...] = dh_ref[...] + rstd * (dxn - xn * jnp.mean(dxn * xn, axis=-1, keepdims=True))

    return pl.pallas_call(
        body, name="inproj_bwd_x", grid=(T // tm,),
        in_specs=[pl.BlockSpec((tm, RET_COLS), lambda i: (i, 0)), pl.BlockSpec((tm, RW_COLS), lambda i: (i, 0)),
                  pl.BlockSpec((tm, RW_COLS), lambda i: (i, 0)), pl.BlockSpec((1, 1, RW_COLS), lambda i: (i, 0, 0)),
                  _full((D_MODEL, IN_COLS)), pl.BlockSpec((tm, D_MODEL), lambda i: (i, 0)), _full((1, D_MODEL)),
                  pl.BlockSpec((tm, D_MODEL), lambda i: (i, 0))],
        out_specs=[pl.BlockSpec((tm, D_MODEL), lambda i: (i, 0)), _full((1, D_MODEL)),
                   pl.BlockSpec((tm, IN_COLS), lambda i: (i, 0))],
        out_shape=[jax.ShapeDtypeStruct((T, D_MODEL), F32), jax.ShapeDtypeStruct((1, D_MODEL), F32),
                   jax.ShapeDtypeStruct((T, IN_COLS), BF16)],
        compiler_params=_cparams(dimension_semantics=("arbitrary",)),
    )(dp_ret, dp_rw, dprev, dbnd, w_b, x, norm_g, dh)


def _inproj_bwd_w(u_t, dpb):
    T = u_t.shape[1]
    tr = 256

    def body(u_ref, d_ref, o_ref):
        o_ref[...] = _dot(u_ref[...], d_ref[...])

    return pl.pallas_call(
        body, name="inproj_bwd_w", grid=(D_MODEL // tr,),
        in_specs=[pl.BlockSpec((tr, T), lambda i: (i, 0)), _full((T, IN_COLS))],
        out_specs=pl.BlockSpec((tr, IN_COLS), lambda i: (i, 0)),
        out_shape=jax.ShapeDtypeStruct((D_MODEL, IN_COLS), F32),
        compiler_params=_cparams(dimension_semantics=("arbitrary",)),
    )(u_t, dpb)


def _tile_boundaries(a, tm, first):
    T, n = a.shape
    zero = jnp.zeros((1, n), a.dtype)
    if first:
        rows = jnp.concatenate([zero, a[tm - 1:T - 1:tm]], axis=0)
    else:
        rows = jnp.concatenate([a[tm:T:tm], zero], axis=0)
    return rows.reshape(T // tm, 1, n)


def _local_step(x, target, w_in_b, w_out_b, lora, small):
    T = x.shape[0]
    tm = _row_tile(T)
    cos, sin = _rope_tables(T)
    tabs = _ret_tables()
    seg128 = _seg_matrix(RET_WIDTH, RET_DV)
    seg64 = _seg_matrix(RW_WIDTH, RW_HEAD)
    e_tab = _wkv_expand_table()
    r_tab = _wkv_reduce_table()
    prep_w = (small["rwkv_mu"], small["w0"], small["a0"], small["k_k"], small["k_a"], lora, seg64, _chunk_tables(tm))
    post_w = (small["rwkv_gn_g"], small["rwkv_gn_b"], small["r_k"], seg64)

    p_ret, p_rw, u = _inproj_fwd(x, small["norm_g"], w_in_b)
    y_ret, ret, s_in_all = _ret_fwd(p_ret, cos, sin, tabs, small["ret_gn_g"], seg128)
    bnd = _tile_boundaries(p_rw, tm, True)
    vecs, v, g = _prep_fwd(p_rw, bnd, *prep_w)
    cols = _to_cols(vecs, T)
    o, states, sa = _wkv_fwd(cols, v, e_tab)
    y_rw = _post_fwd(o, vecs, v, g, *post_w)
    loss, dh, dy, d_w_out, d_gf = _outproj(x, y_ret, y_rw, w_out_b, target, small["final_norm_g"])

    do, dr2, dk2, dv2, dg, d_gn_g, d_gn_b, d_r_k = _post_bwd(o, vecs, v, g, *post_w, dy)
    dv1, dcols = _wkv_bwd(cols, v, do, states, sa, e_tab, r_tab)
    dp_rw, dprev, d_mu, d_w0, d_a0, d_k_k, d_k_a, d_lora = _prep_bwd(
        p_rw, bnd, *prep_w, (_from_cols(dcols, T), dk2, dr2, dv1, dv2, dg))
    dp_ret, d_ret_gn = _ret_bwd(p_ret, cos, sin, tabs, small["ret_gn_g"], seg128, ret, s_in_all, dy)
    dbnd = _tile_boundaries(dprev, tm, False)
    grad_x, d_norm_g, dpb = _inproj_bwd_x(dp_ret, dp_rw, dprev, dbnd, w_in_b, x, small["norm_g"], dh)
    d_w_in = _inproj_bwd_w(jnp.transpose(u), dpb)

    d_small = {"norm_g": d_norm_g, "ret_gn_g": d_ret_gn, "rwkv_mu": d_mu, "w0": d_w0, "a0": d_a0, "k_k": d_k_k,
               "k_a": d_k_a, "r_k": d_r_k, "rwkv_gn_g": d_gn_g, "rwkv_gn_b": d_gn_b, "final_norm_g": d_gf}
    return loss, grad_x, d_w_in, d_w_out, d_lora, d_small


ANY = pl.BlockSpec(memory_space=pl.ANY)
CHIP_FLIPS = ((0, 1), (1, 0), (1, 1))
N_FLIPS = len(CHIP_FLIPS)
LORA_SHARD = RW_WIDTH // N_CHIPS
HALF_IN = D_MODEL // 2
HALF_OUT = OUT_SHARD // 2


def _position():
    return lax.axis_index("x"), lax.axis_index("y"), lax.axis_index("c")


def _flip(v, f):
    return 1 - v if f else v


def _finish(local, remote, landed):
    for cp in landed:
        cp.wait_recv()
    for cp in remote:
        cp.wait_send()
    for cp in local:
        cp.wait()


def _gather_chips(arrs):
    n = len(arrs)

    def body(*refs):
        ins, outs = refs[:n], refs[n:2 * n]
        send, recv, pass_send, pass_recv = refs[2 * n:]
        x, y, c = _position()
        s = 2 * x + y
        sibling = (x, y, 1 - c)

        def copy(src, dst, sems, k, to):
            return pltpu.make_async_remote_copy(src_ref=src, dst_ref=dst, send_sem=sems[0].at[k], recv_sem=sems[1].at[k],
                                                device_id=to, device_id_type=MESH)

        remote, landed, passed, passed_in = [], [], [], []
        for a in range(n):
            for j, (fx, fy) in enumerate(CHIP_FLIPS):
                px, py = _flip(x, fx), _flip(y, fy)
                ps = 2 * px + py
                k = a * N_FLIPS + j
                remote.append(copy(ins[a].at[c], outs[a].at[s, c], (send, recv), k, (px, py, c)))
                landed.append(copy(ins[a].at[c], outs[a].at[ps, c], (send, recv), k, (px, py, c)))
                passed.append(copy(outs[a].at[ps, c], outs[a].at[ps, c], (pass_send, pass_recv), k, sibling))
                passed_in.append(copy(outs[a].at[ps, 1 - c], outs[a].at[ps, 1 - c], (pass_send, pass_recv), k, sibling))
        for cp in remote:
            cp.start()
        for arrived, onward in zip(landed, passed):
            arrived.wait_recv()
            onward.start()
        _finish([], remote + passed, passed_in)

    sems = pltpu.SemaphoreType.DMA((n * N_FLIPS,))
    return pl.pallas_call(
        body, name="gather_weights",
        in_specs=[ANY] * n, out_specs=[ANY] * n,
        out_shape=[jax.ShapeDtypeStruct((N_CHIPS,) + a.shape, a.dtype) for a in arrs],
        scratch_shapes=[sems, sems, sems, sems],
    )(*arrs)


def _pair_exchange(g_in, g_out, g_small):
    def body(gi_ref, go_ref, gs_ref, li_ref, lo_ref, ls_ref, send, recv):
        x, y, c = _position()
        peer = (x, y, 1 - c)
        srcs = (gi_ref.at[:, pl.ds((1 - c) * HALF_IN, HALF_IN), :], go_ref.at[:, pl.ds((1 - c) * HALF_OUT, HALF_OUT), :],
                gs_ref)
        remote = [pltpu.make_async_remote_copy(src_ref=src, dst_ref=dst, send_sem=send.at[k], recv_sem=recv.at[k],
                                               device_id=peer, device_id_type=MESH)
                  for k, (src, dst) in enumerate(zip(srcs, (li_ref, lo_ref, ls_ref)))]
        for cp in remote:
            cp.start()
        _finish([], remote, remote)

    return pl.pallas_call(
        body, name="pair_exchange",
        in_specs=[ANY] * 3, out_specs=[ANY] * 3,
        out_shape=[jax.ShapeDtypeStruct((N_CHIPS, HALF_IN, IN_SHARD), F32),
                   jax.ShapeDtypeStruct((N_CHIPS, HALF_OUT, D_MODEL), F32),
                   jax.ShapeDtypeStruct(g_small.shape, F32)],
        scratch_shapes=[pltpu.SemaphoreType.DMA((3,)), pltpu.SemaphoreType.DMA((3,))],
    )(g_in, g_out, g_small)


def _pair_sum(g_in, g_out, g_small, l_in, l_out, l_small, c_arr):
    tr = HALF_IN // 2

    def body(c_ref, gi_ref, go_ref, gs_ref, li_ref, lo_ref, ls_ref, ci_ref, co_ref, cs_ref):
        ci_ref[...] = (gi_ref[...] + li_ref[...]).astype(BF16)

        @pl.when(pl.program_id(1) == 0)
        def _():
            co_ref[...] = (go_ref[...] + lo_ref[...]).astype(BF16)

        @pl.when((pl.program_id(0) == 0) & (pl.program_id(1) == 0))
        def _():
            cs_ref[...] = gs_ref[...] + ls_ref[...]

    nd = g_small.shape
    return pl.pallas_call(
        body, name="pair_sum",
        grid_spec=pltpu.PrefetchScalarGridSpec(
            num_scalar_prefetch=1, grid=(N_CHIPS, 2),
            in_specs=[pl.BlockSpec((1, tr, IN_SHARD), lambda s, i, c: (s, 2 * c[0] + i, 0)),
                      pl.BlockSpec((1, HALF_OUT, D_MODEL), lambda s, i, c: (s, c[0], 0)),
                      pl.BlockSpec(nd, lambda s, i, c: (0, 0)),
                      pl.BlockSpec((1, tr, IN_SHARD), lambda s, i, c: (s, i, 0)),
                      pl.BlockSpec((1, HALF_OUT, D_MODEL), lambda s, i, c: (s, 0, 0)),
                      pl.BlockSpec(nd, lambda s, i, c: (0, 0))],
            out_specs=[pl.BlockSpec((1, tr, IN_SHARD), lambda s, i, c: (s, i, 0)),
                       pl.BlockSpec((1, HALF_OUT, D_MODEL), lambda s, i, c: (s, 0, 0)),
                       pl.BlockSpec(nd, lambda s, i, c: (0, 0))]),
        out_shape=[jax.ShapeDtypeStruct((N_CHIPS, HALF_IN, IN_SHARD), BF16),
                   jax.ShapeDtypeStruct((N_CHIPS, HALF_OUT, D_MODEL), BF16), jax.ShapeDtypeStruct(nd, F32)],
        compiler_params=_cparams(dimension_semantics=("arbitrary", "arbitrary")),
    )(c_arr, g_in, g_out, g_small, l_in, l_out, l_small)


def _chip_exchange(c_in, c_out, c_small):
    def body(ci_ref, co_ref, cs_ref, li_ref, lo_ref, ls_ref, send, recv):
        x, y, c = _position()
        s = 2 * x + y
        remote = []
        for j, (fx, fy) in enumerate(CHIP_FLIPS):
            px, py = _flip(x, fx), _flip(y, fy)
            ps = 2 * px + py
            for a, (src, dst) in enumerate(((ci_ref.at[ps], li_ref.at[j]), (co_ref.at[ps], lo_ref.at[j]),
                                            (cs_ref, ls_ref.at[j]))):
                k = 3 * j + a
                remote.append(pltpu.make_async_remote_copy(src_ref=src, dst_ref=dst, send_sem=send.at[k],
                                                           recv_sem=recv.at[k], device_id=(px, py, c),
                                                           device_id_type=MESH))
        for cp in remote:
            cp.start()
        _finish([], remote, remote)

    return pl.pallas_call(
        body, name="chip_exchange",
        in_specs=[ANY] * 3, out_specs=[ANY] * 3,
        out_shape=[jax.ShapeDtypeStruct((N_FLIPS, HALF_IN, IN_SHARD), c_in.dtype),
                   jax.ShapeDtypeStruct((N_FLIPS, HALF_OUT, D_MODEL), c_out.dtype),
                   jax.ShapeDtypeStruct((N_FLIPS,) + c_small.shape, F32)],
        scratch_shapes=[pltpu.SemaphoreType.DMA((3 * N_FLIPS,)), pltpu.SemaphoreType.DMA((3 * N_FLIPS,))],
    )(c_in, c_out, c_small)


def _chip_sum(g_in, g_out, p_in, p_out, c_small, l_in, l_out, l_small, sc_arr):
    tr = HALF_IN // 2
    nd = c_small.shape

    def body(s_ref, gi_ref, go_ref, pi_ref, po_ref, cs_ref, li0, li1, li2, lo0, lo1, lo2, ls_ref,
             ri_ref, ro_ref, rs_ref):
        ri_ref[...] = (((gi_ref[0] + pi_ref[0]) + li0[0].astype(F32)) + li1[0].astype(F32)) + li2[0].astype(F32)

        @pl.when(pl.program_id(0) == 0)
        def _():
            ro_ref[...] = (((go_ref[0] + po_ref[0]) + lo0[0].astype(F32)) + lo1[0].astype(F32)) + lo2[0].astype(F32)
            me = s_ref[0]
            parts = (cs_ref[...], ls_ref[0], ls_ref[1], ls_ref[2])

            def of_chip(s):
                m = jnp.bitwise_xor(me, s)
                return jnp.where(m == 0, parts[0], jnp.where(m == 1, parts[1], jnp.where(m == 2, parts[2], parts[3])))

            rs_ref[...] = ((of_chip(0) + of_chip(1)) + of_chip(2)) + of_chip(3)

    def flip_in(j):
        return pl.BlockSpec((1, tr, IN_SHARD), lambda i, s: (j, i, 0))

    def flip_out(j):
        return pl.BlockSpec((1, HALF_OUT, D_MODEL), lambda i, s: (j, 0, 0))

    return pl.pallas_call(
        body, name="chip_sum",
        grid_spec=pltpu.PrefetchScalarGridSpec(
            num_scalar_prefetch=1, grid=(2,),
            in_specs=[pl.BlockSpec((1, tr, IN_SHARD), lambda i, s: (s[0], 2 * s[1] + i, 0)),
                      pl.BlockSpec((1, HALF_OUT, D_MODEL), lambda i, s: (s[0], s[1], 0)),
                      pl.BlockSpec((1, tr, IN_SHARD), lambda i, s: (s[0], i, 0)),
                      pl.BlockSpec((1, HALF_OUT, D_MODEL), lambda i, s: (s[0], 0, 0)),
                      pl.BlockSpec(nd, lambda i, s: (0, 0)),
                      flip_in(0), flip_in(1), flip_in(2), flip_out(0), flip_out(1), flip_out(2),
                      pl.BlockSpec((N_FLIPS,) + nd, lambda i, s: (0, 0, 0))],
            out_specs=[pl.BlockSpec((tr, IN_SHARD), lambda i, s: (i, 0)),
                       pl.BlockSpec((HALF_OUT, D_MODEL), lambda i, s: (0, 0)),
                       pl.BlockSpec(nd, lambda i, s: (0, 0))]),
        out_shape=[jax.ShapeDtypeStruct((HALF_IN, IN_SHARD), F32), jax.ShapeDtypeStruct((HALF_OUT, D_MODEL), F32),
                   jax.ShapeDtypeStruct(nd, F32)],
        compiler_params=_cparams(dimension_semantics=("arbitrary",)),
    )(sc_arr, g_in, g_out, p_in, p_out, c_small, l_in, l_in, l_in, l_out, l_out, l_out, l_small)


def _pair_share(r_in, r_out):
    def body(ri_ref, ro_ref, li_ref, lo_ref, send, recv):
        x, y, c = _position()
        remote = [pltpu.make_async_remote_copy(src_ref=src, dst_ref=dst, send_sem=send.at[k], recv_sem=recv.at[k],
                                               device_id=(x, y, 1 - c), device_id_type=MESH)
                  for k, (src, dst) in enumerate(((ri_ref, li_ref), (ro_ref, lo_ref)))]
        for cp in remote:
            cp.start()
        _finish([], remote, remote)

    return pl.pallas_call(
        body, name="pair_share",
        in_specs=[ANY] * 2, out_specs=[ANY] * 2,
        out_shape=[jax.ShapeDtypeStruct(r_in.shape, F32), jax.ShapeDtypeStruct(r_out.shape, F32)],
        scratch_shapes=[pltpu.SemaphoreType.DMA((2,)), pltpu.SemaphoreType.DMA((2,))],
    )(r_in, r_out)


def _adam_update(w, g, m, v):
    mn = ADAM_B1 * m + (1.0 - ADAM_B1) * g
    vn = ADAM_B2 * v + (1.0 - ADAM_B2) * jnp.square(g)
    m_hat = mn / (1.0 - ADAM_B1 ** ADAM_STEP)
    v_hat = vn / (1.0 - ADAM_B2 ** ADAM_STEP)
    return -ADAM_LR * (m_hat / (jnp.sqrt(v_hat) + ADAM_EPS) + ADAM_WD * w), mn, vn


def _adamw(name, w, g_mine, g_theirs, m, v, core_arr, tr):
    rows, cols = w.shape
    per_half = rows // 2 // tr

    def body(c_ref, w_ref, gm_ref, gt_ref, m_ref, v_ref, g_ref, d_ref, nm_ref, nv_ref):
        mine = (pl.program_id(0) // per_half) == c_ref[0]
        g = jnp.where(mine, gm_ref[...], gt_ref[...])
        d, mn, vn = _adam_update(w_ref[...], g, m_ref[...], v_ref[...])
        g_ref[...] = g
        d_ref[...] = d
        nm_ref[...] = mn
        nv_ref[...] = vn

    spec = pl.BlockSpec((tr, cols), lambda i, c: (i, 0))
    half = pl.BlockSpec((tr, cols), lambda i, c: (i % per_half, 0))
    return pl.pallas_call(
        body, name=name,
        grid_spec=pltpu.PrefetchScalarGridSpec(
            num_scalar_prefetch=1, grid=(rows // tr,),
            in_specs=[spec, half, half, spec, spec], out_specs=[spec] * 4),
        out_shape=[jax.ShapeDtypeStruct((rows, cols), F32)] * 4,
        compiler_params=_cparams(dimension_semantics=("arbitrary",)),
    )(core_arr, w, g_mine, g_theirs, m, v)


def _row_pieces(n):
    return [(k, k * PACK_W, min(PACK_W, n - k * PACK_W)) for k in range(-(-n // PACK_W))]


def _pack_small(d_small, loss, d_lora):
    ns = len(SMALL_NAMES)

    def body(*refs):
        small_refs, (loss_ref, lora_ref, out_ref) = refs[:ns], refs[ns:]
        out_ref[...] = jnp.zeros_like(out_ref)
        out_ref[PACK_LORA_W:PACK_LORA_W + LORA, :] = lora_ref[:LORA, :RW_WIDTH]
        out_ref[PACK_LORA_A:PACK_LORA_A + LORA, :] = lora_ref[LORA:, RW_WIDTH:]
        for name, n, ref in zip(SMALL_NAMES, SMALL_SIZES, small_refs):
            for k, at, w in _row_pieces(n):
                out_ref[PACK_AT[name] + k:PACK_AT[name] + k + 1, 0:w] = ref[:, at:at + w]
        out_ref[PACK_LOSS:PACK_LOSS + 1, :] = loss_ref[...]

    return pl.pallas_call(body, name="pack_small", out_shape=jax.ShapeDtypeStruct((PACK_ROWS, PACK_W), F32),
                          compiler_params=_cparams())(*d_small, loss, d_lora)


def _adamw_small(tot, chip_arr, ws, ms, vs):
    ns = len(SMALL_NAMES)
    n_par = ns + 2

    def body(s_ref, tot_ref, glw_ref, gla_ref, *refs):
        w_refs, m_refs, v_refs = refs[:n_par], refs[n_par:2 * n_par], refs[2 * n_par:3 * n_par]
        outs = refs[3 * n_par:]
        g_refs, d_refs, nm_refs, nv_refs = (outs[i * n_par:(i + 1) * n_par] for i in range(4))
        grads = [jnp.concatenate([tot_ref[PACK_AT[name] + k:PACK_AT[name] + k + 1, 0:w] for k, _, w in _row_pieces(n)],
                                 axis=1) for name, n in zip(SMALL_NAMES, SMALL_SIZES)]
        grads += [glw_ref[...], gla_ref[...]]
        for i, g in enumerate(grads):
            d, mn, vn = _adam_update(w_refs[i][...], g, m_refs[i][...], v_refs[i][...])
            g_refs[i][...] = g
            d_refs[i][...] = d
            nm_refs[i][...] = mn
            nv_refs[i][...] = vn

    def whole(a):
        nd = a.ndim
        return pl.BlockSpec(a.shape, lambda i, s: (0,) * nd)

    shard = (LORA, LORA_SHARD)
    par_specs = [whole(a) for a in ws]
    res = pl.pallas_call(
        body, name="adamw_small",
        grid_spec=pltpu.PrefetchScalarGridSpec(
            num_scalar_prefetch=1, grid=(1,),
            in_specs=[whole(tot), pl.BlockSpec(shard, lambda i, s: (PACK_LORA_W // LORA, s[0])),
                      pl.BlockSpec(shard, lambda i, s: (PACK_LORA_A // LORA, s[0]))] + par_specs * 3,
            out_specs=par_specs * 4),
        out_shape=[jax.ShapeDtypeStruct(a.shape, F32) for a in ws] * 4,
        compiler_params=_cparams(dimension_semantics=("arbitrary",)),
    )(chip_arr, tot, tot, tot, *ws, *ms, *vs)
    return [res[i * n_par:(i + 1) * n_par] for i in range(4)]


def kernel(x, norm_g, w_in, ret_gn_g, rwkv_mu, w_lora_up, w0, a_lora_up, a0, k_k, k_a, r_k, rwkv_gn_g, rwkv_gn_b, w_out, final_norm_g, loss_target, m_norm_g, m_w_in, m_ret_gn_g, m_rwkv_mu, m_w_lora_up, m_w0, m_a_lora_up, m_a0, m_k_k, m_k_a, m_r_k, m_rwkv_gn_g, m_rwkv_gn_b, m_w_out, m_final_norm_g, v_norm_g, v_w_in, v_ret_gn_g, v_rwkv_mu, v_w_lora_up, v_w0, v_a_lora_up, v_a0, v_k_k, v_k_a, v_r_k, v_rwkv_gn_g, v_rwkv_gn_b, v_w_out, v_final_norm_g):
    W = RW_WIDTH
    params = dict(norm_g=norm_g, ret_gn_g=ret_gn_g, rwkv_mu=rwkv_mu, w0=w0, a0=a0, k_k=k_k, k_a=k_a, r_k=r_k,
                  rwkv_gn_g=rwkv_gn_g, rwkv_gn_b=rwkv_gn_b, final_norm_g=final_norm_g)
    moments_m = dict(norm_g=m_norm_g, ret_gn_g=m_ret_gn_g, rwkv_mu=m_rwkv_mu, w0=m_w0, a0=m_a0, k_k=m_k_k, k_a=m_k_a,
                     r_k=m_r_k, rwkv_gn_g=m_rwkv_gn_g, rwkv_gn_b=m_rwkv_gn_b, final_norm_g=m_final_norm_g)
    moments_v = dict(norm_g=v_norm_g, ret_gn_g=v_ret_gn_g, rwkv_mu=v_rwkv_mu, w0=v_w0, a0=v_a0, k_k=v_k_k, k_a=v_k_a,
                     r_k=v_r_k, rwkv_gn_g=v_rwkv_gn_g, rwkv_gn_b=v_rwkv_gn_b, final_norm_g=v_final_norm_g)
    xi, yi, ci = _position()
    chip = (2 * xi + yi).astype(jnp.int32)

    def halves(a):
        return a.reshape(2, a.shape[0] // 2, a.shape[1])

    mine = [halves(w_in[0].astype(BF16)), halves(w_out[0].astype(BF16)), halves(w_lora_up[0]), halves(a_lora_up[0])]
    g_in, g_out, g_lw, g_la = [lax.dynamic_update_slice(g, own[None], (chip, 0, 0, 0))
                               for g, own in zip(_gather_chips(mine), mine)]
    w_in_b = jnp.transpose(g_in.reshape(N_CHIPS, D_MODEL, IN_SHARD), (1, 0, 2)).reshape(D_MODEL, IN_COLS)
    w_out_b = g_out.reshape(D_MODEL, D_MODEL)
    lw = jnp.transpose(g_lw.reshape(N_CHIPS, LORA, LORA_SHARD), (1, 0, 2)).reshape(LORA, W)
    la = jnp.transpose(g_la.reshape(N_CHIPS, LORA, LORA_SHARD), (1, 0, 2)).reshape(LORA, W)
    zero = jnp.zeros((LORA, W), F32)
    lora = jnp.concatenate([jnp.concatenate([lw, zero], axis=1), jnp.concatenate([zero, la], axis=1)], axis=0)
    small = {n: params[n].reshape(1, -1) for n in SMALL_NAMES}

    loss, grad_x, d_w_in, d_w_out, d_lora, d_small = _local_step(x[0], loss_target[0], w_in_b, w_out_b, lora, small)

    core = ci.astype(jnp.int32)
    gi = jnp.transpose(d_w_in.reshape(D_MODEL, N_CHIPS, IN_SHARD), (1, 0, 2))
    go = d_w_out.reshape(N_CHIPS, OUT_SHARD, D_MODEL)
    gs = _pack_small([d_small[n] for n in SMALL_NAMES], loss, d_lora)
    p_in, p_out, p_small = _pair_exchange(gi, go, gs)
    c_in, c_out, c_small = _pair_sum(gi, go, gs, p_in, p_out, p_small, core.reshape(1))
    l_in, l_out, l_small = _chip_exchange(c_in, c_out, c_small)
    r_in, r_out, tot = _chip_sum(gi, go, p_in, p_out, c_small, l_in, l_out, l_small, jnp.stack([chip, core]))
    t_in, t_out = _pair_share(r_in, r_out)

    grad_w_in, d_in, nm_in, nv_in = _adamw("adamw_w_in", w_in[0], r_in, t_in, m_w_in[0], v_w_in[0], core.reshape(1), 256)
    grad_w_out, d_out, nm_out, nv_out = _adamw("adamw_w_out", w_out[0], r_out, t_out, m_w_out[0], v_w_out[0],
                                               core.reshape(1), HALF_OUT)
    par_names = SMALL_NAMES + ("w_lora_up", "a_lora_up")

    def operands(tree, lw_, la_):
        return [tree[n].reshape(1, -1) for n in SMALL_NAMES] + [lw_[0], la_[0]]

    res = _adamw_small(tot, chip.reshape(1), operands(params, w_lora_up, a_lora_up),
                       operands(moments_m, m_w_lora_up, m_a_lora_up), operands(moments_v, v_w_lora_up, v_a_lora_up))

    names = ("norm_g", "w_in", "ret_gn_g", "rwkv_mu", "w_lora_up", "w0", "a_lora_up", "a0", "k_k", "k_a", "r_k",
             "rwkv_gn_g", "rwkv_gn_b", "w_out", "final_norm_g")
    shapes = dict(w_in=w_in.shape, w_out=w_out.shape, w_lora_up=w_lora_up.shape, a_lora_up=a_lora_up.shape,
                  **{n: params[n].shape for n in SMALL_NAMES})

    def leaves(pars, big_in, big_out):
        tree = dict(zip(par_names, pars), w_in=big_in, w_out=big_out)
        return [tree[n].reshape(shapes[n]) for n in names]

    grads = leaves(res[0], grad_w_in, grad_w_out)
    deltas = leaves(res[1], d_in, d_out)
    new_m = leaves(res[2], nm_in, nm_out)
    new_v = leaves(res[3], nv_in, nv_out)
    return (tot[PACK_LOSS, 0], grad_x.reshape(x.shape), *grads, *deltas, *new_m, *new_v)
```

```python
import functools

import numpy as np
import jax
import jax.numpy as jnp
from jax import lax
from jax.experimental import pallas as pl
from jax.experimental.pallas import tpu as pltpu

F32 = jnp.float32
BF16 = jnp.bfloat16
HIGHEST = lax.Precision.HIGHEST
MESH = pl.DeviceIdType.MESH

D_MODEL = 1024
N_CHIPS = 4
RET_HEADS = 4
RET_DK = 64
RET_DV = 128
RET_QK = RET_HEADS * RET_DK
RET_WIDTH = RET_HEADS * RET_DV
RET_COLS = 2 * RET_QK + 2 * RET_WIDTH
RET_CHUNK = 64
RET_GROUP = 4
RW_WIDTH = 512
RW_HEAD = 64
RW_HEADS = 8
LORA = 64
RW_COLS = 4 * RW_WIDTH + 2 * LORA
IN_COLS = RET_COLS + RW_COLS
IN_SHARD = IN_COLS // N_CHIPS
OUT_SHARD = D_MODEL // N_CHIPS
ROPE_BASE = 10000.0
RMS_EPS = 1e-6
RET_GN_EPS = 1e-5
RW_GN_EPS = 64e-5
WKV_CHUNK = 16
N_VEC = 5

ADAM_LR = 0.001
ADAM_B1 = 0.9
ADAM_B2 = 0.999
ADAM_EPS = 1e-08
ADAM_WD = 0.01
ADAM_STEP = 10

VMEM_LIMIT = 56 * 1024 * 1024

PACK_W = 512
SMALL_NAMES = ("norm_g", "ret_gn_g", "rwkv_mu", "w0", "a0", "k_k", "k_a", "r_k", "rwkv_gn_g", "rwkv_gn_b",
               "final_norm_g")
SMALL_SIZES = (1024, 512, 2176, 512, 512, 512, 512, 512, 512, 512, 1024)
PACK_LORA_W = 0
PACK_LORA_A = LORA
PACK_SMALL = 2 * LORA


def _pack_layout():
    rows, at = {}, PACK_SMALL
    for name, n in zip(SMALL_NAMES, SMALL_SIZES):
        rows[name] = at
        at += -(-n // PACK_W)
    return rows, at


PACK_AT, PACK_LOSS = _pack_layout()
PACK_ROWS = -(-(PACK_LOSS + 1) // 8) * 8


def _cparams(**kw):
    return pltpu.CompilerParams(vmem_limit_bytes=VMEM_LIMIT, **kw)


def _dot(a, b, precision=None):
    return jnp.dot(a, b, precision=precision, preferred_element_type=F32)


def _dot_nt(a, b, precision=None):
    return lax.dot_general(a, b, (((1,), (1,)), ((), ())), precision=precision, preferred_element_type=F32)


def _dot_tn(a, b, precision=None):
    return lax.dot_general(a, b, (((0,), (0,)), ((), ())), precision=precision, preferred_element_type=F32)


def _split(x):
    hi = x.astype(BF16)
    lo = (x - hi.astype(F32)).astype(BF16)
    return hi, lo


@jax.custom_vjp
def _segsum(x, seg):
    hi, lo = _split(x)
    return _dot(hi, seg) + _dot(lo, seg)


def _segsum_fwd(x, seg):
    return _segsum(x, seg), seg


def _segsum_bwd(seg, ct):
    return _segsum(ct, seg), jnp.zeros_like(seg)


_segsum.defvjp(_segsum_fwd, _segsum_bwd)


def _softplus(z):
    return jnp.maximum(z, 0.0) + jnp.log(1.0 + jnp.exp(-jnp.abs(z)))


def _full(shape):
    nd = len(shape)
    return pl.BlockSpec(shape, lambda *_: (0,) * nd)


def _rope_tables(T):
    half = RET_DK // 2
    expo = -jnp.arange(half, dtype=F32) / jnp.float32(half)
    freqs = jnp.exp(expo * jnp.float32(np.log(ROPE_BASE)))
    ang = jnp.arange(T, dtype=jnp.int32).astype(F32)[:, None] * freqs[None, :]
    cos = jnp.tile(jnp.cos(ang), (1, 2 * RET_HEADS))
    sin = jnp.tile(jnp.sin(ang), (1, 2 * RET_HEADS))
    return cos, sin


def _ret_tables():
    H, C = RET_HEADS, RET_CHUNK
    hidx = jnp.arange(H, dtype=F32)
    lg = jnp.log(1.0 - jnp.exp2(-5.0 - hidx))
    idx = jnp.arange(C, dtype=F32)
    intra = jnp.exp(lg[:, None, None] * jnp.abs(idx[:, None] - idx[None, :]))
    q_dec = jnp.transpose(jnp.exp(lg[:, None] * (idx[None, :] + 1.0)))
    k_dec = jnp.transpose(jnp.exp(lg[:, None] * (C - 1.0 - idx[None, :])))
    chunk_dec = jnp.exp(lg * C)
    qd = jnp.repeat(q_dec, RET_DK, axis=1)
    kd = jnp.repeat(k_dec, RET_DK, axis=1)
    row_h = np.arange(RET_QK) // RET_DK
    col_h = np.arange(RET_WIDTH) // RET_DV
    bm = jnp.asarray((row_h[:, None] == col_h[None, :]).astype(np.float32))
    cd = bm * jnp.repeat(chunk_dec, RET_DK)[:, None]
    return intra, qd, kd, cd, bm


def _seg_matrix(width, head):
    h = np.arange(width) // head
    return jnp.asarray((h[:, None] == h[None, :]).astype(np.float32), dtype=BF16)


def _wkv_expand_table():
    Tc = WKV_CHUNK
    k = np.arange(2 * RW_HEADS * Tc)
    kh, kt = (k % (RW_HEADS * Tc)) // Tc, k % Tc
    nh = np.arange(RW_WIDTH) // RW_HEAD
    e = (kh[None, :, None] == nh[None, None, :]) & (kt[None, :, None] == np.arange(Tc)[:, None, None])
    return jnp.asarray(e.astype(np.float32), dtype=BF16)


def _wkv_reduce_table():
    Tc = WKV_CHUNK
    kh = np.arange(RW_WIDTH) // RW_HEAD
    n = np.arange(RW_HEADS * Tc)
    nh, nt = n // Tc, n % Tc
    r = (kh[None, :, None] == nh[None, None, :]) & (nt[None, None, :] == np.arange(Tc)[:, None, None])
    return jnp.asarray(r.astype(np.float32), dtype=BF16)


def _inproj_fwd(x, norm_g, w_b):
    T = x.shape[0]
    tm = min(T, 256)

    def body(x_ref, g_ref, w_ref, pret_ref, prw_ref, u_ref):
        xf = x_ref[...]
        rstd = lax.rsqrt(jnp.mean(xf * xf, axis=-1, keepdims=True) + RMS_EPS)
        ub = ((xf * rstd) * g_ref[...]).astype(BF16)
        u_ref[...] = ub
        pret_ref[...] = _dot(ub, w_ref[:, :RET_COLS])
        prw_ref[...] = _dot(ub, w_ref[:, RET_COLS:])

    return pl.pallas_call(
        body, name="inproj_fwd", grid=(T // tm,),
        in_specs=[pl.BlockSpec((tm, D_MODEL), lambda i: (i, 0)), _full((1, D_MODEL)), _full((D_MODEL, IN_COLS))],
        out_specs=[pl.BlockSpec((tm, RET_COLS), lambda i: (i, 0)), pl.BlockSpec((tm, RW_COLS), lambda i: (i, 0)),
                   pl.BlockSpec((tm, D_MODEL), lambda i: (i, 0))],
        out_shape=[jax.ShapeDtypeStruct((T, RET_COLS), F32), jax.ShapeDtypeStruct((T, RW_COLS), F32),
                   jax.ShapeDtypeStruct((T, D_MODEL), BF16)],
        compiler_params=_cparams(dimension_semantics=("arbitrary",)),
    )(x, norm_g, w_b)


def _rot_half(x):
    n = x.shape[1]
    lane = lax.broadcasted_iota(jnp.int32, x.shape, 1)
    first = (lane % RET_DK) < (RET_DK // 2)
    return jnp.where(first, -pltpu.roll(x, n - RET_DK // 2, 1), pltpu.roll(x, RET_DK // 2, 1))


def _rope(x, cos, sin):
    return x * cos + _rot_half(x) * sin


def _rope_bwd(d, cos, sin):
    return d * cos - _rot_half(d * sin)


def _ret_post(ret, g, gn_g, seg):
    mu = _segsum(ret, seg) * (1.0 / RET_DV)
    xc = ret - mu
    var = _segsum(xc * xc, seg) * (1.0 / RET_DV)
    n = xc * lax.rsqrt(var + RET_GN_EPS)
    return (g * jax.nn.sigmoid(g)) * (n * gn_g)


def _ret_scores(qt, kt, d_ref, h):
    lane = lax.broadcasted_iota(jnp.int32, qt.shape, 1)
    qh = jnp.where(lane // RET_DK == h, qt, 0.0)
    return qh, _dot_nt(qh, kt, HIGHEST) * d_ref[h]


def _ret_group(nch):
    return min(RET_GROUP, nch)


def _ret_fwd(p_ret, cos, sin, tabs, gn_g, seg128):
    T = p_ret.shape[0]
    C = RET_CHUNK
    nch = T // C
    G = _ret_group(nch)
    intra_d, qd, kd, cd, bm = tabs

    def body(q_ref, k_ref, v_ref, g_ref, cos_ref, sin_ref, qd_ref, kd_ref, d_ref, cd_ref, bm_ref, gn_ref, seg_ref,
             y_ref, ret_ref, sin_out_ref, s_ref):
        @pl.when(pl.program_id(0) == 0)
        def _():
            s_ref[...] = jnp.zeros_like(s_ref)

        s_in = s_ref[...]
        for i in range(G):
            rows = slice(i * C, (i + 1) * C)
            cosv, sinv = cos_ref[rows, :], sin_ref[rows, :]
            qt = _rope(q_ref[rows, :], cosv, sinv)
            kt = _rope(k_ref[rows, :], cosv, sinv) * (RET_DK ** -0.5)
            v = v_ref[rows, :]
            sin_out_ref[i] = s_in
            inter = _dot(qt * qd_ref[...], s_in, HIGHEST)
            intra = []
            for h in range(RET_HEADS):
                _, a = _ret_scores(qt, kt, d_ref, h)
                intra.append(_dot(a, v[:, h * RET_DV:(h + 1) * RET_DV], HIGHEST))
            ret_ref[rows, :] = jnp.concatenate(intra, axis=1) + inter
            kv = _dot_tn(kt * kd_ref[...], v, HIGHEST)
            s_in = s_in * cd_ref[...] + kv * bm_ref[...]
        s_ref[...] = s_in
        y_ref[...] = _ret_post(ret_ref[...], g_ref[...], gn_ref[...], seg_ref[...]).astype(BF16)

    GC = G * C
    return pl.pallas_call(
        body, name="ret_fwd", grid=(nch // G,),
        in_specs=[pl.BlockSpec((GC, RET_QK), lambda c: (c, 0)), pl.BlockSpec((GC, RET_QK), lambda c: (c, 1)),
                  pl.BlockSpec((GC, RET_WIDTH), lambda c: (c, 1)), pl.BlockSpec((GC, RET_WIDTH), lambda c: (c, 2)),
                  pl.BlockSpec((GC, RET_QK), lambda c: (c, 0)), pl.BlockSpec((GC, RET_QK), lambda c: (c, 0)),
                  _full((C, RET_QK)), _full((C, RET_QK)), _full((RET_HEADS, C, C)),
                  _full((RET_QK, RET_WIDTH)), _full((RET_QK, RET_WIDTH)), _full((1, RET_WIDTH)),
                  _full((RET_WIDTH, RET_WIDTH))],
        out_specs=[pl.BlockSpec((GC, RET_WIDTH), lambda c: (c, 0)), pl.BlockSpec((GC, RET_WIDTH), lambda c: (c, 0)),
                   pl.BlockSpec((G, RET_QK, RET_WIDTH), lambda c: (c, 0, 0))],
        out_shape=[jax.ShapeDtypeStruct((T, RET_WIDTH), BF16), jax.ShapeDtypeStruct((T, RET_WIDTH), F32),
                   jax.ShapeDtypeStruct((nch, RET_QK, RET_WIDTH), F32)],
        scratch_shapes=[pltpu.VMEM((RET_QK, RET_WIDTH), F32)],
        compiler_params=_cparams(dimension_semantics=("arbitrary",)),
    )(p_ret, p_ret, p_ret, p_ret, cos, sin, qd, kd, intra_d, cd, bm, gn_g, seg128)


def _ret_bwd(p_ret, cos, sin, tabs, gn_g, seg128, ret, s_in_all, dy):
    T = p_ret.shape[0]
    C = RET_CHUNK
    nch = T // C
    G = _ret_group(nch)
    ngr = nch // G
    intra_d, qd, kd, cd, bm = tabs

    def rev(j):
        return lambda c: (ngr - 1 - c, j)

    def body(q_ref, k_ref, v_ref, g_ref, cos_ref, sin_ref, qd_ref, kd_ref, d_ref, cd_ref, bm_ref, gn_ref, seg_ref,
             ret_ref, sin_ref_, dy_ref, dp_ref, dgn_ref, ds_ref):
        @pl.when(pl.program_id(0) == 0)
        def _():
            ds_ref[...] = jnp.zeros_like(ds_ref)
            dgn_ref[...] = jnp.zeros_like(dgn_ref)

        seg = seg_ref[...]
        _, post_vjp = jax.vjp(lambda r_, g_, gn_: _ret_post(r_, g_, gn_, seg), ret_ref[...], g_ref[...], gn_ref[...])
        dret_all, dg_all, dgn = post_vjp(dy_ref[...])
        dgn_ref[...] += dgn
        dp_ref[:, 2 * RET_QK + RET_WIDTH:] = dg_all

        qdv, kdv = qd_ref[...], kd_ref[...]
        ds_out = ds_ref[...]
        for i in reversed(range(G)):
            rows = slice(i * C, (i + 1) * C)
            cosv, sinv = cos_ref[rows, :], sin_ref[rows, :]
            qt = _rope(q_ref[rows, :], cosv, sinv)
            kt = _rope(k_ref[rows, :], cosv, sinv) * (RET_DK ** -0.5)
            v = v_ref[rows, :]
            s_in = sin_ref_[i]
            dret = dret_all[rows, :]
            dqt = qdv * _dot_nt(dret, s_in, HIGHEST)
            dkt = kdv * _dot_nt(v, ds_out, HIGHEST)
            dv_all = _dot(kt * kdv, ds_out, HIGHEST)
            dvs = []
            for h in range(RET_HEADS):
                sl = slice(h * RET_DV, (h + 1) * RET_DV)
                qh, a = _ret_scores(qt, kt, d_ref, h)
                lane = lax.broadcasted_iota(jnp.int32, kt.shape, 1)
                kh = jnp.where(lane // RET_DK == h, kt, 0.0)
                da = _dot_nt(dret[:, sl], v[:, sl], HIGHEST) * d_ref[h]
                dvs.append(_dot_tn(a, dret[:, sl], HIGHEST))
                dqt = dqt + _dot(da, kh, HIGHEST)
                dkt = dkt + _dot_tn(da, qh, HIGHEST)
            ds_out = ds_out * cd_ref[...] + _dot_tn(qt * qdv, dret, HIGHEST) * bm_ref[...]
            dp_ref[rows, :RET_QK] = _rope_bwd(dqt, cosv, sinv)
            dp_ref[rows, RET_QK:2 * RET_QK] = _rope_bwd(dkt * (RET_DK ** -0.5), cosv, sinv)
            dp_ref[rows, 2 * RET_QK:2 * RET_QK + RET_WIDTH] = dv_all + jnp.concatenate(dvs, axis=1)
        ds_ref[...] = ds_out

    GC = G * C
    return pl.pallas_call(
        body, name="ret_bwd", grid=(ngr,),
        in_specs=[pl.BlockSpec((GC, RET_QK), rev(0)), pl.BlockSpec((GC, RET_QK), rev(1)),
                  pl.BlockSpec((GC, RET_WIDTH), rev(1)), pl.BlockSpec((GC, RET_WIDTH), rev(2)),
                  pl.BlockSpec((GC, RET_QK), rev(0)), pl.BlockSpec((GC, RET_QK), rev(0)),
                  _full((C, RET_QK)), _full((C, RET_QK)), _full((RET_HEADS, C, C)),
                  _full((RET_QK, RET_WIDTH)), _full((RET_QK, RET_WIDTH)), _full((1, RET_WIDTH)),
                  _full((RET_WIDTH, RET_WIDTH)),
                  pl.BlockSpec((GC, RET_WIDTH), rev(0)),
                  pl.BlockSpec((G, RET_QK, RET_WIDTH), lambda c: (ngr - 1 - c, 0, 0)),
                  pl.BlockSpec((GC, RET_WIDTH), rev(0))],
        out_specs=[pl.BlockSpec((GC, RET_COLS), rev(0)), _full((1, RET_WIDTH))],
        out_shape=[jax.ShapeDtypeStruct((T, RET_COLS), F32), jax.ShapeDtypeStruct((1, RET_WIDTH), F32)],
        scratch_shapes=[pltpu.VMEM((RET_QK, RET_WIDTH), F32)],
        compiler_params=_cparams(dimension_semantics=("arbitrary",)),
    )(p_ret, p_ret, p_ret, p_ret, cos, sin, qd, kd, intra_d, cd, bm, gn_g, seg128, ret, s_in_all, dy)


@jax.custom_vjp
def _chunk_sums(x, tri):
    hi, lo = _split(x)
    return _dot(tri, hi) + _dot(tri, lo)


def _chunk_sums_fwd(x, tri):
    return _chunk_sums(x, tri), tri


def _chunk_sums_bwd(tri, ct):
    hi, lo = _split(ct)
    return _dot_tn(tri, hi) + _dot_tn(tri, lo), jnp.zeros_like(tri)


_chunk_sums.defvjp(_chunk_sums_fwd, _chunk_sums_bwd)


def _chunk_tables(tm):
    t = np.arange(tm)
    same = (t[:, None] // WKV_CHUNK) == (t[None, :] // WKV_CHUNK)
    return jnp.asarray(np.stack([same & (t[None, :] <= t[:, None]), same]).astype(np.float32), dtype=BF16)


def _prep_fn(p, prev, mu, w0, a0, k_k, k_a, lora, seg, tri):
    W = RW_WIDTH
    ps = p + mu * (prev - p)
    r, kr, vr, g = ps[:, 0:W], ps[:, W:2 * W], ps[:, 2 * W:3 * W], ps[:, 3 * W:4 * W]
    z = ps[:, 4 * W:]
    lane = lax.broadcasted_iota(jnp.int32, z.shape, 1)
    z = jnp.where(lane < LORA, jnp.tanh(z), z)
    lo = _dot(z, lora, HIGHEST)
    w_log = -_softplus(-(w0 + lo[:, :W])) - 0.5
    log_decay = -jnp.exp(w_log)
    cum = _chunk_sums(log_decay, tri[0])
    total = _chunk_sums(log_decay, tri[1])
    a = jax.nn.sigmoid(a0 + lo[:, W:])
    kk = kr * k_k
    kk = kk / jnp.maximum(jnp.sqrt(_segsum(kk * kk, seg)), 1e-12)
    k = kr * (1.0 + (a - 1.0) * k_a)
    grow = jnp.exp(-cum)
    return kk * jnp.exp(cum - log_decay), (kk * a) * grow, k * grow, r * jnp.exp(cum), jnp.exp(total), vr, g, r * k


def _post_fn(o, rk, v, g, gn_g, gn_b, r_k, seg):
    mu = _segsum(o, seg) * (1.0 / RW_HEAD)
    oc = o - mu
    var = _segsum(oc * oc, seg) * (1.0 / RW_HEAD)
    on = oc * lax.rsqrt(var + RW_GN_EPS) * gn_g + gn_b
    bonus = _segsum(rk * r_k, seg) * v
    return (g * jax.nn.sigmoid(g)) * (on + bonus)


N_PAIR = (N_VEC + 1) // 2
HALF_LANES = 64


def _swap_halves(x):
    return pltpu.roll(x, HALF_LANES, 1)


def _pack_heads(vecs):
    tm = vecs[0].shape[0]
    low = lax.broadcasted_iota(jnp.int32, (tm, 128), 1) < HALF_LANES
    out = []
    for p in range(N_PAIR):
        a = vecs[2 * p]
        b = vecs[2 * p + 1] if 2 * p + 1 < len(vecs) else None
        heads = []
        for m in range(RW_WIDTH // 128):
            am = a[:, m * 128:(m + 1) * 128]
            bm = jnp.zeros_like(am) if b is None else b[:, m * 128:(m + 1) * 128]
            heads.append(jnp.where(low, am, _swap_halves(bm)))
            heads.append(jnp.where(low, _swap_halves(am), bm))
        out.append(heads)
    return out


def _unpack_heads(hm_ref):
    tm = hm_ref.shape[2]
    low = lax.broadcasted_iota(jnp.int32, (tm, 128), 1) < HALF_LANES
    vecs = []
    for p in range(N_PAIR):
        a, b = [], []
        for m in range(RW_WIDTH // 128):
            even, odd = hm_ref[p, 2 * m], hm_ref[p, 2 * m + 1]
            a.append(jnp.where(low, even, _swap_halves(odd)))
            b.append(jnp.where(low, _swap_halves(even), odd))
        vecs += [jnp.concatenate(a, axis=1), jnp.concatenate(b, axis=1)]
    return vecs[:N_VEC]


def _shift_down(p, first_row):
    row = lax.broadcasted_iota(jnp.int32, p.shape, 0)
    return jnp.where(row == 0, first_row, pltpu.roll(p, 1, 0))


def _shift_up(p, last_row):
    n = p.shape[0]
    row = lax.broadcasted_iota(jnp.int32, p.shape, 0)
    return jnp.where(row == n - 1, last_row, pltpu.roll(p, n - 1, 0))


def _row_tile(T):
    return min(T, 256)


def _prep_fwd(p_rw, bnd, mu, w0, a0, k_k, k_a, lora, seg64, tri):
    T = p_rw.shape[0]
    tm = _row_tile(T)
    W = RW_WIDTH

    def body(p_ref, bnd_ref, mu_ref, w0_ref, a0_ref, kk_ref, ka_ref, lora_ref, seg_ref, tri_ref,
             hm_ref, v_ref, g_ref, rk_ref):
        p = p_ref[...]
        prev = _shift_down(p, bnd_ref[0])
        res = _prep_fn(p, prev, mu_ref[...], w0_ref[...], a0_ref[...], kk_ref[...], ka_ref[...], lora_ref[...],
                       seg_ref[...], (tri_ref[0], tri_ref[1]))
        for pair, heads in enumerate(_pack_heads(res[:N_VEC])):
            for h, val in enumerate(heads):
                hm_ref[pair, h] = val
        v_ref[...] = res[N_VEC]
        g_ref[...] = res[N_VEC + 1]
        rk_ref[...] = res[N_VEC + 2]

    small = _full((1, W))
    row = pl.BlockSpec((tm, W), lambda i: (i, 0))
    return pl.pallas_call(
        body, name="rwkv_prep_fwd", grid=(T // tm,),
        in_specs=[pl.BlockSpec((tm, RW_COLS), lambda i: (i, 0)), pl.BlockSpec((1, 1, RW_COLS), lambda i: (i, 0, 0)),
                  _full((1, RW_COLS)), small, small, small, small, _full((2 * LORA, 2 * W)), _full((W, W)),
                  _full((2, tm, tm))],
        out_specs=[pl.BlockSpec((N_PAIR, RW_HEADS, tm, 128), lambda i: (0, 0, i, 0)), row, row, row],
        out_shape=[jax.ShapeDtypeStruct((N_PAIR, RW_HEADS, T, 128), F32)] + [jax.ShapeDtypeStruct((T, W), F32)] * 3,
        compiler_params=_cparams(dimension_semantics=("arbitrary",)),
    )(p_rw, bnd, mu, w0, a0, k_k, k_a, lora, seg64, tri)


def _prep_bwd(p_rw, bnd, mu, w0, a0, k_k, k_a, lora, seg64, tri, cts):
    T = p_rw.shape[0]
    tm = _row_tile(T)
    W = RW_WIDTH

    def body(p_ref, bnd_ref, mu_ref, w0_ref, a0_ref, kk_ref, ka_ref, lora_ref, seg_ref, tri_ref,
             dhm_ref, drk_ref, dv1_ref, dv2_ref, dg_ref,
             dp_ref, dprev_ref, dmu_ref, dw0_ref, da0_ref, dkk_p_ref, dka_ref, dlora_ref):
        accs = (dmu_ref, dw0_ref, da0_ref, dkk_p_ref, dka_ref, dlora_ref)

        @pl.when(pl.program_id(0) == 0)
        def _():
            for a_ref in accs:
                a_ref[...] = jnp.zeros_like(a_ref)

        p = p_ref[...]
        prev = _shift_down(p, bnd_ref[0])
        seg, tri = seg_ref[...], (tri_ref[0], tri_ref[1])
        _, vjp = jax.vjp(lambda *a: _prep_fn(*a, seg, tri), p, prev, mu_ref[...], w0_ref[...], a0_ref[...],
                         kk_ref[...], ka_ref[...], lora_ref[...])
        ct = (*_unpack_heads(dhm_ref), dv1_ref[...] + dv2_ref[...], dg_ref[...], drk_ref[...])
        grads = vjp(ct)
        dp_ref[...] = grads[0]
        dprev_ref[...] = grads[1]
        for a_ref, gval in zip(accs, grads[2:]):
            a_ref[...] += gval

    small = _full((1, W))
    row = pl.BlockSpec((tm, W), lambda i: (i, 0))
    return pl.pallas_call(
        body, name="rwkv_prep_bwd", grid=(T // tm,),
        in_specs=[pl.BlockSpec((tm, RW_COLS), lambda i: (i, 0)), pl.BlockSpec((1, 1, RW_COLS), lambda i: (i, 0, 0)),
                  _full((1, RW_COLS)), small, small, small, small, _full((2 * LORA, 2 * W)), _full((W, W)),
                  _full((2, tm, tm)), pl.BlockSpec((N_PAIR, RW_HEADS, tm, 128), lambda i: (0, 0, i, 0))] + [row] * 4,
        out_specs=[pl.BlockSpec((tm, RW_COLS), lambda i: (i, 0)), pl.BlockSpec((tm, RW_COLS), lambda i: (i, 0)),
                   _full((1, RW_COLS)), small, small, small, small, _full((2 * LORA, 2 * W))],
        out_shape=[jax.ShapeDtypeStruct((T, RW_COLS), F32), jax.ShapeDtypeStruct((T, RW_COLS), F32),
                   jax.ShapeDtypeStruct((1, RW_COLS), F32)] + [jax.ShapeDtypeStruct((1, W), F32)] * 4
                  + [jax.ShapeDtypeStruct((2 * LORA, 2 * W), F32)],
        compiler_params=_cparams(dimension_semantics=("arbitrary",)),
    )(p_rw, bnd, mu, w0, a0, k_k, k_a, lora, seg64, tri, *cts)


def _post_fwd(o, rk, v, g, gn_g, gn_b, r_k, seg64):
    T = o.shape[0]
    tm = _row_tile(T)
    W = RW_WIDTH

    def body(o_ref, rk_ref, v_ref, g_ref, gg_ref, gb_ref, rkp_ref, seg_ref, y_ref):
        y_ref[...] = _post_fn(o_ref[...], rk_ref[...], v_ref[...], g_ref[...], gg_ref[...], gb_ref[...],
                              rkp_ref[...], seg_ref[...]).astype(BF16)

    row = pl.BlockSpec((tm, W), lambda i: (i, 0))
    small = _full((1, W))
    return pl.pallas_call(
        body, name="rwkv_post_fwd", grid=(T // tm,),
        in_specs=[row] * 4 + [small] * 3 + [_full((W, W))],
        out_specs=row, out_shape=jax.ShapeDtypeStruct((T, W), BF16),
        compiler_params=_cparams(dimension_semantics=("arbitrary",)),
    )(o, rk, v, g, gn_g, gn_b, r_k, seg64)


def _post_bwd(o, rk, v, g, gn_g, gn_b, r_k, seg64, dy):
    T = o.shape[0]
    tm = _row_tile(T)
    W = RW_WIDTH

    def body(o_ref, rk_ref, v_ref, g_ref, gg_ref, gb_ref, rkp_ref, seg_ref, dy_ref,
             do_ref, drk_ref, dv_ref, dg_ref, dgg_ref, dgb_ref, drkp_ref):
        accs = (dgg_ref, dgb_ref, drkp_ref)

        @pl.when(pl.program_id(0) == 0)
        def _():
            for a_ref in accs:
                a_ref[...] = jnp.zeros_like(a_ref)

        seg = seg_ref[...]
        _, vjp = jax.vjp(lambda *a: _post_fn(*a, seg), o_ref[...], rk_ref[...], v_ref[...], g_ref[...],
                         gg_ref[...], gb_ref[...], rkp_ref[...])
        grads = vjp(dy_ref[...])
        for o_, gval in zip((do_ref, drk_ref, dv_ref, dg_ref), grads[:4]):
            o_[...] = gval
        for a_ref, gval in zip(accs, grads[4:]):
            a_ref[...] += gval

    row = pl.BlockSpec((tm, W), lambda i: (i, 0))
    small = _full((1, W))
    return pl.pallas_call(
        body, name="rwkv_post_bwd", grid=(T // tm,),
        in_specs=[row] * 4 + [small] * 3 + [_full((W, W)), pl.BlockSpec((tm, W), lambda i: (i, 1))],
        out_specs=[row] * 4 + [small] * 3,
        out_shape=[jax.ShapeDtypeStruct((T, W), F32)] * 4 + [jax.ShapeDtypeStruct((1, W), F32)] * 3,
        compiler_params=_cparams(dimension_semantics=("arbitrary",)),
    )(o, rk, v, g, gn_g, gn_b, r_k, seg64, dy)


def _wkv_lhs(hm_ref):
    tiles = [jnp.transpose(hm_ref[p].reshape(RW_HEADS * WKV_CHUNK, 128)) for p in range(N_PAIR)]
    hi, lo = _split(jnp.concatenate(tiles, axis=0)[:N_VEC * RW_HEAD])
    return jnp.concatenate([hi, lo], axis=1)


N_STEP_VEC = N_VEC - 1
PAD_ROWS = 16


def _wkv_fwd(cols, v, e_tab):
    T = v.shape[0]
    Tc = WKV_CHUNK
    nch = T // Tc
    J, W = RW_HEAD, RW_WIDTH
    JS = N_STEP_VEC * J

    def body(cols_ref, v_ref, e_ref, o_ref, states_ref, sa_ref, s_ref):
        @pl.when(pl.program_id(0) == 0)
        def _():
            s_ref[...] = jnp.zeros_like(s_ref)

        lhs = _wkv_lhs(cols_ref)
        st = s_ref[...]
        for t in range(Tc):
            ex = _dot(lhs[:JS], e_ref[t])
            states_ref[t] = st
            sa = -jnp.sum(st * ex[0:J], axis=0, keepdims=True)
            st = st + ex[J:2 * J] * sa + ex[2 * J:3 * J] * v_ref[t:t + 1, :]
            sa_ref[t:t + 1, :] = sa
            o_ref[t:t + 1, :] = jnp.sum(st * ex[3 * J:4 * J], axis=0, keepdims=True)
        s_ref[...] = st * _dot(lhs[JS:], e_ref[Tc - 1])

    return pl.pallas_call(
        body, name="wkv_fwd", grid=(nch,),
        in_specs=[pl.BlockSpec((N_PAIR, RW_HEADS, Tc, 128), lambda c: (0, 0, c, 0)),
                  pl.BlockSpec((Tc, W), lambda c: (c, 0)), _full((Tc, 2 * 128, W))],
        out_specs=[pl.BlockSpec((Tc, W), lambda c: (c, 0)), pl.BlockSpec((Tc, J, W), lambda c: (c, 0, 0)),
                   pl.BlockSpec((Tc, W), lambda c: (c, 0))],
        out_shape=[jax.ShapeDtypeStruct((T, W), F32), jax.ShapeDtypeStruct((T, J, W), F32),
                   jax.ShapeDtypeStruct((T, W), F32)],
        scratch_shapes=[pltpu.VMEM((J, W), F32)],
        compiler_params=_cparams(dimension_semantics=("arbitrary",)),
    )(cols, v, e_tab)


def _wkv_bwd(cols, v, do, states, sa, e_tab, r_tab):
    T = v.shape[0]
    Tc = WKV_CHUNK
    nch = T // Tc
    J, W = RW_HEAD, RW_WIDTH
    JS = N_STEP_VEC * J
    blocks = [slice(b * 128, (b + 1) * 128) for b in range(W // 128)]

    def body(cols_ref, v_ref, do_ref, states_ref, sa_ref, e_ref, r_ref, dv_ref, dhm_ref, ds_ref):
        @pl.when(pl.program_id(0) == 0)
        def _():
            ds_ref[...] = jnp.zeros_like(ds_ref)

        lhs = _wkv_lhs(cols_ref)
        last = Tc - 1
        ex = _dot(lhs, e_ref[last])
        dst, ends = [], []
        for b in blocks:
            s_end = (states_ref[last, :, b] + ex[J:2 * J, b] * sa_ref[last:last + 1, b]
                     + ex[2 * J:3 * J, b] * v_ref[last:last + 1, b])
            d_end = ds_ref[:, b]
            ends.append((d_end * s_end).astype(BF16))
            dst.append(d_end * ex[JS:, b])
        d_decay = _dot(jnp.concatenate(ends, axis=1), r_ref[last])
        acc = jnp.zeros((JS + PAD_ROWS, 128), F32)
        for t in reversed(range(Tc)):
            if t != last:
                ex = _dot(lhs[:JS], e_ref[t])
            dvs, prods = [], []
            for i, b in enumerate(blocks):
                kk_e, b_e, k_e, r_e = (ex[n * J:(n + 1) * J, b] for n in range(N_STEP_VEC))
                do_row, v_row, sa_row = do_ref[t:t + 1, b], v_ref[t:t + 1, b], sa_ref[t:t + 1, b]
                s_old = states_ref[t, :, b]
                s_new = states_ref[t + 1, :, b] if t != last else s_old + b_e * sa_row + k_e * v_row
                dsn = dst[i] + r_e * do_row
                dsa = jnp.sum(dsn * b_e, axis=0, keepdims=True)
                dvs.append(jnp.sum(dsn * k_e, axis=0, keepdims=True))
                prods.append(jnp.concatenate(
                    [s_old * (-dsa), dsn * sa_row, dsn * v_row, s_new * do_row, jnp.zeros((PAD_ROWS, 128), F32)],
                    axis=0).astype(BF16))
                dst[i] = dsn - kk_e * dsa
            dv_ref[t:t + 1, :] = jnp.concatenate(dvs, axis=1)
            acc = acc + _dot(jnp.concatenate(prods, axis=1), r_ref[t])
        for i, b in enumerate(blocks):
            ds_ref[:, b] = dst[i]
        tiles = jnp.concatenate([acc[:JS], d_decay, jnp.zeros((2 * N_PAIR * J - N_VEC * J, 128), F32)], axis=0)
        for p in range(N_PAIR):
            dhm_ref[p] = jnp.transpose(tiles[p * 128:(p + 1) * 128]).reshape(RW_HEADS, Tc, 128)

    rev2 = lambda c: (nch - 1 - c, 0)
    rev3 = lambda c: (nch - 1 - c, 0, 0)
    rev_hm = lambda c: (0, 0, nch - 1 - c, 0)
    hm_spec = pl.BlockSpec((N_PAIR, RW_HEADS, Tc, 128), rev_hm)
    return pl.pallas_call(
        body, name="wkv_bwd", grid=(nch,),
        in_specs=[hm_spec, pl.BlockSpec((Tc, W), rev2), pl.BlockSpec((Tc, W), rev2),
                  pl.BlockSpec((Tc, J, W), rev3), pl.BlockSpec((Tc, W), rev2),
                  _full((Tc, 2 * 128, W)), _full((Tc, W, 128))],
        out_specs=[pl.BlockSpec((Tc, W), rev2), hm_spec],
        out_shape=[jax.ShapeDtypeStruct((T, W), F32), jax.ShapeDtypeStruct((N_PAIR, RW_HEADS, T, 128), F32)],
        scratch_shapes=[pltpu.VMEM((J, W), F32)],
        compiler_params=_cparams(dimension_semantics=("arbitrary",)),
    )(cols, v, do, states, sa, e_tab, r_tab)


def _outproj(x, y_ret, y_rw, w_out_b, target, gf):
    T = x.shape[0]
    tm = _row_tile(T)
    W = RW_WIDTH

    def body(x_ref, yr_ref, yw_ref, w_ref, t_ref, gf_ref, loss_ref, dh_ref, dy_ref, dw_ref, dgf_ref):
        @pl.when(pl.program_id(0) == 0)
        def _():
            loss_ref[...] = jnp.zeros_like(loss_ref)
            dw_ref[...] = jnp.zeros_like(dw_ref)
            dgf_ref[...] = jnp.zeros_like(dgf_ref)

        y = jnp.concatenate([yr_ref[...], yw_ref[...]], axis=1)
        w = w_ref[...]
        h = x_ref[...] + _dot(y, w)
        rstd = lax.rsqrt(jnp.mean(h * h, axis=-1, keepdims=True) + RMS_EPS)
        hn = h * rstd
        gfv = gf_ref[...]
        err = hn * gfv - t_ref[...]
        loss_ref[...] += 0.5 * jnp.sum(jnp.mean(err * err, axis=-1))
        dout = err * (1.0 / D_MODEL)
        dgf_ref[...] += jnp.sum(dout * hn, axis=0, keepdims=True)
        dhn = dout * gfv
        dh = rstd * (dhn - hn * jnp.mean(dhn * hn, axis=-1, keepdims=True))
        dh_ref[...] = dh
        dhb = dh.astype(BF16)
        dy_ref[...] = _dot_nt(dhb, w)
        dw_ref[...] += _dot_tn(y, dhb)

    return pl.pallas_call(
        body, name="outproj_loss", grid=(T // tm,),
        in_specs=[pl.BlockSpec((tm, D_MODEL), lambda i: (i, 0)), pl.BlockSpec((tm, W), lambda i: (i, 0)),
                  pl.BlockSpec((tm, W), lambda i: (i, 0)), _full((D_MODEL, D_MODEL)),
                  pl.BlockSpec((tm, D_MODEL), lambda i: (i, 0)), _full((1, D_MODEL))],
        out_specs=[_full((1, PACK_W)), pl.BlockSpec((tm, D_MODEL), lambda i: (i, 0)),
                   pl.BlockSpec((tm, D_MODEL), lambda i: (i, 0)), _full((D_MODEL, D_MODEL)), _full((1, D_MODEL))],
        out_shape=[jax.ShapeDtypeStruct((1, PACK_W), F32), jax.ShapeDtypeStruct((T, D_MODEL), F32),
                   jax.ShapeDtypeStruct((T, D_MODEL), F32), jax.ShapeDtypeStruct((D_MODEL, D_MODEL), F32),
                   jax.ShapeDtypeStruct((1, D_MODEL), F32)],
        compiler_params=_cparams(dimension_semantics=("arbitrary",)),
    )(x, y_ret, y_rw, w_out_b, target, gf)


def _inproj_bwd_x(dp_ret, dp_rw, dprev, dbnd, w_b, x, norm_g, dh):
    T = x.shape[0]
    tm = _row_tile(T)

    def body(dpr_ref, dpw_ref, dprev_ref, dbnd_ref, w_ref, x_ref, g_ref, dh_ref, gx_ref, dg_ref, dpb_ref):
        @pl.when(pl.program_id(0) == 0)
        def _():
            dg_ref[...] = jnp.zeros_like(dg_ref)

        d_rw = dpw_ref[...] + _shift_up(dprev_ref[...], dbnd_ref[0])
        dpb = jnp.concatenate([dpr_ref[...].astype(BF16), d_rw.astype(BF16)], axis=1)
        dpb_ref[...] = dpb
        du = _dot_nt(dpb, w_ref[...])
        xf = x_ref[...]
        rstd = lax.rsqrt(jnp.mean(xf * xf, axis=-1, keepdims=True) + RMS_EPS)
        xn = xf * rstd
        dg_ref[...] += jnp.sum(du * xn, axis=0, keepdims=True)
        dxn = du * g_ref[...]
        gx_ref[...] = dh_ref[...] + rstd * (dxn - xn * jnp.mean(dxn * xn, axis=-1, keepdims=True))

    return pl.pallas_call(
        body, name="inproj_bwd_x", grid=(T // tm,),
        in_specs=[pl.BlockSpec((tm, RET_COLS), lambda i: (i, 0)), pl.BlockSpec((tm, RW_COLS), lambda i: (i, 0)),
                  pl.BlockSpec((tm, RW_COLS), lambda i: (i, 0)), pl.BlockSpec((1, 1, RW_COLS), lambda i: (i, 0, 0)),
                  _full((D_MODEL, IN_COLS)), pl.BlockSpec((tm, D_MODEL), lambda i: (i, 0)), _full((1, D_MODEL)),
                  pl.BlockSpec((tm, D_MODEL), lambda i: (i, 0))],
        out_specs=[pl.BlockSpec((tm, D_MODEL), lambda i: (i, 0)), _full((1, D_MODEL)),
                   pl.BlockSpec((tm, IN_COLS), lambda i: (i, 0))],
        out_shape=[jax.ShapeDtypeStruct((T, D_MODEL), F32), jax.ShapeDtypeStruct((1, D_MODEL), F32),
                   jax.ShapeDtypeStruct((T, IN_COLS), BF16)],
        compiler_params=_cparams(dimension_semantics=("arbitrary",)),
    )(dp_ret, dp_rw, dprev, dbnd, w_b, x, norm_g, dh)


def _inproj_bwd_w(u_t, dpb):
    T = u_t.shape[1]
    tr = 256

    def body(u_ref, d_ref, o_ref):
        o_ref[...] = _dot(u_ref[...], d_ref[...])

    return pl.pallas_call(
        body, name="inproj_bwd_w", grid=(D_MODEL // tr,),
        in_specs=[pl.BlockSpec((tr, T), lambda i: (i, 0)), _full((T, IN_COLS))],
        out_specs=pl.BlockSpec((tr, IN_COLS), lambda i: (i, 0)),
        out_shape=jax.ShapeDtypeStruct((D_MODEL, IN_COLS), F32),
        compiler_params=_cparams(dimension_semantics=("arbitrary",)),
    )(u_t, dpb)


def _tile_boundaries(a, tm, first):
    T, n = a.shape
    zero = jnp.zeros((1, n), a.dtype)
    if first:
        rows = jnp.concatenate([zero, a[tm - 1:T - 1:tm]], axis=0)
    else:
        rows = jnp.concatenate([a[tm:T:tm], zero], axis=0)
    return rows.reshape(T // tm, 1, n)


def _local_step(x, target, w_in_b, w_out_b, lora, small):
    T = x.shape[0]
    tm = _row_tile(T)
    cos, sin = _rope_tables(T)
    tabs = _ret_tables()
    seg128 = _seg_matrix(RET_WIDTH, RET_DV)
    seg64 = _seg_matrix(RW_WIDTH, RW_HEAD)
    e_tab = _wkv_expand_table()
    r_tab = _wkv_reduce_table()
    prep_w = (small["rwkv_mu"], small["w0"], small["a0"], small["k_k"], small["k_a"], lora, seg64, _chunk_tables(tm))
    post_w = (small["rwkv_gn_g"], small["rwkv_gn_b"], small["r_k"], seg64)

    p_ret, p_rw, u = _inproj_fwd(x, small["norm_g"], w_in_b)
    y_ret, ret, s_in_all = _ret_fwd(p_ret, cos, sin, tabs, small["ret_gn_g"], seg128)
    bnd = _tile_boundaries(p_rw, tm, True)
    hm, v, g, rk = _prep_fwd(p_rw, bnd, *prep_w)
    o, states, sa = _wkv_fwd(hm, v, e_tab)
    y_rw = _post_fwd(o, rk, v, g, *post_w)
    loss, dh, dy, d_w_out, d_gf = _outproj(x, y_ret, y_rw, w_out_b, target, small["final_norm_g"])

    do, d_rk, dv2, dg, d_gn_g, d_gn_b, d_r_k = _post_bwd(o, rk, v, g, *post_w, dy)
    dv1, d_hm = _wkv_bwd(hm, v, do, states, sa, e_tab, r_tab)
    dp_rw, dprev, d_mu, d_w0, d_a0, d_k_k, d_k_a, d_lora = _prep_bwd(
        p_rw, bnd, *prep_w, (d_hm, d_rk, dv1, dv2, dg))
    dp_ret, d_ret_gn = _ret_bwd(p_ret, cos, sin, tabs, small["ret_gn_g"], seg128, ret, s_in_all, dy)
    dbnd = _tile_boundaries(dprev, tm, False)
    grad_x, d_norm_g, dpb = _inproj_bwd_x(dp_ret, dp_rw, dprev, dbnd, w_in_b, x, small["norm_g"], dh)
    d_w_in = _inproj_bwd_w(jnp.transpose(u), dpb)

    d_small = {"norm_g": d_norm_g, "ret_gn_g": d_ret_gn, "rwkv_mu": d_mu, "w0": d_w0, "a0": d_a0, "k_k": d_k_k,
               "k_a": d_k_a, "r_k": d_r_k, "rwkv_gn_g": d_gn_g, "rwkv_gn_b": d_gn_b, "final_norm_g": d_gf}
    return loss, grad_x, d_w_in, d_w_out, d_lora, d_small


ANY = pl.BlockSpec(memory_space=pl.ANY)
CHIP_FLIPS = ((0, 1), (1, 0), (1, 1))
N_FLIPS = len(CHIP_FLIPS)
LORA_SHARD = RW_WIDTH // N_CHIPS
HALF_IN = D_MODEL // 2
HALF_OUT = OUT_SHARD // 2


def _position():
    return lax.axis_index("x"), lax.axis_index("y"), lax.axis_index("c")


def _flip(v, f):
    return 1 - v if f else v


def _finish(local, remote, landed):
    for cp in landed:
        cp.wait_recv()
    for cp in remote:
        cp.wait_send()
    for cp in local:
        cp.wait()


def _gather_chips(arrs):
    n = len(arrs)

    def body(*refs):
        ins, outs = refs[:n], refs[n:2 * n]
        send, recv, pass_send, pass_recv = refs[2 * n:]
        x, y, c = _position()
        s = 2 * x + y
        sibling = (x, y, 1 - c)

        def copy(src, dst, sems, k, to):
            return pltpu.make_async_remote_copy(src_ref=src, dst_ref=dst, send_sem=sems[0].at[k], recv_sem=sems[1].at[k],
                                                device_id=to, device_id_type=MESH)

        remote, landed, passed, passed_in = [], [], [], []
        for a in range(n):
            for j, (fx, fy) in enumerate(CHIP_FLIPS):
                px, py = _flip(x, fx), _flip(y, fy)
                ps = 2 * px + py
                k = a * N_FLIPS + j
                remote.append(copy(ins[a].at[c], outs[a].at[s, c], (send, recv), k, (px, py, c)))
                landed.append(copy(ins[a].at[c], outs[a].at[ps, c], (send, recv), k, (px, py, c)))
                passed.append(copy(outs[a].at[ps, c], outs[a].at[ps, c], (pass_send, pass_recv), k, sibling))
                passed_in.append(copy(outs[a].at[ps, 1 - c], outs[a].at[ps, 1 - c], (pass_send, pass_recv), k, sibling))
        for cp in remote:
            cp.start()
        for arrived, onward in zip(landed, passed):
            arrived.wait_recv()
            onward.start()
        _finish([], remote + passed, passed_in)

    sems = pltpu.SemaphoreType.DMA((n * N_FLIPS,))
    return pl.pallas_call(
        body, name="gather_weights",
        in_specs=[ANY] * n, out_specs=[ANY] * n,
        out_shape=[jax.ShapeDtypeStruct((N_CHIPS,) + a.shape, a.dtype) for a in arrs],
        scratch_shapes=[sems, sems, sems, sems],
    )(*arrs)


def _pair_exchange(g_in, g_out, g_small):
    def body(gi_ref, go_ref, gs_ref, li_ref, lo_ref, ls_ref, send, recv):
        x, y, c = _position()
        peer = (x, y, 1 - c)
        srcs = (gi_ref.at[:, pl.ds((1 - c) * HALF_IN, HALF_IN), :], go_ref.at[:, pl.ds((1 - c) * HALF_OUT, HALF_OUT), :],
                gs_ref)
        remote = [pltpu.make_async_remote_copy(src_ref=src, dst_ref=dst, send_sem=send.at[k], recv_sem=recv.at[k],
                                               device_id=peer, device_id_type=MESH)
                  for k, (src, dst) in enumerate(zip(srcs, (li_ref, lo_ref, ls_ref)))]
        for cp in remote:
            cp.start()
        _finish([], remote, remote)

    return pl.pallas_call(
        body, name="pair_exchange",
        in_specs=[ANY] * 3, out_specs=[ANY] * 3,
        out_shape=[jax.ShapeDtypeStruct((N_CHIPS, HALF_IN, IN_SHARD), F32),
                   jax.ShapeDtypeStruct((N_CHIPS, HALF_OUT, D_MODEL), F32),
                   jax.ShapeDtypeStruct(g_small.shape, F32)],
        scratch_shapes=[pltpu.SemaphoreType.DMA((3,)), pltpu.SemaphoreType.DMA((3,))],
    )(g_in, g_out, g_small)


def _pair_sum(g_in, g_out, g_small, l_in, l_out, l_small, c_arr):
    tr = HALF_IN // 2

    def body(c_ref, gi_ref, go_ref, gs_ref, li_ref, lo_ref, ls_ref, ci_ref, co_ref, cs_ref):
        ci_ref[...] = (gi_ref[...] + li_ref[...]).astype(BF16)

        @pl.when(pl.program_id(1) == 0)
        def _():
            co_ref[...] = (go_ref[...] + lo_ref[...]).astype(BF16)

        @pl.when((pl.program_id(0) == 0) & (pl.program_id(1) == 0))
        def _():
            cs_ref[...] = gs_ref[...] + ls_ref[...]

    nd = g_small.shape
    return pl.pallas_call(
        body, name="pair_sum",
        grid_spec=pltpu.PrefetchScalarGridSpec(
            num_scalar_prefetch=1, grid=(N_CHIPS, 2),
            in_specs=[pl.BlockSpec((1, tr, IN_SHARD), lambda s, i, c: (s, 2 * c[0] + i, 0)),
                      pl.BlockSpec((1, HALF_OUT, D_MODEL), lambda s, i, c: (s, c[0], 0)),
                      pl.BlockSpec(nd, lambda s, i, c: (0, 0)),
                      pl.BlockSpec((1, tr, IN_SHARD), lambda s, i, c: (s, i, 0)),
                      pl.BlockSpec((1, HALF_OUT, D_MODEL), lambda s, i, c: (s, 0, 0)),
                      pl.BlockSpec(nd, lambda s, i, c: (0, 0))],
            out_specs=[pl.BlockSpec((1, tr, IN_SHARD), lambda s, i, c: (s, i, 0)),
                       pl.BlockSpec((1, HALF_OUT, D_MODEL), lambda s, i, c: (s, 0, 0)),
                       pl.BlockSpec(nd, lambda s, i, c: (0, 0))]),
        out_shape=[jax.ShapeDtypeStruct((N_CHIPS, HALF_IN, IN_SHARD), BF16),
                   jax.ShapeDtypeStruct((N_CHIPS, HALF_OUT, D_MODEL), BF16), jax.ShapeDtypeStruct(nd, F32)],
        compiler_params=_cparams(dimension_semantics=("arbitrary", "arbitrary")),
    )(c_arr, g_in, g_out, g_small, l_in, l_out, l_small)


def _chip_exchange(c_in, c_out, c_small):
    def body(ci_ref, co_ref, cs_ref, li_ref, lo_ref, ls_ref, send, recv):
        x, y, c = _position()
        s = 2 * x + y
        remote = []
        for j, (fx, fy) in enumerate(CHIP_FLIPS):
            px, py = _flip(x, fx), _flip(y, fy)
            ps = 2 * px + py
            for a, (src, dst) in enumerate(((ci_ref.at[ps], li_ref.at[j]), (co_ref.at[ps], lo_ref.at[j]),
                                            (cs_ref, ls_ref.at[j]))):
                k = 3 * j + a
                remote.append(pltpu.make_async_remote_copy(src_ref=src, dst_ref=dst, send_sem=send.at[k],
                                                           recv_sem=recv.at[k], device_id=(px, py, c),
                                                           device_id_type=MESH))
        for cp in remote:
            cp.start()
        _finish([], remote, remote)

    return pl.pallas_call(
        body, name="chip_exchange",
        in_specs=[ANY] * 3, out_specs=[ANY] * 3,
        out_shape=[jax.ShapeDtypeStruct((N_FLIPS, HALF_IN, IN_SHARD), c_in.dtype),
                   jax.ShapeDtypeStruct((N_FLIPS, HALF_OUT, D_MODEL), c_out.dtype),
                   jax.ShapeDtypeStruct((N_FLIPS,) + c_small.shape, F32)],
        scratch_shapes=[pltpu.SemaphoreType.DMA((3 * N_FLIPS,)), pltpu.SemaphoreType.DMA((3 * N_FLIPS,))],
    )(c_in, c_out, c_small)


def _chip_sum(g_in, g_out, p_in, p_out, c_small, l_in, l_out, l_small, sc_arr):
    tr = HALF_IN // 2
    nd = c_small.shape

    def body(s_ref, gi_ref, go_ref, pi_ref, po_ref, cs_ref, li0, li1, li2, lo0, lo1, lo2, ls_ref,
             ri_ref, ro_ref, rs_ref):
        ri_ref[...] = (((gi_ref[0] + pi_ref[0]) + li0[0].astype(F32)) + li1[0].astype(F32)) + li2[0].astype(F32)

        @pl.when(pl.program_id(0) == 0)
        def _():
            ro_ref[...] = (((go_ref[0] + po_ref[0]) + lo0[0].astype(F32)) + lo1[0].astype(F32)) + lo2[0].astype(F32)
            me = s_ref[0]
            parts = (cs_ref[...], ls_ref[0], ls_ref[1], ls_ref[2])

            def of_chip(s):
                m = jnp.bitwise_xor(me, s)
                return jnp.where(m == 0, parts[0], jnp.where(m == 1, parts[1], jnp.where(m == 2, parts[2], parts[3])))

            rs_ref[...] = ((of_chip(0) + of_chip(1)) + of_chip(2)) + of_chip(3)

    def flip_in(j):
        return pl.BlockSpec((1, tr, IN_SHARD), lambda i, s: (j, i, 0))

    def flip_out(j):
        return pl.BlockSpec((1, HALF_OUT, D_MODEL), lambda i, s: (j, 0, 0))

    return pl.pallas_call(
        body, name="chip_sum",
        grid_spec=pltpu.PrefetchScalarGridSpec(
            num_scalar_prefetch=1, grid=(2,),
            in_specs=[pl.BlockSpec((1, tr, IN_SHARD), lambda i, s: (s[0], 2 * s[1] + i, 0)),
                      pl.BlockSpec((1, HALF_OUT, D_MODEL), lambda i, s: (s[0], s[1], 0)),
                      pl.BlockSpec((1, tr, IN_SHARD), lambda i, s: (s[0], i, 0)),
                      pl.BlockSpec((1, HALF_OUT, D_MODEL), lambda i, s: (s[0], 0, 0)),
                      pl.BlockSpec(nd, lambda i, s: (0, 0)),
                      flip_in(0), flip_in(1), flip_in(2), flip_out(0), flip_out(1), flip_out(2),
                      pl.BlockSpec((N_FLIPS,) + nd, lambda i, s: (0, 0, 0))],
            out_specs=[pl.BlockSpec((tr, IN_SHARD), lambda i, s: (i, 0)),
                       pl.BlockSpec((HALF_OUT, D_MODEL), lambda i, s: (0, 0)),
                       pl.BlockSpec(nd, lambda i, s: (0, 0))]),
        out_shape=[jax.ShapeDtypeStruct((HALF_IN, IN_SHARD), F32), jax.ShapeDtypeStruct((HALF_OUT, D_MODEL), F32),
                   jax.ShapeDtypeStruct(nd, F32)],
        compiler_params=_cparams(dimension_semantics=("arbitrary",)),
    )(sc_arr, g_in, g_out, p_in, p_out, c_small, l_in, l_in, l_in, l_out, l_out, l_out, l_small)


def _pair_share(r_in, r_out):
    def body(ri_ref, ro_ref, li_ref, lo_ref, send, recv):
        x, y, c = _position()
        remote = [pltpu.make_async_remote_copy(src_ref=src, dst_ref=dst, send_sem=send.at[k], recv_sem=recv.at[k],
                                               device_id=(x, y, 1 - c), device_id_type=MESH)
                  for k, (src, dst) in enumerate(((ri_ref, li_ref), (ro_ref, lo_ref)))]
        for cp in remote:
            cp.start()
        _finish([], remote, remote)

    return pl.pallas_call(
        body, name="pair_share",
        in_specs=[ANY] * 2, out_specs=[ANY] * 2,
        out_shape=[jax.ShapeDtypeStruct(r_in.shape, F32), jax.ShapeDtypeStruct(r_out.shape, F32)],
        scratch_shapes=[pltpu.SemaphoreType.DMA((2,)), pltpu.SemaphoreType.DMA((2,))],
    )(r_in, r_out)


def _adam_update(w, g, m, v):
    mn = ADAM_B1 * m + (1.0 - ADAM_B1) * g
    vn = ADAM_B2 * v + (1.0 - ADAM_B2) * jnp.square(g)
    m_hat = mn / (1.0 - ADAM_B1 ** ADAM_STEP)
    v_hat = vn / (1.0 - ADAM_B2 ** ADAM_STEP)
    return -ADAM_LR * (m_hat / (jnp.sqrt(v_hat) + ADAM_EPS) + ADAM_WD * w), mn, vn


def _adamw(name, w, g_mine, g_theirs, m, v, core_arr, tr):
    rows, cols = w.shape
    per_half = rows // 2 // tr

    def body(c_ref, w_ref, gm_ref, gt_ref, m_ref, v_ref, g_ref, d_ref, nm_ref, nv_ref):
        mine = (pl.program_id(0) // per_half) == c_ref[0]
        g = jnp.where(mine, gm_ref[...], gt_ref[...])
        d, mn, vn = _adam_update(w_ref[...], g, m_ref[...], v_ref[...])
        g_ref[...] = g
        d_ref[...] = d
        nm_ref[...] = mn
        nv_ref[...] = vn

    spec = pl.BlockSpec((tr, cols), lambda i, c: (i, 0))
    half = pl.BlockSpec((tr, cols), lambda i, c: (i % per_half, 0))
    return pl.pallas_call(
        body, name=name,
        grid_spec=pltpu.PrefetchScalarGridSpec(
            num_scalar_prefetch=1, grid=(rows // tr,),
            in_specs=[spec, half, half, spec, spec], out_specs=[spec] * 4),
        out_shape=[jax.ShapeDtypeStruct((rows, cols), F32)] * 4,
        compiler_params=_cparams(dimension_semantics=("arbitrary",)),
    )(core_arr, w, g_mine, g_theirs, m, v)


def _row_pieces(n):
    return [(k, k * PACK_W, min(PACK_W, n - k * PACK_W)) for k in range(-(-n // PACK_W))]


def _pack_small(d_small, loss, d_lora):
    ns = len(SMALL_NAMES)

    def body(*refs):
        small_refs, (loss_ref, lora_ref, out_ref) = refs[:ns], refs[ns:]
        out_ref[...] = jnp.zeros_like(out_ref)
        out_ref[PACK_LORA_W:PACK_LORA_W + LORA, :] = lora_ref[:LORA, :RW_WIDTH]
        out_ref[PACK_LORA_A:PACK_LORA_A + LORA, :] = lora_ref[LORA:, RW_WIDTH:]
        for name, n, ref in zip(SMALL_NAMES, SMALL_SIZES, small_refs):
            for k, at, w in _row_pieces(n):
                out_ref[PACK_AT[name] + k:PACK_AT[name] + k + 1, 0:w] = ref[:, at:at + w]
        out_ref[PACK_LOSS:PACK_LOSS + 1, :] = loss_ref[...]

    return pl.pallas_call(body, name="pack_small", out_shape=jax.ShapeDtypeStruct((PACK_ROWS, PACK_W), F32),
                          compiler_params=_cparams())(*d_small, loss, d_lora)


def _adamw_small(tot, chip_arr, ws, ms, vs):
    ns = len(SMALL_NAMES)
    n_par = ns + 2

    def body(s_ref, tot_ref, glw_ref, gla_ref, *refs):
        w_refs, m_refs, v_refs = refs[:n_par], refs[n_par:2 * n_par], refs[2 * n_par:3 * n_par]
        outs = refs[3 * n_par:]
        g_refs, d_refs, nm_refs, nv_refs = (outs[i * n_par:(i + 1) * n_par] for i in range(4))
        grads = [jnp.concatenate([tot_ref[PACK_AT[name] + k:PACK_AT[name] + k + 1, 0:w] for k, _, w in _row_pieces(n)],
                                 axis=1) for name, n in zip(SMALL_NAMES, SMALL_SIZES)]
        grads += [glw_ref[...], gla_ref[...]]
        for i, g in enumerate(grads):
            d, mn, vn = _adam_update(w_refs[i][...], g, m_refs[i][...], v_refs[i][...])
            g_refs[i][...] = g
            d_refs[i][...] = d
            nm_refs[i][...] = mn
            nv_refs[i][...] = vn

    def whole(a):
        nd = a.ndim
        return pl.BlockSpec(a.shape, lambda i, s: (0,) * nd)

    shard = (LORA, LORA_SHARD)
    par_specs = [whole(a) for a in ws]
    res = pl.pallas_call(
        body, name="adamw_small",
        grid_spec=pltpu.PrefetchScalarGridSpec(
            num_scalar_prefetch=1, grid=(1,),
            in_specs=[whole(tot), pl.BlockSpec(shard, lambda i, s: (PACK_LORA_W // LORA, s[0])),
                      pl.BlockSpec(shard, lambda i, s: (PACK_LORA_A // LORA, s[0]))] + par_specs * 3,
            out_specs=par_specs * 4),
        out_shape=[jax.ShapeDtypeStruct(a.shape, F32) for a in ws] * 4,
        compiler_params=_cparams(dimension_semantics=("arbitrary",)),
    )(chip_arr, tot, tot, tot, *ws, *ms, *vs)
    return [res[i * n_par:(i + 1) * n_par] for i in range(4)]


def kernel(x, norm_g, w_in, ret_gn_g, rwkv_mu, w_lora_up, w0, a_lora_up, a0, k_k, k_a, r_k, rwkv_gn_g, rwkv_gn_b, w_out, final_norm_g, loss_target, m_norm_g, m_w_in, m_ret_gn_g, m_rwkv_mu, m_w_lora_up, m_w0, m_a_lora_up, m_a0, m_k_k, m_k_a, m_r_k, m_rwkv_gn_g, m_rwkv_gn_b, m_w_out, m_final_norm_g, v_norm_g, v_w_in, v_ret_gn_g, v_rwkv_mu, v_w_lora_up, v_w0, v_a_lora_up, v_a0, v_k_k, v_k_a, v_r_k, v_rwkv_gn_g, v_rwkv_gn_b, v_w_out, v_final_norm_g):
    W = RW_WIDTH
    params = dict(norm_g=norm_g, ret_gn_g=ret_gn_g, rwkv_mu=rwkv_mu, w0=w0, a0=a0, k_k=k_k, k_a=k_a, r_k=r_k,
                  rwkv_gn_g=rwkv_gn_g, rwkv_gn_b=rwkv_gn_b, final_norm_g=final_norm_g)
    moments_m = dict(norm_g=m_norm_g, ret_gn_g=m_ret_gn_g, rwkv_mu=m_rwkv_mu, w0=m_w0, a0=m_a0, k_k=m_k_k, k_a=m_k_a,
                     r_k=m_r_k, rwkv_gn_g=m_rwkv_gn_g, rwkv_gn_b=m_rwkv_gn_b, final_norm_g=m_final_norm_g)
    moments_v = dict(norm_g=v_norm_g, ret_gn_g=v_ret_gn_g, rwkv_mu=v_rwkv_mu, w0=v_w0, a0=v_a0, k_k=v_k_k, k_a=v_k_a,
                     r_k=v_r_k, rwkv_gn_g=v_rwkv_gn_g, rwkv_gn_b=v_rwkv_gn_b, final_norm_g=v_final_norm_g)
    xi, yi, ci = _position()
    chip = (2 * xi + yi).astype(jnp.int32)

    def halves(a):
        return a.reshape(2, a.shape[0] // 2, a.shape[1])

    mine = [halves(w_in[0].astype(BF16)), halves(w_out[0].astype(BF16)), halves(w_lora_up[0]), halves(a_lora_up[0])]
    g_in, g_out, g_lw, g_la = [lax.dynamic_update_slice(g, own[None], (chip, 0, 0, 0))
                               for g, own in zip(_gather_chips(mine), mine)]
    w_in_b = jnp.transpose(g_in.reshape(N_CHIPS, D_MODEL, IN_SHARD), (1, 0, 2)).reshape(D_MODEL, IN_COLS)
    w_out_b = g_out.reshape(D_MODEL, D_MODEL)
    lw = jnp.transpose(g_lw.reshape(N_CHIPS, LORA, LORA_SHARD), (1, 0, 2)).reshape(LORA, W)
    la = jnp.transpose(g_la.reshape(N_CHIPS, LORA, LORA_SHARD), (1, 0, 2)).reshape(LORA, W)
    zero = jnp.zeros((LORA, W), F32)
    lora = jnp.concatenate([jnp.concatenate([lw, zero], axis=1), jnp.concatenate([zero, la], axis=1)], axis=0)
    small = {n: params[n].reshape(1, -1) for n in SMALL_NAMES}

    loss, grad_x, d_w_in, d_w_out, d_lora, d_small = _local_step(x[0], loss_target[0], w_in_b, w_out_b, lora, small)

    core = ci.astype(jnp.int32)
    gi = jnp.transpose(d_w_in.reshape(D_MODEL, N_CHIPS, IN_SHARD), (1, 0, 2))
    go = d_w_out.reshape(N_CHIPS, OUT_SHARD, D_MODEL)
    gs = _pack_small([d_small[n] for n in SMALL_NAMES], loss, d_lora)
    p_in, p_out, p_small = _pair_exchange(gi, go, gs)
    c_in, c_out, c_small = _pair_sum(gi, go, gs, p_in, p_out, p_small, core.reshape(1))
    l_in, l_out, l_small = _chip_exchange(c_in, c_out, c_small)
    r_in, r_out, tot = _chip_sum(gi, go, p_in, p_out, c_small, l_in, l_out, l_small, jnp.stack([chip, core]))
    t_in, t_out = _pair_share(r_in, r_out)

    grad_w_in, d_in, nm_in, nv_in = _adamw("adamw_w_in", w_in[0], r_in, t_in, m_w_in[0], v_w_in[0], core.reshape(1), 256)
    grad_w_out, d_out, nm_out, nv_out = _adamw("adamw_w_out", w_out[0], r_out, t_out, m_w_out[0], v_w_out[0],
                                               core.reshape(1), HALF_OUT)
    par_names = SMALL_NAMES + ("w_lora_up", "a_lora_up")

    def operands(tree, lw_, la_):
        return [tree[n].reshape(1, -1) for n in SMALL_NAMES] + [lw_[0], la_[0]]

    res = _adamw_small(tot, chip.reshape(1), operands(params, w_lora_up, a_lora_up),
                       operands(moments_m, m_w_lora_up, m_a_lora_up), operands(moments_v, v_w_lora_up, v_a_lora_up))

    names = ("norm_g", "w_in", "ret_gn_g", "rwkv_mu", "w_lora_up", "w0", "a_lora_up", "a0", "k_k", "k_a", "r_k",
             "rwkv_gn_g", "rwkv_gn_b", "w_out", "final_norm_g")
    shapes = dict(w_in=w_in.shape, w_out=w_out.shape, w_lora_up=w_lora_up.shape, a_lora_up=a_lora_up.shape,
                  **{n: params[n].shape for n in SMALL_NAMES})

    def leaves(pars, big_in, big_out):
        tree = dict(zip(par_names, pars), w_in=big_in, w_out=big_out)
        return [tree[n].reshape(shapes[n]) for n in names]

    grads = leaves(res[0], grad_w_in, grad_w_out)
    deltas = leaves(res[1], d_in, d_out)
    new_m = leaves(res[2], nm_in, nm_out)
    new_v = leaves(res[3], nv_in, nv_out)
    return (tot[PACK_LOSS, 0], grad_x.reshape(x.shape), *grads, *deltas, *new_m, *new_v)
```

```python
import functools

import numpy as np
import jax
import jax.numpy as jnp
from jax import lax
from jax.experimental import pallas as pl
from jax.experimental.pallas import tpu as pltpu

F32 = jnp.float32
BF16 = jnp.bfloat16
X3 = "bf16x3"
MESH = pl.DeviceIdType.MESH

D_MODEL = 1024
N_CHIPS = 4
RET_HEADS = 4
RET_DK = 64
RET_DV = 128
RET_QK = RET_HEADS * RET_DK
RET_WIDTH = RET_HEADS * RET_DV
RET_COLS = 2 * RET_QK + 2 * RET_WIDTH
RET_CHUNK = 64
RET_GROUP = 4
RW_WIDTH = 512
RW_HEAD = 64
RW_HEADS = 8
LORA = 64
RW_COLS = 4 * RW_WIDTH + 2 * LORA
IN_COLS = RET_COLS + RW_COLS
IN_SHARD = IN_COLS // N_CHIPS
OUT_SHARD = D_MODEL // N_CHIPS
ROPE_BASE = 10000.0
RMS_EPS = 1e-6
RET_GN_EPS = 1e-5
RW_GN_EPS = 64e-5
WKV_CHUNK = 16
N_VEC = 5

ADAM_LR = 0.001
ADAM_B1 = 0.9
ADAM_B2 = 0.999
ADAM_EPS = 1e-08
ADAM_WD = 0.01
ADAM_STEP = 10

VMEM_LIMIT = 56 * 1024 * 1024

PACK_W = 512
SMALL_NAMES = ("norm_g", "ret_gn_g", "rwkv_mu", "w0", "a0", "k_k", "k_a", "r_k", "rwkv_gn_g", "rwkv_gn_b",
               "final_norm_g")
SMALL_SIZES = (1024, 512, 2176, 512, 512, 512, 512, 512, 512, 512, 1024)
PACK_LORA_W = 0
PACK_LORA_A = LORA
PACK_SMALL = 2 * LORA


def _pack_layout():
    rows, at = {}, PACK_SMALL
    for name, n in zip(SMALL_NAMES, SMALL_SIZES):
        rows[name] = at
        at += -(-n // PACK_W)
    return rows, at


PACK_AT, PACK_LOSS = _pack_layout()
PACK_ROWS = -(-(PACK_LOSS + 1) // 8) * 8


def _cparams(**kw):
    return pltpu.CompilerParams(vmem_limit_bytes=VMEM_LIMIT, **kw)


def _split(x):
    hi = x.astype(BF16)
    lo = (x - hi.astype(F32)).astype(BF16)
    return hi, lo


def _dot_dims(a, b, dims, precision):
    if precision != X3:
        return lax.dot_general(a, b, dims, preferred_element_type=F32)
    (ah, al), (bh, bl) = _split(a), _split(b)
    dot = lambda u, w: lax.dot_general(u, w, dims, preferred_element_type=F32)
    return dot(ah, bh) + dot(ah, bl) + dot(al, bh)


def _dot(a, b, precision=None):
    return _dot_dims(a, b, (((1,), (0,)), ((), ())), precision)


def _dot_nt(a, b, precision=None):
    return _dot_dims(a, b, (((1,), (1,)), ((), ())), precision)


def _dot_tn(a, b, precision=None):
    return _dot_dims(a, b, (((0,), (0,)), ((), ())), precision)


@jax.custom_vjp
def _segsum(x, seg):
    hi, lo = _split(x)
    return _dot(hi, seg) + _dot(lo, seg)


def _segsum_fwd(x, seg):
    return _segsum(x, seg), seg


def _segsum_bwd(seg, ct):
    return _segsum(ct, seg), jnp.zeros_like(seg)


_segsum.defvjp(_segsum_fwd, _segsum_bwd)


def _softplus(z):
    return jnp.maximum(z, 0.0) + jnp.log(1.0 + jnp.exp(-jnp.abs(z)))


def _full(shape):
    nd = len(shape)
    return pl.BlockSpec(shape, lambda *_: (0,) * nd)


def _rope_tables(T):
    half = RET_DK // 2
    expo = -jnp.arange(half, dtype=F32) / jnp.float32(half)
    freqs = jnp.exp(expo * jnp.float32(np.log(ROPE_BASE)))
    ang = jnp.arange(T, dtype=jnp.int32).astype(F32)[:, None] * freqs[None, :]
    cos = jnp.tile(jnp.cos(ang), (1, 2 * RET_HEADS))
    sin = jnp.tile(jnp.sin(ang), (1, 2 * RET_HEADS))
    return cos, sin


def _ret_tables():
    H, C = RET_HEADS, RET_CHUNK
    hidx = jnp.arange(H, dtype=F32)
    lg = jnp.log(1.0 - jnp.exp2(-5.0 - hidx))
    idx = jnp.arange(C, dtype=F32)
    intra = jnp.exp(lg[:, None, None] * jnp.abs(idx[:, None] - idx[None, :]))
    q_dec = jnp.transpose(jnp.exp(lg[:, None] * (idx[None, :] + 1.0)))
    k_dec = jnp.transpose(jnp.exp(lg[:, None] * (C - 1.0 - idx[None, :])))
    chunk_dec = jnp.exp(lg * C)
    qd = jnp.repeat(q_dec, RET_DK, axis=1)
    kd = jnp.repeat(k_dec, RET_DK, axis=1)
    row_h = np.arange(RET_QK) // RET_DK
    col_h = np.arange(RET_WIDTH) // RET_DV
    bm = jnp.asarray((row_h[:, None] == col_h[None, :]).astype(np.float32))
    cd = bm * jnp.repeat(chunk_dec, RET_DK)[:, None]
    return intra, qd, kd, cd, bm


def _seg_matrix(width, head):
    h = np.arange(width) // head
    return jnp.asarray((h[:, None] == h[None, :]).astype(np.float32), dtype=BF16)


def _wkv_expand_table():
    Tc = WKV_CHUNK
    k = np.arange(2 * RW_HEADS * Tc)
    kh, kt = (k % (RW_HEADS * Tc)) // Tc, k % Tc
    nh = np.arange(RW_WIDTH) // RW_HEAD
    e = (kh[None, :, None] == nh[None, None, :]) & (kt[None, :, None] == np.arange(Tc)[:, None, None])
    return jnp.asarray(e.astype(np.float32), dtype=BF16)


def _wkv_reduce_table():
    Tc = WKV_CHUNK
    kh = np.arange(RW_WIDTH) // RW_HEAD
    n = np.arange(RW_HEADS * Tc)
    nh, nt = n // Tc, n % Tc
    r = (kh[None, :, None] == nh[None, None, :]) & (nt[None, None, :] == np.arange(Tc)[:, None, None])
    return jnp.asarray(r.astype(np.float32), dtype=BF16)


def _inproj_fwd(x, norm_g, w_b):
    T = x.shape[0]
    tm = min(T, 256)

    def body(x_ref, g_ref, w_ref, pret_ref, prw_ref, u_ref):
        xf = x_ref[...]
        rstd = lax.rsqrt(jnp.mean(xf * xf, axis=-1, keepdims=True) + RMS_EPS)
        ub = ((xf * rstd) * g_ref[...]).astype(BF16)
        u_ref[...] = ub
        pret_ref[...] = _dot(ub, w_ref[:, :RET_COLS])
        prw_ref[...] = _dot(ub, w_ref[:, RET_COLS:])

    return pl.pallas_call(
        body, name="inproj_fwd", grid=(T // tm,),
        in_specs=[pl.BlockSpec((tm, D_MODEL), lambda i: (i, 0)), _full((1, D_MODEL)), _full((D_MODEL, IN_COLS))],
        out_specs=[pl.BlockSpec((tm, RET_COLS), lambda i: (i, 0)), pl.BlockSpec((tm, RW_COLS), lambda i: (i, 0)),
                   pl.BlockSpec((tm, D_MODEL), lambda i: (i, 0))],
        out_shape=[jax.ShapeDtypeStruct((T, RET_COLS), F32), jax.ShapeDtypeStruct((T, RW_COLS), F32),
                   jax.ShapeDtypeStruct((T, D_MODEL), BF16)],
        compiler_params=_cparams(dimension_semantics=("arbitrary",)),
    )(x, norm_g, w_b)


def _rot_half(x):
    n = x.shape[1]
    lane = lax.broadcasted_iota(jnp.int32, x.shape, 1)
    first = (lane % RET_DK) < (RET_DK // 2)
    return jnp.where(first, -pltpu.roll(x, n - RET_DK // 2, 1), pltpu.roll(x, RET_DK // 2, 1))


def _rope(x, cos, sin):
    return x * cos + _rot_half(x) * sin


def _rope_bwd(d, cos, sin):
    return d * cos - _rot_half(d * sin)


def _ret_post(ret, g, gn_g, seg):
    mu = _segsum(ret, seg) * (1.0 / RET_DV)
    xc = ret - mu
    var = _segsum(xc * xc, seg) * (1.0 / RET_DV)
    n = xc * lax.rsqrt(var + RET_GN_EPS)
    return (g * jax.nn.sigmoid(g)) * (n * gn_g)


def _ret_scores(qt, kt, d_ref, h):
    lane = lax.broadcasted_iota(jnp.int32, qt.shape, 1)
    qh = jnp.where(lane // RET_DK == h, qt, 0.0)
    return qh, _dot_nt(qh, kt, X3) * d_ref[h]


def _ret_group(nch):
    return min(RET_GROUP, nch)


def _ret_fwd(p_ret, cos, sin, tabs, gn_g, seg128):
    T = p_ret.shape[0]
    C = RET_CHUNK
    nch = T // C
    G = _ret_group(nch)
    intra_d, qd, kd, cd, bm = tabs

    def body(q_ref, k_ref, v_ref, g_ref, cos_ref, sin_ref, qd_ref, kd_ref, d_ref, cd_ref, bm_ref, gn_ref, seg_ref,
             y_ref, ret_ref, sin_out_ref, s_ref):
        @pl.when(pl.program_id(0) == 0)
        def _():
            s_ref[...] = jnp.zeros_like(s_ref)

        s_in = s_ref[...]
        for i in range(G):
            rows = slice(i * C, (i + 1) * C)
            cosv, sinv = cos_ref[rows, :], sin_ref[rows, :]
            qt = _rope(q_ref[rows, :], cosv, sinv)
            kt = _rope(k_ref[rows, :], cosv, sinv) * (RET_DK ** -0.5)
            v = v_ref[rows, :]
            sin_out_ref[i] = s_in
            inter = _dot(qt * qd_ref[...], s_in, X3)
            intra = []
            for h in range(RET_HEADS):
                _, a = _ret_scores(qt, kt, d_ref, h)
                intra.append(_dot(a, v[:, h * RET_DV:(h + 1) * RET_DV], X3))
            ret_ref[rows, :] = jnp.concatenate(intra, axis=1) + inter
            kv = _dot_tn(kt * kd_ref[...], v, X3)
            s_in = s_in * cd_ref[...] + kv * bm_ref[...]
        s_ref[...] = s_in
        y_ref[...] = _ret_post(ret_ref[...], g_ref[...], gn_ref[...], seg_ref[...]).astype(BF16)

    GC = G * C
    return pl.pallas_call(
        body, name="ret_fwd", grid=(nch // G,),
        in_specs=[pl.BlockSpec((GC, RET_QK), lambda c: (c, 0)), pl.BlockSpec((GC, RET_QK), lambda c: (c, 1)),
                  pl.BlockSpec((GC, RET_WIDTH), lambda c: (c, 1)), pl.BlockSpec((GC, RET_WIDTH), lambda c: (c, 2)),
                  pl.BlockSpec((GC, RET_QK), lambda c: (c, 0)), pl.BlockSpec((GC, RET_QK), lambda c: (c, 0)),
                  _full((C, RET_QK)), _full((C, RET_QK)), _full((RET_HEADS, C, C)),
                  _full((RET_QK, RET_WIDTH)), _full((RET_QK, RET_WIDTH)), _full((1, RET_WIDTH)),
                  _full((RET_WIDTH, RET_WIDTH))],
        out_specs=[pl.BlockSpec((GC, RET_WIDTH), lambda c: (c, 0)), pl.BlockSpec((GC, RET_WIDTH), lambda c: (c, 0)),
                   pl.BlockSpec((G, RET_QK, RET_WIDTH), lambda c: (c, 0, 0))],
        out_shape=[jax.ShapeDtypeStruct((T, RET_WIDTH), BF16), jax.ShapeDtypeStruct((T, RET_WIDTH), F32),
                   jax.ShapeDtypeStruct((nch, RET_QK, RET_WIDTH), F32)],
        scratch_shapes=[pltpu.VMEM((RET_QK, RET_WIDTH), F32)],
        compiler_params=_cparams(dimension_semantics=("arbitrary",)),
    )(p_ret, p_ret, p_ret, p_ret, cos, sin, qd, kd, intra_d, cd, bm, gn_g, seg128)


def _ret_bwd(p_ret, cos, sin, tabs, gn_g, seg128, ret, s_in_all, dy):
    T = p_ret.shape[0]
    C = RET_CHUNK
    nch = T // C
    G = _ret_group(nch)
    ngr = nch // G
    intra_d, qd, kd, cd, bm = tabs

    def rev(j):
        return lambda c: (ngr - 1 - c, j)

    def body(q_ref, k_ref, v_ref, g_ref, cos_ref, sin_ref, qd_ref, kd_ref, d_ref, cd_ref, bm_ref, gn_ref, seg_ref,
             ret_ref, sin_ref_, dy_ref, dp_ref, dgn_ref, ds_ref):
        @pl.when(pl.program_id(0) == 0)
        def _():
            ds_ref[...] = jnp.zeros_like(ds_ref)
            dgn_ref[...] = jnp.zeros_like(dgn_ref)

        seg = seg_ref[...]
        _, post_vjp = jax.vjp(lambda r_, g_, gn_: _ret_post(r_, g_, gn_, seg), ret_ref[...], g_ref[...], gn_ref[...])
        dret_all, dg_all, dgn = post_vjp(dy_ref[...])
        dgn_ref[...] += dgn
        dp_ref[:, 2 * RET_QK + RET_WIDTH:] = dg_all

        qdv, kdv = qd_ref[...], kd_ref[...]
        ds_out = ds_ref[...]
        for i in reversed(range(G)):
            rows = slice(i * C, (i + 1) * C)
            cosv, sinv = cos_ref[rows, :], sin_ref[rows, :]
            qt = _rope(q_ref[rows, :], cosv, sinv)
            kt = _rope(k_ref[rows, :], cosv, sinv) * (RET_DK ** -0.5)
            v = v_ref[rows, :]
            s_in = sin_ref_[i]
            dret = dret_all[rows, :]
            dqt = qdv * _dot_nt(dret, s_in, X3)
            dkt = kdv * _dot_nt(v, ds_out, X3)
            dv_all = _dot(kt * kdv, ds_out, X3)
            dvs = []
            for h in range(RET_HEADS):
                sl = slice(h * RET_DV, (h + 1) * RET_DV)
                qh, a = _ret_scores(qt, kt, d_ref, h)
                lane = lax.broadcasted_iota(jnp.int32, kt.shape, 1)
                kh = jnp.where(lane // RET_DK == h, kt, 0.0)
                da = _dot_nt(dret[:, sl], v[:, sl], X3) * d_ref[h]
                dvs.append(_dot_tn(a, dret[:, sl], X3))
                dqt = dqt + _dot(da, kh, X3)
                dkt = dkt + _dot_tn(da, qh, X3)
            ds_out = ds_out * cd_ref[...] + _dot_tn(qt * qdv, dret, X3) * bm_ref[...]
            dp_ref[rows, :RET_QK] = _rope_bwd(dqt, cosv, sinv)
            dp_ref[rows, RET_QK:2 * RET_QK] = _rope_bwd(dkt * (RET_DK ** -0.5), cosv, sinv)
            dp_ref[rows, 2 * RET_QK:2 * RET_QK + RET_WIDTH] = dv_all + jnp.concatenate(dvs, axis=1)
        ds_ref[...] = ds_out

    GC = G * C
    return pl.pallas_call(
        body, name="ret_bwd", grid=(ngr,),
        in_specs=[pl.BlockSpec((GC, RET_QK), rev(0)), pl.BlockSpec((GC, RET_QK), rev(1)),
                  pl.BlockSpec((GC, RET_WIDTH), rev(1)), pl.BlockSpec((GC, RET_WIDTH), rev(2)),
                  pl.BlockSpec((GC, RET_QK), rev(0)), pl.BlockSpec((GC, RET_QK), rev(0)),
                  _full((C, RET_QK)), _full((C, RET_QK)), _full((RET_HEADS, C, C)),
                  _full((RET_QK, RET_WIDTH)), _full((RET_QK, RET_WIDTH)), _full((1, RET_WIDTH)),
                  _full((RET_WIDTH, RET_WIDTH)),
                  pl.BlockSpec((GC, RET_WIDTH), rev(0)),
                  pl.BlockSpec((G, RET_QK, RET_WIDTH), lambda c: (ngr - 1 - c, 0, 0)),
                  pl.BlockSpec((GC, RET_WIDTH), rev(0))],
        out_specs=[pl.BlockSpec((GC, RET_COLS), rev(0)), _full((1, RET_WIDTH))],
        out_shape=[jax.ShapeDtypeStruct((T, RET_COLS), F32), jax.ShapeDtypeStruct((1, RET_WIDTH), F32)],
        scratch_shapes=[pltpu.VMEM((RET_QK, RET_WIDTH), F32)],
        compiler_params=_cparams(dimension_semantics=("arbitrary",)),
    )(p_ret, p_ret, p_ret, p_ret, cos, sin, qd, kd, intra_d, cd, bm, gn_g, seg128, ret, s_in_all, dy)


@jax.custom_vjp
def _chunk_sums(x, tri):
    hi, lo = _split(x)
    return _dot(tri, hi) + _dot(tri, lo)


def _chunk_sums_fwd(x, tri):
    return _chunk_sums(x, tri), tri


def _chunk_sums_bwd(tri, ct):
    hi, lo = _split(ct)
    return _dot_tn(tri, hi) + _dot_tn(tri, lo), jnp.zeros_like(tri)


_chunk_sums.defvjp(_chunk_sums_fwd, _chunk_sums_bwd)


@jax.custom_vjp
def _lora_dot(z, lora):
    return _dot(z, lora, X3)


def _lora_dot_fwd(z, lora):
    return _lora_dot(z, lora), (z, lora)


def _lora_dot_bwd(res, ct):
    z, lora = res
    return _dot_nt(ct, lora, X3), _dot_tn(z, ct, X3)


_lora_dot.defvjp(_lora_dot_fwd, _lora_dot_bwd)


def _chunk_tables(tm):
    t = np.arange(tm)
    same = (t[:, None] // WKV_CHUNK) == (t[None, :] // WKV_CHUNK)
    return jnp.asarray(np.stack([same & (t[None, :] <= t[:, None]), same]).astype(np.float32), dtype=BF16)


def _prep_fn(p, prev, mu, w0, a0, k_k, k_a, lora, seg, tri):
    W = RW_WIDTH
    ps = p + mu * (prev - p)
    r, kr, vr, g = ps[:, 0:W], ps[:, W:2 * W], ps[:, 2 * W:3 * W], ps[:, 3 * W:4 * W]
    z = ps[:, 4 * W:]
    lane = lax.broadcasted_iota(jnp.int32, z.shape, 1)
    z = jnp.where(lane < LORA, jnp.tanh(z), z)
    lo = _lora_dot(z, lora)
    w_log = -_softplus(-(w0 + lo[:, :W])) - 0.5
    log_decay = -jnp.exp(w_log)
    cum = _chunk_sums(log_decay, tri[0])
    total = _chunk_sums(log_decay, tri[1])
    a = jax.nn.sigmoid(a0 + lo[:, W:])
    kk = kr * k_k
    kk = kk / jnp.maximum(jnp.sqrt(_segsum(kk * kk, seg)), 1e-12)
    k = kr * (1.0 + (a - 1.0) * k_a)
    grow = jnp.exp(-cum)
    return kk * jnp.exp(cum - log_decay), (kk * a) * grow, k * grow, r * jnp.exp(cum), jnp.exp(total), vr, g, r * k


def _post_fn(o, rk, v, g, gn_g, gn_b, r_k, seg):
    mu = _segsum(o, seg) * (1.0 / RW_HEAD)
    oc = o - mu
    var = _segsum(oc * oc, seg) * (1.0 / RW_HEAD)
    on = oc * lax.rsqrt(var + RW_GN_EPS) * gn_g + gn_b
    bonus = _segsum(rk * r_k, seg) * v
    return (g * jax.nn.sigmoid(g)) * (on + bonus)


N_PAIR = (N_VEC + 1) // 2
HALF_LANES = 64


def _swap_halves(x):
    return pltpu.roll(x, HALF_LANES, 1)


def _pack_heads(vecs):
    tm = vecs[0].shape[0]
    low = lax.broadcasted_iota(jnp.int32, (tm, 128), 1) < HALF_LANES
    out = []
    for p in range(N_PAIR):
        a = vecs[2 * p]
        b = vecs[2 * p + 1] if 2 * p + 1 < len(vecs) else None
        heads = []
        for m in range(RW_WIDTH // 128):
            am = a[:, m * 128:(m + 1) * 128]
            bm = jnp.zeros_like(am) if b is None else b[:, m * 128:(m + 1) * 128]
            heads.append(jnp.where(low, am, _swap_halves(bm)))
            heads.append(jnp.where(low, _swap_halves(am), bm))
        out.append(heads)
    return out


def _unpack_heads(hm_ref):
    tm = hm_ref.shape[2]
    low = lax.broadcasted_iota(jnp.int32, (tm, 128), 1) < HALF_LANES
    vecs = []
    for p in range(N_PAIR):
        a, b = [], []
        for m in range(RW_WIDTH // 128):
            even, odd = hm_ref[p, 2 * m], hm_ref[p, 2 * m + 1]
            a.append(jnp.where(low, even, _swap_halves(odd)))
            b.append(jnp.where(low, _swap_halves(even), odd))
        vecs += [jnp.concatenate(a, axis=1), jnp.concatenate(b, axis=1)]
    return vecs[:N_VEC]


def _shift_down(p, first_row):
    row = lax.broadcasted_iota(jnp.int32, p.shape, 0)
    return jnp.where(row == 0, first_row, pltpu.roll(p, 1, 0))


def _shift_up(p, last_row):
    n = p.shape[0]
    row = lax.broadcasted_iota(jnp.int32, p.shape, 0)
    return jnp.where(row == n - 1, last_row, pltpu.roll(p, n - 1, 0))


def _row_tile(T):
    return min(T, 256)


def _prep_fwd(p_rw, bnd, mu, w0, a0, k_k, k_a, lora, seg64, tri):
    T = p_rw.shape[0]
    tm = _row_tile(T)
    W = RW_WIDTH

    def body(p_ref, bnd_ref, mu_ref, w0_ref, a0_ref, kk_ref, ka_ref, lora_ref, seg_ref, tri_ref,
             hm_ref, v_ref, g_ref, rk_ref):
        p = p_ref[...]
        prev = _shift_down(p, bnd_ref[0])
        res = _prep_fn(p, prev, mu_ref[...], w0_ref[...], a0_ref[...], kk_ref[...], ka_ref[...], lora_ref[...],
                       seg_ref[...], (tri_ref[0], tri_ref[1]))
        for pair, heads in enumerate(_pack_heads(res[:N_VEC])):
            for h, val in enumerate(heads):
                hm_ref[pair, h] = val
        v_ref[...] = res[N_VEC]
        g_ref[...] = res[N_VEC + 1]
        rk_ref[...] = res[N_VEC + 2]

    small = _full((1, W))
    row = pl.BlockSpec((tm, W), lambda i: (i, 0))
    return pl.pallas_call(
        body, name="rwkv_prep_fwd", grid=(T // tm,),
        in_specs=[pl.BlockSpec((tm, RW_COLS), lambda i: (i, 0)), pl.BlockSpec((1, 1, RW_COLS), lambda i: (i, 0, 0)),
                  _full((1, RW_COLS)), small, small, small, small, _full((2 * LORA, 2 * W)), _full((W, W)),
                  _full((2, tm, tm))],
        out_specs=[pl.BlockSpec((N_PAIR, RW_HEADS, tm, 128), lambda i: (0, 0, i, 0)), row, row, row],
        out_shape=[jax.ShapeDtypeStruct((N_PAIR, RW_HEADS, T, 128), F32)] + [jax.ShapeDtypeStruct((T, W), F32)] * 3,
        compiler_params=_cparams(dimension_semantics=("arbitrary",)),
    )(p_rw, bnd, mu, w0, a0, k_k, k_a, lora, seg64, tri)


def _prep_bwd(p_rw, bnd, mu, w0, a0, k_k, k_a, lora, seg64, tri, cts):
    T = p_rw.shape[0]
    tm = _row_tile(T)
    W = RW_WIDTH

    def body(p_ref, bnd_ref, mu_ref, w0_ref, a0_ref, kk_ref, ka_ref, lora_ref, seg_ref, tri_ref,
             dhm_ref, drk_ref, dv1_ref, dv2_ref, dg_ref,
             dp_ref, dprev_ref, dmu_ref, dw0_ref, da0_ref, dkk_p_ref, dka_ref, dlora_ref):
        accs = (dmu_ref, dw0_ref, da0_ref, dkk_p_ref, dka_ref, dlora_ref)

        @pl.when(pl.program_id(0) == 0)
        def _():
            for a_ref in accs:
                a_ref[...] = jnp.zeros_like(a_ref)

        p = p_ref[...]
        prev = _shift_down(p, bnd_ref[0])
        seg, tri = seg_ref[...], (tri_ref[0], tri_ref[1])
        _, vjp = jax.vjp(lambda *a: _prep_fn(*a, seg, tri), p, prev, mu_ref[...], w0_ref[...], a0_ref[...],
                         kk_ref[...], ka_ref[...], lora_ref[...])
        ct = (*_unpack_heads(dhm_ref), dv1_ref[...] + dv2_ref[...], dg_ref[...], drk_ref[...])
        grads = vjp(ct)
        dp_ref[...] = grads[0]
        dprev_ref[...] = grads[1]
        for a_ref, gval in zip(accs, grads[2:]):
            a_ref[...] += gval

    small = _full((1, W))
    row = pl.BlockSpec((tm, W), lambda i: (i, 0))
    return pl.pallas_call(
        body, name="rwkv_prep_bwd", grid=(T // tm,),
        in_specs=[pl.BlockSpec((tm, RW_COLS), lambda i: (i, 0)), pl.BlockSpec((1, 1, RW_COLS), lambda i: (i, 0, 0)),
                  _full((1, RW_COLS)), small, small, small, small, _full((2 * LORA, 2 * W)), _full((W, W)),
                  _full((2, tm, tm)), pl.BlockSpec((N_PAIR, RW_HEADS, tm, 128), lambda i: (0, 0, i, 0))] + [row] * 4,
        out_specs=[pl.BlockSpec((tm, RW_COLS), lambda i: (i, 0)), pl.BlockSpec((tm, RW_COLS), lambda i: (i, 0)),
                   _full((1, RW_COLS)), small, small, small, small, _full((2 * LORA, 2 * W))],
        out_shape=[jax.ShapeDtypeStruct((T, RW_COLS), F32), jax.ShapeDtypeStruct((T, RW_COLS), F32),
                   jax.ShapeDtypeStruct((1, RW_COLS), F32)] + [jax.ShapeDtypeStruct((1, W), F32)] * 4
                  + [jax.ShapeDtypeStruct((2 * LORA, 2 * W), F32)],
        compiler_params=_cparams(dimension_semantics=("arbitrary",)),
    )(p_rw, bnd, mu, w0, a0, k_k, k_a, lora, seg64, tri, *cts)


def _post_fwd(o, rk, v, g, gn_g, gn_b, r_k, seg64):
    T = o.shape[0]
    tm = _row_tile(T)
    W = RW_WIDTH

    def body(o_ref, rk_ref, v_ref, g_ref, gg_ref, gb_ref, rkp_ref, seg_ref, y_ref):
        y_ref[...] = _post_fn(o_ref[...], rk_ref[...], v_ref[...], g_ref[...], gg_ref[...], gb_ref[...],
                              rkp_ref[...], seg_ref[...]).astype(BF16)

    row = pl.BlockSpec((tm, W), lambda i: (i, 0))
    small = _full((1, W))
    return pl.pallas_call(
        body, name="rwkv_post_fwd", grid=(T // tm,),
        in_specs=[row] * 4 + [small] * 3 + [_full((W, W))],
        out_specs=row, out_shape=jax.ShapeDtypeStruct((T, W), BF16),
        compiler_params=_cparams(dimension_semantics=("arbitrary",)),
    )(o, rk, v, g, gn_g, gn_b, r_k, seg64)


def _post_bwd(o, rk, v, g, gn_g, gn_b, r_k, seg64, dy):
    T = o.shape[0]
    tm = _row_tile(T)
    W = RW_WIDTH

    def body(o_ref, rk_ref, v_ref, g_ref, gg_ref, gb_ref, rkp_ref, seg_ref, dy_ref,
             do_ref, drk_ref, dv_ref, dg_ref, dgg_ref, dgb_ref, drkp_ref):
        accs = (dgg_ref, dgb_ref, drkp_ref)

        @pl.when(pl.program_id(0) == 0)
        def _():
            for a_ref in accs:
                a_ref[...] = jnp.zeros_like(a_ref)

        seg = seg_ref[...]
        _, vjp = jax.vjp(lambda *a: _post_fn(*a, seg), o_ref[...], rk_ref[...], v_ref[...], g_ref[...],
                         gg_ref[...], gb_ref[...], rkp_ref[...])
        grads = vjp(dy_ref[...])
        for o_, gval in zip((do_ref, drk_ref, dv_ref, dg_ref), grads[:4]):
            o_[...] = gval
        for a_ref, gval in zip(accs, grads[4:]):
            a_ref[...] += gval

    row = pl.BlockSpec((tm, W), lambda i: (i, 0))
    small = _full((1, W))
    return pl.pallas_call(
        body, name="rwkv_post_bwd", grid=(T // tm,),
        in_specs=[row] * 4 + [small] * 3 + [_full((W, W)), pl.BlockSpec((tm, W), lambda i: (i, 1))],
        out_specs=[row] * 4 + [small] * 3,
        out_shape=[jax.ShapeDtypeStruct((T, W), F32)] * 4 + [jax.ShapeDtypeStruct((1, W), F32)] * 3,
        compiler_params=_cparams(dimension_semantics=("arbitrary",)),
    )(o, rk, v, g, gn_g, gn_b, r_k, seg64, dy)


def _wkv_lhs(hm_ref):
    tiles = [jnp.transpose(hm_ref[p].reshape(RW_HEADS * WKV_CHUNK, 128)) for p in range(N_PAIR)]
    hi, lo = _split(jnp.concatenate(tiles, axis=0)[:N_VEC * RW_HEAD])
    return jnp.concatenate([hi, lo], axis=1)


N_STEP_VEC = N_VEC - 1
PAD_ROWS = 16


def _wkv_fwd(cols, v, e_tab):
    T = v.shape[0]
    Tc = WKV_CHUNK
    nch = T // Tc
    J, W = RW_HEAD, RW_WIDTH
    JS = N_STEP_VEC * J

    def body(cols_ref, v_ref, e_ref, o_ref, states_ref, sa_ref, s_ref):
        @pl.when(pl.program_id(0) == 0)
        def _():
            s_ref[...] = jnp.zeros_like(s_ref)

        lhs = _wkv_lhs(cols_ref)
        st = s_ref[...]
        for t in range(Tc):
            ex = _dot(lhs[:JS], e_ref[t])
            states_ref[t] = st
            sa = -jnp.sum(st * ex[0:J], axis=0, keepdims=True)
            st = st + ex[J:2 * J] * sa + ex[2 * J:3 * J] * v_ref[t:t + 1, :]
            sa_ref[t:t + 1, :] = sa
            o_ref[t:t + 1, :] = jnp.sum(st * ex[3 * J:4 * J], axis=0, keepdims=True)
        s_ref[...] = st * _dot(lhs[JS:], e_ref[Tc - 1])

    return pl.pallas_call(
        body, name="wkv_fwd", grid=(nch,),
        in_specs=[pl.BlockSpec((N_PAIR, RW_HEADS, Tc, 128), lambda c: (0, 0, c, 0)),
                  pl.BlockSpec((Tc, W), lambda c: (c, 0)), _full((Tc, 2 * 128, W))],
        out_specs=[pl.BlockSpec((Tc, W), lambda c: (c, 0)), pl.BlockSpec((Tc, J, W), lambda c: (c, 0, 0)),
                   pl.BlockSpec((Tc, W), lambda c: (c, 0))],
        out_shape=[jax.ShapeDtypeStruct((T, W), F32), jax.ShapeDtypeStruct((T, J, W), F32),
                   jax.ShapeDtypeStruct((T, W), F32)],
        scratch_shapes=[pltpu.VMEM((J, W), F32)],
        compiler_params=_cparams(dimension_semantics=("arbitrary",)),
    )(cols, v, e_tab)


def _wkv_bwd(cols, v, do, states, sa, e_tab, r_tab):
    T = v.shape[0]
    Tc = WKV_CHUNK
    nch = T // Tc
    J, W = RW_HEAD, RW_WIDTH
    JS = N_STEP_VEC * J
    blocks = [slice(b * 128, (b + 1) * 128) for b in range(W // 128)]

    def body(cols_ref, v_ref, do_ref, states_ref, sa_ref, e_ref, r_ref, dv_ref, dhm_ref, ds_ref):
        @pl.when(pl.program_id(0) == 0)
        def _():
            ds_ref[...] = jnp.zeros_like(ds_ref)

        lhs = _wkv_lhs(cols_ref)
        last = Tc - 1
        ex = _dot(lhs, e_ref[last])
        dst, ends = [], []
        for b in blocks:
            s_end = (states_ref[last, :, b] + ex[J:2 * J, b] * sa_ref[last:last + 1, b]
                     + ex[2 * J:3 * J, b] * v_ref[last:last + 1, b])
            d_end = ds_ref[:, b]
            ends.append((d_end * s_end).astype(BF16))
            dst.append(d_end * ex[JS:, b])
        d_decay = _dot(jnp.concatenate(ends, axis=1), r_ref[last])
        acc = jnp.zeros((JS + PAD_ROWS, 128), F32)
        for t in reversed(range(Tc)):
            if t != last:
                ex = _dot(lhs[:JS], e_ref[t])
            dvs, prods = [], []
            for i, b in enumerate(blocks):
                kk_e, b_e, k_e, r_e = (ex[n * J:(n + 1) * J, b] for n in range(N_STEP_VEC))
                do_row, v_row, sa_row = do_ref[t:t + 1, b], v_ref[t:t + 1, b], sa_ref[t:t + 1, b]
                s_old = states_ref[t, :, b]
                s_new = states_ref[t + 1, :, b] if t != last else s_old + b_e * sa_row + k_e * v_row
                dsn = dst[i] + r_e * do_row
                dsa = jnp.sum(dsn * b_e, axis=0, keepdims=True)
                dvs.append(jnp.sum(dsn * k_e, axis=0, keepdims=True))
                prods.append(jnp.concatenate(
                    [s_old * (-dsa), dsn * sa_row, dsn * v_row, s_new * do_row, jnp.zeros((PAD_ROWS, 128), F32)],
                    axis=0).astype(BF16))
                dst[i] = dsn - kk_e * dsa
            dv_ref[t:t + 1, :] = jnp.concatenate(dvs, axis=1)
            acc = acc + _dot(jnp.concatenate(prods, axis=1), r_ref[t])
        for i, b in enumerate(blocks):
            ds_ref[:, b] = dst[i]
        tiles = jnp.concatenate([acc[:JS], d_decay, jnp.zeros((2 * N_PAIR * J - N_VEC * J, 128), F32)], axis=0)
        for p in range(N_PAIR):
            dhm_ref[p] = jnp.transpose(tiles[p * 128:(p + 1) * 128]).reshape(RW_HEADS, Tc, 128)

    rev2 = lambda c: (nch - 1 - c, 0)
    rev3 = lambda c: (nch - 1 - c, 0, 0)
    rev_hm = lambda c: (0, 0, nch - 1 - c, 0)
    hm_spec = pl.BlockSpec((N_PAIR, RW_HEADS, Tc, 128), rev_hm)
    return pl.pallas_call(
        body, name="wkv_bwd", grid=(nch,),
        in_specs=[hm_spec, pl.BlockSpec((Tc, W), rev2), pl.BlockSpec((Tc, W), rev2),
                  pl.BlockSpec((Tc, J, W), rev3), pl.BlockSpec((Tc, W), rev2),
                  _full((Tc, 2 * 128, W)), _full((Tc, W, 128))],
        out_specs=[pl.BlockSpec((Tc, W), rev2), hm_spec],
        out_shape=[jax.ShapeDtypeStruct((T, W), F32), jax.ShapeDtypeStruct((N_PAIR, RW_HEADS, T, 128), F32)],
        scratch_shapes=[pltpu.VMEM((J, W), F32)],
        compiler_params=_cparams(dimension_semantics=("arbitrary",)),
    )(cols, v, do, states, sa, e_tab, r_tab)


def _outproj(x, y_ret, y_rw, w_out_b, target, gf):
    T = x.shape[0]
    tm = _row_tile(T)
    W = RW_WIDTH

    def body(x_ref, yr_ref, yw_ref, w_ref, t_ref, gf_ref, loss_ref, dh_ref, dy_ref, dw_ref, dgf_ref):
        @pl.when(pl.program_id(0) == 0)
        def _():
            loss_ref[...] = jnp.zeros_like(loss_ref)
            dw_ref[...] = jnp.zeros_like(dw_ref)
            dgf_ref[...] = jnp.zeros_like(dgf_ref)

        y = jnp.concatenate([yr_ref[...], yw_ref[...]], axis=1)
        w = w_ref[...]
        h = x_ref[...] + _dot(y, w)
        rstd = lax.rsqrt(jnp.mean(h * h, axis=-1, keepdims=True) + RMS_EPS)
        hn = h * rstd
        gfv = gf_ref[...]
        err = hn * gfv - t_ref[...]
        loss_ref[...] += 0.5 * jnp.sum(jnp.mean(err * err, axis=-1))
        dout = err * (1.0 / D_MODEL)
        dgf_ref[...] += jnp.sum(dout * hn, axis=0, keepdims=True)
        dhn = dout * gfv
        dh = rstd * (dhn - hn * jnp.mean(dhn * hn, axis=-1, keepdims=True))
        dh_ref[...] = dh
        dhb = dh.astype(BF16)
        dy_ref[...] = _dot_nt(dhb, w)
        dw_ref[...] += _dot_tn(y, dhb)

    return pl.pallas_call(
        body, name="outproj_loss", grid=(T // tm,),
        in_specs=[pl.BlockSpec((tm, D_MODEL), lambda i: (i, 0)), pl.BlockSpec((tm, W), lambda i: (i, 0)),
                  pl.BlockSpec((tm, W), lambda i: (i, 0)), _full((D_MODEL, D_MODEL)),
                  pl.BlockSpec((tm, D_MODEL), lambda i: (i, 0)), _full((1, D_MODEL))],
        out_specs=[_full((1, PACK_W)), pl.BlockSpec((tm, D_MODEL), lambda i: (i, 0)),
                   pl.BlockSpec((tm, D_MODEL), lambda i: (i, 0)), _full((D_MODEL, D_MODEL)), _full((1, D_MODEL))],
        out_shape=[jax.ShapeDtypeStruct((1, PACK_W), F32), jax.ShapeDtypeStruct((T, D_MODEL), F32),
                   jax.ShapeDtypeStruct((T, D_MODEL), F32), jax.ShapeDtypeStruct((D_MODEL, D_MODEL), F32),
                   jax.ShapeDtypeStruct((1, D_MODEL), F32)],
        compiler_params=_cparams(dimension_semantics=("arbitrary",)),
    )(x, y_ret, y_rw, w_out_b, target, gf)


def _inproj_bwd_x(dp_ret, dp_rw, dprev, dbnd, w_b, x, norm_g, dh):
    T = x.shape[0]
    tm = _row_tile(T)

    def body(dpr_ref, dpw_ref, dprev_ref, dbnd_ref, w_ref, x_ref, g_ref, dh_ref, gx_ref, dg_ref, dpb_ref):
        @pl.when(pl.program_id(0) == 0)
        def _():
            dg_ref[...] = jnp.zeros_like(dg_ref)

        d_rw = dpw_ref[...] + _shift_up(dprev_ref[...], dbnd_ref[0])
        dpb = jnp.concatenate([dpr_ref[...].astype(BF16), d_rw.astype(BF16)], axis=1)
        dpb_ref[...] = dpb
        du = _dot_nt(dpb, w_ref[...])
        xf = x_ref[...]
        rstd = lax.rsqrt(jnp.mean(xf * xf, axis=-1, keepdims=True) + RMS_EPS)
        xn = xf * rstd
        dg_ref[...] += jnp.sum(du * xn, axis=0, keepdims=True)
        dxn = du * g_ref[...]
        gx_ref[...] = dh_ref[...] + rstd * (dxn - xn * jnp.mean(dxn * xn, axis=-1, keepdims=True))

    return pl.pallas_call(
        body, name="inproj_bwd_x", grid=(T // tm,),
        in_specs=[pl.BlockSpec((tm, RET_COLS), lambda i: (i, 0)), pl.BlockSpec((tm, RW_COLS), lambda i: (i, 0)),
                  pl.BlockSpec((tm, RW_COLS), lambda i: (i, 0)), pl.BlockSpec((1, 1, RW_COLS), lambda i: (i, 0, 0)),
                  _full((D_MODEL, IN_COLS)), pl.BlockSpec((tm, D_MODEL), lambda i: (i, 0)), _full((1, D_MODEL)),
                  pl.BlockSpec((tm, D_MODEL), lambda i: (i, 0))],
        out_specs=[pl.BlockSpec((tm, D_MODEL), lambda i: (i, 0)), _full((1, D_MODEL)),
                   pl.BlockSpec((tm, IN_COLS), lambda i: (i, 0))],
        out_shape=[jax.ShapeDtypeStruct((T, D_MODEL), F32), jax.ShapeDtypeStruct((1, D_MODEL), F32),
                   jax.ShapeDtypeStruct((T, IN_COLS), BF16)],
        compiler_params=_cparams(dimension_semantics=("arbitrary",)),
    )(dp_ret, dp_rw, dprev, dbnd, w_b, x, norm_g, dh)


def _inproj_bwd_w(u_t, dpb):
    T = u_t.shape[1]
    tr = 256

    def body(u_ref, d_ref, o_ref):
        o_ref[...] = _dot(u_ref[...], d_ref[...])

    return pl.pallas_call(
        body, name="inproj_bwd_w", grid=(D_MODEL // tr,),
        in_specs=[pl.BlockSpec((tr, T), lambda i: (i, 0)), _full((T, IN_COLS))],
        out_specs=pl.BlockSpec((tr, IN_COLS), lambda i: (i, 0)),
        out_shape=jax.ShapeDtypeStruct((D_MODEL, IN_COLS), F32),
        compiler_params=_cparams(dimension_semantics=("arbitrary",)),
    )(u_t, dpb)


def _tile_boundaries(a, tm, first):
    T, n = a.shape
    zero = jnp.zeros((1, n), a.dtype)
    if first:
        rows = jnp.concatenate([zero, a[tm - 1:T - 1:tm]], axis=0)
    else:
        rows = jnp.concatenate([a[tm:T:tm], zero], axis=0)
    return rows.reshape(T // tm, 1, n)


def _local_step(x, target, w_in_b, w_out_b, lora, small):
    T = x.shape[0]
    tm = _row_tile(T)
    cos, sin = _rope_tables(T)
    tabs = _ret_tables()
    seg128 = _seg_matrix(RET_WIDTH, RET_DV)
    seg64 = _seg_matrix(RW_WIDTH, RW_HEAD)
    e_tab = _wkv_expand_table()
    r_tab = _wkv_reduce_table()
    prep_w = (small["rwkv_mu"], small["w0"], small["a0"], small["k_k"], small["k_a"], lora, seg64, _chunk_tables(tm))
    post_w = (small["rwkv_gn_g"], small["rwkv_gn_b"], small["r_k"], seg64)

    p_ret, p_rw, u = _inproj_fwd(x, small["norm_g"], w_in_b)
    y_ret, ret, s_in_all = _ret_fwd(p_ret, cos, sin, tabs, small["ret_gn_g"], seg128)
    bnd = _tile_boundaries(p_rw, tm, True)
    hm, v, g, rk = _prep_fwd(p_rw, bnd, *prep_w)
    o, states, sa = _wkv_fwd(hm, v, e_tab)
    y_rw = _post_fwd(o, rk, v, g, *post_w)
    loss, dh, dy, d_w_out, d_gf = _outproj(x, y_ret, y_rw, w_out_b, target, small["final_norm_g"])

    do, d_rk, dv2, dg, d_gn_g, d_gn_b, d_r_k = _post_bwd(o, rk, v, g, *post_w, dy)
    dv1, d_hm = _wkv_bwd(hm, v, do, states, sa, e_tab, r_tab)
    dp_rw, dprev, d_mu, d_w0, d_a0, d_k_k, d_k_a, d_lora = _prep_bwd(
        p_rw, bnd, *prep_w, (d_hm, d_rk, dv1, dv2, dg))
    dp_ret, d_ret_gn = _ret_bwd(p_ret, cos, sin, tabs, small["ret_gn_g"], seg128, ret, s_in_all, dy)
    dbnd = _tile_boundaries(dprev, tm, False)
    grad_x, d_norm_g, dpb = _inproj_bwd_x(dp_ret, dp_rw, dprev, dbnd, w_in_b, x, small["norm_g"], dh)
    d_w_in = _inproj_bwd_w(jnp.transpose(u), dpb)

    d_small = {"norm_g": d_norm_g, "ret_gn_g": d_ret_gn, "rwkv_mu": d_mu, "w0": d_w0, "a0": d_a0, "k_k": d_k_k,
               "k_a": d_k_a, "r_k": d_r_k, "rwkv_gn_g": d_gn_g, "rwkv_gn_b": d_gn_b, "final_norm_g": d_gf}
    return loss, grad_x, d_w_in, d_w_out, d_lora, d_small


ANY = pl.BlockSpec(memory_space=pl.ANY)
CHIP_FLIPS = ((0, 1), (1, 0), (1, 1))
N_FLIPS = len(CHIP_FLIPS)
LORA_SHARD = RW_WIDTH // N_CHIPS
HALF_IN = D_MODEL // 2
HALF_OUT = OUT_SHARD // 2


def _position():
    return lax.axis_index("x"), lax.axis_index("y"), lax.axis_index("c")


def _flip(v, f):
    return 1 - v if f else v


def _finish(local, remote, landed):
    for cp in landed:
        cp.wait_recv()
    for cp in remote:
        cp.wait_send()
    for cp in local:
        cp.wait()


def _gather_chips(arrs):
    n = len(arrs)

    def body(*refs):
        ins, outs = refs[:n], refs[n:2 * n]
        send, recv, pass_send, pass_recv = refs[2 * n:]
        x, y, c = _position()
        s = 2 * x + y
        sibling = (x, y, 1 - c)

        def copy(src, dst, sems, k, to):
            return pltpu.make_async_remote_copy(src_ref=src, dst_ref=dst, send_sem=sems[0].at[k], recv_sem=sems[1].at[k],
                                                device_id=to, device_id_type=MESH)

        remote, landed, passed, passed_in = [], [], [], []
        for a in range(n):
            for j, (fx, fy) in enumerate(CHIP_FLIPS):
                px, py = _flip(x, fx), _flip(y, fy)
                ps = 2 * px + py
                k = a * N_FLIPS + j
                remote.append(copy(ins[a].at[c], outs[a].at[s, c], (send, recv), k, (px, py, c)))
                landed.append(copy(ins[a].at[c], outs[a].at[ps, c], (send, recv), k, (px, py, c)))
                passed.append(copy(outs[a].at[ps, c], outs[a].at[ps, c], (pass_send, pass_recv), k, sibling))
                passed_in.append(copy(outs[a].at[ps, 1 - c], outs[a].at[ps, 1 - c], (pass_send, pass_recv), k, sibling))
        for cp in remote:
            cp.start()
        for arrived, onward in zip(landed, passed):
            arrived.wait_recv()
            onward.start()
        _finish([], remote + passed, passed_in)

    sems = pltpu.SemaphoreType.DMA((n * N_FLIPS,))
    return pl.pallas_call(
        body, name="gather_weights",
        in_specs=[ANY] * n, out_specs=[ANY] * n,
        out_shape=[jax.ShapeDtypeStruct((N_CHIPS,) + a.shape, a.dtype) for a in arrs],
        scratch_shapes=[sems, sems, sems, sems],
    )(*arrs)


def _pair_exchange(g_in, g_out, g_small):
    def body(gi_ref, go_ref, gs_ref, li_ref, lo_ref, ls_ref, send, recv):
        x, y, c = _position()
        peer = (x, y, 1 - c)
        srcs = (gi_ref.at[:, pl.ds((1 - c) * HALF_IN, HALF_IN), :], go_ref.at[:, pl.ds((1 - c) * HALF_OUT, HALF_OUT), :],
                gs_ref)
        remote = [pltpu.make_async_remote_copy(src_ref=src, dst_ref=dst, send_sem=send.at[k], recv_sem=recv.at[k],
                                               device_id=peer, device_id_type=MESH)
                  for k, (src, dst) in enumerate(zip(srcs, (li_ref, lo_ref, ls_ref)))]
        for cp in remote:
            cp.start()
        _finish([], remote, remote)

    return pl.pallas_call(
        body, name="pair_exchange",
        in_specs=[ANY] * 3, out_specs=[ANY] * 3,
        out_shape=[jax.ShapeDtypeStruct((N_CHIPS, HALF_IN, IN_SHARD), F32),
                   jax.ShapeDtypeStruct((N_CHIPS, HALF_OUT, D_MODEL), F32),
                   jax.ShapeDtypeStruct(g_small.shape, F32)],
        scratch_shapes=[pltpu.SemaphoreType.DMA((3,)), pltpu.SemaphoreType.DMA((3,))],
    )(g_in, g_out, g_small)


def _pair_sum(g_in, g_out, g_small, l_in, l_out, l_small, c_arr):
    tr = HALF_IN // 2

    def body(c_ref, gi_ref, go_ref, gs_ref, li_ref, lo_ref, ls_ref, ci_ref, co_ref, cs_ref):
        ci_ref[...] = (gi_ref[...] + li_ref[...]).astype(BF16)

        @pl.when(pl.program_id(1) == 0)
        def _():
            co_ref[...] = (go_ref[...] + lo_ref[...]).astype(BF16)

        @pl.when((pl.program_id(0) == 0) & (pl.program_id(1) == 0))
        def _():
            cs_ref[...] = gs_ref[...] + ls_ref[...]

    nd = g_small.shape
    return pl.pallas_call(
        body, name="pair_sum",
        grid_spec=pltpu.PrefetchScalarGridSpec(
            num_scalar_prefetch=1, grid=(N_CHIPS, 2),
            in_specs=[pl.BlockSpec((1, tr, IN_SHARD), lambda s, i, c: (s, 2 * c[0] + i, 0)),
                      pl.BlockSpec((1, HALF_OUT, D_MODEL), lambda s, i, c: (s, c[0], 0)),
                      pl.BlockSpec(nd, lambda s, i, c: (0, 0)),
                      pl.BlockSpec((1, tr, IN_SHARD), lambda s, i, c: (s, i, 0)),
                      pl.BlockSpec((1, HALF_OUT, D_MODEL), lambda s, i, c: (s, 0, 0)),
                      pl.BlockSpec(nd, lambda s, i, c: (0, 0))],
            out_specs=[pl.BlockSpec((1, tr, IN_SHARD), lambda s, i, c: (s, i, 0)),
                       pl.BlockSpec((1, HALF_OUT, D_MODEL), lambda s, i, c: (s, 0, 0)),
                       pl.BlockSpec(nd, lambda s, i, c: (0, 0))]),
        out_shape=[jax.ShapeDtypeStruct((N_CHIPS, HALF_IN, IN_SHARD), BF16),
                   jax.ShapeDtypeStruct((N_CHIPS, HALF_OUT, D_MODEL), BF16), jax.ShapeDtypeStruct(nd, F32)],
        compiler_params=_cparams(dimension_semantics=("arbitrary", "arbitrary")),
    )(c_arr, g_in, g_out, g_small, l_in, l_out, l_small)


def _chip_exchange(c_in, c_out, c_small):
    def body(ci_ref, co_ref, cs_ref, li_ref, lo_ref, ls_ref, send, recv):
        x, y, c = _position()
        s = 2 * x + y
        remote = []
        for j, (fx, fy) in enumerate(CHIP_FLIPS):
            px, py = _flip(x, fx), _flip(y, fy)
            ps = 2 * px + py
            for a, (src, dst) in enumerate(((ci_ref.at[ps], li_ref.at[j]), (co_ref.at[ps], lo_ref.at[j]),
                                            (cs_ref, ls_ref.at[j]))):
                k = 3 * j + a
                remote.append(pltpu.make_async_remote_copy(src_ref=src, dst_ref=dst, send_sem=send.at[k],
                                                           recv_sem=recv.at[k], device_id=(px, py, c),
                                                           device_id_type=MESH))
        for cp in remote:
            cp.start()
        _finish([], remote, remote)

    return pl.pallas_call(
        body, name="chip_exchange",
        in_specs=[ANY] * 3, out_specs=[ANY] * 3,
        out_shape=[jax.ShapeDtypeStruct((N_FLIPS, HALF_IN, IN_SHARD), c_in.dtype),
                   jax.ShapeDtypeStruct((N_FLIPS, HALF_OUT, D_MODEL), c_out.dtype),
                   jax.ShapeDtypeStruct((N_FLIPS,) + c_small.shape, F32)],
        scratch_shapes=[pltpu.SemaphoreType.DMA((3 * N_FLIPS,)), pltpu.SemaphoreType.DMA((3 * N_FLIPS,))],
    )(c_in, c_out, c_small)


def _chip_sum(g_in, g_out, p_in, p_out, c_small, l_in, l_out, l_small, sc_arr):
    tr = HALF_IN // 2
    nd = c_small.shape

    def body(s_ref, gi_ref, go_ref, pi_ref, po_ref, cs_ref, li0, li1, li2, lo0, lo1, lo2, ls_ref,
             ri_ref, ro_ref, rs_ref):
        ri_ref[...] = (((gi_ref[0] + pi_ref[0]) + li0[0].astype(F32)) + li1[0].astype(F32)) + li2[0].astype(F32)

        @pl.when(pl.program_id(0) == 0)
        def _():
            ro_ref[...] = (((go_ref[0] + po_ref[0]) + lo0[0].astype(F32)) + lo1[0].astype(F32)) + lo2[0].astype(F32)
            me = s_ref[0]
            parts = (cs_ref[...], ls_ref[0], ls_ref[1], ls_ref[2])

            def of_chip(s):
                m = jnp.bitwise_xor(me, s)
                return jnp.where(m == 0, parts[0], jnp.where(m == 1, parts[1], jnp.where(m == 2, parts[2], parts[3])))

            rs_ref[...] = ((of_chip(0) + of_chip(1)) + of_chip(2)) + of_chip(3)

    def flip_in(j):
        return pl.BlockSpec((1, tr, IN_SHARD), lambda i, s: (j, i, 0))

    def flip_out(j):
        return pl.BlockSpec((1, HALF_OUT, D_MODEL), lambda i, s: (j, 0, 0))

    return pl.pallas_call(
        body, name="chip_sum",
        grid_spec=pltpu.PrefetchScalarGridSpec(
            num_scalar_prefetch=1, grid=(2,),
            in_specs=[pl.BlockSpec((1, tr, IN_SHARD), lambda i, s: (s[0], 2 * s[1] + i, 0)),
                      pl.BlockSpec((1, HALF_OUT, D_MODEL), lambda i, s: (s[0], s[1], 0)),
                      pl.BlockSpec((1, tr, IN_SHARD), lambda i, s: (s[0], i, 0)),
                      pl.BlockSpec((1, HALF_OUT, D_MODEL), lambda i, s: (s[0], 0, 0)),
                      pl.BlockSpec(nd, lambda i, s: (0, 0)),
                      flip_in(0), flip_in(1), flip_in(2), flip_out(0), flip_out(1), flip_out(2),
                      pl.BlockSpec((N_FLIPS,) + nd, lambda i, s: (0, 0, 0))],
            out_specs=[pl.BlockSpec((tr, IN_SHARD), lambda i, s: (i, 0)),
                       pl.BlockSpec((HALF_OUT, D_MODEL), lambda i, s: (0, 0)),
                       pl.BlockSpec(nd, lambda i, s: (0, 0))]),
        out_shape=[jax.ShapeDtypeStruct((HALF_IN, IN_SHARD), F32), jax.ShapeDtypeStruct((HALF_OUT, D_MODEL), F32),
                   jax.ShapeDtypeStruct(nd, F32)],
        compiler_params=_cparams(dimension_semantics=("arbitrary",)),
    )(sc_arr, g_in, g_out, p_in, p_out, c_small, l_in, l_in, l_in, l_out, l_out, l_out, l_small)


def _pair_share(r_in, r_out):
    def body(ri_ref, ro_ref, li_ref, lo_ref, send, recv):
        x, y, c = _position()
        remote = [pltpu.make_async_remote_copy(src_ref=src, dst_ref=dst, send_sem=send.at[k], recv_sem=recv.at[k],
                                               device_id=(x, y, 1 - c), device_id_type=MESH)
                  for k, (src, dst) in enumerate(((ri_ref, li_ref), (ro_ref, lo_ref)))]
        for cp in remote:
            cp.start()
        _finish([], remote, remote)

    return pl.pallas_call(
        body, name="pair_share",
        in_specs=[ANY] * 2, out_specs=[ANY] * 2,
        out_shape=[jax.ShapeDtypeStruct(r_in.shape, F32), jax.ShapeDtypeStruct(r_out.shape, F32)],
        scratch_shapes=[pltpu.SemaphoreType.DMA((2,)), pltpu.SemaphoreType.DMA((2,))],
    )(r_in, r_out)


def _adam_update(w, g, m, v):
    mn = ADAM_B1 * m + (1.0 - ADAM_B1) * g
    vn = ADAM_B2 * v + (1.0 - ADAM_B2) * jnp.square(g)
    m_hat = mn / (1.0 - ADAM_B1 ** ADAM_STEP)
    v_hat = vn / (1.0 - ADAM_B2 ** ADAM_STEP)
    return -ADAM_LR * (m_hat / (jnp.sqrt(v_hat) + ADAM_EPS) + ADAM_WD * w), mn, vn


def _adamw(name, w, g_mine, g_theirs, m, v, core_arr, tr):
    rows, cols = w.shape
    per_half = rows // 2 // tr

    def body(c_ref, w_ref, gm_ref, gt_ref, m_ref, v_ref, g_ref, d_ref, nm_ref, nv_ref):
        mine = (pl.program_id(0) // per_half) == c_ref[0]
        g = jnp.where(mine, gm_ref[...], gt_ref[...])
        d, mn, vn = _adam_update(w_ref[...], g, m_ref[...], v_ref[...])
        g_ref[...] = g
        d_ref[...] = d
        nm_ref[...] = mn
        nv_ref[...] = vn

    spec = pl.BlockSpec((tr, cols), lambda i, c: (i, 0))
    half = pl.BlockSpec((tr, cols), lambda i, c: (i % per_half, 0))
    return pl.pallas_call(
        body, name=name,
        grid_spec=pltpu.PrefetchScalarGridSpec(
            num_scalar_prefetch=1, grid=(rows // tr,),
            in_specs=[spec, half, half, spec, spec], out_specs=[spec] * 4),
        out_shape=[jax.ShapeDtypeStruct((rows, cols), F32)] * 4,
        compiler_params=_cparams(dimension_semantics=("arbitrary",)),
    )(core_arr, w, g_mine, g_theirs, m, v)


def _row_pieces(n):
    return [(k, k * PACK_W, min(PACK_W, n - k * PACK_W)) for k in range(-(-n // PACK_W))]


def _pack_small(d_small, loss, d_lora):
    ns = len(SMALL_NAMES)

    def body(*refs):
        small_refs, (loss_ref, lora_ref, out_ref) = refs[:ns], refs[ns:]
        out_ref[...] = jnp.zeros_like(out_ref)
        out_ref[PACK_LORA_W:PACK_LORA_W + LORA, :] = lora_ref[:LORA, :RW_WIDTH]
        out_ref[PACK_LORA_A:PACK_LORA_A + LORA, :] = lora_ref[LORA:, RW_WIDTH:]
        for name, n, ref in zip(SMALL_NAMES, SMALL_SIZES, small_refs):
            for k, at, w in _row_pieces(n):
                out_ref[PACK_AT[name] + k:PACK_AT[name] + k + 1, 0:w] = ref[:, at:at + w]
        out_ref[PACK_LOSS:PACK_LOSS + 1, :] = loss_ref[...]

    return pl.pallas_call(body, name="pack_small", out_shape=jax.ShapeDtypeStruct((PACK_ROWS, PACK_W), F32),
                          compiler_params=_cparams())(*d_small, loss, d_lora)


def _adamw_small(tot, chip_arr, ws, ms, vs):
    ns = len(SMALL_NAMES)
    n_par = ns + 2

    def body(s_ref, tot_ref, glw_ref, gla_ref, *refs):
        w_refs, m_refs, v_refs = refs[:n_par], refs[n_par:2 * n_par], refs[2 * n_par:3 * n_par]
        outs = refs[3 * n_par:]
        g_refs, d_refs, nm_refs, nv_refs = (outs[i * n_par:(i + 1) * n_par] for i in range(4))
        grads = [jnp.concatenate([tot_ref[PACK_AT[name] + k:PACK_AT[name] + k + 1, 0:w] for k, _, w in _row_pieces(n)],
                                 axis=1) for name, n in zip(SMALL_NAMES, SMALL_SIZES)]
        grads += [glw_ref[...], gla_ref[...]]
        for i, g in enumerate(grads):
            d, mn, vn = _adam_update(w_refs[i][...], g, m_refs[i][...], v_refs[i][...])
            g_refs[i][...] = g
            d_refs[i][...] = d
            nm_refs[i][...] = mn
            nv_refs[i][...] = vn

    def whole(a):
        nd = a.ndim
        return pl.BlockSpec(a.shape, lambda i, s: (0,) * nd)

    shard = (LORA, LORA_SHARD)
    par_specs = [whole(a) for a in ws]
    res = pl.pallas_call(
        body, name="adamw_small",
        grid_spec=pltpu.PrefetchScalarGridSpec(
            num_scalar_prefetch=1, grid=(1,),
            in_specs=[whole(tot), pl.BlockSpec(shard, lambda i, s: (PACK_LORA_W // LORA, s[0])),
                      pl.BlockSpec(shard, lambda i, s: (PACK_LORA_A // LORA, s[0]))] + par_specs * 3,
            out_specs=par_specs * 4),
        out_shape=[jax.ShapeDtypeStruct(a.shape, F32) for a in ws] * 4,
        compiler_params=_cparams(dimension_semantics=("arbitrary",)),
    )(chip_arr, tot, tot, tot, *ws, *ms, *vs)
    return [res[i * n_par:(i + 1) * n_par] for i in range(4)]


def kernel(x, norm_g, w_in, ret_gn_g, rwkv_mu, w_lora_up, w0, a_lora_up, a0, k_k, k_a, r_k, rwkv_gn_g, rwkv_gn_b, w_out, final_norm_g, loss_target, m_norm_g, m_w_in, m_ret_gn_g, m_rwkv_mu, m_w_lora_up, m_w0, m_a_lora_up, m_a0, m_k_k, m_k_a, m_r_k, m_rwkv_gn_g, m_rwkv_gn_b, m_w_out, m_final_norm_g, v_norm_g, v_w_in, v_ret_gn_g, v_rwkv_mu, v_w_lora_up, v_w0, v_a_lora_up, v_a0, v_k_k, v_k_a, v_r_k, v_rwkv_gn_g, v_rwkv_gn_b, v_w_out, v_final_norm_g):
    W = RW_WIDTH
    params = dict(norm_g=norm_g, ret_gn_g=ret_gn_g, rwkv_mu=rwkv_mu, w0=w0, a0=a0, k_k=k_k, k_a=k_a, r_k=r_k,
                  rwkv_gn_g=rwkv_gn_g, rwkv_gn_b=rwkv_gn_b, final_norm_g=final_norm_g)
    moments_m = dict(norm_g=m_norm_g, ret_gn_g=m_ret_gn_g, rwkv_mu=m_rwkv_mu, w0=m_w0, a0=m_a0, k_k=m_k_k, k_a=m_k_a,
                     r_k=m_r_k, rwkv_gn_g=m_rwkv_gn_g, rwkv_gn_b=m_rwkv_gn_b, final_norm_g=m_final_norm_g)
    moments_v = dict(norm_g=v_norm_g, ret_gn_g=v_ret_gn_g, rwkv_mu=v_rwkv_mu, w0=v_w0, a0=v_a0, k_k=v_k_k, k_a=v_k_a,
                     r_k=v_r_k, rwkv_gn_g=v_rwkv_gn_g, rwkv_gn_b=v_rwkv_gn_b, final_norm_g=v_final_norm_g)
    xi, yi, ci = _position()
    chip = (2 * xi + yi).astype(jnp.int32)

    def halves(a):
        return a.reshape(2, a.shape[0] // 2, a.shape[1])

    mine = [halves(w_in[0].astype(BF16)), halves(w_out[0].astype(BF16)), halves(w_lora_up[0]), halves(a_lora_up[0])]
    g_in, g_out, g_lw, g_la = [lax.dynamic_update_slice(g, own[None], (chip, 0, 0, 0))
                               for g, own in zip(_gather_chips(mine), mine)]
    w_in_b = jnp.transpose(g_in.reshape(N_CHIPS, D_MODEL, IN_SHARD), (1, 0, 2)).reshape(D_MODEL, IN_COLS)
    w_out_b = g_out.reshape(D_MODEL, D_MODEL)
    lw = jnp.transpose(g_lw.reshape(N_CHIPS, LORA, LORA_SHARD), (1, 0, 2)).reshape(LORA, W)
    la = jnp.transpose(g_la.reshape(N_CHIPS, LORA, LORA_SHARD), (1, 0, 2)).reshape(LORA, W)
    zero = jnp.zeros((LORA, W), F32)
    lora = jnp.concatenate([jnp.concatenate([lw, zero], axis=1), jnp.concatenate([zero, la], axis=1)], axis=0)
    small = {n: params[n].reshape(1, -1) for n in SMALL_NAMES}

    loss, grad_x, d_w_in, d_w_out, d_lora, d_small = _local_step(x[0], loss_target[0], w_in_b, w_out_b, lora, small)

    core = ci.astype(jnp.int32)
    gi = jnp.transpose(d_w_in.reshape(D_MODEL, N_CHIPS, IN_SHARD), (1, 0, 2))
    go = d_w_out.reshape(N_CHIPS, OUT_SHARD, D_MODEL)
    gs = _pack_small([d_small[n] for n in SMALL_NAMES], loss, d_lora)
    p_in, p_out, p_small = _pair_exchange(gi, go, gs)
    c_in, c_out, c_small = _pair_sum(gi, go, gs, p_in, p_out, p_small, core.reshape(1))
    l_in, l_out, l_small = _chip_exchange(c_in, c_out, c_small)
    r_in, r_out, tot = _chip_sum(gi, go, p_in, p_out, c_small, l_in, l_out, l_small, jnp.stack([chip, core]))
    t_in, t_out = _pair_share(r_in, r_out)

    grad_w_in, d_in, nm_in, nv_in = _adamw("adamw_w_in", w_in[0], r_in, t_in, m_w_in[0], v_w_in[0], core.reshape(1), 256)
    grad_w_out, d_out, nm_out, nv_out = _adamw("adamw_w_out", w_out[0], r_out, t_out, m_w_out[0], v_w_out[0],
                                               core.reshape(1), HALF_OUT)
    par_names = SMALL_NAMES + ("w_lora_up", "a_lora_up")

    def operands(tree, lw_, la_):
        return [tree[n].reshape(1, -1) for n in SMALL_NAMES] + [lw_[0], la_[0]]

    res = _adamw_small(tot, chip.reshape(1), operands(params, w_lora_up, a_lora_up),
                       operands(moments_m, m_w_lora_up, m_a_lora_up), operands(moments_v, v_w_lora_up, v_a_lora_up))

    names = ("norm_g", "w_in", "ret_gn_g", "rwkv_mu", "w_lora_up", "w0", "a_lora_up", "a0", "k_k", "k_a", "r_k",
             "rwkv_gn_g", "rwkv_gn_b", "w_out", "final_norm_g")
    shapes = dict(w_in=w_in.shape, w_out=w_out.shape, w_lora_up=w_lora_up.shape, a_lora_up=a_lora_up.shape,
                  **{n: params[n].shape for n in SMALL_NAMES})

    def leaves(pars, big_in, big_out):
        tree = dict(zip(par_names, pars), w_in=big_in, w_out=big_out)
        return [tree[n].reshape(shapes[n]) for n in names]

    grads = leaves(res[0], grad_w_in, grad_w_out)
    deltas = leaves(res[1], d_in, d_out)
    new_m = leaves(res[2], nm_in, nm_out)
    new_v = leaves(res[3], nv_in, nv_out)
    return (tot[PACK_LOSS, 0], grad_x.reshape(x.shape), *grads, *deltas, *new_m, *new_v)
```

```python
import functools

import numpy as np
import jax
import jax.numpy as jnp
from jax import lax
from jax.experimental import pallas as pl
from jax.experimental.pallas import tpu as pltpu

F32 = jnp.float32
BF16 = jnp.bfloat16
X3 = "bf16x3"
MESH = pl.DeviceIdType.MESH

D_MODEL = 1024
N_CHIPS = 4
RET_HEADS = 4
RET_DK = 64
RET_DV = 128
RET_QK = RET_HEADS * RET_DK
RET_WIDTH = RET_HEADS * RET_DV
RET_COLS = 2 * RET_QK + 2 * RET_WIDTH
RET_CHUNK = 64
RET_GROUP = 4
RW_WIDTH = 512
RW_HEAD = 64
RW_HEADS = 8
LORA = 64
RW_COLS = 4 * RW_WIDTH + 2 * LORA
IN_COLS = RET_COLS + RW_COLS
IN_SHARD = IN_COLS // N_CHIPS
OUT_SHARD = D_MODEL // N_CHIPS
ROPE_BASE = 10000.0
RMS_EPS = 1e-6
RET_GN_EPS = 1e-5
RW_GN_EPS = 64e-5
WKV_CHUNK = 16
WKV_GROUP = 4
N_VEC = 5

ADAM_LR = 0.001
ADAM_B1 = 0.9
ADAM_B2 = 0.999
ADAM_EPS = 1e-08
ADAM_WD = 0.01
ADAM_STEP = 10

VMEM_LIMIT = 56 * 1024 * 1024

PACK_W = 512
SMALL_NAMES = ("norm_g", "ret_gn_g", "rwkv_mu", "w0", "a0", "k_k", "k_a", "r_k", "rwkv_gn_g", "rwkv_gn_b",
               "final_norm_g")
SMALL_SIZES = (1024, 512, 2176, 512, 512, 512, 512, 512, 512, 512, 1024)
PACK_LORA_W = 0
PACK_LORA_A = LORA
PACK_SMALL = 2 * LORA


def _pack_layout():
    rows, at = {}, PACK_SMALL
    for name, n in zip(SMALL_NAMES, SMALL_SIZES):
        rows[name] = at
        at += -(-n // PACK_W)
    return rows, at


PACK_AT, PACK_LOSS = _pack_layout()
PACK_ROWS = -(-(PACK_LOSS + 1) // 8) * 8


def _cparams(**kw):
    return pltpu.CompilerParams(vmem_limit_bytes=VMEM_LIMIT, **kw)


def _split(x):
    hi = x.astype(BF16)
    lo = (x - hi.astype(F32)).astype(BF16)
    return hi, lo


def _dot_dims(a, b, dims, precision):
    if precision != X3:
        return lax.dot_general(a, b, dims, preferred_element_type=F32)
    (ah, al), (bh, bl) = _split(a), _split(b)
    dot = lambda u, w: lax.dot_general(u, w, dims, preferred_element_type=F32)
    return dot(ah, bh) + dot(ah, bl) + dot(al, bh)


def _dot(a, b, precision=None):
    return _dot_dims(a, b, (((1,), (0,)), ((), ())), precision)


def _dot_nt(a, b, precision=None):
    return _dot_dims(a, b, (((1,), (1,)), ((), ())), precision)


def _dot_tn(a, b, precision=None):
    return _dot_dims(a, b, (((0,), (0,)), ((), ())), precision)


@jax.custom_vjp
def _segsum(x, seg):
    hi, lo = _split(x)
    return _dot(hi, seg) + _dot(lo, seg)


def _segsum_fwd(x, seg):
    return _segsum(x, seg), seg


def _segsum_bwd(seg, ct):
    return _segsum(ct, seg), jnp.zeros_like(seg)


_segsum.defvjp(_segsum_fwd, _segsum_bwd)


def _softplus(z):
    return jnp.maximum(z, 0.0) + jnp.log(1.0 + jnp.exp(-jnp.abs(z)))


def _full(shape):
    nd = len(shape)
    return pl.BlockSpec(shape, lambda *_: (0,) * nd)


def _rope_tables(T):
    half = RET_DK // 2
    expo = -jnp.arange(half, dtype=F32) / jnp.float32(half)
    freqs = jnp.exp(expo * jnp.float32(np.log(ROPE_BASE)))
    ang = jnp.arange(T, dtype=jnp.int32).astype(F32)[:, None] * freqs[None, :]
    cos = jnp.tile(jnp.cos(ang), (1, 2 * RET_HEADS))
    sin = jnp.tile(jnp.sin(ang), (1, 2 * RET_HEADS))
    return cos, sin


def _ret_tables():
    H, C = RET_HEADS, RET_CHUNK
    hidx = jnp.arange(H, dtype=F32)
    lg = jnp.log(1.0 - jnp.exp2(-5.0 - hidx))
    idx = jnp.arange(C, dtype=F32)
    intra = jnp.exp(lg[:, None, None] * jnp.abs(idx[:, None] - idx[None, :]))
    q_dec = jnp.transpose(jnp.exp(lg[:, None] * (idx[None, :] + 1.0)))
    k_dec = jnp.transpose(jnp.exp(lg[:, None] * (C - 1.0 - idx[None, :])))
    chunk_dec = jnp.exp(lg * C)
    qd = jnp.repeat(q_dec, RET_DK, axis=1)
    kd = jnp.repeat(k_dec, RET_DK, axis=1)
    row_h = np.arange(RET_QK) // RET_DK
    col_h = np.arange(RET_WIDTH) // RET_DV
    bm = jnp.asarray((row_h[:, None] == col_h[None, :]).astype(np.float32))
    cd = bm * jnp.repeat(chunk_dec, RET_DK)[:, None]
    return intra, qd, kd, cd, bm


def _seg_matrix(width, head):
    h = np.arange(width) // head
    return jnp.asarray((h[:, None] == h[None, :]).astype(np.float32), dtype=BF16)


def _wkv_expand_table():
    Tc = WKV_CHUNK
    k = np.arange(2 * RW_HEADS * Tc)
    kh, kt = (k % (RW_HEADS * Tc)) // Tc, k % Tc
    nh = np.arange(RW_WIDTH) // RW_HEAD
    e = (kh[None, :, None] == nh[None, None, :]) & (kt[None, :, None] == np.arange(Tc)[:, None, None])
    return jnp.asarray(e.astype(np.float32), dtype=BF16)


def _wkv_reduce_table():
    Tc = WKV_CHUNK
    kh = np.arange(RW_WIDTH) // RW_HEAD
    n = np.arange(RW_HEADS * Tc)
    nh, nt = n // Tc, n % Tc
    r = (kh[None, :, None] == nh[None, None, :]) & (nt[None, None, :] == np.arange(Tc)[:, None, None])
    return jnp.asarray(r.astype(np.float32), dtype=BF16)


def _inproj_fwd(x, norm_g, w_b):
    T = x.shape[0]
    tm = min(T, 256)

    def body(x_ref, g_ref, w_ref, pret_ref, prw_ref, u_ref):
        xf = x_ref[...]
        rstd = lax.rsqrt(jnp.mean(xf * xf, axis=-1, keepdims=True) + RMS_EPS)
        ub = ((xf * rstd) * g_ref[...]).astype(BF16)
        u_ref[...] = ub
        pret_ref[...] = _dot(ub, w_ref[:, :RET_COLS])
        prw_ref[...] = _dot(ub, w_ref[:, RET_COLS:])

    return pl.pallas_call(
        body, name="inproj_fwd", grid=(T // tm,),
        in_specs=[pl.BlockSpec((tm, D_MODEL), lambda i: (i, 0)), _full((1, D_MODEL)), _full((D_MODEL, IN_COLS))],
        out_specs=[pl.BlockSpec((tm, RET_COLS), lambda i: (i, 0)), pl.BlockSpec((tm, RW_COLS), lambda i: (i, 0)),
                   pl.BlockSpec((tm, D_MODEL), lambda i: (i, 0))],
        out_shape=[jax.ShapeDtypeStruct((T, RET_COLS), F32), jax.ShapeDtypeStruct((T, RW_COLS), F32),
                   jax.ShapeDtypeStruct((T, D_MODEL), BF16)],
        compiler_params=_cparams(dimension_semantics=("arbitrary",)),
    )(x, norm_g, w_b)


def _rot_half(x):
    n = x.shape[1]
    lane = lax.broadcasted_iota(jnp.int32, x.shape, 1)
    first = (lane % RET_DK) < (RET_DK // 2)
    return jnp.where(first, -pltpu.roll(x, n - RET_DK // 2, 1), pltpu.roll(x, RET_DK // 2, 1))


def _rope(x, cos, sin):
    return x * cos + _rot_half(x) * sin


def _rope_bwd(d, cos, sin):
    return d * cos - _rot_half(d * sin)


def _ret_post(ret, g, gn_g, seg):
    mu = _segsum(ret, seg) * (1.0 / RET_DV)
    xc = ret - mu
    var = _segsum(xc * xc, seg) * (1.0 / RET_DV)
    n = xc * lax.rsqrt(var + RET_GN_EPS)
    return (g * jax.nn.sigmoid(g)) * (n * gn_g)


def _ret_scores(qt, kt, d_ref, h):
    lane = lax.broadcasted_iota(jnp.int32, qt.shape, 1)
    qh = jnp.where(lane // RET_DK == h, qt, 0.0)
    return qh, _dot_nt(qh, kt, X3) * d_ref[h]


def _ret_group(nch):
    return min(RET_GROUP, nch)


def _ret_fwd(p_ret, cos, sin, tabs, gn_g, seg128):
    T = p_ret.shape[0]
    C = RET_CHUNK
    nch = T // C
    G = _ret_group(nch)
    intra_d, qd, kd, cd, bm = tabs

    def body(q_ref, k_ref, v_ref, g_ref, cos_ref, sin_ref, qd_ref, kd_ref, d_ref, cd_ref, bm_ref, gn_ref, seg_ref,
             y_ref, ret_ref, sin_out_ref, s_ref):
        @pl.when(pl.program_id(0) == 0)
        def _():
            s_ref[...] = jnp.zeros_like(s_ref)

        s_in = s_ref[...]
        for i in range(G):
            rows = slice(i * C, (i + 1) * C)
            cosv, sinv = cos_ref[rows, :], sin_ref[rows, :]
            qt = _rope(q_ref[rows, :], cosv, sinv)
            kt = _rope(k_ref[rows, :], cosv, sinv) * (RET_DK ** -0.5)
            v = v_ref[rows, :]
            sin_out_ref[i] = s_in
            inter = _dot(qt * qd_ref[...], s_in, X3)
            intra = []
            for h in range(RET_HEADS):
                _, a = _ret_scores(qt, kt, d_ref, h)
                intra.append(_dot(a, v[:, h * RET_DV:(h + 1) * RET_DV], X3))
            ret_ref[rows, :] = jnp.concatenate(intra, axis=1) + inter
            kv = _dot_tn(kt * kd_ref[...], v, X3)
            s_in = s_in * cd_ref[...] + kv * bm_ref[...]
        s_ref[...] = s_in
        y_ref[...] = _ret_post(ret_ref[...], g_ref[...], gn_ref[...], seg_ref[...]).astype(BF16)

    GC = G * C
    return pl.pallas_call(
        body, name="ret_fwd", grid=(nch // G,),
        in_specs=[pl.BlockSpec((GC, RET_QK), lambda c: (c, 0)), pl.BlockSpec((GC, RET_QK), lambda c: (c, 1)),
                  pl.BlockSpec((GC, RET_WIDTH), lambda c: (c, 1)), pl.BlockSpec((GC, RET_WIDTH), lambda c: (c, 2)),
                  pl.BlockSpec((GC, RET_QK), lambda c: (c, 0)), pl.BlockSpec((GC, RET_QK), lambda c: (c, 0)),
                  _full((C, RET_QK)), _full((C, RET_QK)), _full((RET_HEADS, C, C)),
                  _full((RET_QK, RET_WIDTH)), _full((RET_QK, RET_WIDTH)), _full((1, RET_WIDTH)),
                  _full((RET_WIDTH, RET_WIDTH))],
        out_specs=[pl.BlockSpec((GC, RET_WIDTH), lambda c: (c, 0)), pl.BlockSpec((GC, RET_WIDTH), lambda c: (c, 0)),
                   pl.BlockSpec((G, RET_QK, RET_WIDTH), lambda c: (c, 0, 0))],
        out_shape=[jax.ShapeDtypeStruct((T, RET_WIDTH), BF16), jax.ShapeDtypeStruct((T, RET_WIDTH), F32),
                   jax.ShapeDtypeStruct((nch, RET_QK, RET_WIDTH), F32)],
        scratch_shapes=[pltpu.VMEM((RET_QK, RET_WIDTH), F32)],
        compiler_params=_cparams(dimension_semantics=("arbitrary",)),
    )(p_ret, p_ret, p_ret, p_ret, cos, sin, qd, kd, intra_d, cd, bm, gn_g, seg128)


def _ret_bwd(p_ret, cos, sin, tabs, gn_g, seg128, ret, s_in_all, dy):
    T = p_ret.shape[0]
    C = RET_CHUNK
    nch = T // C
    G = _ret_group(nch)
    ngr = nch // G
    intra_d, qd, kd, cd, bm = tabs

    def rev(j):
        return lambda c: (ngr - 1 - c, j)

    def body(q_ref, k_ref, v_ref, g_ref, cos_ref, sin_ref, qd_ref, kd_ref, d_ref, cd_ref, bm_ref, gn_ref, seg_ref,
             ret_ref, sin_ref_, dy_ref, dp_ref, dgn_ref, ds_ref):
        @pl.when(pl.program_id(0) == 0)
        def _():
            ds_ref[...] = jnp.zeros_like(ds_ref)
            dgn_ref[...] = jnp.zeros_like(dgn_ref)

        seg = seg_ref[...]
        _, post_vjp = jax.vjp(lambda r_, g_, gn_: _ret_post(r_, g_, gn_, seg), ret_ref[...], g_ref[...], gn_ref[...])
        dret_all, dg_all, dgn = post_vjp(dy_ref[...])
        dgn_ref[...] += dgn
        dp_ref[:, 2 * RET_QK + RET_WIDTH:] = dg_all

        qdv, kdv = qd_ref[...], kd_ref[...]
        ds_out = ds_ref[...]
        for i in reversed(range(G)):
            rows = slice(i * C, (i + 1) * C)
            cosv, sinv = cos_ref[rows, :], sin_ref[rows, :]
            qt = _rope(q_ref[rows, :], cosv, sinv)
            kt = _rope(k_ref[rows, :], cosv, sinv) * (RET_DK ** -0.5)
            v = v_ref[rows, :]
            s_in = sin_ref_[i]
            dret = dret_all[rows, :]
            dqt = qdv * _dot_nt(dret, s_in, X3)
            dkt = kdv * _dot_nt(v, ds_out, X3)
            dv_all = _dot(kt * kdv, ds_out, X3)
            dvs = []
            for h in range(RET_HEADS):
                sl = slice(h * RET_DV, (h + 1) * RET_DV)
                qh, a = _ret_scores(qt, kt, d_ref, h)
                lane = lax.broadcasted_iota(jnp.int32, kt.shape, 1)
                kh = jnp.where(lane // RET_DK == h, kt, 0.0)
                da = _dot_nt(dret[:, sl], v[:, sl], X3) * d_ref[h]
                dvs.append(_dot_tn(a, dret[:, sl], X3))
                dqt = dqt + _dot(da, kh, X3)
                dkt = dkt + _dot_tn(da, qh, X3)
            ds_out = ds_out * cd_ref[...] + _dot_tn(qt * qdv, dret, X3) * bm_ref[...]
            dp_ref[rows, :RET_QK] = _rope_bwd(dqt, cosv, sinv)
            dp_ref[rows, RET_QK:2 * RET_QK] = _rope_bwd(dkt * (RET_DK ** -0.5), cosv, sinv)
            dp_ref[rows, 2 * RET_QK:2 * RET_QK + RET_WIDTH] = dv_all + jnp.concatenate(dvs, axis=1)
        ds_ref[...] = ds_out

    GC = G * C
    return pl.pallas_call(
        body, name="ret_bwd", grid=(ngr,),
        in_specs=[pl.BlockSpec((GC, RET_QK), rev(0)), pl.BlockSpec((GC, RET_QK), rev(1)),
                  pl.BlockSpec((GC, RET_WIDTH), rev(1)), pl.BlockSpec((GC, RET_WIDTH), rev(2)),
                  pl.BlockSpec((GC, RET_QK), rev(0)), pl.BlockSpec((GC, RET_QK), rev(0)),
                  _full((C, RET_QK)), _full((C, RET_QK)), _full((RET_HEADS, C, C)),
                  _full((RET_QK, RET_WIDTH)), _full((RET_QK, RET_WIDTH)), _full((1, RET_WIDTH)),
                  _full((RET_WIDTH, RET_WIDTH)),
                  pl.BlockSpec((GC, RET_WIDTH), rev(0)),
                  pl.BlockSpec((G, RET_QK, RET_WIDTH), lambda c: (ngr - 1 - c, 0, 0)),
                  pl.BlockSpec((GC, RET_WIDTH), rev(0))],
        out_specs=[pl.BlockSpec((GC, RET_COLS), rev(0)), _full((1, RET_WIDTH))],
        out_shape=[jax.ShapeDtypeStruct((T, RET_COLS), F32), jax.ShapeDtypeStruct((1, RET_WIDTH), F32)],
        scratch_shapes=[pltpu.VMEM((RET_QK, RET_WIDTH), F32)],
        compiler_params=_cparams(dimension_semantics=("arbitrary",)),
    )(p_ret, p_ret, p_ret, p_ret, cos, sin, qd, kd, intra_d, cd, bm, gn_g, seg128, ret, s_in_all, dy)


@jax.custom_vjp
def _chunk_sums(x, tri):
    hi, lo = _split(x)
    return _dot(tri, hi) + _dot(tri, lo)


def _chunk_sums_fwd(x, tri):
    return _chunk_sums(x, tri), tri


def _chunk_sums_bwd(tri, ct):
    hi, lo = _split(ct)
    return _dot_tn(tri, hi) + _dot_tn(tri, lo), jnp.zeros_like(tri)


_chunk_sums.defvjp(_chunk_sums_fwd, _chunk_sums_bwd)


@jax.custom_vjp
def _lora_dot(z, lora):
    return _dot(z, lora, X3)


def _lora_dot_fwd(z, lora):
    return _lora_dot(z, lora), (z, lora)


def _lora_dot_bwd(res, ct):
    z, lora = res
    return _dot_nt(ct, lora, X3), _dot_tn(z, ct, X3)


_lora_dot.defvjp(_lora_dot_fwd, _lora_dot_bwd)


def _chunk_tables(tm):
    t = np.arange(tm)
    same = (t[:, None] // WKV_CHUNK) == (t[None, :] // WKV_CHUNK)
    return jnp.asarray(np.stack([same & (t[None, :] <= t[:, None]), same]).astype(np.float32), dtype=BF16)


def _prep_fn(p, prev, mu, w0, a0, k_k, k_a, lora, seg, tri):
    W = RW_WIDTH
    ps = p + mu * (prev - p)
    r, kr, vr, g = ps[:, 0:W], ps[:, W:2 * W], ps[:, 2 * W:3 * W], ps[:, 3 * W:4 * W]
    z = ps[:, 4 * W:]
    lane = lax.broadcasted_iota(jnp.int32, z.shape, 1)
    z = jnp.where(lane < LORA, jnp.tanh(z), z)
    lo = _lora_dot(z, lora)
    w_log = -_softplus(-(w0 + lo[:, :W])) - 0.5
    log_decay = -jnp.exp(w_log)
    cum = _chunk_sums(log_decay, tri[0])
    total = _chunk_sums(log_decay, tri[1])
    a = jax.nn.sigmoid(a0 + lo[:, W:])
    kk = kr * k_k
    kk = kk / jnp.maximum(jnp.sqrt(_segsum(kk * kk, seg)), 1e-12)
    k = kr * (1.0 + (a - 1.0) * k_a)
    grow = jnp.exp(-cum)
    return kk * jnp.exp(cum - log_decay), (kk * a) * grow, k * grow, r * jnp.exp(cum), jnp.exp(total), vr, g, r * k


def _post_fn(o, rk, v, g, gn_g, gn_b, r_k, seg):
    mu = _segsum(o, seg) * (1.0 / RW_HEAD)
    oc = o - mu
    var = _segsum(oc * oc, seg) * (1.0 / RW_HEAD)
    on = oc * lax.rsqrt(var + RW_GN_EPS) * gn_g + gn_b
    bonus = _segsum(rk * r_k, seg) * v
    return (g * jax.nn.sigmoid(g)) * (on + bonus)


N_PAIR = (N_VEC + 1) // 2
HALF_LANES = 64


def _swap_halves(x):
    return pltpu.roll(x, HALF_LANES, 1)


def _pack_heads(vecs):
    tm = vecs[0].shape[0]
    low = lax.broadcasted_iota(jnp.int32, (tm, 128), 1) < HALF_LANES
    out = []
    for p in range(N_PAIR):
        a = vecs[2 * p]
        b = vecs[2 * p + 1] if 2 * p + 1 < len(vecs) else None
        heads = []
        for m in range(RW_WIDTH // 128):
            am = a[:, m * 128:(m + 1) * 128]
            bm = jnp.zeros_like(am) if b is None else b[:, m * 128:(m + 1) * 128]
            heads.append(jnp.where(low, am, _swap_halves(bm)))
            heads.append(jnp.where(low, _swap_halves(am), bm))
        out.append(heads)
    return out


def _unpack_heads(hm_ref):
    tm = hm_ref.shape[2]
    low = lax.broadcasted_iota(jnp.int32, (tm, 128), 1) < HALF_LANES
    vecs = []
    for p in range(N_PAIR):
        a, b = [], []
        for m in range(RW_WIDTH // 128):
            even, odd = hm_ref[p, 2 * m], hm_ref[p, 2 * m + 1]
            a.append(jnp.where(low, even, _swap_halves(odd)))
            b.append(jnp.where(low, _swap_halves(even), odd))
        vecs += [jnp.concatenate(a, axis=1), jnp.concatenate(b, axis=1)]
    return vecs[:N_VEC]


def _shift_down(p, first_row):
    row = lax.broadcasted_iota(jnp.int32, p.shape, 0)
    return jnp.where(row == 0, first_row, pltpu.roll(p, 1, 0))


def _shift_up(p, last_row):
    n = p.shape[0]
    row = lax.broadcasted_iota(jnp.int32, p.shape, 0)
    return jnp.where(row == n - 1, last_row, pltpu.roll(p, n - 1, 0))


def _row_tile(T):
    return min(T, 256)


def _prep_fwd(p_rw, bnd, mu, w0, a0, k_k, k_a, lora, seg64, tri):
    T = p_rw.shape[0]
    tm = _row_tile(T)
    W = RW_WIDTH

    def body(p_ref, bnd_ref, mu_ref, w0_ref, a0_ref, kk_ref, ka_ref, lora_ref, seg_ref, tri_ref,
             hm_ref, v_ref, g_ref, rk_ref):
        p = p_ref[...]
        prev = _shift_down(p, bnd_ref[0])
        res = _prep_fn(p, prev, mu_ref[...], w0_ref[...], a0_ref[...], kk_ref[...], ka_ref[...], lora_ref[...],
                       seg_ref[...], (tri_ref[0], tri_ref[1]))
        for pair, heads in enumerate(_pack_heads(res[:N_VEC])):
            for h, val in enumerate(heads):
                hm_ref[pair, h] = val
        v_ref[...] = res[N_VEC]
        g_ref[...] = res[N_VEC + 1]
        rk_ref[...] = res[N_VEC + 2]

    small = _full((1, W))
    row = pl.BlockSpec((tm, W), lambda i: (i, 0))
    return pl.pallas_call(
        body, name="rwkv_prep_fwd", grid=(T // tm,),
        in_specs=[pl.BlockSpec((tm, RW_COLS), lambda i: (i, 0)), pl.BlockSpec((1, 1, RW_COLS), lambda i: (i, 0, 0)),
                  _full((1, RW_COLS)), small, small, small, small, _full((2 * LORA, 2 * W)), _full((W, W)),
                  _full((2, tm, tm))],
        out_specs=[pl.BlockSpec((N_PAIR, RW_HEADS, tm, 128), lambda i: (0, 0, i, 0)), row, row, row],
        out_shape=[jax.ShapeDtypeStruct((N_PAIR, RW_HEADS, T, 128), F32)] + [jax.ShapeDtypeStruct((T, W), F32)] * 3,
        compiler_params=_cparams(dimension_semantics=("arbitrary",)),
    )(p_rw, bnd, mu, w0, a0, k_k, k_a, lora, seg64, tri)


def _prep_bwd(p_rw, bnd, mu, w0, a0, k_k, k_a, lora, seg64, tri, cts):
    T = p_rw.shape[0]
    tm = _row_tile(T)
    W = RW_WIDTH

    def body(p_ref, bnd_ref, mu_ref, w0_ref, a0_ref, kk_ref, ka_ref, lora_ref, seg_ref, tri_ref,
             dhm_ref, drk_ref, dv1_ref, dv2_ref, dg_ref,
             dp_ref, dprev_ref, dmu_ref, dw0_ref, da0_ref, dkk_p_ref, dka_ref, dlora_ref):
        accs = (dmu_ref, dw0_ref, da0_ref, dkk_p_ref, dka_ref, dlora_ref)

        @pl.when(pl.program_id(0) == 0)
        def _():
            for a_ref in accs:
                a_ref[...] = jnp.zeros_like(a_ref)

        p = p_ref[...]
        prev = _shift_down(p, bnd_ref[0])
        seg, tri = seg_ref[...], (tri_ref[0], tri_ref[1])
        _, vjp = jax.vjp(lambda *a: _prep_fn(*a, seg, tri), p, prev, mu_ref[...], w0_ref[...], a0_ref[...],
                         kk_ref[...], ka_ref[...], lora_ref[...])
        ct = (*_unpack_heads(dhm_ref), dv1_ref[...] + dv2_ref[...], dg_ref[...], drk_ref[...])
        grads = vjp(ct)
        dp_ref[...] = grads[0]
        dprev_ref[...] = grads[1]
        for a_ref, gval in zip(accs, grads[2:]):
            a_ref[...] += gval

    small = _full((1, W))
    row = pl.BlockSpec((tm, W), lambda i: (i, 0))
    return pl.pallas_call(
        body, name="rwkv_prep_bwd", grid=(T // tm,),
        in_specs=[pl.BlockSpec((tm, RW_COLS), lambda i: (i, 0)), pl.BlockSpec((1, 1, RW_COLS), lambda i: (i, 0, 0)),
                  _full((1, RW_COLS)), small, small, small, small, _full((2 * LORA, 2 * W)), _full((W, W)),
                  _full((2, tm, tm)), pl.BlockSpec((N_PAIR, RW_HEADS, tm, 128), lambda i: (0, 0, i, 0))] + [row] * 4,
        out_specs=[pl.BlockSpec((tm, RW_COLS), lambda i: (i, 0)), pl.BlockSpec((tm, RW_COLS), lambda i: (i, 0)),
                   _full((1, RW_COLS)), small, small, small, small, _full((2 * LORA, 2 * W))],
        out_shape=[jax.ShapeDtypeStruct((T, RW_COLS), F32), jax.ShapeDtypeStruct((T, RW_COLS), F32),
                   jax.ShapeDtypeStruct((1, RW_COLS), F32)] + [jax.ShapeDtypeStruct((1, W), F32)] * 4
                  + [jax.ShapeDtypeStruct((2 * LORA, 2 * W), F32)],
        compiler_params=_cparams(dimension_semantics=("arbitrary",)),
    )(p_rw, bnd, mu, w0, a0, k_k, k_a, lora, seg64, tri, *cts)


def _post_fwd(o, rk, v, g, gn_g, gn_b, r_k, seg64):
    T = o.shape[0]
    tm = _row_tile(T)
    W = RW_WIDTH

    def body(o_ref, rk_ref, v_ref, g_ref, gg_ref, gb_ref, rkp_ref, seg_ref, y_ref):
        y_ref[...] = _post_fn(o_ref[...], rk_ref[...], v_ref[...], g_ref[...], gg_ref[...], gb_ref[...],
                              rkp_ref[...], seg_ref[...]).astype(BF16)

    row = pl.BlockSpec((tm, W), lambda i: (i, 0))
    small = _full((1, W))
    return pl.pallas_call(
        body, name="rwkv_post_fwd", grid=(T // tm,),
        in_specs=[row] * 4 + [small] * 3 + [_full((W, W))],
        out_specs=row, out_shape=jax.ShapeDtypeStruct((T, W), BF16),
        compiler_params=_cparams(dimension_semantics=("arbitrary",)),
    )(o, rk, v, g, gn_g, gn_b, r_k, seg64)


def _post_bwd(o, rk, v, g, gn_g, gn_b, r_k, seg64, dy):
    T = o.shape[0]
    tm = _row_tile(T)
    W = RW_WIDTH

    def body(o_ref, rk_ref, v_ref, g_ref, gg_ref, gb_ref, rkp_ref, seg_ref, dy_ref,
             do_ref, drk_ref, dv_ref, dg_ref, dgg_ref, dgb_ref, drkp_ref):
        accs = (dgg_ref, dgb_ref, drkp_ref)

        @pl.when(pl.program_id(0) == 0)
        def _():
            for a_ref in accs:
                a_ref[...] = jnp.zeros_like(a_ref)

        seg = seg_ref[...]
        _, vjp = jax.vjp(lambda *a: _post_fn(*a, seg), o_ref[...], rk_ref[...], v_ref[...], g_ref[...],
                         gg_ref[...], gb_ref[...], rkp_ref[...])
        grads = vjp(dy_ref[...])
        for o_, gval in zip((do_ref, drk_ref, dv_ref, dg_ref), grads[:4]):
            o_[...] = gval
        for a_ref, gval in zip(accs, grads[4:]):
            a_ref[...] += gval

    row = pl.BlockSpec((tm, W), lambda i: (i, 0))
    small = _full((1, W))
    return pl.pallas_call(
        body, name="rwkv_post_bwd", grid=(T // tm,),
        in_specs=[row] * 4 + [small] * 3 + [_full((W, W)), pl.BlockSpec((tm, W), lambda i: (i, 1))],
        out_specs=[row] * 4 + [small] * 3,
        out_shape=[jax.ShapeDtypeStruct((T, W), F32)] * 4 + [jax.ShapeDtypeStruct((1, W), F32)] * 3,
        compiler_params=_cparams(dimension_semantics=("arbitrary",)),
    )(o, rk, v, g, gn_g, gn_b, r_k, seg64, dy)


def _wkv_lhs(hm_ref, rows):
    tiles = [jnp.transpose(hm_ref[p, :, rows, :].reshape(RW_HEADS * WKV_CHUNK, 128)) for p in range(N_PAIR)]
    hi, lo = _split(jnp.concatenate(tiles, axis=0)[:N_VEC * RW_HEAD])
    return jnp.concatenate([hi, lo], axis=1)


def _wkv_group(nch):
    return min(WKV_GROUP, nch)


N_STEP_VEC = N_VEC - 1
PAD_ROWS = 16


def _wkv_fwd(cols, v, e_tab):
    T = v.shape[0]
    Tc = WKV_CHUNK
    nch = T // Tc
    G = _wkv_group(nch)
    J, W = RW_HEAD, RW_WIDTH
    JS = N_STEP_VEC * J

    def body(cols_ref, v_ref, e_ref, o_ref, states_ref, sa_ref, s_ref):
        @pl.when(pl.program_id(0) == 0)
        def _():
            s_ref[...] = jnp.zeros_like(s_ref)

        st = s_ref[...]
        for c in range(G):
            lhs = _wkv_lhs(cols_ref, slice(c * Tc, (c + 1) * Tc))
            for t in range(Tc):
                row = slice(c * Tc + t, c * Tc + t + 1)
                ex = _dot(lhs[:JS], e_ref[t])
                states_ref[c * Tc + t] = st
                sa = -jnp.sum(st * ex[0:J], axis=0, keepdims=True)
                st = st + ex[J:2 * J] * sa + ex[2 * J:3 * J] * v_ref[row, :]
                sa_ref[row, :] = sa
                o_ref[row, :] = jnp.sum(st * ex[3 * J:4 * J], axis=0, keepdims=True)
            st = st * _dot(lhs[JS:], e_ref[Tc - 1])
        s_ref[...] = st

    GT = G * Tc
    return pl.pallas_call(
        body, name="wkv_fwd", grid=(nch // G,),
        in_specs=[pl.BlockSpec((N_PAIR, RW_HEADS, GT, 128), lambda c: (0, 0, c, 0)),
                  pl.BlockSpec((GT, W), lambda c: (c, 0)), _full((Tc, 2 * 128, W))],
        out_specs=[pl.BlockSpec((GT, W), lambda c: (c, 0)), pl.BlockSpec((GT, J, W), lambda c: (c, 0, 0)),
                   pl.BlockSpec((GT, W), lambda c: (c, 0))],
        out_shape=[jax.ShapeDtypeStruct((T, W), F32), jax.ShapeDtypeStruct((T, J, W), F32),
                   jax.ShapeDtypeStruct((T, W), F32)],
        scratch_shapes=[pltpu.VMEM((J, W), F32)],
        compiler_params=_cparams(dimension_semantics=("arbitrary",)),
    )(cols, v, e_tab)


def _wkv_bwd(cols, v, do, states, sa, e_tab, r_tab):
    T = v.shape[0]
    Tc = WKV_CHUNK
    nch = T // Tc
    G = _wkv_group(nch)
    ngr = nch // G
    J, W = RW_HEAD, RW_WIDTH
    JS = N_STEP_VEC * J
    blocks = [slice(b * 128, (b + 1) * 128) for b in range(W // 128)]

    def body(cols_ref, v_ref, do_ref, states_ref, sa_ref, e_ref, r_ref, dv_ref, dhm_ref, ds_ref):
        @pl.when(pl.program_id(0) == 0)
        def _():
            ds_ref[...] = jnp.zeros_like(ds_ref)

        d_carry = [ds_ref[:, b] for b in blocks]
        last = Tc - 1
        for c in reversed(range(G)):
            at = c * Tc
            rows = slice(at, at + Tc)
            lhs = _wkv_lhs(cols_ref, rows)
            ex = _dot(lhs, e_ref[last])
            dst, ends = [], []
            for i, b in enumerate(blocks):
                s_end = (states_ref[at + last, :, b] + ex[J:2 * J, b] * sa_ref[at + last:at + Tc, b]
                         + ex[2 * J:3 * J, b] * v_ref[at + last:at + Tc, b])
                ends.append((d_carry[i] * s_end).astype(BF16))
                dst.append(d_carry[i] * ex[JS:, b])
            d_decay = _dot(jnp.concatenate(ends, axis=1), r_ref[last])
            acc = jnp.zeros((JS + PAD_ROWS, 128), F32)
            for t in reversed(range(Tc)):
                row = slice(at + t, at + t + 1)
                if t != last:
                    ex = _dot(lhs[:JS], e_ref[t])
                dvs, prods = [], []
                for i, b in enumerate(blocks):
                    kk_e, b_e, k_e, r_e = (ex[n * J:(n + 1) * J, b] for n in range(N_STEP_VEC))
                    do_row, v_row, sa_row = do_ref[row, b], v_ref[row, b], sa_ref[row, b]
                    s_old = states_ref[at + t, :, b]
                    s_new = states_ref[at + t + 1, :, b] if t != last else s_old + b_e * sa_row + k_e * v_row
                    dsn = dst[i] + r_e * do_row
                    dsa = jnp.sum(dsn * b_e, axis=0, keepdims=True)
                    dvs.append(jnp.sum(dsn * k_e, axis=0, keepdims=True))
                    prods.append(jnp.concatenate(
                        [s_old * (-dsa), dsn * sa_row, dsn * v_row, s_new * do_row, jnp.zeros((PAD_ROWS, 128), F32)],
                        axis=0).astype(BF16))
                    dst[i] = dsn - kk_e * dsa
                dv_ref[row, :] = jnp.concatenate(dvs, axis=1)
                acc = acc + _dot(jnp.concatenate(prods, axis=1), r_ref[t])
            d_carry = dst
            tiles = jnp.concatenate([acc[:JS], d_decay, jnp.zeros((2 * N_PAIR * J - N_VEC * J, 128), F32)], axis=0)
            for p in range(N_PAIR):
                dhm_ref[p, :, rows, :] = jnp.transpose(tiles[p * 128:(p + 1) * 128]).reshape(RW_HEADS, Tc, 128)
        for i, b in enumerate(blocks):
            ds_ref[:, b] = d_carry[i]

    GT = G * Tc
    rev2 = lambda c: (ngr - 1 - c, 0)
    rev3 = lambda c: (ngr - 1 - c, 0, 0)
    rev_hm = lambda c: (0, 0, ngr - 1 - c, 0)
    hm_spec = pl.BlockSpec((N_PAIR, RW_HEADS, GT, 128), rev_hm)
    return pl.pallas_call(
        body, name="wkv_bwd", grid=(ngr,),
        in_specs=[hm_spec, pl.BlockSpec((GT, W), rev2), pl.BlockSpec((GT, W), rev2),
                  pl.BlockSpec((GT, J, W), rev3), pl.BlockSpec((GT, W), rev2),
                  _full((Tc, 2 * 128, W)), _full((Tc, W, 128))],
        out_specs=[pl.BlockSpec((GT, W), rev2), hm_spec],
        out_shape=[jax.ShapeDtypeStruct((T, W), F32), jax.ShapeDtypeStruct((N_PAIR, RW_HEADS, T, 128), F32)],
        scratch_shapes=[pltpu.VMEM((J, W), F32)],
        compiler_params=_cparams(dimension_semantics=("arbitrary",)),
    )(cols, v, do, states, sa, e_tab, r_tab)


def _outproj(x, y_ret, y_rw, w_out_b, target, gf):
    T = x.shape[0]
    tm = _row_tile(T)
    W = RW_WIDTH

    def body(x_ref, yr_ref, yw_ref, w_ref, t_ref, gf_ref, loss_ref, dh_ref, dy_ref, dw_ref, dgf_ref):
        @pl.when(pl.program_id(0) == 0)
        def _():
            loss_ref[...] = jnp.zeros_like(loss_ref)
            dw_ref[...] = jnp.zeros_like(dw_ref)
            dgf_ref[...] = jnp.zeros_like(dgf_ref)

        y = jnp.concatenate([yr_ref[...], yw_ref[...]], axis=1)
        w = w_ref[...]
        h = x_ref[...] + _dot(y, w)
        rstd = lax.rsqrt(jnp.mean(h * h, axis=-1, keepdims=True) + RMS_EPS)
        hn = h * rstd
        gfv = gf_ref[...]
        err = hn * gfv - t_ref[...]
        loss_ref[...] += 0.5 * jnp.sum(jnp.mean(err * err, axis=-1))
        dout = err * (1.0 / D_MODEL)
        dgf_ref[...] += jnp.sum(dout * hn, axis=0, keepdims=True)
        dhn = dout * gfv
        dh = rstd * (dhn - hn * jnp.mean(dhn * hn, axis=-1, keepdims=True))
        dh_ref[...] = dh
        dhb = dh.astype(BF16)
        dy_ref[...] = _dot_nt(dhb, w)
        dw_ref[...] += _dot_tn(y, dhb)

    return pl.pallas_call(
        body, name="outproj_loss", grid=(T // tm,),
        in_specs=[pl.BlockSpec((tm, D_MODEL), lambda i: (i, 0)), pl.BlockSpec((tm, W), lambda i: (i, 0)),
                  pl.BlockSpec((tm, W), lambda i: (i, 0)), _full((D_MODEL, D_MODEL)),
                  pl.BlockSpec((tm, D_MODEL), lambda i: (i, 0)), _full((1, D_MODEL))],
        out_specs=[_full((1, PACK_W)), pl.BlockSpec((tm, D_MODEL), lambda i: (i, 0)),
                   pl.BlockSpec((tm, D_MODEL), lambda i: (i, 0)), _full((D_MODEL, D_MODEL)), _full((1, D_MODEL))],
        out_shape=[jax.ShapeDtypeStruct((1, PACK_W), F32), jax.ShapeDtypeStruct((T, D_MODEL), F32),
                   jax.ShapeDtypeStruct((T, D_MODEL), F32), jax.ShapeDtypeStruct((D_MODEL, D_MODEL), F32),
                   jax.ShapeDtypeStruct((1, D_MODEL), F32)],
        compiler_params=_cparams(dimension_semantics=("arbitrary",)),
    )(x, y_ret, y_rw, w_out_b, target, gf)


def _inproj_bwd_x(dp_ret, dp_rw, dprev, dbnd, w_b, x, norm_g, dh):
    T = x.shape[0]
    tm = _row_tile(T)

    def body(dpr_ref, dpw_ref, dprev_ref, dbnd_ref, w_ref, x_ref, g_ref, dh_ref, gx_ref, dg_ref, dpb_ref):
        @pl.when(pl.program_id(0) == 0)
        def _():
            dg_ref[...] = jnp.zeros_like(dg_ref)

        d_rw = dpw_ref[...] + _shift_up(dprev_ref[...], dbnd_ref[0])
        dpb = jnp.concatenate([dpr_ref[...].astype(BF16), d_rw.astype(BF16)], axis=1)
        dpb_ref[...] = dpb
        du = _dot_nt(dpb, w_ref[...])
        xf = x_ref[...]
        rstd = lax.rsqrt(jnp.mean(xf * xf, axis=-1, keepdims=True) + RMS_EPS)
        xn = xf * rstd
        dg_ref[...] += jnp.sum(du * xn, axis=0, keepdims=True)
        dxn = du * g_ref[...]
        gx_ref[...] = dh_ref[...] + rstd * (dxn - xn * jnp.mean(dxn * xn, axis=-1, keepdims=True))

    return pl.pallas_call(
        body, name="inproj_bwd_x", grid=(T // tm,),
        in_specs=[pl.BlockSpec((tm, RET_COLS), lambda i: (i, 0)), pl.BlockSpec((tm, RW_COLS), lambda i: (i, 0)),
                  pl.BlockSpec((tm, RW_COLS), lambda i: (i, 0)), pl.BlockSpec((1, 1, RW_COLS), lambda i: (i, 0, 0)),
                  _full((D_MODEL, IN_COLS)), pl.BlockSpec((tm, D_MODEL), lambda i: (i, 0)), _full((1, D_MODEL)),
                  pl.BlockSpec((tm, D_MODEL), lambda i: (i, 0))],
        out_specs=[pl.BlockSpec((tm, D_MODEL), lambda i: (i, 0)), _full((1, D_MODEL)),
                   pl.BlockSpec((tm, IN_COLS), lambda i: (i, 0))],
        out_shape=[jax.ShapeDtypeStruct((T, D_MODEL), F32), jax.ShapeDtypeStruct((1, D_MODEL), F32),
                   jax.ShapeDtypeStruct((T, IN_COLS), BF16)],
        compiler_params=_cparams(dimension_semantics=("arbitrary",)),
    )(dp_ret, dp_rw, dprev, dbnd, w_b, x, norm_g, dh)


def _inproj_bwd_w(u_t, dpb):
    T = u_t.shape[1]
    tr = 256

    def body(u_ref, d_ref, o_ref):
        o_ref[...] = _dot(u_ref[...], d_ref[...])

    return pl.pallas_call(
        body, name="inproj_bwd_w", grid=(D_MODEL // tr,),
        in_specs=[pl.BlockSpec((tr, T), lambda i: (i, 0)), _full((T, IN_COLS))],
        out_specs=pl.BlockSpec((tr, IN_COLS), lambda i: (i, 0)),
        out_shape=jax.ShapeDtypeStruct((D_MODEL, IN_COLS), F32),
        compiler_params=_cparams(dimension_semantics=("arbitrary",)),
    )(u_t, dpb)


def _tile_boundaries(a, tm, first):
    T, n = a.shape
    zero = jnp.zeros((1, n), a.dtype)
    if first:
        rows = jnp.concatenate([zero, a[tm - 1:T - 1:tm]], axis=0)
    else:
        rows = jnp.concatenate([a[tm:T:tm], zero], axis=0)
    return rows.reshape(T // tm, 1, n)


def _local_step(x, target, w_in_b, w_out_b, lora, small):
    T = x.shape[0]
    tm = _row_tile(T)
    cos, sin = _rope_tables(T)
    tabs = _ret_tables()
    seg128 = _seg_matrix(RET_WIDTH, RET_DV)
    seg64 = _seg_matrix(RW_WIDTH, RW_HEAD)
    e_tab = _wkv_expand_table()
    r_tab = _wkv_reduce_table()
    prep_w = (small["rwkv_mu"], small["w0"], small["a0"], small["k_k"], small["k_a"], lora, seg64, _chunk_tables(tm))
    post_w = (small["rwkv_gn_g"], small["rwkv_gn_b"], small["r_k"], seg64)

    p_ret, p_rw, u = _inproj_fwd(x, small["norm_g"], w_in_b)
    y_ret, ret, s_in_all = _ret_fwd(p_ret, cos, sin, tabs, small["ret_gn_g"], seg128)
    bnd = _tile_boundaries(p_rw, tm, True)
    hm, v, g, rk = _prep_fwd(p_rw, bnd, *prep_w)
    o, states, sa = _wkv_fwd(hm, v, e_tab)
    y_rw = _post_fwd(o, rk, v, g, *post_w)
    loss, dh, dy, d_w_out, d_gf = _outproj(x, y_ret, y_rw, w_out_b, target, small["final_norm_g"])

    do, d_rk, dv2, dg, d_gn_g, d_gn_b, d_r_k = _post_bwd(o, rk, v, g, *post_w, dy)
    dv1, d_hm = _wkv_bwd(hm, v, do, states, sa, e_tab, r_tab)
    dp_rw, dprev, d_mu, d_w0, d_a0, d_k_k, d_k_a, d_lora = _prep_bwd(
        p_rw, bnd, *prep_w, (d_hm, d_rk, dv1, dv2, dg))
    dp_ret, d_ret_gn = _ret_bwd(p_ret, cos, sin, tabs, small["ret_gn_g"], seg128, ret, s_in_all, dy)
    dbnd = _tile_boundaries(dprev, tm, False)
    grad_x, d_norm_g, dpb = _inproj_bwd_x(dp_ret, dp_rw, dprev, dbnd, w_in_b, x, small["norm_g"], dh)
    d_w_in = _inproj_bwd_w(jnp.transpose(u), dpb)

    d_small = {"norm_g": d_norm_g, "ret_gn_g": d_ret_gn, "rwkv_mu": d_mu, "w0": d_w0, "a0": d_a0, "k_k": d_k_k,
               "k_a": d_k_a, "r_k": d_r_k, "rwkv_gn_g": d_gn_g, "rwkv_gn_b": d_gn_b, "final_norm_g": d_gf}
    return loss, grad_x, d_w_in, d_w_out, d_lora, d_small


ANY = pl.BlockSpec(memory_space=pl.ANY)
CHIP_FLIPS = ((0, 1), (1, 0), (1, 1))
N_FLIPS = len(CHIP_FLIPS)
LORA_SHARD = RW_WIDTH // N_CHIPS
HALF_IN = D_MODEL // 2
HALF_OUT = OUT_SHARD // 2


def _position():
    return lax.axis_index("x"), lax.axis_index("y"), lax.axis_index("c")


def _flip(v, f):
    return 1 - v if f else v


def _finish(local, remote, landed):
    for cp in landed:
        cp.wait_recv()
    for cp in remote:
        cp.wait_send()
    for cp in local:
        cp.wait()


def _gather_chips(arrs):
    n = len(arrs)

    def body(*refs):
        ins, outs = refs[:n], refs[n:2 * n]
        send, recv, pass_send, pass_recv = refs[2 * n:]
        x, y, c = _position()
        s = 2 * x + y
        sibling = (x, y, 1 - c)

        def copy(src, dst, sems, k, to):
            return pltpu.make_async_remote_copy(src_ref=src, dst_ref=dst, send_sem=sems[0].at[k], recv_sem=sems[1].at[k],
                                                device_id=to, device_id_type=MESH)

        remote, landed, passed, passed_in = [], [], [], []
        for a in range(n):
            for j, (fx, fy) in enumerate(CHIP_FLIPS):
                px, py = _flip(x, fx), _flip(y, fy)
                ps = 2 * px + py
                k = a * N_FLIPS + j
                remote.append(copy(ins[a].at[c], outs[a].at[s, c], (send, recv), k, (px, py, c)))
                landed.append(copy(ins[a].at[c], outs[a].at[ps, c], (send, recv), k, (px, py, c)))
                passed.append(copy(outs[a].at[ps, c], outs[a].at[ps, c], (pass_send, pass_recv), k, sibling))
                passed_in.append(copy(outs[a].at[ps, 1 - c], outs[a].at[ps, 1 - c], (pass_send, pass_recv), k, sibling))
        for cp in remote:
            cp.start()
        for arrived, onward in zip(landed, passed):
            arrived.wait_recv()
            onward.start()
        _finish([], remote + passed, passed_in)

    sems = pltpu.SemaphoreType.DMA((n * N_FLIPS,))
    return pl.pallas_call(
        body, name="gather_weights",
        in_specs=[ANY] * n, out_specs=[ANY] * n,
        out_shape=[jax.ShapeDtypeStruct((N_CHIPS,) + a.shape, a.dtype) for a in arrs],
        scratch_shapes=[sems, sems, sems, sems],
    )(*arrs)


def _pair_exchange(g_in, g_out, g_small):
    def body(gi_ref, go_ref, gs_ref, li_ref, lo_ref, ls_ref, send, recv):
        x, y, c = _position()
        peer = (x, y, 1 - c)
        srcs = (gi_ref.at[:, pl.ds((1 - c) * HALF_IN, HALF_IN), :], go_ref.at[:, pl.ds((1 - c) * HALF_OUT, HALF_OUT), :],
                gs_ref)
        remote = [pltpu.make_async_remote_copy(src_ref=src, dst_ref=dst, send_sem=send.at[k], recv_sem=recv.at[k],
                                               device_id=peer, device_id_type=MESH)
                  for k, (src, dst) in enumerate(zip(srcs, (li_ref, lo_ref, ls_ref)))]
        for cp in remote:
            cp.start()
        _finish([], remote, remote)

    return pl.pallas_call(
        body, name="pair_exchange",
        in_specs=[ANY] * 3, out_specs=[ANY] * 3,
        out_shape=[jax.ShapeDtypeStruct((N_CHIPS, HALF_IN, IN_SHARD), F32),
                   jax.ShapeDtypeStruct((N_CHIPS, HALF_OUT, D_MODEL), F32),
                   jax.ShapeDtypeStruct(g_small.shape, F32)],
        scratch_shapes=[pltpu.SemaphoreType.DMA((3,)), pltpu.SemaphoreType.DMA((3,))],
    )(g_in, g_out, g_small)


def _pair_sum(g_in, g_out, g_small, l_in, l_out, l_small, c_arr):
    tr = HALF_IN // 2

    def body(c_ref, gi_ref, go_ref, gs_ref, li_ref, lo_ref, ls_ref, ci_ref, co_ref, cs_ref):
        ci_ref[...] = (gi_ref[...] + li_ref[...]).astype(BF16)

        @pl.when(pl.program_id(1) == 0)
        def _():
            co_ref[...] = (go_ref[...] + lo_ref[...]).astype(BF16)

        @pl.when((pl.program_id(0) == 0) & (pl.program_id(1) == 0))
        def _():
            cs_ref[...] = gs_ref[...] + ls_ref[...]

    nd = g_small.shape
    return pl.pallas_call(
        body, name="pair_sum",
        grid_spec=pltpu.PrefetchScalarGridSpec(
            num_scalar_prefetch=1, grid=(N_CHIPS, 2),
            in_specs=[pl.BlockSpec((1, tr, IN_SHARD), lambda s, i, c: (s, 2 * c[0] + i, 0)),
                      pl.BlockSpec((1, HALF_OUT, D_MODEL), lambda s, i, c: (s, c[0], 0)),
                      pl.BlockSpec(nd, lambda s, i, c: (0, 0)),
                      pl.BlockSpec((1, tr, IN_SHARD), lambda s, i, c: (s, i, 0)),
                      pl.BlockSpec((1, HALF_OUT, D_MODEL), lambda s, i, c: (s, 0, 0)),
                      pl.BlockSpec(nd, lambda s, i, c: (0, 0))],
            out_specs=[pl.BlockSpec((1, tr, IN_SHARD), lambda s, i, c: (s, i, 0)),
                       pl.BlockSpec((1, HALF_OUT, D_MODEL), lambda s, i, c: (s, 0, 0)),
                       pl.BlockSpec(nd, lambda s, i, c: (0, 0))]),
        out_shape=[jax.ShapeDtypeStruct((N_CHIPS, HALF_IN, IN_SHARD), BF16),
                   jax.ShapeDtypeStruct((N_CHIPS, HALF_OUT, D_MODEL), BF16), jax.ShapeDtypeStruct(nd, F32)],
        compiler_params=_cparams(dimension_semantics=("arbitrary", "arbitrary")),
    )(c_arr, g_in, g_out, g_small, l_in, l_out, l_small)


def _chip_exchange(c_in, c_out, c_small):
    def body(ci_ref, co_ref, cs_ref, li_ref, lo_ref, ls_ref, send, recv):
        x, y, c = _position()
        s = 2 * x + y
        remote = []
        for j, (fx, fy) in enumerate(CHIP_FLIPS):
            px, py = _flip(x, fx), _flip(y, fy)
            ps = 2 * px + py
            for a, (src, dst) in enumerate(((ci_ref.at[ps], li_ref.at[j]), (co_ref.at[ps], lo_ref.at[j]),
                                            (cs_ref, ls_ref.at[j]))):
                k = 3 * j + a
                remote.append(pltpu.make_async_remote_copy(src_ref=src, dst_ref=dst, send_sem=send.at[k],
                                                           recv_sem=recv.at[k], device_id=(px, py, c),
                                                           device_id_type=MESH))
        for cp in remote:
            cp.start()
        _finish([], remote, remote)

    return pl.pallas_call(
        body, name="chip_exchange",
        in_specs=[ANY] * 3, out_specs=[ANY] * 3,
        out_shape=[jax.ShapeDtypeStruct((N_FLIPS, HALF_IN, IN_SHARD), c_in.dtype),
                   jax.ShapeDtypeStruct((N_FLIPS, HALF_OUT, D_MODEL), c_out.dtype),
                   jax.ShapeDtypeStruct((N_FLIPS,) + c_small.shape, F32)],
        scratch_shapes=[pltpu.SemaphoreType.DMA((3 * N_FLIPS,)), pltpu.SemaphoreType.DMA((3 * N_FLIPS,))],
    )(c_in, c_out, c_small)


def _chip_sum(g_in, g_out, p_in, p_out, c_small, l_in, l_out, l_small, sc_arr):
    tr = HALF_IN // 2
    nd = c_small.shape

    def body(s_ref, gi_ref, go_ref, pi_ref, po_ref, cs_ref, li0, li1, li2, lo0, lo1, lo2, ls_ref,
             ri_ref, ro_ref, rs_ref):
        ri_ref[...] = (((gi_ref[0] + pi_ref[0]) + li0[0].astype(F32)) + li1[0].astype(F32)) + li2[0].astype(F32)

        @pl.when(pl.program_id(0) == 0)
        def _():
            ro_ref[...] = (((go_ref[0] + po_ref[0]) + lo0[0].astype(F32)) + lo1[0].astype(F32)) + lo2[0].astype(F32)
            me = s_ref[0]
            parts = (cs_ref[...], ls_ref[0], ls_ref[1], ls_ref[2])

            def of_chip(s):
                m = jnp.bitwise_xor(me, s)
                return jnp.where(m == 0, parts[0], jnp.where(m == 1, parts[1], jnp.where(m == 2, parts[2], parts[3])))

            rs_ref[...] = ((of_chip(0) + of_chip(1)) + of_chip(2)) + of_chip(3)

    def flip_in(j):
        return pl.BlockSpec((1, tr, IN_SHARD), lambda i, s: (j, i, 0))

    def flip_out(j):
        return pl.BlockSpec((1, HALF_OUT, D_MODEL), lambda i, s: (j, 0, 0))

    return pl.pallas_call(
        body, name="chip_sum",
        grid_spec=pltpu.PrefetchScalarGridSpec(
            num_scalar_prefetch=1, grid=(2,),
            in_specs=[pl.BlockSpec((1, tr, IN_SHARD), lambda i, s: (s[0], 2 * s[1] + i, 0)),
                      pl.BlockSpec((1, HALF_OUT, D_MODEL), lambda i, s: (s[0], s[1], 0)),
                      pl.BlockSpec((1, tr, IN_SHARD), lambda i, s: (s[0], i, 0)),
                      pl.BlockSpec((1, HALF_OUT, D_MODEL), lambda i, s: (s[0], 0, 0)),
                      pl.BlockSpec(nd, lambda i, s: (0, 0)),
                      flip_in(0), flip_in(1), flip_in(2), flip_out(0), flip_out(1), flip_out(2),
                      pl.BlockSpec((N_FLIPS,) + nd, lambda i, s: (0, 0, 0))],
            out_specs=[pl.BlockSpec((tr, IN_SHARD), lambda i, s: (i, 0)),
                       pl.BlockSpec((HALF_OUT, D_MODEL), lambda i, s: (0, 0)),
                       pl.BlockSpec(nd, lambda i, s: (0, 0))]),
        out_shape=[jax.ShapeDtypeStruct((HALF_IN, IN_SHARD), F32), jax.ShapeDtypeStruct((HALF_OUT, D_MODEL), F32),
                   jax.ShapeDtypeStruct(nd, F32)],
        compiler_params=_cparams(dimension_semantics=("arbitrary",)),
    )(sc_arr, g_in, g_out, p_in, p_out, c_small, l_in, l_in, l_in, l_out, l_out, l_out, l_small)


def _pair_share(r_in, r_out):
    def body(ri_ref, ro_ref, li_ref, lo_ref, send, recv):
        x, y, c = _position()
        remote = [pltpu.make_async_remote_copy(src_ref=src, dst_ref=dst, send_sem=send.at[k], recv_sem=recv.at[k],
                                               device_id=(x, y, 1 - c), device_id_type=MESH)
                  for k, (src, dst) in enumerate(((ri_ref, li_ref), (ro_ref, lo_ref)))]
        for cp in remote:
            cp.start()
        _finish([], remote, remote)

    return pl.pallas_call(
        body, name="pair_share",
        in_specs=[ANY] * 2, out_specs=[ANY] * 2,
        out_shape=[jax.ShapeDtypeStruct(r_in.shape, F32), jax.ShapeDtypeStruct(r_out.shape, F32)],
        scratch_shapes=[pltpu.SemaphoreType.DMA((2,)), pltpu.SemaphoreType.DMA((2,))],
    )(r_in, r_out)


def _adam_update(w, g, m, v):
    mn = ADAM_B1 * m + (1.0 - ADAM_B1) * g
    vn = ADAM_B2 * v + (1.0 - ADAM_B2) * jnp.square(g)
    m_hat = mn / (1.0 - ADAM_B1 ** ADAM_STEP)
    v_hat = vn / (1.0 - ADAM_B2 ** ADAM_STEP)
    return -ADAM_LR * (m_hat / (jnp.sqrt(v_hat) + ADAM_EPS) + ADAM_WD * w), mn, vn


def _adamw(name, w, g_mine, g_theirs, m, v, core_arr, tr):
    rows, cols = w.shape
    per_half = rows // 2 // tr

    def body(c_ref, w_ref, gm_ref, gt_ref, m_ref, v_ref, g_ref, d_ref, nm_ref, nv_ref):
        mine = (pl.program_id(0) // per_half) == c_ref[0]
        g = jnp.where(mine, gm_ref[...], gt_ref[...])
        d, mn, vn = _adam_update(w_ref[...], g, m_ref[...], v_ref[...])
        g_ref[...] = g
        d_ref[...] = d
        nm_ref[...] = mn
        nv_ref[...] = vn

    spec = pl.BlockSpec((tr, cols), lambda i, c: (i, 0))
    half = pl.BlockSpec((tr, cols), lambda i, c: (i % per_half, 0))
    return pl.pallas_call(
        body, name=name,
        grid_spec=pltpu.PrefetchScalarGridSpec(
            num_scalar_prefetch=1, grid=(rows // tr,),
            in_specs=[spec, half, half, spec, spec], out_specs=[spec] * 4),
        out_shape=[jax.ShapeDtypeStruct((rows, cols), F32)] * 4,
        compiler_params=_cparams(dimension_semantics=("arbitrary",)),
    )(core_arr, w, g_mine, g_theirs, m, v)


def _row_pieces(n):
    return [(k, k * PACK_W, min(PACK_W, n - k * PACK_W)) for k in range(-(-n // PACK_W))]


def _pack_small(d_small, loss, d_lora):
    ns = len(SMALL_NAMES)

    def body(*refs):
        small_refs, (loss_ref, lora_ref, out_ref) = refs[:ns], refs[ns:]
        out_ref[...] = jnp.zeros_like(out_ref)
        out_ref[PACK_LORA_W:PACK_LORA_W + LORA, :] = lora_ref[:LORA, :RW_WIDTH]
        out_ref[PACK_LORA_A:PACK_LORA_A + LORA, :] = lora_ref[LORA:, RW_WIDTH:]
        for name, n, ref in zip(SMALL_NAMES, SMALL_SIZES, small_refs):
            for k, at, w in _row_pieces(n):
                out_ref[PACK_AT[name] + k:PACK_AT[name] + k + 1, 0:w] = ref[:, at:at + w]
        out_ref[PACK_LOSS:PACK_LOSS + 1, :] = loss_ref[...]

    return pl.pallas_call(body, name="pack_small", out_shape=jax.ShapeDtypeStruct((PACK_ROWS, PACK_W), F32),
                          compiler_params=_cparams())(*d_small, loss, d_lora)


def _adamw_small(tot, chip_arr, ws, ms, vs):
    ns = len(SMALL_NAMES)
    n_par = ns + 2

    def body(s_ref, tot_ref, glw_ref, gla_ref, *refs):
        w_refs, m_refs, v_refs = refs[:n_par], refs[n_par:2 * n_par], refs[2 * n_par:3 * n_par]
        outs = refs[3 * n_par:]
        g_refs, d_refs, nm_refs, nv_refs = (outs[i * n_par:(i + 1) * n_par] for i in range(4))
        grads = [jnp.concatenate([tot_ref[PACK_AT[name] + k:PACK_AT[name] + k + 1, 0:w] for k, _, w in _row_pieces(n)],
                                 axis=1) for name, n in zip(SMALL_NAMES, SMALL_SIZES)]
        grads += [glw_ref[...], gla_ref[...]]
        for i, g in enumerate(grads):
            d, mn, vn = _adam_update(w_refs[i][...], g, m_refs[i][...], v_refs[i][...])
            g_refs[i][...] = g
            d_refs[i][...] = d
            nm_refs[i][...] = mn
            nv_refs[i][...] = vn

    def whole(a):
        nd = a.ndim
        return pl.BlockSpec(a.shape, lambda i, s: (0,) * nd)

    shard = (LORA, LORA_SHARD)
    par_specs = [whole(a) for a in ws]
    res = pl.pallas_call(
        body, name="adamw_small",
        grid_spec=pltpu.PrefetchScalarGridSpec(
            num_scalar_prefetch=1, grid=(1,),
            in_specs=[whole(tot), pl.BlockSpec(shard, lambda i, s: (PACK_LORA_W // LORA, s[0])),
                      pl.BlockSpec(shard, lambda i, s: (PACK_LORA_A // LORA, s[0]))] + par_specs * 3,
            out_specs=par_specs * 4),
        out_shape=[jax.ShapeDtypeStruct(a.shape, F32) for a in ws] * 4,
        compiler_params=_cparams(dimension_semantics=("arbitrary",)),
    )(chip_arr, tot, tot, tot, *ws, *ms, *vs)
    return [res[i * n_par:(i + 1) * n_par] for i in range(4)]


def kernel(x, norm_g, w_in, ret_gn_g, rwkv_mu, w_lora_up, w0, a_lora_up, a0, k_k, k_a, r_k, rwkv_gn_g, rwkv_gn_b, w_out, final_norm_g, loss_target, m_norm_g, m_w_in, m_ret_gn_g, m_rwkv_mu, m_w_lora_up, m_w0, m_a_lora_up, m_a0, m_k_k, m_k_a, m_r_k, m_rwkv_gn_g, m_rwkv_gn_b, m_w_out, m_final_norm_g, v_norm_g, v_w_in, v_ret_gn_g, v_rwkv_mu, v_w_lora_up, v_w0, v_a_lora_up, v_a0, v_k_k, v_k_a, v_r_k, v_rwkv_gn_g, v_rwkv_gn_b, v_w_out, v_final_norm_g):
    W = RW_WIDTH
    params = dict(norm_g=norm_g, ret_gn_g=ret_gn_g, rwkv_mu=rwkv_mu, w0=w0, a0=a0, k_k=k_k, k_a=k_a, r_k=r_k,
                  rwkv_gn_g=rwkv_gn_g, rwkv_gn_b=rwkv_gn_b, final_norm_g=final_norm_g)
    moments_m = dict(norm_g=m_norm_g, ret_gn_g=m_ret_gn_g, rwkv_mu=m_rwkv_mu, w0=m_w0, a0=m_a0, k_k=m_k_k, k_a=m_k_a,
                     r_k=m_r_k, rwkv_gn_g=m_rwkv_gn_g, rwkv_gn_b=m_rwkv_gn_b, final_norm_g=m_final_norm_g)
    moments_v = dict(norm_g=v_norm_g, ret_gn_g=v_ret_gn_g, rwkv_mu=v_rwkv_mu, w0=v_w0, a0=v_a0, k_k=v_k_k, k_a=v_k_a,
                     r_k=v_r_k, rwkv_gn_g=v_rwkv_gn_g, rwkv_gn_b=v_rwkv_gn_b, final_norm_g=v_final_norm_g)
    xi, yi, ci = _position()
    chip = (2 * xi + yi).astype(jnp.int32)

    def halves(a):
        return a.reshape(2, a.shape[0] // 2, a.shape[1])

    mine = [halves(w_in[0].astype(BF16)), halves(w_out[0].astype(BF16)), halves(w_lora_up[0]), halves(a_lora_up[0])]
    g_in, g_out, g_lw, g_la = [lax.dynamic_update_slice(g, own[None], (chip, 0, 0, 0))
                               for g, own in zip(_gather_chips(mine), mine)]
    w_in_b = jnp.transpose(g_in.reshape(N_CHIPS, D_MODEL, IN_SHARD), (1, 0, 2)).reshape(D_MODEL, IN_COLS)
    w_out_b = g_out.reshape(D_MODEL, D_MODEL)
    lw = jnp.transpose(g_lw.reshape(N_CHIPS, LORA, LORA_SHARD), (1, 0, 2)).reshape(LORA, W)
    la = jnp.transpose(g_la.reshape(N_CHIPS, LORA, LORA_SHARD), (1, 0, 2)).reshape(LORA, W)
    zero = jnp.zeros((LORA, W), F32)
    lora = jnp.concatenate([jnp.concatenate([lw, zero], axis=1), jnp.concatenate([zero, la], axis=1)], axis=0)
    small = {n: params[n].reshape(1, -1) for n in SMALL_NAMES}

    loss, grad_x, d_w_in, d_w_out, d_lora, d_small = _local_step(x[0], loss_target[0], w_in_b, w_out_b, lora, small)

    core = ci.astype(jnp.int32)
    gi = jnp.transpose(d_w_in.reshape(D_MODEL, N_CHIPS, IN_SHARD), (1, 0, 2))
    go = d_w_out.reshape(N_CHIPS, OUT_SHARD, D_MODEL)
    gs = _pack_small([d_small[n] for n in SMALL_NAMES], loss, d_lora)
    p_in, p_out, p_small = _pair_exchange(gi, go, gs)
    c_in, c_out, c_small = _pair_sum(gi, go, gs, p_in, p_out, p_small, core.reshape(1))
    l_in, l_out, l_small = _chip_exchange(c_in, c_out, c_small)
    r_in, r_out, tot = _chip_sum(gi, go, p_in, p_out, c_small, l_in, l_out, l_small, jnp.stack([chip, core]))
    t_in, t_out = _pair_share(r_in, r_out)

    grad_w_in, d_in, nm_in, nv_in = _adamw("adamw_w_in", w_in[0], r_in, t_in, m_w_in[0], v_w_in[0], core.reshape(1), 256)
    grad_w_out, d_out, nm_out, nv_out = _adamw("adamw_w_out", w_out[0], r_out, t_out, m_w_out[0], v_w_out[0],
                                               core.reshape(1), HALF_OUT)
    par_names = SMALL_NAMES + ("w_lora_up", "a_lora_up")

    def operands(tree, lw_, la_):
        return [tree[n].reshape(1, -1) for n in SMALL_NAMES] + [lw_[0], la_[0]]

    res = _adamw_small(tot, chip.reshape(1), operands(params, w_lora_up, a_lora_up),
                       operands(moments_m, m_w_lora_up, m_a_lora_up), operands(moments_v, v_w_lora_up, v_a_lora_up))

    names = ("norm_g", "w_in", "ret_gn_g", "rwkv_mu", "w_lora_up", "w0", "a_lora_up", "a0", "k_k", "k_a", "r_k",
             "rwkv_gn_g", "rwkv_gn_b", "w_out", "final_norm_g")
    shapes = dict(w_in=w_in.shape, w_out=w_out.shape, w_lora_up=w_lora_up.shape, a_lora_up=a_lora_up.shape,
                  **{n: params[n].shape for n in SMALL_NAMES})

    def leaves(pars, big_in, big_out):
        tree = dict(zip(par_names, pars), w_in=big_in, w_out=big_out)
        return [tree[n].reshape(shapes[n]) for n in names]

    grads = leaves(res[0], grad_w_in, grad_w_out)
    deltas = leaves(res[1], d_in, d_out)
    new_m = leaves(res[2], nm_in, nm_out)
    new_v = leaves(res[3], nv_in, nv_out)
    return (tot[PACK_LOSS, 0], grad_x.reshape(x.shape), *grads, *deltas, *new_m, *new_v)
```

```python
import functools

import numpy as np
import jax
import jax.numpy as jnp
from jax import lax
from jax.experimental import pallas as pl
from jax.experimental.pallas import tpu as pltpu

F32 = jnp.float32
BF16 = jnp.bfloat16
X3 = "bf16x3"
MESH = pl.DeviceIdType.MESH

D_MODEL = 1024
N_CHIPS = 4
RET_HEADS = 4
RET_DK = 64
RET_DV = 128
RET_QK = RET_HEADS * RET_DK
RET_WIDTH = RET_HEADS * RET_DV
RET_COLS = 2 * RET_QK + 2 * RET_WIDTH
RET_CHUNK = 64
RET_GROUP = 4
RW_WIDTH = 512
RW_HEAD = 64
RW_HEADS = 8
LORA = 64
RW_COLS = 4 * RW_WIDTH + 2 * LORA
IN_COLS = RET_COLS + RW_COLS
IN_SHARD = IN_COLS // N_CHIPS
OUT_SHARD = D_MODEL // N_CHIPS
ROPE_BASE = 10000.0
RMS_EPS = 1e-6
RET_GN_EPS = 1e-5
RW_GN_EPS = 64e-5
WKV_CHUNK = 16
WKV_GROUP = 4
N_VEC = 5

ADAM_LR = 0.001
ADAM_B1 = 0.9
ADAM_B2 = 0.999
ADAM_EPS = 1e-08
ADAM_WD = 0.01
ADAM_STEP = 10

VMEM_LIMIT = 56 * 1024 * 1024

PACK_W = 512
SMALL_NAMES = ("norm_g", "ret_gn_g", "rwkv_mu", "w0", "a0", "k_k", "k_a", "r_k", "rwkv_gn_g", "rwkv_gn_b",
               "final_norm_g")
SMALL_SIZES = (1024, 512, 2176, 512, 512, 512, 512, 512, 512, 512, 1024)
PACK_LORA_W = 0
PACK_LORA_A = LORA
PACK_SMALL = 2 * LORA


def _pack_layout():
    rows, at = {}, PACK_SMALL
    for name, n in zip(SMALL_NAMES, SMALL_SIZES):
        rows[name] = at
        at += -(-n // PACK_W)
    return rows, at


PACK_AT, PACK_LOSS = _pack_layout()
PACK_ROWS = -(-(PACK_LOSS + 1) // 8) * 8


def _cparams(**kw):
    return pltpu.CompilerParams(vmem_limit_bytes=VMEM_LIMIT, **kw)


def _split(x):
    hi = x.astype(BF16)
    lo = (x - hi.astype(F32)).astype(BF16)
    return hi, lo


def _dot_dims(a, b, dims, precision):
    if precision != X3:
        return lax.dot_general(a, b, dims, preferred_element_type=F32)
    (ah, al), (bh, bl) = _split(a), _split(b)
    dot = lambda u, w: lax.dot_general(u, w, dims, preferred_element_type=F32)
    return dot(ah, bh) + dot(ah, bl) + dot(al, bh)


def _dot(a, b, precision=None):
    return _dot_dims(a, b, (((1,), (0,)), ((), ())), precision)


def _dot_nt(a, b, precision=None):
    return _dot_dims(a, b, (((1,), (1,)), ((), ())), precision)


def _dot_tn(a, b, precision=None):
    return _dot_dims(a, b, (((0,), (0,)), ((), ())), precision)


@jax.custom_vjp
def _segsum(x, seg):
    hi, lo = _split(x)
    return _dot(hi, seg) + _dot(lo, seg)


def _segsum_fwd(x, seg):
    return _segsum(x, seg), seg


def _segsum_bwd(seg, ct):
    return _segsum(ct, seg), jnp.zeros_like(seg)


_segsum.defvjp(_segsum_fwd, _segsum_bwd)


def _softplus(z):
    return jnp.maximum(z, 0.0) + jnp.log(1.0 + jnp.exp(-jnp.abs(z)))


def _full(shape):
    nd = len(shape)
    return pl.BlockSpec(shape, lambda *_: (0,) * nd)


def _rope_tables(T):
    half = RET_DK // 2
    expo = -jnp.arange(half, dtype=F32) / jnp.float32(half)
    freqs = jnp.exp(expo * jnp.float32(np.log(ROPE_BASE)))
    ang = jnp.arange(T, dtype=jnp.int32).astype(F32)[:, None] * freqs[None, :]
    cos = jnp.tile(jnp.cos(ang), (1, 2 * RET_HEADS))
    sin = jnp.tile(jnp.sin(ang), (1, 2 * RET_HEADS))
    return cos, sin


def _ret_tables():
    H, C = RET_HEADS, RET_CHUNK
    hidx = jnp.arange(H, dtype=F32)
    lg = jnp.log(1.0 - jnp.exp2(-5.0 - hidx))
    idx = jnp.arange(C, dtype=F32)
    intra = jnp.exp(lg[:, None, None] * jnp.abs(idx[:, None] - idx[None, :]))
    q_dec = jnp.transpose(jnp.exp(lg[:, None] * (idx[None, :] + 1.0)))
    k_dec = jnp.transpose(jnp.exp(lg[:, None] * (C - 1.0 - idx[None, :])))
    chunk_dec = jnp.exp(lg * C)
    qd = jnp.repeat(q_dec, RET_DK, axis=1)
    kd = jnp.repeat(k_dec, RET_DK, axis=1)
    row_h = np.arange(RET_QK) // RET_DK
    col_h = np.arange(RET_WIDTH) // RET_DV
    bm = jnp.asarray((row_h[:, None] == col_h[None, :]).astype(np.float32))
    cd = bm * jnp.repeat(chunk_dec, RET_DK)[:, None]
    return intra, qd, kd, cd, bm


def _seg_matrix(width, head):
    h = np.arange(width) // head
    return jnp.asarray((h[:, None] == h[None, :]).astype(np.float32), dtype=BF16)


def _wkv_expand_table():
    Tc = WKV_CHUNK
    k = np.arange(2 * RW_HEADS * Tc)
    kh, kt = (k % (RW_HEADS * Tc)) // Tc, k % Tc
    nh = np.arange(RW_WIDTH) // RW_HEAD
    e = (kh[None, :, None] == nh[None, None, :]) & (kt[None, :, None] == np.arange(Tc)[:, None, None])
    return jnp.asarray(e.astype(np.float32), dtype=BF16)


def _wkv_reduce_table():
    Tc = WKV_CHUNK
    kh = np.arange(RW_WIDTH) // RW_HEAD
    n = np.arange(RW_HEADS * Tc)
    nh, nt = n // Tc, n % Tc
    r = (kh[None, :, None] == nh[None, None, :]) & (nt[None, None, :] == np.arange(Tc)[:, None, None])
    return jnp.asarray(r.astype(np.float32), dtype=BF16)


def _inproj_fwd(x, norm_g, w_t):
    T = x.shape[0]
    tm = _row_tile(T)

    def body(x_ref, g_ref, w_ref, pret_ref, prw_ref, u_ref, last_ref):
        xf = x_ref[...]
        rstd = lax.rsqrt(jnp.mean(xf * xf, axis=-1, keepdims=True) + RMS_EPS)
        ub = ((xf * rstd) * g_ref[...]).astype(BF16)
        u_ref[...] = ub
        pret_ref[...] = _dot_nt(ub, w_ref[:RET_COLS, :])
        p_rw = _dot_nt(ub, w_ref[RET_COLS:, :])
        prw_ref[...] = p_rw
        last_ref[0] = p_rw[tm - 1:tm, :]

    return pl.pallas_call(
        body, name="inproj_fwd", grid=(T // tm,),
        in_specs=[pl.BlockSpec((tm, D_MODEL), lambda i: (i, 0)), _full((1, D_MODEL)), _full((IN_COLS, D_MODEL))],
        out_specs=[pl.BlockSpec((tm, RET_COLS), lambda i: (i, 0)), pl.BlockSpec((tm, RW_COLS), lambda i: (i, 0)),
                   pl.BlockSpec((tm, D_MODEL), lambda i: (i, 0)), pl.BlockSpec((1, 1, RW_COLS), lambda i: (i, 0, 0))],
        out_shape=[jax.ShapeDtypeStruct((T, RET_COLS), F32), jax.ShapeDtypeStruct((T, RW_COLS), F32),
                   jax.ShapeDtypeStruct((T, D_MODEL), BF16), jax.ShapeDtypeStruct((T // tm, 1, RW_COLS), F32)],
        compiler_params=_cparams(dimension_semantics=("arbitrary",)),
    )(x, norm_g, w_t)


def _rot_half(x):
    n = x.shape[1]
    lane = lax.broadcasted_iota(jnp.int32, x.shape, 1)
    first = (lane % RET_DK) < (RET_DK // 2)
    return jnp.where(first, -pltpu.roll(x, n - RET_DK // 2, 1), pltpu.roll(x, RET_DK // 2, 1))


def _rope(x, cos, sin):
    return x * cos + _rot_half(x) * sin


def _rope_bwd(d, cos, sin):
    return d * cos - _rot_half(d * sin)


def _ret_post(ret, g, gn_g, seg):
    mu = _segsum(ret, seg) * (1.0 / RET_DV)
    xc = ret - mu
    var = _segsum(xc * xc, seg) * (1.0 / RET_DV)
    n = xc * lax.rsqrt(var + RET_GN_EPS)
    return (g * jax.nn.sigmoid(g)) * (n * gn_g)


def _ret_scores(qt, kt, d_ref, h):
    lane = lax.broadcasted_iota(jnp.int32, qt.shape, 1)
    qh = jnp.where(lane // RET_DK == h, qt, 0.0)
    return qh, _dot_nt(qh, kt, X3) * d_ref[h]


def _ret_group(nch):
    return min(RET_GROUP, nch)


def _ret_fwd(p_ret, cos, sin, tabs, gn_g, seg128):
    T = p_ret.shape[0]
    C = RET_CHUNK
    nch = T // C
    G = _ret_group(nch)
    intra_d, qd, kd, cd, bm = tabs

    def body(q_ref, k_ref, v_ref, g_ref, cos_ref, sin_ref, qd_ref, kd_ref, d_ref, cd_ref, bm_ref, gn_ref, seg_ref,
             y_ref, ret_ref, sin_out_ref, s_ref):
        @pl.when(pl.program_id(0) == 0)
        def _():
            s_ref[...] = jnp.zeros_like(s_ref)

        s_in = s_ref[...]
        for i in range(G):
            rows = slice(i * C, (i + 1) * C)
            cosv, sinv = cos_ref[rows, :], sin_ref[rows, :]
            qt = _rope(q_ref[rows, :], cosv, sinv)
            kt = _rope(k_ref[rows, :], cosv, sinv) * (RET_DK ** -0.5)
            v = v_ref[rows, :]
            sin_out_ref[i] = s_in
            inter = _dot(qt * qd_ref[...], s_in, X3)
            intra = []
            for h in range(RET_HEADS):
                _, a = _ret_scores(qt, kt, d_ref, h)
                intra.append(_dot(a, v[:, h * RET_DV:(h + 1) * RET_DV], X3))
            ret_ref[rows, :] = jnp.concatenate(intra, axis=1) + inter
            kv = _dot_tn(kt * kd_ref[...], v, X3)
            s_in = s_in * cd_ref[...] + kv * bm_ref[...]
        s_ref[...] = s_in
        y_ref[...] = _ret_post(ret_ref[...], g_ref[...], gn_ref[...], seg_ref[...]).astype(BF16)

    GC = G * C
    return pl.pallas_call(
        body, name="ret_fwd", grid=(nch // G,),
        in_specs=[pl.BlockSpec((GC, RET_QK), lambda c: (c, 0)), pl.BlockSpec((GC, RET_QK), lambda c: (c, 1)),
                  pl.BlockSpec((GC, RET_WIDTH), lambda c: (c, 1)), pl.BlockSpec((GC, RET_WIDTH), lambda c: (c, 2)),
                  pl.BlockSpec((GC, RET_QK), lambda c: (c, 0)), pl.BlockSpec((GC, RET_QK), lambda c: (c, 0)),
                  _full((C, RET_QK)), _full((C, RET_QK)), _full((RET_HEADS, C, C)),
                  _full((RET_QK, RET_WIDTH)), _full((RET_QK, RET_WIDTH)), _full((1, RET_WIDTH)),
                  _full((RET_WIDTH, RET_WIDTH))],
        out_specs=[pl.BlockSpec((GC, RET_WIDTH), lambda c: (c, 0)), pl.BlockSpec((GC, RET_WIDTH), lambda c: (c, 0)),
                   pl.BlockSpec((G, RET_QK, RET_WIDTH), lambda c: (c, 0, 0))],
        out_shape=[jax.ShapeDtypeStruct((T, RET_WIDTH), BF16), jax.ShapeDtypeStruct((T, RET_WIDTH), F32),
                   jax.ShapeDtypeStruct((nch, RET_QK, RET_WIDTH), F32)],
        scratch_shapes=[pltpu.VMEM((RET_QK, RET_WIDTH), F32)],
        compiler_params=_cparams(dimension_semantics=("arbitrary",)),
    )(p_ret, p_ret, p_ret, p_ret, cos, sin, qd, kd, intra_d, cd, bm, gn_g, seg128)


def _ret_bwd(p_ret, cos, sin, tabs, gn_g, seg128, ret, s_in_all, dy):
    T = p_ret.shape[0]
    C = RET_CHUNK
    nch = T // C
    G = _ret_group(nch)
    ngr = nch // G
    intra_d, qd, kd, cd, bm = tabs

    def rev(j):
        return lambda c: (ngr - 1 - c, j)

    def body(q_ref, k_ref, v_ref, g_ref, cos_ref, sin_ref, qd_ref, kd_ref, d_ref, cd_ref, bm_ref, gn_ref, seg_ref,
             ret_ref, sin_ref_, dy_ref, dp_ref, dgn_ref, ds_ref):
        @pl.when(pl.program_id(0) == 0)
        def _():
            ds_ref[...] = jnp.zeros_like(ds_ref)
            dgn_ref[...] = jnp.zeros_like(dgn_ref)

        seg = seg_ref[...]
        _, post_vjp = jax.vjp(lambda r_, g_, gn_: _ret_post(r_, g_, gn_, seg), ret_ref[...], g_ref[...], gn_ref[...])
        dret_all, dg_all, dgn = post_vjp(dy_ref[...])
        dgn_ref[...] += dgn
        dp_ref[:, 2 * RET_QK + RET_WIDTH:] = dg_all

        qdv, kdv = qd_ref[...], kd_ref[...]
        ds_out = ds_ref[...]
        for i in reversed(range(G)):
            rows = slice(i * C, (i + 1) * C)
            cosv, sinv = cos_ref[rows, :], sin_ref[rows, :]
            qt = _rope(q_ref[rows, :], cosv, sinv)
            kt = _rope(k_ref[rows, :], cosv, sinv) * (RET_DK ** -0.5)
            v = v_ref[rows, :]
            s_in = sin_ref_[i]
            dret = dret_all[rows, :]
            dqt = qdv * _dot_nt(dret, s_in, X3)
            dkt = kdv * _dot_nt(v, ds_out, X3)
            dv_all = _dot(kt * kdv, ds_out, X3)
            dvs = []
            for h in range(RET_HEADS):
                sl = slice(h * RET_DV, (h + 1) * RET_DV)
                qh, a = _ret_scores(qt, kt, d_ref, h)
                lane = lax.broadcasted_iota(jnp.int32, kt.shape, 1)
                kh = jnp.where(lane // RET_DK == h, kt, 0.0)
                da = _dot_nt(dret[:, sl], v[:, sl], X3) * d_ref[h]
                dvs.append(_dot_tn(a, dret[:, sl], X3))
                dqt = dqt + _dot(da, kh, X3)
                dkt = dkt + _dot_tn(da, qh, X3)
            ds_out = ds_out * cd_ref[...] + _dot_tn(qt * qdv, dret, X3) * bm_ref[...]
            dp_ref[rows, :RET_QK] = _rope_bwd(dqt, cosv, sinv)
            dp_ref[rows, RET_QK:2 * RET_QK] = _rope_bwd(dkt * (RET_DK ** -0.5), cosv, sinv)
            dp_ref[rows, 2 * RET_QK:2 * RET_QK + RET_WIDTH] = dv_all + jnp.concatenate(dvs, axis=1)
        ds_ref[...] = ds_out

    GC = G * C
    return pl.pallas_call(
        body, name="ret_bwd", grid=(ngr,),
        in_specs=[pl.BlockSpec((GC, RET_QK), rev(0)), pl.BlockSpec((GC, RET_QK), rev(1)),
                  pl.BlockSpec((GC, RET_WIDTH), rev(1)), pl.BlockSpec((GC, RET_WIDTH), rev(2)),
                  pl.BlockSpec((GC, RET_QK), rev(0)), pl.BlockSpec((GC, RET_QK), rev(0)),
                  _full((C, RET_QK)), _full((C, RET_QK)), _full((RET_HEADS, C, C)),
                  _full((RET_QK, RET_WIDTH)), _full((RET_QK, RET_WIDTH)), _full((1, RET_WIDTH)),
                  _full((RET_WIDTH, RET_WIDTH)),
                  pl.BlockSpec((GC, RET_WIDTH), rev(0)),
                  pl.BlockSpec((G, RET_QK, RET_WIDTH), lambda c: (ngr - 1 - c, 0, 0)),
                  pl.BlockSpec((GC, RET_WIDTH), rev(0))],
        out_specs=[pl.BlockSpec((GC, RET_COLS), rev(0)), _full((1, RET_WIDTH))],
        out_shape=[jax.ShapeDtypeStruct((T, RET_COLS), F32), jax.ShapeDtypeStruct((1, RET_WIDTH), F32)],
        scratch_shapes=[pltpu.VMEM((RET_QK, RET_WIDTH), F32)],
        compiler_params=_cparams(dimension_semantics=("arbitrary",)),
    )(p_ret, p_ret, p_ret, p_ret, cos, sin, qd, kd, intra_d, cd, bm, gn_g, seg128, ret, s_in_all, dy)


@jax.custom_vjp
def _chunk_sums(x, tri):
    hi, lo = _split(x)
    return _dot(tri, hi) + _dot(tri, lo)


def _chunk_sums_fwd(x, tri):
    return _chunk_sums(x, tri), tri


def _chunk_sums_bwd(tri, ct):
    hi, lo = _split(ct)
    return _dot_tn(tri, hi) + _dot_tn(tri, lo), jnp.zeros_like(tri)


_chunk_sums.defvjp(_chunk_sums_fwd, _chunk_sums_bwd)


@jax.custom_vjp
def _lora_dot(z, lora):
    return _dot(z, lora, X3)


def _lora_dot_fwd(z, lora):
    return _lora_dot(z, lora), (z, lora)


def _lora_dot_bwd(res, ct):
    z, lora = res
    return _dot_nt(ct, lora, X3), _dot_tn(z, ct, X3)


_lora_dot.defvjp(_lora_dot_fwd, _lora_dot_bwd)


def _chunk_tables(tm):
    t = np.arange(tm)
    same = (t[:, None] // WKV_CHUNK) == (t[None, :] // WKV_CHUNK)
    return jnp.asarray(np.stack([same & (t[None, :] <= t[:, None]), same]).astype(np.float32), dtype=BF16)


def _prep_fn(p, prev, mu, w0, a0, k_k, k_a, lora, seg, tri):
    W = RW_WIDTH
    ps = p + mu * (prev - p)
    r, kr, vr, g = ps[:, 0:W], ps[:, W:2 * W], ps[:, 2 * W:3 * W], ps[:, 3 * W:4 * W]
    z = ps[:, 4 * W:]
    lane = lax.broadcasted_iota(jnp.int32, z.shape, 1)
    z = jnp.where(lane < LORA, jnp.tanh(z), z)
    lo = _lora_dot(z, lora)
    w_log = -_softplus(-(w0 + lo[:, :W])) - 0.5
    log_decay = -jnp.exp(w_log)
    cum = _chunk_sums(log_decay, tri[0])
    total = _chunk_sums(log_decay, tri[1])
    a = jax.nn.sigmoid(a0 + lo[:, W:])
    kk = kr * k_k
    kk = kk / jnp.maximum(jnp.sqrt(_segsum(kk * kk, seg)), 1e-12)
    k = kr * (1.0 + (a - 1.0) * k_a)
    grow = jnp.exp(-cum)
    return kk * jnp.exp(cum - log_decay), (kk * a) * grow, k * grow, r * jnp.exp(cum), jnp.exp(total), vr, g, r * k


def _post_fn(o, rk, v, g, gn_g, gn_b, r_k, seg):
    mu = _segsum(o, seg) * (1.0 / RW_HEAD)
    oc = o - mu
    var = _segsum(oc * oc, seg) * (1.0 / RW_HEAD)
    on = oc * lax.rsqrt(var + RW_GN_EPS) * gn_g + gn_b
    bonus = _segsum(rk * r_k, seg) * v
    return (g * jax.nn.sigmoid(g)) * (on + bonus)


N_PAIR = (N_VEC + 1) // 2
HALF_LANES = 64


def _swap_halves(x):
    return pltpu.roll(x, HALF_LANES, 1)


def _pack_heads(vecs):
    tm = vecs[0].shape[0]
    low = lax.broadcasted_iota(jnp.int32, (tm, 128), 1) < HALF_LANES
    out = []
    for p in range(N_PAIR):
        a = vecs[2 * p]
        b = vecs[2 * p + 1] if 2 * p + 1 < len(vecs) else None
        heads = []
        for m in range(RW_WIDTH // 128):
            am = a[:, m * 128:(m + 1) * 128]
            bm = jnp.zeros_like(am) if b is None else b[:, m * 128:(m + 1) * 128]
            heads.append(jnp.where(low, am, _swap_halves(bm)))
            heads.append(jnp.where(low, _swap_halves(am), bm))
        out.append(heads)
    return out


def _unpack_heads(hm_ref):
    tm = hm_ref.shape[2]
    low = lax.broadcasted_iota(jnp.int32, (tm, 128), 1) < HALF_LANES
    vecs = []
    for p in range(N_PAIR):
        a, b = [], []
        for m in range(RW_WIDTH // 128):
            even, odd = hm_ref[p, 2 * m], hm_ref[p, 2 * m + 1]
            a.append(jnp.where(low, even, _swap_halves(odd)))
            b.append(jnp.where(low, _swap_halves(even), odd))
        vecs += [jnp.concatenate(a, axis=1), jnp.concatenate(b, axis=1)]
    return vecs[:N_VEC]


def _shift_down(p, first_row):
    row = lax.broadcasted_iota(jnp.int32, p.shape, 0)
    return jnp.where(row == 0, first_row, pltpu.roll(p, 1, 0))


def _shift_up(p, last_row):
    n = p.shape[0]
    row = lax.broadcasted_iota(jnp.int32, p.shape, 0)
    return jnp.where(row == n - 1, last_row, pltpu.roll(p, n - 1, 0))


def _row_tile(T):
    return min(T, 256)


def _prep_fwd(p_rw, bnd, mu, w0, a0, k_k, k_a, lora, seg64, tri):
    T = p_rw.shape[0]
    tm = _row_tile(T)
    W = RW_WIDTH

    def body(p_ref, bnd_ref, mu_ref, w0_ref, a0_ref, kk_ref, ka_ref, lora_ref, seg_ref, tri_ref,
             hm_ref, v_ref, g_ref, rk_ref):
        p = p_ref[...]
        prev = _shift_down(p, jnp.where(pl.program_id(0) == 0, 0.0, bnd_ref[0]))
        res = _prep_fn(p, prev, mu_ref[...], w0_ref[...], a0_ref[...], kk_ref[...], ka_ref[...], lora_ref[...],
                       seg_ref[...], (tri_ref[0], tri_ref[1]))
        for pair, heads in enumerate(_pack_heads(res[:N_VEC])):
            for h, val in enumerate(heads):
                hm_ref[pair, h] = val
        v_ref[...] = res[N_VEC]
        g_ref[...] = res[N_VEC + 1]
        rk_ref[...] = res[N_VEC + 2]

    small = _full((1, W))
    row = pl.BlockSpec((tm, W), lambda i: (i, 0))
    return pl.pallas_call(
        body, name="rwkv_prep_fwd", grid=(T // tm,),
        in_specs=[pl.BlockSpec((tm, RW_COLS), lambda i: (i, 0)),
                  pl.BlockSpec((1, 1, RW_COLS), lambda i: (jnp.maximum(i - 1, 0), 0, 0)),
                  _full((1, RW_COLS)), small, small, small, small, _full((2 * LORA, 2 * W)), _full((W, W)),
                  _full((2, tm, tm))],
        out_specs=[pl.BlockSpec((N_PAIR, RW_HEADS, tm, 128), lambda i: (0, 0, i, 0)), row, row, row],
        out_shape=[jax.ShapeDtypeStruct((N_PAIR, RW_HEADS, T, 128), F32)] + [jax.ShapeDtypeStruct((T, W), F32)] * 3,
        compiler_params=_cparams(dimension_semantics=("arbitrary",)),
    )(p_rw, bnd, mu, w0, a0, k_k, k_a, lora, seg64, tri)


def _prep_bwd(p_rw, bnd, mu, w0, a0, k_k, k_a, lora, seg64, tri, cts):
    T = p_rw.shape[0]
    tm = _row_tile(T)
    W = RW_WIDTH

    def body(p_ref, bnd_ref, mu_ref, w0_ref, a0_ref, kk_ref, ka_ref, lora_ref, seg_ref, tri_ref,
             dhm_ref, drk_ref, dv1_ref, dv2_ref, dg_ref,
             dp_ref, dprev_ref, dfirst_ref, dmu_ref, dw0_ref, da0_ref, dkk_p_ref, dka_ref, dlora_ref):
        accs = (dmu_ref, dw0_ref, da0_ref, dkk_p_ref, dka_ref, dlora_ref)

        @pl.when(pl.program_id(0) == 0)
        def _():
            for a_ref in accs:
                a_ref[...] = jnp.zeros_like(a_ref)

        p = p_ref[...]
        prev = _shift_down(p, jnp.where(pl.program_id(0) == 0, 0.0, bnd_ref[0]))
        seg, tri = seg_ref[...], (tri_ref[0], tri_ref[1])
        _, vjp = jax.vjp(lambda *a: _prep_fn(*a, seg, tri), p, prev, mu_ref[...], w0_ref[...], a0_ref[...],
                         kk_ref[...], ka_ref[...], lora_ref[...])
        ct = (*_unpack_heads(dhm_ref), dv1_ref[...] + dv2_ref[...], dg_ref[...], drk_ref[...])
        grads = vjp(ct)
        dp_ref[...] = grads[0]
        dprev_ref[...] = grads[1]
        dfirst_ref[0] = grads[1][0:1, :]
        for a_ref, gval in zip(accs, grads[2:]):
            a_ref[...] += gval

    small = _full((1, W))
    row = pl.BlockSpec((tm, W), lambda i: (i, 0))
    return pl.pallas_call(
        body, name="rwkv_prep_bwd", grid=(T // tm,),
        in_specs=[pl.BlockSpec((tm, RW_COLS), lambda i: (i, 0)),
                  pl.BlockSpec((1, 1, RW_COLS), lambda i: (jnp.maximum(i - 1, 0), 0, 0)),
                  _full((1, RW_COLS)), small, small, small, small, _full((2 * LORA, 2 * W)), _full((W, W)),
                  _full((2, tm, tm)), pl.BlockSpec((N_PAIR, RW_HEADS, tm, 128), lambda i: (0, 0, i, 0))] + [row] * 4,
        out_specs=[pl.BlockSpec((tm, RW_COLS), lambda i: (i, 0)), pl.BlockSpec((tm, RW_COLS), lambda i: (i, 0)),
                   pl.BlockSpec((1, 1, RW_COLS), lambda i: (i, 0, 0)),
                   _full((1, RW_COLS)), small, small, small, small, _full((2 * LORA, 2 * W))],
        out_shape=[jax.ShapeDtypeStruct((T, RW_COLS), F32), jax.ShapeDtypeStruct((T, RW_COLS), F32),
                   jax.ShapeDtypeStruct((T // tm, 1, RW_COLS), F32),
                   jax.ShapeDtypeStruct((1, RW_COLS), F32)] + [jax.ShapeDtypeStruct((1, W), F32)] * 4
                  + [jax.ShapeDtypeStruct((2 * LORA, 2 * W), F32)],
        compiler_params=_cparams(dimension_semantics=("arbitrary",)),
    )(p_rw, bnd, mu, w0, a0, k_k, k_a, lora, seg64, tri, *cts)


def _post_fwd(o, rk, v, g, gn_g, gn_b, r_k, seg64):
    T = o.shape[0]
    tm = _row_tile(T)
    W = RW_WIDTH

    def body(o_ref, rk_ref, v_ref, g_ref, gg_ref, gb_ref, rkp_ref, seg_ref, y_ref):
        y_ref[...] = _post_fn(o_ref[...], rk_ref[...], v_ref[...], g_ref[...], gg_ref[...], gb_ref[...],
                              rkp_ref[...], seg_ref[...]).astype(BF16)

    row = pl.BlockSpec((tm, W), lambda i: (i, 0))
    small = _full((1, W))
    return pl.pallas_call(
        body, name="rwkv_post_fwd", grid=(T // tm,),
        in_specs=[row] * 4 + [small] * 3 + [_full((W, W))],
        out_specs=row, out_shape=jax.ShapeDtypeStruct((T, W), BF16),
        compiler_params=_cparams(dimension_semantics=("arbitrary",)),
    )(o, rk, v, g, gn_g, gn_b, r_k, seg64)


def _post_bwd(o, rk, v, g, gn_g, gn_b, r_k, seg64, dy):
    T = o.shape[0]
    tm = _row_tile(T)
    W = RW_WIDTH

    def body(o_ref, rk_ref, v_ref, g_ref, gg_ref, gb_ref, rkp_ref, seg_ref, dy_ref,
             do_ref, drk_ref, dv_ref, dg_ref, dgg_ref, dgb_ref, drkp_ref):
        accs = (dgg_ref, dgb_ref, drkp_ref)

        @pl.when(pl.program_id(0) == 0)
        def _():
            for a_ref in accs:
                a_ref[...] = jnp.zeros_like(a_ref)

        seg = seg_ref[...]
        _, vjp = jax.vjp(lambda *a: _post_fn(*a, seg), o_ref[...], rk_ref[...], v_ref[...], g_ref[...],
                         gg_ref[...], gb_ref[...], rkp_ref[...])
        grads = vjp(dy_ref[...])
        for o_, gval in zip((do_ref, drk_ref, dv_ref, dg_ref), grads[:4]):
            o_[...] = gval
        for a_ref, gval in zip(accs, grads[4:]):
            a_ref[...] += gval

    row = pl.BlockSpec((tm, W), lambda i: (i, 0))
    small = _full((1, W))
    return pl.pallas_call(
        body, name="rwkv_post_bwd", grid=(T // tm,),
        in_specs=[row] * 4 + [small] * 3 + [_full((W, W)), pl.BlockSpec((tm, W), lambda i: (i, 1))],
        out_specs=[row] * 4 + [small] * 3,
        out_shape=[jax.ShapeDtypeStruct((T, W), F32)] * 4 + [jax.ShapeDtypeStruct((1, W), F32)] * 3,
        compiler_params=_cparams(dimension_semantics=("arbitrary",)),
    )(o, rk, v, g, gn_g, gn_b, r_k, seg64, dy)


def _wkv_lhs(hm_ref, rows):
    tiles = [jnp.transpose(hm_ref[p, :, rows, :].reshape(RW_HEADS * WKV_CHUNK, 128)) for p in range(N_PAIR)]
    hi, lo = _split(jnp.concatenate(tiles, axis=0)[:N_VEC * RW_HEAD])
    return jnp.concatenate([hi, lo], axis=1)


def _wkv_group(nch):
    return min(WKV_GROUP, nch)


N_STEP_VEC = N_VEC - 1
PAD_ROWS = 16


def _wkv_fwd(cols, v, e_tab):
    T = v.shape[0]
    Tc = WKV_CHUNK
    nch = T // Tc
    G = _wkv_group(nch)
    J, W = RW_HEAD, RW_WIDTH
    JS = N_STEP_VEC * J

    def body(cols_ref, v_ref, e_ref, o_ref, states_ref, sa_ref, s_ref):
        @pl.when(pl.program_id(0) == 0)
        def _():
            s_ref[...] = jnp.zeros_like(s_ref)

        st = s_ref[...]
        for c in range(G):
            lhs = _wkv_lhs(cols_ref, slice(c * Tc, (c + 1) * Tc))
            for t in range(Tc):
                row = slice(c * Tc + t, c * Tc + t + 1)
                ex = _dot(lhs[:JS], e_ref[t])
                states_ref[c * Tc + t] = st
                sa = -jnp.sum(st * ex[0:J], axis=0, keepdims=True)
                st = st + ex[J:2 * J] * sa + ex[2 * J:3 * J] * v_ref[row, :]
                sa_ref[row, :] = sa
                o_ref[row, :] = jnp.sum(st * ex[3 * J:4 * J], axis=0, keepdims=True)
            st = st * _dot(lhs[JS:], e_ref[Tc - 1])
        s_ref[...] = st

    GT = G * Tc
    return pl.pallas_call(
        body, name="wkv_fwd", grid=(nch // G,),
        in_specs=[pl.BlockSpec((N_PAIR, RW_HEADS, GT, 128), lambda c: (0, 0, c, 0)),
                  pl.BlockSpec((GT, W), lambda c: (c, 0)), _full((Tc, 2 * 128, W))],
        out_specs=[pl.BlockSpec((GT, W), lambda c: (c, 0)), pl.BlockSpec((GT, J, W), lambda c: (c, 0, 0)),
                   pl.BlockSpec((GT, W), lambda c: (c, 0))],
        out_shape=[jax.ShapeDtypeStruct((T, W), F32), jax.ShapeDtypeStruct((T, J, W), F32),
                   jax.ShapeDtypeStruct((T, W), F32)],
        scratch_shapes=[pltpu.VMEM((J, W), F32)],
        compiler_params=_cparams(dimension_semantics=("arbitrary",)),
    )(cols, v, e_tab)


def _wkv_bwd(cols, v, do, states, sa, e_tab, r_tab):
    T = v.shape[0]
    Tc = WKV_CHUNK
    nch = T // Tc
    G = _wkv_group(nch)
    ngr = nch // G
    J, W = RW_HEAD, RW_WIDTH
    JS = N_STEP_VEC * J
    blocks = [slice(b * 128, (b + 1) * 128) for b in range(W // 128)]

    def body(cols_ref, v_ref, do_ref, states_ref, sa_ref, e_ref, r_ref, dv_ref, dhm_ref, ds_ref):
        @pl.when(pl.program_id(0) == 0)
        def _():
            ds_ref[...] = jnp.zeros_like(ds_ref)

        d_carry = [ds_ref[:, b] for b in blocks]
        last = Tc - 1
        for c in reversed(range(G)):
            at = c * Tc
            rows = slice(at, at + Tc)
            lhs = _wkv_lhs(cols_ref, rows)
            ex = _dot(lhs, e_ref[last])
            dst, ends = [], []
            for i, b in enumerate(blocks):
                s_end = (states_ref[at + last, :, b] + ex[J:2 * J, b] * sa_ref[at + last:at + Tc, b]
                         + ex[2 * J:3 * J, b] * v_ref[at + last:at + Tc, b])
                ends.append((d_carry[i] * s_end).astype(BF16))
                dst.append(d_carry[i] * ex[JS:, b])
            d_decay = _dot(jnp.concatenate(ends, axis=1), r_ref[last])
            acc = jnp.zeros((JS + PAD_ROWS, 128), F32)
            for t in reversed(range(Tc)):
                row = slice(at + t, at + t + 1)
                if t != last:
                    ex = _dot(lhs[:JS], e_ref[t])
                dvs, prods = [], []
                for i, b in enumerate(blocks):
                    kk_e, b_e, k_e, r_e = (ex[n * J:(n + 1) * J, b] for n in range(N_STEP_VEC))
                    do_row, v_row, sa_row = do_ref[row, b], v_ref[row, b], sa_ref[row, b]
                    s_old = states_ref[at + t, :, b]
                    s_new = states_ref[at + t + 1, :, b] if t != last else s_old + b_e * sa_row + k_e * v_row
                    dsn = dst[i] + r_e * do_row
                    dsa = jnp.sum(dsn * b_e, axis=0, keepdims=True)
                    dvs.append(jnp.sum(dsn * k_e, axis=0, keepdims=True))
                    prods.append(jnp.concatenate(
                        [s_old * (-dsa), dsn * sa_row, dsn * v_row, s_new * do_row, jnp.zeros((PAD_ROWS, 128), F32)],
                        axis=0).astype(BF16))
                    dst[i] = dsn - kk_e * dsa
                dv_ref[row, :] = jnp.concatenate(dvs, axis=1)
                acc = acc + _dot(jnp.concatenate(prods, axis=1), r_ref[t])
            d_carry = dst
            tiles = jnp.concatenate([acc[:JS], d_decay, jnp.zeros((2 * N_PAIR * J - N_VEC * J, 128), F32)], axis=0)
            for p in range(N_PAIR):
                dhm_ref[p, :, rows, :] = jnp.transpose(tiles[p * 128:(p + 1) * 128]).reshape(RW_HEADS, Tc, 128)
        for i, b in enumerate(blocks):
            ds_ref[:, b] = d_carry[i]

    GT = G * Tc
    rev2 = lambda c: (ngr - 1 - c, 0)
    rev3 = lambda c: (ngr - 1 - c, 0, 0)
    rev_hm = lambda c: (0, 0, ngr - 1 - c, 0)
    hm_spec = pl.BlockSpec((N_PAIR, RW_HEADS, GT, 128), rev_hm)
    return pl.pallas_call(
        body, name="wkv_bwd", grid=(ngr,),
        in_specs=[hm_spec, pl.BlockSpec((GT, W), rev2), pl.BlockSpec((GT, W), rev2),
                  pl.BlockSpec((GT, J, W), rev3), pl.BlockSpec((GT, W), rev2),
                  _full((Tc, 2 * 128, W)), _full((Tc, W, 128))],
        out_specs=[pl.BlockSpec((GT, W), rev2), hm_spec],
        out_shape=[jax.ShapeDtypeStruct((T, W), F32), jax.ShapeDtypeStruct((N_PAIR, RW_HEADS, T, 128), F32)],
        scratch_shapes=[pltpu.VMEM((J, W), F32)],
        compiler_params=_cparams(dimension_semantics=("arbitrary",)),
    )(cols, v, do, states, sa, e_tab, r_tab)


def _outproj(x, y_ret, y_rw, w_out_b, target, gf):
    T = x.shape[0]
    tm = _row_tile(T)
    W = RW_WIDTH

    def body(x_ref, yr_ref, yw_ref, w_ref, t_ref, gf_ref, loss_ref, dh_ref, dy_ref, dw_ref, dgf_ref):
        @pl.when(pl.program_id(0) == 0)
        def _():
            loss_ref[...] = jnp.zeros_like(loss_ref)
            dw_ref[...] = jnp.zeros_like(dw_ref)
            dgf_ref[...] = jnp.zeros_like(dgf_ref)

        y = jnp.concatenate([yr_ref[...], yw_ref[...]], axis=1)
        w = w_ref[...]
        h = x_ref[...] + _dot(y, w)
        rstd = lax.rsqrt(jnp.mean(h * h, axis=-1, keepdims=True) + RMS_EPS)
        hn = h * rstd
        gfv = gf_ref[...]
        err = hn * gfv - t_ref[...]
        loss_ref[...] += 0.5 * jnp.sum(jnp.mean(err * err, axis=-1))
        dout = err * (1.0 / D_MODEL)
        dgf_ref[...] += jnp.sum(dout * hn, axis=0, keepdims=True)
        dhn = dout * gfv
        dh = rstd * (dhn - hn * jnp.mean(dhn * hn, axis=-1, keepdims=True))
        dh_ref[...] = dh
        dhb = dh.astype(BF16)
        dy_ref[...] = _dot_nt(dhb, w)
        dw_ref[...] += _dot_tn(y, dhb)

    return pl.pallas_call(
        body, name="outproj_loss", grid=(T // tm,),
        in_specs=[pl.BlockSpec((tm, D_MODEL), lambda i: (i, 0)), pl.BlockSpec((tm, W), lambda i: (i, 0)),
                  pl.BlockSpec((tm, W), lambda i: (i, 0)), _full((D_MODEL, D_MODEL)),
                  pl.BlockSpec((tm, D_MODEL), lambda i: (i, 0)), _full((1, D_MODEL))],
        out_specs=[_full((1, PACK_W)), pl.BlockSpec((tm, D_MODEL), lambda i: (i, 0)),
                   pl.BlockSpec((tm, D_MODEL), lambda i: (i, 0)), _full((D_MODEL, D_MODEL)), _full((1, D_MODEL))],
        out_shape=[jax.ShapeDtypeStruct((1, PACK_W), F32), jax.ShapeDtypeStruct((T, D_MODEL), F32),
                   jax.ShapeDtypeStruct((T, D_MODEL), F32), jax.ShapeDtypeStruct((D_MODEL, D_MODEL), F32),
                   jax.ShapeDtypeStruct((1, D_MODEL), F32)],
        compiler_params=_cparams(dimension_semantics=("arbitrary",)),
    )(x, y_ret, y_rw, w_out_b, target, gf)


def _inproj_bwd_x(dp_ret, dp_rw, dprev, dfirst, w_t, x, norm_g, dh):
    T = x.shape[0]
    tm = _row_tile(T)
    nt = T // tm

    def body(dpr_ref, dpw_ref, dprev_ref, dnext_ref, w_ref, x_ref, g_ref, dh_ref, gx_ref, dg_ref, dpt_ref):
        @pl.when(pl.program_id(0) == 0)
        def _():
            dg_ref[...] = jnp.zeros_like(dg_ref)

        next_row = jnp.where(pl.program_id(0) == nt - 1, 0.0, dnext_ref[0])
        dp = jnp.concatenate([dpr_ref[...], dpw_ref[...] + _shift_up(dprev_ref[...], next_row)], axis=1)
        dpt_ref[...] = jnp.transpose(dp).astype(BF16)
        du = _dot(dp.astype(BF16), w_ref[...])
        xf = x_ref[...]
        rstd = lax.rsqrt(jnp.mean(xf * xf, axis=-1, keepdims=True) + RMS_EPS)
        xn = xf * rstd
        dg_ref[...] += jnp.sum(du * xn, axis=0, keepdims=True)
        dxn = du * g_ref[...]
        gx_ref[...] = dh_ref[...] + rstd * (dxn - xn * jnp.mean(dxn * xn, axis=-1, keepdims=True))

    return pl.pallas_call(
        body, name="inproj_bwd_x", grid=(T // tm,),
        in_specs=[pl.BlockSpec((tm, RET_COLS), lambda i: (i, 0)), pl.BlockSpec((tm, RW_COLS), lambda i: (i, 0)),
                  pl.BlockSpec((tm, RW_COLS), lambda i: (i, 0)),
                  pl.BlockSpec((1, 1, RW_COLS), lambda i: (jnp.minimum(i + 1, nt - 1), 0, 0)),
                  _full((IN_COLS, D_MODEL)), pl.BlockSpec((tm, D_MODEL), lambda i: (i, 0)), _full((1, D_MODEL)),
                  pl.BlockSpec((tm, D_MODEL), lambda i: (i, 0))],
        out_specs=[pl.BlockSpec((tm, D_MODEL), lambda i: (i, 0)), _full((1, D_MODEL)),
                   pl.BlockSpec((IN_COLS, tm), lambda i: (0, i))],
        out_shape=[jax.ShapeDtypeStruct((T, D_MODEL), F32), jax.ShapeDtypeStruct((1, D_MODEL), F32),
                   jax.ShapeDtypeStruct((IN_COLS, T), BF16)],
        compiler_params=_cparams(dimension_semantics=("arbitrary",)),
    )(dp_ret, dp_rw, dprev, dfirst, w_t, x, norm_g, dh)


def _inproj_bwd_w(dp_t, u):
    T = u.shape[0]

    def body(d_ref, u_ref, o_ref):
        o_ref[...] = _dot(d_ref[...], u_ref[...])

    return pl.pallas_call(
        body, name="inproj_bwd_w", grid=(N_CHIPS,),
        in_specs=[pl.BlockSpec((IN_SHARD, T), lambda i: (i, 0)), _full((T, D_MODEL))],
        out_specs=pl.BlockSpec((IN_SHARD, D_MODEL), lambda i: (i, 0)),
        out_shape=jax.ShapeDtypeStruct((IN_COLS, D_MODEL), F32),
        compiler_params=_cparams(dimension_semantics=("arbitrary",)),
    )(dp_t, u)


def _local_step(x, target, w_in_t, w_out_b, lora, small):
    T = x.shape[0]
    tm = _row_tile(T)
    cos, sin = _rope_tables(T)
    tabs = _ret_tables()
    seg128 = _seg_matrix(RET_WIDTH, RET_DV)
    seg64 = _seg_matrix(RW_WIDTH, RW_HEAD)
    e_tab = _wkv_expand_table()
    r_tab = _wkv_reduce_table()
    prep_w = (small["rwkv_mu"], small["w0"], small["a0"], small["k_k"], small["k_a"], lora, seg64, _chunk_tables(tm))
    post_w = (small["rwkv_gn_g"], small["rwkv_gn_b"], small["r_k"], seg64)

    p_ret, p_rw, u, bnd = _inproj_fwd(x, small["norm_g"], w_in_t)
    y_ret, ret, s_in_all = _ret_fwd(p_ret, cos, sin, tabs, small["ret_gn_g"], seg128)
    hm, v, g, rk = _prep_fwd(p_rw, bnd, *prep_w)
    o, states, sa = _wkv_fwd(hm, v, e_tab)
    y_rw = _post_fwd(o, rk, v, g, *post_w)
    loss, dh, dy, d_w_out, d_gf = _outproj(x, y_ret, y_rw, w_out_b, target, small["final_norm_g"])

    do, d_rk, dv2, dg, d_gn_g, d_gn_b, d_r_k = _post_bwd(o, rk, v, g, *post_w, dy)
    dv1, d_hm = _wkv_bwd(hm, v, do, states, sa, e_tab, r_tab)
    dp_rw, dprev, dfirst, d_mu, d_w0, d_a0, d_k_k, d_k_a, d_lora = _prep_bwd(
        p_rw, bnd, *prep_w, (d_hm, d_rk, dv1, dv2, dg))
    dp_ret, d_ret_gn = _ret_bwd(p_ret, cos, sin, tabs, small["ret_gn_g"], seg128, ret, s_in_all, dy)
    grad_x, d_norm_g, dp_t = _inproj_bwd_x(dp_ret, dp_rw, dprev, dfirst, w_in_t, x, small["norm_g"], dh)
    d_w_in = _inproj_bwd_w(dp_t, u)

    d_small = {"norm_g": d_norm_g, "ret_gn_g": d_ret_gn, "rwkv_mu": d_mu, "w0": d_w0, "a0": d_a0, "k_k": d_k_k,
               "k_a": d_k_a, "r_k": d_r_k, "rwkv_gn_g": d_gn_g, "rwkv_gn_b": d_gn_b, "final_norm_g": d_gf}
    return loss, grad_x, d_w_in, d_w_out, d_lora, d_small


ANY = pl.BlockSpec(memory_space=pl.ANY)
CHIP_FLIPS = ((0, 1), (1, 0), (1, 1))
N_FLIPS = len(CHIP_FLIPS)
LORA_SHARD = RW_WIDTH // N_CHIPS
HALF_IN = IN_SHARD // 2
HALF_OUT = OUT_SHARD // 2


def _position():
    return lax.axis_index("x"), lax.axis_index("y"), lax.axis_index("c")


def _flip(v, f):
    return 1 - v if f else v


def _finish(local, remote, landed):
    for cp in landed:
        cp.wait_recv()
    for cp in remote:
        cp.wait_send()
    for cp in local:
        cp.wait()


def _gather_chips(arrs):
    n = len(arrs)

    def body(*refs):
        ins, outs = refs[:n], refs[n:2 * n]
        send, recv, pass_send, pass_recv = refs[2 * n:]
        x, y, c = _position()
        s = 2 * x + y
        sibling = (x, y, 1 - c)

        def copy(src, dst, sems, k, to):
            return pltpu.make_async_remote_copy(src_ref=src, dst_ref=dst, send_sem=sems[0].at[k], recv_sem=sems[1].at[k],
                                                device_id=to, device_id_type=MESH)

        remote, landed, passed, passed_in = [], [], [], []
        for a in range(n):
            for j, (fx, fy) in enumerate(CHIP_FLIPS):
                px, py = _flip(x, fx), _flip(y, fy)
                ps = 2 * px + py
                k = a * N_FLIPS + j
                remote.append(copy(ins[a].at[c], outs[a].at[s, c], (send, recv), k, (px, py, c)))
                landed.append(copy(ins[a].at[c], outs[a].at[ps, c], (send, recv), k, (px, py, c)))
                passed.append(copy(outs[a].at[ps, c], outs[a].at[ps, c], (pass_send, pass_recv), k, sibling))
                passed_in.append(copy(outs[a].at[ps, 1 - c], outs[a].at[ps, 1 - c], (pass_send, pass_recv), k, sibling))
        for cp in remote:
            cp.start()
        for arrived, onward in zip(landed, passed):
            arrived.wait_recv()
            onward.start()
        _finish([], remote + passed, passed_in)

    sems = pltpu.SemaphoreType.DMA((n * N_FLIPS,))
    return pl.pallas_call(
        body, name="gather_weights",
        in_specs=[ANY] * n, out_specs=[ANY] * n,
        out_shape=[jax.ShapeDtypeStruct((N_CHIPS,) + a.shape, a.dtype) for a in arrs],
        scratch_shapes=[sems, sems, sems, sems],
    )(*arrs)


def _pair_exchange(g_in, g_out, g_small):
    def body(gi_ref, go_ref, gs_ref, li_ref, lo_ref, ls_ref, send, recv):
        x, y, c = _position()
        peer = (x, y, 1 - c)
        srcs = (gi_ref.at[:, pl.ds((1 - c) * HALF_IN, HALF_IN), :], go_ref.at[:, pl.ds((1 - c) * HALF_OUT, HALF_OUT), :],
                gs_ref)
        remote = [pltpu.make_async_remote_copy(src_ref=src, dst_ref=dst, send_sem=send.at[k], recv_sem=recv.at[k],
                                               device_id=peer, device_id_type=MESH)
                  for k, (src, dst) in enumerate(zip(srcs, (li_ref, lo_ref, ls_ref)))]
        for cp in remote:
            cp.start()
        _finish([], remote, remote)

    return pl.pallas_call(
        body, name="pair_exchange",
        in_specs=[ANY] * 3, out_specs=[ANY] * 3,
        out_shape=[jax.ShapeDtypeStruct((N_CHIPS, HALF_IN, D_MODEL), F32),
                   jax.ShapeDtypeStruct((N_CHIPS, HALF_OUT, D_MODEL), F32),
                   jax.ShapeDtypeStruct(g_small.shape, F32)],
        scratch_shapes=[pltpu.SemaphoreType.DMA((3,)), pltpu.SemaphoreType.DMA((3,))],
    )(g_in, g_out, g_small)


def _pair_sum(g_in, g_out, g_small, l_in, l_out, l_small, c_arr):
    def body(c_ref, gi_ref, go_ref, gs_ref, li_ref, lo_ref, ls_ref, ci_ref, co_ref, cs_ref):
        ci_ref[...] = (gi_ref[...] + li_ref[...]).astype(BF16)
        co_ref[...] = (go_ref[...] + lo_ref[...]).astype(BF16)

        @pl.when(pl.program_id(0) == 0)
        def _():
            cs_ref[...] = gs_ref[...] + ls_ref[...]

    nd = g_small.shape
    return pl.pallas_call(
        body, name="pair_sum",
        grid_spec=pltpu.PrefetchScalarGridSpec(
            num_scalar_prefetch=1, grid=(N_CHIPS,),
            in_specs=[pl.BlockSpec((1, HALF_IN, D_MODEL), lambda s, c: (s, c[0], 0)),
                      pl.BlockSpec((1, HALF_OUT, D_MODEL), lambda s, c: (s, c[0], 0)),
                      pl.BlockSpec(nd, lambda s, c: (0, 0)),
                      pl.BlockSpec((1, HALF_IN, D_MODEL), lambda s, c: (s, 0, 0)),
                      pl.BlockSpec((1, HALF_OUT, D_MODEL), lambda s, c: (s, 0, 0)),
                      pl.BlockSpec(nd, lambda s, c: (0, 0))],
            out_specs=[pl.BlockSpec((1, HALF_IN, D_MODEL), lambda s, c: (s, 0, 0)),
                       pl.BlockSpec((1, HALF_OUT, D_MODEL), lambda s, c: (s, 0, 0)),
                       pl.BlockSpec(nd, lambda s, c: (0, 0))]),
        out_shape=[jax.ShapeDtypeStruct((N_CHIPS, HALF_IN, D_MODEL), BF16),
                   jax.ShapeDtypeStruct((N_CHIPS, HALF_OUT, D_MODEL), BF16), jax.ShapeDtypeStruct(nd, F32)],
        compiler_params=_cparams(dimension_semantics=("arbitrary",)),
    )(c_arr, g_in, g_out, g_small, l_in, l_out, l_small)


def _chip_exchange(c_in, c_out, c_small):
    def body(ci_ref, co_ref, cs_ref, li_ref, lo_ref, ls_ref, send, recv):
        x, y, c = _position()
        s = 2 * x + y
        remote = []
        for j, (fx, fy) in enumerate(CHIP_FLIPS):
            px, py = _flip(x, fx), _flip(y, fy)
            ps = 2 * px + py
            for a, (src, dst) in enumerate(((ci_ref.at[ps], li_ref.at[j]), (co_ref.at[ps], lo_ref.at[j]),
                                            (cs_ref, ls_ref.at[j]))):
                k = 3 * j + a
                remote.append(pltpu.make_async_remote_copy(src_ref=src, dst_ref=dst, send_sem=send.at[k],
                                                           recv_sem=recv.at[k], device_id=(px, py, c),
                                                           device_id_type=MESH))
        for cp in remote:
            cp.start()
        _finish([], remote, remote)

    return pl.pallas_call(
        body, name="chip_exchange",
        in_specs=[ANY] * 3, out_specs=[ANY] * 3,
        out_shape=[jax.ShapeDtypeStruct((N_FLIPS, HALF_IN, D_MODEL), c_in.dtype),
                   jax.ShapeDtypeStruct((N_FLIPS, HALF_OUT, D_MODEL), c_out.dtype),
                   jax.ShapeDtypeStruct((N_FLIPS,) + c_small.shape, F32)],
        scratch_shapes=[pltpu.SemaphoreType.DMA((3 * N_FLIPS,)), pltpu.SemaphoreType.DMA((3 * N_FLIPS,))],
    )(c_in, c_out, c_small)


def _chip_sum(g_in, g_out, p_in, p_out, c_small, l_in, l_out, l_small, sc_arr):
    nd = c_small.shape

    def body(s_ref, gi_ref, go_ref, pi_ref, po_ref, cs_ref, li0, li1, li2, lo0, lo1, lo2, ls_ref,
             ri_ref, ro_ref, rs_ref):
        ri_ref[...] = (((gi_ref[0] + pi_ref[0]) + li0[0].astype(F32)) + li1[0].astype(F32)) + li2[0].astype(F32)
        ro_ref[...] = (((go_ref[0] + po_ref[0]) + lo0[0].astype(F32)) + lo1[0].astype(F32)) + lo2[0].astype(F32)
        me = s_ref[0]
        parts = (cs_ref[...], ls_ref[0], ls_ref[1], ls_ref[2])

        def of_chip(s):
            m = jnp.bitwise_xor(me, s)
            return jnp.where(m == 0, parts[0], jnp.where(m == 1, parts[1], jnp.where(m == 2, parts[2], parts[3])))

        rs_ref[...] = ((of_chip(0) + of_chip(1)) + of_chip(2)) + of_chip(3)

    def flip_in(j):
        return pl.BlockSpec((1, HALF_IN, D_MODEL), lambda i, s: (j, 0, 0))

    def flip_out(j):
        return pl.BlockSpec((1, HALF_OUT, D_MODEL), lambda i, s: (j, 0, 0))

    return pl.pallas_call(
        body, name="chip_sum",
        grid_spec=pltpu.PrefetchScalarGridSpec(
            num_scalar_prefetch=1, grid=(1,),
            in_specs=[pl.BlockSpec((1, HALF_IN, D_MODEL), lambda i, s: (s[0], s[1], 0)),
                      pl.BlockSpec((1, HALF_OUT, D_MODEL), lambda i, s: (s[0], s[1], 0)),
                      pl.BlockSpec((1, HALF_IN, D_MODEL), lambda i, s: (s[0], 0, 0)),
                      pl.BlockSpec((1, HALF_OUT, D_MODEL), lambda i, s: (s[0], 0, 0)),
                      pl.BlockSpec(nd, lambda i, s: (0, 0)),
                      flip_in(0), flip_in(1), flip_in(2), flip_out(0), flip_out(1), flip_out(2),
                      pl.BlockSpec((N_FLIPS,) + nd, lambda i, s: (0, 0, 0))],
            out_specs=[pl.BlockSpec((HALF_IN, D_MODEL), lambda i, s: (0, 0)),
                       pl.BlockSpec((HALF_OUT, D_MODEL), lambda i, s: (0, 0)),
                       pl.BlockSpec(nd, lambda i, s: (0, 0))]),
        out_shape=[jax.ShapeDtypeStruct((HALF_IN, D_MODEL), F32), jax.ShapeDtypeStruct((HALF_OUT, D_MODEL), F32),
                   jax.ShapeDtypeStruct(nd, F32)],
        compiler_params=_cparams(dimension_semantics=("arbitrary",)),
    )(sc_arr, g_in, g_out, p_in, p_out, c_small, l_in, l_in, l_in, l_out, l_out, l_out, l_small)


def _pair_share(r_in, r_out):
    def body(ri_ref, ro_ref, li_ref, lo_ref, send, recv):
        x, y, c = _position()
        remote = [pltpu.make_async_remote_copy(src_ref=src, dst_ref=dst, send_sem=send.at[k], recv_sem=recv.at[k],
                                               device_id=(x, y, 1 - c), device_id_type=MESH)
                  for k, (src, dst) in enumerate(((ri_ref, li_ref), (ro_ref, lo_ref)))]
        for cp in remote:
            cp.start()
        _finish([], remote, remote)

    return pl.pallas_call(
        body, name="pair_share",
        in_specs=[ANY] * 2, out_specs=[ANY] * 2,
        out_shape=[jax.ShapeDtypeStruct(r_in.shape, F32), jax.ShapeDtypeStruct(r_out.shape, F32)],
        scratch_shapes=[pltpu.SemaphoreType.DMA((2,)), pltpu.SemaphoreType.DMA((2,))],
    )(r_in, r_out)


def _adam_update(w, g, m, v):
    mn = ADAM_B1 * m + (1.0 - ADAM_B1) * g
    vn = ADAM_B2 * v + (1.0 - ADAM_B2) * jnp.square(g)
    m_hat = mn / (1.0 - ADAM_B1 ** ADAM_STEP)
    v_hat = vn / (1.0 - ADAM_B2 ** ADAM_STEP)
    return -ADAM_LR * (m_hat / (jnp.sqrt(v_hat) + ADAM_EPS) + ADAM_WD * w), mn, vn


def _adamw(name, w, g_mine, g_theirs, m, v, core_arr, tr):
    rows, cols = w.shape
    per_half = rows // 2 // tr

    def body(c_ref, w_ref, gm_ref, gt_ref, m_ref, v_ref, g_ref, d_ref, nm_ref, nv_ref):
        mine = (pl.program_id(0) // per_half) == c_ref[0]
        g = jnp.where(mine, gm_ref[...], gt_ref[...])
        d, mn, vn = _adam_update(w_ref[...], g, m_ref[...], v_ref[...])
        g_ref[...] = g
        d_ref[...] = d
        nm_ref[...] = mn
        nv_ref[...] = vn

    spec = pl.BlockSpec((tr, cols), lambda i, c: (i, 0))
    half = pl.BlockSpec((tr, cols), lambda i, c: (i % per_half, 0))
    return pl.pallas_call(
        body, name=name,
        grid_spec=pltpu.PrefetchScalarGridSpec(
            num_scalar_prefetch=1, grid=(rows // tr,),
            in_specs=[spec, half, half, spec, spec], out_specs=[spec] * 4),
        out_shape=[jax.ShapeDtypeStruct((rows, cols), F32)] * 4,
        compiler_params=_cparams(dimension_semantics=("arbitrary",)),
    )(core_arr, w, g_mine, g_theirs, m, v)


def _row_pieces(n):
    return [(k, k * PACK_W, min(PACK_W, n - k * PACK_W)) for k in range(-(-n // PACK_W))]


def _pack_small(d_small, loss, d_lora):
    ns = len(SMALL_NAMES)

    def body(*refs):
        small_refs, (loss_ref, lora_ref, out_ref) = refs[:ns], refs[ns:]
        out_ref[...] = jnp.zeros_like(out_ref)
        out_ref[PACK_LORA_W:PACK_LORA_W + LORA, :] = lora_ref[:LORA, :RW_WIDTH]
        out_ref[PACK_LORA_A:PACK_LORA_A + LORA, :] = lora_ref[LORA:, RW_WIDTH:]
        for name, n, ref in zip(SMALL_NAMES, SMALL_SIZES, small_refs):
            for k, at, w in _row_pieces(n):
                out_ref[PACK_AT[name] + k:PACK_AT[name] + k + 1, 0:w] = ref[:, at:at + w]
        out_ref[PACK_LOSS:PACK_LOSS + 1, :] = loss_ref[...]

    return pl.pallas_call(body, name="pack_small", out_shape=jax.ShapeDtypeStruct((PACK_ROWS, PACK_W), F32),
                          compiler_params=_cparams())(*d_small, loss, d_lora)


def _adamw_small(tot, chip_arr, ws, ms, vs):
    ns = len(SMALL_NAMES)
    n_par = ns + 2

    def body(s_ref, tot_ref, glw_ref, gla_ref, *refs):
        w_refs, m_refs, v_refs = refs[:n_par], refs[n_par:2 * n_par], refs[2 * n_par:3 * n_par]
        outs = refs[3 * n_par:]
        g_refs, d_refs, nm_refs, nv_refs = (outs[i * n_par:(i + 1) * n_par] for i in range(4))
        grads = [jnp.concatenate([tot_ref[PACK_AT[name] + k:PACK_AT[name] + k + 1, 0:w] for k, _, w in _row_pieces(n)],
                                 axis=1) for name, n in zip(SMALL_NAMES, SMALL_SIZES)]
        grads += [glw_ref[...], gla_ref[...]]
        for i, g in enumerate(grads):
            d, mn, vn = _adam_update(w_refs[i][...], g, m_refs[i][...], v_refs[i][...])
            g_refs[i][...] = g
            d_refs[i][...] = d
            nm_refs[i][...] = mn
            nv_refs[i][...] = vn

    def whole(a):
        nd = a.ndim
        return pl.BlockSpec(a.shape, lambda i, s: (0,) * nd)

    shard = (LORA, LORA_SHARD)
    par_specs = [whole(a) for a in ws]
    res = pl.pallas_call(
        body, name="adamw_small",
        grid_spec=pltpu.PrefetchScalarGridSpec(
            num_scalar_prefetch=1, grid=(1,),
            in_specs=[whole(tot), pl.BlockSpec(shard, lambda i, s: (PACK_LORA_W // LORA, s[0])),
                      pl.BlockSpec(shard, lambda i, s: (PACK_LORA_A // LORA, s[0]))] + par_specs * 3,
            out_specs=par_specs * 4),
        out_shape=[jax.ShapeDtypeStruct(a.shape, F32) for a in ws] * 4,
        compiler_params=_cparams(dimension_semantics=("arbitrary",)),
    )(chip_arr, tot, tot, tot, *ws, *ms, *vs)
    return [res[i * n_par:(i + 1) * n_par] for i in range(4)]


def kernel(x, norm_g, w_in, ret_gn_g, rwkv_mu, w_lora_up, w0, a_lora_up, a0, k_k, k_a, r_k, rwkv_gn_g, rwkv_gn_b, w_out, final_norm_g, loss_target, m_norm_g, m_w_in, m_ret_gn_g, m_rwkv_mu, m_w_lora_up, m_w0, m_a_lora_up, m_a0, m_k_k, m_k_a, m_r_k, m_rwkv_gn_g, m_rwkv_gn_b, m_w_out, m_final_norm_g, v_norm_g, v_w_in, v_ret_gn_g, v_rwkv_mu, v_w_lora_up, v_w0, v_a_lora_up, v_a0, v_k_k, v_k_a, v_r_k, v_rwkv_gn_g, v_rwkv_gn_b, v_w_out, v_final_norm_g):
    W = RW_WIDTH
    params = dict(norm_g=norm_g, ret_gn_g=ret_gn_g, rwkv_mu=rwkv_mu, w0=w0, a0=a0, k_k=k_k, k_a=k_a, r_k=r_k,
                  rwkv_gn_g=rwkv_gn_g, rwkv_gn_b=rwkv_gn_b, final_norm_g=final_norm_g)
    moments_m = dict(norm_g=m_norm_g, ret_gn_g=m_ret_gn_g, rwkv_mu=m_rwkv_mu, w0=m_w0, a0=m_a0, k_k=m_k_k, k_a=m_k_a,
                     r_k=m_r_k, rwkv_gn_g=m_rwkv_gn_g, rwkv_gn_b=m_rwkv_gn_b, final_norm_g=m_final_norm_g)
    moments_v = dict(norm_g=v_norm_g, ret_gn_g=v_ret_gn_g, rwkv_mu=v_rwkv_mu, w0=v_w0, a0=v_a0, k_k=v_k_k, k_a=v_k_a,
                     r_k=v_r_k, rwkv_gn_g=v_rwkv_gn_g, rwkv_gn_b=v_rwkv_gn_b, final_norm_g=v_final_norm_g)
    xi, yi, ci = _position()
    chip = (2 * xi + yi).astype(jnp.int32)

    def halves(a):
        return a.reshape(2, a.shape[0] // 2, a.shape[1])

    w_t, m_t, v_t = (jnp.transpose(a[0]) for a in (w_in, m_w_in, v_w_in))
    mine = [halves(w_t.astype(BF16)), halves(w_out[0].astype(BF16)), halves(w_lora_up[0]), halves(a_lora_up[0])]
    g_in, g_out, g_lw, g_la = [lax.dynamic_update_slice(g, own[None], (chip, 0, 0, 0))
                               for g, own in zip(_gather_chips(mine), mine)]
    w_in_t = g_in.reshape(IN_COLS, D_MODEL)
    w_out_b = g_out.reshape(D_MODEL, D_MODEL)
    lw = jnp.transpose(g_lw.reshape(N_CHIPS, LORA, LORA_SHARD), (1, 0, 2)).reshape(LORA, W)
    la = jnp.transpose(g_la.reshape(N_CHIPS, LORA, LORA_SHARD), (1, 0, 2)).reshape(LORA, W)
    zero = jnp.zeros((LORA, W), F32)
    lora = jnp.concatenate([jnp.concatenate([lw, zero], axis=1), jnp.concatenate([zero, la], axis=1)], axis=0)
    small = {n: params[n].reshape(1, -1) for n in SMALL_NAMES}

    loss, grad_x, d_w_in, d_w_out, d_lora, d_small = _local_step(x[0], loss_target[0], w_in_t, w_out_b, lora, small)

    core = ci.astype(jnp.int32)
    gi = d_w_in.reshape(N_CHIPS, IN_SHARD, D_MODEL)
    go = d_w_out.reshape(N_CHIPS, OUT_SHARD, D_MODEL)
    gs = _pack_small([d_small[n] for n in SMALL_NAMES], loss, d_lora)
    p_in, p_out, p_small = _pair_exchange(gi, go, gs)
    c_in, c_out, c_small = _pair_sum(gi, go, gs, p_in, p_out, p_small, core.reshape(1))
    l_in, l_out, l_small = _chip_exchange(c_in, c_out, c_small)
    r_in, r_out, tot = _chip_sum(gi, go, p_in, p_out, c_small, l_in, l_out, l_small, jnp.stack([chip, core]))
    t_in, t_out = _pair_share(r_in, r_out)

    grad_w_in, d_in, nm_in, nv_in = (jnp.transpose(a) for a in _adamw(
        "adamw_w_in", w_t, r_in, t_in, m_t, v_t, core.reshape(1), HALF_IN // 2))
    grad_w_out, d_out, nm_out, nv_out = _adamw("adamw_w_out", w_out[0], r_out, t_out, m_w_out[0], v_w_out[0],
                                               core.reshape(1), HALF_OUT)
    par_names = SMALL_NAMES + ("w_lora_up", "a_lora_up")

    def operands(tree, lw_, la_):
        return [tree[n].reshape(1, -1) for n in SMALL_NAMES] + [lw_[0], la_[0]]

    res = _adamw_small(tot, chip.reshape(1), operands(params, w_lora_up, a_lora_up),
                       operands(moments_m, m_w_lora_up, m_a_lora_up), operands(moments_v, v_w_lora_up, v_a_lora_up))

    names = ("norm_g", "w_in", "ret_gn_g", "rwkv_mu", "w_lora_up", "w0", "a_lora_up", "a0", "k_k", "k_a", "r_k",
             "rwkv_gn_g", "rwkv_gn_b", "w_out", "final_norm_g")
    shapes = dict(w_in=w_in.shape, w_out=w_out.shape, w_lora_up=w_lora_up.shape, a_lora_up=a_lora_up.shape,
                  **{n: params[n].shape for n in SMALL_NAMES})

    def leaves(pars, big_in, big_out):
        tree = dict(zip(par_names, pars), w_in=big_in, w_out=big_out)
        return [tree[n].reshape(shapes[n]) for n in names]

    grads = leaves(res[0], grad_w_in, grad_w_out)
    deltas = leaves(res[1], d_in, d_out)
    new_m = leaves(res[2], nm_in, nm_out)
    new_v = leaves(res[3], nv_in, nv_out)
    return (tot[PACK_LOSS, 0], grad_x.reshape(x.shape), *grads, *deltas, *new_m, *new_v)
```

```python
import functools

import numpy as np
import jax
import jax.numpy as jnp
from jax import lax
from jax.experimental import pallas as pl
from jax.experimental.pallas import tpu as pltpu

F32 = jnp.float32
BF16 = jnp.bfloat16
X3 = "bf16x3"
MESH = pl.DeviceIdType.MESH

D_MODEL = 1024
N_CHIPS = 4
RET_HEADS = 4
RET_DK = 64
RET_DV = 128
RET_QK = RET_HEADS * RET_DK
RET_WIDTH = RET_HEADS * RET_DV
RET_COLS = 2 * RET_QK + 2 * RET_WIDTH
RET_CHUNK = 64
RET_GROUP = 8
RW_WIDTH = 512
RW_HEAD = 64
RW_HEADS = 8
LORA = 64
RW_COLS = 4 * RW_WIDTH + 2 * LORA
IN_COLS = RET_COLS + RW_COLS
IN_SHARD = IN_COLS // N_CHIPS
OUT_SHARD = D_MODEL // N_CHIPS
ROPE_BASE = 10000.0
RMS_EPS = 1e-6
RET_GN_EPS = 1e-5
RW_GN_EPS = 64e-5
WKV_CHUNK = 16
WKV_GROUP = 8
N_VEC = 5

ADAM_LR = 0.001
ADAM_B1 = 0.9
ADAM_B2 = 0.999
ADAM_EPS = 1e-08
ADAM_WD = 0.01
ADAM_STEP = 10

VMEM_LIMIT = 56 * 1024 * 1024

PACK_W = 512
SMALL_NAMES = ("norm_g", "ret_gn_g", "rwkv_mu", "w0", "a0", "k_k", "k_a", "r_k", "rwkv_gn_g", "rwkv_gn_b",
               "final_norm_g")
SMALL_SIZES = (1024, 512, 2176, 512, 512, 512, 512, 512, 512, 512, 1024)
PACK_LORA_W = 0
PACK_LORA_A = LORA
PACK_SMALL = 2 * LORA


def _pack_layout():
    rows, at = {}, PACK_SMALL
    for name, n in zip(SMALL_NAMES, SMALL_SIZES):
        rows[name] = at
        at += -(-n // PACK_W)
    return rows, at


PACK_AT, PACK_LOSS = _pack_layout()
PACK_ROWS = -(-(PACK_LOSS + 1) // 8) * 8


def _cparams(**kw):
    return pltpu.CompilerParams(vmem_limit_bytes=VMEM_LIMIT, **kw)


def _split(x):
    hi = x.astype(BF16)
    lo = (x - hi.astype(F32)).astype(BF16)
    return hi, lo


def _dot_dims(a, b, dims, precision):
    if precision != X3:
        return lax.dot_general(a, b, dims, preferred_element_type=F32)
    (ah, al), (bh, bl) = _split(a), _split(b)
    dot = lambda u, w: lax.dot_general(u, w, dims, preferred_element_type=F32)
    return dot(ah, bh) + dot(ah, bl) + dot(al, bh)


def _dot(a, b, precision=None):
    return _dot_dims(a, b, (((1,), (0,)), ((), ())), precision)


def _dot_nt(a, b, precision=None):
    return _dot_dims(a, b, (((1,), (1,)), ((), ())), precision)


def _dot_tn(a, b, precision=None):
    return _dot_dims(a, b, (((0,), (0,)), ((), ())), precision)


@jax.custom_vjp
def _segsum(x, seg):
    hi, lo = _split(x)
    return _dot(hi, seg) + _dot(lo, seg)


def _segsum_fwd(x, seg):
    return _segsum(x, seg), seg


def _segsum_bwd(seg, ct):
    return _segsum(ct, seg), jnp.zeros_like(seg)


_segsum.defvjp(_segsum_fwd, _segsum_bwd)


def _softplus(z):
    return jnp.maximum(z, 0.0) + jnp.log(1.0 + jnp.exp(-jnp.abs(z)))


def _full(shape):
    nd = len(shape)
    return pl.BlockSpec(shape, lambda *_: (0,) * nd)


def _rope_tables(T):
    half = RET_DK // 2
    expo = -jnp.arange(half, dtype=F32) / jnp.float32(half)
    freqs = jnp.exp(expo * jnp.float32(np.log(ROPE_BASE)))
    ang = jnp.arange(T, dtype=jnp.int32).astype(F32)[:, None] * freqs[None, :]
    cos = jnp.tile(jnp.cos(ang), (1, 2 * RET_HEADS))
    sin = jnp.tile(jnp.sin(ang), (1, 2 * RET_HEADS))
    return cos, sin


def _ret_tables():
    H, C = RET_HEADS, RET_CHUNK
    hidx = jnp.arange(H, dtype=F32)
    lg = jnp.log(1.0 - jnp.exp2(-5.0 - hidx))
    idx = jnp.arange(C, dtype=F32)
    intra = jnp.exp(lg[:, None, None] * jnp.abs(idx[:, None] - idx[None, :]))
    q_dec = jnp.transpose(jnp.exp(lg[:, None] * (idx[None, :] + 1.0)))
    k_dec = jnp.transpose(jnp.exp(lg[:, None] * (C - 1.0 - idx[None, :])))
    chunk_dec = jnp.exp(lg * C)
    qd = jnp.repeat(q_dec, RET_DK, axis=1)
    kd = jnp.repeat(k_dec, RET_DK, axis=1)
    row_h = np.arange(RET_QK) // RET_DK
    col_h = np.arange(RET_WIDTH) // RET_DV
    bm = jnp.asarray((row_h[:, None] == col_h[None, :]).astype(np.float32))
    cd = bm * jnp.repeat(chunk_dec, RET_DK)[:, None]
    return intra, qd, kd, cd, bm


def _seg_matrix(width, head):
    h = np.arange(width) // head
    return jnp.asarray((h[:, None] == h[None, :]).astype(np.float32), dtype=BF16)


def _wkv_expand_table():
    Tc = WKV_CHUNK
    k = np.arange(2 * RW_HEADS * Tc)
    kh, kt = (k % (RW_HEADS * Tc)) // Tc, k % Tc
    nh = np.arange(RW_WIDTH) // RW_HEAD
    e = (kh[None, :, None] == nh[None, None, :]) & (kt[None, :, None] == np.arange(Tc)[:, None, None])
    return jnp.asarray(e.astype(np.float32), dtype=BF16)


def _wkv_reduce_table():
    Tc = WKV_CHUNK
    kh = np.arange(RW_WIDTH) // RW_HEAD
    n = np.arange(RW_HEADS * Tc)
    nh, nt = n // Tc, n % Tc
    r = (kh[None, :, None] == nh[None, None, :]) & (nt[None, None, :] == np.arange(Tc)[:, None, None])
    return jnp.asarray(r.astype(np.float32), dtype=BF16)


def _inproj_fwd(x, norm_g, w_t):
    T = x.shape[0]
    tm = _row_tile(T)

    def body(x_ref, g_ref, w_ref, pret_ref, prw_ref, u_ref, last_ref):
        xf = x_ref[...]
        rstd = lax.rsqrt(jnp.mean(xf * xf, axis=-1, keepdims=True) + RMS_EPS)
        ub = ((xf * rstd) * g_ref[...]).astype(BF16)
        u_ref[...] = ub
        pret_ref[...] = _dot_nt(ub, w_ref[:RET_COLS, :])
        p_rw = _dot_nt(ub, w_ref[RET_COLS:, :])
        prw_ref[...] = p_rw
        last_ref[0] = p_rw[tm - 1:tm, :]

    return pl.pallas_call(
        body, name="inproj_fwd", grid=(T // tm,),
        in_specs=[pl.BlockSpec((tm, D_MODEL), lambda i: (i, 0)), _full((1, D_MODEL)), _full((IN_COLS, D_MODEL))],
        out_specs=[pl.BlockSpec((tm, RET_COLS), lambda i: (i, 0)), pl.BlockSpec((tm, RW_COLS), lambda i: (i, 0)),
                   pl.BlockSpec((tm, D_MODEL), lambda i: (i, 0)), pl.BlockSpec((1, 1, RW_COLS), lambda i: (i, 0, 0))],
        out_shape=[jax.ShapeDtypeStruct((T, RET_COLS), F32), jax.ShapeDtypeStruct((T, RW_COLS), F32),
                   jax.ShapeDtypeStruct((T, D_MODEL), BF16), jax.ShapeDtypeStruct((T // tm, 1, RW_COLS), F32)],
        compiler_params=_cparams(dimension_semantics=("arbitrary",)),
    )(x, norm_g, w_t)


def _rot_half(x):
    n = x.shape[1]
    lane = lax.broadcasted_iota(jnp.int32, x.shape, 1)
    first = (lane % RET_DK) < (RET_DK // 2)
    return jnp.where(first, -pltpu.roll(x, n - RET_DK // 2, 1), pltpu.roll(x, RET_DK // 2, 1))


def _rope(x, cos, sin):
    return x * cos + _rot_half(x) * sin


def _rope_bwd(d, cos, sin):
    return d * cos - _rot_half(d * sin)


def _ret_post(ret, g, gn_g, seg):
    mu = _segsum(ret, seg) * (1.0 / RET_DV)
    xc = ret - mu
    var = _segsum(xc * xc, seg) * (1.0 / RET_DV)
    n = xc * lax.rsqrt(var + RET_GN_EPS)
    return (g * jax.nn.sigmoid(g)) * (n * gn_g)


def _ret_scores(qt, kt, d_ref, h):
    lane = lax.broadcasted_iota(jnp.int32, qt.shape, 1)
    qh = jnp.where(lane // RET_DK == h, qt, 0.0)
    return qh, _dot_nt(qh, kt, X3) * d_ref[h]


def _ret_group(nch):
    return min(RET_GROUP, nch)


def _ret_fwd(p_ret, cos, sin, tabs, gn_g, seg128):
    T = p_ret.shape[0]
    C = RET_CHUNK
    nch = T // C
    G = _ret_group(nch)
    intra_d, qd, kd, cd, bm = tabs

    def body(q_ref, k_ref, v_ref, g_ref, cos_ref, sin_ref, qd_ref, kd_ref, d_ref, cd_ref, bm_ref, gn_ref, seg_ref,
             y_ref, ret_ref, sin_out_ref, s_ref, qt_buf, kv_buf):
        @pl.when(pl.program_id(0) == 0)
        def _():
            s_ref[...] = jnp.zeros_like(s_ref)

        cosv, sinv = cos_ref[...], sin_ref[...]
        qt_all = _rope(q_ref[...], cosv, sinv)
        kt_all = _rope(k_ref[...], cosv, sinv) * (RET_DK ** -0.5)
        for i in range(G):
            rows = slice(i * C, (i + 1) * C)
            qt, kt, v = qt_all[rows], kt_all[rows], v_ref[rows, :]
            intra = []
            for h in range(RET_HEADS):
                _, a = _ret_scores(qt, kt, d_ref, h)
                intra.append(_dot(a, v[:, h * RET_DV:(h + 1) * RET_DV], X3))
            ret_ref[rows, :] = jnp.concatenate(intra, axis=1)
            qt_buf[rows, :] = qt * qd_ref[...]
            kv_buf[i] = _dot_tn(kt * kd_ref[...], v, X3) * bm_ref[...]
        s_in = s_ref[...]
        for i in range(G):
            rows = slice(i * C, (i + 1) * C)
            sin_out_ref[i] = s_in
            ret_ref[rows, :] += _dot(qt_buf[rows, :], s_in, X3)
            s_in = s_in * cd_ref[...] + kv_buf[i]
        s_ref[...] = s_in
        y_ref[...] = _ret_post(ret_ref[...], g_ref[...], gn_ref[...], seg_ref[...]).astype(BF16)

    GC = G * C
    return pl.pallas_call(
        body, name="ret_fwd", grid=(nch // G,),
        in_specs=[pl.BlockSpec((GC, RET_QK), lambda c: (c, 0)), pl.BlockSpec((GC, RET_QK), lambda c: (c, 1)),
                  pl.BlockSpec((GC, RET_WIDTH), lambda c: (c, 1)), pl.BlockSpec((GC, RET_WIDTH), lambda c: (c, 2)),
                  pl.BlockSpec((GC, RET_QK), lambda c: (c, 0)), pl.BlockSpec((GC, RET_QK), lambda c: (c, 0)),
                  _full((C, RET_QK)), _full((C, RET_QK)), _full((RET_HEADS, C, C)),
                  _full((RET_QK, RET_WIDTH)), _full((RET_QK, RET_WIDTH)), _full((1, RET_WIDTH)),
                  _full((RET_WIDTH, RET_WIDTH))],
        out_specs=[pl.BlockSpec((GC, RET_WIDTH), lambda c: (c, 0)), pl.BlockSpec((GC, RET_WIDTH), lambda c: (c, 0)),
                   pl.BlockSpec((G, RET_QK, RET_WIDTH), lambda c: (c, 0, 0))],
        out_shape=[jax.ShapeDtypeStruct((T, RET_WIDTH), BF16), jax.ShapeDtypeStruct((T, RET_WIDTH), F32),
                   jax.ShapeDtypeStruct((nch, RET_QK, RET_WIDTH), F32)],
        scratch_shapes=[pltpu.VMEM((RET_QK, RET_WIDTH), F32), pltpu.VMEM((GC, RET_QK), F32),
                        pltpu.VMEM((G, RET_QK, RET_WIDTH), F32)],
        compiler_params=_cparams(dimension_semantics=("arbitrary",)),
    )(p_ret, p_ret, p_ret, p_ret, cos, sin, qd, kd, intra_d, cd, bm, gn_g, seg128)


def _ret_bwd(p_ret, cos, sin, tabs, gn_g, seg128, ret, s_in_all, dy):
    T = p_ret.shape[0]
    C = RET_CHUNK
    nch = T // C
    G = _ret_group(nch)
    ngr = nch // G
    intra_d, qd, kd, cd, bm = tabs

    def rev(j):
        return lambda c: (ngr - 1 - c, j)

    def body(q_ref, k_ref, v_ref, g_ref, cos_ref, sin_ref, qd_ref, kd_ref, d_ref, cd_ref, bm_ref, gn_ref, seg_ref,
             ret_ref, sin_ref_, dy_ref, dp_ref, dgn_ref, ds_ref, dkt_buf, ktk_buf, g_buf):
        @pl.when(pl.program_id(0) == 0)
        def _():
            ds_ref[...] = jnp.zeros_like(ds_ref)
            dgn_ref[...] = jnp.zeros_like(dgn_ref)

        seg = seg_ref[...]
        _, post_vjp = jax.vjp(lambda r_, g_, gn_: _ret_post(r_, g_, gn_, seg), ret_ref[...], g_ref[...], gn_ref[...])
        dret_all, dg_all, dgn = post_vjp(dy_ref[...])
        dgn_ref[...] += dgn
        v_cols = slice(2 * RET_QK, 2 * RET_QK + RET_WIDTH)
        dp_ref[:, 2 * RET_QK + RET_WIDTH:] = dg_all

        qdv, kdv = qd_ref[...], kd_ref[...]
        cosv, sinv = cos_ref[...], sin_ref[...]
        qt_all = _rope(q_ref[...], cosv, sinv)
        kt_all = _rope(k_ref[...], cosv, sinv) * (RET_DK ** -0.5)
        for i in range(G):
            rows = slice(i * C, (i + 1) * C)
            qt, kt, v, dret = qt_all[rows], kt_all[rows], v_ref[rows, :], dret_all[rows, :]
            dqt = qdv * _dot_nt(dret, sin_ref_[i], X3)
            dkt = jnp.zeros_like(kt)
            dvs = []
            for h in range(RET_HEADS):
                sl = slice(h * RET_DV, (h + 1) * RET_DV)
                qh, a = _ret_scores(qt, kt, d_ref, h)
                lane = lax.broadcasted_iota(jnp.int32, kt.shape, 1)
                kh = jnp.where(lane // RET_DK == h, kt, 0.0)
                da = _dot_nt(dret[:, sl], v[:, sl], X3) * d_ref[h]
                dvs.append(_dot_tn(a, dret[:, sl], X3))
                dqt = dqt + _dot(da, kh, X3)
                dkt = dkt + _dot_tn(da, qh, X3)
            dp_ref[rows, :RET_QK] = _rope_bwd(dqt, cosv[rows], sinv[rows])
            dp_ref[rows, v_cols] = jnp.concatenate(dvs, axis=1)
            dkt_buf[rows, :] = dkt
            ktk_buf[rows, :] = kt * kdv
            g_buf[i] = _dot_tn(qt * qdv, dret, X3) * bm_ref[...]
        ds_out = ds_ref[...]
        for i in reversed(range(G)):
            rows = slice(i * C, (i + 1) * C)
            dkt = dkt_buf[rows, :] + kdv * _dot_nt(v_ref[rows, :], ds_out, X3)
            dp_ref[rows, RET_QK:2 * RET_QK] = _rope_bwd(dkt * (RET_DK ** -0.5), cosv[rows], sinv[rows])
            dp_ref[rows, v_cols] += _dot(ktk_buf[rows, :], ds_out, X3)
            ds_out = ds_out * cd_ref[...] + g_buf[i]
        ds_ref[...] = ds_out

    GC = G * C
    return pl.pallas_call(
        body, name="ret_bwd", grid=(ngr,),
        in_specs=[pl.BlockSpec((GC, RET_QK), rev(0)), pl.BlockSpec((GC, RET_QK), rev(1)),
                  pl.BlockSpec((GC, RET_WIDTH), rev(1)), pl.BlockSpec((GC, RET_WIDTH), rev(2)),
                  pl.BlockSpec((GC, RET_QK), rev(0)), pl.BlockSpec((GC, RET_QK), rev(0)),
                  _full((C, RET_QK)), _full((C, RET_QK)), _full((RET_HEADS, C, C)),
                  _full((RET_QK, RET_WIDTH)), _full((RET_QK, RET_WIDTH)), _full((1, RET_WIDTH)),
                  _full((RET_WIDTH, RET_WIDTH)),
                  pl.BlockSpec((GC, RET_WIDTH), rev(0)),
                  pl.BlockSpec((G, RET_QK, RET_WIDTH), lambda c: (ngr - 1 - c, 0, 0)),
                  pl.BlockSpec((GC, RET_WIDTH), rev(0))],
        out_specs=[pl.BlockSpec((GC, RET_COLS), rev(0)), _full((1, RET_WIDTH))],
        out_shape=[jax.ShapeDtypeStruct((T, RET_COLS), F32), jax.ShapeDtypeStruct((1, RET_WIDTH), F32)],
        scratch_shapes=[pltpu.VMEM((RET_QK, RET_WIDTH), F32), pltpu.VMEM((GC, RET_QK), F32),
                        pltpu.VMEM((GC, RET_QK), F32), pltpu.VMEM((G, RET_QK, RET_WIDTH), F32)],
        compiler_params=_cparams(dimension_semantics=("arbitrary",)),
    )(p_ret, p_ret, p_ret, p_ret, cos, sin, qd, kd, intra_d, cd, bm, gn_g, seg128, ret, s_in_all, dy)


@jax.custom_vjp
def _chunk_sums(x, tri):
    hi, lo = _split(x)
    return _dot(tri, hi) + _dot(tri, lo)


def _chunk_sums_fwd(x, tri):
    return _chunk_sums(x, tri), tri


def _chunk_sums_bwd(tri, ct):
    hi, lo = _split(ct)
    return _dot_tn(tri, hi) + _dot_tn(tri, lo), jnp.zeros_like(tri)


_chunk_sums.defvjp(_chunk_sums_fwd, _chunk_sums_bwd)


@jax.custom_vjp
def _lora_dot(z, lora):
    return _dot(z, lora, X3)


def _lora_dot_fwd(z, lora):
    return _lora_dot(z, lora), (z, lora)


def _lora_dot_bwd(res, ct):
    z, lora = res
    return _dot_nt(ct, lora, X3), _dot_tn(z, ct, X3)


_lora_dot.defvjp(_lora_dot_fwd, _lora_dot_bwd)


def _chunk_tables(tm):
    t = np.arange(tm)
    same = (t[:, None] // WKV_CHUNK) == (t[None, :] // WKV_CHUNK)
    return jnp.asarray(np.stack([same & (t[None, :] <= t[:, None]), same]).astype(np.float32), dtype=BF16)


def _prep_fn(p, prev, mu, w0, a0, k_k, k_a, lora, seg, tri):
    W = RW_WIDTH
    ps = p + mu * (prev - p)
    r, kr, vr, g = ps[:, 0:W], ps[:, W:2 * W], ps[:, 2 * W:3 * W], ps[:, 3 * W:4 * W]
    z = ps[:, 4 * W:]
    lane = lax.broadcasted_iota(jnp.int32, z.shape, 1)
    z = jnp.where(lane < LORA, jnp.tanh(z), z)
    lo = _lora_dot(z, lora)
    w_log = -_softplus(-(w0 + lo[:, :W])) - 0.5
    log_decay = -jnp.exp(w_log)
    cum = _chunk_sums(log_decay, tri[0])
    total = _chunk_sums(log_decay, tri[1])
    a = jax.nn.sigmoid(a0 + lo[:, W:])
    kk = kr * k_k
    kk = kk / jnp.maximum(jnp.sqrt(_segsum(kk * kk, seg)), 1e-12)
    k = kr * (1.0 + (a - 1.0) * k_a)
    grow = jnp.exp(-cum)
    return kk * jnp.exp(cum - log_decay), (kk * a) * grow, k * grow, r * jnp.exp(cum), jnp.exp(total), vr, g, r * k


def _post_fn(o, rk, v, g, gn_g, gn_b, r_k, seg):
    mu = _segsum(o, seg) * (1.0 / RW_HEAD)
    oc = o - mu
    var = _segsum(oc * oc, seg) * (1.0 / RW_HEAD)
    on = oc * lax.rsqrt(var + RW_GN_EPS) * gn_g + gn_b
    bonus = _segsum(rk * r_k, seg) * v
    return (g * jax.nn.sigmoid(g)) * (on + bonus)


N_PAIR = (N_VEC + 1) // 2
HALF_LANES = 64


def _swap_halves(x):
    return pltpu.roll(x, HALF_LANES, 1)


def _pack_heads(vecs):
    tm = vecs[0].shape[0]
    low = lax.broadcasted_iota(jnp.int32, (tm, 128), 1) < HALF_LANES
    out = []
    for p in range(N_PAIR):
        a = vecs[2 * p]
        b = vecs[2 * p + 1] if 2 * p + 1 < len(vecs) else None
        heads = []
        for m in range(RW_WIDTH // 128):
            am = a[:, m * 128:(m + 1) * 128]
            bm = jnp.zeros_like(am) if b is None else b[:, m * 128:(m + 1) * 128]
            heads.append(jnp.where(low, am, _swap_halves(bm)))
            heads.append(jnp.where(low, _swap_halves(am), bm))
        out.append(heads)
    return out


def _unpack_heads(hm_ref):
    tm = hm_ref.shape[2]
    low = lax.broadcasted_iota(jnp.int32, (tm, 128), 1) < HALF_LANES
    vecs = []
    for p in range(N_PAIR):
        a, b = [], []
        for m in range(RW_WIDTH // 128):
            even, odd = hm_ref[p, 2 * m], hm_ref[p, 2 * m + 1]
            a.append(jnp.where(low, even, _swap_halves(odd)))
            b.append(jnp.where(low, _swap_halves(even), odd))
        vecs += [jnp.concatenate(a, axis=1), jnp.concatenate(b, axis=1)]
    return vecs[:N_VEC]


def _shift_down(p, first_row):
    row = lax.broadcasted_iota(jnp.int32, p.shape, 0)
    return jnp.where(row == 0, first_row, pltpu.roll(p, 1, 0))


def _shift_up(p, last_row):
    n = p.shape[0]
    row = lax.broadcasted_iota(jnp.int32, p.shape, 0)
    return jnp.where(row == n - 1, last_row, pltpu.roll(p, n - 1, 0))


def _row_tile(T):
    return min(T, 256)


def _prep_fwd(p_rw, bnd, mu, w0, a0, k_k, k_a, lora, seg64, tri):
    T = p_rw.shape[0]
    tm = _row_tile(T)
    W = RW_WIDTH

    def body(p_ref, bnd_ref, mu_ref, w0_ref, a0_ref, kk_ref, ka_ref, lora_ref, seg_ref, tri_ref,
             hm_ref, v_ref, g_ref, rk_ref):
        p = p_ref[...]
        prev = _shift_down(p, jnp.where(pl.program_id(0) == 0, 0.0, bnd_ref[0]))
        res = _prep_fn(p, prev, mu_ref[...], w0_ref[...], a0_ref[...], kk_ref[...], ka_ref[...], lora_ref[...],
                       seg_ref[...], (tri_ref[0], tri_ref[1]))
        for pair, heads in enumerate(_pack_heads(res[:N_VEC])):
            for h, val in enumerate(heads):
                hm_ref[pair, h] = val
        v_ref[...] = res[N_VEC]
        g_ref[...] = res[N_VEC + 1]
        rk_ref[...] = res[N_VEC + 2]

    small = _full((1, W))
    row = pl.BlockSpec((tm, W), lambda i: (i, 0))
    return pl.pallas_call(
        body, name="rwkv_prep_fwd", grid=(T // tm,),
        in_specs=[pl.BlockSpec((tm, RW_COLS), lambda i: (i, 0)),
                  pl.BlockSpec((1, 1, RW_COLS), lambda i: (jnp.maximum(i - 1, 0), 0, 0)),
                  _full((1, RW_COLS)), small, small, small, small, _full((2 * LORA, 2 * W)), _full((W, W)),
                  _full((2, tm, tm))],
        out_specs=[pl.BlockSpec((N_PAIR, RW_HEADS, tm, 128), lambda i: (0, 0, i, 0)), row, row, row],
        out_shape=[jax.ShapeDtypeStruct((N_PAIR, RW_HEADS, T, 128), F32)] + [jax.ShapeDtypeStruct((T, W), F32)] * 3,
        compiler_params=_cparams(dimension_semantics=("arbitrary",)),
    )(p_rw, bnd, mu, w0, a0, k_k, k_a, lora, seg64, tri)


def _prep_bwd(p_rw, bnd, mu, w0, a0, k_k, k_a, lora, seg64, tri, cts):
    T = p_rw.shape[0]
    tm = _row_tile(T)
    W = RW_WIDTH

    def body(p_ref, bnd_ref, mu_ref, w0_ref, a0_ref, kk_ref, ka_ref, lora_ref, seg_ref, tri_ref,
             dhm_ref, drk_ref, dv1_ref, dv2_ref, dg_ref,
             dp_ref, dprev_ref, dfirst_ref, dmu_ref, dw0_ref, da0_ref, dkk_p_ref, dka_ref, dlora_ref):
        accs = (dmu_ref, dw0_ref, da0_ref, dkk_p_ref, dka_ref, dlora_ref)

        @pl.when(pl.program_id(0) == 0)
        def _():
            for a_ref in accs:
                a_ref[...] = jnp.zeros_like(a_ref)

        p = p_ref[...]
        prev = _shift_down(p, jnp.where(pl.program_id(0) == 0, 0.0, bnd_ref[0]))
        seg, tri = seg_ref[...], (tri_ref[0], tri_ref[1])
        _, vjp = jax.vjp(lambda *a: _prep_fn(*a, seg, tri), p, prev, mu_ref[...], w0_ref[...], a0_ref[...],
                         kk_ref[...], ka_ref[...], lora_ref[...])
        ct = (*_unpack_heads(dhm_ref), dv1_ref[...] + dv2_ref[...], dg_ref[...], drk_ref[...])
        grads = vjp(ct)
        dp_ref[...] = grads[0]
        dprev_ref[...] = grads[1]
        dfirst_ref[0] = grads[1][0:1, :]
        for a_ref, gval in zip(accs, grads[2:]):
            a_ref[...] += gval

    small = _full((1, W))
    row = pl.BlockSpec((tm, W), lambda i: (i, 0))
    return pl.pallas_call(
        body, name="rwkv_prep_bwd", grid=(T // tm,),
        in_specs=[pl.BlockSpec((tm, RW_COLS), lambda i: (i, 0)),
                  pl.BlockSpec((1, 1, RW_COLS), lambda i: (jnp.maximum(i - 1, 0), 0, 0)),
                  _full((1, RW_COLS)), small, small, small, small, _full((2 * LORA, 2 * W)), _full((W, W)),
                  _full((2, tm, tm)), pl.BlockSpec((N_PAIR, RW_HEADS, tm, 128), lambda i: (0, 0, i, 0))] + [row] * 4,
        out_specs=[pl.BlockSpec((tm, RW_COLS), lambda i: (i, 0)), pl.BlockSpec((tm, RW_COLS), lambda i: (i, 0)),
                   pl.BlockSpec((1, 1, RW_COLS), lambda i: (i, 0, 0)),
                   _full((1, RW_COLS)), small, small, small, small, _full((2 * LORA, 2 * W))],
        out_shape=[jax.ShapeDtypeStruct((T, RW_COLS), F32), jax.ShapeDtypeStruct((T, RW_COLS), F32),
                   jax.ShapeDtypeStruct((T // tm, 1, RW_COLS), F32),
                   jax.ShapeDtypeStruct((1, RW_COLS), F32)] + [jax.ShapeDtypeStruct((1, W), F32)] * 4
                  + [jax.ShapeDtypeStruct((2 * LORA, 2 * W), F32)],
        compiler_params=_cparams(dimension_semantics=("arbitrary",)),
    )(p_rw, bnd, mu, w0, a0, k_k, k_a, lora, seg64, tri, *cts)


def _post_fwd(o, rk, v, g, gn_g, gn_b, r_k, seg64):
    T = o.shape[0]
    tm = _row_tile(T)
    W = RW_WIDTH

    def body(o_ref, rk_ref, v_ref, g_ref, gg_ref, gb_ref, rkp_ref, seg_ref, y_ref):
        y_ref[...] = _post_fn(o_ref[...], rk_ref[...], v_ref[...], g_ref[...], gg_ref[...], gb_ref[...],
                              rkp_ref[...], seg_ref[...]).astype(BF16)

    row = pl.BlockSpec((tm, W), lambda i: (i, 0))
    small = _full((1, W))
    return pl.pallas_call(
        body, name="rwkv_post_fwd", grid=(T // tm,),
        in_specs=[row] * 4 + [small] * 3 + [_full((W, W))],
        out_specs=row, out_shape=jax.ShapeDtypeStruct((T, W), BF16),
        compiler_params=_cparams(dimension_semantics=("arbitrary",)),
    )(o, rk, v, g, gn_g, gn_b, r_k, seg64)


def _post_bwd(o, rk, v, g, gn_g, gn_b, r_k, seg64, dy):
    T = o.shape[0]
    tm = _row_tile(T)
    W = RW_WIDTH

    def body(o_ref, rk_ref, v_ref, g_ref, gg_ref, gb_ref, rkp_ref, seg_ref, dy_ref,
             do_ref, drk_ref, dv_ref, dg_ref, dgg_ref, dgb_ref, drkp_ref):
        accs = (dgg_ref, dgb_ref, drkp_ref)

        @pl.when(pl.program_id(0) == 0)
        def _():
            for a_ref in accs:
                a_ref[...] = jnp.zeros_like(a_ref)

        seg = seg_ref[...]
        _, vjp = jax.vjp(lambda *a: _post_fn(*a, seg), o_ref[...], rk_ref[...], v_ref[...], g_ref[...],
                         gg_ref[...], gb_ref[...], rkp_ref[...])
        grads = vjp(dy_ref[...])
        for o_, gval in zip((do_ref, drk_ref, dv_ref, dg_ref), grads[:4]):
            o_[...] = gval
        for a_ref, gval in zip(accs, grads[4:]):
            a_ref[...] += gval

    row = pl.BlockSpec((tm, W), lambda i: (i, 0))
    small = _full((1, W))
    return pl.pallas_call(
        body, name="rwkv_post_bwd", grid=(T // tm,),
        in_specs=[row] * 4 + [small] * 3 + [_full((W, W)), pl.BlockSpec((tm, W), lambda i: (i, 1))],
        out_specs=[row] * 4 + [small] * 3,
        out_shape=[jax.ShapeDtypeStruct((T, W), F32)] * 4 + [jax.ShapeDtypeStruct((1, W), F32)] * 3,
        compiler_params=_cparams(dimension_semantics=("arbitrary",)),
    )(o, rk, v, g, gn_g, gn_b, r_k, seg64, dy)


def _wkv_lhs(hm_ref, rows):
    tiles = [jnp.transpose(hm_ref[p, :, rows, :].reshape(RW_HEADS * WKV_CHUNK, 128)) for p in range(N_PAIR)]
    hi, lo = _split(jnp.concatenate(tiles, axis=0)[:N_VEC * RW_HEAD])
    return jnp.concatenate([hi, lo], axis=1)


def _wkv_group(nch):
    return min(WKV_GROUP, nch)


N_STEP_VEC = N_VEC - 1
PAD_ROWS = 16


def _wkv_fwd(cols, v, e_tab):
    T = v.shape[0]
    Tc = WKV_CHUNK
    nch = T // Tc
    G = _wkv_group(nch)
    J, W = RW_HEAD, RW_WIDTH
    JS = N_STEP_VEC * J

    def body(cols_ref, v_ref, e_ref, o_ref, states_ref, sa_ref, s_ref):
        @pl.when(pl.program_id(0) == 0)
        def _():
            s_ref[...] = jnp.zeros_like(s_ref)

        st = s_ref[...]
        for c in range(G):
            lhs = _wkv_lhs(cols_ref, slice(c * Tc, (c + 1) * Tc))
            for t in range(Tc):
                row = slice(c * Tc + t, c * Tc + t + 1)
                ex = _dot(lhs[:JS], e_ref[t])
                states_ref[c * Tc + t] = st
                sa = -jnp.sum(st * ex[0:J], axis=0, keepdims=True)
                st = st + ex[J:2 * J] * sa + ex[2 * J:3 * J] * v_ref[row, :]
                sa_ref[row, :] = sa
                o_ref[row, :] = jnp.sum(st * ex[3 * J:4 * J], axis=0, keepdims=True)
            st = st * _dot(lhs[JS:], e_ref[Tc - 1])
        s_ref[...] = st

    GT = G * Tc
    return pl.pallas_call(
        body, name="wkv_fwd", grid=(nch // G,),
        in_specs=[pl.BlockSpec((N_PAIR, RW_HEADS, GT, 128), lambda c: (0, 0, c, 0)),
                  pl.BlockSpec((GT, W), lambda c: (c, 0)), _full((Tc, 2 * 128, W))],
        out_specs=[pl.BlockSpec((GT, W), lambda c: (c, 0)), pl.BlockSpec((GT, J, W), lambda c: (c, 0, 0)),
                   pl.BlockSpec((GT, W), lambda c: (c, 0))],
        out_shape=[jax.ShapeDtypeStruct((T, W), F32), jax.ShapeDtypeStruct((T, J, W), F32),
                   jax.ShapeDtypeStruct((T, W), F32)],
        scratch_shapes=[pltpu.VMEM((J, W), F32)],
        compiler_params=_cparams(dimension_semantics=("arbitrary",)),
    )(cols, v, e_tab)


def _wkv_bwd(cols, v, do, states, sa, e_tab, r_tab):
    T = v.shape[0]
    Tc = WKV_CHUNK
    nch = T // Tc
    G = _wkv_group(nch)
    ngr = nch // G
    J, W = RW_HEAD, RW_WIDTH
    JS = N_STEP_VEC * J
    blocks = [slice(b * 128, (b + 1) * 128) for b in range(W // 128)]

    def body(cols_ref, v_ref, do_ref, states_ref, sa_ref, e_ref, r_ref, dv_ref, dhm_ref, ds_ref):
        @pl.when(pl.program_id(0) == 0)
        def _():
            ds_ref[...] = jnp.zeros_like(ds_ref)

        d_carry = [ds_ref[:, b] for b in blocks]
        last = Tc - 1
        for c in reversed(range(G)):
            at = c * Tc
            rows = slice(at, at + Tc)
            lhs = _wkv_lhs(cols_ref, rows)
            ex = _dot(lhs, e_ref[last])
            dst, ends = [], []
            for i, b in enumerate(blocks):
                s_end = (states_ref[at + last, :, b] + ex[J:2 * J, b] * sa_ref[at + last:at + Tc, b]
                         + ex[2 * J:3 * J, b] * v_ref[at + last:at + Tc, b])
                ends.append((d_carry[i] * s_end).astype(BF16))
                dst.append(d_carry[i] * ex[JS:, b])
            d_decay = _dot(jnp.concatenate(ends, axis=1), r_ref[last])
            acc = jnp.zeros((JS + PAD_ROWS, 128), F32)
            for t in reversed(range(Tc)):
                row = slice(at + t, at + t + 1)
                if t != last:
                    ex = _dot(lhs[:JS], e_ref[t])
                dvs, prods = [], []
                for i, b in enumerate(blocks):
                    kk_e, b_e, k_e, r_e = (ex[n * J:(n + 1) * J, b] for n in range(N_STEP_VEC))
                    do_row, v_row, sa_row = do_ref[row, b], v_ref[row, b], sa_ref[row, b]
                    s_old = states_ref[at + t, :, b]
                    s_new = states_ref[at + t + 1, :, b] if t != last else s_old + b_e * sa_row + k_e * v_row
                    dsn = dst[i] + r_e * do_row
                    dsa = jnp.sum(dsn * b_e, axis=0, keepdims=True)
                    dvs.append(jnp.sum(dsn * k_e, axis=0, keepdims=True))
                    prods.append(jnp.concatenate(
                        [s_old * (-dsa), dsn * sa_row, dsn * v_row, s_new * do_row, jnp.zeros((PAD_ROWS, 128), F32)],
                        axis=0).astype(BF16))
                    dst[i] = dsn - kk_e * dsa
                dv_ref[row, :] = jnp.concatenate(dvs, axis=1)
                acc = acc + _dot(jnp.concatenate(prods, axis=1), r_ref[t])
            d_carry = dst
            tiles = jnp.concatenate([acc[:JS], d_decay, jnp.zeros((2 * N_PAIR * J - N_VEC * J, 128), F32)], axis=0)
            for p in range(N_PAIR):
                dhm_ref[p, :, rows, :] = jnp.transpose(tiles[p * 128:(p + 1) * 128]).reshape(RW_HEADS, Tc, 128)
        for i, b in enumerate(blocks):
            ds_ref[:, b] = d_carry[i]

    GT = G * Tc
    rev2 = lambda c: (ngr - 1 - c, 0)
    rev3 = lambda c: (ngr - 1 - c, 0, 0)
    rev_hm = lambda c: (0, 0, ngr - 1 - c, 0)
    hm_spec = pl.BlockSpec((N_PAIR, RW_HEADS, GT, 128), rev_hm)
    return pl.pallas_call(
        body, name="wkv_bwd", grid=(ngr,),
        in_specs=[hm_spec, pl.BlockSpec((GT, W), rev2), pl.BlockSpec((GT, W), rev2),
                  pl.BlockSpec((GT, J, W), rev3), pl.BlockSpec((GT, W), rev2),
                  _full((Tc, 2 * 128, W)), _full((Tc, W, 128))],
        out_specs=[pl.BlockSpec((GT, W), rev2), hm_spec],
        out_shape=[jax.ShapeDtypeStruct((T, W), F32), jax.ShapeDtypeStruct((N_PAIR, RW_HEADS, T, 128), F32)],
        scratch_shapes=[pltpu.VMEM((J, W), F32)],
        compiler_params=_cparams(dimension_semantics=("arbitrary",)),
    )(cols, v, do, states, sa, e_tab, r_tab)


def _outproj(x, y_ret, y_rw, w_out_b, target, gf):
    T = x.shape[0]
    tm = _row_tile(T)
    W = RW_WIDTH

    def body(x_ref, yr_ref, yw_ref, w_ref, t_ref, gf_ref, loss_ref, dh_ref, dy_ref, dw_ref, dgf_ref):
        @pl.when(pl.program_id(0) == 0)
        def _():
            loss_ref[...] = jnp.zeros_like(loss_ref)
            dw_ref[...] = jnp.zeros_like(dw_ref)
            dgf_ref[...] = jnp.zeros_like(dgf_ref)

        y = jnp.concatenate([yr_ref[...], yw_ref[...]], axis=1)
        w = w_ref[...]
        h = x_ref[...] + _dot(y, w)
        rstd = lax.rsqrt(jnp.mean(h * h, axis=-1, keepdims=True) + RMS_EPS)
        hn = h * rstd
        gfv = gf_ref[...]
        err = hn * gfv - t_ref[...]
        loss_ref[...] += 0.5 * jnp.sum(jnp.mean(err * err, axis=-1))
        dout = err * (1.0 / D_MODEL)
        dgf_ref[...] += jnp.sum(dout * hn, axis=0, keepdims=True)
        dhn = dout * gfv
        dh = rstd * (dhn - hn * jnp.mean(dhn * hn, axis=-1, keepdims=True))
        dh_ref[...] = dh
        dhb = dh.astype(BF16)
        dy_ref[...] = _dot_nt(dhb, w)
        dw_ref[...] += _dot_tn(y, dhb)

    return pl.pallas_call(
        body, name="outproj_loss", grid=(T // tm,),
        in_specs=[pl.BlockSpec((tm, D_MODEL), lambda i: (i, 0)), pl.BlockSpec((tm, W), lambda i: (i, 0)),
                  pl.BlockSpec((tm, W), lambda i: (i, 0)), _full((D_MODEL, D_MODEL)),
                  pl.BlockSpec((tm, D_MODEL), lambda i: (i, 0)), _full((1, D_MODEL))],
        out_specs=[_full((1, PACK_W)), pl.BlockSpec((tm, D_MODEL), lambda i: (i, 0)),
                   pl.BlockSpec((tm, D_MODEL), lambda i: (i, 0)), _full((D_MODEL, D_MODEL)), _full((1, D_MODEL))],
        out_shape=[jax.ShapeDtypeStruct((1, PACK_W), F32), jax.ShapeDtypeStruct((T, D_MODEL), F32),
                   jax.ShapeDtypeStruct((T, D_MODEL), F32), jax.ShapeDtypeStruct((D_MODEL, D_MODEL), F32),
                   jax.ShapeDtypeStruct((1, D_MODEL), F32)],
        compiler_params=_cparams(dimension_semantics=("arbitrary",)),
    )(x, y_ret, y_rw, w_out_b, target, gf)


def _inproj_bwd_x(dp_ret, dp_rw, dprev, dfirst, w_t, x, norm_g, dh):
    T = x.shape[0]
    tm = _row_tile(T)
    nt = T // tm

    def body(dpr_ref, dpw_ref, dprev_ref, dnext_ref, w_ref, x_ref, g_ref, dh_ref, gx_ref, dg_ref, dpt_ref):
        @pl.when(pl.program_id(0) == 0)
        def _():
            dg_ref[...] = jnp.zeros_like(dg_ref)

        next_row = jnp.where(pl.program_id(0) == nt - 1, 0.0, dnext_ref[0])
        dp = jnp.concatenate([dpr_ref[...], dpw_ref[...] + _shift_up(dprev_ref[...], next_row)], axis=1)
        dpt_ref[...] = jnp.transpose(dp).astype(BF16)
        du = _dot(dp.astype(BF16), w_ref[...])
        xf = x_ref[...]
        rstd = lax.rsqrt(jnp.mean(xf * xf, axis=-1, keepdims=True) + RMS_EPS)
        xn = xf * rstd
        dg_ref[...] += jnp.sum(du * xn, axis=0, keepdims=True)
        dxn = du * g_ref[...]
        gx_ref[...] = dh_ref[...] + rstd * (dxn - xn * jnp.mean(dxn * xn, axis=-1, keepdims=True))

    return pl.pallas_call(
        body, name="inproj_bwd_x", grid=(T // tm,),
        in_specs=[pl.BlockSpec((tm, RET_COLS), lambda i: (i, 0)), pl.BlockSpec((tm, RW_COLS), lambda i: (i, 0)),
                  pl.BlockSpec((tm, RW_COLS), lambda i: (i, 0)),
                  pl.BlockSpec((1, 1, RW_COLS), lambda i: (jnp.minimum(i + 1, nt - 1), 0, 0)),
                  _full((IN_COLS, D_MODEL)), pl.BlockSpec((tm, D_MODEL), lambda i: (i, 0)), _full((1, D_MODEL)),
                  pl.BlockSpec((tm, D_MODEL), lambda i: (i, 0))],
        out_specs=[pl.BlockSpec((tm, D_MODEL), lambda i: (i, 0)), _full((1, D_MODEL)),
                   pl.BlockSpec((IN_COLS, tm), lambda i: (0, i))],
        out_shape=[jax.ShapeDtypeStruct((T, D_MODEL), F32), jax.ShapeDtypeStruct((1, D_MODEL), F32),
                   jax.ShapeDtypeStruct((IN_COLS, T), BF16)],
        compiler_params=_cparams(dimension_semantics=("arbitrary",)),
    )(dp_ret, dp_rw, dprev, dfirst, w_t, x, norm_g, dh)


def _inproj_bwd_w(dp_t, u):
    T = u.shape[0]

    def body(d_ref, u_ref, o_ref):
        o_ref[...] = _dot(d_ref[...], u_ref[...])

    return pl.pallas_call(
        body, name="inproj_bwd_w", grid=(N_CHIPS,),
        in_specs=[pl.BlockSpec((IN_SHARD, T), lambda i: (i, 0)), _full((T, D_MODEL))],
        out_specs=pl.BlockSpec((IN_SHARD, D_MODEL), lambda i: (i, 0)),
        out_shape=jax.ShapeDtypeStruct((IN_COLS, D_MODEL), F32),
        compiler_params=_cparams(dimension_semantics=("arbitrary",)),
    )(dp_t, u)


def _local_step(x, target, w_in_t, w_out_b, lora, small):
    T = x.shape[0]
    tm = _row_tile(T)
    cos, sin = _rope_tables(T)
    tabs = _ret_tables()
    seg128 = _seg_matrix(RET_WIDTH, RET_DV)
    seg64 = _seg_matrix(RW_WIDTH, RW_HEAD)
    e_tab = _wkv_expand_table()
    r_tab = _wkv_reduce_table()
    prep_w = (small["rwkv_mu"], small["w0"], small["a0"], small["k_k"], small["k_a"], lora, seg64, _chunk_tables(tm))
    post_w = (small["rwkv_gn_g"], small["rwkv_gn_b"], small["r_k"], seg64)

    p_ret, p_rw, u, bnd = _inproj_fwd(x, small["norm_g"], w_in_t)
    y_ret, ret, s_in_all = _ret_fwd(p_ret, cos, sin, tabs, small["ret_gn_g"], seg128)
    hm, v, g, rk = _prep_fwd(p_rw, bnd, *prep_w)
    o, states, sa = _wkv_fwd(hm, v, e_tab)
    y_rw = _post_fwd(o, rk, v, g, *post_w)
    loss, dh, dy, d_w_out, d_gf = _outproj(x, y_ret, y_rw, w_out_b, target, small["final_norm_g"])

    do, d_rk, dv2, dg, d_gn_g, d_gn_b, d_r_k = _post_bwd(o, rk, v, g, *post_w, dy)
    dv1, d_hm = _wkv_bwd(hm, v, do, states, sa, e_tab, r_tab)
    dp_rw, dprev, dfirst, d_mu, d_w0, d_a0, d_k_k, d_k_a, d_lora = _prep_bwd(
        p_rw, bnd, *prep_w, (d_hm, d_rk, dv1, dv2, dg))
    dp_ret, d_ret_gn = _ret_bwd(p_ret, cos, sin, tabs, small["ret_gn_g"], seg128, ret, s_in_all, dy)
    grad_x, d_norm_g, dp_t = _inproj_bwd_x(dp_ret, dp_rw, dprev, dfirst, w_in_t, x, small["norm_g"], dh)
    d_w_in = _inproj_bwd_w(dp_t, u)

    d_small = {"norm_g": d_norm_g, "ret_gn_g": d_ret_gn, "rwkv_mu": d_mu, "w0": d_w0, "a0": d_a0, "k_k": d_k_k,
               "k_a": d_k_a, "r_k": d_r_k, "rwkv_gn_g": d_gn_g, "rwkv_gn_b": d_gn_b, "final_norm_g": d_gf}
    return loss, grad_x, d_w_in, d_w_out, d_lora, d_small


ANY = pl.BlockSpec(memory_space=pl.ANY)
CHIP_FLIPS = ((0, 1), (1, 0), (1, 1))
N_FLIPS = len(CHIP_FLIPS)
LORA_SHARD = RW_WIDTH // N_CHIPS
HALF_IN = IN_SHARD // 2
HALF_OUT = OUT_SHARD // 2


def _position():
    return lax.axis_index("x"), lax.axis_index("y"), lax.axis_index("c")


def _flip(v, f):
    return 1 - v if f else v


def _finish(local, remote, landed):
    for cp in landed:
        cp.wait_recv()
    for cp in remote:
        cp.wait_send()
    for cp in local:
        cp.wait()


def _gather_chips(arrs):
    n = len(arrs)

    def body(*refs):
        ins, outs = refs[:n], refs[n:2 * n]
        send, recv, pass_send, pass_recv = refs[2 * n:]
        x, y, c = _position()
        s = 2 * x + y
        sibling = (x, y, 1 - c)

        def copy(src, dst, sems, k, to):
            return pltpu.make_async_remote_copy(src_ref=src, dst_ref=dst, send_sem=sems[0].at[k], recv_sem=sems[1].at[k],
                                                device_id=to, device_id_type=MESH)

        remote, landed, passed, passed_in = [], [], [], []
        for a in range(n):
            for j, (fx, fy) in enumerate(CHIP_FLIPS):
                px, py = _flip(x, fx), _flip(y, fy)
                ps = 2 * px + py
                k = a * N_FLIPS + j
                remote.append(copy(ins[a].at[c], outs[a].at[s, c], (send, recv), k, (px, py, c)))
                landed.append(copy(ins[a].at[c], outs[a].at[ps, c], (send, recv), k, (px, py, c)))
                passed.append(copy(outs[a].at[ps, c], outs[a].at[ps, c], (pass_send, pass_recv), k, sibling))
                passed_in.append(copy(outs[a].at[ps, 1 - c], outs[a].at[ps, 1 - c], (pass_send, pass_recv), k, sibling))
        for cp in remote:
            cp.start()
        for arrived, onward in zip(landed, passed):
            arrived.wait_recv()
            onward.start()
        _finish([], remote + passed, passed_in)

    sems = pltpu.SemaphoreType.DMA((n * N_FLIPS,))
    return pl.pallas_call(
        body, name="gather_weights",
        in_specs=[ANY] * n, out_specs=[ANY] * n,
        out_shape=[jax.ShapeDtypeStruct((N_CHIPS,) + a.shape, a.dtype) for a in arrs],
        scratch_shapes=[sems, sems, sems, sems],
    )(*arrs)


def _pair_exchange(g_in, g_out, g_small):
    def body(gi_ref, go_ref, gs_ref, li_ref, lo_ref, ls_ref, send, recv):
        x, y, c = _position()
        peer = (x, y, 1 - c)
        srcs = (gi_ref.at[:, pl.ds((1 - c) * HALF_IN, HALF_IN), :], go_ref.at[:, pl.ds((1 - c) * HALF_OUT, HALF_OUT), :],
                gs_ref)
        remote = [pltpu.make_async_remote_copy(src_ref=src, dst_ref=dst, send_sem=send.at[k], recv_sem=recv.at[k],
                                               device_id=peer, device_id_type=MESH)
                  for k, (src, dst) in enumerate(zip(srcs, (li_ref, lo_ref, ls_ref)))]
        for cp in remote:
            cp.start()
        _finish([], remote, remote)

    return pl.pallas_call(
        body, name="pair_exchange",
        in_specs=[ANY] * 3, out_specs=[ANY] * 3,
        out_shape=[jax.ShapeDtypeStruct((N_CHIPS, HALF_IN, D_MODEL), F32),
                   jax.ShapeDtypeStruct((N_CHIPS, HALF_OUT, D_MODEL), F32),
                   jax.ShapeDtypeStruct(g_small.shape, F32)],
        scratch_shapes=[pltpu.SemaphoreType.DMA((3,)), pltpu.SemaphoreType.DMA((3,))],
    )(g_in, g_out, g_small)


def _pair_sum(g_in, g_out, g_small, l_in, l_out, l_small, c_arr):
    def body(c_ref, gi_ref, go_ref, gs_ref, li_ref, lo_ref, ls_ref, ci_ref, co_ref, cs_ref):
        ci_ref[...] = (gi_ref[...] + li_ref[...]).astype(BF16)
        co_ref[...] = (go_ref[...] + lo_ref[...]).astype(BF16)

        @pl.when(pl.program_id(0) == 0)
        def _():
            cs_ref[...] = gs_ref[...] + ls_ref[...]

    nd = g_small.shape
    return pl.pallas_call(
        body, name="pair_sum",
        grid_spec=pltpu.PrefetchScalarGridSpec(
            num_scalar_prefetch=1, grid=(N_CHIPS,),
            in_specs=[pl.BlockSpec((1, HALF_IN, D_MODEL), lambda s, c: (s, c[0], 0)),
                      pl.BlockSpec((1, HALF_OUT, D_MODEL), lambda s, c: (s, c[0], 0)),
                      pl.BlockSpec(nd, lambda s, c: (0, 0)),
                      pl.BlockSpec((1, HALF_IN, D_MODEL), lambda s, c: (s, 0, 0)),
                      pl.BlockSpec((1, HALF_OUT, D_MODEL), lambda s, c: (s, 0, 0)),
                      pl.BlockSpec(nd, lambda s, c: (0, 0))],
            out_specs=[pl.BlockSpec((1, HALF_IN, D_MODEL), lambda s, c: (s, 0, 0)),
                       pl.BlockSpec((1, HALF_OUT, D_MODEL), lambda s, c: (s, 0, 0)),
                       pl.BlockSpec(nd, lambda s, c: (0, 0))]),
        out_shape=[jax.ShapeDtypeStruct((N_CHIPS, HALF_IN, D_MODEL), BF16),
                   jax.ShapeDtypeStruct((N_CHIPS, HALF_OUT, D_MODEL), BF16), jax.ShapeDtypeStruct(nd, F32)],
        compiler_params=_cparams(dimension_semantics=("arbitrary",)),
    )(c_arr, g_in, g_out, g_small, l_in, l_out, l_small)


def _chip_exchange(c_in, c_out, c_small):
    def body(ci_ref, co_ref, cs_ref, li_ref, lo_ref, ls_ref, send, recv):
        x, y, c = _position()
        s = 2 * x + y
        remote = []
        for j, (fx, fy) in enumerate(CHIP_FLIPS):
            px, py = _flip(x, fx), _flip(y, fy)
            ps = 2 * px + py
            for a, (src, dst) in enumerate(((ci_ref.at[ps], li_ref.at[j]), (co_ref.at[ps], lo_ref.at[j]),
                                            (cs_ref, ls_ref.at[j]))):
                k = 3 * j + a
                remote.append(pltpu.make_async_remote_copy(src_ref=src, dst_ref=dst, send_sem=send.at[k],
                                                           recv_sem=recv.at[k], device_id=(px, py, c),
                                                           device_id_type=MESH))
        for cp in remote:
            cp.start()
        _finish([], remote, remote)

    return pl.pallas_call(
        body, name="chip_exchange",
        in_specs=[ANY] * 3, out_specs=[ANY] * 3,
        out_shape=[jax.ShapeDtypeStruct((N_FLIPS, HALF_IN, D_MODEL), c_in.dtype),
                   jax.ShapeDtypeStruct((N_FLIPS, HALF_OUT, D_MODEL), c_out.dtype),
                   jax.ShapeDtypeStruct((N_FLIPS,) + c_small.shape, F32)],
        scratch_shapes=[pltpu.SemaphoreType.DMA((3 * N_FLIPS,)), pltpu.SemaphoreType.DMA((3 * N_FLIPS,))],
    )(c_in, c_out, c_small)


def _chip_sum(g_in, g_out, p_in, p_out, c_small, l_in, l_out, l_small, sc_arr):
    nd = c_small.shape

    def body(s_ref, gi_ref, go_ref, pi_ref, po_ref, cs_ref, li0, li1, li2, lo0, lo1, lo2, ls_ref,
             ri_ref, ro_ref, rs_ref):
        ri_ref[...] = (((gi_ref[0] + pi_ref[0]) + li0[0].astype(F32)) + li1[0].astype(F32)) + li2[0].astype(F32)
        ro_ref[...] = (((go_ref[0] + po_ref[0]) + lo0[0].astype(F32)) + lo1[0].astype(F32)) + lo2[0].astype(F32)
        me = s_ref[0]
        parts = (cs_ref[...], ls_ref[0], ls_ref[1], ls_ref[2])

        def of_chip(s):
            m = jnp.bitwise_xor(me, s)
            return jnp.where(m == 0, parts[0], jnp.where(m == 1, parts[1], jnp.where(m == 2, parts[2], parts[3])))

        rs_ref[...] = ((of_chip(0) + of_chip(1)) + of_chip(2)) + of_chip(3)

    def flip_in(j):
        return pl.BlockSpec((1, HALF_IN, D_MODEL), lambda i, s: (j, 0, 0))

    def flip_out(j):
        return pl.BlockSpec((1, HALF_OUT, D_MODEL), lambda i, s: (j, 0, 0))

    return pl.pallas_call(
        body, name="chip_sum",
        grid_spec=pltpu.PrefetchScalarGridSpec(
            num_scalar_prefetch=1, grid=(1,),
            in_specs=[pl.BlockSpec((1, HALF_IN, D_MODEL), lambda i, s: (s[0], s[1], 0)),
                      pl.BlockSpec((1, HALF_OUT, D_MODEL), lambda i, s: (s[0], s[1], 0)),
                      pl.BlockSpec((1, HALF_IN, D_MODEL), lambda i, s: (s[0], 0, 0)),
                      pl.BlockSpec((1, HALF_OUT, D_MODEL), lambda i, s: (s[0], 0, 0)),
                      pl.BlockSpec(nd, lambda i, s: (0, 0)),
                      flip_in(0), flip_in(1), flip_in(2), flip_out(0), flip_out(1), flip_out(2),
                      pl.BlockSpec((N_FLIPS,) + nd, lambda i, s: (0, 0, 0))],
            out_specs=[pl.BlockSpec((HALF_IN, D_MODEL), lambda i, s: (0, 0)),
                       pl.BlockSpec((HALF_OUT, D_MODEL), lambda i, s: (0, 0)),
                       pl.BlockSpec(nd, lambda i, s: (0, 0))]),
        out_shape=[jax.ShapeDtypeStruct((HALF_IN, D_MODEL), F32), jax.ShapeDtypeStruct((HALF_OUT, D_MODEL), F32),
                   jax.ShapeDtypeStruct(nd, F32)],
        compiler_params=_cparams(dimension_semantics=("arbitrary",)),
    )(sc_arr, g_in, g_out, p_in, p_out, c_small, l_in, l_in, l_in, l_out, l_out, l_out, l_small)


def _pair_share(r_in, r_out):
    def body(ri_ref, ro_ref, li_ref, lo_ref, send, recv):
        x, y, c = _position()
        remote = [pltpu.make_async_remote_copy(src_ref=src, dst_ref=dst, send_sem=send.at[k], recv_sem=recv.at[k],
                                               device_id=(x, y, 1 - c), device_id_type=MESH)
                  for k, (src, dst) in enumerate(((ri_ref, li_ref), (ro_ref, lo_ref)))]
        for cp in remote:
            cp.start()
        _finish([], remote, remote)

    return pl.pallas_call(
        body, name="pair_share",
        in_specs=[ANY] * 2, out_specs=[ANY] * 2,
        out_shape=[jax.ShapeDtypeStruct(r_in.shape, F32), jax.ShapeDtypeStruct(r_out.shape, F32)],
        scratch_shapes=[pltpu.SemaphoreType.DMA((2,)), pltpu.SemaphoreType.DMA((2,))],
    )(r_in, r_out)


def _adam_update(w, g, m, v):
    mn = ADAM_B1 * m + (1.0 - ADAM_B1) * g
    vn = ADAM_B2 * v + (1.0 - ADAM_B2) * jnp.square(g)
    m_hat = mn / (1.0 - ADAM_B1 ** ADAM_STEP)
    v_hat = vn / (1.0 - ADAM_B2 ** ADAM_STEP)
    return -ADAM_LR * (m_hat / (jnp.sqrt(v_hat) + ADAM_EPS) + ADAM_WD * w), mn, vn


def _adamw(name, w, g_mine, g_theirs, m, v, core_arr, tr):
    rows, cols = w.shape
    per_half = rows // 2 // tr

    def body(c_ref, w_ref, gm_ref, gt_ref, m_ref, v_ref, g_ref, d_ref, nm_ref, nv_ref):
        mine = (pl.program_id(0) // per_half) == c_ref[0]
        g = jnp.where(mine, gm_ref[...], gt_ref[...])
        d, mn, vn = _adam_update(w_ref[...], g, m_ref[...], v_ref[...])
        g_ref[...] = g
        d_ref[...] = d
        nm_ref[...] = mn
        nv_ref[...] = vn

    spec = pl.BlockSpec((tr, cols), lambda i, c: (i, 0))
    half = pl.BlockSpec((tr, cols), lambda i, c: (i % per_half, 0))
    return pl.pallas_call(
        body, name=name,
        grid_spec=pltpu.PrefetchScalarGridSpec(
            num_scalar_prefetch=1, grid=(rows // tr,),
            in_specs=[spec, half, half, spec, spec], out_specs=[spec] * 4),
        out_shape=[jax.ShapeDtypeStruct((rows, cols), F32)] * 4,
        compiler_params=_cparams(dimension_semantics=("arbitrary",)),
    )(core_arr, w, g_mine, g_theirs, m, v)


def _row_pieces(n):
    return [(k, k * PACK_W, min(PACK_W, n - k * PACK_W)) for k in range(-(-n // PACK_W))]


def _pack_small(d_small, loss, d_lora):
    ns = len(SMALL_NAMES)

    def body(*refs):
        small_refs, (loss_ref, lora_ref, out_ref) = refs[:ns], refs[ns:]
        out_ref[...] = jnp.zeros_like(out_ref)
        out_ref[PACK_LORA_W:PACK_LORA_W + LORA, :] = lora_ref[:LORA, :RW_WIDTH]
        out_ref[PACK_LORA_A:PACK_LORA_A + LORA, :] = lora_ref[LORA:, RW_WIDTH:]
        for name, n, ref in zip(SMALL_NAMES, SMALL_SIZES, small_refs):
            for k, at, w in _row_pieces(n):
                out_ref[PACK_AT[name] + k:PACK_AT[name] + k + 1, 0:w] = ref[:, at:at + w]
        out_ref[PACK_LOSS:PACK_LOSS + 1, :] = loss_ref[...]

    return pl.pallas_call(body, name="pack_small", out_shape=jax.ShapeDtypeStruct((PACK_ROWS, PACK_W), F32),
                          compiler_params=_cparams())(*d_small, loss, d_lora)


def _adamw_small(tot, chip_arr, ws, ms, vs):
    ns = len(SMALL_NAMES)
    n_par = ns + 2

    def body(s_ref, tot_ref, glw_ref, gla_ref, *refs):
        w_refs, m_refs, v_refs = refs[:n_par], refs[n_par:2 * n_par], refs[2 * n_par:3 * n_par]
        outs = refs[3 * n_par:]
        g_refs, d_refs, nm_refs, nv_refs = (outs[i * n_par:(i + 1) * n_par] for i in range(4))
        grads = [jnp.concatenate([tot_ref[PACK_AT[name] + k:PACK_AT[name] + k + 1, 0:w] for k, _, w in _row_pieces(n)],
                                 axis=1) for name, n in zip(SMALL_NAMES, SMALL_SIZES)]
        grads += [glw_ref[...], gla_ref[...]]
        for i, g in enumerate(grads):
            d, mn, vn = _adam_update(w_refs[i][...], g, m_refs[i][...], v_refs[i][...])
            g_refs[i][...] = g
            d_refs[i][...] = d
            nm_refs[i][...] = mn
            nv_refs[i][...] = vn

    def whole(a):
        nd = a.ndim
        return pl.BlockSpec(a.shape, lambda i, s: (0,) * nd)

    shard = (LORA, LORA_SHARD)
    par_specs = [whole(a) for a in ws]
    res = pl.pallas_call(
        body, name="adamw_small",
        grid_spec=pltpu.PrefetchScalarGridSpec(
            num_scalar_prefetch=1, grid=(1,),
            in_specs=[whole(tot), pl.BlockSpec(shard, lambda i, s: (PACK_LORA_W // LORA, s[0])),
                      pl.BlockSpec(shard, lambda i, s: (PACK_LORA_A // LORA, s[0]))] + par_specs * 3,
            out_specs=par_specs * 4),
        out_shape=[jax.ShapeDtypeStruct(a.shape, F32) for a in ws] * 4,
        compiler_params=_cparams(dimension_semantics=("arbitrary",)),
    )(chip_arr, tot, tot, tot, *ws, *ms, *vs)
    return [res[i * n_par:(i + 1) * n_par] for i in range(4)]


def kernel(x, norm_g, w_in, ret_gn_g, rwkv_mu, w_lora_up, w0, a_lora_up, a0, k_k, k_a, r_k, rwkv_gn_g, rwkv_gn_b, w_out, final_norm_g, loss_target, m_norm_g, m_w_in, m_ret_gn_g, m_rwkv_mu, m_w_lora_up, m_w0, m_a_lora_up, m_a0, m_k_k, m_k_a, m_r_k, m_rwkv_gn_g, m_rwkv_gn_b, m_w_out, m_final_norm_g, v_norm_g, v_w_in, v_ret_gn_g, v_rwkv_mu, v_w_lora_up, v_w0, v_a_lora_up, v_a0, v_k_k, v_k_a, v_r_k, v_rwkv_gn_g, v_rwkv_gn_b, v_w_out, v_final_norm_g):
    W = RW_WIDTH
    params = dict(norm_g=norm_g, ret_gn_g=ret_gn_g, rwkv_mu=rwkv_mu, w0=w0, a0=a0, k_k=k_k, k_a=k_a, r_k=r_k,
                  rwkv_gn_g=rwkv_gn_g, rwkv_gn_b=rwkv_gn_b, final_norm_g=final_norm_g)
    moments_m = dict(norm_g=m_norm_g, ret_gn_g=m_ret_gn_g, rwkv_mu=m_rwkv_mu, w0=m_w0, a0=m_a0, k_k=m_k_k, k_a=m_k_a,
                     r_k=m_r_k, rwkv_gn_g=m_rwkv_gn_g, rwkv_gn_b=m_rwkv_gn_b, final_norm_g=m_final_norm_g)
    moments_v = dict(norm_g=v_norm_g, ret_gn_g=v_ret_gn_g, rwkv_mu=v_rwkv_mu, w0=v_w0, a0=v_a0, k_k=v_k_k, k_a=v_k_a,
                     r_k=v_r_k, rwkv_gn_g=v_rwkv_gn_g, rwkv_gn_b=v_rwkv_gn_b, final_norm_g=v_final_norm_g)
    xi, yi, ci = _position()
    chip = (2 * xi + yi).astype(jnp.int32)

    def halves(a):
        return a.reshape(2, a.shape[0] // 2, a.shape[1])

    w_t, m_t, v_t = (jnp.transpose(a[0]) for a in (w_in, m_w_in, v_w_in))
    mine = [halves(w_t.astype(BF16)), halves(w_out[0].astype(BF16)), halves(w_lora_up[0]), halves(a_lora_up[0])]
    g_in, g_out, g_lw, g_la = [lax.dynamic_update_slice(g, own[None], (chip, 0, 0, 0))
                               for g, own in zip(_gather_chips(mine), mine)]
    w_in_t = g_in.reshape(IN_COLS, D_MODEL)
    w_out_b = g_out.reshape(D_MODEL, D_MODEL)
    lw = jnp.transpose(g_lw.reshape(N_CHIPS, LORA, LORA_SHARD), (1, 0, 2)).reshape(LORA, W)
    la = jnp.transpose(g_la.reshape(N_CHIPS, LORA, LORA_SHARD), (1, 0, 2)).reshape(LORA, W)
    zero = jnp.zeros((LORA, W), F32)
    lora = jnp.concatenate([jnp.concatenate([lw, zero], axis=1), jnp.concatenate([zero, la], axis=1)], axis=0)
    small = {n: params[n].reshape(1, -1) for n in SMALL_NAMES}

    loss, grad_x, d_w_in, d_w_out, d_lora, d_small = _local_step(x[0], loss_target[0], w_in_t, w_out_b, lora, small)

    core = ci.astype(jnp.int32)
    gi = d_w_in.reshape(N_CHIPS, IN_SHARD, D_MODEL)
    go = d_w_out.reshape(N_CHIPS, OUT_SHARD, D_MODEL)
    gs = _pack_small([d_small[n] for n in SMALL_NAMES], loss, d_lora)
    p_in, p_out, p_small = _pair_exchange(gi, go, gs)
    c_in, c_out, c_small = _pair_sum(gi, go, gs, p_in, p_out, p_small, core.reshape(1))
    l_in, l_out, l_small = _chip_exchange(c_in, c_out, c_small)
    r_in, r_out, tot = _chip_sum(gi, go, p_in, p_out, c_small, l_in, l_out, l_small, jnp.stack([chip, core]))
    t_in, t_out = _pair_share(r_in, r_out)

    grad_w_in, d_in, nm_in, nv_in = (jnp.transpose(a) for a in _adamw(
        "adamw_w_in", w_t, r_in, t_in, m_t, v_t, core.reshape(1), HALF_IN // 2))
    grad_w_out, d_out, nm_out, nv_out = _adamw("adamw_w_out", w_out[0], r_out, t_out, m_w_out[0], v_w_out[0],
                                               core.reshape(1), HALF_OUT)
    par_names = SMALL_NAMES + ("w_lora_up", "a_lora_up")

    def operands(tree, lw_, la_):
        return [tree[n].reshape(1, -1) for n in SMALL_NAMES] + [lw_[0], la_[0]]

    res = _adamw_small(tot, chip.reshape(1), operands(params, w_lora_up, a_lora_up),
                       operands(moments_m, m_w_lora_up, m_a_lora_up), operands(moments_v, v_w_lora_up, v_a_lora_up))

    names = ("norm_g", "w_in", "ret_gn_g", "rwkv_mu", "w_lora_up", "w0", "a_lora_up", "a0", "k_k", "k_a", "r_k",
             "rwkv_gn_g", "rwkv_gn_b", "w_out", "final_norm_g")
    shapes = dict(w_in=w_in.shape, w_out=w_out.shape, w_lora_up=w_lora_up.shape, a_lora_up=a_lora_up.shape,
                  **{n: params[n].shape for n in SMALL_NAMES})

    def leaves(pars, big_in, big_out):
        tree = dict(zip(par_names, pars), w_in=big_in, w_out=big_out)
        return [tree[n].reshape(shapes[n]) for n in names]

    grads = leaves(res[0], grad_w_in, grad_w_out)
    deltas = leaves(res[1], d_in, d_out)
    new_m = leaves(res[2], nm_in, nm_out)
    new_v = leaves(res[3], nv_in, nv_out)
    return (tot[PACK_LOSS, 0], grad_x.reshape(x.shape), *grads, *deltas, *new_m, *new_v)
```

```python
import functools

import numpy as np
import jax
import jax.numpy as jnp
from jax import lax
from jax.experimental import pallas as pl
from jax.experimental.pallas import tpu as pltpu

F32 = jnp.float32
BF16 = jnp.bfloat16
X3 = "bf16x3"
B1 = "bf16"
MESH = pl.DeviceIdType.MESH

D_MODEL = 1024
N_CHIPS = 4
RET_HEADS = 4
RET_DK = 64
RET_DV = 128
RET_QK = RET_HEADS * RET_DK
RET_WIDTH = RET_HEADS * RET_DV
RET_COLS = 2 * RET_QK + 2 * RET_WIDTH
RET_CHUNK = 64
RET_GROUP = 8
RW_WIDTH = 512
RW_HEAD = 64
RW_HEADS = 8
LORA = 64
RW_COLS = 4 * RW_WIDTH + 2 * LORA
IN_COLS = RET_COLS + RW_COLS
IN_SHARD = IN_COLS // N_CHIPS
OUT_SHARD = D_MODEL // N_CHIPS
ROPE_BASE = 10000.0
RMS_EPS = 1e-6
RET_GN_EPS = 1e-5
RW_GN_EPS = 64e-5
WKV_CHUNK = 16
WKV_GROUP = 8
N_VEC = 5

ADAM_LR = 0.001
ADAM_B1 = 0.9
ADAM_B2 = 0.999
ADAM_EPS = 1e-08
ADAM_WD = 0.01
ADAM_STEP = 10

VMEM_LIMIT = 56 * 1024 * 1024

PACK_W = 512
SMALL_NAMES = ("norm_g", "ret_gn_g", "rwkv_mu", "w0", "a0", "k_k", "k_a", "r_k", "rwkv_gn_g", "rwkv_gn_b",
               "final_norm_g")
SMALL_SIZES = (1024, 512, 2176, 512, 512, 512, 512, 512, 512, 512, 1024)
PACK_LORA_W = 0
PACK_LORA_A = LORA
PACK_SMALL = 2 * LORA


def _pack_layout():
    rows, at = {}, PACK_SMALL
    for name, n in zip(SMALL_NAMES, SMALL_SIZES):
        rows[name] = at
        at += -(-n // PACK_W)
    return rows, at


PACK_AT, PACK_LOSS = _pack_layout()
PACK_ROWS = -(-(PACK_LOSS + 1) // 8) * 8


def _cparams(**kw):
    return pltpu.CompilerParams(vmem_limit_bytes=VMEM_LIMIT, **kw)


def _split(x):
    hi = x.astype(BF16)
    lo = (x - hi.astype(F32)).astype(BF16)
    return hi, lo


def _dot_dims(a, b, dims, precision):
    if precision == B1:
        a, b = a.astype(BF16), b.astype(BF16)
    if precision != X3:
        return lax.dot_general(a, b, dims, preferred_element_type=F32)
    (ah, al), (bh, bl) = _split(a), _split(b)
    dot = lambda u, w: lax.dot_general(u, w, dims, preferred_element_type=F32)
    return dot(ah, bh) + dot(ah, bl) + dot(al, bh)


def _dot(a, b, precision=None):
    return _dot_dims(a, b, (((1,), (0,)), ((), ())), precision)


def _dot_nt(a, b, precision=None):
    return _dot_dims(a, b, (((1,), (1,)), ((), ())), precision)


def _dot_tn(a, b, precision=None):
    return _dot_dims(a, b, (((0,), (0,)), ((), ())), precision)


@jax.custom_vjp
def _segsum(x, seg):
    hi, lo = _split(x)
    return _dot(hi, seg) + _dot(lo, seg)


def _segsum_fwd(x, seg):
    return _segsum(x, seg), seg


def _segsum_bwd(seg, ct):
    return _segsum(ct, seg), jnp.zeros_like(seg)


_segsum.defvjp(_segsum_fwd, _segsum_bwd)


def _softplus(z):
    return jnp.maximum(z, 0.0) + jnp.log(1.0 + jnp.exp(-jnp.abs(z)))


def _full(shape):
    nd = len(shape)
    return pl.BlockSpec(shape, lambda *_: (0,) * nd)


def _rope_tables(T):
    half = RET_DK // 2
    expo = -jnp.arange(half, dtype=F32) / jnp.float32(half)
    freqs = jnp.exp(expo * jnp.float32(np.log(ROPE_BASE)))
    ang = jnp.arange(T, dtype=jnp.int32).astype(F32)[:, None] * freqs[None, :]
    cos = jnp.tile(jnp.cos(ang), (1, 2 * RET_HEADS))
    sin = jnp.tile(jnp.sin(ang), (1, 2 * RET_HEADS))
    return cos, sin


def _ret_tables():
    H, C = RET_HEADS, RET_CHUNK
    hidx = jnp.arange(H, dtype=F32)
    lg = jnp.log(1.0 - jnp.exp2(-5.0 - hidx))
    idx = jnp.arange(C, dtype=F32)
    intra = jnp.exp(lg[:, None, None] * jnp.abs(idx[:, None] - idx[None, :]))
    q_dec = jnp.transpose(jnp.exp(lg[:, None] * (idx[None, :] + 1.0)))
    k_dec = jnp.transpose(jnp.exp(lg[:, None] * (C - 1.0 - idx[None, :])))
    chunk_dec = jnp.exp(lg * C)
    qd = jnp.repeat(q_dec, RET_DK, axis=1)
    kd = jnp.repeat(k_dec, RET_DK, axis=1)
    row_h = np.arange(RET_QK) // RET_DK
    col_h = np.arange(RET_WIDTH) // RET_DV
    bm = jnp.asarray((row_h[:, None] == col_h[None, :]).astype(np.float32))
    cd = bm * jnp.repeat(chunk_dec, RET_DK)[:, None]
    return intra, qd, kd, cd, bm


def _seg_matrix(width, head):
    h = np.arange(width) // head
    return jnp.asarray((h[:, None] == h[None, :]).astype(np.float32), dtype=BF16)


def _wkv_expand_table():
    Tc = WKV_CHUNK
    k = np.arange(2 * RW_HEADS * Tc)
    kh, kt = (k % (RW_HEADS * Tc)) // Tc, k % Tc
    nh = np.arange(RW_WIDTH) // RW_HEAD
    e = (kh[None, :, None] == nh[None, None, :]) & (kt[None, :, None] == np.arange(Tc)[:, None, None])
    return jnp.asarray(e.astype(np.float32), dtype=BF16)


def _wkv_reduce_table():
    Tc = WKV_CHUNK
    kh = np.arange(RW_WIDTH) // RW_HEAD
    n = np.arange(RW_HEADS * Tc)
    nh, nt = n // Tc, n % Tc
    r = (kh[None, :, None] == nh[None, None, :]) & (nt[None, None, :] == np.arange(Tc)[:, None, None])
    return jnp.asarray(r.astype(np.float32), dtype=BF16)


def _inproj_fwd(x, norm_g, w_t):
    T = x.shape[0]
    tm = _row_tile(T)

    def body(x_ref, g_ref, w_ref, pret_ref, prw_ref, u_ref, last_ref):
        xf = x_ref[...]
        rstd = lax.rsqrt(jnp.mean(xf * xf, axis=-1, keepdims=True) + RMS_EPS)
        ub = ((xf * rstd) * g_ref[...]).astype(BF16)
        u_ref[...] = ub
        pret_ref[...] = _dot_nt(ub, w_ref[:RET_COLS, :])
        p_rw = _dot_nt(ub, w_ref[RET_COLS:, :])
        prw_ref[...] = p_rw
        last_ref[0] = p_rw[tm - 1:tm, :]

    return pl.pallas_call(
        body, name="inproj_fwd", grid=(T // tm,),
        in_specs=[pl.BlockSpec((tm, D_MODEL), lambda i: (i, 0)), _full((1, D_MODEL)), _full((IN_COLS, D_MODEL))],
        out_specs=[pl.BlockSpec((tm, RET_COLS), lambda i: (i, 0)), pl.BlockSpec((tm, RW_COLS), lambda i: (i, 0)),
                   pl.BlockSpec((tm, D_MODEL), lambda i: (i, 0)), pl.BlockSpec((1, 1, RW_COLS), lambda i: (i, 0, 0))],
        out_shape=[jax.ShapeDtypeStruct((T, RET_COLS), F32), jax.ShapeDtypeStruct((T, RW_COLS), F32),
                   jax.ShapeDtypeStruct((T, D_MODEL), BF16), jax.ShapeDtypeStruct((T // tm, 1, RW_COLS), F32)],
        compiler_params=_cparams(dimension_semantics=("arbitrary",)),
    )(x, norm_g, w_t)


def _rot_half(x):
    n = x.shape[1]
    lane = lax.broadcasted_iota(jnp.int32, x.shape, 1)
    first = (lane % RET_DK) < (RET_DK // 2)
    return jnp.where(first, -pltpu.roll(x, n - RET_DK // 2, 1), pltpu.roll(x, RET_DK // 2, 1))


def _rope(x, cos, sin):
    return x * cos + _rot_half(x) * sin


def _rope_bwd(d, cos, sin):
    return d * cos - _rot_half(d * sin)


def _ret_post(ret, g, gn_g, seg):
    mu = _segsum(ret, seg) * (1.0 / RET_DV)
    xc = ret - mu
    var = _segsum(xc * xc, seg) * (1.0 / RET_DV)
    n = xc * lax.rsqrt(var + RET_GN_EPS)
    return (g * jax.nn.sigmoid(g)) * (n * gn_g)


def _ret_scores(qt, kt, d_ref, h):
    lane = lax.broadcasted_iota(jnp.int32, qt.shape, 1)
    qh = jnp.where(lane // RET_DK == h, qt, 0.0)
    return qh, _dot_nt(qh, kt, B1) * d_ref[h]


def _ret_group(nch):
    return min(RET_GROUP, nch)


def _ret_fwd(p_ret, cos, sin, tabs, gn_g, seg128):
    T = p_ret.shape[0]
    C = RET_CHUNK
    nch = T // C
    G = _ret_group(nch)
    intra_d, qd, kd, cd, bm = tabs

    def body(q_ref, k_ref, v_ref, g_ref, cos_ref, sin_ref, qd_ref, kd_ref, d_ref, cd_ref, bm_ref, gn_ref, seg_ref,
             y_ref, ret_ref, sin_out_ref, s_ref, qt_buf, kv_buf):
        @pl.when(pl.program_id(0) == 0)
        def _():
            s_ref[...] = jnp.zeros_like(s_ref)

        cosv, sinv = cos_ref[...], sin_ref[...]
        qt_all = _rope(q_ref[...], cosv, sinv)
        kt_all = _rope(k_ref[...], cosv, sinv) * (RET_DK ** -0.5)
        for i in range(G):
            rows = slice(i * C, (i + 1) * C)
            qt, kt, v = qt_all[rows], kt_all[rows], v_ref[rows, :]
            intra = []
            for h in range(RET_HEADS):
                _, a = _ret_scores(qt, kt, d_ref, h)
                intra.append(_dot(a, v[:, h * RET_DV:(h + 1) * RET_DV], B1))
            ret_ref[rows, :] = jnp.concatenate(intra, axis=1)
            qt_buf[rows, :] = qt * qd_ref[...]
            kv_buf[i] = _dot_tn(kt * kd_ref[...], v, B1) * bm_ref[...]
        s_in = s_ref[...]
        for i in range(G):
            rows = slice(i * C, (i + 1) * C)
            sin_out_ref[i] = s_in
            ret_ref[rows, :] += _dot(qt_buf[rows, :], s_in, B1)
            s_in = s_in * cd_ref[...] + kv_buf[i]
        s_ref[...] = s_in
        y_ref[...] = _ret_post(ret_ref[...], g_ref[...], gn_ref[...], seg_ref[...]).astype(BF16)

    GC = G * C
    return pl.pallas_call(
        body, name="ret_fwd", grid=(nch // G,),
        in_specs=[pl.BlockSpec((GC, RET_QK), lambda c: (c, 0)), pl.BlockSpec((GC, RET_QK), lambda c: (c, 1)),
                  pl.BlockSpec((GC, RET_WIDTH), lambda c: (c, 1)), pl.BlockSpec((GC, RET_WIDTH), lambda c: (c, 2)),
                  pl.BlockSpec((GC, RET_QK), lambda c: (c, 0)), pl.BlockSpec((GC, RET_QK), lambda c: (c, 0)),
                  _full((C, RET_QK)), _full((C, RET_QK)), _full((RET_HEADS, C, C)),
                  _full((RET_QK, RET_WIDTH)), _full((RET_QK, RET_WIDTH)), _full((1, RET_WIDTH)),
                  _full((RET_WIDTH, RET_WIDTH))],
        out_specs=[pl.BlockSpec((GC, RET_WIDTH), lambda c: (c, 0)), pl.BlockSpec((GC, RET_WIDTH), lambda c: (c, 0)),
                   pl.BlockSpec((G, RET_QK, RET_WIDTH), lambda c: (c, 0, 0))],
        out_shape=[jax.ShapeDtypeStruct((T, RET_WIDTH), BF16), jax.ShapeDtypeStruct((T, RET_WIDTH), F32),
                   jax.ShapeDtypeStruct((nch, RET_QK, RET_WIDTH), F32)],
        scratch_shapes=[pltpu.VMEM((RET_QK, RET_WIDTH), F32), pltpu.VMEM((GC, RET_QK), F32),
                        pltpu.VMEM((G, RET_QK, RET_WIDTH), F32)],
        compiler_params=_cparams(dimension_semantics=("arbitrary",)),
    )(p_ret, p_ret, p_ret, p_ret, cos, sin, qd, kd, intra_d, cd, bm, gn_g, seg128)


def _ret_bwd(p_ret, cos, sin, tabs, gn_g, seg128, ret, s_in_all, dy):
    T = p_ret.shape[0]
    C = RET_CHUNK
    nch = T // C
    G = _ret_group(nch)
    ngr = nch // G
    intra_d, qd, kd, cd, bm = tabs

    def rev(j):
        return lambda c: (ngr - 1 - c, j)

    def body(q_ref, k_ref, v_ref, g_ref, cos_ref, sin_ref, qd_ref, kd_ref, d_ref, cd_ref, bm_ref, gn_ref, seg_ref,
             ret_ref, sin_ref_, dy_ref, dp_ref, dgn_ref, ds_ref, dkt_buf, ktk_buf, g_buf):
        @pl.when(pl.program_id(0) == 0)
        def _():
            ds_ref[...] = jnp.zeros_like(ds_ref)
            dgn_ref[...] = jnp.zeros_like(dgn_ref)

        seg = seg_ref[...]
        _, post_vjp = jax.vjp(lambda r_, g_, gn_: _ret_post(r_, g_, gn_, seg), ret_ref[...], g_ref[...], gn_ref[...])
        dret_all, dg_all, dgn = post_vjp(dy_ref[...])
        dgn_ref[...] += dgn
        v_cols = slice(2 * RET_QK, 2 * RET_QK + RET_WIDTH)
        dp_ref[:, 2 * RET_QK + RET_WIDTH:] = dg_all

        qdv, kdv = qd_ref[...], kd_ref[...]
        cosv, sinv = cos_ref[...], sin_ref[...]
        qt_all = _rope(q_ref[...], cosv, sinv)
        kt_all = _rope(k_ref[...], cosv, sinv) * (RET_DK ** -0.5)
        for i in range(G):
            rows = slice(i * C, (i + 1) * C)
            qt, kt, v, dret = qt_all[rows], kt_all[rows], v_ref[rows, :], dret_all[rows, :]
            dqt = qdv * _dot_nt(dret, sin_ref_[i], B1)
            dkt = jnp.zeros_like(kt)
            dvs = []
            for h in range(RET_HEADS):
                sl = slice(h * RET_DV, (h + 1) * RET_DV)
                qh, a = _ret_scores(qt, kt, d_ref, h)
                lane = lax.broadcasted_iota(jnp.int32, kt.shape, 1)
                kh = jnp.where(lane // RET_DK == h, kt, 0.0)
                da = _dot_nt(dret[:, sl], v[:, sl], B1) * d_ref[h]
                dvs.append(_dot_tn(a, dret[:, sl], B1))
                dqt = dqt + _dot(da, kh, B1)
                dkt = dkt + _dot_tn(da, qh, B1)
            dp_ref[rows, :RET_QK] = _rope_bwd(dqt, cosv[rows], sinv[rows])
            dp_ref[rows, v_cols] = jnp.concatenate(dvs, axis=1)
            dkt_buf[rows, :] = dkt
            ktk_buf[rows, :] = kt * kdv
            g_buf[i] = _dot_tn(qt * qdv, dret, B1) * bm_ref[...]
        ds_out = ds_ref[...]
        for i in reversed(range(G)):
            rows = slice(i * C, (i + 1) * C)
            dkt = dkt_buf[rows, :] + kdv * _dot_nt(v_ref[rows, :], ds_out, B1)
            dp_ref[rows, RET_QK:2 * RET_QK] = _rope_bwd(dkt * (RET_DK ** -0.5), cosv[rows], sinv[rows])
            dp_ref[rows, v_cols] += _dot(ktk_buf[rows, :], ds_out, B1)
            ds_out = ds_out * cd_ref[...] + g_buf[i]
        ds_ref[...] = ds_out

    GC = G * C
    return pl.pallas_call(
        body, name="ret_bwd", grid=(ngr,),
        in_specs=[pl.BlockSpec((GC, RET_QK), rev(0)), pl.BlockSpec((GC, RET_QK), rev(1)),
                  pl.BlockSpec((GC, RET_WIDTH), rev(1)), pl.BlockSpec((GC, RET_WIDTH), rev(2)),
                  pl.BlockSpec((GC, RET_QK), rev(0)), pl.BlockSpec((GC, RET_QK), rev(0)),
                  _full((C, RET_QK)), _full((C, RET_QK)), _full((RET_HEADS, C, C)),
                  _full((RET_QK, RET_WIDTH)), _full((RET_QK, RET_WIDTH)), _full((1, RET_WIDTH)),
                  _full((RET_WIDTH, RET_WIDTH)),
                  pl.BlockSpec((GC, RET_WIDTH), rev(0)),
                  pl.BlockSpec((G, RET_QK, RET_WIDTH), lambda c: (ngr - 1 - c, 0, 0)),
                  pl.BlockSpec((GC, RET_WIDTH), rev(0))],
        out_specs=[pl.BlockSpec((GC, RET_COLS), rev(0)), _full((1, RET_WIDTH))],
        out_shape=[jax.ShapeDtypeStruct((T, RET_COLS), F32), jax.ShapeDtypeStruct((1, RET_WIDTH), F32)],
        scratch_shapes=[pltpu.VMEM((RET_QK, RET_WIDTH), F32), pltpu.VMEM((GC, RET_QK), F32),
                        pltpu.VMEM((GC, RET_QK), F32), pltpu.VMEM((G, RET_QK, RET_WIDTH), F32)],
        compiler_params=_cparams(dimension_semantics=("arbitrary",)),
    )(p_ret, p_ret, p_ret, p_ret, cos, sin, qd, kd, intra_d, cd, bm, gn_g, seg128, ret, s_in_all, dy)


@jax.custom_vjp
def _chunk_sums(x, tri):
    hi, lo = _split(x)
    return _dot(tri, hi) + _dot(tri, lo)


def _chunk_sums_fwd(x, tri):
    return _chunk_sums(x, tri), tri


def _chunk_sums_bwd(tri, ct):
    hi, lo = _split(ct)
    return _dot_tn(tri, hi) + _dot_tn(tri, lo), jnp.zeros_like(tri)


_chunk_sums.defvjp(_chunk_sums_fwd, _chunk_sums_bwd)


@jax.custom_vjp
def _lora_dot(z, lora):
    return _dot(z, lora, X3)


def _lora_dot_fwd(z, lora):
    return _lora_dot(z, lora), (z, lora)


def _lora_dot_bwd(res, ct):
    z, lora = res
    return _dot_nt(ct, lora, X3), _dot_tn(z, ct, X3)


_lora_dot.defvjp(_lora_dot_fwd, _lora_dot_bwd)


def _chunk_tables(tm):
    t = np.arange(tm)
    same = (t[:, None] // WKV_CHUNK) == (t[None, :] // WKV_CHUNK)
    return jnp.asarray(np.stack([same & (t[None, :] <= t[:, None]), same]).astype(np.float32), dtype=BF16)


def _prep_fn(p, prev, mu, w0, a0, k_k, k_a, lora, seg, tri):
    W = RW_WIDTH
    ps = p + mu * (prev - p)
    r, kr, vr, g = ps[:, 0:W], ps[:, W:2 * W], ps[:, 2 * W:3 * W], ps[:, 3 * W:4 * W]
    z = ps[:, 4 * W:]
    lane = lax.broadcasted_iota(jnp.int32, z.shape, 1)
    z = jnp.where(lane < LORA, jnp.tanh(z), z)
    lo = _lora_dot(z, lora)
    w_log = -_softplus(-(w0 + lo[:, :W])) - 0.5
    log_decay = -jnp.exp(w_log)
    cum = _chunk_sums(log_decay, tri[0])
    total = _chunk_sums(log_decay, tri[1])
    a = jax.nn.sigmoid(a0 + lo[:, W:])
    kk = kr * k_k
    kk = kk / jnp.maximum(jnp.sqrt(_segsum(kk * kk, seg)), 1e-12)
    k = kr * (1.0 + (a - 1.0) * k_a)
    grow = jnp.exp(-cum)
    return kk * jnp.exp(cum - log_decay), (kk * a) * grow, k * grow, r * jnp.exp(cum), jnp.exp(total), vr, g, r * k


def _post_fn(o, rk, v, g, gn_g, gn_b, r_k, seg):
    mu = _segsum(o, seg) * (1.0 / RW_HEAD)
    oc = o - mu
    var = _segsum(oc * oc, seg) * (1.0 / RW_HEAD)
    on = oc * lax.rsqrt(var + RW_GN_EPS) * gn_g + gn_b
    bonus = _segsum(rk * r_k, seg) * v
    return (g * jax.nn.sigmoid(g)) * (on + bonus)


N_PAIR = (N_VEC + 1) // 2
HALF_LANES = 64


def _swap_halves(x):
    return pltpu.roll(x, HALF_LANES, 1)


def _pack_heads(vecs):
    tm = vecs[0].shape[0]
    low = lax.broadcasted_iota(jnp.int32, (tm, 128), 1) < HALF_LANES
    out = []
    for p in range(N_PAIR):
        a = vecs[2 * p]
        b = vecs[2 * p + 1] if 2 * p + 1 < len(vecs) else None
        heads = []
        for m in range(RW_WIDTH // 128):
            am = a[:, m * 128:(m + 1) * 128]
            bm = jnp.zeros_like(am) if b is None else b[:, m * 128:(m + 1) * 128]
            heads.append(jnp.where(low, am, _swap_halves(bm)))
            heads.append(jnp.where(low, _swap_halves(am), bm))
        out.append(heads)
    return out


def _unpack_heads(hm_ref):
    tm = hm_ref.shape[2]
    low = lax.broadcasted_iota(jnp.int32, (tm, 128), 1) < HALF_LANES
    vecs = []
    for p in range(N_PAIR):
        a, b = [], []
        for m in range(RW_WIDTH // 128):
            even, odd = hm_ref[p, 2 * m], hm_ref[p, 2 * m + 1]
            a.append(jnp.where(low, even, _swap_halves(odd)))
            b.append(jnp.where(low, _swap_halves(even), odd))
        vecs += [jnp.concatenate(a, axis=1), jnp.concatenate(b, axis=1)]
    return vecs[:N_VEC]


def _shift_down(p, first_row):
    row = lax.broadcasted_iota(jnp.int32, p.shape, 0)
    return jnp.where(row == 0, first_row, pltpu.roll(p, 1, 0))


def _shift_up(p, last_row):
    n = p.shape[0]
    row = lax.broadcasted_iota(jnp.int32, p.shape, 0)
    return jnp.where(row == n - 1, last_row, pltpu.roll(p, n - 1, 0))


def _row_tile(T):
    return min(T, 256)


def _prep_fwd(p_rw, bnd, mu, w0, a0, k_k, k_a, lora, seg64, tri):
    T = p_rw.shape[0]
    tm = _row_tile(T)
    W = RW_WIDTH

    def body(p_ref, bnd_ref, mu_ref, w0_ref, a0_ref, kk_ref, ka_ref, lora_ref, seg_ref, tri_ref,
             hm_ref, v_ref, g_ref, rk_ref):
        p = p_ref[...]
        prev = _shift_down(p, jnp.where(pl.program_id(0) == 0, 0.0, bnd_ref[0]))
        res = _prep_fn(p, prev, mu_ref[...], w0_ref[...], a0_ref[...], kk_ref[...], ka_ref[...], lora_ref[...],
                       seg_ref[...], (tri_ref[0], tri_ref[1]))
        for pair, heads in enumerate(_pack_heads(res[:N_VEC])):
            for h, val in enumerate(heads):
                hm_ref[pair, h] = val
        v_ref[...] = res[N_VEC]
        g_ref[...] = res[N_VEC + 1]
        rk_ref[...] = res[N_VEC + 2]

    small = _full((1, W))
    row = pl.BlockSpec((tm, W), lambda i: (i, 0))
    return pl.pallas_call(
        body, name="rwkv_prep_fwd", grid=(T // tm,),
        in_specs=[pl.BlockSpec((tm, RW_COLS), lambda i: (i, 0)),
                  pl.BlockSpec((1, 1, RW_COLS), lambda i: (jnp.maximum(i - 1, 0), 0, 0)),
                  _full((1, RW_COLS)), small, small, small, small, _full((2 * LORA, 2 * W)), _full((W, W)),
                  _full((2, tm, tm))],
        out_specs=[pl.BlockSpec((N_PAIR, RW_HEADS, tm, 128), lambda i: (0, 0, i, 0)), row, row, row],
        out_shape=[jax.ShapeDtypeStruct((N_PAIR, RW_HEADS, T, 128), F32)] + [jax.ShapeDtypeStruct((T, W), F32)] * 3,
        compiler_params=_cparams(dimension_semantics=("arbitrary",)),
    )(p_rw, bnd, mu, w0, a0, k_k, k_a, lora, seg64, tri)


def _prep_bwd(p_rw, bnd, mu, w0, a0, k_k, k_a, lora, seg64, tri, cts):
    T = p_rw.shape[0]
    tm = _row_tile(T)
    W = RW_WIDTH

    def body(p_ref, bnd_ref, mu_ref, w0_ref, a0_ref, kk_ref, ka_ref, lora_ref, seg_ref, tri_ref,
             dhm_ref, drk_ref, dv1_ref, dv2_ref, dg_ref,
             dp_ref, dprev_ref, dfirst_ref, dmu_ref, dw0_ref, da0_ref, dkk_p_ref, dka_ref, dlora_ref):
        accs = (dmu_ref, dw0_ref, da0_ref, dkk_p_ref, dka_ref, dlora_ref)

        @pl.when(pl.program_id(0) == 0)
        def _():
            for a_ref in accs:
                a_ref[...] = jnp.zeros_like(a_ref)

        p = p_ref[...]
        prev = _shift_down(p, jnp.where(pl.program_id(0) == 0, 0.0, bnd_ref[0]))
        seg, tri = seg_ref[...], (tri_ref[0], tri_ref[1])
        _, vjp = jax.vjp(lambda *a: _prep_fn(*a, seg, tri), p, prev, mu_ref[...], w0_ref[...], a0_ref[...],
                         kk_ref[...], ka_ref[...], lora_ref[...])
        ct = (*_unpack_heads(dhm_ref), dv1_ref[...] + dv2_ref[...], dg_ref[...], drk_ref[...])
        grads = vjp(ct)
        dp_ref[...] = grads[0]
        dprev_ref[...] = grads[1]
        dfirst_ref[0] = grads[1][0:1, :]
        for a_ref, gval in zip(accs, grads[2:]):
            a_ref[...] += gval

    small = _full((1, W))
    row = pl.BlockSpec((tm, W), lambda i: (i, 0))
    return pl.pallas_call(
        body, name="rwkv_prep_bwd", grid=(T // tm,),
        in_specs=[pl.BlockSpec((tm, RW_COLS), lambda i: (i, 0)),
                  pl.BlockSpec((1, 1, RW_COLS), lambda i: (jnp.maximum(i - 1, 0), 0, 0)),
                  _full((1, RW_COLS)), small, small, small, small, _full((2 * LORA, 2 * W)), _full((W, W)),
                  _full((2, tm, tm)), pl.BlockSpec((N_PAIR, RW_HEADS, tm, 128), lambda i: (0, 0, i, 0))] + [row] * 4,
        out_specs=[pl.BlockSpec((tm, RW_COLS), lambda i: (i, 0)), pl.BlockSpec((tm, RW_COLS), lambda i: (i, 0)),
                   pl.BlockSpec((1, 1, RW_COLS), lambda i: (i, 0, 0)),
                   _full((1, RW_COLS)), small, small, small, small, _full((2 * LORA, 2 * W))],
        out_shape=[jax.ShapeDtypeStruct((T, RW_COLS), F32), jax.ShapeDtypeStruct((T, RW_COLS), F32),
                   jax.ShapeDtypeStruct((T // tm, 1, RW_COLS), F32),
                   jax.ShapeDtypeStruct((1, RW_COLS), F32)] + [jax.ShapeDtypeStruct((1, W), F32)] * 4
                  + [jax.ShapeDtypeStruct((2 * LORA, 2 * W), F32)],
        compiler_params=_cparams(dimension_semantics=("arbitrary",)),
    )(p_rw, bnd, mu, w0, a0, k_k, k_a, lora, seg64, tri, *cts)


def _post_fwd(o, rk, v, g, gn_g, gn_b, r_k, seg64):
    T = o.shape[0]
    tm = _row_tile(T)
    W = RW_WIDTH

    def body(o_ref, rk_ref, v_ref, g_ref, gg_ref, gb_ref, rkp_ref, seg_ref, y_ref):
        y_ref[...] = _post_fn(o_ref[...], rk_ref[...], v_ref[...], g_ref[...], gg_ref[...], gb_ref[...],
                              rkp_ref[...], seg_ref[...]).astype(BF16)

    row = pl.BlockSpec((tm, W), lambda i: (i, 0))
    small = _full((1, W))
    return pl.pallas_call(
        body, name="rwkv_post_fwd", grid=(T // tm,),
        in_specs=[row] * 4 + [small] * 3 + [_full((W, W))],
        out_specs=row, out_shape=jax.ShapeDtypeStruct((T, W), BF16),
        compiler_params=_cparams(dimension_semantics=("arbitrary",)),
    )(o, rk, v, g, gn_g, gn_b, r_k, seg64)


def _post_bwd(o, rk, v, g, gn_g, gn_b, r_k, seg64, dy):
    T = o.shape[0]
    tm = _row_tile(T)
    W = RW_WIDTH

    def body(o_ref, rk_ref, v_ref, g_ref, gg_ref, gb_ref, rkp_ref, seg_ref, dy_ref,
             do_ref, drk_ref, dv_ref, dg_ref, dgg_ref, dgb_ref, drkp_ref):
        accs = (dgg_ref, dgb_ref, drkp_ref)

        @pl.when(pl.program_id(0) == 0)
        def _():
            for a_ref in accs:
                a_ref[...] = jnp.zeros_like(a_ref)

        seg = seg_ref[...]
        _, vjp = jax.vjp(lambda *a: _post_fn(*a, seg), o_ref[...], rk_ref[...], v_ref[...], g_ref[...],
                         gg_ref[...], gb_ref[...], rkp_ref[...])
        grads = vjp(dy_ref[...])
        for o_, gval in zip((do_ref, drk_ref, dv_ref, dg_ref), grads[:4]):
            o_[...] = gval
        for a_ref, gval in zip(accs, grads[4:]):
            a_ref[...] += gval

    row = pl.BlockSpec((tm, W), lambda i: (i, 0))
    small = _full((1, W))
    return pl.pallas_call(
        body, name="rwkv_post_bwd", grid=(T // tm,),
        in_specs=[row] * 4 + [small] * 3 + [_full((W, W)), pl.BlockSpec((tm, W), lambda i: (i, 1))],
        out_specs=[row] * 4 + [small] * 3,
        out_shape=[jax.ShapeDtypeStruct((T, W), F32)] * 4 + [jax.ShapeDtypeStruct((1, W), F32)] * 3,
        compiler_params=_cparams(dimension_semantics=("arbitrary",)),
    )(o, rk, v, g, gn_g, gn_b, r_k, seg64, dy)


def _wkv_lhs(hm_ref, rows):
    tiles = [jnp.transpose(hm_ref[p, :, rows, :].reshape(RW_HEADS * WKV_CHUNK, 128)) for p in range(N_PAIR)]
    hi, lo = _split(jnp.concatenate(tiles, axis=0)[:N_VEC * RW_HEAD])
    return jnp.concatenate([hi, lo], axis=1)


def _wkv_group(nch):
    return min(WKV_GROUP, nch)


N_STEP_VEC = N_VEC - 1
PAD_ROWS = 16


def _wkv_fwd(cols, v, e_tab):
    T = v.shape[0]
    Tc = WKV_CHUNK
    nch = T // Tc
    G = _wkv_group(nch)
    J, W = RW_HEAD, RW_WIDTH
    JS = N_STEP_VEC * J

    def body(cols_ref, v_ref, e_ref, o_ref, states_ref, sa_ref, s_ref):
        @pl.when(pl.program_id(0) == 0)
        def _():
            s_ref[...] = jnp.zeros_like(s_ref)

        st = s_ref[...]
        for c in range(G):
            lhs = _wkv_lhs(cols_ref, slice(c * Tc, (c + 1) * Tc))
            for t in range(Tc):
                row = slice(c * Tc + t, c * Tc + t + 1)
                ex = _dot(lhs[:JS], e_ref[t])
                states_ref[c * Tc + t] = st
                sa = -jnp.sum(st * ex[0:J], axis=0, keepdims=True)
                st = st + ex[J:2 * J] * sa + ex[2 * J:3 * J] * v_ref[row, :]
                sa_ref[row, :] = sa
                o_ref[row, :] = jnp.sum(st * ex[3 * J:4 * J], axis=0, keepdims=True)
            st = st * _dot(lhs[JS:], e_ref[Tc - 1])
        s_ref[...] = st

    GT = G * Tc
    return pl.pallas_call(
        body, name="wkv_fwd", grid=(nch // G,),
        in_specs=[pl.BlockSpec((N_PAIR, RW_HEADS, GT, 128), lambda c: (0, 0, c, 0)),
                  pl.BlockSpec((GT, W), lambda c: (c, 0)), _full((Tc, 2 * 128, W))],
        out_specs=[pl.BlockSpec((GT, W), lambda c: (c, 0)), pl.BlockSpec((GT, J, W), lambda c: (c, 0, 0)),
                   pl.BlockSpec((GT, W), lambda c: (c, 0))],
        out_shape=[jax.ShapeDtypeStruct((T, W), F32), jax.ShapeDtypeStruct((T, J, W), F32),
                   jax.ShapeDtypeStruct((T, W), F32)],
        scratch_shapes=[pltpu.VMEM((J, W), F32)],
        compiler_params=_cparams(dimension_semantics=("arbitrary",)),
    )(cols, v, e_tab)


def _wkv_bwd(cols, v, do, states, sa, e_tab, r_tab):
    T = v.shape[0]
    Tc = WKV_CHUNK
    nch = T // Tc
    G = _wkv_group(nch)
    ngr = nch // G
    J, W = RW_HEAD, RW_WIDTH
    JS = N_STEP_VEC * J
    blocks = [slice(b * 128, (b + 1) * 128) for b in range(W // 128)]

    def body(cols_ref, v_ref, do_ref, states_ref, sa_ref, e_ref, r_ref, dv_ref, dhm_ref, ds_ref):
        @pl.when(pl.program_id(0) == 0)
        def _():
            ds_ref[...] = jnp.zeros_like(ds_ref)

        d_carry = [ds_ref[:, b] for b in blocks]
        last = Tc - 1
        for c in reversed(range(G)):
            at = c * Tc
            rows = slice(at, at + Tc)
            lhs = _wkv_lhs(cols_ref, rows)
            ex = _dot(lhs, e_ref[last])
            dst, ends = [], []
            for i, b in enumerate(blocks):
                s_end = (states_ref[at + last, :, b] + ex[J:2 * J, b] * sa_ref[at + last:at + Tc, b]
                         + ex[2 * J:3 * J, b] * v_ref[at + last:at + Tc, b])
                ends.append((d_carry[i] * s_end).astype(BF16))
                dst.append(d_carry[i] * ex[JS:, b])
            d_decay = _dot(jnp.concatenate(ends, axis=1), r_ref[last])
            acc = jnp.zeros((JS + PAD_ROWS, 128), F32)
            for t in reversed(range(Tc)):
                row = slice(at + t, at + t + 1)
                if t != last:
                    ex = _dot(lhs[:JS], e_ref[t])
                dvs, prods = [], []
                for i, b in enumerate(blocks):
                    kk_e, b_e, k_e, r_e = (ex[n * J:(n + 1) * J, b] for n in range(N_STEP_VEC))
                    do_row, v_row, sa_row = do_ref[row, b], v_ref[row, b], sa_ref[row, b]
                    s_old = states_ref[at + t, :, b]
                    s_new = states_ref[at + t + 1, :, b] if t != last else s_old + b_e * sa_row + k_e * v_row
                    dsn = dst[i] + r_e * do_row
                    dsa = jnp.sum(dsn * b_e, axis=0, keepdims=True)
                    dvs.append(jnp.sum(dsn * k_e, axis=0, keepdims=True))
                    prods.append(jnp.concatenate(
                        [s_old * (-dsa), dsn * sa_row, dsn * v_row, s_new * do_row, jnp.zeros((PAD_ROWS, 128), F32)],
                        axis=0).astype(BF16))
                    dst[i] = dsn - kk_e * dsa
                dv_ref[row, :] = jnp.concatenate(dvs, axis=1)
                acc = acc + _dot(jnp.concatenate(prods, axis=1), r_ref[t])
            d_carry = dst
            tiles = jnp.concatenate([acc[:JS], d_decay, jnp.zeros((2 * N_PAIR * J - N_VEC * J, 128), F32)], axis=0)
            for p in range(N_PAIR):
                dhm_ref[p, :, rows, :] = jnp.transpose(tiles[p * 128:(p + 1) * 128]).reshape(RW_HEADS, Tc, 128)
        for i, b in enumerate(blocks):
            ds_ref[:, b] = d_carry[i]

    GT = G * Tc
    rev2 = lambda c: (ngr - 1 - c, 0)
    rev3 = lambda c: (ngr - 1 - c, 0, 0)
    rev_hm = lambda c: (0, 0, ngr - 1 - c, 0)
    hm_spec = pl.BlockSpec((N_PAIR, RW_HEADS, GT, 128), rev_hm)
    return pl.pallas_call(
        body, name="wkv_bwd", grid=(ngr,),
        in_specs=[hm_spec, pl.BlockSpec((GT, W), rev2), pl.BlockSpec((GT, W), rev2),
                  pl.BlockSpec((GT, J, W), rev3), pl.BlockSpec((GT, W), rev2),
                  _full((Tc, 2 * 128, W)), _full((Tc, W, 128))],
        out_specs=[pl.BlockSpec((GT, W), rev2), hm_spec],
        out_shape=[jax.ShapeDtypeStruct((T, W), F32), jax.ShapeDtypeStruct((N_PAIR, RW_HEADS, T, 128), F32)],
        scratch_shapes=[pltpu.VMEM((J, W), F32)],
        compiler_params=_cparams(dimension_semantics=("arbitrary",)),
    )(cols, v, do, states, sa, e_tab, r_tab)


def _outproj(x, y_ret, y_rw, w_out_b, target, gf):
    T = x.shape[0]
    tm = _row_tile(T)
    W = RW_WIDTH

    def body(x_ref, yr_ref, yw_ref, w_ref, t_ref, gf_ref, loss_ref, dh_ref, dy_ref, dw_ref, dgf_ref):
        @pl.when(pl.program_id(0) == 0)
        def _():
            loss_ref[...] = jnp.zeros_like(loss_ref)
            dw_ref[...] = jnp.zeros_like(dw_ref)
            dgf_ref[...] = jnp.zeros_like(dgf_ref)

        y = jnp.concatenate([yr_ref[...], yw_ref[...]], axis=1)
        w = w_ref[...]
        h = x_ref[...] + _dot(y, w)
        rstd = lax.rsqrt(jnp.mean(h * h, axis=-1, keepdims=True) + RMS_EPS)
        hn = h * rstd
        gfv = gf_ref[...]
        err = hn * gfv - t_ref[...]
        loss_ref[...] += 0.5 * jnp.sum(jnp.mean(err * err, axis=-1))
        dout = err * (1.0 / D_MODEL)
        dgf_ref[...] += jnp.sum(dout * hn, axis=0, keepdims=True)
        dhn = dout * gfv
        dh = rstd * (dhn - hn * jnp.mean(dhn * hn, axis=-1, keepdims=True))
        dh_ref[...] = dh
        dhb = dh.astype(BF16)
        dy_ref[...] = _dot_nt(dhb, w)
        dw_ref[...] += _dot_tn(y, dhb)

    return pl.pallas_call(
        body, name="outproj_loss", grid=(T // tm,),
        in_specs=[pl.BlockSpec((tm, D_MODEL), lambda i: (i, 0)), pl.BlockSpec((tm, W), lambda i: (i, 0)),
                  pl.BlockSpec((tm, W), lambda i: (i, 0)), _full((D_MODEL, D_MODEL)),
                  pl.BlockSpec((tm, D_MODEL), lambda i: (i, 0)), _full((1, D_MODEL))],
        out_specs=[_full((1, PACK_W)), pl.BlockSpec((tm, D_MODEL), lambda i: (i, 0)),
                   pl.BlockSpec((tm, D_MODEL), lambda i: (i, 0)), _full((D_MODEL, D_MODEL)), _full((1, D_MODEL))],
        out_shape=[jax.ShapeDtypeStruct((1, PACK_W), F32), jax.ShapeDtypeStruct((T, D_MODEL), F32),
                   jax.ShapeDtypeStruct((T, D_MODEL), F32), jax.ShapeDtypeStruct((D_MODEL, D_MODEL), F32),
                   jax.ShapeDtypeStruct((1, D_MODEL), F32)],
        compiler_params=_cparams(dimension_semantics=("arbitrary",)),
    )(x, y_ret, y_rw, w_out_b, target, gf)


def _inproj_bwd_x(dp_ret, dp_rw, dprev, dfirst, w_t, x, norm_g, dh):
    T = x.shape[0]
    tm = _row_tile(T)
    nt = T // tm

    def body(dpr_ref, dpw_ref, dprev_ref, dnext_ref, w_ref, x_ref, g_ref, dh_ref, gx_ref, dg_ref, dpt_ref):
        @pl.when(pl.program_id(0) == 0)
        def _():
            dg_ref[...] = jnp.zeros_like(dg_ref)

        next_row = jnp.where(pl.program_id(0) == nt - 1, 0.0, dnext_ref[0])
        dp = jnp.concatenate([dpr_ref[...], dpw_ref[...] + _shift_up(dprev_ref[...], next_row)], axis=1)
        dpt_ref[...] = jnp.transpose(dp).astype(BF16)
        du = _dot(dp.astype(BF16), w_ref[...])
        xf = x_ref[...]
        rstd = lax.rsqrt(jnp.mean(xf * xf, axis=-1, keepdims=True) + RMS_EPS)
        xn = xf * rstd
        dg_ref[...] += jnp.sum(du * xn, axis=0, keepdims=True)
        dxn = du * g_ref[...]
        gx_ref[...] = dh_ref[...] + rstd * (dxn - xn * jnp.mean(dxn * xn, axis=-1, keepdims=True))

    return pl.pallas_call(
        body, name="inproj_bwd_x", grid=(T // tm,),
        in_specs=[pl.BlockSpec((tm, RET_COLS), lambda i: (i, 0)), pl.BlockSpec((tm, RW_COLS), lambda i: (i, 0)),
                  pl.BlockSpec((tm, RW_COLS), lambda i: (i, 0)),
                  pl.BlockSpec((1, 1, RW_COLS), lambda i: (jnp.minimum(i + 1, nt - 1), 0, 0)),
                  _full((IN_COLS, D_MODEL)), pl.BlockSpec((tm, D_MODEL), lambda i: (i, 0)), _full((1, D_MODEL)),
                  pl.BlockSpec((tm, D_MODEL), lambda i: (i, 0))],
        out_specs=[pl.BlockSpec((tm, D_MODEL), lambda i: (i, 0)), _full((1, D_MODEL)),
                   pl.BlockSpec((IN_COLS, tm), lambda i: (0, i))],
        out_shape=[jax.ShapeDtypeStruct((T, D_MODEL), F32), jax.ShapeDtypeStruct((1, D_MODEL), F32),
                   jax.ShapeDtypeStruct((IN_COLS, T), BF16)],
        compiler_params=_cparams(dimension_semantics=("arbitrary",)),
    )(dp_ret, dp_rw, dprev, dfirst, w_t, x, norm_g, dh)


def _inproj_bwd_w(dp_t, u):
    T = u.shape[0]

    def body(d_ref, u_ref, o_ref):
        o_ref[...] = _dot(d_ref[...], u_ref[...])

    return pl.pallas_call(
        body, name="inproj_bwd_w", grid=(N_CHIPS,),
        in_specs=[pl.BlockSpec((IN_SHARD, T), lambda i: (i, 0)), _full((T, D_MODEL))],
        out_specs=pl.BlockSpec((IN_SHARD, D_MODEL), lambda i: (i, 0)),
        out_shape=jax.ShapeDtypeStruct((IN_COLS, D_MODEL), F32),
        compiler_params=_cparams(dimension_semantics=("arbitrary",)),
    )(dp_t, u)


def _local_step(x, target, w_in_t, w_out_b, lora, small):
    T = x.shape[0]
    tm = _row_tile(T)
    cos, sin = _rope_tables(T)
    tabs = _ret_tables()
    seg128 = _seg_matrix(RET_WIDTH, RET_DV)
    seg64 = _seg_matrix(RW_WIDTH, RW_HEAD)
    e_tab = _wkv_expand_table()
    r_tab = _wkv_reduce_table()
    prep_w = (small["rwkv_mu"], small["w0"], small["a0"], small["k_k"], small["k_a"], lora, seg64, _chunk_tables(tm))
    post_w = (small["rwkv_gn_g"], small["rwkv_gn_b"], small["r_k"], seg64)

    p_ret, p_rw, u, bnd = _inproj_fwd(x, small["norm_g"], w_in_t)
    y_ret, ret, s_in_all = _ret_fwd(p_ret, cos, sin, tabs, small["ret_gn_g"], seg128)
    hm, v, g, rk = _prep_fwd(p_rw, bnd, *prep_w)
    o, states, sa = _wkv_fwd(hm, v, e_tab)
    y_rw = _post_fwd(o, rk, v, g, *post_w)
    loss, dh, dy, d_w_out, d_gf = _outproj(x, y_ret, y_rw, w_out_b, target, small["final_norm_g"])

    do, d_rk, dv2, dg, d_gn_g, d_gn_b, d_r_k = _post_bwd(o, rk, v, g, *post_w, dy)
    dv1, d_hm = _wkv_bwd(hm, v, do, states, sa, e_tab, r_tab)
    dp_rw, dprev, dfirst, d_mu, d_w0, d_a0, d_k_k, d_k_a, d_lora = _prep_bwd(
        p_rw, bnd, *prep_w, (d_hm, d_rk, dv1, dv2, dg))
    dp_ret, d_ret_gn = _ret_bwd(p_ret, cos, sin, tabs, small["ret_gn_g"], seg128, ret, s_in_all, dy)
    grad_x, d_norm_g, dp_t = _inproj_bwd_x(dp_ret, dp_rw, dprev, dfirst, w_in_t, x, small["norm_g"], dh)
    d_w_in = _inproj_bwd_w(dp_t, u)

    d_small = {"norm_g": d_norm_g, "ret_gn_g": d_ret_gn, "rwkv_mu": d_mu, "w0": d_w0, "a0": d_a0, "k_k": d_k_k,
               "k_a": d_k_a, "r_k": d_r_k, "rwkv_gn_g": d_gn_g, "rwkv_gn_b": d_gn_b, "final_norm_g": d_gf}
    return loss, grad_x, d_w_in, d_w_out, d_lora, d_small


ANY = pl.BlockSpec(memory_space=pl.ANY)
CHIP_FLIPS = ((0, 1), (1, 0), (1, 1))
N_FLIPS = len(CHIP_FLIPS)
LORA_SHARD = RW_WIDTH // N_CHIPS
HALF_IN = IN_SHARD // 2
HALF_OUT = OUT_SHARD // 2


def _position():
    return lax.axis_index("x"), lax.axis_index("y"), lax.axis_index("c")


def _flip(v, f):
    return 1 - v if f else v


def _finish(local, remote, landed):
    for cp in landed:
        cp.wait_recv()
    for cp in remote:
        cp.wait_send()
    for cp in local:
        cp.wait()


def _gather_chips(arrs):
    n = len(arrs)

    def body(*refs):
        ins, outs = refs[:n], refs[n:2 * n]
        send, recv, pass_send, pass_recv = refs[2 * n:]
        x, y, c = _position()
        s = 2 * x + y
        sibling = (x, y, 1 - c)

        def copy(src, dst, sems, k, to):
            return pltpu.make_async_remote_copy(src_ref=src, dst_ref=dst, send_sem=sems[0].at[k], recv_sem=sems[1].at[k],
                                                device_id=to, device_id_type=MESH)

        remote, landed, passed, passed_in = [], [], [], []
        for a in range(n):
            for j, (fx, fy) in enumerate(CHIP_FLIPS):
                px, py = _flip(x, fx), _flip(y, fy)
                ps = 2 * px + py
                k = a * N_FLIPS + j
                remote.append(copy(ins[a].at[c], outs[a].at[s, c], (send, recv), k, (px, py, c)))
                landed.append(copy(ins[a].at[c], outs[a].at[ps, c], (send, recv), k, (px, py, c)))
                passed.append(copy(outs[a].at[ps, c], outs[a].at[ps, c], (pass_send, pass_recv), k, sibling))
                passed_in.append(copy(outs[a].at[ps, 1 - c], outs[a].at[ps, 1 - c], (pass_send, pass_recv), k, sibling))
        for cp in remote:
            cp.start()
        for arrived, onward in zip(landed, passed):
            arrived.wait_recv()
            onward.start()
        _finish([], remote + passed, passed_in)

    sems = pltpu.SemaphoreType.DMA((n * N_FLIPS,))
    return pl.pallas_call(
        body, name="gather_weights",
        in_specs=[ANY] * n, out_specs=[ANY] * n,
        out_shape=[jax.ShapeDtypeStruct((N_CHIPS,) + a.shape, a.dtype) for a in arrs],
        scratch_shapes=[sems, sems, sems, sems],
    )(*arrs)


def _pair_exchange(g_in, g_out, g_small):
    def body(gi_ref, go_ref, gs_ref, li_ref, lo_ref, ls_ref, send, recv):
        x, y, c = _position()
        peer = (x, y, 1 - c)
        srcs = (gi_ref.at[:, pl.ds((1 - c) * HALF_IN, HALF_IN), :], go_ref.at[:, pl.ds((1 - c) * HALF_OUT, HALF_OUT), :],
                gs_ref)
        remote = [pltpu.make_async_remote_copy(src_ref=src, dst_ref=dst, send_sem=send.at[k], recv_sem=recv.at[k],
                                               device_id=peer, device_id_type=MESH)
                  for k, (src, dst) in enumerate(zip(srcs, (li_ref, lo_ref, ls_ref)))]
        for cp in remote:
            cp.start()
        _finish([], remote, remote)

    return pl.pallas_call(
        body, name="pair_exchange",
        in_specs=[ANY] * 3, out_specs=[ANY] * 3,
        out_shape=[jax.ShapeDtypeStruct((N_CHIPS, HALF_IN, D_MODEL), F32),
                   jax.ShapeDtypeStruct((N_CHIPS, HALF_OUT, D_MODEL), F32),
                   jax.ShapeDtypeStruct(g_small.shape, F32)],
        scratch_shapes=[pltpu.SemaphoreType.DMA((3,)), pltpu.SemaphoreType.DMA((3,))],
    )(g_in, g_out, g_small)


def _pair_sum(g_in, g_out, g_small, l_in, l_out, l_small, c_arr):
    def body(c_ref, gi_ref, go_ref, gs_ref, li_ref, lo_ref, ls_ref, ci_ref, co_ref, cs_ref):
        ci_ref[...] = (gi_ref[...] + li_ref[...]).astype(BF16)
        co_ref[...] = (go_ref[...] + lo_ref[...]).astype(BF16)

        @pl.when(pl.program_id(0) == 0)
        def _():
            cs_ref[...] = gs_ref[...] + ls_ref[...]

    nd = g_small.shape
    return pl.pallas_call(
        body, name="pair_sum",
        grid_spec=pltpu.PrefetchScalarGridSpec(
            num_scalar_prefetch=1, grid=(N_CHIPS,),
            in_specs=[pl.BlockSpec((1, HALF_IN, D_MODEL), lambda s, c: (s, c[0], 0)),
                      pl.BlockSpec((1, HALF_OUT, D_MODEL), lambda s, c: (s, c[0], 0)),
                      pl.BlockSpec(nd, lambda s, c: (0, 0)),
                      pl.BlockSpec((1, HALF_IN, D_MODEL), lambda s, c: (s, 0, 0)),
                      pl.BlockSpec((1, HALF_OUT, D_MODEL), lambda s, c: (s, 0, 0)),
                      pl.BlockSpec(nd, lambda s, c: (0, 0))],
            out_specs=[pl.BlockSpec((1, HALF_IN, D_MODEL), lambda s, c: (s, 0, 0)),
                       pl.BlockSpec((1, HALF_OUT, D_MODEL), lambda s, c: (s, 0, 0)),
                       pl.BlockSpec(nd, lambda s, c: (0, 0))]),
        out_shape=[jax.ShapeDtypeStruct((N_CHIPS, HALF_IN, D_MODEL), BF16),
                   jax.ShapeDtypeStruct((N_CHIPS, HALF_OUT, D_MODEL), BF16), jax.ShapeDtypeStruct(nd, F32)],
        compiler_params=_cparams(dimension_semantics=("arbitrary",)),
    )(c_arr, g_in, g_out, g_small, l_in, l_out, l_small)


def _chip_exchange(c_in, c_out, c_small):
    def body(ci_ref, co_ref, cs_ref, li_ref, lo_ref, ls_ref, send, recv):
        x, y, c = _position()
        s = 2 * x + y
        remote = []
        for j, (fx, fy) in enumerate(CHIP_FLIPS):
            px, py = _flip(x, fx), _flip(y, fy)
            ps = 2 * px + py
            for a, (src, dst) in enumerate(((ci_ref.at[ps], li_ref.at[j]), (co_ref.at[ps], lo_ref.at[j]),
                                            (cs_ref, ls_ref.at[j]))):
                k = 3 * j + a
                remote.append(pltpu.make_async_remote_copy(src_ref=src, dst_ref=dst, send_sem=send.at[k],
                                                           recv_sem=recv.at[k], device_id=(px, py, c),
                                                           device_id_type=MESH))
        for cp in remote:
            cp.start()
        _finish([], remote, remote)

    return pl.pallas_call(
        body, name="chip_exchange",
        in_specs=[ANY] * 3, out_specs=[ANY] * 3,
        out_shape=[jax.ShapeDtypeStruct((N_FLIPS, HALF_IN, D_MODEL), c_in.dtype),
                   jax.ShapeDtypeStruct((N_FLIPS, HALF_OUT, D_MODEL), c_out.dtype),
                   jax.ShapeDtypeStruct((N_FLIPS,) + c_small.shape, F32)],
        scratch_shapes=[pltpu.SemaphoreType.DMA((3 * N_FLIPS,)), pltpu.SemaphoreType.DMA((3 * N_FLIPS,))],
    )(c_in, c_out, c_small)


def _chip_sum(g_in, g_out, p_in, p_out, c_small, l_in, l_out, l_small, sc_arr):
    nd = c_small.shape

    def body(s_ref, gi_ref, go_ref, pi_ref, po_ref, cs_ref, li0, li1, li2, lo0, lo1, lo2, ls_ref,
             ri_ref, ro_ref, rs_ref):
        ri_ref[...] = (((gi_ref[0] + pi_ref[0]) + li0[0].astype(F32)) + li1[0].astype(F32)) + li2[0].astype(F32)
        ro_ref[...] = (((go_ref[0] + po_ref[0]) + lo0[0].astype(F32)) + lo1[0].astype(F32)) + lo2[0].astype(F32)
        me = s_ref[0]
        parts = (cs_ref[...], ls_ref[0], ls_ref[1], ls_ref[2])

        def of_chip(s):
            m = jnp.bitwise_xor(me, s)
            return jnp.where(m == 0, parts[0], jnp.where(m == 1, parts[1], jnp.where(m == 2, parts[2], parts[3])))

        rs_ref[...] = ((of_chip(0) + of_chip(1)) + of_chip(2)) + of_chip(3)

    def flip_in(j):
        return pl.BlockSpec((1, HALF_IN, D_MODEL), lambda i, s: (j, 0, 0))

    def flip_out(j):
        return pl.BlockSpec((1, HALF_OUT, D_MODEL), lambda i, s: (j, 0, 0))

    return pl.pallas_call(
        body, name="chip_sum",
        grid_spec=pltpu.PrefetchScalarGridSpec(
            num_scalar_prefetch=1, grid=(1,),
            in_specs=[pl.BlockSpec((1, HALF_IN, D_MODEL), lambda i, s: (s[0], s[1], 0)),
                      pl.BlockSpec((1, HALF_OUT, D_MODEL), lambda i, s: (s[0], s[1], 0)),
                      pl.BlockSpec((1, HALF_IN, D_MODEL), lambda i, s: (s[0], 0, 0)),
                      pl.BlockSpec((1, HALF_OUT, D_MODEL), lambda i, s: (s[0], 0, 0)),
                      pl.BlockSpec(nd, lambda i, s: (0, 0)),
                      flip_in(0), flip_in(1), flip_in(2), flip_out(0), flip_out(1), flip_out(2),
                      pl.BlockSpec((N_FLIPS,) + nd, lambda i, s: (0, 0, 0))],
            out_specs=[pl.BlockSpec((HALF_IN, D_MODEL), lambda i, s: (0, 0)),
                       pl.BlockSpec((HALF_OUT, D_MODEL), lambda i, s: (0, 0)),
                       pl.BlockSpec(nd, lambda i, s: (0, 0))]),
        out_shape=[jax.ShapeDtypeStruct((HALF_IN, D_MODEL), F32), jax.ShapeDtypeStruct((HALF_OUT, D_MODEL), F32),
                   jax.ShapeDtypeStruct(nd, F32)],
        compiler_params=_cparams(dimension_semantics=("arbitrary",)),
    )(sc_arr, g_in, g_out, p_in, p_out, c_small, l_in, l_in, l_in, l_out, l_out, l_out, l_small)


def _pair_share(r_in, r_out):
    def body(ri_ref, ro_ref, li_ref, lo_ref, send, recv):
        x, y, c = _position()
        remote = [pltpu.make_async_remote_copy(src_ref=src, dst_ref=dst, send_sem=send.at[k], recv_sem=recv.at[k],
                                               device_id=(x, y, 1 - c), device_id_type=MESH)
                  for k, (src, dst) in enumerate(((ri_ref, li_ref), (ro_ref, lo_ref)))]
        for cp in remote:
            cp.start()
        _finish([], remote, remote)

    return pl.pallas_call(
        body, name="pair_share",
        in_specs=[ANY] * 2, out_specs=[ANY] * 2,
        out_shape=[jax.ShapeDtypeStruct(r_in.shape, F32), jax.ShapeDtypeStruct(r_out.shape, F32)],
        scratch_shapes=[pltpu.SemaphoreType.DMA((2,)), pltpu.SemaphoreType.DMA((2,))],
    )(r_in, r_out)


def _adam_update(w, g, m, v):
    mn = ADAM_B1 * m + (1.0 - ADAM_B1) * g
    vn = ADAM_B2 * v + (1.0 - ADAM_B2) * jnp.square(g)
    m_hat = mn / (1.0 - ADAM_B1 ** ADAM_STEP)
    v_hat = vn / (1.0 - ADAM_B2 ** ADAM_STEP)
    return -ADAM_LR * (m_hat / (jnp.sqrt(v_hat) + ADAM_EPS) + ADAM_WD * w), mn, vn


def _adamw(name, w, g_mine, g_theirs, m, v, core_arr, tr):
    rows, cols = w.shape
    per_half = rows // 2 // tr

    def body(c_ref, w_ref, gm_ref, gt_ref, m_ref, v_ref, g_ref, d_ref, nm_ref, nv_ref):
        mine = (pl.program_id(0) // per_half) == c_ref[0]
        g = jnp.where(mine, gm_ref[...], gt_ref[...])
        d, mn, vn = _adam_update(w_ref[...], g, m_ref[...], v_ref[...])
        g_ref[...] = g
        d_ref[...] = d
        nm_ref[...] = mn
        nv_ref[...] = vn

    spec = pl.BlockSpec((tr, cols), lambda i, c: (i, 0))
    half = pl.BlockSpec((tr, cols), lambda i, c: (i % per_half, 0))
    return pl.pallas_call(
        body, name=name,
        grid_spec=pltpu.PrefetchScalarGridSpec(
            num_scalar_prefetch=1, grid=(rows // tr,),
            in_specs=[spec, half, half, spec, spec], out_specs=[spec] * 4),
        out_shape=[jax.ShapeDtypeStruct((rows, cols), F32)] * 4,
        compiler_params=_cparams(dimension_semantics=("arbitrary",)),
    )(core_arr, w, g_mine, g_theirs, m, v)


def _row_pieces(n):
    return [(k, k * PACK_W, min(PACK_W, n - k * PACK_W)) for k in range(-(-n // PACK_W))]


def _pack_small(d_small, loss, d_lora):
    ns = len(SMALL_NAMES)

    def body(*refs):
        small_refs, (loss_ref, lora_ref, out_ref) = refs[:ns], refs[ns:]
        out_ref[...] = jnp.zeros_like(out_ref)
        out_ref[PACK_LORA_W:PACK_LORA_W + LORA, :] = lora_ref[:LORA, :RW_WIDTH]
        out_ref[PACK_LORA_A:PACK_LORA_A + LORA, :] = lora_ref[LORA:, RW_WIDTH:]
        for name, n, ref in zip(SMALL_NAMES, SMALL_SIZES, small_refs):
            for k, at, w in _row_pieces(n):
                out_ref[PACK_AT[name] + k:PACK_AT[name] + k + 1, 0:w] = ref[:, at:at + w]
        out_ref[PACK_LOSS:PACK_LOSS + 1, :] = loss_ref[...]

    return pl.pallas_call(body, name="pack_small", out_shape=jax.ShapeDtypeStruct((PACK_ROWS, PACK_W), F32),
                          compiler_params=_cparams())(*d_small, loss, d_lora)


def _adamw_small(tot, chip_arr, ws, ms, vs):
    ns = len(SMALL_NAMES)
    n_par = ns + 2

    def body(s_ref, tot_ref, glw_ref, gla_ref, *refs):
        w_refs, m_refs, v_refs = refs[:n_par], refs[n_par:2 * n_par], refs[2 * n_par:3 * n_par]
        outs = refs[3 * n_par:]
        g_refs, d_refs, nm_refs, nv_refs = (outs[i * n_par:(i + 1) * n_par] for i in range(4))
        grads = [jnp.concatenate([tot_ref[PACK_AT[name] + k:PACK_AT[name] + k + 1, 0:w] for k, _, w in _row_pieces(n)],
                                 axis=1) for name, n in zip(SMALL_NAMES, SMALL_SIZES)]
        grads += [glw_ref[...], gla_ref[...]]
        for i, g in enumerate(grads):
            d, mn, vn = _adam_update(w_refs[i][...], g, m_refs[i][...], v_refs[i][...])
            g_refs[i][...] = g
            d_refs[i][...] = d
            nm_refs[i][...] = mn
            nv_refs[i][...] = vn

    def whole(a):
        nd = a.ndim
        return pl.BlockSpec(a.shape, lambda i, s: (0,) * nd)

    shard = (LORA, LORA_SHARD)
    par_specs = [whole(a) for a in ws]
    res = pl.pallas_call(
        body, name="adamw_small",
        grid_spec=pltpu.PrefetchScalarGridSpec(
            num_scalar_prefetch=1, grid=(1,),
            in_specs=[whole(tot), pl.BlockSpec(shard, lambda i, s: (PACK_LORA_W // LORA, s[0])),
                      pl.BlockSpec(shard, lambda i, s: (PACK_LORA_A // LORA, s[0]))] + par_specs * 3,
            out_specs=par_specs * 4),
        out_shape=[jax.ShapeDtypeStruct(a.shape, F32) for a in ws] * 4,
        compiler_params=_cparams(dimension_semantics=("arbitrary",)),
    )(chip_arr, tot, tot, tot, *ws, *ms, *vs)
    return [res[i * n_par:(i + 1) * n_par] for i in range(4)]


def kernel(x, norm_g, w_in, ret_gn_g, rwkv_mu, w_lora_up, w0, a_lora_up, a0, k_k, k_a, r_k, rwkv_gn_g, rwkv_gn_b, w_out, final_norm_g, loss_target, m_norm_g, m_w_in, m_ret_gn_g, m_rwkv_mu, m_w_lora_up, m_w0, m_a_lora_up, m_a0, m_k_k, m_k_a, m_r_k, m_rwkv_gn_g, m_rwkv_gn_b, m_w_out, m_final_norm_g, v_norm_g, v_w_in, v_ret_gn_g, v_rwkv_mu, v_w_lora_up, v_w0, v_a_lora_up, v_a0, v_k_k, v_k_a, v_r_k, v_rwkv_gn_g, v_rwkv_gn_b, v_w_out, v_final_norm_g):
    W = RW_WIDTH
    params = dict(norm_g=norm_g, ret_gn_g=ret_gn_g, rwkv_mu=rwkv_mu, w0=w0, a0=a0, k_k=k_k, k_a=k_a, r_k=r_k,
                  rwkv_gn_g=rwkv_gn_g, rwkv_gn_b=rwkv_gn_b, final_norm_g=final_norm_g)
    moments_m = dict(norm_g=m_norm_g, ret_gn_g=m_ret_gn_g, rwkv_mu=m_rwkv_mu, w0=m_w0, a0=m_a0, k_k=m_k_k, k_a=m_k_a,
                     r_k=m_r_k, rwkv_gn_g=m_rwkv_gn_g, rwkv_gn_b=m_rwkv_gn_b, final_norm_g=m_final_norm_g)
    moments_v = dict(norm_g=v_norm_g, ret_gn_g=v_ret_gn_g, rwkv_mu=v_rwkv_mu, w0=v_w0, a0=v_a0, k_k=v_k_k, k_a=v_k_a,
                     r_k=v_r_k, rwkv_gn_g=v_rwkv_gn_g, rwkv_gn_b=v_rwkv_gn_b, final_norm_g=v_final_norm_g)
    xi, yi, ci = _position()
    chip = (2 * xi + yi).astype(jnp.int32)

    def halves(a):
        return a.reshape(2, a.shape[0] // 2, a.shape[1])

    w_t, m_t, v_t = (jnp.transpose(a[0]) for a in (w_in, m_w_in, v_w_in))
    mine = [halves(w_t.astype(BF16)), halves(w_out[0].astype(BF16)), halves(w_lora_up[0]), halves(a_lora_up[0])]
    g_in, g_out, g_lw, g_la = [lax.dynamic_update_slice(g, own[None], (chip, 0, 0, 0))
                               for g, own in zip(_gather_chips(mine), mine)]
    w_in_t = g_in.reshape(IN_COLS, D_MODEL)
    w_out_b = g_out.reshape(D_MODEL, D_MODEL)
    lw = jnp.transpose(g_lw.reshape(N_CHIPS, LORA, LORA_SHARD), (1, 0, 2)).reshape(LORA, W)
    la = jnp.transpose(g_la.reshape(N_CHIPS, LORA, LORA_SHARD), (1, 0, 2)).reshape(LORA, W)
    zero = jnp.zeros((LORA, W), F32)
    lora = jnp.concatenate([jnp.concatenate([lw, zero], axis=1), jnp.concatenate([zero, la], axis=1)], axis=0)
    small = {n: params[n].reshape(1, -1) for n in SMALL_NAMES}

    loss, grad_x, d_w_in, d_w_out, d_lora, d_small = _local_step(x[0], loss_target[0], w_in_t, w_out_b, lora, small)

    core = ci.astype(jnp.int32)
    gi = d_w_in.reshape(N_CHIPS, IN_SHARD, D_MODEL)
    go = d_w_out.reshape(N_CHIPS, OUT_SHARD, D_MODEL)
    gs = _pack_small([d_small[n] for n in SMALL_NAMES], loss, d_lora)
    p_in, p_out, p_small = _pair_exchange(gi, go, gs)
    c_in, c_out, c_small = _pair_sum(gi, go, gs, p_in, p_out, p_small, core.reshape(1))
    l_in, l_out, l_small = _chip_exchange(c_in, c_out, c_small)
    r_in, r_out, tot = _chip_sum(gi, go, p_in, p_out, c_small, l_in, l_out, l_small, jnp.stack([chip, core]))
    t_in, t_out = _pair_share(r_in, r_out)

    grad_w_in, d_in, nm_in, nv_in = (jnp.transpose(a) for a in _adamw(
        "adamw_w_in", w_t, r_in, t_in, m_t, v_t, core.reshape(1), HALF_IN // 2))
    grad_w_out, d_out, nm_out, nv_out = _adamw("adamw_w_out", w_out[0], r_out, t_out, m_w_out[0], v_w_out[0],
                                               core.reshape(1), HALF_OUT)
    par_names = SMALL_NAMES + ("w_lora_up", "a_lora_up")

    def operands(tree, lw_, la_):
        return [tree[n].reshape(1, -1) for n in SMALL_NAMES] + [lw_[0], la_[0]]

    res = _adamw_small(tot, chip.reshape(1), operands(params, w_lora_up, a_lora_up),
                       operands(moments_m, m_w_lora_up, m_a_lora_up), operands(moments_v, v_w_lora_up, v_a_lora_up))

    names = ("norm_g", "w_in", "ret_gn_g", "rwkv_mu", "w_lora_up", "w0", "a_lora_up", "a0", "k_k", "k_a", "r_k",
             "rwkv_gn_g", "rwkv_gn_b", "w_out", "final_norm_g")
    shapes = dict(w_in=w_in.shape, w_out=w_out.shape, w_lora_up=w_lora_up.shape, a_lora_up=a_lora_up.shape,
                  **{n: params[n].shape for n in SMALL_NAMES})

    def leaves(pars, big_in, big_out):
        tree = dict(zip(par_names, pars), w_in=big_in, w_out=big_out)
        return [tree[n].reshape(shapes[n]) for n in names]

    grads = leaves(res[0], grad_w_in, grad_w_out)
    deltas = leaves(res[1], d_in, d_out)
    new_m = leaves(res[2], nm_in, nm_out)
    new_v = leaves(res[3], nv_in, nv_out)
    return (tot[PACK_LOSS, 0], grad_x.reshape(x.shape), *grads, *deltas, *new_m, *new_v)
```

```python
import functools

import numpy as np
import jax
import jax.numpy as jnp
from jax import lax
from jax.experimental import pallas as pl
from jax.experimental.pallas import tpu as pltpu

F32 = jnp.float32
BF16 = jnp.bfloat16
X3 = "bf16x3"
B1 = "bf16"
MESH = pl.DeviceIdType.MESH

D_MODEL = 1024
N_CHIPS = 4
RET_HEADS = 4
RET_DK = 64
RET_DV = 128
RET_QK = RET_HEADS * RET_DK
RET_WIDTH = RET_HEADS * RET_DV
RET_COLS = 2 * RET_QK + 2 * RET_WIDTH
RET_CHUNK = 64
RET_GROUP = 8
RW_WIDTH = 512
RW_HEAD = 64
RW_HEADS = 8
LORA = 64
RW_COLS = 4 * RW_WIDTH + 2 * LORA
IN_COLS = RET_COLS + RW_COLS
IN_SHARD = IN_COLS // N_CHIPS
OUT_SHARD = D_MODEL // N_CHIPS
ROPE_BASE = 10000.0
RMS_EPS = 1e-6
RET_GN_EPS = 1e-5
RW_GN_EPS = 64e-5
WKV_CHUNK = 16
WKV_GROUP = 8
N_VEC = 5

ADAM_LR = 0.001
ADAM_B1 = 0.9
ADAM_B2 = 0.999
ADAM_EPS = 1e-08
ADAM_WD = 0.01
ADAM_STEP = 10

VMEM_LIMIT = 56 * 1024 * 1024

PACK_W = 512
SMALL_NAMES = ("norm_g", "ret_gn_g", "rwkv_mu", "w0", "a0", "k_k", "k_a", "r_k", "rwkv_gn_g", "rwkv_gn_b",
               "final_norm_g")
SMALL_SIZES = (1024, 512, 2176, 512, 512, 512, 512, 512, 512, 512, 1024)
PACK_LORA_W = 0
PACK_LORA_A = LORA
PACK_SMALL = 2 * LORA


def _pack_layout():
    rows, at = {}, PACK_SMALL
    for name, n in zip(SMALL_NAMES, SMALL_SIZES):
        rows[name] = at
        at += -(-n // PACK_W)
    return rows, at


PACK_AT, PACK_LOSS = _pack_layout()
PACK_ROWS = -(-(PACK_LOSS + 1) // 8) * 8


def _cparams(**kw):
    return pltpu.CompilerParams(vmem_limit_bytes=VMEM_LIMIT, **kw)


def _split(x):
    hi = x.astype(BF16)
    lo = (x - hi.astype(F32)).astype(BF16)
    return hi, lo


def _dot_dims(a, b, dims, precision):
    if precision == B1:
        a, b = a.astype(BF16), b.astype(BF16)
    if precision != X3:
        return lax.dot_general(a, b, dims, preferred_element_type=F32)
    (ah, al), (bh, bl) = _split(a), _split(b)
    dot = lambda u, w: lax.dot_general(u, w, dims, preferred_element_type=F32)
    return dot(ah, bh) + dot(ah, bl) + dot(al, bh)


def _dot(a, b, precision=None):
    return _dot_dims(a, b, (((1,), (0,)), ((), ())), precision)


def _dot_nt(a, b, precision=None):
    return _dot_dims(a, b, (((1,), (1,)), ((), ())), precision)


def _dot_tn(a, b, precision=None):
    return _dot_dims(a, b, (((0,), (0,)), ((), ())), precision)


@jax.custom_vjp
def _segsum(x, seg):
    hi, lo = _split(x)
    return _dot(hi, seg) + _dot(lo, seg)


def _segsum_fwd(x, seg):
    return _segsum(x, seg), seg


def _segsum_bwd(seg, ct):
    return _segsum(ct, seg), jnp.zeros_like(seg)


_segsum.defvjp(_segsum_fwd, _segsum_bwd)


def _softplus(z):
    return jnp.maximum(z, 0.0) + jnp.log(1.0 + jnp.exp(-jnp.abs(z)))


def _full(shape):
    nd = len(shape)
    return pl.BlockSpec(shape, lambda *_: (0,) * nd)


def _rope_tables(T):
    half = RET_DK // 2
    expo = -jnp.arange(half, dtype=F32) / jnp.float32(half)
    freqs = jnp.exp(expo * jnp.float32(np.log(ROPE_BASE)))
    ang = jnp.arange(T, dtype=jnp.int32).astype(F32)[:, None] * freqs[None, :]
    cos = jnp.tile(jnp.cos(ang), (1, 2 * RET_HEADS))
    sin = jnp.tile(jnp.sin(ang), (1, 2 * RET_HEADS))
    return cos, sin


def _ret_tables():
    H, C = RET_HEADS, RET_CHUNK
    hidx = jnp.arange(H, dtype=F32)
    lg = jnp.log(1.0 - jnp.exp2(-5.0 - hidx))
    idx = jnp.arange(C, dtype=F32)
    intra = jnp.exp(lg[:, None, None] * jnp.abs(idx[:, None] - idx[None, :]))
    q_dec = jnp.transpose(jnp.exp(lg[:, None] * (idx[None, :] + 1.0)))
    k_dec = jnp.transpose(jnp.exp(lg[:, None] * (C - 1.0 - idx[None, :])))
    chunk_dec = jnp.exp(lg * C)
    qd = jnp.repeat(q_dec, RET_DK, axis=1)
    kd = jnp.repeat(k_dec, RET_DK, axis=1)
    row_h = np.arange(RET_QK) // RET_DK
    col_h = np.arange(RET_WIDTH) // RET_DV
    bm = jnp.asarray((row_h[:, None] == col_h[None, :]).astype(np.float32))
    cd = bm * jnp.repeat(chunk_dec, RET_DK)[:, None]
    return intra, qd, kd, cd, bm


def _seg_matrix(width, head):
    h = np.arange(width) // head
    return jnp.asarray((h[:, None] == h[None, :]).astype(np.float32), dtype=BF16)


def _wkv_expand_table():
    Tc = WKV_CHUNK
    k = np.arange(2 * RW_HEADS * Tc)
    kh, kt = (k % (RW_HEADS * Tc)) // Tc, k % Tc
    nh = np.arange(RW_WIDTH) // RW_HEAD
    e = (kh[None, :, None] == nh[None, None, :]) & (kt[None, :, None] == np.arange(Tc)[:, None, None])
    return jnp.asarray(e.astype(np.float32), dtype=BF16)


def _wkv_reduce_table():
    Tc = WKV_CHUNK
    kh = np.arange(RW_WIDTH) // RW_HEAD
    n = np.arange(RW_HEADS * Tc)
    nh, nt = n // Tc, n % Tc
    r = (kh[None, :, None] == nh[None, None, :]) & (nt[None, None, :] == np.arange(Tc)[:, None, None])
    return jnp.asarray(r.astype(np.float32), dtype=BF16)


def _inproj_fwd(x, norm_g, w_t):
    T = x.shape[0]
    tm = _row_tile(T)

    def body(x_ref, g_ref, w_ref, pret_ref, prw_ref, u_ref, last_ref):
        xf = x_ref[...]
        rstd = lax.rsqrt(jnp.mean(xf * xf, axis=-1, keepdims=True) + RMS_EPS)
        ub = ((xf * rstd) * g_ref[...]).astype(BF16)
        u_ref[...] = ub
        pret_ref[...] = _dot_nt(ub, w_ref[:RET_COLS, :])
        p_rw = _dot_nt(ub, w_ref[RET_COLS:, :])
        prw_ref[...] = p_rw
        last_ref[0] = p_rw[tm - 1:tm, :]

    return pl.pallas_call(
        body, name="inproj_fwd", grid=(T // tm,),
        in_specs=[pl.BlockSpec((tm, D_MODEL), lambda i: (i, 0)), _full((1, D_MODEL)), _full((IN_COLS, D_MODEL))],
        out_specs=[pl.BlockSpec((tm, RET_COLS), lambda i: (i, 0)), pl.BlockSpec((tm, RW_COLS), lambda i: (i, 0)),
                   pl.BlockSpec((tm, D_MODEL), lambda i: (i, 0)), pl.BlockSpec((1, 1, RW_COLS), lambda i: (i, 0, 0))],
        out_shape=[jax.ShapeDtypeStruct((T, RET_COLS), F32), jax.ShapeDtypeStruct((T, RW_COLS), F32),
                   jax.ShapeDtypeStruct((T, D_MODEL), BF16), jax.ShapeDtypeStruct((T // tm, 1, RW_COLS), F32)],
        compiler_params=_cparams(dimension_semantics=("arbitrary",)),
    )(x, norm_g, w_t)


def _rot_half(x):
    n = x.shape[1]
    lane = lax.broadcasted_iota(jnp.int32, x.shape, 1)
    first = (lane % RET_DK) < (RET_DK // 2)
    return jnp.where(first, -pltpu.roll(x, n - RET_DK // 2, 1), pltpu.roll(x, RET_DK // 2, 1))


def _rope(x, cos, sin):
    return x * cos + _rot_half(x) * sin


def _rope_bwd(d, cos, sin):
    return d * cos - _rot_half(d * sin)


def _ret_post(ret, g, gn_g):
    heads = []
    for h in range(RET_HEADS):
        xh = ret[:, h * RET_DV:(h + 1) * RET_DV]
        xc = xh - jnp.mean(xh, axis=-1, keepdims=True)
        heads.append(xc * lax.rsqrt(jnp.mean(xc * xc, axis=-1, keepdims=True) + RET_GN_EPS))
    return (g * jax.nn.sigmoid(g)) * (jnp.concatenate(heads, axis=1) * gn_g)


def _ret_scores(qt, kt, d_ref, h):
    lane = lax.broadcasted_iota(jnp.int32, qt.shape, 1)
    qh = jnp.where(lane // RET_DK == h, qt, 0.0)
    return qh, _dot_nt(qh, kt, B1) * d_ref[h]


def _ret_group(nch):
    return min(RET_GROUP, nch)


def _ret_fwd(p_ret, cos, sin, tabs, gn_g):
    T = p_ret.shape[0]
    C = RET_CHUNK
    nch = T // C
    G = _ret_group(nch)
    intra_d, qd, kd, cd, bm = tabs

    def body(q_ref, k_ref, v_ref, g_ref, cos_ref, sin_ref, qd_ref, kd_ref, d_ref, cd_ref, bm_ref, gn_ref,
             y_ref, ret_ref, sin_out_ref, s_ref, qt_buf, kv_buf):
        @pl.when(pl.program_id(0) == 0)
        def _():
            s_ref[...] = jnp.zeros_like(s_ref)

        cosv, sinv = cos_ref[...], sin_ref[...]
        qt_all = _rope(q_ref[...], cosv, sinv)
        kt_all = _rope(k_ref[...], cosv, sinv) * (RET_DK ** -0.5)
        for i in range(G):
            rows = slice(i * C, (i + 1) * C)
            qt, kt, v = qt_all[rows], kt_all[rows], v_ref[rows, :]
            intra = []
            for h in range(RET_HEADS):
                _, a = _ret_scores(qt, kt, d_ref, h)
                intra.append(_dot(a, v[:, h * RET_DV:(h + 1) * RET_DV], B1))
            ret_ref[rows, :] = jnp.concatenate(intra, axis=1)
            qt_buf[rows, :] = qt * qd_ref[...]
            kv_buf[i] = _dot_tn(kt * kd_ref[...], v, B1) * bm_ref[...]
        s_in = s_ref[...]
        for i in range(G):
            rows = slice(i * C, (i + 1) * C)
            sin_out_ref[i] = s_in
            ret_ref[rows, :] += _dot(qt_buf[rows, :], s_in, B1)
            s_in = s_in * cd_ref[...] + kv_buf[i]
        s_ref[...] = s_in
        y_ref[...] = _ret_post(ret_ref[...], g_ref[...], gn_ref[...]).astype(BF16)

    GC = G * C
    return pl.pallas_call(
        body, name="ret_fwd", grid=(nch // G,),
        in_specs=[pl.BlockSpec((GC, RET_QK), lambda c: (c, 0)), pl.BlockSpec((GC, RET_QK), lambda c: (c, 1)),
                  pl.BlockSpec((GC, RET_WIDTH), lambda c: (c, 1)), pl.BlockSpec((GC, RET_WIDTH), lambda c: (c, 2)),
                  pl.BlockSpec((GC, RET_QK), lambda c: (c, 0)), pl.BlockSpec((GC, RET_QK), lambda c: (c, 0)),
                  _full((C, RET_QK)), _full((C, RET_QK)), _full((RET_HEADS, C, C)),
                  _full((RET_QK, RET_WIDTH)), _full((RET_QK, RET_WIDTH)), _full((1, RET_WIDTH))],
        out_specs=[pl.BlockSpec((GC, RET_WIDTH), lambda c: (c, 0)), pl.BlockSpec((GC, RET_WIDTH), lambda c: (c, 0)),
                   pl.BlockSpec((G, RET_QK, RET_WIDTH), lambda c: (c, 0, 0))],
        out_shape=[jax.ShapeDtypeStruct((T, RET_WIDTH), BF16), jax.ShapeDtypeStruct((T, RET_WIDTH), F32),
                   jax.ShapeDtypeStruct((nch, RET_QK, RET_WIDTH), F32)],
        scratch_shapes=[pltpu.VMEM((RET_QK, RET_WIDTH), F32), pltpu.VMEM((GC, RET_QK), F32),
                        pltpu.VMEM((G, RET_QK, RET_WIDTH), F32)],
        compiler_params=_cparams(dimension_semantics=("arbitrary",)),
    )(p_ret, p_ret, p_ret, p_ret, cos, sin, qd, kd, intra_d, cd, bm, gn_g)


def _ret_bwd(p_ret, cos, sin, tabs, gn_g, ret, s_in_all, dy):
    T = p_ret.shape[0]
    C = RET_CHUNK
    nch = T // C
    G = _ret_group(nch)
    ngr = nch // G
    intra_d, qd, kd, cd, bm = tabs

    def rev(j):
        return lambda c: (ngr - 1 - c, j)

    def body(q_ref, k_ref, v_ref, g_ref, cos_ref, sin_ref, qd_ref, kd_ref, d_ref, cd_ref, bm_ref, gn_ref,
             ret_ref, sin_ref_, dy_ref, dp_ref, dgn_ref, ds_ref, dkt_buf, ktk_buf, g_buf):
        @pl.when(pl.program_id(0) == 0)
        def _():
            ds_ref[...] = jnp.zeros_like(ds_ref)
            dgn_ref[...] = jnp.zeros_like(dgn_ref)

        _, post_vjp = jax.vjp(_ret_post, ret_ref[...], g_ref[...], gn_ref[...])
        dret_all, dg_all, dgn = post_vjp(dy_ref[...])
        dgn_ref[...] += dgn
        v_cols = slice(2 * RET_QK, 2 * RET_QK + RET_WIDTH)
        dp_ref[:, 2 * RET_QK + RET_WIDTH:] = dg_all

        qdv, kdv = qd_ref[...], kd_ref[...]
        cosv, sinv = cos_ref[...], sin_ref[...]
        qt_all = _rope(q_ref[...], cosv, sinv)
        kt_all = _rope(k_ref[...], cosv, sinv) * (RET_DK ** -0.5)
        for i in range(G):
            rows = slice(i * C, (i + 1) * C)
            qt, kt, v, dret = qt_all[rows], kt_all[rows], v_ref[rows, :], dret_all[rows, :]
            dqt = qdv * _dot_nt(dret, sin_ref_[i], B1)
            dkt = jnp.zeros_like(kt)
            dvs = []
            for h in range(RET_HEADS):
                sl = slice(h * RET_DV, (h + 1) * RET_DV)
                qh, a = _ret_scores(qt, kt, d_ref, h)
                lane = lax.broadcasted_iota(jnp.int32, kt.shape, 1)
                kh = jnp.where(lane // RET_DK == h, kt, 0.0)
                da = _dot_nt(dret[:, sl], v[:, sl], B1) * d_ref[h]
                dvs.append(_dot_tn(a, dret[:, sl], B1))
                dqt = dqt + _dot(da, kh, B1)
                dkt = dkt + _dot_tn(da, qh, B1)
            dp_ref[rows, :RET_QK] = _rope_bwd(dqt, cosv[rows], sinv[rows])
            dp_ref[rows, v_cols] = jnp.concatenate(dvs, axis=1)
            dkt_buf[rows, :] = dkt
            ktk_buf[rows, :] = kt * kdv
            g_buf[i] = _dot_tn(qt * qdv, dret, B1) * bm_ref[...]
        ds_out = ds_ref[...]
        for i in reversed(range(G)):
            rows = slice(i * C, (i + 1) * C)
            dkt = dkt_buf[rows, :] + kdv * _dot_nt(v_ref[rows, :], ds_out, B1)
            dp_ref[rows, RET_QK:2 * RET_QK] = _rope_bwd(dkt * (RET_DK ** -0.5), cosv[rows], sinv[rows])
            dp_ref[rows, v_cols] += _dot(ktk_buf[rows, :], ds_out, B1)
            ds_out = ds_out * cd_ref[...] + g_buf[i]
        ds_ref[...] = ds_out

    GC = G * C
    return pl.pallas_call(
        body, name="ret_bwd", grid=(ngr,),
        in_specs=[pl.BlockSpec((GC, RET_QK), rev(0)), pl.BlockSpec((GC, RET_QK), rev(1)),
                  pl.BlockSpec((GC, RET_WIDTH), rev(1)), pl.BlockSpec((GC, RET_WIDTH), rev(2)),
                  pl.BlockSpec((GC, RET_QK), rev(0)), pl.BlockSpec((GC, RET_QK), rev(0)),
                  _full((C, RET_QK)), _full((C, RET_QK)), _full((RET_HEADS, C, C)),
                  _full((RET_QK, RET_WIDTH)), _full((RET_QK, RET_WIDTH)), _full((1, RET_WIDTH)),
                  pl.BlockSpec((GC, RET_WIDTH), rev(0)),
                  pl.BlockSpec((G, RET_QK, RET_WIDTH), lambda c: (ngr - 1 - c, 0, 0)),
                  pl.BlockSpec((GC, RET_WIDTH), rev(0))],
        out_specs=[pl.BlockSpec((GC, RET_COLS), rev(0)), _full((1, RET_WIDTH))],
        out_shape=[jax.ShapeDtypeStruct((T, RET_COLS), F32), jax.ShapeDtypeStruct((1, RET_WIDTH), F32)],
        scratch_shapes=[pltpu.VMEM((RET_QK, RET_WIDTH), F32), pltpu.VMEM((GC, RET_QK), F32),
                        pltpu.VMEM((GC, RET_QK), F32), pltpu.VMEM((G, RET_QK, RET_WIDTH), F32)],
        compiler_params=_cparams(dimension_semantics=("arbitrary",)),
    )(p_ret, p_ret, p_ret, p_ret, cos, sin, qd, kd, intra_d, cd, bm, gn_g, ret, s_in_all, dy)


@jax.custom_vjp
def _chunk_sums(x, tri):
    hi, lo = _split(x)
    return _dot(tri, hi) + _dot(tri, lo)


def _chunk_sums_fwd(x, tri):
    return _chunk_sums(x, tri), tri


def _chunk_sums_bwd(tri, ct):
    hi, lo = _split(ct)
    return _dot_tn(tri, hi) + _dot_tn(tri, lo), jnp.zeros_like(tri)


_chunk_sums.defvjp(_chunk_sums_fwd, _chunk_sums_bwd)


@jax.custom_vjp
def _lora_dot(z, lora):
    return _dot(z, lora, X3)


def _lora_dot_fwd(z, lora):
    return _lora_dot(z, lora), (z, lora)


def _lora_dot_bwd(res, ct):
    z, lora = res
    return _dot_nt(ct, lora, X3), _dot_tn(z, ct, X3)


_lora_dot.defvjp(_lora_dot_fwd, _lora_dot_bwd)


def _chunk_tables(tm):
    t = np.arange(tm)
    same = (t[:, None] // WKV_CHUNK) == (t[None, :] // WKV_CHUNK)
    return jnp.asarray(np.stack([same & (t[None, :] <= t[:, None]), same]).astype(np.float32), dtype=BF16)


def _prep_fn(p, prev, mu, w0, a0, k_k, k_a, lora, seg, tri):
    W = RW_WIDTH
    ps = p + mu * (prev - p)
    r, kr, vr, g = ps[:, 0:W], ps[:, W:2 * W], ps[:, 2 * W:3 * W], ps[:, 3 * W:4 * W]
    z = ps[:, 4 * W:]
    lane = lax.broadcasted_iota(jnp.int32, z.shape, 1)
    z = jnp.where(lane < LORA, jnp.tanh(z), z)
    lo = _lora_dot(z, lora)
    w_log = -_softplus(-(w0 + lo[:, :W])) - 0.5
    log_decay = -jnp.exp(w_log)
    cum = _chunk_sums(log_decay, tri[0])
    total = _chunk_sums(log_decay, tri[1])
    a = jax.nn.sigmoid(a0 + lo[:, W:])
    kk = kr * k_k
    kk = kk / jnp.maximum(jnp.sqrt(_segsum(kk * kk, seg)), 1e-12)
    k = kr * (1.0 + (a - 1.0) * k_a)
    grow = jnp.exp(-cum)
    return kk * jnp.exp(cum - log_decay), (kk * a) * grow, k * grow, r * jnp.exp(cum), jnp.exp(total), vr, g, r * k


def _post_fn(o, rk, v, g, gn_g, gn_b, r_k, seg):
    mu = _segsum(o, seg) * (1.0 / RW_HEAD)
    oc = o - mu
    var = _segsum(oc * oc, seg) * (1.0 / RW_HEAD)
    on = oc * lax.rsqrt(var + RW_GN_EPS) * gn_g + gn_b
    bonus = _segsum(rk * r_k, seg) * v
    return (g * jax.nn.sigmoid(g)) * (on + bonus)


N_PAIR = (N_VEC + 1) // 2
HALF_LANES = 64


def _swap_halves(x):
    return pltpu.roll(x, HALF_LANES, 1)


def _pack_heads(vecs):
    tm = vecs[0].shape[0]
    low = lax.broadcasted_iota(jnp.int32, (tm, 128), 1) < HALF_LANES
    out = []
    for p in range(N_PAIR):
        a = vecs[2 * p]
        b = vecs[2 * p + 1] if 2 * p + 1 < len(vecs) else None
        heads = []
        for m in range(RW_WIDTH // 128):
            am = a[:, m * 128:(m + 1) * 128]
            bm = jnp.zeros_like(am) if b is None else b[:, m * 128:(m + 1) * 128]
            heads.append(jnp.where(low, am, _swap_halves(bm)))
            heads.append(jnp.where(low, _swap_halves(am), bm))
        out.append(heads)
    return out


def _unpack_heads(hm_ref):
    tm = hm_ref.shape[2]
    low = lax.broadcasted_iota(jnp.int32, (tm, 128), 1) < HALF_LANES
    vecs = []
    for p in range(N_PAIR):
        a, b = [], []
        for m in range(RW_WIDTH // 128):
            even, odd = hm_ref[p, 2 * m], hm_ref[p, 2 * m + 1]
            a.append(jnp.where(low, even, _swap_halves(odd)))
            b.append(jnp.where(low, _swap_halves(even), odd))
        vecs += [jnp.concatenate(a, axis=1), jnp.concatenate(b, axis=1)]
    return vecs[:N_VEC]


def _shift_down(p, first_row):
    row = lax.broadcasted_iota(jnp.int32, p.shape, 0)
    return jnp.where(row == 0, first_row, pltpu.roll(p, 1, 0))


def _shift_up(p, last_row):
    n = p.shape[0]
    row = lax.broadcasted_iota(jnp.int32, p.shape, 0)
    return jnp.where(row == n - 1, last_row, pltpu.roll(p, n - 1, 0))


def _row_tile(T):
    return min(T, 256)


def _prep_fwd(p_rw, bnd, mu, w0, a0, k_k, k_a, lora, seg64, tri):
    T = p_rw.shape[0]
    tm = _row_tile(T)
    W = RW_WIDTH

    def body(p_ref, bnd_ref, mu_ref, w0_ref, a0_ref, kk_ref, ka_ref, lora_ref, seg_ref, tri_ref,
             hm_ref, v_ref, g_ref, rk_ref):
        p = p_ref[...]
        prev = _shift_down(p, jnp.where(pl.program_id(0) == 0, 0.0, bnd_ref[0]))
        res = _prep_fn(p, prev, mu_ref[...], w0_ref[...], a0_ref[...], kk_ref[...], ka_ref[...], lora_ref[...],
                       seg_ref[...], (tri_ref[0], tri_ref[1]))
        for pair, heads in enumerate(_pack_heads(res[:N_VEC])):
            for h, val in enumerate(heads):
                hm_ref[pair, h] = val
        v_ref[...] = res[N_VEC]
        g_ref[...] = res[N_VEC + 1]
        rk_ref[...] = res[N_VEC + 2]

    small = _full((1, W))
    row = pl.BlockSpec((tm, W), lambda i: (i, 0))
    return pl.pallas_call(
        body, name="rwkv_prep_fwd", grid=(T // tm,),
        in_specs=[pl.BlockSpec((tm, RW_COLS), lambda i: (i, 0)),
                  pl.BlockSpec((1, 1, RW_COLS), lambda i: (jnp.maximum(i - 1, 0), 0, 0)),
                  _full((1, RW_COLS)), small, small, small, small, _full((2 * LORA, 2 * W)), _full((W, W)),
                  _full((2, tm, tm))],
        out_specs=[pl.BlockSpec((N_PAIR, RW_HEADS, tm, 128), lambda i: (0, 0, i, 0)), row, row, row],
        out_shape=[jax.ShapeDtypeStruct((N_PAIR, RW_HEADS, T, 128), F32)] + [jax.ShapeDtypeStruct((T, W), F32)] * 3,
        compiler_params=_cparams(dimension_semantics=("arbitrary",)),
    )(p_rw, bnd, mu, w0, a0, k_k, k_a, lora, seg64, tri)


def _prep_bwd(p_rw, bnd, mu, w0, a0, k_k, k_a, lora, seg64, tri, cts):
    T = p_rw.shape[0]
    tm = _row_tile(T)
    W = RW_WIDTH

    def body(p_ref, bnd_ref, mu_ref, w0_ref, a0_ref, kk_ref, ka_ref, lora_ref, seg_ref, tri_ref,
             dhm_ref, drk_ref, dv1_ref, dv2_ref, dg_ref,
             dp_ref, dprev_ref, dfirst_ref, dmu_ref, dw0_ref, da0_ref, dkk_p_ref, dka_ref, dlora_ref):
        accs = (dmu_ref, dw0_ref, da0_ref, dkk_p_ref, dka_ref, dlora_ref)

        @pl.when(pl.program_id(0) == 0)
        def _():
            for a_ref in accs:
                a_ref[...] = jnp.zeros_like(a_ref)

        p = p_ref[...]
        prev = _shift_down(p, jnp.where(pl.program_id(0) == 0, 0.0, bnd_ref[0]))
        seg, tri = seg_ref[...], (tri_ref[0], tri_ref[1])
        _, vjp = jax.vjp(lambda *a: _prep_fn(*a, seg, tri), p, prev, mu_ref[...], w0_ref[...], a0_ref[...],
                         kk_ref[...], ka_ref[...], lora_ref[...])
        ct = (*_unpack_heads(dhm_ref), dv1_ref[...] + dv2_ref[...], dg_ref[...], drk_ref[...])
        grads = vjp(ct)
        dp_ref[...] = grads[0]
        dprev_ref[...] = grads[1]
        dfirst_ref[0] = grads[1][0:1, :]
        for a_ref, gval in zip(accs, grads[2:]):
            a_ref[...] += gval

    small = _full((1, W))
    row = pl.BlockSpec((tm, W), lambda i: (i, 0))
    return pl.pallas_call(
        body, name="rwkv_prep_bwd", grid=(T // tm,),
        in_specs=[pl.BlockSpec((tm, RW_COLS), lambda i: (i, 0)),
                  pl.BlockSpec((1, 1, RW_COLS), lambda i: (jnp.maximum(i - 1, 0), 0, 0)),
                  _full((1, RW_COLS)), small, small, small, small, _full((2 * LORA, 2 * W)), _full((W, W)),
                  _full((2, tm, tm)), pl.BlockSpec((N_PAIR, RW_HEADS, tm, 128), lambda i: (0, 0, i, 0))] + [row] * 4,
        out_specs=[pl.BlockSpec((tm, RW_COLS), lambda i: (i, 0)), pl.BlockSpec((tm, RW_COLS), lambda i: (i, 0)),
                   pl.BlockSpec((1, 1, RW_COLS), lambda i: (i, 0, 0)),
                   _full((1, RW_COLS)), small, small, small, small, _full((2 * LORA, 2 * W))],
        out_shape=[jax.ShapeDtypeStruct((T, RW_COLS), F32), jax.ShapeDtypeStruct((T, RW_COLS), F32),
                   jax.ShapeDtypeStruct((T // tm, 1, RW_COLS), F32),
                   jax.ShapeDtypeStruct((1, RW_COLS), F32)] + [jax.ShapeDtypeStruct((1, W), F32)] * 4
                  + [jax.ShapeDtypeStruct((2 * LORA, 2 * W), F32)],
        compiler_params=_cparams(dimension_semantics=("arbitrary",)),
    )(p_rw, bnd, mu, w0, a0, k_k, k_a, lora, seg64, tri, *cts)


def _post_fwd(o, rk, v, g, gn_g, gn_b, r_k, seg64):
    T = o.shape[0]
    tm = _row_tile(T)
    W = RW_WIDTH

    def body(o_ref, rk_ref, v_ref, g_ref, gg_ref, gb_ref, rkp_ref, seg_ref, y_ref):
        y_ref[...] = _post_fn(o_ref[...], rk_ref[...], v_ref[...], g_ref[...], gg_ref[...], gb_ref[...],
                              rkp_ref[...], seg_ref[...]).astype(BF16)

    row = pl.BlockSpec((tm, W), lambda i: (i, 0))
    small = _full((1, W))
    return pl.pallas_call(
        body, name="rwkv_post_fwd", grid=(T // tm,),
        in_specs=[row] * 4 + [small] * 3 + [_full((W, W))],
        out_specs=row, out_shape=jax.ShapeDtypeStruct((T, W), BF16),
        compiler_params=_cparams(dimension_semantics=("arbitrary",)),
    )(o, rk, v, g, gn_g, gn_b, r_k, seg64)


def _post_bwd(o, rk, v, g, gn_g, gn_b, r_k, seg64, dy):
    T = o.shape[0]
    tm = _row_tile(T)
    W = RW_WIDTH

    def body(o_ref, rk_ref, v_ref, g_ref, gg_ref, gb_ref, rkp_ref, seg_ref, dy_ref,
             do_ref, drk_ref, dv_ref, dg_ref, dgg_ref, dgb_ref, drkp_ref):
        accs = (dgg_ref, dgb_ref, drkp_ref)

        @pl.when(pl.program_id(0) == 0)
        def _():
            for a_ref in accs:
                a_ref[...] = jnp.zeros_like(a_ref)

        seg = seg_ref[...]
        _, vjp = jax.vjp(lambda *a: _post_fn(*a, seg), o_ref[...], rk_ref[...], v_ref[...], g_ref[...],
                         gg_ref[...], gb_ref[...], rkp_ref[...])
        grads = vjp(dy_ref[...])
        for o_, gval in zip((do_ref, drk_ref, dv_ref, dg_ref), grads[:4]):
            o_[...] = gval
        for a_ref, gval in zip(accs, grads[4:]):
            a_ref[...] += gval

    row = pl.BlockSpec((tm, W), lambda i: (i, 0))
    small = _full((1, W))
    return pl.pallas_call(
        body, name="rwkv_post_bwd", grid=(T // tm,),
        in_specs=[row] * 4 + [small] * 3 + [_full((W, W)), pl.BlockSpec((tm, W), lambda i: (i, 1))],
        out_specs=[row] * 4 + [small] * 3,
        out_shape=[jax.ShapeDtypeStruct((T, W), F32)] * 4 + [jax.ShapeDtypeStruct((1, W), F32)] * 3,
        compiler_params=_cparams(dimension_semantics=("arbitrary",)),
    )(o, rk, v, g, gn_g, gn_b, r_k, seg64, dy)


def _wkv_lhs(hm_ref, rows):
    tiles = [jnp.transpose(hm_ref[p, :, rows, :].reshape(RW_HEADS * WKV_CHUNK, 128)) for p in range(N_PAIR)]
    hi, lo = _split(jnp.concatenate(tiles, axis=0)[:N_VEC * RW_HEAD])
    return jnp.concatenate([hi, lo], axis=1)


def _wkv_group(nch):
    return min(WKV_GROUP, nch)


N_STEP_VEC = N_VEC - 1
PAD_ROWS = 16


def _wkv_fwd(cols, v, e_tab):
    T = v.shape[0]
    Tc = WKV_CHUNK
    nch = T // Tc
    G = _wkv_group(nch)
    J, W = RW_HEAD, RW_WIDTH
    JS = N_STEP_VEC * J

    def body(cols_ref, v_ref, e_ref, o_ref, states_ref, sa_ref, s_ref):
        @pl.when(pl.program_id(0) == 0)
        def _():
            s_ref[...] = jnp.zeros_like(s_ref)

        st = s_ref[...]
        for c in range(G):
            lhs = _wkv_lhs(cols_ref, slice(c * Tc, (c + 1) * Tc))
            for t in range(Tc):
                row = slice(c * Tc + t, c * Tc + t + 1)
                ex = _dot(lhs[:JS], e_ref[t])
                states_ref[c * Tc + t] = st
                sa = -jnp.sum(st * ex[0:J], axis=0, keepdims=True)
                st = st + ex[J:2 * J] * sa + ex[2 * J:3 * J] * v_ref[row, :]
                sa_ref[row, :] = sa
                o_ref[row, :] = jnp.sum(st * ex[3 * J:4 * J], axis=0, keepdims=True)
            st = st * _dot(lhs[JS:], e_ref[Tc - 1])
        s_ref[...] = st

    GT = G * Tc
    return pl.pallas_call(
        body, name="wkv_fwd", grid=(nch // G,),
        in_specs=[pl.BlockSpec((N_PAIR, RW_HEADS, GT, 128), lambda c: (0, 0, c, 0)),
                  pl.BlockSpec((GT, W), lambda c: (c, 0)), _full((Tc, 2 * 128, W))],
        out_specs=[pl.BlockSpec((GT, W), lambda c: (c, 0)), pl.BlockSpec((GT, J, W), lambda c: (c, 0, 0)),
                   pl.BlockSpec((GT, W), lambda c: (c, 0))],
        out_shape=[jax.ShapeDtypeStruct((T, W), F32), jax.ShapeDtypeStruct((T, J, W), F32),
                   jax.ShapeDtypeStruct((T, W), F32)],
        scratch_shapes=[pltpu.VMEM((J, W), F32)],
        compiler_params=_cparams(dimension_semantics=("arbitrary",)),
    )(cols, v, e_tab)


def _wkv_bwd(cols, v, do, states, sa, e_tab, r_tab):
    T = v.shape[0]
    Tc = WKV_CHUNK
    nch = T // Tc
    G = _wkv_group(nch)
    ngr = nch // G
    J, W = RW_HEAD, RW_WIDTH
    JS = N_STEP_VEC * J
    blocks = [slice(b * 128, (b + 1) * 128) for b in range(W // 128)]

    def body(cols_ref, v_ref, do_ref, states_ref, sa_ref, e_ref, r_ref, dv_ref, dhm_ref, ds_ref):
        @pl.when(pl.program_id(0) == 0)
        def _():
            ds_ref[...] = jnp.zeros_like(ds_ref)

        d_carry = [ds_ref[:, b] for b in blocks]
        last = Tc - 1
        for c in reversed(range(G)):
            at = c * Tc
            rows = slice(at, at + Tc)
            lhs = _wkv_lhs(cols_ref, rows)
            ex = _dot(lhs, e_ref[last])
            dst, ends = [], []
            for i, b in enumerate(blocks):
                s_end = (states_ref[at + last, :, b] + ex[J:2 * J, b] * sa_ref[at + last:at + Tc, b]
                         + ex[2 * J:3 * J, b] * v_ref[at + last:at + Tc, b])
                ends.append((d_carry[i] * s_end).astype(BF16))
                dst.append(d_carry[i] * ex[JS:, b])
            d_decay = _dot(jnp.concatenate(ends, axis=1), r_ref[last])
            acc = jnp.zeros((JS + PAD_ROWS, 128), F32)
            for t in reversed(range(Tc)):
                row = slice(at + t, at + t + 1)
                if t != last:
                    ex = _dot(lhs[:JS], e_ref[t])
                dvs, prods = [], []
                for i, b in enumerate(blocks):
                    kk_e, b_e, k_e, r_e = (ex[n * J:(n + 1) * J, b] for n in range(N_STEP_VEC))
                    do_row, v_row, sa_row = do_ref[row, b], v_ref[row, b], sa_ref[row, b]
                    s_old = states_ref[at + t, :, b]
                    s_new = states_ref[at + t + 1, :, b] if t != last else s_old + b_e * sa_row + k_e * v_row
                    dsn = dst[i] + r_e * do_row
                    dsa = jnp.sum(dsn * b_e, axis=0, keepdims=True)
                    dvs.append(jnp.sum(dsn * k_e, axis=0, keepdims=True))
                    prods.append(jnp.concatenate(
                        [s_old * (-dsa), dsn * sa_row, dsn * v_row, s_new * do_row, jnp.zeros((PAD_ROWS, 128), F32)],
                        axis=0).astype(BF16))
                    dst[i] = dsn - kk_e * dsa
                dv_ref[row, :] = jnp.concatenate(dvs, axis=1)
                acc = acc + _dot(jnp.concatenate(prods, axis=1), r_ref[t])
            d_carry = dst
            tiles = jnp.concatenate([acc[:JS], d_decay, jnp.zeros((2 * N_PAIR * J - N_VEC * J, 128), F32)], axis=0)
            for p in range(N_PAIR):
                dhm_ref[p, :, rows, :] = jnp.transpose(tiles[p * 128:(p + 1) * 128]).reshape(RW_HEADS, Tc, 128)
        for i, b in enumerate(blocks):
            ds_ref[:, b] = d_carry[i]

    GT = G * Tc
    rev2 = lambda c: (ngr - 1 - c, 0)
    rev3 = lambda c: (ngr - 1 - c, 0, 0)
    rev_hm = lambda c: (0, 0, ngr - 1 - c, 0)
    hm_spec = pl.BlockSpec((N_PAIR, RW_HEADS, GT, 128), rev_hm)
    return pl.pallas_call(
        body, name="wkv_bwd", grid=(ngr,),
        in_specs=[hm_spec, pl.BlockSpec((GT, W), rev2), pl.BlockSpec((GT, W), rev2),
                  pl.BlockSpec((GT, J, W), rev3), pl.BlockSpec((GT, W), rev2),
                  _full((Tc, 2 * 128, W)), _full((Tc, W, 128))],
        out_specs=[pl.BlockSpec((GT, W), rev2), hm_spec],
        out_shape=[jax.ShapeDtypeStruct((T, W), F32), jax.ShapeDtypeStruct((N_PAIR, RW_HEADS, T, 128), F32)],
        scratch_shapes=[pltpu.VMEM((J, W), F32)],
        compiler_params=_cparams(dimension_semantics=("arbitrary",)),
    )(cols, v, do, states, sa, e_tab, r_tab)


def _outproj(x, y_ret, y_rw, w_out_b, target, gf):
    T = x.shape[0]
    tm = _row_tile(T)
    W = RW_WIDTH

    def body(x_ref, yr_ref, yw_ref, w_ref, t_ref, gf_ref, loss_ref, dh_ref, dy_ref, dw_ref, dgf_ref):
        @pl.when(pl.program_id(0) == 0)
        def _():
            loss_ref[...] = jnp.zeros_like(loss_ref)
            dw_ref[...] = jnp.zeros_like(dw_ref)
            dgf_ref[...] = jnp.zeros_like(dgf_ref)

        y = jnp.concatenate([yr_ref[...], yw_ref[...]], axis=1)
        w = w_ref[...]
        h = x_ref[...] + _dot(y, w)
        rstd = lax.rsqrt(jnp.mean(h * h, axis=-1, keepdims=True) + RMS_EPS)
        hn = h * rstd
        gfv = gf_ref[...]
        err = hn * gfv - t_ref[...]
        loss_ref[...] += 0.5 * jnp.sum(jnp.mean(err * err, axis=-1))
        dout = err * (1.0 / D_MODEL)
        dgf_ref[...] += jnp.sum(dout * hn, axis=0, keepdims=True)
        dhn = dout * gfv
        dh = rstd * (dhn - hn * jnp.mean(dhn * hn, axis=-1, keepdims=True))
        dh_ref[...] = dh
        dhb = dh.astype(BF16)
        dy_ref[...] = _dot_nt(dhb, w)
        dw_ref[...] += _dot_tn(y, dhb)

    return pl.pallas_call(
        body, name="outproj_loss", grid=(T // tm,),
        in_specs=[pl.BlockSpec((tm, D_MODEL), lambda i: (i, 0)), pl.BlockSpec((tm, W), lambda i: (i, 0)),
                  pl.BlockSpec((tm, W), lambda i: (i, 0)), _full((D_MODEL, D_MODEL)),
                  pl.BlockSpec((tm, D_MODEL), lambda i: (i, 0)), _full((1, D_MODEL))],
        out_specs=[_full((1, PACK_W)), pl.BlockSpec((tm, D_MODEL), lambda i: (i, 0)),
                   pl.BlockSpec((tm, D_MODEL), lambda i: (i, 0)), _full((D_MODEL, D_MODEL)), _full((1, D_MODEL))],
        out_shape=[jax.ShapeDtypeStruct((1, PACK_W), F32), jax.ShapeDtypeStruct((T, D_MODEL), F32),
                   jax.ShapeDtypeStruct((T, D_MODEL), F32), jax.ShapeDtypeStruct((D_MODEL, D_MODEL), F32),
                   jax.ShapeDtypeStruct((1, D_MODEL), F32)],
        compiler_params=_cparams(dimension_semantics=("arbitrary",)),
    )(x, y_ret, y_rw, w_out_b, target, gf)


def _inproj_bwd_x(dp_ret, dp_rw, dprev, dfirst, w_t, x, norm_g, dh):
    T = x.shape[0]
    tm = _row_tile(T)
    nt = T // tm

    def body(dpr_ref, dpw_ref, dprev_ref, dnext_ref, w_ref, x_ref, g_ref, dh_ref, gx_ref, dg_ref, dpt_ref):
        @pl.when(pl.program_id(0) == 0)
        def _():
            dg_ref[...] = jnp.zeros_like(dg_ref)

        next_row = jnp.where(pl.program_id(0) == nt - 1, 0.0, dnext_ref[0])
        dp = jnp.concatenate([dpr_ref[...], dpw_ref[...] + _shift_up(dprev_ref[...], next_row)], axis=1)
        dpt_ref[...] = jnp.transpose(dp).astype(BF16)
        du = _dot(dp.astype(BF16), w_ref[...])
        xf = x_ref[...]
        rstd = lax.rsqrt(jnp.mean(xf * xf, axis=-1, keepdims=True) + RMS_EPS)
        xn = xf * rstd
        dg_ref[...] += jnp.sum(du * xn, axis=0, keepdims=True)
        dxn = du * g_ref[...]
        gx_ref[...] = dh_ref[...] + rstd * (dxn - xn * jnp.mean(dxn * xn, axis=-1, keepdims=True))

    return pl.pallas_call(
        body, name="inproj_bwd_x", grid=(T // tm,),
        in_specs=[pl.BlockSpec((tm, RET_COLS), lambda i: (i, 0)), pl.BlockSpec((tm, RW_COLS), lambda i: (i, 0)),
                  pl.BlockSpec((tm, RW_COLS), lambda i: (i, 0)),
                  pl.BlockSpec((1, 1, RW_COLS), lambda i: (jnp.minimum(i + 1, nt - 1), 0, 0)),
                  _full((IN_COLS, D_MODEL)), pl.BlockSpec((tm, D_MODEL), lambda i: (i, 0)), _full((1, D_MODEL)),
                  pl.BlockSpec((tm, D_MODEL), lambda i: (i, 0))],
        out_specs=[pl.BlockSpec((tm, D_MODEL), lambda i: (i, 0)), _full((1, D_MODEL)),
                   pl.BlockSpec((IN_COLS, tm), lambda i: (0, i))],
        out_shape=[jax.ShapeDtypeStruct((T, D_MODEL), F32), jax.ShapeDtypeStruct((1, D_MODEL), F32),
                   jax.ShapeDtypeStruct((IN_COLS, T), BF16)],
        compiler_params=_cparams(dimension_semantics=("arbitrary",)),
    )(dp_ret, dp_rw, dprev, dfirst, w_t, x, norm_g, dh)


def _inproj_bwd_w(dp_t, u):
    T = u.shape[0]

    def body(d_ref, u_ref, o_ref):
        o_ref[...] = _dot(d_ref[...], u_ref[...])

    return pl.pallas_call(
        body, name="inproj_bwd_w", grid=(N_CHIPS,),
        in_specs=[pl.BlockSpec((IN_SHARD, T), lambda i: (i, 0)), _full((T, D_MODEL))],
        out_specs=pl.BlockSpec((IN_SHARD, D_MODEL), lambda i: (i, 0)),
        out_shape=jax.ShapeDtypeStruct((IN_COLS, D_MODEL), F32),
        compiler_params=_cparams(dimension_semantics=("arbitrary",)),
    )(dp_t, u)


def _local_step(x, target, w_in_t, w_out_b, lora, small):
    T = x.shape[0]
    tm = _row_tile(T)
    cos, sin = _rope_tables(T)
    tabs = _ret_tables()
    seg64 = _seg_matrix(RW_WIDTH, RW_HEAD)
    e_tab = _wkv_expand_table()
    r_tab = _wkv_reduce_table()
    prep_w = (small["rwkv_mu"], small["w0"], small["a0"], small["k_k"], small["k_a"], lora, seg64, _chunk_tables(tm))
    post_w = (small["rwkv_gn_g"], small["rwkv_gn_b"], small["r_k"], seg64)

    p_ret, p_rw, u, bnd = _inproj_fwd(x, small["norm_g"], w_in_t)
    y_ret, ret, s_in_all = _ret_fwd(p_ret, cos, sin, tabs, small["ret_gn_g"])
    hm, v, g, rk = _prep_fwd(p_rw, bnd, *prep_w)
    o, states, sa = _wkv_fwd(hm, v, e_tab)
    y_rw = _post_fwd(o, rk, v, g, *post_w)
    loss, dh, dy, d_w_out, d_gf = _outproj(x, y_ret, y_rw, w_out_b, target, small["final_norm_g"])

    do, d_rk, dv2, dg, d_gn_g, d_gn_b, d_r_k = _post_bwd(o, rk, v, g, *post_w, dy)
    dv1, d_hm = _wkv_bwd(hm, v, do, states, sa, e_tab, r_tab)
    dp_rw, dprev, dfirst, d_mu, d_w0, d_a0, d_k_k, d_k_a, d_lora = _prep_bwd(
        p_rw, bnd, *prep_w, (d_hm, d_rk, dv1, dv2, dg))
    dp_ret, d_ret_gn = _ret_bwd(p_ret, cos, sin, tabs, small["ret_gn_g"], ret, s_in_all, dy)
    grad_x, d_norm_g, dp_t = _inproj_bwd_x(dp_ret, dp_rw, dprev, dfirst, w_in_t, x, small["norm_g"], dh)
    d_w_in = _inproj_bwd_w(dp_t, u)

    d_small = {"norm_g": d_norm_g, "ret_gn_g": d_ret_gn, "rwkv_mu": d_mu, "w0": d_w0, "a0": d_a0, "k_k": d_k_k,
               "k_a": d_k_a, "r_k": d_r_k, "rwkv_gn_g": d_gn_g, "rwkv_gn_b": d_gn_b, "final_norm_g": d_gf}
    return loss, grad_x, d_w_in, d_w_out, d_lora, d_small


ANY = pl.BlockSpec(memory_space=pl.ANY)
CHIP_FLIPS = ((0, 1), (1, 0), (1, 1))
N_FLIPS = len(CHIP_FLIPS)
LORA_SHARD = RW_WIDTH // N_CHIPS
HALF_IN = IN_SHARD // 2
HALF_OUT = OUT_SHARD // 2


def _position():
    return lax.axis_index("x"), lax.axis_index("y"), lax.axis_index("c")


def _flip(v, f):
    return 1 - v if f else v


def _finish(local, remote, landed):
    for cp in landed:
        cp.wait_recv()
    for cp in remote:
        cp.wait_send()
    for cp in local:
        cp.wait()


def _gather_chips(arrs):
    n = len(arrs)

    def body(*refs):
        ins, outs = refs[:n], refs[n:2 * n]
        send, recv, pass_send, pass_recv = refs[2 * n:]
        x, y, c = _position()
        s = 2 * x + y
        sibling = (x, y, 1 - c)

        def copy(src, dst, sems, k, to):
            return pltpu.make_async_remote_copy(src_ref=src, dst_ref=dst, send_sem=sems[0].at[k], recv_sem=sems[1].at[k],
                                                device_id=to, device_id_type=MESH)

        remote, landed, passed, passed_in = [], [], [], []
        for a in range(n):
            for j, (fx, fy) in enumerate(CHIP_FLIPS):
                px, py = _flip(x, fx), _flip(y, fy)
                ps = 2 * px + py
                k = a * N_FLIPS + j
                remote.append(copy(ins[a].at[c], outs[a].at[s, c], (send, recv), k, (px, py, c)))
                landed.append(copy(ins[a].at[c], outs[a].at[ps, c], (send, recv), k, (px, py, c)))
                passed.append(copy(outs[a].at[ps, c], outs[a].at[ps, c], (pass_send, pass_recv), k, sibling))
                passed_in.append(copy(outs[a].at[ps, 1 - c], outs[a].at[ps, 1 - c], (pass_send, pass_recv), k, sibling))
        for cp in remote:
            cp.start()
        for arrived, onward in zip(landed, passed):
            arrived.wait_recv()
            onward.start()
        _finish([], remote + passed, passed_in)

    sems = pltpu.SemaphoreType.DMA((n * N_FLIPS,))
    return pl.pallas_call(
        body, name="gather_weights",
        in_specs=[ANY] * n, out_specs=[ANY] * n,
        out_shape=[jax.ShapeDtypeStruct((N_CHIPS,) + a.shape, a.dtype) for a in arrs],
        scratch_shapes=[sems, sems, sems, sems],
    )(*arrs)


def _pair_exchange(g_in, g_out, g_small):
    def body(gi_ref, go_ref, gs_ref, li_ref, lo_ref, ls_ref, send, recv):
        x, y, c = _position()
        peer = (x, y, 1 - c)
        srcs = (gi_ref.at[:, pl.ds((1 - c) * HALF_IN, HALF_IN), :], go_ref.at[:, pl.ds((1 - c) * HALF_OUT, HALF_OUT), :],
                gs_ref)
        remote = [pltpu.make_async_remote_copy(src_ref=src, dst_ref=dst, send_sem=send.at[k], recv_sem=recv.at[k],
                                               device_id=peer, device_id_type=MESH)
                  for k, (src, dst) in enumerate(zip(srcs, (li_ref, lo_ref, ls_ref)))]
        for cp in remote:
            cp.start()
        _finish([], remote, remote)

    return pl.pallas_call(
        body, name="pair_exchange",
        in_specs=[ANY] * 3, out_specs=[ANY] * 3,
        out_shape=[jax.ShapeDtypeStruct((N_CHIPS, HALF_IN, D_MODEL), F32),
                   jax.ShapeDtypeStruct((N_CHIPS, HALF_OUT, D_MODEL), F32),
                   jax.ShapeDtypeStruct(g_small.shape, F32)],
        scratch_shapes=[pltpu.SemaphoreType.DMA((3,)), pltpu.SemaphoreType.DMA((3,))],
    )(g_in, g_out, g_small)


def _pair_sum(g_in, g_out, g_small, l_in, l_out, l_small, c_arr):
    def body(c_ref, gi_ref, go_ref, gs_ref, li_ref, lo_ref, ls_ref, ci_ref, co_ref, cs_ref):
        ci_ref[...] = (gi_ref[...] + li_ref[...]).astype(BF16)
        co_ref[...] = (go_ref[...] + lo_ref[...]).astype(BF16)

        @pl.when(pl.program_id(0) == 0)
        def _():
            cs_ref[...] = gs_ref[...] + ls_ref[...]

    nd = g_small.shape
    return pl.pallas_call(
        body, name="pair_sum",
        grid_spec=pltpu.PrefetchScalarGridSpec(
            num_scalar_prefetch=1, grid=(N_CHIPS,),
            in_specs=[pl.BlockSpec((1, HALF_IN, D_MODEL), lambda s, c: (s, c[0], 0)),
                      pl.BlockSpec((1, HALF_OUT, D_MODEL), lambda s, c: (s, c[0], 0)),
                      pl.BlockSpec(nd, lambda s, c: (0, 0)),
                      pl.BlockSpec((1, HALF_IN, D_MODEL), lambda s, c: (s, 0, 0)),
                      pl.BlockSpec((1, HALF_OUT, D_MODEL), lambda s, c: (s, 0, 0)),
                      pl.BlockSpec(nd, lambda s, c: (0, 0))],
            out_specs=[pl.BlockSpec((1, HALF_IN, D_MODEL), lambda s, c: (s, 0, 0)),
                       pl.BlockSpec((1, HALF_OUT, D_MODEL), lambda s, c: (s, 0, 0)),
                       pl.BlockSpec(nd, lambda s, c: (0, 0))]),
        out_shape=[jax.ShapeDtypeStruct((N_CHIPS, HALF_IN, D_MODEL), BF16),
                   jax.ShapeDtypeStruct((N_CHIPS, HALF_OUT, D_MODEL), BF16), jax.ShapeDtypeStruct(nd, F32)],
        compiler_params=_cparams(dimension_semantics=("arbitrary",)),
    )(c_arr, g_in, g_out, g_small, l_in, l_out, l_small)


def _chip_exchange(c_in, c_out, c_small):
    def body(ci_ref, co_ref, cs_ref, li_ref, lo_ref, ls_ref, send, recv):
        x, y, c = _position()
        s = 2 * x + y
        remote = []
        for j, (fx, fy) in enumerate(CHIP_FLIPS):
            px, py = _flip(x, fx), _flip(y, fy)
            ps = 2 * px + py
            for a, (src, dst) in enumerate(((ci_ref.at[ps], li_ref.at[j]), (co_ref.at[ps], lo_ref.at[j]),
                                            (cs_ref, ls_ref.at[j]))):
                k = 3 * j + a
                remote.append(pltpu.make_async_remote_copy(src_ref=src, dst_ref=dst, send_sem=send.at[k],
                                                           recv_sem=recv.at[k], device_id=(px, py, c),
                                                           device_id_type=MESH))
        for cp in remote:
            cp.start()
        _finish([], remote, remote)

    return pl.pallas_call(
        body, name="chip_exchange",
        in_specs=[ANY] * 3, out_specs=[ANY] * 3,
        out_shape=[jax.ShapeDtypeStruct((N_FLIPS, HALF_IN, D_MODEL), c_in.dtype),
                   jax.ShapeDtypeStruct((N_FLIPS, HALF_OUT, D_MODEL), c_out.dtype),
                   jax.ShapeDtypeStruct((N_FLIPS,) + c_small.shape, F32)],
        scratch_shapes=[pltpu.SemaphoreType.DMA((3 * N_FLIPS,)), pltpu.SemaphoreType.DMA((3 * N_FLIPS,))],
    )(c_in, c_out, c_small)


def _chip_sum(g_in, g_out, p_in, p_out, c_small, l_in, l_out, l_small, sc_arr):
    nd = c_small.shape

    def body(s_ref, gi_ref, go_ref, pi_ref, po_ref, cs_ref, li0, li1, li2, lo0, lo1, lo2, ls_ref,
             ri_ref, ro_ref, rs_ref):
        ri_ref[...] = (((gi_ref[0] + pi_ref[0]) + li0[0].astype(F32)) + li1[0].astype(F32)) + li2[0].astype(F32)
        ro_ref[...] = (((go_ref[0] + po_ref[0]) + lo0[0].astype(F32)) + lo1[0].astype(F32)) + lo2[0].astype(F32)
        me = s_ref[0]
        parts = (cs_ref[...], ls_ref[0], ls_ref[1], ls_ref[2])

        def of_chip(s):
            m = jnp.bitwise_xor(me, s)
            return jnp.where(m == 0, parts[0], jnp.where(m == 1, parts[1], jnp.where(m == 2, parts[2], parts[3])))

        rs_ref[...] = ((of_chip(0) + of_chip(1)) + of_chip(2)) + of_chip(3)

    def flip_in(j):
        return pl.BlockSpec((1, HALF_IN, D_MODEL), lambda i, s: (j, 0, 0))

    def flip_out(j):
        return pl.BlockSpec((1, HALF_OUT, D_MODEL), lambda i, s: (j, 0, 0))

    return pl.pallas_call(
        body, name="chip_sum",
        grid_spec=pltpu.PrefetchScalarGridSpec(
            num_scalar_prefetch=1, grid=(1,),
            in_specs=[pl.BlockSpec((1, HALF_IN, D_MODEL), lambda i, s: (s[0], s[1], 0)),
                      pl.BlockSpec((1, HALF_OUT, D_MODEL), lambda i, s: (s[0], s[1], 0)),
                      pl.BlockSpec((1, HALF_IN, D_MODEL), lambda i, s: (s[0], 0, 0)),
                      pl.BlockSpec((1, HALF_OUT, D_MODEL), lambda i, s: (s[0], 0, 0)),
                      pl.BlockSpec(nd, lambda i, s: (0, 0)),
                      flip_in(0), flip_in(1), flip_in(2), flip_out(0), flip_out(1), flip_out(2),
                      pl.BlockSpec((N_FLIPS,) + nd, lambda i, s: (0, 0, 0))],
            out_specs=[pl.BlockSpec((HALF_IN, D_MODEL), lambda i, s: (0, 0)),
                       pl.BlockSpec((HALF_OUT, D_MODEL), lambda i, s: (0, 0)),
                       pl.BlockSpec(nd, lambda i, s: (0, 0))]),
        out_shape=[jax.ShapeDtypeStruct((HALF_IN, D_MODEL), F32), jax.ShapeDtypeStruct((HALF_OUT, D_MODEL), F32),
                   jax.ShapeDtypeStruct(nd, F32)],
        compiler_params=_cparams(dimension_semantics=("arbitrary",)),
    )(sc_arr, g_in, g_out, p_in, p_out, c_small, l_in, l_in, l_in, l_out, l_out, l_out, l_small)


def _pair_share(r_in, r_out):
    def body(ri_ref, ro_ref, li_ref, lo_ref, send, recv):
        x, y, c = _position()
        remote = [pltpu.make_async_remote_copy(src_ref=src, dst_ref=dst, send_sem=send.at[k], recv_sem=recv.at[k],
                                               device_id=(x, y, 1 - c), device_id_type=MESH)
                  for k, (src, dst) in enumerate(((ri_ref, li_ref), (ro_ref, lo_ref)))]
        for cp in remote:
            cp.start()
        _finish([], remote, remote)

    return pl.pallas_call(
        body, name="pair_share",
        in_specs=[ANY] * 2, out_specs=[ANY] * 2,
        out_shape=[jax.ShapeDtypeStruct(r_in.shape, F32), jax.ShapeDtypeStruct(r_out.shape, F32)],
        scratch_shapes=[pltpu.SemaphoreType.DMA((2,)), pltpu.SemaphoreType.DMA((2,))],
    )(r_in, r_out)


def _adam_update(w, g, m, v):
    mn = ADAM_B1 * m + (1.0 - ADAM_B1) * g
    vn = ADAM_B2 * v + (1.0 - ADAM_B2) * jnp.square(g)
    m_hat = mn / (1.0 - ADAM_B1 ** ADAM_STEP)
    v_hat = vn / (1.0 - ADAM_B2 ** ADAM_STEP)
    return -ADAM_LR * (m_hat / (jnp.sqrt(v_hat) + ADAM_EPS) + ADAM_WD * w), mn, vn


def _adamw(name, w, g_mine, g_theirs, m, v, core_arr, tr):
    rows, cols = w.shape
    per_half = rows // 2 // tr

    def body(c_ref, w_ref, gm_ref, gt_ref, m_ref, v_ref, g_ref, d_ref, nm_ref, nv_ref):
        mine = (pl.program_id(0) // per_half) == c_ref[0]
        g = jnp.where(mine, gm_ref[...], gt_ref[...])
        d, mn, vn = _adam_update(w_ref[...], g, m_ref[...], v_ref[...])
        g_ref[...] = g
        d_ref[...] = d
        nm_ref[...] = mn
        nv_ref[...] = vn

    spec = pl.BlockSpec((tr, cols), lambda i, c: (i, 0))
    half = pl.BlockSpec((tr, cols), lambda i, c: (i % per_half, 0))
    return pl.pallas_call(
        body, name=name,
        grid_spec=pltpu.PrefetchScalarGridSpec(
            num_scalar_prefetch=1, grid=(rows // tr,),
            in_specs=[spec, half, half, spec, spec], out_specs=[spec] * 4),
        out_shape=[jax.ShapeDtypeStruct((rows, cols), F32)] * 4,
        compiler_params=_cparams(dimension_semantics=("arbitrary",)),
    )(core_arr, w, g_mine, g_theirs, m, v)


def _row_pieces(n):
    return [(k, k * PACK_W, min(PACK_W, n - k * PACK_W)) for k in range(-(-n // PACK_W))]


def _pack_small(d_small, loss, d_lora):
    ns = len(SMALL_NAMES)

    def body(*refs):
        small_refs, (loss_ref, lora_ref, out_ref) = refs[:ns], refs[ns:]
        out_ref[...] = jnp.zeros_like(out_ref)
        out_ref[PACK_LORA_W:PACK_LORA_W + LORA, :] = lora_ref[:LORA, :RW_WIDTH]
        out_ref[PACK_LORA_A:PACK_LORA_A + LORA, :] = lora_ref[LORA:, RW_WIDTH:]
        for name, n, ref in zip(SMALL_NAMES, SMALL_SIZES, small_refs):
            for k, at, w in _row_pieces(n):
                out_ref[PACK_AT[name] + k:PACK_AT[name] + k + 1, 0:w] = ref[:, at:at + w]
        out_ref[PACK_LOSS:PACK_LOSS + 1, :] = loss_ref[...]

    return pl.pallas_call(body, name="pack_small", out_shape=jax.ShapeDtypeStruct((PACK_ROWS, PACK_W), F32),
                          compiler_params=_cparams())(*d_small, loss, d_lora)


def _adamw_small(tot, chip_arr, ws, ms, vs):
    ns = len(SMALL_NAMES)
    n_par = ns + 2

    def body(s_ref, tot_ref, glw_ref, gla_ref, *refs):
        w_refs, m_refs, v_refs = refs[:n_par], refs[n_par:2 * n_par], refs[2 * n_par:3 * n_par]
        outs = refs[3 * n_par:]
        g_refs, d_refs, nm_refs, nv_refs = (outs[i * n_par:(i + 1) * n_par] for i in range(4))
        grads = [jnp.concatenate([tot_ref[PACK_AT[name] + k:PACK_AT[name] + k + 1, 0:w] for k, _, w in _row_pieces(n)],
                                 axis=1) for name, n in zip(SMALL_NAMES, SMALL_SIZES)]
        grads += [glw_ref[...], gla_ref[...]]
        for i, g in enumerate(grads):
            d, mn, vn = _adam_update(w_refs[i][...], g, m_refs[i][...], v_refs[i][...])
            g_refs[i][...] = g
            d_refs[i][...] = d
            nm_refs[i][...] = mn
            nv_refs[i][...] = vn

    def whole(a):
        nd = a.ndim
        return pl.BlockSpec(a.shape, lambda i, s: (0,) * nd)

    shard = (LORA, LORA_SHARD)
    par_specs = [whole(a) for a in ws]
    res = pl.pallas_call(
        body, name="adamw_small",
        grid_spec=pltpu.PrefetchScalarGridSpec(
            num_scalar_prefetch=1, grid=(1,),
            in_specs=[whole(tot), pl.BlockSpec(shard, lambda i, s: (PACK_LORA_W // LORA, s[0])),
                      pl.BlockSpec(shard, lambda i, s: (PACK_LORA_A // LORA, s[0]))] + par_specs * 3,
            out_specs=par_specs * 4),
        out_shape=[jax.ShapeDtypeStruct(a.shape, F32) for a in ws] * 4,
        compiler_params=_cparams(dimension_semantics=("arbitrary",)),
    )(chip_arr, tot, tot, tot, *ws, *ms, *vs)
    return [res[i * n_par:(i + 1) * n_par] for i in range(4)]


def kernel(x, norm_g, w_in, ret_gn_g, rwkv_mu, w_lora_up, w0, a_lora_up, a0, k_k, k_a, r_k, rwkv_gn_g, rwkv_gn_b, w_out, final_norm_g, loss_target, m_norm_g, m_w_in, m_ret_gn_g, m_rwkv_mu, m_w_lora_up, m_w0, m_a_lora_up, m_a0, m_k_k, m_k_a, m_r_k, m_rwkv_gn_g, m_rwkv_gn_b, m_w_out, m_final_norm_g, v_norm_g, v_w_in, v_ret_gn_g, v_rwkv_mu, v_w_lora_up, v_w0, v_a_lora_up, v_a0, v_k_k, v_k_a, v_r_k, v_rwkv_gn_g, v_rwkv_gn_b, v_w_out, v_final_norm_g):
    W = RW_WIDTH
    params = dict(norm_g=norm_g, ret_gn_g=ret_gn_g, rwkv_mu=rwkv_mu, w0=w0, a0=a0, k_k=k_k, k_a=k_a, r_k=r_k,
                  rwkv_gn_g=rwkv_gn_g, rwkv_gn_b=rwkv_gn_b, final_norm_g=final_norm_g)
    moments_m = dict(norm_g=m_norm_g, ret_gn_g=m_ret_gn_g, rwkv_mu=m_rwkv_mu, w0=m_w0, a0=m_a0, k_k=m_k_k, k_a=m_k_a,
                     r_k=m_r_k, rwkv_gn_g=m_rwkv_gn_g, rwkv_gn_b=m_rwkv_gn_b, final_norm_g=m_final_norm_g)
    moments_v = dict(norm_g=v_norm_g, ret_gn_g=v_ret_gn_g, rwkv_mu=v_rwkv_mu, w0=v_w0, a0=v_a0, k_k=v_k_k, k_a=v_k_a,
                     r_k=v_r_k, rwkv_gn_g=v_rwkv_gn_g, rwkv_gn_b=v_rwkv_gn_b, final_norm_g=v_final_norm_g)
    xi, yi, ci = _position()
    chip = (2 * xi + yi).astype(jnp.int32)

    def halves(a):
        return a.reshape(2, a.shape[0] // 2, a.shape[1])

    w_t, m_t, v_t = (jnp.transpose(a[0]) for a in (w_in, m_w_in, v_w_in))
    mine = [halves(w_t.astype(BF16)), halves(w_out[0].astype(BF16)), halves(w_lora_up[0]), halves(a_lora_up[0])]
    g_in, g_out, g_lw, g_la = [lax.dynamic_update_slice(g, own[None], (chip, 0, 0, 0))
                               for g, own in zip(_gather_chips(mine), mine)]
    w_in_t = g_in.reshape(IN_COLS, D_MODEL)
    w_out_b = g_out.reshape(D_MODEL, D_MODEL)
    lw = jnp.transpose(g_lw.reshape(N_CHIPS, LORA, LORA_SHARD), (1, 0, 2)).reshape(LORA, W)
    la = jnp.transpose(g_la.reshape(N_CHIPS, LORA, LORA_SHARD), (1, 0, 2)).reshape(LORA, W)
    zero = jnp.zeros((LORA, W), F32)
    lora = jnp.concatenate([jnp.concatenate([lw, zero], axis=1), jnp.concatenate([zero, la], axis=1)], axis=0)
    small = {n: params[n].reshape(1, -1) for n in SMALL_NAMES}

    loss, grad_x, d_w_in, d_w_out, d_lora, d_small = _local_step(x[0], loss_target[0], w_in_t, w_out_b, lora, small)

    core = ci.astype(jnp.int32)
    gi = d_w_in.reshape(N_CHIPS, IN_SHARD, D_MODEL)
    go = d_w_out.reshape(N_CHIPS, OUT_SHARD, D_MODEL)
    gs = _pack_small([d_small[n] for n in SMALL_NAMES], loss, d_lora)
    p_in, p_out, p_small = _pair_exchange(gi, go, gs)
    c_in, c_out, c_small = _pair_sum(gi, go, gs, p_in, p_out, p_small, core.reshape(1))
    l_in, l_out, l_small = _chip_exchange(c_in, c_out, c_small)
    r_in, r_out, tot = _chip_sum(gi, go, p_in, p_out, c_small, l_in, l_out, l_small, jnp.stack([chip, core]))
    t_in, t_out = _pair_share(r_in, r_out)

    grad_w_in, d_in, nm_in, nv_in = (jnp.transpose(a) for a in _adamw(
        "adamw_w_in", w_t, r_in, t_in, m_t, v_t, core.reshape(1), HALF_IN // 2))
    grad_w_out, d_out, nm_out, nv_out = _adamw("adamw_w_out", w_out[0], r_out, t_out, m_w_out[0], v_w_out[0],
                                               core.reshape(1), HALF_OUT)
    par_names = SMALL_NAMES + ("w_lora_up", "a_lora_up")

    def operands(tree, lw_, la_):
        return [tree[n].reshape(1, -1) for n in SMALL_NAMES] + [lw_[0], la_[0]]

    res = _adamw_small(tot, chip.reshape(1), operands(params, w_lora_up, a_lora_up),
                       operands(moments_m, m_w_lora_up, m_a_lora_up), operands(moments_v, v_w_lora_up, v_a_lora_up))

    names = ("norm_g", "w_in", "ret_gn_g", "rwkv_mu", "w_lora_up", "w0", "a_lora_up", "a0", "k_k", "k_a", "r_k",
             "rwkv_gn_g", "rwkv_gn_b", "w_out", "final_norm_g")
    shapes = dict(w_in=w_in.shape, w_out=w_out.shape, w_lora_up=w_lora_up.shape, a_lora_up=a_lora_up.shape,
                  **{n: params[n].shape for n in SMALL_NAMES})

    def leaves(pars, big_in, big_out):
        tree = dict(zip(par_names, pars), w_in=big_in, w_out=big_out)
        return [tree[n].reshape(shapes[n]) for n in names]

    grads = leaves(res[0], grad_w_in, grad_w_out)
    deltas = leaves(res[1], d_in, d_out)
    new_m = leaves(res[2], nm_in, nm_out)
    new_v = leaves(res[3], nv_in, nv_out)
    return (tot[PACK_LOSS, 0], grad_x.reshape(x.shape), *grads, *deltas, *new_m, *new_v)
```

```python
import functools

import numpy as np
import jax
import jax.numpy as jnp
from jax import lax
from jax.experimental import pallas as pl
from jax.experimental.pallas import tpu as pltpu

F32 = jnp.float32
BF16 = jnp.bfloat16
X3 = "bf16x3"
B1 = "bf16"
MESH = pl.DeviceIdType.MESH

D_MODEL = 1024
N_CHIPS = 4
RET_HEADS = 4
RET_DK = 64
RET_DV = 128
RET_QK = RET_HEADS * RET_DK
RET_WIDTH = RET_HEADS * RET_DV
RET_COLS = 2 * RET_QK + 2 * RET_WIDTH
RET_CHUNK = 64
RET_GROUP = 8
RW_WIDTH = 512
RW_HEAD = 64
RW_HEADS = 8
LORA = 64
RW_COLS = 4 * RW_WIDTH + 2 * LORA
IN_COLS = RET_COLS + RW_COLS
IN_SHARD = IN_COLS // N_CHIPS
OUT_SHARD = D_MODEL // N_CHIPS
ROPE_BASE = 10000.0
RMS_EPS = 1e-6
RET_GN_EPS = 1e-5
RW_GN_EPS = 64e-5
WKV_CHUNK = 16
WKV_GROUP = 8
N_VEC = 5

ADAM_LR = 0.001
ADAM_B1 = 0.9
ADAM_B2 = 0.999
ADAM_EPS = 1e-08
ADAM_WD = 0.01
ADAM_STEP = 10

VMEM_LIMIT = 56 * 1024 * 1024

PACK_W = 512
SMALL_NAMES = ("norm_g", "ret_gn_g", "rwkv_mu", "w0", "a0", "k_k", "k_a", "r_k", "rwkv_gn_g", "rwkv_gn_b",
               "final_norm_g")
SMALL_SIZES = (1024, 512, 2176, 512, 512, 512, 512, 512, 512, 512, 1024)
PACK_LORA_W = 0
PACK_LORA_A = LORA
PACK_SMALL = 2 * LORA


def _pack_layout():
    rows, at = {}, PACK_SMALL
    for name, n in zip(SMALL_NAMES, SMALL_SIZES):
        rows[name] = at
        at += -(-n // PACK_W)
    return rows, at


PACK_AT, PACK_LOSS = _pack_layout()
PACK_ROWS = -(-(PACK_LOSS + 1) // 8) * 8


def _cparams(**kw):
    return pltpu.CompilerParams(vmem_limit_bytes=VMEM_LIMIT, **kw)


def _split(x):
    hi = x.astype(BF16)
    lo = (x - hi.astype(F32)).astype(BF16)
    return hi, lo


def _dot_dims(a, b, dims, precision):
    if precision == B1:
        a, b = a.astype(BF16), b.astype(BF16)
    if precision != X3:
        return lax.dot_general(a, b, dims, preferred_element_type=F32)
    (ah, al), (bh, bl) = _split(a), _split(b)
    dot = lambda u, w: lax.dot_general(u, w, dims, preferred_element_type=F32)
    return dot(ah, bh) + dot(ah, bl) + dot(al, bh)


def _dot(a, b, precision=None):
    return _dot_dims(a, b, (((1,), (0,)), ((), ())), precision)


def _dot_nt(a, b, precision=None):
    return _dot_dims(a, b, (((1,), (1,)), ((), ())), precision)


def _dot_tn(a, b, precision=None):
    return _dot_dims(a, b, (((0,), (0,)), ((), ())), precision)


@jax.custom_vjp
def _segsum(x, seg):
    hi, lo = _split(x)
    return _dot(hi, seg) + _dot(lo, seg)


def _segsum_fwd(x, seg):
    return _segsum(x, seg), seg


def _segsum_bwd(seg, ct):
    return _segsum(ct, seg), jnp.zeros_like(seg)


_segsum.defvjp(_segsum_fwd, _segsum_bwd)


def _softplus(z):
    return jnp.maximum(z, 0.0) + jnp.log(1.0 + jnp.exp(-jnp.abs(z)))


def _full(shape):
    nd = len(shape)
    return pl.BlockSpec(shape, lambda *_: (0,) * nd)


def _rope_tables(T):
    half = RET_DK // 2
    expo = -jnp.arange(half, dtype=F32) / jnp.float32(half)
    freqs = jnp.exp(expo * jnp.float32(np.log(ROPE_BASE)))
    ang = jnp.arange(T, dtype=jnp.int32).astype(F32)[:, None] * freqs[None, :]
    cos = jnp.tile(jnp.cos(ang), (1, 2 * RET_HEADS))
    sin = jnp.tile(jnp.sin(ang), (1, 2 * RET_HEADS))
    return cos, sin


def _ret_tables():
    H, C = RET_HEADS, RET_CHUNK
    hidx = jnp.arange(H, dtype=F32)
    lg = jnp.log(1.0 - jnp.exp2(-5.0 - hidx))
    idx = jnp.arange(C, dtype=F32)
    intra = jnp.exp(lg[:, None, None] * jnp.abs(idx[:, None] - idx[None, :]))
    q_dec = jnp.transpose(jnp.exp(lg[:, None] * (idx[None, :] + 1.0)))
    k_dec = jnp.transpose(jnp.exp(lg[:, None] * (C - 1.0 - idx[None, :])))
    chunk_dec = jnp.exp(lg * C)
    qd = jnp.repeat(q_dec, RET_DK, axis=1)
    kd = jnp.repeat(k_dec, RET_DK, axis=1)
    row_h = np.arange(RET_QK) // RET_DK
    col_h = np.arange(RET_WIDTH) // RET_DV
    bm = jnp.asarray((row_h[:, None] == col_h[None, :]).astype(np.float32))
    cd = bm * jnp.repeat(chunk_dec, RET_DK)[:, None]
    return intra, qd, kd, cd, bm


def _seg_matrix(width, head):
    h = np.arange(width) // head
    return jnp.asarray((h[:, None] == h[None, :]).astype(np.float32), dtype=BF16)


def _wkv_expand_table():
    Tc = WKV_CHUNK
    k = np.arange(2 * RW_HEADS * Tc)
    kh, kt = (k % (RW_HEADS * Tc)) // Tc, k % Tc
    nh = np.arange(RW_WIDTH) // RW_HEAD
    e = (kh[None, :, None] == nh[None, None, :]) & (kt[None, :, None] == np.arange(Tc)[:, None, None])
    return jnp.asarray(e.astype(np.float32), dtype=BF16)


def _wkv_reduce_table():
    Tc = WKV_CHUNK
    kh = np.arange(RW_WIDTH) // RW_HEAD
    n = np.arange(RW_HEADS * Tc)
    nh, nt = n // Tc, n % Tc
    r = (kh[None, :, None] == nh[None, None, :]) & (nt[None, None, :] == np.arange(Tc)[:, None, None])
    return jnp.asarray(r.astype(np.float32), dtype=BF16)


def _inproj_fwd(x, norm_g, w_t, late):
    T = x.shape[0]
    tm = _row_tile(T)
    nt = T // tm
    n = len(late)

    def body(x_ref, g_ref, w_ref, *refs):
        late_in, (pret_ref, prw_ref, u_ref, last_ref) = refs[:n], refs[n:n + 4]
        late_out, sems = refs[n + 4:2 * n + 4], refs[2 * n + 4:]

        @pl.when(pl.program_id(0) == 0)
        def _():
            _gather_start(_gather_copies(late_in, late_out, sems))

        xf = x_ref[...]
        rstd = lax.rsqrt(jnp.mean(xf * xf, axis=-1, keepdims=True) + RMS_EPS)
        ub = ((xf * rstd) * g_ref[...]).astype(BF16)
        u_ref[...] = ub
        pret_ref[...] = _dot_nt(ub, w_ref[:RET_COLS, :])
        p_rw = _dot_nt(ub, w_ref[RET_COLS:, :])
        prw_ref[...] = p_rw
        last_ref[0] = p_rw[tm - 1:tm, :]

        @pl.when(pl.program_id(0) == nt - 1)
        def _():
            _gather_finish(_gather_copies(late_in, late_out, sems))

    return pl.pallas_call(
        body, name="inproj_fwd", grid=(nt,),
        in_specs=[pl.BlockSpec((tm, D_MODEL), lambda i: (i, 0)), _full((1, D_MODEL)), _full((IN_COLS, D_MODEL))]
                 + [ANY] * n,
        out_specs=[pl.BlockSpec((tm, RET_COLS), lambda i: (i, 0)), pl.BlockSpec((tm, RW_COLS), lambda i: (i, 0)),
                   pl.BlockSpec((tm, D_MODEL), lambda i: (i, 0)), pl.BlockSpec((1, 1, RW_COLS), lambda i: (i, 0, 0))]
                  + [ANY] * n,
        out_shape=[jax.ShapeDtypeStruct((T, RET_COLS), F32), jax.ShapeDtypeStruct((T, RW_COLS), F32),
                   jax.ShapeDtypeStruct((T, D_MODEL), BF16), jax.ShapeDtypeStruct((nt, 1, RW_COLS), F32)]
                  + _gather_shapes(late),
        scratch_shapes=_gather_sems(n),
        compiler_params=_cparams(dimension_semantics=("arbitrary",)),
    )(x, norm_g, w_t, *late)


def _rot_half(x):
    n = x.shape[1]
    lane = lax.broadcasted_iota(jnp.int32, x.shape, 1)
    first = (lane % RET_DK) < (RET_DK // 2)
    return jnp.where(first, -pltpu.roll(x, n - RET_DK // 2, 1), pltpu.roll(x, RET_DK // 2, 1))


def _rope(x, cos, sin):
    return x * cos + _rot_half(x) * sin


def _rope_bwd(d, cos, sin):
    return d * cos - _rot_half(d * sin)


def _ret_post(ret, g, gn_g):
    heads = []
    for h in range(RET_HEADS):
        xh = ret[:, h * RET_DV:(h + 1) * RET_DV]
        xc = xh - jnp.mean(xh, axis=-1, keepdims=True)
        heads.append(xc * lax.rsqrt(jnp.mean(xc * xc, axis=-1, keepdims=True) + RET_GN_EPS))
    return (g * jax.nn.sigmoid(g)) * (jnp.concatenate(heads, axis=1) * gn_g)


def _ret_scores(qt, kt, d_ref, h):
    lane = lax.broadcasted_iota(jnp.int32, qt.shape, 1)
    qh = jnp.where(lane // RET_DK == h, qt, 0.0)
    return qh, _dot_nt(qh, kt, B1) * d_ref[h]


def _ret_group(nch):
    return min(RET_GROUP, nch)


def _ret_fwd(p_ret, cos, sin, tabs, gn_g):
    T = p_ret.shape[0]
    C = RET_CHUNK
    nch = T // C
    G = _ret_group(nch)
    intra_d, qd, kd, cd, bm = tabs

    def body(q_ref, k_ref, v_ref, g_ref, cos_ref, sin_ref, qd_ref, kd_ref, d_ref, cd_ref, bm_ref, gn_ref,
             y_ref, ret_ref, sin_out_ref, s_ref, qt_buf, kv_buf):
        @pl.when(pl.program_id(0) == 0)
        def _():
            s_ref[...] = jnp.zeros_like(s_ref)

        cosv, sinv = cos_ref[...], sin_ref[...]
        qt_all = _rope(q_ref[...], cosv, sinv)
        kt_all = _rope(k_ref[...], cosv, sinv) * (RET_DK ** -0.5)
        for i in range(G):
            rows = slice(i * C, (i + 1) * C)
            qt, kt, v = qt_all[rows], kt_all[rows], v_ref[rows, :]
            intra = []
            for h in range(RET_HEADS):
                _, a = _ret_scores(qt, kt, d_ref, h)
                intra.append(_dot(a, v[:, h * RET_DV:(h + 1) * RET_DV], B1))
            ret_ref[rows, :] = jnp.concatenate(intra, axis=1)
            qt_buf[rows, :] = qt * qd_ref[...]
            kv_buf[i] = _dot_tn(kt * kd_ref[...], v, B1) * bm_ref[...]
        s_in = s_ref[...]
        for i in range(G):
            rows = slice(i * C, (i + 1) * C)
            sin_out_ref[i] = s_in
            ret_ref[rows, :] += _dot(qt_buf[rows, :], s_in, B1)
            s_in = s_in * cd_ref[...] + kv_buf[i]
        s_ref[...] = s_in
        y_ref[...] = _ret_post(ret_ref[...], g_ref[...], gn_ref[...]).astype(BF16)

    GC = G * C
    return pl.pallas_call(
        body, name="ret_fwd", grid=(nch // G,),
        in_specs=[pl.BlockSpec((GC, RET_QK), lambda c: (c, 0)), pl.BlockSpec((GC, RET_QK), lambda c: (c, 1)),
                  pl.BlockSpec((GC, RET_WIDTH), lambda c: (c, 1)), pl.BlockSpec((GC, RET_WIDTH), lambda c: (c, 2)),
                  pl.BlockSpec((GC, RET_QK), lambda c: (c, 0)), pl.BlockSpec((GC, RET_QK), lambda c: (c, 0)),
                  _full((C, RET_QK)), _full((C, RET_QK)), _full((RET_HEADS, C, C)),
                  _full((RET_QK, RET_WIDTH)), _full((RET_QK, RET_WIDTH)), _full((1, RET_WIDTH))],
        out_specs=[pl.BlockSpec((GC, RET_WIDTH), lambda c: (c, 0)), pl.BlockSpec((GC, RET_WIDTH), lambda c: (c, 0)),
                   pl.BlockSpec((G, RET_QK, RET_WIDTH), lambda c: (c, 0, 0))],
        out_shape=[jax.ShapeDtypeStruct((T, RET_WIDTH), BF16), jax.ShapeDtypeStruct((T, RET_WIDTH), F32),
                   jax.ShapeDtypeStruct((nch, RET_QK, RET_WIDTH), F32)],
        scratch_shapes=[pltpu.VMEM((RET_QK, RET_WIDTH), F32), pltpu.VMEM((GC, RET_QK), F32),
                        pltpu.VMEM((G, RET_QK, RET_WIDTH), F32)],
        compiler_params=_cparams(dimension_semantics=("arbitrary",)),
    )(p_ret, p_ret, p_ret, p_ret, cos, sin, qd, kd, intra_d, cd, bm, gn_g)


def _ret_bwd(p_ret, cos, sin, tabs, gn_g, ret, s_in_all, dy):
    T = p_ret.shape[0]
    C = RET_CHUNK
    nch = T // C
    G = _ret_group(nch)
    ngr = nch // G
    intra_d, qd, kd, cd, bm = tabs

    def rev(j):
        return lambda c: (ngr - 1 - c, j)

    def body(q_ref, k_ref, v_ref, g_ref, cos_ref, sin_ref, qd_ref, kd_ref, d_ref, cd_ref, bm_ref, gn_ref,
             ret_ref, sin_ref_, dy_ref, dp_ref, dgn_ref, ds_ref, dkt_buf, ktk_buf, g_buf):
        @pl.when(pl.program_id(0) == 0)
        def _():
            ds_ref[...] = jnp.zeros_like(ds_ref)
            dgn_ref[...] = jnp.zeros_like(dgn_ref)

        _, post_vjp = jax.vjp(_ret_post, ret_ref[...], g_ref[...], gn_ref[...])
        dret_all, dg_all, dgn = post_vjp(dy_ref[...])
        dgn_ref[...] += dgn
        v_cols = slice(2 * RET_QK, 2 * RET_QK + RET_WIDTH)
        dp_ref[:, 2 * RET_QK + RET_WIDTH:] = dg_all

        qdv, kdv = qd_ref[...], kd_ref[...]
        cosv, sinv = cos_ref[...], sin_ref[...]
        qt_all = _rope(q_ref[...], cosv, sinv)
        kt_all = _rope(k_ref[...], cosv, sinv) * (RET_DK ** -0.5)
        for i in range(G):
            rows = slice(i * C, (i + 1) * C)
            qt, kt, v, dret = qt_all[rows], kt_all[rows], v_ref[rows, :], dret_all[rows, :]
            dqt = qdv * _dot_nt(dret, sin_ref_[i], B1)
            dkt = jnp.zeros_like(kt)
            dvs = []
            for h in range(RET_HEADS):
                sl = slice(h * RET_DV, (h + 1) * RET_DV)
                qh, a = _ret_scores(qt, kt, d_ref, h)
                lane = lax.broadcasted_iota(jnp.int32, kt.shape, 1)
                kh = jnp.where(lane // RET_DK == h, kt, 0.0)
                da = _dot_nt(dret[:, sl], v[:, sl], B1) * d_ref[h]
                dvs.append(_dot_tn(a, dret[:, sl], B1))
                dqt = dqt + _dot(da, kh, B1)
                dkt = dkt + _dot_tn(da, qh, B1)
            dp_ref[rows, :RET_QK] = _rope_bwd(dqt, cosv[rows], sinv[rows])
            dp_ref[rows, v_cols] = jnp.concatenate(dvs, axis=1)
            dkt_buf[rows, :] = dkt
            ktk_buf[rows, :] = kt * kdv
            g_buf[i] = _dot_tn(qt * qdv, dret, B1) * bm_ref[...]
        ds_out = ds_ref[...]
        for i in reversed(range(G)):
            rows = slice(i * C, (i + 1) * C)
            dkt = dkt_buf[rows, :] + kdv * _dot_nt(v_ref[rows, :], ds_out, B1)
            dp_ref[rows, RET_QK:2 * RET_QK] = _rope_bwd(dkt * (RET_DK ** -0.5), cosv[rows], sinv[rows])
            dp_ref[rows, v_cols] += _dot(ktk_buf[rows, :], ds_out, B1)
            ds_out = ds_out * cd_ref[...] + g_buf[i]
        ds_ref[...] = ds_out

    GC = G * C
    return pl.pallas_call(
        body, name="ret_bwd", grid=(ngr,),
        in_specs=[pl.BlockSpec((GC, RET_QK), rev(0)), pl.BlockSpec((GC, RET_QK), rev(1)),
                  pl.BlockSpec((GC, RET_WIDTH), rev(1)), pl.BlockSpec((GC, RET_WIDTH), rev(2)),
                  pl.BlockSpec((GC, RET_QK), rev(0)), pl.BlockSpec((GC, RET_QK), rev(0)),
                  _full((C, RET_QK)), _full((C, RET_QK)), _full((RET_HEADS, C, C)),
                  _full((RET_QK, RET_WIDTH)), _full((RET_QK, RET_WIDTH)), _full((1, RET_WIDTH)),
                  pl.BlockSpec((GC, RET_WIDTH), rev(0)),
                  pl.BlockSpec((G, RET_QK, RET_WIDTH), lambda c: (ngr - 1 - c, 0, 0)),
                  pl.BlockSpec((GC, RET_WIDTH), rev(0))],
        out_specs=[pl.BlockSpec((GC, RET_COLS), rev(0)), _full((1, RET_WIDTH))],
        out_shape=[jax.ShapeDtypeStruct((T, RET_COLS), F32), jax.ShapeDtypeStruct((1, RET_WIDTH), F32)],
        scratch_shapes=[pltpu.VMEM((RET_QK, RET_WIDTH), F32), pltpu.VMEM((GC, RET_QK), F32),
                        pltpu.VMEM((GC, RET_QK), F32), pltpu.VMEM((G, RET_QK, RET_WIDTH), F32)],
        compiler_params=_cparams(dimension_semantics=("arbitrary",)),
    )(p_ret, p_ret, p_ret, p_ret, cos, sin, qd, kd, intra_d, cd, bm, gn_g, ret, s_in_all, dy)


@jax.custom_vjp
def _chunk_sums(x, tri):
    hi, lo = _split(x)
    return _dot(tri, hi) + _dot(tri, lo)


def _chunk_sums_fwd(x, tri):
    return _chunk_sums(x, tri), tri


def _chunk_sums_bwd(tri, ct):
    hi, lo = _split(ct)
    return _dot_tn(tri, hi) + _dot_tn(tri, lo), jnp.zeros_like(tri)


_chunk_sums.defvjp(_chunk_sums_fwd, _chunk_sums_bwd)


@jax.custom_vjp
def _lora_dot(z, lora):
    return _dot(z, lora, X3)


def _lora_dot_fwd(z, lora):
    return _lora_dot(z, lora), (z, lora)


def _lora_dot_bwd(res, ct):
    z, lora = res
    return _dot_nt(ct, lora, X3), _dot_tn(z, ct, X3)


_lora_dot.defvjp(_lora_dot_fwd, _lora_dot_bwd)


def _chunk_tables(tm):
    t = np.arange(tm)
    same = (t[:, None] // WKV_CHUNK) == (t[None, :] // WKV_CHUNK)
    return jnp.asarray(np.stack([same & (t[None, :] <= t[:, None]), same]).astype(np.float32), dtype=BF16)


def _prep_fn(p, prev, mu, w0, a0, k_k, k_a, lora, seg, tri):
    W = RW_WIDTH
    ps = p + mu * (prev - p)
    r, kr, vr, g = ps[:, 0:W], ps[:, W:2 * W], ps[:, 2 * W:3 * W], ps[:, 3 * W:4 * W]
    z = ps[:, 4 * W:]
    lane = lax.broadcasted_iota(jnp.int32, z.shape, 1)
    z = jnp.where(lane < LORA, jnp.tanh(z), z)
    lo = _lora_dot(z, lora)
    w_log = -_softplus(-(w0 + lo[:, :W])) - 0.5
    log_decay = -jnp.exp(w_log)
    cum = _chunk_sums(log_decay, tri[0])
    total = _chunk_sums(log_decay, tri[1])
    a = jax.nn.sigmoid(a0 + lo[:, W:])
    kk = kr * k_k
    kk = kk / jnp.maximum(jnp.sqrt(_segsum(kk * kk, seg)), 1e-12)
    k = kr * (1.0 + (a - 1.0) * k_a)
    grow = jnp.exp(-cum)
    return kk * jnp.exp(cum - log_decay), (kk * a) * grow, k * grow, r * jnp.exp(cum), jnp.exp(total), vr, g, r * k


def _post_fn(o, rk, v, g, gn_g, gn_b, r_k, seg):
    mu = _segsum(o, seg) * (1.0 / RW_HEAD)
    oc = o - mu
    var = _segsum(oc * oc, seg) * (1.0 / RW_HEAD)
    on = oc * lax.rsqrt(var + RW_GN_EPS) * gn_g + gn_b
    bonus = _segsum(rk * r_k, seg) * v
    return (g * jax.nn.sigmoid(g)) * (on + bonus)


N_PAIR = (N_VEC + 1) // 2
HALF_LANES = 64


def _swap_halves(x):
    return pltpu.roll(x, HALF_LANES, 1)


def _pack_heads(vecs):
    tm = vecs[0].shape[0]
    low = lax.broadcasted_iota(jnp.int32, (tm, 128), 1) < HALF_LANES
    out = []
    for p in range(N_PAIR):
        a = vecs[2 * p]
        b = vecs[2 * p + 1] if 2 * p + 1 < len(vecs) else None
        heads = []
        for m in range(RW_WIDTH // 128):
            am = a[:, m * 128:(m + 1) * 128]
            bm = jnp.zeros_like(am) if b is None else b[:, m * 128:(m + 1) * 128]
            heads.append(jnp.where(low, am, _swap_halves(bm)))
            heads.append(jnp.where(low, _swap_halves(am), bm))
        out.append(heads)
    return out


def _unpack_heads(hm_ref):
    tm = hm_ref.shape[2]
    low = lax.broadcasted_iota(jnp.int32, (tm, 128), 1) < HALF_LANES
    vecs = []
    for p in range(N_PAIR):
        a, b = [], []
        for m in range(RW_WIDTH // 128):
            even, odd = hm_ref[p, 2 * m], hm_ref[p, 2 * m + 1]
            a.append(jnp.where(low, even, _swap_halves(odd)))
            b.append(jnp.where(low, _swap_halves(even), odd))
        vecs += [jnp.concatenate(a, axis=1), jnp.concatenate(b, axis=1)]
    return vecs[:N_VEC]


def _shift_down(p, first_row):
    row = lax.broadcasted_iota(jnp.int32, p.shape, 0)
    return jnp.where(row == 0, first_row, pltpu.roll(p, 1, 0))


def _shift_up(p, last_row):
    n = p.shape[0]
    row = lax.broadcasted_iota(jnp.int32, p.shape, 0)
    return jnp.where(row == n - 1, last_row, pltpu.roll(p, n - 1, 0))


def _row_tile(T):
    return min(T, 256)


def _prep_fwd(p_rw, bnd, mu, w0, a0, k_k, k_a, lora, seg64, tri):
    T = p_rw.shape[0]
    tm = _row_tile(T)
    W = RW_WIDTH

    def body(p_ref, bnd_ref, mu_ref, w0_ref, a0_ref, kk_ref, ka_ref, lora_ref, seg_ref, tri_ref,
             hm_ref, v_ref, g_ref, rk_ref):
        p = p_ref[...]
        prev = _shift_down(p, jnp.where(pl.program_id(0) == 0, 0.0, bnd_ref[0]))
        res = _prep_fn(p, prev, mu_ref[...], w0_ref[...], a0_ref[...], kk_ref[...], ka_ref[...], lora_ref[...],
                       seg_ref[...], (tri_ref[0], tri_ref[1]))
        for pair, heads in enumerate(_pack_heads(res[:N_VEC])):
            for h, val in enumerate(heads):
                hm_ref[pair, h] = val
        v_ref[...] = res[N_VEC]
        g_ref[...] = res[N_VEC + 1]
        rk_ref[...] = res[N_VEC + 2]

    small = _full((1, W))
    row = pl.BlockSpec((tm, W), lambda i: (i, 0))
    return pl.pallas_call(
        body, name="rwkv_prep_fwd", grid=(T // tm,),
        in_specs=[pl.BlockSpec((tm, RW_COLS), lambda i: (i, 0)),
                  pl.BlockSpec((1, 1, RW_COLS), lambda i: (jnp.maximum(i - 1, 0), 0, 0)),
                  _full((1, RW_COLS)), small, small, small, small, _full((2 * LORA, 2 * W)), _full((W, W)),
                  _full((2, tm, tm))],
        out_specs=[pl.BlockSpec((N_PAIR, RW_HEADS, tm, 128), lambda i: (0, 0, i, 0)), row, row, row],
        out_shape=[jax.ShapeDtypeStruct((N_PAIR, RW_HEADS, T, 128), F32)] + [jax.ShapeDtypeStruct((T, W), F32)] * 3,
        compiler_params=_cparams(dimension_semantics=("arbitrary",)),
    )(p_rw, bnd, mu, w0, a0, k_k, k_a, lora, seg64, tri)


def _prep_bwd(p_rw, bnd, mu, w0, a0, k_k, k_a, lora, seg64, tri, cts):
    T = p_rw.shape[0]
    tm = _row_tile(T)
    W = RW_WIDTH

    def body(p_ref, bnd_ref, mu_ref, w0_ref, a0_ref, kk_ref, ka_ref, lora_ref, seg_ref, tri_ref,
             dhm_ref, drk_ref, dv1_ref, dv2_ref, dg_ref,
             dp_ref, dprev_ref, dfirst_ref, dmu_ref, dw0_ref, da0_ref, dkk_p_ref, dka_ref, dlora_ref):
        accs = (dmu_ref, dw0_ref, da0_ref, dkk_p_ref, dka_ref, dlora_ref)

        @pl.when(pl.program_id(0) == 0)
        def _():
            for a_ref in accs:
                a_ref[...] = jnp.zeros_like(a_ref)

        p = p_ref[...]
        prev = _shift_down(p, jnp.where(pl.program_id(0) == 0, 0.0, bnd_ref[0]))
        seg, tri = seg_ref[...], (tri_ref[0], tri_ref[1])
        _, vjp = jax.vjp(lambda *a: _prep_fn(*a, seg, tri), p, prev, mu_ref[...], w0_ref[...], a0_ref[...],
                         kk_ref[...], ka_ref[...], lora_ref[...])
        ct = (*_unpack_heads(dhm_ref), dv1_ref[...] + dv2_ref[...], dg_ref[...], drk_ref[...])
        grads = vjp(ct)
        dp_ref[...] = grads[0]
        dprev_ref[...] = grads[1]
        dfirst_ref[0] = grads[1][0:1, :]
        for a_ref, gval in zip(accs, grads[2:]):
            a_ref[...] += gval

    small = _full((1, W))
    row = pl.BlockSpec((tm, W), lambda i: (i, 0))
    return pl.pallas_call(
        body, name="rwkv_prep_bwd", grid=(T // tm,),
        in_specs=[pl.BlockSpec((tm, RW_COLS), lambda i: (i, 0)),
                  pl.BlockSpec((1, 1, RW_COLS), lambda i: (jnp.maximum(i - 1, 0), 0, 0)),
                  _full((1, RW_COLS)), small, small, small, small, _full((2 * LORA, 2 * W)), _full((W, W)),
                  _full((2, tm, tm)), pl.BlockSpec((N_PAIR, RW_HEADS, tm, 128), lambda i: (0, 0, i, 0))] + [row] * 4,
        out_specs=[pl.BlockSpec((tm, RW_COLS), lambda i: (i, 0)), pl.BlockSpec((tm, RW_COLS), lambda i: (i, 0)),
                   pl.BlockSpec((1, 1, RW_COLS), lambda i: (i, 0, 0)),
                   _full((1, RW_COLS)), small, small, small, small, _full((2 * LORA, 2 * W))],
        out_shape=[jax.ShapeDtypeStruct((T, RW_COLS), F32), jax.ShapeDtypeStruct((T, RW_COLS), F32),
                   jax.ShapeDtypeStruct((T // tm, 1, RW_COLS), F32),
                   jax.ShapeDtypeStruct((1, RW_COLS), F32)] + [jax.ShapeDtypeStruct((1, W), F32)] * 4
                  + [jax.ShapeDtypeStruct((2 * LORA, 2 * W), F32)],
        compiler_params=_cparams(dimension_semantics=("arbitrary",)),
    )(p_rw, bnd, mu, w0, a0, k_k, k_a, lora, seg64, tri, *cts)


def _post_fwd(o, rk, v, g, gn_g, gn_b, r_k, seg64):
    T = o.shape[0]
    tm = _row_tile(T)
    W = RW_WIDTH

    def body(o_ref, rk_ref, v_ref, g_ref, gg_ref, gb_ref, rkp_ref, seg_ref, y_ref):
        y_ref[...] = _post_fn(o_ref[...], rk_ref[...], v_ref[...], g_ref[...], gg_ref[...], gb_ref[...],
                              rkp_ref[...], seg_ref[...]).astype(BF16)

    row = pl.BlockSpec((tm, W), lambda i: (i, 0))
    small = _full((1, W))
    return pl.pallas_call(
        body, name="rwkv_post_fwd", grid=(T // tm,),
        in_specs=[row] * 4 + [small] * 3 + [_full((W, W))],
        out_specs=row, out_shape=jax.ShapeDtypeStruct((T, W), BF16),
        compiler_params=_cparams(dimension_semantics=("arbitrary",)),
    )(o, rk, v, g, gn_g, gn_b, r_k, seg64)


def _post_bwd(o, rk, v, g, gn_g, gn_b, r_k, seg64, dy):
    T = o.shape[0]
    tm = _row_tile(T)
    W = RW_WIDTH

    def body(o_ref, rk_ref, v_ref, g_ref, gg_ref, gb_ref, rkp_ref, seg_ref, dy_ref,
             do_ref, drk_ref, dv_ref, dg_ref, dgg_ref, dgb_ref, drkp_ref):
        accs = (dgg_ref, dgb_ref, drkp_ref)

        @pl.when(pl.program_id(0) == 0)
        def _():
            for a_ref in accs:
                a_ref[...] = jnp.zeros_like(a_ref)

        seg = seg_ref[...]
        _, vjp = jax.vjp(lambda *a: _post_fn(*a, seg), o_ref[...], rk_ref[...], v_ref[...], g_ref[...],
                         gg_ref[...], gb_ref[...], rkp_ref[...])
        grads = vjp(dy_ref[...])
        for o_, gval in zip((do_ref, drk_ref, dv_ref, dg_ref), grads[:4]):
            o_[...] = gval
        for a_ref, gval in zip(accs, grads[4:]):
            a_ref[...] += gval

    row = pl.BlockSpec((tm, W), lambda i: (i, 0))
    small = _full((1, W))
    return pl.pallas_call(
        body, name="rwkv_post_bwd", grid=(T // tm,),
        in_specs=[row] * 4 + [small] * 3 + [_full((W, W)), pl.BlockSpec((tm, W), lambda i: (i, 1))],
        out_specs=[row] * 4 + [small] * 3,
        out_shape=[jax.ShapeDtypeStruct((T, W), F32)] * 4 + [jax.ShapeDtypeStruct((1, W), F32)] * 3,
        compiler_params=_cparams(dimension_semantics=("arbitrary",)),
    )(o, rk, v, g, gn_g, gn_b, r_k, seg64, dy)


def _wkv_lhs(hm_ref, rows):
    tiles = [jnp.transpose(hm_ref[p, :, rows, :].reshape(RW_HEADS * WKV_CHUNK, 128)) for p in range(N_PAIR)]
    hi, lo = _split(jnp.concatenate(tiles, axis=0)[:N_VEC * RW_HEAD])
    return jnp.concatenate([hi, lo], axis=1)


def _wkv_group(nch):
    return min(WKV_GROUP, nch)


N_STEP_VEC = N_VEC - 1
PAD_ROWS = 16


def _wkv_fwd(cols, v, e_tab):
    T = v.shape[0]
    Tc = WKV_CHUNK
    nch = T // Tc
    G = _wkv_group(nch)
    J, W = RW_HEAD, RW_WIDTH
    JS = N_STEP_VEC * J

    def body(cols_ref, v_ref, e_ref, o_ref, states_ref, sa_ref, s_ref):
        @pl.when(pl.program_id(0) == 0)
        def _():
            s_ref[...] = jnp.zeros_like(s_ref)

        st = s_ref[...]
        for c in range(G):
            lhs = _wkv_lhs(cols_ref, slice(c * Tc, (c + 1) * Tc))
            for t in range(Tc):
                row = slice(c * Tc + t, c * Tc + t + 1)
                ex = _dot(lhs[:JS], e_ref[t])
                states_ref[c * Tc + t] = st
                sa = -jnp.sum(st * ex[0:J], axis=0, keepdims=True)
                st = st + ex[J:2 * J] * sa + ex[2 * J:3 * J] * v_ref[row, :]
                sa_ref[row, :] = sa
                o_ref[row, :] = jnp.sum(st * ex[3 * J:4 * J], axis=0, keepdims=True)
            st = st * _dot(lhs[JS:], e_ref[Tc - 1])
        s_ref[...] = st

    GT = G * Tc
    return pl.pallas_call(
        body, name="wkv_fwd", grid=(nch // G,),
        in_specs=[pl.BlockSpec((N_PAIR, RW_HEADS, GT, 128), lambda c: (0, 0, c, 0)),
                  pl.BlockSpec((GT, W), lambda c: (c, 0)), _full((Tc, 2 * 128, W))],
        out_specs=[pl.BlockSpec((GT, W), lambda c: (c, 0)), pl.BlockSpec((GT, J, W), lambda c: (c, 0, 0)),
                   pl.BlockSpec((GT, W), lambda c: (c, 0))],
        out_shape=[jax.ShapeDtypeStruct((T, W), F32), jax.ShapeDtypeStruct((T, J, W), F32),
                   jax.ShapeDtypeStruct((T, W), F32)],
        scratch_shapes=[pltpu.VMEM((J, W), F32)],
        compiler_params=_cparams(dimension_semantics=("arbitrary",)),
    )(cols, v, e_tab)


def _wkv_bwd(cols, v, do, states, sa, e_tab, r_tab):
    T = v.shape[0]
    Tc = WKV_CHUNK
    nch = T // Tc
    G = _wkv_group(nch)
    ngr = nch // G
    J, W = RW_HEAD, RW_WIDTH
    JS = N_STEP_VEC * J
    blocks = [slice(b * 128, (b + 1) * 128) for b in range(W // 128)]

    def body(cols_ref, v_ref, do_ref, states_ref, sa_ref, e_ref, r_ref, dv_ref, dhm_ref, ds_ref):
        @pl.when(pl.program_id(0) == 0)
        def _():
            ds_ref[...] = jnp.zeros_like(ds_ref)

        d_carry = [ds_ref[:, b] for b in blocks]
        last = Tc - 1
        for c in reversed(range(G)):
            at = c * Tc
            rows = slice(at, at + Tc)
            lhs = _wkv_lhs(cols_ref, rows)
            ex = _dot(lhs, e_ref[last])
            dst, ends = [], []
            for i, b in enumerate(blocks):
                s_end = (states_ref[at + last, :, b] + ex[J:2 * J, b] * sa_ref[at + last:at + Tc, b]
                         + ex[2 * J:3 * J, b] * v_ref[at + last:at + Tc, b])
                ends.append((d_carry[i] * s_end).astype(BF16))
                dst.append(d_carry[i] * ex[JS:, b])
            d_decay = _dot(jnp.concatenate(ends, axis=1), r_ref[last])
            acc = jnp.zeros((JS + PAD_ROWS, 128), F32)
            for t in reversed(range(Tc)):
                row = slice(at + t, at + t + 1)
                if t != last:
                    ex = _dot(lhs[:JS], e_ref[t])
                dvs, prods = [], []
                for i, b in enumerate(blocks):
                    kk_e, b_e, k_e, r_e = (ex[n * J:(n + 1) * J, b] for n in range(N_STEP_VEC))
                    do_row, v_row, sa_row = do_ref[row, b], v_ref[row, b], sa_ref[row, b]
                    s_old = states_ref[at + t, :, b]
                    s_new = states_ref[at + t + 1, :, b] if t != last else s_old + b_e * sa_row + k_e * v_row
                    dsn = dst[i] + r_e * do_row
                    dsa = jnp.sum(dsn * b_e, axis=0, keepdims=True)
                    dvs.append(jnp.sum(dsn * k_e, axis=0, keepdims=True))
                    prods.append(jnp.concatenate(
                        [s_old * (-dsa), dsn * sa_row, dsn * v_row, s_new * do_row, jnp.zeros((PAD_ROWS, 128), F32)],
                        axis=0).astype(BF16))
                    dst[i] = dsn - kk_e * dsa
                dv_ref[row, :] = jnp.concatenate(dvs, axis=1)
                acc = acc + _dot(jnp.concatenate(prods, axis=1), r_ref[t])
            d_carry = dst
            tiles = jnp.concatenate([acc[:JS], d_decay, jnp.zeros((2 * N_PAIR * J - N_VEC * J, 128), F32)], axis=0)
            for p in range(N_PAIR):
                dhm_ref[p, :, rows, :] = jnp.transpose(tiles[p * 128:(p + 1) * 128]).reshape(RW_HEADS, Tc, 128)
        for i, b in enumerate(blocks):
            ds_ref[:, b] = d_carry[i]

    GT = G * Tc
    rev2 = lambda c: (ngr - 1 - c, 0)
    rev3 = lambda c: (ngr - 1 - c, 0, 0)
    rev_hm = lambda c: (0, 0, ngr - 1 - c, 0)
    hm_spec = pl.BlockSpec((N_PAIR, RW_HEADS, GT, 128), rev_hm)
    return pl.pallas_call(
        body, name="wkv_bwd", grid=(ngr,),
        in_specs=[hm_spec, pl.BlockSpec((GT, W), rev2), pl.BlockSpec((GT, W), rev2),
                  pl.BlockSpec((GT, J, W), rev3), pl.BlockSpec((GT, W), rev2),
                  _full((Tc, 2 * 128, W)), _full((Tc, W, 128))],
        out_specs=[pl.BlockSpec((GT, W), rev2), hm_spec],
        out_shape=[jax.ShapeDtypeStruct((T, W), F32), jax.ShapeDtypeStruct((N_PAIR, RW_HEADS, T, 128), F32)],
        scratch_shapes=[pltpu.VMEM((J, W), F32)],
        compiler_params=_cparams(dimension_semantics=("arbitrary",)),
    )(cols, v, do, states, sa, e_tab, r_tab)


def _outproj(x, y_ret, y_rw, w_out_b, target, gf):
    T = x.shape[0]
    tm = _row_tile(T)
    W = RW_WIDTH

    def body(x_ref, yr_ref, yw_ref, w_ref, t_ref, gf_ref, loss_ref, dh_ref, dy_ref, dw_ref, dgf_ref):
        @pl.when(pl.program_id(0) == 0)
        def _():
            loss_ref[...] = jnp.zeros_like(loss_ref)
            dw_ref[...] = jnp.zeros_like(dw_ref)
            dgf_ref[...] = jnp.zeros_like(dgf_ref)

        y = jnp.concatenate([yr_ref[...], yw_ref[...]], axis=1)
        w = w_ref[...]
        h = x_ref[...] + _dot(y, w)
        rstd = lax.rsqrt(jnp.mean(h * h, axis=-1, keepdims=True) + RMS_EPS)
        hn = h * rstd
        gfv = gf_ref[...]
        err = hn * gfv - t_ref[...]
        loss_ref[...] += 0.5 * jnp.sum(jnp.mean(err * err, axis=-1))
        dout = err * (1.0 / D_MODEL)
        dgf_ref[...] += jnp.sum(dout * hn, axis=0, keepdims=True)
        dhn = dout * gfv
        dh = rstd * (dhn - hn * jnp.mean(dhn * hn, axis=-1, keepdims=True))
        dh_ref[...] = dh
        dhb = dh.astype(BF16)
        dy_ref[...] = _dot_nt(dhb, w)
        dw_ref[...] += _dot_tn(y, dhb)

    return pl.pallas_call(
        body, name="outproj_loss", grid=(T // tm,),
        in_specs=[pl.BlockSpec((tm, D_MODEL), lambda i: (i, 0)), pl.BlockSpec((tm, W), lambda i: (i, 0)),
                  pl.BlockSpec((tm, W), lambda i: (i, 0)), _full((D_MODEL, D_MODEL)),
                  pl.BlockSpec((tm, D_MODEL), lambda i: (i, 0)), _full((1, D_MODEL))],
        out_specs=[_full((1, PACK_W)), pl.BlockSpec((tm, D_MODEL), lambda i: (i, 0)),
                   pl.BlockSpec((tm, D_MODEL), lambda i: (i, 0)), _full((D_MODEL, D_MODEL)), _full((1, D_MODEL))],
        out_shape=[jax.ShapeDtypeStruct((1, PACK_W), F32), jax.ShapeDtypeStruct((T, D_MODEL), F32),
                   jax.ShapeDtypeStruct((T, D_MODEL), F32), jax.ShapeDtypeStruct((D_MODEL, D_MODEL), F32),
                   jax.ShapeDtypeStruct((1, D_MODEL), F32)],
        compiler_params=_cparams(dimension_semantics=("arbitrary",)),
    )(x, y_ret, y_rw, w_out_b, target, gf)


def _inproj_bwd_x(dp_ret, dp_rw, dprev, dfirst, w_t, x, norm_g, dh):
    T = x.shape[0]
    tm = _row_tile(T)
    nt = T // tm

    def body(dpr_ref, dpw_ref, dprev_ref, dnext_ref, w_ref, x_ref, g_ref, dh_ref, gx_ref, dg_ref, dpt_ref):
        @pl.when(pl.program_id(0) == 0)
        def _():
            dg_ref[...] = jnp.zeros_like(dg_ref)

        next_row = jnp.where(pl.program_id(0) == nt - 1, 0.0, dnext_ref[0])
        dp = jnp.concatenate([dpr_ref[...], dpw_ref[...] + _shift_up(dprev_ref[...], next_row)], axis=1)
        dpt_ref[...] = jnp.transpose(dp).astype(BF16)
        du = _dot(dp.astype(BF16), w_ref[...])
        xf = x_ref[...]
        rstd = lax.rsqrt(jnp.mean(xf * xf, axis=-1, keepdims=True) + RMS_EPS)
        xn = xf * rstd
        dg_ref[...] += jnp.sum(du * xn, axis=0, keepdims=True)
        dxn = du * g_ref[...]
        gx_ref[...] = dh_ref[...] + rstd * (dxn - xn * jnp.mean(dxn * xn, axis=-1, keepdims=True))

    return pl.pallas_call(
        body, name="inproj_bwd_x", grid=(T // tm,),
        in_specs=[pl.BlockSpec((tm, RET_COLS), lambda i: (i, 0)), pl.BlockSpec((tm, RW_COLS), lambda i: (i, 0)),
                  pl.BlockSpec((tm, RW_COLS), lambda i: (i, 0)),
                  pl.BlockSpec((1, 1, RW_COLS), lambda i: (jnp.minimum(i + 1, nt - 1), 0, 0)),
                  _full((IN_COLS, D_MODEL)), pl.BlockSpec((tm, D_MODEL), lambda i: (i, 0)), _full((1, D_MODEL)),
                  pl.BlockSpec((tm, D_MODEL), lambda i: (i, 0))],
        out_specs=[pl.BlockSpec((tm, D_MODEL), lambda i: (i, 0)), _full((1, D_MODEL)),
                   pl.BlockSpec((IN_COLS, tm), lambda i: (0, i))],
        out_shape=[jax.ShapeDtypeStruct((T, D_MODEL), F32), jax.ShapeDtypeStruct((1, D_MODEL), F32),
                   jax.ShapeDtypeStruct((IN_COLS, T), BF16)],
        compiler_params=_cparams(dimension_semantics=("arbitrary",)),
    )(dp_ret, dp_rw, dprev, dfirst, w_t, x, norm_g, dh)


def _inproj_bwd_w(dp_t, u):
    T = u.shape[0]

    def body(d_ref, u_ref, o_ref):
        o_ref[...] = _dot(d_ref[...], u_ref[...])

    return pl.pallas_call(
        body, name="inproj_bwd_w", grid=(N_CHIPS,),
        in_specs=[pl.BlockSpec((IN_SHARD, T), lambda i: (i, 0)), _full((T, D_MODEL))],
        out_specs=pl.BlockSpec((IN_SHARD, D_MODEL), lambda i: (i, 0)),
        out_shape=jax.ShapeDtypeStruct((IN_COLS, D_MODEL), F32),
        compiler_params=_cparams(dimension_semantics=("arbitrary",)),
    )(dp_t, u)


def _with_own(gathered, own, chip):
    return lax.dynamic_update_slice(gathered, own[None], (chip, 0, 0, 0))


def _local_step(x, target, w_in_t, late, chip, small):
    T = x.shape[0]
    tm = _row_tile(T)
    W = RW_WIDTH
    cos, sin = _rope_tables(T)
    tabs = _ret_tables()
    seg64 = _seg_matrix(RW_WIDTH, RW_HEAD)
    e_tab = _wkv_expand_table()
    r_tab = _wkv_reduce_table()

    p_ret, p_rw, u, bnd, *gathered = _inproj_fwd(x, small["norm_g"], w_in_t, late)
    g_out, g_lw, g_la = (_with_own(g, own, chip) for g, own in zip(gathered, late))
    w_out_b = g_out.reshape(D_MODEL, D_MODEL)
    lw = jnp.transpose(g_lw.reshape(N_CHIPS, LORA, LORA_SHARD), (1, 0, 2)).reshape(LORA, W)
    la = jnp.transpose(g_la.reshape(N_CHIPS, LORA, LORA_SHARD), (1, 0, 2)).reshape(LORA, W)
    zero = jnp.zeros((LORA, W), F32)
    lora = jnp.concatenate([jnp.concatenate([lw, zero], axis=1), jnp.concatenate([zero, la], axis=1)], axis=0)
    prep_w = (small["rwkv_mu"], small["w0"], small["a0"], small["k_k"], small["k_a"], lora, seg64, _chunk_tables(tm))
    post_w = (small["rwkv_gn_g"], small["rwkv_gn_b"], small["r_k"], seg64)

    y_ret, ret, s_in_all = _ret_fwd(p_ret, cos, sin, tabs, small["ret_gn_g"])
    hm, v, g, rk = _prep_fwd(p_rw, bnd, *prep_w)
    o, states, sa = _wkv_fwd(hm, v, e_tab)
    y_rw = _post_fwd(o, rk, v, g, *post_w)
    loss, dh, dy, d_w_out, d_gf = _outproj(x, y_ret, y_rw, w_out_b, target, small["final_norm_g"])

    do, d_rk, dv2, dg, d_gn_g, d_gn_b, d_r_k = _post_bwd(o, rk, v, g, *post_w, dy)
    dv1, d_hm = _wkv_bwd(hm, v, do, states, sa, e_tab, r_tab)
    dp_rw, dprev, dfirst, d_mu, d_w0, d_a0, d_k_k, d_k_a, d_lora = _prep_bwd(
        p_rw, bnd, *prep_w, (d_hm, d_rk, dv1, dv2, dg))
    dp_ret, d_ret_gn = _ret_bwd(p_ret, cos, sin, tabs, small["ret_gn_g"], ret, s_in_all, dy)
    grad_x, d_norm_g, dp_t = _inproj_bwd_x(dp_ret, dp_rw, dprev, dfirst, w_in_t, x, small["norm_g"], dh)
    d_w_in = _inproj_bwd_w(dp_t, u)

    d_small = {"norm_g": d_norm_g, "ret_gn_g": d_ret_gn, "rwkv_mu": d_mu, "w0": d_w0, "a0": d_a0, "k_k": d_k_k,
               "k_a": d_k_a, "r_k": d_r_k, "rwkv_gn_g": d_gn_g, "rwkv_gn_b": d_gn_b, "final_norm_g": d_gf}
    return loss, grad_x, d_w_in, d_w_out, d_lora, d_small


ANY = pl.BlockSpec(memory_space=pl.ANY)
CHIP_FLIPS = ((0, 1), (1, 0), (1, 1))
N_FLIPS = len(CHIP_FLIPS)
LORA_SHARD = RW_WIDTH // N_CHIPS
HALF_IN = IN_SHARD // 2
HALF_OUT = OUT_SHARD // 2


def _position():
    return lax.axis_index("x"), lax.axis_index("y"), lax.axis_index("c")


def _flip(v, f):
    return 1 - v if f else v


def _finish(local, remote, landed):
    for cp in landed:
        cp.wait_recv()
    for cp in remote:
        cp.wait_send()
    for cp in local:
        cp.wait()


def _gather_copies(ins, outs, sems):
    send, recv, pass_send, pass_recv = sems
    x, y, c = _position()
    s = 2 * x + y
    sibling = (x, y, 1 - c)

    def copy(src, dst, pair, k, to):
        return pltpu.make_async_remote_copy(src_ref=src, dst_ref=dst, send_sem=pair[0].at[k], recv_sem=pair[1].at[k],
                                            device_id=to, device_id_type=MESH)

    remote, landed, passed, passed_in = [], [], [], []
    for a in range(len(ins)):
        for j, (fx, fy) in enumerate(CHIP_FLIPS):
            px, py = _flip(x, fx), _flip(y, fy)
            ps = 2 * px + py
            k = a * N_FLIPS + j
            remote.append(copy(ins[a].at[c], outs[a].at[s, c], (send, recv), k, (px, py, c)))
            landed.append(copy(ins[a].at[c], outs[a].at[ps, c], (send, recv), k, (px, py, c)))
            passed.append(copy(outs[a].at[ps, c], outs[a].at[ps, c], (pass_send, pass_recv), k, sibling))
            passed_in.append(copy(outs[a].at[ps, 1 - c], outs[a].at[ps, 1 - c], (pass_send, pass_recv), k, sibling))
    return remote, landed, passed, passed_in


def _gather_start(copies):
    for cp in copies[0]:
        cp.start()


def _gather_finish(copies):
    remote, landed, passed, passed_in = copies
    for arrived, onward in zip(landed, passed):
        arrived.wait_recv()
        onward.start()
    _finish([], remote + passed, passed_in)


def _gather_sems(n):
    return [pltpu.SemaphoreType.DMA((n * N_FLIPS,))] * 4


def _gather_shapes(arrs):
    return [jax.ShapeDtypeStruct((N_CHIPS,) + a.shape, a.dtype) for a in arrs]


def _gather_chips(arrs):
    n = len(arrs)

    def body(*refs):
        copies = _gather_copies(refs[:n], refs[n:2 * n], refs[2 * n:])
        _gather_start(copies)
        _gather_finish(copies)

    return pl.pallas_call(
        body, name="gather_weights",
        in_specs=[ANY] * n, out_specs=[ANY] * n,
        out_shape=_gather_shapes(arrs), scratch_shapes=_gather_sems(n),
    )(*arrs)


def _pair_exchange(g_in, g_out, g_small):
    def body(gi_ref, go_ref, gs_ref, li_ref, lo_ref, ls_ref, send, recv):
        x, y, c = _position()
        peer = (x, y, 1 - c)
        srcs = (gi_ref.at[:, pl.ds((1 - c) * HALF_IN, HALF_IN), :], go_ref.at[:, pl.ds((1 - c) * HALF_OUT, HALF_OUT), :],
                gs_ref)
        remote = [pltpu.make_async_remote_copy(src_ref=src, dst_ref=dst, send_sem=send.at[k], recv_sem=recv.at[k],
                                               device_id=peer, device_id_type=MESH)
                  for k, (src, dst) in enumerate(zip(srcs, (li_ref, lo_ref, ls_ref)))]
        for cp in remote:
            cp.start()
        _finish([], remote, remote)

    return pl.pallas_call(
        body, name="pair_exchange",
        in_specs=[ANY] * 3, out_specs=[ANY] * 3,
        out_shape=[jax.ShapeDtypeStruct((N_CHIPS, HALF_IN, D_MODEL), F32),
                   jax.ShapeDtypeStruct((N_CHIPS, HALF_OUT, D_MODEL), F32),
                   jax.ShapeDtypeStruct(g_small.shape, F32)],
        scratch_shapes=[pltpu.SemaphoreType.DMA((3,)), pltpu.SemaphoreType.DMA((3,))],
    )(g_in, g_out, g_small)


def _pair_sum(g_in, g_out, g_small, l_in, l_out, l_small, c_arr):
    def body(c_ref, gi_ref, go_ref, gs_ref, li_ref, lo_ref, ls_ref, ci_ref, co_ref, cs_ref):
        ci_ref[...] = (gi_ref[...] + li_ref[...]).astype(BF16)
        co_ref[...] = (go_ref[...] + lo_ref[...]).astype(BF16)

        @pl.when(pl.program_id(0) == 0)
        def _():
            cs_ref[...] = gs_ref[...] + ls_ref[...]

    nd = g_small.shape
    return pl.pallas_call(
        body, name="pair_sum",
        grid_spec=pltpu.PrefetchScalarGridSpec(
            num_scalar_prefetch=1, grid=(N_CHIPS,),
            in_specs=[pl.BlockSpec((1, HALF_IN, D_MODEL), lambda s, c: (s, c[0], 0)),
                      pl.BlockSpec((1, HALF_OUT, D_MODEL), lambda s, c: (s, c[0], 0)),
                      pl.BlockSpec(nd, lambda s, c: (0, 0)),
                      pl.BlockSpec((1, HALF_IN, D_MODEL), lambda s, c: (s, 0, 0)),
                      pl.BlockSpec((1, HALF_OUT, D_MODEL), lambda s, c: (s, 0, 0)),
                      pl.BlockSpec(nd, lambda s, c: (0, 0))],
            out_specs=[pl.BlockSpec((1, HALF_IN, D_MODEL), lambda s, c: (s, 0, 0)),
                       pl.BlockSpec((1, HALF_OUT, D_MODEL), lambda s, c: (s, 0, 0)),
                       pl.BlockSpec(nd, lambda s, c: (0, 0))]),
        out_shape=[jax.ShapeDtypeStruct((N_CHIPS, HALF_IN, D_MODEL), BF16),
                   jax.ShapeDtypeStruct((N_CHIPS, HALF_OUT, D_MODEL), BF16), jax.ShapeDtypeStruct(nd, F32)],
        compiler_params=_cparams(dimension_semantics=("arbitrary",)),
    )(c_arr, g_in, g_out, g_small, l_in, l_out, l_small)


def _chip_exchange(c_in, c_out, c_small):
    def body(ci_ref, co_ref, cs_ref, li_ref, lo_ref, ls_ref, send, recv):
        x, y, c = _position()
        s = 2 * x + y
        remote = []
        for j, (fx, fy) in enumerate(CHIP_FLIPS):
            px, py = _flip(x, fx), _flip(y, fy)
            ps = 2 * px + py
            for a, (src, dst) in enumerate(((ci_ref.at[ps], li_ref.at[j]), (co_ref.at[ps], lo_ref.at[j]),
                                            (cs_ref, ls_ref.at[j]))):
                k = 3 * j + a
                remote.append(pltpu.make_async_remote_copy(src_ref=src, dst_ref=dst, send_sem=send.at[k],
                                                           recv_sem=recv.at[k], device_id=(px, py, c),
                                                           device_id_type=MESH))
        for cp in remote:
            cp.start()
        _finish([], remote, remote)

    return pl.pallas_call(
        body, name="chip_exchange",
        in_specs=[ANY] * 3, out_specs=[ANY] * 3,
        out_shape=[jax.ShapeDtypeStruct((N_FLIPS, HALF_IN, D_MODEL), c_in.dtype),
                   jax.ShapeDtypeStruct((N_FLIPS, HALF_OUT, D_MODEL), c_out.dtype),
                   jax.ShapeDtypeStruct((N_FLIPS,) + c_small.shape, F32)],
        scratch_shapes=[pltpu.SemaphoreType.DMA((3 * N_FLIPS,)), pltpu.SemaphoreType.DMA((3 * N_FLIPS,))],
    )(c_in, c_out, c_small)


def _chip_sum(g_in, g_out, p_in, p_out, c_small, l_in, l_out, l_small, sc_arr):
    nd = c_small.shape

    def body(s_ref, gi_ref, go_ref, pi_ref, po_ref, cs_ref, li0, li1, li2, lo0, lo1, lo2, ls_ref,
             ri_ref, ro_ref, rs_ref):
        ri_ref[...] = (((gi_ref[0] + pi_ref[0]) + li0[0].astype(F32)) + li1[0].astype(F32)) + li2[0].astype(F32)
        ro_ref[...] = (((go_ref[0] + po_ref[0]) + lo0[0].astype(F32)) + lo1[0].astype(F32)) + lo2[0].astype(F32)
        me = s_ref[0]
        parts = (cs_ref[...], ls_ref[0], ls_ref[1], ls_ref[2])

        def of_chip(s):
            m = jnp.bitwise_xor(me, s)
            return jnp.where(m == 0, parts[0], jnp.where(m == 1, parts[1], jnp.where(m == 2, parts[2], parts[3])))

        rs_ref[...] = ((of_chip(0) + of_chip(1)) + of_chip(2)) + of_chip(3)

    def flip_in(j):
        return pl.BlockSpec((1, HALF_IN, D_MODEL), lambda i, s: (j, 0, 0))

    def flip_out(j):
        return pl.BlockSpec((1, HALF_OUT, D_MODEL), lambda i, s: (j, 0, 0))

    return pl.pallas_call(
        body, name="chip_sum",
        grid_spec=pltpu.PrefetchScalarGridSpec(
            num_scalar_prefetch=1, grid=(1,),
            in_specs=[pl.BlockSpec((1, HALF_IN, D_MODEL), lambda i, s: (s[0], s[1], 0)),
                      pl.BlockSpec((1, HALF_OUT, D_MODEL), lambda i, s: (s[0], s[1], 0)),
                      pl.BlockSpec((1, HALF_IN, D_MODEL), lambda i, s: (s[0], 0, 0)),
                      pl.BlockSpec((1, HALF_OUT, D_MODEL), lambda i, s: (s[0], 0, 0)),
                      pl.BlockSpec(nd, lambda i, s: (0, 0)),
                      flip_in(0), flip_in(1), flip_in(2), flip_out(0), flip_out(1), flip_out(2),
                      pl.BlockSpec((N_FLIPS,) + nd, lambda i, s: (0, 0, 0))],
            out_specs=[pl.BlockSpec((HALF_IN, D_MODEL), lambda i, s: (0, 0)),
                       pl.BlockSpec((HALF_OUT, D_MODEL), lambda i, s: (0, 0)),
                       pl.BlockSpec(nd, lambda i, s: (0, 0))]),
        out_shape=[jax.ShapeDtypeStruct((HALF_IN, D_MODEL), F32), jax.ShapeDtypeStruct((HALF_OUT, D_MODEL), F32),
                   jax.ShapeDtypeStruct(nd, F32)],
        compiler_params=_cparams(dimension_semantics=("arbitrary",)),
    )(sc_arr, g_in, g_out, p_in, p_out, c_small, l_in, l_in, l_in, l_out, l_out, l_out, l_small)


def _pair_share(r_in, r_out):
    def body(ri_ref, ro_ref, li_ref, lo_ref, send, recv):
        x, y, c = _position()
        remote = [pltpu.make_async_remote_copy(src_ref=src, dst_ref=dst, send_sem=send.at[k], recv_sem=recv.at[k],
                                               device_id=(x, y, 1 - c), device_id_type=MESH)
                  for k, (src, dst) in enumerate(((ri_ref, li_ref), (ro_ref, lo_ref)))]
        for cp in remote:
            cp.start()
        _finish([], remote, remote)

    return pl.pallas_call(
        body, name="pair_share",
        in_specs=[ANY] * 2, out_specs=[ANY] * 2,
        out_shape=[jax.ShapeDtypeStruct(r_in.shape, F32), jax.ShapeDtypeStruct(r_out.shape, F32)],
        scratch_shapes=[pltpu.SemaphoreType.DMA((2,)), pltpu.SemaphoreType.DMA((2,))],
    )(r_in, r_out)


def _adam_update(w, g, m, v):
    mn = ADAM_B1 * m + (1.0 - ADAM_B1) * g
    vn = ADAM_B2 * v + (1.0 - ADAM_B2) * jnp.square(g)
    m_hat = mn / (1.0 - ADAM_B1 ** ADAM_STEP)
    v_hat = vn / (1.0 - ADAM_B2 ** ADAM_STEP)
    return -ADAM_LR * (m_hat / (jnp.sqrt(v_hat) + ADAM_EPS) + ADAM_WD * w), mn, vn


def _adamw(name, w, g_mine, g_theirs, m, v, core_arr, tr):
    rows, cols = w.shape
    per_half = rows // 2 // tr

    def body(c_ref, w_ref, gm_ref, gt_ref, m_ref, v_ref, g_ref, d_ref, nm_ref, nv_ref):
        mine = (pl.program_id(0) // per_half) == c_ref[0]
        g = jnp.where(mine, gm_ref[...], gt_ref[...])
        d, mn, vn = _adam_update(w_ref[...], g, m_ref[...], v_ref[...])
        g_ref[...] = g
        d_ref[...] = d
        nm_ref[...] = mn
        nv_ref[...] = vn

    spec = pl.BlockSpec((tr, cols), lambda i, c: (i, 0))
    half = pl.BlockSpec((tr, cols), lambda i, c: (i % per_half, 0))
    return pl.pallas_call(
        body, name=name,
        grid_spec=pltpu.PrefetchScalarGridSpec(
            num_scalar_prefetch=1, grid=(rows // tr,),
            in_specs=[spec, half, half, spec, spec], out_specs=[spec] * 4),
        out_shape=[jax.ShapeDtypeStruct((rows, cols), F32)] * 4,
        compiler_params=_cparams(dimension_semantics=("arbitrary",)),
    )(core_arr, w, g_mine, g_theirs, m, v)


def _row_pieces(n):
    return [(k, k * PACK_W, min(PACK_W, n - k * PACK_W)) for k in range(-(-n // PACK_W))]


def _pack_small(d_small, loss, d_lora):
    ns = len(SMALL_NAMES)

    def body(*refs):
        small_refs, (loss_ref, lora_ref, out_ref) = refs[:ns], refs[ns:]
        out_ref[...] = jnp.zeros_like(out_ref)
        out_ref[PACK_LORA_W:PACK_LORA_W + LORA, :] = lora_ref[:LORA, :RW_WIDTH]
        out_ref[PACK_LORA_A:PACK_LORA_A + LORA, :] = lora_ref[LORA:, RW_WIDTH:]
        for name, n, ref in zip(SMALL_NAMES, SMALL_SIZES, small_refs):
            for k, at, w in _row_pieces(n):
                out_ref[PACK_AT[name] + k:PACK_AT[name] + k + 1, 0:w] = ref[:, at:at + w]
        out_ref[PACK_LOSS:PACK_LOSS + 1, :] = loss_ref[...]

    return pl.pallas_call(body, name="pack_small", out_shape=jax.ShapeDtypeStruct((PACK_ROWS, PACK_W), F32),
                          compiler_params=_cparams())(*d_small, loss, d_lora)


def _adamw_small(tot, chip_arr, ws, ms, vs):
    ns = len(SMALL_NAMES)
    n_par = ns + 2

    def body(s_ref, tot_ref, glw_ref, gla_ref, *refs):
        w_refs, m_refs, v_refs = refs[:n_par], refs[n_par:2 * n_par], refs[2 * n_par:3 * n_par]
        outs = refs[3 * n_par:]
        g_refs, d_refs, nm_refs, nv_refs = (outs[i * n_par:(i + 1) * n_par] for i in range(4))
        grads = [jnp.concatenate([tot_ref[PACK_AT[name] + k:PACK_AT[name] + k + 1, 0:w] for k, _, w in _row_pieces(n)],
                                 axis=1) for name, n in zip(SMALL_NAMES, SMALL_SIZES)]
        grads += [glw_ref[...], gla_ref[...]]
        for i, g in enumerate(grads):
            d, mn, vn = _adam_update(w_refs[i][...], g, m_refs[i][...], v_refs[i][...])
            g_refs[i][...] = g
            d_refs[i][...] = d
            nm_refs[i][...] = mn
            nv_refs[i][...] = vn

    def whole(a):
        nd = a.ndim
        return pl.BlockSpec(a.shape, lambda i, s: (0,) * nd)

    shard = (LORA, LORA_SHARD)
    par_specs = [whole(a) for a in ws]
    res = pl.pallas_call(
        body, name="adamw_small",
        grid_spec=pltpu.PrefetchScalarGridSpec(
            num_scalar_prefetch=1, grid=(1,),
            in_specs=[whole(tot), pl.BlockSpec(shard, lambda i, s: (PACK_LORA_W // LORA, s[0])),
                      pl.BlockSpec(shard, lambda i, s: (PACK_LORA_A // LORA, s[0]))] + par_specs * 3,
            out_specs=par_specs * 4),
        out_shape=[jax.ShapeDtypeStruct(a.shape, F32) for a in ws] * 4,
        compiler_params=_cparams(dimension_semantics=("arbitrary",)),
    )(chip_arr, tot, tot, tot, *ws, *ms, *vs)
    return [res[i * n_par:(i + 1) * n_par] for i in range(4)]


def kernel(x, norm_g, w_in, ret_gn_g, rwkv_mu, w_lora_up, w0, a_lora_up, a0, k_k, k_a, r_k, rwkv_gn_g, rwkv_gn_b, w_out, final_norm_g, loss_target, m_norm_g, m_w_in, m_ret_gn_g, m_rwkv_mu, m_w_lora_up, m_w0, m_a_lora_up, m_a0, m_k_k, m_k_a, m_r_k, m_rwkv_gn_g, m_rwkv_gn_b, m_w_out, m_final_norm_g, v_norm_g, v_w_in, v_ret_gn_g, v_rwkv_mu, v_w_lora_up, v_w0, v_a_lora_up, v_a0, v_k_k, v_k_a, v_r_k, v_rwkv_gn_g, v_rwkv_gn_b, v_w_out, v_final_norm_g):
    W = RW_WIDTH
    params = dict(norm_g=norm_g, ret_gn_g=ret_gn_g, rwkv_mu=rwkv_mu, w0=w0, a0=a0, k_k=k_k, k_a=k_a, r_k=r_k,
                  rwkv_gn_g=rwkv_gn_g, rwkv_gn_b=rwkv_gn_b, final_norm_g=final_norm_g)
    moments_m = dict(norm_g=m_norm_g, ret_gn_g=m_ret_gn_g, rwkv_mu=m_rwkv_mu, w0=m_w0, a0=m_a0, k_k=m_k_k, k_a=m_k_a,
                     r_k=m_r_k, rwkv_gn_g=m_rwkv_gn_g, rwkv_gn_b=m_rwkv_gn_b, final_norm_g=m_final_norm_g)
    moments_v = dict(norm_g=v_norm_g, ret_gn_g=v_ret_gn_g, rwkv_mu=v_rwkv_mu, w0=v_w0, a0=v_a0, k_k=v_k_k, k_a=v_k_a,
                     r_k=v_r_k, rwkv_gn_g=v_rwkv_gn_g, rwkv_gn_b=v_rwkv_gn_b, final_norm_g=v_final_norm_g)
    xi, yi, ci = _position()
    chip = (2 * xi + yi).astype(jnp.int32)

    def halves(a):
        return a.reshape(2, a.shape[0] // 2, a.shape[1])

    w_t, m_t, v_t = (jnp.transpose(a[0]) for a in (w_in, m_w_in, v_w_in))
    own_in = halves(w_t.astype(BF16))
    w_in_t = _with_own(_gather_chips([own_in])[0], own_in, chip).reshape(IN_COLS, D_MODEL)
    late = [halves(w_out[0].astype(BF16)), halves(w_lora_up[0]), halves(a_lora_up[0])]
    small = {n: params[n].reshape(1, -1) for n in SMALL_NAMES}

    loss, grad_x, d_w_in, d_w_out, d_lora, d_small = _local_step(x[0], loss_target[0], w_in_t, late, chip, small)

    core = ci.astype(jnp.int32)
    gi = d_w_in.reshape(N_CHIPS, IN_SHARD, D_MODEL)
    go = d_w_out.reshape(N_CHIPS, OUT_SHARD, D_MODEL)
    gs = _pack_small([d_small[n] for n in SMALL_NAMES], loss, d_lora)
    p_in, p_out, p_small = _pair_exchange(gi, go, gs)
    c_in, c_out, c_small = _pair_sum(gi, go, gs, p_in, p_out, p_small, core.reshape(1))
    l_in, l_out, l_small = _chip_exchange(c_in, c_out, c_small)
    r_in, r_out, tot = _chip_sum(gi, go, p_in, p_out, c_small, l_in, l_out, l_small, jnp.stack([chip, core]))
    t_in, t_out = _pair_share(r_in, r_out)

    grad_w_in, d_in, nm_in, nv_in = (jnp.transpose(a) for a in _adamw(
        "adamw_w_in", w_t, r_in, t_in, m_t, v_t, core.reshape(1), HALF_IN // 2))
    grad_w_out, d_out, nm_out, nv_out = _adamw("adamw_w_out", w_out[0], r_out, t_out, m_w_out[0], v_w_out[0],
                                               core.reshape(1), HALF_OUT)
    par_names = SMALL_NAMES + ("w_lora_up", "a_lora_up")

    def operands(tree, lw_, la_):
        return [tree[n].reshape(1, -1) for n in SMALL_NAMES] + [lw_[0], la_[0]]

    res = _adamw_small(tot, chip.reshape(1), operands(params, w_lora_up, a_lora_up),
                       operands(moments_m, m_w_lora_up, m_a_lora_up), operands(moments_v, v_w_lora_up, v_a_lora_up))

    names = ("norm_g", "w_in", "ret_gn_g", "rwkv_mu", "w_lora_up", "w0", "a_lora_up", "a0", "k_k", "k_a", "r_k",
             "rwkv_gn_g", "rwkv_gn_b", "w_out", "final_norm_g")
    shapes = dict(w_in=w_in.shape, w_out=w_out.shape, w_lora_up=w_lora_up.shape, a_lora_up=a_lora_up.shape,
                  **{n: params[n].shape for n in SMALL_NAMES})

    def leaves(pars, big_in, big_out):
        tree = dict(zip(par_names, pars), w_in=big_in, w_out=big_out)
        return [tree[n].reshape(shapes[n]) for n in names]

    grads = leaves(res[0], grad_w_in, grad_w_out)
    deltas = leaves(res[1], d_in, d_out)
    new_m = leaves(res[2], nm_in, nm_out)
    new_v = leaves(res[3], nv_in, nv_out)
    return (tot[PACK_LOSS, 0], grad_x.reshape(x.shape), *grads, *deltas, *new_m, *new_v)
```

```python
import functools

import numpy as np
import jax
import jax.numpy as jnp
from jax import lax
from jax.experimental import pallas as pl
from jax.experimental.pallas import tpu as pltpu

F32 = jnp.float32
BF16 = jnp.bfloat16
X3 = "bf16x3"
B1 = "bf16"
MESH = pl.DeviceIdType.MESH

D_MODEL = 1024
N_CHIPS = 4
RET_HEADS = 4
RET_DK = 64
RET_DV = 128
RET_QK = RET_HEADS * RET_DK
RET_WIDTH = RET_HEADS * RET_DV
RET_COLS = 2 * RET_QK + 2 * RET_WIDTH
RET_CHUNK = 64
RET_GROUP = 8
RW_WIDTH = 512
RW_HEAD = 64
RW_HEADS = 8
LORA = 64
RW_COLS = 4 * RW_WIDTH + 2 * LORA
IN_COLS = RET_COLS + RW_COLS
IN_SHARD = IN_COLS // N_CHIPS
OUT_SHARD = D_MODEL // N_CHIPS
ROPE_BASE = 10000.0
RMS_EPS = 1e-6
RET_GN_EPS = 1e-5
RW_GN_EPS = 64e-5
WKV_CHUNK = 16
WKV_GROUP = 8
N_VEC = 5

ADAM_LR = 0.001
ADAM_B1 = 0.9
ADAM_B2 = 0.999
ADAM_EPS = 1e-08
ADAM_WD = 0.01
ADAM_STEP = 10

VMEM_LIMIT = 56 * 1024 * 1024

PACK_W = 512
SMALL_NAMES = ("norm_g", "ret_gn_g", "rwkv_mu", "w0", "a0", "k_k", "k_a", "r_k", "rwkv_gn_g", "rwkv_gn_b",
               "final_norm_g")
SMALL_SIZES = (1024, 512, 2176, 512, 512, 512, 512, 512, 512, 512, 1024)
PACK_LORA_W = 0
PACK_LORA_A = LORA
PACK_SMALL = 2 * LORA


def _pack_layout():
    rows, at = {}, PACK_SMALL
    for name, n in zip(SMALL_NAMES, SMALL_SIZES):
        rows[name] = at
        at += -(-n // PACK_W)
    return rows, at


PACK_AT, PACK_LOSS = _pack_layout()
PACK_ROWS = -(-(PACK_LOSS + 1) // 8) * 8


def _cparams(**kw):
    return pltpu.CompilerParams(vmem_limit_bytes=VMEM_LIMIT, **kw)


def _split(x):
    hi = x.astype(BF16)
    lo = (x - hi.astype(F32)).astype(BF16)
    return hi, lo


def _dot_dims(a, b, dims, precision):
    if precision == B1:
        a, b = a.astype(BF16), b.astype(BF16)
    if precision != X3:
        return lax.dot_general(a, b, dims, preferred_element_type=F32)
    (ah, al), (bh, bl) = _split(a), _split(b)
    dot = lambda u, w: lax.dot_general(u, w, dims, preferred_element_type=F32)
    return dot(ah, bh) + dot(ah, bl) + dot(al, bh)


def _dot(a, b, precision=None):
    return _dot_dims(a, b, (((1,), (0,)), ((), ())), precision)


def _dot_nt(a, b, precision=None):
    return _dot_dims(a, b, (((1,), (1,)), ((), ())), precision)


def _dot_tn(a, b, precision=None):
    return _dot_dims(a, b, (((0,), (0,)), ((), ())), precision)


@jax.custom_vjp
def _segsum(x, seg):
    hi, lo = _split(x)
    return _dot(hi, seg) + _dot(lo, seg)


def _segsum_fwd(x, seg):
    return _segsum(x, seg), seg


def _segsum_bwd(seg, ct):
    return _segsum(ct, seg), jnp.zeros_like(seg)


_segsum.defvjp(_segsum_fwd, _segsum_bwd)


def _softplus(z):
    return jnp.maximum(z, 0.0) + jnp.log(1.0 + jnp.exp(-jnp.abs(z)))


def _full(shape):
    nd = len(shape)
    return pl.BlockSpec(shape, lambda *_: (0,) * nd)


def _rope_tables(T):
    half = RET_DK // 2
    expo = -jnp.arange(half, dtype=F32) / jnp.float32(half)
    freqs = jnp.exp(expo * jnp.float32(np.log(ROPE_BASE)))
    ang = jnp.arange(T, dtype=jnp.int32).astype(F32)[:, None] * freqs[None, :]
    cos = jnp.tile(jnp.cos(ang), (1, 2 * RET_HEADS))
    sin = jnp.tile(jnp.sin(ang), (1, 2 * RET_HEADS))
    return cos, sin


def _ret_tables():
    H, C = RET_HEADS, RET_CHUNK
    hidx = jnp.arange(H, dtype=F32)
    lg = jnp.log(1.0 - jnp.exp2(-5.0 - hidx))
    idx = jnp.arange(C, dtype=F32)
    intra = jnp.exp(lg[:, None, None] * jnp.abs(idx[:, None] - idx[None, :]))
    q_dec = jnp.transpose(jnp.exp(lg[:, None] * (idx[None, :] + 1.0)))
    k_dec = jnp.transpose(jnp.exp(lg[:, None] * (C - 1.0 - idx[None, :])))
    chunk_dec = jnp.exp(lg * C)
    qd = jnp.repeat(q_dec, RET_DK, axis=1)
    kd = jnp.repeat(k_dec, RET_DK, axis=1)
    row_h = np.arange(RET_QK) // RET_DK
    col_h = np.arange(RET_WIDTH) // RET_DV
    bm = jnp.asarray((row_h[:, None] == col_h[None, :]).astype(np.float32))
    cd = bm * jnp.repeat(chunk_dec, RET_DK)[:, None]
    return intra, qd, kd, cd, bm


def _seg_matrix(width, head):
    h = np.arange(width) // head
    return jnp.asarray((h[:, None] == h[None, :]).astype(np.float32), dtype=BF16)


def _wkv_expand_table():
    Tc = WKV_CHUNK
    k = np.arange(2 * RW_HEADS * Tc)
    kh, kt = (k % (RW_HEADS * Tc)) // Tc, k % Tc
    nh = np.arange(RW_WIDTH) // RW_HEAD
    e = (kh[None, :, None] == nh[None, None, :]) & (kt[None, :, None] == np.arange(Tc)[:, None, None])
    return jnp.asarray(e.astype(np.float32), dtype=BF16)


def _wkv_reduce_table():
    Tc = WKV_CHUNK
    kh = np.arange(RW_WIDTH) // RW_HEAD
    n = np.arange(RW_HEADS * Tc)
    nh, nt = n // Tc, n % Tc
    r = (kh[None, :, None] == nh[None, None, :]) & (nt[None, None, :] == np.arange(Tc)[:, None, None])
    return jnp.asarray(r.astype(np.float32), dtype=BF16)


def _inproj_fwd(x, norm_g, w_t, late):
    T = x.shape[0]
    tm = _row_tile(T)
    nt = T // tm
    n = len(late)

    def body(x_ref, g_ref, w_ref, *refs):
        late_in, (pret_ref, prw_ref, u_ref, last_ref) = refs[:n], refs[n:n + 4]
        late_out, sems = refs[n + 4:2 * n + 4], refs[2 * n + 4:]

        @pl.when(pl.program_id(0) == 0)
        def _():
            _gather_start(_gather_copies(late_in, late_out, sems))

        xf = x_ref[...]
        rstd = lax.rsqrt(jnp.mean(xf * xf, axis=-1, keepdims=True) + RMS_EPS)
        ub = ((xf * rstd) * g_ref[...]).astype(BF16)
        u_ref[...] = ub
        pret_ref[...] = _dot_nt(ub, w_ref[:RET_COLS, :])
        p_rw = _dot_nt(ub, w_ref[RET_COLS:, :])
        prw_ref[...] = p_rw
        last_ref[0] = p_rw[tm - 1:tm, :]

        @pl.when(pl.program_id(0) == nt - 1)
        def _():
            _gather_finish(_gather_copies(late_in, late_out, sems))

    return pl.pallas_call(
        body, name="inproj_fwd", grid=(nt,),
        in_specs=[pl.BlockSpec((tm, D_MODEL), lambda i: (i, 0)), _full((1, D_MODEL)), _full((IN_COLS, D_MODEL))]
                 + [ANY] * n,
        out_specs=[pl.BlockSpec((tm, RET_COLS), lambda i: (i, 0)), pl.BlockSpec((tm, RW_COLS), lambda i: (i, 0)),
                   pl.BlockSpec((tm, D_MODEL), lambda i: (i, 0)), pl.BlockSpec((1, 1, RW_COLS), lambda i: (i, 0, 0))]
                  + [ANY] * n,
        out_shape=[jax.ShapeDtypeStruct((T, RET_COLS), F32), jax.ShapeDtypeStruct((T, RW_COLS), F32),
                   jax.ShapeDtypeStruct((T, D_MODEL), BF16), jax.ShapeDtypeStruct((nt, 1, RW_COLS), F32)]
                  + _gather_shapes(late),
        scratch_shapes=_gather_sems(n),
        compiler_params=_cparams(dimension_semantics=("arbitrary",)),
    )(x, norm_g, w_t, *late)


def _rot_half(x):
    n = x.shape[1]
    lane = lax.broadcasted_iota(jnp.int32, x.shape, 1)
    first = (lane % RET_DK) < (RET_DK // 2)
    return jnp.where(first, -pltpu.roll(x, n - RET_DK // 2, 1), pltpu.roll(x, RET_DK // 2, 1))


def _rope(x, cos, sin):
    return x * cos + _rot_half(x) * sin


def _rope_bwd(d, cos, sin):
    return d * cos - _rot_half(d * sin)


def _ret_post(ret, g, gn_g):
    heads = []
    for h in range(RET_HEADS):
        xh = ret[:, h * RET_DV:(h + 1) * RET_DV]
        xc = xh - jnp.mean(xh, axis=-1, keepdims=True)
        heads.append(xc * lax.rsqrt(jnp.mean(xc * xc, axis=-1, keepdims=True) + RET_GN_EPS))
    return (g * jax.nn.sigmoid(g)) * (jnp.concatenate(heads, axis=1) * gn_g)


def _ret_scores(qt, kt, d_ref, h):
    lane = lax.broadcasted_iota(jnp.int32, qt.shape, 1)
    qh = jnp.where(lane // RET_DK == h, qt, 0.0)
    return qh, _dot_nt(qh, kt, B1) * d_ref[h]


def _ret_group(nch):
    return min(RET_GROUP, nch)


def _ret_fwd(p_ret, cos, sin, tabs, gn_g):
    T = p_ret.shape[0]
    C = RET_CHUNK
    nch = T // C
    G = _ret_group(nch)
    intra_d, qd, kd, cd, bm = tabs

    def body(q_ref, k_ref, v_ref, g_ref, cos_ref, sin_ref, qd_ref, kd_ref, d_ref, cd_ref, bm_ref, gn_ref,
             y_ref, ret_ref, sin_out_ref, s_ref, qt_buf, kv_buf):
        @pl.when(pl.program_id(0) == 0)
        def _():
            s_ref[...] = jnp.zeros_like(s_ref)

        cosv, sinv = cos_ref[...], sin_ref[...]
        qt_all = _rope(q_ref[...], cosv, sinv)
        kt_all = _rope(k_ref[...], cosv, sinv) * (RET_DK ** -0.5)
        for i in range(G):
            rows = slice(i * C, (i + 1) * C)
            qt, kt, v = qt_all[rows], kt_all[rows], v_ref[rows, :]
            intra = []
            for h in range(RET_HEADS):
                _, a = _ret_scores(qt, kt, d_ref, h)
                intra.append(_dot(a, v[:, h * RET_DV:(h + 1) * RET_DV], B1))
            ret_ref[rows, :] = jnp.concatenate(intra, axis=1)
            qt_buf[rows, :] = qt * qd_ref[...]
            kv_buf[i] = _dot_tn(kt * kd_ref[...], v, B1) * bm_ref[...]
        s_in = s_ref[...]
        for i in range(G):
            rows = slice(i * C, (i + 1) * C)
            sin_out_ref[i] = s_in
            ret_ref[rows, :] += _dot(qt_buf[rows, :], s_in, B1)
            s_in = s_in * cd_ref[...] + kv_buf[i]
        s_ref[...] = s_in
        y_ref[...] = _ret_post(ret_ref[...], g_ref[...], gn_ref[...]).astype(BF16)

    GC = G * C
    return pl.pallas_call(
        body, name="ret_fwd", grid=(nch // G,),
        in_specs=[pl.BlockSpec((GC, RET_QK), lambda c: (c, 0)), pl.BlockSpec((GC, RET_QK), lambda c: (c, 1)),
                  pl.BlockSpec((GC, RET_WIDTH), lambda c: (c, 1)), pl.BlockSpec((GC, RET_WIDTH), lambda c: (c, 2)),
                  pl.BlockSpec((GC, RET_QK), lambda c: (c, 0)), pl.BlockSpec((GC, RET_QK), lambda c: (c, 0)),
                  _full((C, RET_QK)), _full((C, RET_QK)), _full((RET_HEADS, C, C)),
                  _full((RET_QK, RET_WIDTH)), _full((RET_QK, RET_WIDTH)), _full((1, RET_WIDTH))],
        out_specs=[pl.BlockSpec((GC, RET_WIDTH), lambda c: (c, 0)), pl.BlockSpec((GC, RET_WIDTH), lambda c: (c, 0)),
                   pl.BlockSpec((G, RET_QK, RET_WIDTH), lambda c: (c, 0, 0))],
        out_shape=[jax.ShapeDtypeStruct((T, RET_WIDTH), BF16), jax.ShapeDtypeStruct((T, RET_WIDTH), F32),
                   jax.ShapeDtypeStruct((nch, RET_QK, RET_WIDTH), F32)],
        scratch_shapes=[pltpu.VMEM((RET_QK, RET_WIDTH), F32), pltpu.VMEM((GC, RET_QK), F32),
                        pltpu.VMEM((G, RET_QK, RET_WIDTH), F32)],
        compiler_params=_cparams(dimension_semantics=("arbitrary",)),
    )(p_ret, p_ret, p_ret, p_ret, cos, sin, qd, kd, intra_d, cd, bm, gn_g)


def _ret_bwd(p_ret, cos, sin, tabs, gn_g, ret, s_in_all, dy):
    T = p_ret.shape[0]
    C = RET_CHUNK
    nch = T // C
    G = _ret_group(nch)
    ngr = nch // G
    intra_d, qd, kd, cd, bm = tabs

    def rev(j):
        return lambda c: (ngr - 1 - c, j)

    def body(q_ref, k_ref, v_ref, g_ref, cos_ref, sin_ref, qd_ref, kd_ref, d_ref, cd_ref, bm_ref, gn_ref,
             ret_ref, sin_ref_, dy_ref, dp_ref, dgn_ref, ds_ref, dkt_buf, ktk_buf, g_buf):
        @pl.when(pl.program_id(0) == 0)
        def _():
            ds_ref[...] = jnp.zeros_like(ds_ref)
            dgn_ref[...] = jnp.zeros_like(dgn_ref)

        _, post_vjp = jax.vjp(_ret_post, ret_ref[...], g_ref[...], gn_ref[...])
        dret_all, dg_all, dgn = post_vjp(dy_ref[...])
        dgn_ref[...] += dgn
        v_cols = slice(2 * RET_QK, 2 * RET_QK + RET_WIDTH)
        dp_ref[:, 2 * RET_QK + RET_WIDTH:] = dg_all

        qdv, kdv = qd_ref[...], kd_ref[...]
        cosv, sinv = cos_ref[...], sin_ref[...]
        qt_all = _rope(q_ref[...], cosv, sinv)
        kt_all = _rope(k_ref[...], cosv, sinv) * (RET_DK ** -0.5)
        for i in range(G):
            rows = slice(i * C, (i + 1) * C)
            qt, kt, v, dret = qt_all[rows], kt_all[rows], v_ref[rows, :], dret_all[rows, :]
            dqt = qdv * _dot_nt(dret, sin_ref_[i], B1)
            dkt = jnp.zeros_like(kt)
            dvs = []
            for h in range(RET_HEADS):
                sl = slice(h * RET_DV, (h + 1) * RET_DV)
                qh, a = _ret_scores(qt, kt, d_ref, h)
                lane = lax.broadcasted_iota(jnp.int32, kt.shape, 1)
                kh = jnp.where(lane // RET_DK == h, kt, 0.0)
                da = _dot_nt(dret[:, sl], v[:, sl], B1) * d_ref[h]
                dvs.append(_dot_tn(a, dret[:, sl], B1))
                dqt = dqt + _dot(da, kh, B1)
                dkt = dkt + _dot_tn(da, qh, B1)
            dp_ref[rows, :RET_QK] = _rope_bwd(dqt, cosv[rows], sinv[rows])
            dp_ref[rows, v_cols] = jnp.concatenate(dvs, axis=1)
            dkt_buf[rows, :] = dkt
            ktk_buf[rows, :] = kt * kdv
            g_buf[i] = _dot_tn(qt * qdv, dret, B1) * bm_ref[...]
        ds_out = ds_ref[...]
        for i in reversed(range(G)):
            rows = slice(i * C, (i + 1) * C)
            dkt = dkt_buf[rows, :] + kdv * _dot_nt(v_ref[rows, :], ds_out, B1)
            dp_ref[rows, RET_QK:2 * RET_QK] = _rope_bwd(dkt * (RET_DK ** -0.5), cosv[rows], sinv[rows])
            dp_ref[rows, v_cols] += _dot(ktk_buf[rows, :], ds_out, B1)
            ds_out = ds_out * cd_ref[...] + g_buf[i]
        ds_ref[...] = ds_out

    GC = G * C
    return pl.pallas_call(
        body, name="ret_bwd", grid=(ngr,),
        in_specs=[pl.BlockSpec((GC, RET_QK), rev(0)), pl.BlockSpec((GC, RET_QK), rev(1)),
                  pl.BlockSpec((GC, RET_WIDTH), rev(1)), pl.BlockSpec((GC, RET_WIDTH), rev(2)),
                  pl.BlockSpec((GC, RET_QK), rev(0)), pl.BlockSpec((GC, RET_QK), rev(0)),
                  _full((C, RET_QK)), _full((C, RET_QK)), _full((RET_HEADS, C, C)),
                  _full((RET_QK, RET_WIDTH)), _full((RET_QK, RET_WIDTH)), _full((1, RET_WIDTH)),
                  pl.BlockSpec((GC, RET_WIDTH), rev(0)),
                  pl.BlockSpec((G, RET_QK, RET_WIDTH), lambda c: (ngr - 1 - c, 0, 0)),
                  pl.BlockSpec((GC, RET_WIDTH), rev(0))],
        out_specs=[pl.BlockSpec((GC, RET_COLS), rev(0)), _full((1, RET_WIDTH))],
        out_shape=[jax.ShapeDtypeStruct((T, RET_COLS), F32), jax.ShapeDtypeStruct((1, RET_WIDTH), F32)],
        scratch_shapes=[pltpu.VMEM((RET_QK, RET_WIDTH), F32), pltpu.VMEM((GC, RET_QK), F32),
                        pltpu.VMEM((GC, RET_QK), F32), pltpu.VMEM((G, RET_QK, RET_WIDTH), F32)],
        compiler_params=_cparams(dimension_semantics=("arbitrary",)),
    )(p_ret, p_ret, p_ret, p_ret, cos, sin, qd, kd, intra_d, cd, bm, gn_g, ret, s_in_all, dy)


@jax.custom_vjp
def _chunk_sums(x, tri):
    hi, lo = _split(x)
    return _dot(tri, hi) + _dot(tri, lo)


def _chunk_sums_fwd(x, tri):
    return _chunk_sums(x, tri), tri


def _chunk_sums_bwd(tri, ct):
    hi, lo = _split(ct)
    return _dot_tn(tri, hi) + _dot_tn(tri, lo), jnp.zeros_like(tri)


_chunk_sums.defvjp(_chunk_sums_fwd, _chunk_sums_bwd)


@jax.custom_vjp
def _lora_dot(z, lora):
    return _dot(z, lora, X3)


def _lora_dot_fwd(z, lora):
    return _lora_dot(z, lora), (z, lora)


def _lora_dot_bwd(res, ct):
    z, lora = res
    return _dot_nt(ct, lora, X3), _dot_tn(z, ct, X3)


_lora_dot.defvjp(_lora_dot_fwd, _lora_dot_bwd)


def _chunk_tables(tm):
    t = np.arange(tm)
    same = (t[:, None] // WKV_CHUNK) == (t[None, :] // WKV_CHUNK)
    return jnp.asarray(np.stack([same & (t[None, :] <= t[:, None]), same]).astype(np.float32), dtype=BF16)


def _prep_fn(p, prev, mu, w0, a0, k_k, k_a, lora, seg, tri):
    W = RW_WIDTH
    ps = p + mu * (prev - p)
    r, kr, vr, g = ps[:, 0:W], ps[:, W:2 * W], ps[:, 2 * W:3 * W], ps[:, 3 * W:4 * W]
    z = ps[:, 4 * W:]
    lane = lax.broadcasted_iota(jnp.int32, z.shape, 1)
    z = jnp.where(lane < LORA, jnp.tanh(z), z)
    lo = _lora_dot(z, lora)
    w_log = -_softplus(-(w0 + lo[:, :W])) - 0.5
    log_decay = -jnp.exp(w_log)
    cum = _chunk_sums(log_decay, tri[0])
    total = _chunk_sums(log_decay, tri[1])
    a = jax.nn.sigmoid(a0 + lo[:, W:])
    kk = kr * k_k
    kk = kk / jnp.maximum(jnp.sqrt(_segsum(kk * kk, seg)), 1e-12)
    k = kr * (1.0 + (a - 1.0) * k_a)
    grow = jnp.exp(-cum)
    return kk * jnp.exp(cum - log_decay), (kk * a) * grow, k * grow, r * jnp.exp(cum), jnp.exp(total), vr, g, r * k


def _post_fn(o, rk, v, g, gn_g, gn_b, r_k, seg):
    mu = _segsum(o, seg) * (1.0 / RW_HEAD)
    oc = o - mu
    var = _segsum(oc * oc, seg) * (1.0 / RW_HEAD)
    on = oc * lax.rsqrt(var + RW_GN_EPS) * gn_g + gn_b
    bonus = _segsum(rk * r_k, seg) * v
    return (g * jax.nn.sigmoid(g)) * (on + bonus)


N_PAIR = (N_VEC + 1) // 2
HALF_LANES = 64


def _swap_halves(x):
    return pltpu.roll(x, HALF_LANES, 1)


def _pack_heads(vecs):
    tm = vecs[0].shape[0]
    low = lax.broadcasted_iota(jnp.int32, (tm, 128), 1) < HALF_LANES
    out = []
    for p in range(N_PAIR):
        a = vecs[2 * p]
        b = vecs[2 * p + 1] if 2 * p + 1 < len(vecs) else None
        heads = []
        for m in range(RW_WIDTH // 128):
            am = a[:, m * 128:(m + 1) * 128]
            bm = jnp.zeros_like(am) if b is None else b[:, m * 128:(m + 1) * 128]
            heads.append(jnp.where(low, am, _swap_halves(bm)))
            heads.append(jnp.where(low, _swap_halves(am), bm))
        out.append(heads)
    return out


def _unpack_heads(hm_ref):
    tm = hm_ref.shape[2]
    low = lax.broadcasted_iota(jnp.int32, (tm, 128), 1) < HALF_LANES
    vecs = []
    for p in range(N_PAIR):
        a, b = [], []
        for m in range(RW_WIDTH // 128):
            even, odd = hm_ref[p, 2 * m], hm_ref[p, 2 * m + 1]
            a.append(jnp.where(low, even, _swap_halves(odd)))
            b.append(jnp.where(low, _swap_halves(even), odd))
        vecs += [jnp.concatenate(a, axis=1), jnp.concatenate(b, axis=1)]
    return vecs[:N_VEC]


def _shift_down(p, first_row):
    row = lax.broadcasted_iota(jnp.int32, p.shape, 0)
    return jnp.where(row == 0, first_row, pltpu.roll(p, 1, 0))


def _shift_up(p, last_row):
    n = p.shape[0]
    row = lax.broadcasted_iota(jnp.int32, p.shape, 0)
    return jnp.where(row == n - 1, last_row, pltpu.roll(p, n - 1, 0))


def _row_tile(T):
    return min(T, 256)


def _prep_fwd(p_rw, bnd, mu, w0, a0, k_k, k_a, lora, seg64, tri):
    T = p_rw.shape[0]
    tm = _row_tile(T)
    W = RW_WIDTH

    def body(p_ref, bnd_ref, mu_ref, w0_ref, a0_ref, kk_ref, ka_ref, lora_ref, seg_ref, tri_ref,
             hm_ref, v_ref, g_ref, rk_ref):
        p = p_ref[...]
        prev = _shift_down(p, jnp.where(pl.program_id(0) == 0, 0.0, bnd_ref[0]))
        res = _prep_fn(p, prev, mu_ref[...], w0_ref[...], a0_ref[...], kk_ref[...], ka_ref[...], lora_ref[...],
                       seg_ref[...], (tri_ref[0], tri_ref[1]))
        for pair, heads in enumerate(_pack_heads(res[:N_VEC])):
            for h, val in enumerate(heads):
                hm_ref[pair, h] = val
        v_ref[...] = res[N_VEC]
        g_ref[...] = res[N_VEC + 1]
        rk_ref[...] = res[N_VEC + 2]

    small = _full((1, W))
    row = pl.BlockSpec((tm, W), lambda i: (i, 0))
    return pl.pallas_call(
        body, name="rwkv_prep_fwd", grid=(T // tm,),
        in_specs=[pl.BlockSpec((tm, RW_COLS), lambda i: (i, 0)),
                  pl.BlockSpec((1, 1, RW_COLS), lambda i: (jnp.maximum(i - 1, 0), 0, 0)),
                  _full((1, RW_COLS)), small, small, small, small, _full((2 * LORA, 2 * W)), _full((W, W)),
                  _full((2, tm, tm))],
        out_specs=[pl.BlockSpec((N_PAIR, RW_HEADS, tm, 128), lambda i: (0, 0, i, 0)), row, row, row],
        out_shape=[jax.ShapeDtypeStruct((N_PAIR, RW_HEADS, T, 128), F32)] + [jax.ShapeDtypeStruct((T, W), F32)] * 3,
        compiler_params=_cparams(dimension_semantics=("arbitrary",)),
    )(p_rw, bnd, mu, w0, a0, k_k, k_a, lora, seg64, tri)


def _prep_bwd(p_rw, bnd, mu, w0, a0, k_k, k_a, lora, seg64, tri, cts):
    T = p_rw.shape[0]
    tm = _row_tile(T)
    W = RW_WIDTH

    def body(p_ref, bnd_ref, mu_ref, w0_ref, a0_ref, kk_ref, ka_ref, lora_ref, seg_ref, tri_ref,
             dhm_ref, drk_ref, dv1_ref, dv2_ref, dg_ref,
             dp_ref, dprev_ref, dfirst_ref, dmu_ref, dw0_ref, da0_ref, dkk_p_ref, dka_ref, dlora_ref):
        accs = (dmu_ref, dw0_ref, da0_ref, dkk_p_ref, dka_ref, dlora_ref)

        @pl.when(pl.program_id(0) == 0)
        def _():
            for a_ref in accs:
                a_ref[...] = jnp.zeros_like(a_ref)

        p = p_ref[...]
        prev = _shift_down(p, jnp.where(pl.program_id(0) == 0, 0.0, bnd_ref[0]))
        seg, tri = seg_ref[...], (tri_ref[0], tri_ref[1])
        _, vjp = jax.vjp(lambda *a: _prep_fn(*a, seg, tri), p, prev, mu_ref[...], w0_ref[...], a0_ref[...],
                         kk_ref[...], ka_ref[...], lora_ref[...])
        ct = (*_unpack_heads(dhm_ref), dv1_ref[...] + dv2_ref[...], dg_ref[...], drk_ref[...])
        grads = vjp(ct)
        dp_ref[...] = grads[0]
        dprev_ref[...] = grads[1]
        dfirst_ref[0] = grads[1][0:1, :]
        for a_ref, gval in zip(accs, grads[2:]):
            a_ref[...] += gval

    small = _full((1, W))
    row = pl.BlockSpec((tm, W), lambda i: (i, 0))
    return pl.pallas_call(
        body, name="rwkv_prep_bwd", grid=(T // tm,),
        in_specs=[pl.BlockSpec((tm, RW_COLS), lambda i: (i, 0)),
                  pl.BlockSpec((1, 1, RW_COLS), lambda i: (jnp.maximum(i - 1, 0), 0, 0)),
                  _full((1, RW_COLS)), small, small, small, small, _full((2 * LORA, 2 * W)), _full((W, W)),
                  _full((2, tm, tm)), pl.BlockSpec((N_PAIR, RW_HEADS, tm, 128), lambda i: (0, 0, i, 0))] + [row] * 4,
        out_specs=[pl.BlockSpec((tm, RW_COLS), lambda i: (i, 0)), pl.BlockSpec((tm, RW_COLS), lambda i: (i, 0)),
                   pl.BlockSpec((1, 1, RW_COLS), lambda i: (i, 0, 0)),
                   _full((1, RW_COLS)), small, small, small, small, _full((2 * LORA, 2 * W))],
        out_shape=[jax.ShapeDtypeStruct((T, RW_COLS), F32), jax.ShapeDtypeStruct((T, RW_COLS), F32),
                   jax.ShapeDtypeStruct((T // tm, 1, RW_COLS), F32),
                   jax.ShapeDtypeStruct((1, RW_COLS), F32)] + [jax.ShapeDtypeStruct((1, W), F32)] * 4
                  + [jax.ShapeDtypeStruct((2 * LORA, 2 * W), F32)],
        compiler_params=_cparams(dimension_semantics=("arbitrary",)),
    )(p_rw, bnd, mu, w0, a0, k_k, k_a, lora, seg64, tri, *cts)


def _post_fwd(o, rk, v, g, gn_g, gn_b, r_k, seg64):
    T = o.shape[0]
    tm = _row_tile(T)
    W = RW_WIDTH

    def body(o_ref, rk_ref, v_ref, g_ref, gg_ref, gb_ref, rkp_ref, seg_ref, y_ref):
        y_ref[...] = _post_fn(o_ref[...], rk_ref[...], v_ref[...], g_ref[...], gg_ref[...], gb_ref[...],
                              rkp_ref[...], seg_ref[...]).astype(BF16)

    row = pl.BlockSpec((tm, W), lambda i: (i, 0))
    small = _full((1, W))
    return pl.pallas_call(
        body, name="rwkv_post_fwd", grid=(T // tm,),
        in_specs=[row] * 4 + [small] * 3 + [_full((W, W))],
        out_specs=row, out_shape=jax.ShapeDtypeStruct((T, W), BF16),
        compiler_params=_cparams(dimension_semantics=("arbitrary",)),
    )(o, rk, v, g, gn_g, gn_b, r_k, seg64)


def _post_bwd(o, rk, v, g, gn_g, gn_b, r_k, seg64, dy):
    T = o.shape[0]
    tm = _row_tile(T)
    W = RW_WIDTH

    def body(o_ref, rk_ref, v_ref, g_ref, gg_ref, gb_ref, rkp_ref, seg_ref, dy_ref,
             do_ref, drk_ref, dv_ref, dg_ref, dgg_ref, dgb_ref, drkp_ref):
        accs = (dgg_ref, dgb_ref, drkp_ref)

        @pl.when(pl.program_id(0) == 0)
        def _():
            for a_ref in accs:
                a_ref[...] = jnp.zeros_like(a_ref)

        seg = seg_ref[...]
        _, vjp = jax.vjp(lambda *a: _post_fn(*a, seg), o_ref[...], rk_ref[...], v_ref[...], g_ref[...],
                         gg_ref[...], gb_ref[...], rkp_ref[...])
        grads = vjp(dy_ref[...])
        for o_, gval in zip((do_ref, drk_ref, dv_ref, dg_ref), grads[:4]):
            o_[...] = gval
        for a_ref, gval in zip(accs, grads[4:]):
            a_ref[...] += gval

    row = pl.BlockSpec((tm, W), lambda i: (i, 0))
    small = _full((1, W))
    return pl.pallas_call(
        body, name="rwkv_post_bwd", grid=(T // tm,),
        in_specs=[row] * 4 + [small] * 3 + [_full((W, W)), pl.BlockSpec((tm, W), lambda i: (i, 1))],
        out_specs=[row] * 4 + [small] * 3,
        out_shape=[jax.ShapeDtypeStruct((T, W), F32)] * 4 + [jax.ShapeDtypeStruct((1, W), F32)] * 3,
        compiler_params=_cparams(dimension_semantics=("arbitrary",)),
    )(o, rk, v, g, gn_g, gn_b, r_k, seg64, dy)


def _wkv_lhs(hm_ref, rows):
    tiles = [jnp.transpose(hm_ref[p, :, rows, :].reshape(RW_HEADS * WKV_CHUNK, 128)) for p in range(N_PAIR)]
    hi, lo = _split(jnp.concatenate(tiles, axis=0)[:N_VEC * RW_HEAD])
    return jnp.concatenate([hi, lo], axis=1)


def _wkv_group(nch):
    return min(WKV_GROUP, nch)


N_STEP_VEC = N_VEC - 1
PAD_ROWS = 16


def _wkv_fwd(cols, v, e_tab):
    T = v.shape[0]
    Tc = WKV_CHUNK
    nch = T // Tc
    G = _wkv_group(nch)
    J, W = RW_HEAD, RW_WIDTH
    JS = N_STEP_VEC * J

    def body(cols_ref, v_ref, e_ref, o_ref, states_ref, sa_ref, s_ref):
        @pl.when(pl.program_id(0) == 0)
        def _():
            s_ref[...] = jnp.zeros_like(s_ref)

        st = s_ref[...]
        for c in range(G):
            lhs = _wkv_lhs(cols_ref, slice(c * Tc, (c + 1) * Tc))
            for t in range(Tc):
                row = slice(c * Tc + t, c * Tc + t + 1)
                ex = _dot(lhs[:JS], e_ref[t])
                states_ref[c * Tc + t] = st
                sa = -jnp.sum(st * ex[0:J], axis=0, keepdims=True)
                st = st + ex[J:2 * J] * sa + ex[2 * J:3 * J] * v_ref[row, :]
                sa_ref[row, :] = sa
                o_ref[row, :] = jnp.sum(st * ex[3 * J:4 * J], axis=0, keepdims=True)
            st = st * _dot(lhs[JS:], e_ref[Tc - 1])
        s_ref[...] = st

    GT = G * Tc
    return pl.pallas_call(
        body, name="wkv_fwd", grid=(nch // G,),
        in_specs=[pl.BlockSpec((N_PAIR, RW_HEADS, GT, 128), lambda c: (0, 0, c, 0)),
                  pl.BlockSpec((GT, W), lambda c: (c, 0)), _full((Tc, 2 * 128, W))],
        out_specs=[pl.BlockSpec((GT, W), lambda c: (c, 0)), pl.BlockSpec((GT, J, W), lambda c: (c, 0, 0)),
                   pl.BlockSpec((GT, W), lambda c: (c, 0))],
        out_shape=[jax.ShapeDtypeStruct((T, W), F32), jax.ShapeDtypeStruct((T, J, W), F32),
                   jax.ShapeDtypeStruct((T, W), F32)],
        scratch_shapes=[pltpu.VMEM((J, W), F32)],
        compiler_params=_cparams(dimension_semantics=("arbitrary",)),
    )(cols, v, e_tab)


def _wkv_bwd(cols, v, do, states, sa, e_tab, r_tab):
    T = v.shape[0]
    Tc = WKV_CHUNK
    nch = T // Tc
    G = _wkv_group(nch)
    ngr = nch // G
    J, W = RW_HEAD, RW_WIDTH
    JS = N_STEP_VEC * J
    blocks = [slice(b * 128, (b + 1) * 128) for b in range(W // 128)]

    def body(cols_ref, v_ref, do_ref, states_ref, sa_ref, e_ref, r_ref, dv_ref, dhm_ref, ds_ref):
        @pl.when(pl.program_id(0) == 0)
        def _():
            ds_ref[...] = jnp.zeros_like(ds_ref)

        d_carry = [ds_ref[:, b] for b in blocks]
        last = Tc - 1
        for c in reversed(range(G)):
            at = c * Tc
            rows = slice(at, at + Tc)
            lhs = _wkv_lhs(cols_ref, rows)
            ex = _dot(lhs, e_ref[last])
            dst, ends = [], []
            for i, b in enumerate(blocks):
                s_end = (states_ref[at + last, :, b] + ex[J:2 * J, b] * sa_ref[at + last:at + Tc, b]
                         + ex[2 * J:3 * J, b] * v_ref[at + last:at + Tc, b])
                ends.append((d_carry[i] * s_end).astype(BF16))
                dst.append(d_carry[i] * ex[JS:, b])
            d_decay = _dot(jnp.concatenate(ends, axis=1), r_ref[last])
            acc = jnp.zeros((JS + PAD_ROWS, 128), F32)
            for t in reversed(range(Tc)):
                row = slice(at + t, at + t + 1)
                if t != last:
                    ex = _dot(lhs[:JS], e_ref[t])
                dvs, prods = [], []
                for i, b in enumerate(blocks):
                    kk_e, b_e, k_e, r_e = (ex[n * J:(n + 1) * J, b] for n in range(N_STEP_VEC))
                    do_row, v_row, sa_row = do_ref[row, b], v_ref[row, b], sa_ref[row, b]
                    s_old = states_ref[at + t, :, b]
                    s_new = states_ref[at + t + 1, :, b] if t != last else s_old + b_e * sa_row + k_e * v_row
                    dsn = dst[i] + r_e * do_row
                    dsa = jnp.sum(dsn * b_e, axis=0, keepdims=True)
                    dvs.append(jnp.sum(dsn * k_e, axis=0, keepdims=True))
                    prods.append(jnp.concatenate(
                        [s_old * (-dsa), dsn * sa_row, dsn * v_row, s_new * do_row, jnp.zeros((PAD_ROWS, 128), F32)],
                        axis=0).astype(BF16))
                    dst[i] = dsn - kk_e * dsa
                dv_ref[row, :] = jnp.concatenate(dvs, axis=1)
                acc = acc + _dot(jnp.concatenate(prods, axis=1), r_ref[t])
            d_carry = dst
            tiles = jnp.concatenate([acc[:JS], d_decay, jnp.zeros((2 * N_PAIR * J - N_VEC * J, 128), F32)], axis=0)
            for p in range(N_PAIR):
                dhm_ref[p, :, rows, :] = jnp.transpose(tiles[p * 128:(p + 1) * 128]).reshape(RW_HEADS, Tc, 128)
        for i, b in enumerate(blocks):
            ds_ref[:, b] = d_carry[i]

    GT = G * Tc
    rev2 = lambda c: (ngr - 1 - c, 0)
    rev3 = lambda c: (ngr - 1 - c, 0, 0)
    rev_hm = lambda c: (0, 0, ngr - 1 - c, 0)
    hm_spec = pl.BlockSpec((N_PAIR, RW_HEADS, GT, 128), rev_hm)
    return pl.pallas_call(
        body, name="wkv_bwd", grid=(ngr,),
        in_specs=[hm_spec, pl.BlockSpec((GT, W), rev2), pl.BlockSpec((GT, W), rev2),
                  pl.BlockSpec((GT, J, W), rev3), pl.BlockSpec((GT, W), rev2),
                  _full((Tc, 2 * 128, W)), _full((Tc, W, 128))],
        out_specs=[pl.BlockSpec((GT, W), rev2), hm_spec],
        out_shape=[jax.ShapeDtypeStruct((T, W), F32), jax.ShapeDtypeStruct((N_PAIR, RW_HEADS, T, 128), F32)],
        scratch_shapes=[pltpu.VMEM((J, W), F32)],
        compiler_params=_cparams(dimension_semantics=("arbitrary",)),
    )(cols, v, do, states, sa, e_tab, r_tab)


def _outproj(x, y_ret, y_rw, w_out_b, target, gf):
    T = x.shape[0]
    tm = _row_tile(T)
    W = RW_WIDTH

    def body(x_ref, yr_ref, yw_ref, w_ref, t_ref, gf_ref, loss_ref, dh_ref, dy_ref, dw_ref, dgf_ref):
        @pl.when(pl.program_id(0) == 0)
        def _():
            loss_ref[...] = jnp.zeros_like(loss_ref)
            dw_ref[...] = jnp.zeros_like(dw_ref)
            dgf_ref[...] = jnp.zeros_like(dgf_ref)

        y = jnp.concatenate([yr_ref[...], yw_ref[...]], axis=1)
        w = w_ref[...]
        h = x_ref[...] + _dot(y, w)
        rstd = lax.rsqrt(jnp.mean(h * h, axis=-1, keepdims=True) + RMS_EPS)
        hn = h * rstd
        gfv = gf_ref[...]
        err = hn * gfv - t_ref[...]
        loss_ref[...] += 0.5 * jnp.sum(jnp.mean(err * err, axis=-1))
        dout = err * (1.0 / D_MODEL)
        dgf_ref[...] += jnp.sum(dout * hn, axis=0, keepdims=True)
        dhn = dout * gfv
        dh = rstd * (dhn - hn * jnp.mean(dhn * hn, axis=-1, keepdims=True))
        dh_ref[...] = dh
        dhb = dh.astype(BF16)
        dy_ref[...] = _dot_nt(dhb, w)
        dw_ref[...] += _dot_tn(y, dhb)

    return pl.pallas_call(
        body, name="outproj_loss", grid=(T // tm,),
        in_specs=[pl.BlockSpec((tm, D_MODEL), lambda i: (i, 0)), pl.BlockSpec((tm, W), lambda i: (i, 0)),
                  pl.BlockSpec((tm, W), lambda i: (i, 0)), _full((D_MODEL, D_MODEL)),
                  pl.BlockSpec((tm, D_MODEL), lambda i: (i, 0)), _full((1, D_MODEL))],
        out_specs=[_full((1, PACK_W)), pl.BlockSpec((tm, D_MODEL), lambda i: (i, 0)),
                   pl.BlockSpec((tm, D_MODEL), lambda i: (i, 0)), _full((D_MODEL, D_MODEL)), _full((1, D_MODEL))],
        out_shape=[jax.ShapeDtypeStruct((1, PACK_W), F32), jax.ShapeDtypeStruct((T, D_MODEL), F32),
                   jax.ShapeDtypeStruct((T, D_MODEL), F32), jax.ShapeDtypeStruct((D_MODEL, D_MODEL), F32),
                   jax.ShapeDtypeStruct((1, D_MODEL), F32)],
        compiler_params=_cparams(dimension_semantics=("arbitrary",)),
    )(x, y_ret, y_rw, w_out_b, target, gf)


def _inproj_bwd_x(dp_ret, dp_rw, dprev, dfirst, w_t, x, norm_g, dh):
    T = x.shape[0]
    tm = _row_tile(T)
    nt = T // tm

    def body(dpr_ref, dpw_ref, dprev_ref, dnext_ref, w_ref, x_ref, g_ref, dh_ref, gx_ref, dg_ref, dpt_ref):
        @pl.when(pl.program_id(0) == 0)
        def _():
            dg_ref[...] = jnp.zeros_like(dg_ref)

        next_row = jnp.where(pl.program_id(0) == nt - 1, 0.0, dnext_ref[0])
        dp = jnp.concatenate([dpr_ref[...], dpw_ref[...] + _shift_up(dprev_ref[...], next_row)], axis=1)
        dpt_ref[...] = jnp.transpose(dp).astype(BF16)
        du = _dot(dp.astype(BF16), w_ref[...])
        xf = x_ref[...]
        rstd = lax.rsqrt(jnp.mean(xf * xf, axis=-1, keepdims=True) + RMS_EPS)
        xn = xf * rstd
        dg_ref[...] += jnp.sum(du * xn, axis=0, keepdims=True)
        dxn = du * g_ref[...]
        gx_ref[...] = dh_ref[...] + rstd * (dxn - xn * jnp.mean(dxn * xn, axis=-1, keepdims=True))

    return pl.pallas_call(
        body, name="inproj_bwd_x", grid=(T // tm,),
        in_specs=[pl.BlockSpec((tm, RET_COLS), lambda i: (i, 0)), pl.BlockSpec((tm, RW_COLS), lambda i: (i, 0)),
                  pl.BlockSpec((tm, RW_COLS), lambda i: (i, 0)),
                  pl.BlockSpec((1, 1, RW_COLS), lambda i: (jnp.minimum(i + 1, nt - 1), 0, 0)),
                  _full((IN_COLS, D_MODEL)), pl.BlockSpec((tm, D_MODEL), lambda i: (i, 0)), _full((1, D_MODEL)),
                  pl.BlockSpec((tm, D_MODEL), lambda i: (i, 0))],
        out_specs=[pl.BlockSpec((tm, D_MODEL), lambda i: (i, 0)), _full((1, D_MODEL)),
                   pl.BlockSpec((IN_COLS, tm), lambda i: (0, i))],
        out_shape=[jax.ShapeDtypeStruct((T, D_MODEL), F32), jax.ShapeDtypeStruct((1, D_MODEL), F32),
                   jax.ShapeDtypeStruct((IN_COLS, T), BF16)],
        compiler_params=_cparams(dimension_semantics=("arbitrary",)),
    )(dp_ret, dp_rw, dprev, dfirst, w_t, x, norm_g, dh)


def _inproj_bwd_w(dp_t, u):
    T = u.shape[0]

    def body(d_ref, u_ref, o_ref):
        o_ref[...] = _dot(d_ref[...], u_ref[...])

    return pl.pallas_call(
        body, name="inproj_bwd_w", grid=(N_CHIPS,),
        in_specs=[pl.BlockSpec((IN_SHARD, T), lambda i: (i, 0)), _full((T, D_MODEL))],
        out_specs=pl.BlockSpec((IN_SHARD, D_MODEL), lambda i: (i, 0)),
        out_shape=jax.ShapeDtypeStruct((IN_COLS, D_MODEL), F32),
        compiler_params=_cparams(dimension_semantics=("arbitrary",)),
    )(dp_t, u)


def _with_own(gathered, own, chip):
    return lax.dynamic_update_slice(gathered, own[None], (chip, 0, 0, 0))


def _local_step(x, target, w_in_t, late, chip, small):
    T = x.shape[0]
    tm = _row_tile(T)
    W = RW_WIDTH
    cos, sin = _rope_tables(T)
    tabs = _ret_tables()
    seg64 = _seg_matrix(RW_WIDTH, RW_HEAD)
    e_tab = _wkv_expand_table()
    r_tab = _wkv_reduce_table()

    p_ret, p_rw, u, bnd, *gathered = _inproj_fwd(x, small["norm_g"], w_in_t, late)
    g_out, g_lw, g_la = (_with_own(g, own, chip) for g, own in zip(gathered, late))
    w_out_b = g_out.reshape(D_MODEL, D_MODEL)
    lw = jnp.transpose(g_lw.reshape(N_CHIPS, LORA, LORA_SHARD), (1, 0, 2)).reshape(LORA, W)
    la = jnp.transpose(g_la.reshape(N_CHIPS, LORA, LORA_SHARD), (1, 0, 2)).reshape(LORA, W)
    zero = jnp.zeros((LORA, W), F32)
    lora = jnp.concatenate([jnp.concatenate([lw, zero], axis=1), jnp.concatenate([zero, la], axis=1)], axis=0)
    prep_w = (small["rwkv_mu"], small["w0"], small["a0"], small["k_k"], small["k_a"], lora, seg64, _chunk_tables(tm))
    post_w = (small["rwkv_gn_g"], small["rwkv_gn_b"], small["r_k"], seg64)

    y_ret, ret, s_in_all = _ret_fwd(p_ret, cos, sin, tabs, small["ret_gn_g"])
    hm, v, g, rk = _prep_fwd(p_rw, bnd, *prep_w)
    o, states, sa = _wkv_fwd(hm, v, e_tab)
    y_rw = _post_fwd(o, rk, v, g, *post_w)
    loss, dh, dy, d_w_out, d_gf = _outproj(x, y_ret, y_rw, w_out_b, target, small["final_norm_g"])

    do, d_rk, dv2, dg, d_gn_g, d_gn_b, d_r_k = _post_bwd(o, rk, v, g, *post_w, dy)
    dv1, d_hm = _wkv_bwd(hm, v, do, states, sa, e_tab, r_tab)
    dp_rw, dprev, dfirst, d_mu, d_w0, d_a0, d_k_k, d_k_a, d_lora = _prep_bwd(
        p_rw, bnd, *prep_w, (d_hm, d_rk, dv1, dv2, dg))
    dp_ret, d_ret_gn = _ret_bwd(p_ret, cos, sin, tabs, small["ret_gn_g"], ret, s_in_all, dy)
    grad_x, d_norm_g, dp_t = _inproj_bwd_x(dp_ret, dp_rw, dprev, dfirst, w_in_t, x, small["norm_g"], dh)
    d_w_in = _inproj_bwd_w(dp_t, u)

    d_small = {"norm_g": d_norm_g, "ret_gn_g": d_ret_gn, "rwkv_mu": d_mu, "w0": d_w0, "a0": d_a0, "k_k": d_k_k,
               "k_a": d_k_a, "r_k": d_r_k, "rwkv_gn_g": d_gn_g, "rwkv_gn_b": d_gn_b, "final_norm_g": d_gf}
    return loss, grad_x, d_w_in, d_w_out, d_lora, d_small


ANY = pl.BlockSpec(memory_space=pl.ANY)
CHIP_FLIPS = ((0, 1), (1, 0), (1, 1))
N_FLIPS = len(CHIP_FLIPS)
LORA_SHARD = RW_WIDTH // N_CHIPS
HALF_IN = IN_SHARD // 2
HALF_OUT = OUT_SHARD // 2


def _position():
    return lax.axis_index("x"), lax.axis_index("y"), lax.axis_index("c")


def _flip(v, f):
    return 1 - v if f else v


def _finish(local, remote, landed):
    for cp in landed:
        cp.wait_recv()
    for cp in remote:
        cp.wait_send()
    for cp in local:
        cp.wait()


def _gather_copies(ins, outs, sems):
    send, recv, pass_send, pass_recv = sems
    x, y, c = _position()
    s = 2 * x + y
    sibling = (x, y, 1 - c)
    first, second = (1 - c, c), (c, 1 - c)
    (x1, y1), (x2, y2) = ((x + fx - 2 * x * fx, y + fy - 2 * y * fy) for fx, fy in (first, second))
    s1, s2, sd = 2 * x1 + y1, 2 * x2 + y2, 2 * (1 - x) + (1 - y)

    def copy(src, dst, pair, k, to):
        return pltpu.make_async_remote_copy(src_ref=src, dst_ref=dst, send_sem=pair[0].at[k], recv_sem=pair[1].at[k],
                                            device_id=to, device_id_type=MESH)

    at_once, steps, passed_in, sends = [], [], [], []
    for a in range(len(ins)):
        k = a * N_FLIPS
        own, out = ins[a].at[c], outs[a]
        ici = (send, recv)
        to_first = copy(own, out.at[s, c], ici, k, (x1, y1, c))
        to_second = copy(own, out.at[s, c], ici, k + 1, (x2, y2, c))
        relay = copy(out.at[s1, c], out.at[s1, c], ici, k + 2, (x2, y2, c))
        passes = [copy(out.at[slot, c], out.at[slot, c], (pass_send, pass_recv), k + j, sibling)
                  for j, slot in enumerate((s1, s2, sd))]
        at_once += [to_first, to_second]
        steps += [(copy(own, out.at[s1, c], ici, k, (x1, y1, c)), [relay, passes[0]]),
                  (copy(own, out.at[s2, c], ici, k + 1, (x2, y2, c)), [passes[1]]),
                  (copy(own, out.at[sd, c], ici, k + 2, (x2, y2, c)), [passes[2]])]
        passed_in += [copy(out.at[slot, 1 - c], out.at[slot, 1 - c], (pass_send, pass_recv), k + j, sibling)
                      for j, slot in enumerate((s2, s1, sd))]
        sends += [to_first, to_second, relay] + passes
    return at_once, steps, passed_in, sends


def _gather_start(copies):
    for cp in copies[0]:
        cp.start()


def _gather_finish(copies):
    _, steps, passed_in, sends = copies
    for arrived, then in steps:
        arrived.wait_recv()
        for cp in then:
            cp.start()
    _finish([], sends, passed_in)


def _gather_sems(n):
    return [pltpu.SemaphoreType.DMA((n * N_FLIPS,))] * 4


def _gather_shapes(arrs):
    return [jax.ShapeDtypeStruct((N_CHIPS,) + a.shape, a.dtype) for a in arrs]


def _gather_chips(arrs):
    n = len(arrs)

    def body(*refs):
        copies = _gather_copies(refs[:n], refs[n:2 * n], refs[2 * n:])
        _gather_start(copies)
        _gather_finish(copies)

    return pl.pallas_call(
        body, name="gather_weights",
        in_specs=[ANY] * n, out_specs=[ANY] * n,
        out_shape=_gather_shapes(arrs), scratch_shapes=_gather_sems(n),
    )(*arrs)


def _pair_exchange(g_in, g_out, g_small):
    def body(gi_ref, go_ref, gs_ref, li_ref, lo_ref, ls_ref, send, recv):
        x, y, c = _position()
        peer = (x, y, 1 - c)
        srcs = (gi_ref.at[:, pl.ds((1 - c) * HALF_IN, HALF_IN), :], go_ref.at[:, pl.ds((1 - c) * HALF_OUT, HALF_OUT), :],
                gs_ref)
        remote = [pltpu.make_async_remote_copy(src_ref=src, dst_ref=dst, send_sem=send.at[k], recv_sem=recv.at[k],
                                               device_id=peer, device_id_type=MESH)
                  for k, (src, dst) in enumerate(zip(srcs, (li_ref, lo_ref, ls_ref)))]
        for cp in remote:
            cp.start()
        _finish([], remote, remote)

    return pl.pallas_call(
        body, name="pair_exchange",
        in_specs=[ANY] * 3, out_specs=[ANY] * 3,
        out_shape=[jax.ShapeDtypeStruct((N_CHIPS, HALF_IN, D_MODEL), F32),
                   jax.ShapeDtypeStruct((N_CHIPS, HALF_OUT, D_MODEL), F32),
                   jax.ShapeDtypeStruct(g_small.shape, F32)],
        scratch_shapes=[pltpu.SemaphoreType.DMA((3,)), pltpu.SemaphoreType.DMA((3,))],
    )(g_in, g_out, g_small)


def _pair_sum(g_in, g_out, g_small, l_in, l_out, l_small, c_arr):
    def body(c_ref, gi_ref, go_ref, gs_ref, li_ref, lo_ref, ls_ref, ci_ref, co_ref, cs_ref):
        ci_ref[...] = (gi_ref[...] + li_ref[...]).astype(BF16)
        co_ref[...] = (go_ref[...] + lo_ref[...]).astype(BF16)

        @pl.when(pl.program_id(0) == 0)
        def _():
            cs_ref[...] = gs_ref[...] + ls_ref[...]

    nd = g_small.shape
    return pl.pallas_call(
        body, name="pair_sum",
        grid_spec=pltpu.PrefetchScalarGridSpec(
            num_scalar_prefetch=1, grid=(N_CHIPS,),
            in_specs=[pl.BlockSpec((1, HALF_IN, D_MODEL), lambda s, c: (s, c[0], 0)),
                      pl.BlockSpec((1, HALF_OUT, D_MODEL), lambda s, c: (s, c[0], 0)),
                      pl.BlockSpec(nd, lambda s, c: (0, 0)),
                      pl.BlockSpec((1, HALF_IN, D_MODEL), lambda s, c: (s, 0, 0)),
                      pl.BlockSpec((1, HALF_OUT, D_MODEL), lambda s, c: (s, 0, 0)),
                      pl.BlockSpec(nd, lambda s, c: (0, 0))],
            out_specs=[pl.BlockSpec((1, HALF_IN, D_MODEL), lambda s, c: (s, 0, 0)),
                       pl.BlockSpec((1, HALF_OUT, D_MODEL), lambda s, c: (s, 0, 0)),
                       pl.BlockSpec(nd, lambda s, c: (0, 0))]),
        out_shape=[jax.ShapeDtypeStruct((N_CHIPS, HALF_IN, D_MODEL), BF16),
                   jax.ShapeDtypeStruct((N_CHIPS, HALF_OUT, D_MODEL), BF16), jax.ShapeDtypeStruct(nd, F32)],
        compiler_params=_cparams(dimension_semantics=("arbitrary",)),
    )(c_arr, g_in, g_out, g_small, l_in, l_out, l_small)


def _chip_exchange(c_in, c_out, c_small):
    def body(ci_ref, co_ref, cs_ref, li_ref, lo_ref, ls_ref, send, recv):
        x, y, c = _position()
        s = 2 * x + y
        remote = []
        for j, (fx, fy) in enumerate(CHIP_FLIPS):
            px, py = _flip(x, fx), _flip(y, fy)
            ps = 2 * px + py
            for a, (src, dst) in enumerate(((ci_ref.at[ps], li_ref.at[j]), (co_ref.at[ps], lo_ref.at[j]),
                                            (cs_ref, ls_ref.at[j]))):
                k = 3 * j + a
                remote.append(pltpu.make_async_remote_copy(src_ref=src, dst_ref=dst, send_sem=send.at[k],
                                                           recv_sem=recv.at[k], device_id=(px, py, c),
                                                           device_id_type=MESH))
        for cp in remote:
            cp.start()
        _finish([], remote, remote)

    return pl.pallas_call(
        body, name="chip_exchange",
        in_specs=[ANY] * 3, out_specs=[ANY] * 3,
        out_shape=[jax.ShapeDtypeStruct((N_FLIPS, HALF_IN, D_MODEL), c_in.dtype),
                   jax.ShapeDtypeStruct((N_FLIPS, HALF_OUT, D_MODEL), c_out.dtype),
                   jax.ShapeDtypeStruct((N_FLIPS,) + c_small.shape, F32)],
        scratch_shapes=[pltpu.SemaphoreType.DMA((3 * N_FLIPS,)), pltpu.SemaphoreType.DMA((3 * N_FLIPS,))],
    )(c_in, c_out, c_small)


def _chip_sum(g_in, g_out, p_in, p_out, c_small, l_in, l_out, l_small, sc_arr):
    nd = c_small.shape

    def body(s_ref, gi_ref, go_ref, pi_ref, po_ref, cs_ref, li0, li1, li2, lo0, lo1, lo2, ls_ref,
             ri_ref, ro_ref, rs_ref):
        ri_ref[...] = (((gi_ref[0] + pi_ref[0]) + li0[0].astype(F32)) + li1[0].astype(F32)) + li2[0].astype(F32)
        ro_ref[...] = (((go_ref[0] + po_ref[0]) + lo0[0].astype(F32)) + lo1[0].astype(F32)) + lo2[0].astype(F32)
        me = s_ref[0]
        parts = (cs_ref[...], ls_ref[0], ls_ref[1], ls_ref[2])

        def of_chip(s):
            m = jnp.bitwise_xor(me, s)
            return jnp.where(m == 0, parts[0], jnp.where(m == 1, parts[1], jnp.where(m == 2, parts[2], parts[3])))

        rs_ref[...] = ((of_chip(0) + of_chip(1)) + of_chip(2)) + of_chip(3)

    def flip_in(j):
        return pl.BlockSpec((1, HALF_IN, D_MODEL), lambda i, s: (j, 0, 0))

    def flip_out(j):
        return pl.BlockSpec((1, HALF_OUT, D_MODEL), lambda i, s: (j, 0, 0))

    return pl.pallas_call(
        body, name="chip_sum",
        grid_spec=pltpu.PrefetchScalarGridSpec(
            num_scalar_prefetch=1, grid=(1,),
            in_specs=[pl.BlockSpec((1, HALF_IN, D_MODEL), lambda i, s: (s[0], s[1], 0)),
                      pl.BlockSpec((1, HALF_OUT, D_MODEL), lambda i, s: (s[0], s[1], 0)),
                      pl.BlockSpec((1, HALF_IN, D_MODEL), lambda i, s: (s[0], 0, 0)),
                      pl.BlockSpec((1, HALF_OUT, D_MODEL), lambda i, s: (s[0], 0, 0)),
                      pl.BlockSpec(nd, lambda i, s: (0, 0)),
                      flip_in(0), flip_in(1), flip_in(2), flip_out(0), flip_out(1), flip_out(2),
                      pl.BlockSpec((N_FLIPS,) + nd, lambda i, s: (0, 0, 0))],
            out_specs=[pl.BlockSpec((HALF_IN, D_MODEL), lambda i, s: (0, 0)),
                       pl.BlockSpec((HALF_OUT, D_MODEL), lambda i, s: (0, 0)),
                       pl.BlockSpec(nd, lambda i, s: (0, 0))]),
        out_shape=[jax.ShapeDtypeStruct((HALF_IN, D_MODEL), F32), jax.ShapeDtypeStruct((HALF_OUT, D_MODEL), F32),
                   jax.ShapeDtypeStruct(nd, F32)],
        compiler_params=_cparams(dimension_semantics=("arbitrary",)),
    )(sc_arr, g_in, g_out, p_in, p_out, c_small, l_in, l_in, l_in, l_out, l_out, l_out, l_small)


def _pair_share(r_in, r_out):
    def body(ri_ref, ro_ref, li_ref, lo_ref, send, recv):
        x, y, c = _position()
        remote = [pltpu.make_async_remote_copy(src_ref=src, dst_ref=dst, send_sem=send.at[k], recv_sem=recv.at[k],
                                               device_id=(x, y, 1 - c), device_id_type=MESH)
                  for k, (src, dst) in enumerate(((ri_ref, li_ref), (ro_ref, lo_ref)))]
        for cp in remote:
            cp.start()
        _finish([], remote, remote)

    return pl.pallas_call(
        body, name="pair_share",
        in_specs=[ANY] * 2, out_specs=[ANY] * 2,
        out_shape=[jax.ShapeDtypeStruct(r_in.shape, F32), jax.ShapeDtypeStruct(r_out.shape, F32)],
        scratch_shapes=[pltpu.SemaphoreType.DMA((2,)), pltpu.SemaphoreType.DMA((2,))],
    )(r_in, r_out)


def _adam_update(w, g, m, v):
    mn = ADAM_B1 * m + (1.0 - ADAM_B1) * g
    vn = ADAM_B2 * v + (1.0 - ADAM_B2) * jnp.square(g)
    m_hat = mn / (1.0 - ADAM_B1 ** ADAM_STEP)
    v_hat = vn / (1.0 - ADAM_B2 ** ADAM_STEP)
    return -ADAM_LR * (m_hat / (jnp.sqrt(v_hat) + ADAM_EPS) + ADAM_WD * w), mn, vn


def _adamw(name, w, g_mine, g_theirs, m, v, core_arr, tr):
    rows, cols = w.shape
    per_half = rows // 2 // tr

    def body(c_ref, w_ref, gm_ref, gt_ref, m_ref, v_ref, g_ref, d_ref, nm_ref, nv_ref):
        mine = (pl.program_id(0) // per_half) == c_ref[0]
        g = jnp.where(mine, gm_ref[...], gt_ref[...])
        d, mn, vn = _adam_update(w_ref[...], g, m_ref[...], v_ref[...])
        g_ref[...] = g
        d_ref[...] = d
        nm_ref[...] = mn
        nv_ref[...] = vn

    spec = pl.BlockSpec((tr, cols), lambda i, c: (i, 0))
    half = pl.BlockSpec((tr, cols), lambda i, c: (i % per_half, 0))
    return pl.pallas_call(
        body, name=name,
        grid_spec=pltpu.PrefetchScalarGridSpec(
            num_scalar_prefetch=1, grid=(rows // tr,),
            in_specs=[spec, half, half, spec, spec], out_specs=[spec] * 4),
        out_shape=[jax.ShapeDtypeStruct((rows, cols), F32)] * 4,
        compiler_params=_cparams(dimension_semantics=("arbitrary",)),
    )(core_arr, w, g_mine, g_theirs, m, v)


def _row_pieces(n):
    return [(k, k * PACK_W, min(PACK_W, n - k * PACK_W)) for k in range(-(-n // PACK_W))]


def _pack_small(d_small, loss, d_lora):
    ns = len(SMALL_NAMES)

    def body(*refs):
        small_refs, (loss_ref, lora_ref, out_ref) = refs[:ns], refs[ns:]
        out_ref[...] = jnp.zeros_like(out_ref)
        out_ref[PACK_LORA_W:PACK_LORA_W + LORA, :] = lora_ref[:LORA, :RW_WIDTH]
        out_ref[PACK_LORA_A:PACK_LORA_A + LORA, :] = lora_ref[LORA:, RW_WIDTH:]
        for name, n, ref in zip(SMALL_NAMES, SMALL_SIZES, small_refs):
            for k, at, w in _row_pieces(n):
                out_ref[PACK_AT[name] + k:PACK_AT[name] + k + 1, 0:w] = ref[:, at:at + w]
        out_ref[PACK_LOSS:PACK_LOSS + 1, :] = loss_ref[...]

    return pl.pallas_call(body, name="pack_small", out_shape=jax.ShapeDtypeStruct((PACK_ROWS, PACK_W), F32),
                          compiler_params=_cparams())(*d_small, loss, d_lora)


def _adamw_small(tot, chip_arr, ws, ms, vs):
    ns = len(SMALL_NAMES)
    n_par = ns + 2

    def body(s_ref, tot_ref, glw_ref, gla_ref, *refs):
        w_refs, m_refs, v_refs = refs[:n_par], refs[n_par:2 * n_par], refs[2 * n_par:3 * n_par]
        outs = refs[3 * n_par:]
        g_refs, d_refs, nm_refs, nv_refs = (outs[i * n_par:(i + 1) * n_par] for i in range(4))
        grads = [jnp.concatenate([tot_ref[PACK_AT[name] + k:PACK_AT[name] + k + 1, 0:w] for k, _, w in _row_pieces(n)],
                                 axis=1) for name, n in zip(SMALL_NAMES, SMALL_SIZES)]
        grads += [glw_ref[...], gla_ref[...]]
        for i, g in enumerate(grads):
            d, mn, vn = _adam_update(w_refs[i][...], g, m_refs[i][...], v_refs[i][...])
            g_refs[i][...] = g
            d_refs[i][...] = d
            nm_refs[i][...] = mn
            nv_refs[i][...] = vn

    def whole(a):
        nd = a.ndim
        return pl.BlockSpec(a.shape, lambda i, s: (0,) * nd)

    shard = (LORA, LORA_SHARD)
    par_specs = [whole(a) for a in ws]
    res = pl.pallas_call(
        body, name="adamw_small",
        grid_spec=pltpu.PrefetchScalarGridSpec(
            num_scalar_prefetch=1, grid=(1,),
            in_specs=[whole(tot), pl.BlockSpec(shard, lambda i, s: (PACK_LORA_W // LORA, s[0])),
                      pl.BlockSpec(shard, lambda i, s: (PACK_LORA_A // LORA, s[0]))] + par_specs * 3,
            out_specs=par_specs * 4),
        out_shape=[jax.ShapeDtypeStruct(a.shape, F32) for a in ws] * 4,
        compiler_params=_cparams(dimension_semantics=("arbitrary",)),
    )(chip_arr, tot, tot, tot, *ws, *ms, *vs)
    return [res[i * n_par:(i + 1) * n_par] for i in range(4)]


def kernel(x, norm_g, w_in, ret_gn_g, rwkv_mu, w_lora_up, w0, a_lora_up, a0, k_k, k_a, r_k, rwkv_gn_g, rwkv_gn_b, w_out, final_norm_g, loss_target, m_norm_g, m_w_in, m_ret_gn_g, m_rwkv_mu, m_w_lora_up, m_w0, m_a_lora_up, m_a0, m_k_k, m_k_a, m_r_k, m_rwkv_gn_g, m_rwkv_gn_b, m_w_out, m_final_norm_g, v_norm_g, v_w_in, v_ret_gn_g, v_rwkv_mu, v_w_lora_up, v_w0, v_a_lora_up, v_a0, v_k_k, v_k_a, v_r_k, v_rwkv_gn_g, v_rwkv_gn_b, v_w_out, v_final_norm_g):
    W = RW_WIDTH
    params = dict(norm_g=norm_g, ret_gn_g=ret_gn_g, rwkv_mu=rwkv_mu, w0=w0, a0=a0, k_k=k_k, k_a=k_a, r_k=r_k,
                  rwkv_gn_g=rwkv_gn_g, rwkv_gn_b=rwkv_gn_b, final_norm_g=final_norm_g)
    moments_m = dict(norm_g=m_norm_g, ret_gn_g=m_ret_gn_g, rwkv_mu=m_rwkv_mu, w0=m_w0, a0=m_a0, k_k=m_k_k, k_a=m_k_a,
                     r_k=m_r_k, rwkv_gn_g=m_rwkv_gn_g, rwkv_gn_b=m_rwkv_gn_b, final_norm_g=m_final_norm_g)
    moments_v = dict(norm_g=v_norm_g, ret_gn_g=v_ret_gn_g, rwkv_mu=v_rwkv_mu, w0=v_w0, a0=v_a0, k_k=v_k_k, k_a=v_k_a,
                     r_k=v_r_k, rwkv_gn_g=v_rwkv_gn_g, rwkv_gn_b=v_rwkv_gn_b, final_norm_g=v_final_norm_g)
    xi, yi, ci = _position()
    chip = (2 * xi + yi).astype(jnp.int32)

    def halves(a):
        return a.reshape(2, a.shape[0] // 2, a.shape[1])

    w_t, m_t, v_t = (jnp.transpose(a[0]) for a in (w_in, m_w_in, v_w_in))
    own_in = halves(w_t.astype(BF16))
    w_in_t = _with_own(_gather_chips([own_in])[0], own_in, chip).reshape(IN_COLS, D_MODEL)
    late = [halves(w_out[0].astype(BF16)), halves(w_lora_up[0]), halves(a_lora_up[0])]
    small = {n: params[n].reshape(1, -1) for n in SMALL_NAMES}

    loss, grad_x, d_w_in, d_w_out, d_lora, d_small = _local_step(x[0], loss_target[0], w_in_t, late, chip, small)

    core = ci.astype(jnp.int32)
    gi = d_w_in.reshape(N_CHIPS, IN_SHARD, D_MODEL)
    go = d_w_out.reshape(N_CHIPS, OUT_SHARD, D_MODEL)
    gs = _pack_small([d_small[n] for n in SMALL_NAMES], loss, d_lora)
    p_in, p_out, p_small = _pair_exchange(gi, go, gs)
    c_in, c_out, c_small = _pair_sum(gi, go, gs, p_in, p_out, p_small, core.reshape(1))
    l_in, l_out, l_small = _chip_exchange(c_in, c_out, c_small)
    r_in, r_out, tot = _chip_sum(gi, go, p_in, p_out, c_small, l_in, l_out, l_small, jnp.stack([chip, core]))
    t_in, t_out = _pair_share(r_in, r_out)

    grad_w_in, d_in, nm_in, nv_in = (jnp.transpose(a) for a in _adamw(
        "adamw_w_in", w_t, r_in, t_in, m_t, v_t, core.reshape(1), HALF_IN // 2))
    grad_w_out, d_out, nm_out, nv_out = _adamw("adamw_w_out", w_out[0], r_out, t_out, m_w_out[0], v_w_out[0],
                                               core.reshape(1), HALF_OUT)
    par_names = SMALL_NAMES + ("w_lora_up", "a_lora_up")

    def operands(tree, lw_, la_):
        return [tree[n].reshape(1, -1) for n in SMALL_NAMES] + [lw_[0], la_[0]]

    res = _adamw_small(tot, chip.reshape(1), operands(params, w_lora_up, a_lora_up),
                       operands(moments_m, m_w_lora_up, m_a_lora_up), operands(moments_v, v_w_lora_up, v_a_lora_up))

    names = ("norm_g", "w_in", "ret_gn_g", "rwkv_mu", "w_lora_up", "w0", "a_lora_up", "a0", "k_k", "k_a", "r_k",
             "rwkv_gn_g", "rwkv_gn_b", "w_out", "final_norm_g")
    shapes = dict(w_in=w_in.shape, w_out=w_out.shape, w_lora_up=w_lora_up.shape, a_lora_up=a_lora_up.shape,
                  **{n: params[n].shape for n in SMALL_NAMES})

    def leaves(pars, big_in, big_out):
        tree = dict(zip(par_names, pars), w_in=big_in, w_out=big_out)
        return [tree[n].reshape(shapes[n]) for n in names]

    grads = leaves(res[0], grad_w_in, grad_w_out)
    deltas = leaves(res[1], d_in, d_out)
    new_m = leaves(res[2], nm_in, nm_out)
    new_v = leaves(res[3], nv_in, nv_out)
    return (tot[PACK_LOSS, 0], grad_x.reshape(x.shape), *grads, *deltas, *new_m, *new_v)
```

```python
import functools

import numpy as np
import jax
import jax.numpy as jnp
from jax import lax
from jax.experimental import pallas as pl
from jax.experimental.pallas import tpu as pltpu

F32 = jnp.float32
BF16 = jnp.bfloat16
X3 = "bf16x3"
B1 = "bf16"
MESH = pl.DeviceIdType.MESH

D_MODEL = 1024
N_CHIPS = 4
RET_HEADS = 4
RET_DK = 64
RET_DV = 128
RET_QK = RET_HEADS * RET_DK
RET_WIDTH = RET_HEADS * RET_DV
RET_COLS = 2 * RET_QK + 2 * RET_WIDTH
RET_CHUNK = 64
RET_GROUP = 8
RW_WIDTH = 512
RW_HEAD = 64
RW_HEADS = 8
LORA = 64
RW_COLS = 4 * RW_WIDTH + 2 * LORA
IN_COLS = RET_COLS + RW_COLS
IN_SHARD = IN_COLS // N_CHIPS
OUT_SHARD = D_MODEL // N_CHIPS
ROPE_BASE = 10000.0
RMS_EPS = 1e-6
RET_GN_EPS = 1e-5
RW_GN_EPS = 64e-5
WKV_CHUNK = 16
WKV_GROUP = 8
N_VEC = 5

ADAM_LR = 0.001
ADAM_B1 = 0.9
ADAM_B2 = 0.999
ADAM_EPS = 1e-08
ADAM_WD = 0.01
ADAM_STEP = 10

VMEM_LIMIT = 56 * 1024 * 1024

PACK_W = 512
SMALL_NAMES = ("norm_g", "ret_gn_g", "rwkv_mu", "w0", "a0", "k_k", "k_a", "r_k", "rwkv_gn_g", "rwkv_gn_b",
               "final_norm_g")
SMALL_SIZES = (1024, 512, 2176, 512, 512, 512, 512, 512, 512, 512, 1024)
PACK_LORA_W = 0
PACK_LORA_A = LORA
PACK_SMALL = 2 * LORA


def _pack_layout():
    rows, at = {}, PACK_SMALL
    for name, n in zip(SMALL_NAMES, SMALL_SIZES):
        rows[name] = at
        at += -(-n // PACK_W)
    return rows, at


PACK_AT, PACK_LOSS = _pack_layout()
PACK_ROWS = -(-(PACK_LOSS + 1) // 8) * 8


def _cparams(**kw):
    return pltpu.CompilerParams(vmem_limit_bytes=VMEM_LIMIT, **kw)


def _split(x):
    hi = x.astype(BF16)
    lo = (x - hi.astype(F32)).astype(BF16)
    return hi, lo


def _dot_dims(a, b, dims, precision):
    if precision == B1:
        a, b = a.astype(BF16), b.astype(BF16)
    if precision != X3:
        return lax.dot_general(a, b, dims, preferred_element_type=F32)
    (ah, al), (bh, bl) = _split(a), _split(b)
    dot = lambda u, w: lax.dot_general(u, w, dims, preferred_element_type=F32)
    return dot(ah, bh) + dot(ah, bl) + dot(al, bh)


def _dot(a, b, precision=None):
    return _dot_dims(a, b, (((1,), (0,)), ((), ())), precision)


def _dot_nt(a, b, precision=None):
    return _dot_dims(a, b, (((1,), (1,)), ((), ())), precision)


def _dot_tn(a, b, precision=None):
    return _dot_dims(a, b, (((0,), (0,)), ((), ())), precision)


@jax.custom_vjp
def _segsum(x, seg):
    hi, lo = _split(x)
    return _dot(hi, seg) + _dot(lo, seg)


def _segsum_fwd(x, seg):
    return _segsum(x, seg), seg


def _segsum_bwd(seg, ct):
    return _segsum(ct, seg), jnp.zeros_like(seg)


_segsum.defvjp(_segsum_fwd, _segsum_bwd)


def _softplus(z):
    return jnp.maximum(z, 0.0) + jnp.log(1.0 + jnp.exp(-jnp.abs(z)))


def _full(shape):
    nd = len(shape)
    return pl.BlockSpec(shape, lambda *_: (0,) * nd)


def _rope_tables(T):
    half = RET_DK // 2
    expo = -jnp.arange(half, dtype=F32) / jnp.float32(half)
    freqs = jnp.exp(expo * jnp.float32(np.log(ROPE_BASE)))
    ang = jnp.arange(T, dtype=jnp.int32).astype(F32)[:, None] * freqs[None, :]
    cos = jnp.tile(jnp.cos(ang), (1, 2 * RET_HEADS))
    sin = jnp.tile(jnp.sin(ang), (1, 2 * RET_HEADS))
    return cos, sin


def _ret_tables():
    H, C = RET_HEADS, RET_CHUNK
    hidx = jnp.arange(H, dtype=F32)
    lg = jnp.log(1.0 - jnp.exp2(-5.0 - hidx))
    idx = jnp.arange(C, dtype=F32)
    intra = jnp.exp(lg[:, None, None] * jnp.abs(idx[:, None] - idx[None, :]))
    q_dec = jnp.transpose(jnp.exp(lg[:, None] * (idx[None, :] + 1.0)))
    k_dec = jnp.transpose(jnp.exp(lg[:, None] * (C - 1.0 - idx[None, :])))
    chunk_dec = jnp.exp(lg * C)
    qd = jnp.repeat(q_dec, RET_DK, axis=1)
    kd = jnp.repeat(k_dec, RET_DK, axis=1)
    row_h = np.arange(RET_QK) // RET_DK
    col_h = np.arange(RET_WIDTH) // RET_DV
    bm = jnp.asarray((row_h[:, None] == col_h[None, :]).astype(np.float32))
    cd = bm * jnp.repeat(chunk_dec, RET_DK)[:, None]
    return intra, qd, kd, cd, bm


def _seg_matrix(width, head):
    h = np.arange(width) // head
    return jnp.asarray((h[:, None] == h[None, :]).astype(np.float32), dtype=BF16)


def _wkv_expand_table():
    Tc = WKV_CHUNK
    k = np.arange(2 * RW_HEADS * Tc)
    kh, kt = (k % (RW_HEADS * Tc)) // Tc, k % Tc
    nh = np.arange(RW_WIDTH) // RW_HEAD
    e = (kh[None, :, None] == nh[None, None, :]) & (kt[None, :, None] == np.arange(Tc)[:, None, None])
    return jnp.asarray(e.astype(np.float32), dtype=BF16)


def _wkv_reduce_table():
    Tc = WKV_CHUNK
    kh = np.arange(RW_WIDTH) // RW_HEAD
    n = np.arange(RW_HEADS * Tc)
    nh, nt = n // Tc, n % Tc
    r = (kh[None, :, None] == nh[None, None, :]) & (nt[None, None, :] == np.arange(Tc)[:, None, None])
    return jnp.asarray(r.astype(np.float32), dtype=BF16)


def _inproj_fwd(x, norm_g, w_t, late):
    T = x.shape[0]
    tm = _row_tile(T)
    nt = T // tm
    n = len(late)

    def body(x_ref, g_ref, w_ref, *refs):
        late_in, (pret_ref, prw_ref, u_ref, last_ref) = refs[:n], refs[n:n + 4]
        late_out, sems = refs[n + 4:2 * n + 4], refs[2 * n + 4:]

        @pl.when(pl.program_id(0) == 0)
        def _():
            _gather_start(_gather_copies(late_in, late_out, sems))

        xf = x_ref[...]
        rstd = lax.rsqrt(jnp.mean(xf * xf, axis=-1, keepdims=True) + RMS_EPS)
        ub = ((xf * rstd) * g_ref[...]).astype(BF16)
        u_ref[...] = ub
        pret_ref[...] = _dot_nt(ub, w_ref[:RET_COLS, :])
        p_rw = _dot_nt(ub, w_ref[RET_COLS:, :])
        prw_ref[...] = p_rw
        last_ref[0] = p_rw[tm - 1:tm, :]

        @pl.when(pl.program_id(0) == nt // 2)
        def _():
            _gather_relay(_gather_copies(late_in, late_out, sems))

        @pl.when(pl.program_id(0) == nt - 1)
        def _():
            _gather_finish(_gather_copies(late_in, late_out, sems))

    return pl.pallas_call(
        body, name="inproj_fwd", grid=(nt,),
        in_specs=[pl.BlockSpec((tm, D_MODEL), lambda i: (i, 0)), _full((1, D_MODEL)), _full((IN_COLS, D_MODEL))]
                 + [ANY] * n,
        out_specs=[pl.BlockSpec((tm, RET_COLS), lambda i: (i, 0)), pl.BlockSpec((tm, RW_COLS), lambda i: (i, 0)),
                   pl.BlockSpec((tm, D_MODEL), lambda i: (i, 0)), pl.BlockSpec((1, 1, RW_COLS), lambda i: (i, 0, 0))]
                  + [ANY] * n,
        out_shape=[jax.ShapeDtypeStruct((T, RET_COLS), F32), jax.ShapeDtypeStruct((T, RW_COLS), F32),
                   jax.ShapeDtypeStruct((T, D_MODEL), BF16), jax.ShapeDtypeStruct((nt, 1, RW_COLS), F32)]
                  + _gather_shapes(late),
        scratch_shapes=_gather_sems(n),
        compiler_params=_cparams(dimension_semantics=("arbitrary",)),
    )(x, norm_g, w_t, *late)


def _rot_half(x):
    n = x.shape[1]
    lane = lax.broadcasted_iota(jnp.int32, x.shape, 1)
    first = (lane % RET_DK) < (RET_DK // 2)
    return jnp.where(first, -pltpu.roll(x, n - RET_DK // 2, 1), pltpu.roll(x, RET_DK // 2, 1))


def _rope(x, cos, sin):
    return x * cos + _rot_half(x) * sin


def _rope_bwd(d, cos, sin):
    return d * cos - _rot_half(d * sin)


def _ret_post(ret, g, gn_g):
    heads = []
    for h in range(RET_HEADS):
        xh = ret[:, h * RET_DV:(h + 1) * RET_DV]
        xc = xh - jnp.mean(xh, axis=-1, keepdims=True)
        heads.append(xc * lax.rsqrt(jnp.mean(xc * xc, axis=-1, keepdims=True) + RET_GN_EPS))
    return (g * jax.nn.sigmoid(g)) * (jnp.concatenate(heads, axis=1) * gn_g)


def _ret_scores(qt, kt, d_ref, h):
    lane = lax.broadcasted_iota(jnp.int32, qt.shape, 1)
    qh = jnp.where(lane // RET_DK == h, qt, 0.0)
    return qh, _dot_nt(qh, kt, B1) * d_ref[h]


def _ret_group(nch):
    return min(RET_GROUP, nch)


def _ret_fwd(p_ret, cos, sin, tabs, gn_g):
    T = p_ret.shape[0]
    C = RET_CHUNK
    nch = T // C
    G = _ret_group(nch)
    intra_d, qd, kd, cd, bm = tabs

    def body(q_ref, k_ref, v_ref, g_ref, cos_ref, sin_ref, qd_ref, kd_ref, d_ref, cd_ref, bm_ref, gn_ref,
             y_ref, ret_ref, sin_out_ref, s_ref, qt_buf, kv_buf):
        @pl.when(pl.program_id(0) == 0)
        def _():
            s_ref[...] = jnp.zeros_like(s_ref)

        cosv, sinv = cos_ref[...], sin_ref[...]
        qt_all = _rope(q_ref[...], cosv, sinv)
        kt_all = _rope(k_ref[...], cosv, sinv) * (RET_DK ** -0.5)
        for i in range(G):
            rows = slice(i * C, (i + 1) * C)
            qt, kt, v = qt_all[rows], kt_all[rows], v_ref[rows, :]
            intra = []
            for h in range(RET_HEADS):
                _, a = _ret_scores(qt, kt, d_ref, h)
                intra.append(_dot(a, v[:, h * RET_DV:(h + 1) * RET_DV], B1))
            ret_ref[rows, :] = jnp.concatenate(intra, axis=1)
            qt_buf[rows, :] = qt * qd_ref[...]
            kv_buf[i] = _dot_tn(kt * kd_ref[...], v, B1) * bm_ref[...]
        s_in = s_ref[...]
        for i in range(G):
            rows = slice(i * C, (i + 1) * C)
            sin_out_ref[i] = s_in
            ret_ref[rows, :] += _dot(qt_buf[rows, :], s_in, B1)
            s_in = s_in * cd_ref[...] + kv_buf[i]
        s_ref[...] = s_in
        y_ref[...] = _ret_post(ret_ref[...], g_ref[...], gn_ref[...]).astype(BF16)

    GC = G * C
    return pl.pallas_call(
        body, name="ret_fwd", grid=(nch // G,),
        in_specs=[pl.BlockSpec((GC, RET_QK), lambda c: (c, 0)), pl.BlockSpec((GC, RET_QK), lambda c: (c, 1)),
                  pl.BlockSpec((GC, RET_WIDTH), lambda c: (c, 1)), pl.BlockSpec((GC, RET_WIDTH), lambda c: (c, 2)),
                  pl.BlockSpec((GC, RET_QK), lambda c: (c, 0)), pl.BlockSpec((GC, RET_QK), lambda c: (c, 0)),
                  _full((C, RET_QK)), _full((C, RET_QK)), _full((RET_HEADS, C, C)),
                  _full((RET_QK, RET_WIDTH)), _full((RET_QK, RET_WIDTH)), _full((1, RET_WIDTH))],
        out_specs=[pl.BlockSpec((GC, RET_WIDTH), lambda c: (c, 0)), pl.BlockSpec((GC, RET_WIDTH), lambda c: (c, 0)),
                   pl.BlockSpec((G, RET_QK, RET_WIDTH), lambda c: (c, 0, 0))],
        out_shape=[jax.ShapeDtypeStruct((T, RET_WIDTH), BF16), jax.ShapeDtypeStruct((T, RET_WIDTH), F32),
                   jax.ShapeDtypeStruct((nch, RET_QK, RET_WIDTH), F32)],
        scratch_shapes=[pltpu.VMEM((RET_QK, RET_WIDTH), F32), pltpu.VMEM((GC, RET_QK), F32),
                        pltpu.VMEM((G, RET_QK, RET_WIDTH), F32)],
        compiler_params=_cparams(dimension_semantics=("arbitrary",)),
    )(p_ret, p_ret, p_ret, p_ret, cos, sin, qd, kd, intra_d, cd, bm, gn_g)


def _ret_bwd(p_ret, cos, sin, tabs, gn_g, ret, s_in_all, dy):
    T = p_ret.shape[0]
    C = RET_CHUNK
    nch = T // C
    G = _ret_group(nch)
    ngr = nch // G
    intra_d, qd, kd, cd, bm = tabs

    def rev(j):
        return lambda c: (ngr - 1 - c, j)

    def body(q_ref, k_ref, v_ref, g_ref, cos_ref, sin_ref, qd_ref, kd_ref, d_ref, cd_ref, bm_ref, gn_ref,
             ret_ref, sin_ref_, dy_ref, dp_ref, dgn_ref, ds_ref, dkt_buf, ktk_buf, g_buf):
        @pl.when(pl.program_id(0) == 0)
        def _():
            ds_ref[...] = jnp.zeros_like(ds_ref)
            dgn_ref[...] = jnp.zeros_like(dgn_ref)

        _, post_vjp = jax.vjp(_ret_post, ret_ref[...], g_ref[...], gn_ref[...])
        dret_all, dg_all, dgn = post_vjp(dy_ref[...])
        dgn_ref[...] += dgn
        v_cols = slice(2 * RET_QK, 2 * RET_QK + RET_WIDTH)
        dp_ref[:, 2 * RET_QK + RET_WIDTH:] = dg_all

        qdv, kdv = qd_ref[...], kd_ref[...]
        cosv, sinv = cos_ref[...], sin_ref[...]
        qt_all = _rope(q_ref[...], cosv, sinv)
        kt_all = _rope(k_ref[...], cosv, sinv) * (RET_DK ** -0.5)
        for i in range(G):
            rows = slice(i * C, (i + 1) * C)
            qt, kt, v, dret = qt_all[rows], kt_all[rows], v_ref[rows, :], dret_all[rows, :]
            dqt = qdv * _dot_nt(dret, sin_ref_[i], B1)
            dkt = jnp.zeros_like(kt)
            dvs = []
            for h in range(RET_HEADS):
                sl = slice(h * RET_DV, (h + 1) * RET_DV)
                qh, a = _ret_scores(qt, kt, d_ref, h)
                lane = lax.broadcasted_iota(jnp.int32, kt.shape, 1)
                kh = jnp.where(lane // RET_DK == h, kt, 0.0)
                da = _dot_nt(dret[:, sl], v[:, sl], B1) * d_ref[h]
                dvs.append(_dot_tn(a, dret[:, sl], B1))
                dqt = dqt + _dot(da, kh, B1)
                dkt = dkt + _dot_tn(da, qh, B1)
            dp_ref[rows, :RET_QK] = _rope_bwd(dqt, cosv[rows], sinv[rows])
            dp_ref[rows, v_cols] = jnp.concatenate(dvs, axis=1)
            dkt_buf[rows, :] = dkt
            ktk_buf[rows, :] = kt * kdv
            g_buf[i] = _dot_tn(qt * qdv, dret, B1) * bm_ref[...]
        ds_out = ds_ref[...]
        for i in reversed(range(G)):
            rows = slice(i * C, (i + 1) * C)
            dkt = dkt_buf[rows, :] + kdv * _dot_nt(v_ref[rows, :], ds_out, B1)
            dp_ref[rows, RET_QK:2 * RET_QK] = _rope_bwd(dkt * (RET_DK ** -0.5), cosv[rows], sinv[rows])
            dp_ref[rows, v_cols] += _dot(ktk_buf[rows, :], ds_out, B1)
            ds_out = ds_out * cd_ref[...] + g_buf[i]
        ds_ref[...] = ds_out

    GC = G * C
    return pl.pallas_call(
        body, name="ret_bwd", grid=(ngr,),
        in_specs=[pl.BlockSpec((GC, RET_QK), rev(0)), pl.BlockSpec((GC, RET_QK), rev(1)),
                  pl.BlockSpec((GC, RET_WIDTH), rev(1)), pl.BlockSpec((GC, RET_WIDTH), rev(2)),
                  pl.BlockSpec((GC, RET_QK), rev(0)), pl.BlockSpec((GC, RET_QK), rev(0)),
                  _full((C, RET_QK)), _full((C, RET_QK)), _full((RET_HEADS, C, C)),
                  _full((RET_QK, RET_WIDTH)), _full((RET_QK, RET_WIDTH)), _full((1, RET_WIDTH)),
                  pl.BlockSpec((GC, RET_WIDTH), rev(0)),
                  pl.BlockSpec((G, RET_QK, RET_WIDTH), lambda c: (ngr - 1 - c, 0, 0)),
                  pl.BlockSpec((GC, RET_WIDTH), rev(0))],
        out_specs=[pl.BlockSpec((GC, RET_COLS), rev(0)), _full((1, RET_WIDTH))],
        out_shape=[jax.ShapeDtypeStruct((T, RET_COLS), F32), jax.ShapeDtypeStruct((1, RET_WIDTH), F32)],
        scratch_shapes=[pltpu.VMEM((RET_QK, RET_WIDTH), F32), pltpu.VMEM((GC, RET_QK), F32),
                        pltpu.VMEM((GC, RET_QK), F32), pltpu.VMEM((G, RET_QK, RET_WIDTH), F32)],
        compiler_params=_cparams(dimension_semantics=("arbitrary",)),
    )(p_ret, p_ret, p_ret, p_ret, cos, sin, qd, kd, intra_d, cd, bm, gn_g, ret, s_in_all, dy)


@jax.custom_vjp
def _chunk_sums(x, tri):
    hi, lo = _split(x)
    return _dot(tri, hi) + _dot(tri, lo)


def _chunk_sums_fwd(x, tri):
    return _chunk_sums(x, tri), tri


def _chunk_sums_bwd(tri, ct):
    hi, lo = _split(ct)
    return _dot_tn(tri, hi) + _dot_tn(tri, lo), jnp.zeros_like(tri)


_chunk_sums.defvjp(_chunk_sums_fwd, _chunk_sums_bwd)


@jax.custom_vjp
def _lora_dot(z, lora):
    return _dot(z, lora, X3)


def _lora_dot_fwd(z, lora):
    return _lora_dot(z, lora), (z, lora)


def _lora_dot_bwd(res, ct):
    z, lora = res
    return _dot_nt(ct, lora, X3), _dot_tn(z, ct, X3)


_lora_dot.defvjp(_lora_dot_fwd, _lora_dot_bwd)


def _chunk_tables(tm):
    t = np.arange(tm)
    same = (t[:, None] // WKV_CHUNK) == (t[None, :] // WKV_CHUNK)
    return jnp.asarray(np.stack([same & (t[None, :] <= t[:, None]), same]).astype(np.float32), dtype=BF16)


def _prep_fn(p, prev, mu, w0, a0, k_k, k_a, lora, seg, tri):
    W = RW_WIDTH
    ps = p + mu * (prev - p)
    r, kr, vr, g = ps[:, 0:W], ps[:, W:2 * W], ps[:, 2 * W:3 * W], ps[:, 3 * W:4 * W]
    z = ps[:, 4 * W:]
    lane = lax.broadcasted_iota(jnp.int32, z.shape, 1)
    z = jnp.where(lane < LORA, jnp.tanh(z), z)
    lo = _lora_dot(z, lora)
    w_log = -_softplus(-(w0 + lo[:, :W])) - 0.5
    log_decay = -jnp.exp(w_log)
    cum = _chunk_sums(log_decay, tri[0])
    total = _chunk_sums(log_decay, tri[1])
    a = jax.nn.sigmoid(a0 + lo[:, W:])
    kk = kr * k_k
    kk = kk / jnp.maximum(jnp.sqrt(_segsum(kk * kk, seg)), 1e-12)
    k = kr * (1.0 + (a - 1.0) * k_a)
    grow = jnp.exp(-cum)
    return kk * jnp.exp(cum - log_decay), (kk * a) * grow, k * grow, r * jnp.exp(cum), jnp.exp(total), vr, g, r * k


def _post_fn(o, rk, v, g, gn_g, gn_b, r_k, seg):
    mu = _segsum(o, seg) * (1.0 / RW_HEAD)
    oc = o - mu
    var = _segsum(oc * oc, seg) * (1.0 / RW_HEAD)
    on = oc * lax.rsqrt(var + RW_GN_EPS) * gn_g + gn_b
    bonus = _segsum(rk * r_k, seg) * v
    return (g * jax.nn.sigmoid(g)) * (on + bonus)


N_PAIR = (N_VEC + 1) // 2
HALF_LANES = 64


def _swap_halves(x):
    return pltpu.roll(x, HALF_LANES, 1)


def _pack_heads(vecs):
    tm = vecs[0].shape[0]
    low = lax.broadcasted_iota(jnp.int32, (tm, 128), 1) < HALF_LANES
    out = []
    for p in range(N_PAIR):
        a = vecs[2 * p]
        b = vecs[2 * p + 1] if 2 * p + 1 < len(vecs) else None
        heads = []
        for m in range(RW_WIDTH // 128):
            am = a[:, m * 128:(m + 1) * 128]
            bm = jnp.zeros_like(am) if b is None else b[:, m * 128:(m + 1) * 128]
            heads.append(jnp.where(low, am, _swap_halves(bm)))
            heads.append(jnp.where(low, _swap_halves(am), bm))
        out.append(heads)
    return out


def _unpack_heads(hm_ref):
    tm = hm_ref.shape[2]
    low = lax.broadcasted_iota(jnp.int32, (tm, 128), 1) < HALF_LANES
    vecs = []
    for p in range(N_PAIR):
        a, b = [], []
        for m in range(RW_WIDTH // 128):
            even, odd = hm_ref[p, 2 * m], hm_ref[p, 2 * m + 1]
            a.append(jnp.where(low, even, _swap_halves(odd)))
            b.append(jnp.where(low, _swap_halves(even), odd))
        vecs += [jnp.concatenate(a, axis=1), jnp.concatenate(b, axis=1)]
    return vecs[:N_VEC]


def _shift_down(p, first_row):
    row = lax.broadcasted_iota(jnp.int32, p.shape, 0)
    return jnp.where(row == 0, first_row, pltpu.roll(p, 1, 0))


def _shift_up(p, last_row):
    n = p.shape[0]
    row = lax.broadcasted_iota(jnp.int32, p.shape, 0)
    return jnp.where(row == n - 1, last_row, pltpu.roll(p, n - 1, 0))


def _row_tile(T):
    return min(T, 256)


def _prep_fwd(p_rw, bnd, mu, w0, a0, k_k, k_a, lora, seg64, tri):
    T = p_rw.shape[0]
    tm = _row_tile(T)
    W = RW_WIDTH

    def body(p_ref, bnd_ref, mu_ref, w0_ref, a0_ref, kk_ref, ka_ref, lora_ref, seg_ref, tri_ref,
             hm_ref, v_ref, g_ref, rk_ref):
        p = p_ref[...]
        prev = _shift_down(p, jnp.where(pl.program_id(0) == 0, 0.0, bnd_ref[0]))
        res = _prep_fn(p, prev, mu_ref[...], w0_ref[...], a0_ref[...], kk_ref[...], ka_ref[...], lora_ref[...],
                       seg_ref[...], (tri_ref[0], tri_ref[1]))
        for pair, heads in enumerate(_pack_heads(res[:N_VEC])):
            for h, val in enumerate(heads):
                hm_ref[pair, h] = val
        v_ref[...] = res[N_VEC]
        g_ref[...] = res[N_VEC + 1]
        rk_ref[...] = res[N_VEC + 2]

    small = _full((1, W))
    row = pl.BlockSpec((tm, W), lambda i: (i, 0))
    return pl.pallas_call(
        body, name="rwkv_prep_fwd", grid=(T // tm,),
        in_specs=[pl.BlockSpec((tm, RW_COLS), lambda i: (i, 0)),
                  pl.BlockSpec((1, 1, RW_COLS), lambda i: (jnp.maximum(i - 1, 0), 0, 0)),
                  _full((1, RW_COLS)), small, small, small, small, _full((2 * LORA, 2 * W)), _full((W, W)),
                  _full((2, tm, tm))],
        out_specs=[pl.BlockSpec((N_PAIR, RW_HEADS, tm, 128), lambda i: (0, 0, i, 0)), row, row, row],
        out_shape=[jax.ShapeDtypeStruct((N_PAIR, RW_HEADS, T, 128), F32)] + [jax.ShapeDtypeStruct((T, W), F32)] * 3,
        compiler_params=_cparams(dimension_semantics=("arbitrary",)),
    )(p_rw, bnd, mu, w0, a0, k_k, k_a, lora, seg64, tri)


def _prep_bwd(p_rw, bnd, mu, w0, a0, k_k, k_a, lora, seg64, tri, cts):
    T = p_rw.shape[0]
    tm = _row_tile(T)
    W = RW_WIDTH

    def body(p_ref, bnd_ref, mu_ref, w0_ref, a0_ref, kk_ref, ka_ref, lora_ref, seg_ref, tri_ref,
             dhm_ref, drk_ref, dv1_ref, dv2_ref, dg_ref,
             dp_ref, dprev_ref, dfirst_ref, dmu_ref, dw0_ref, da0_ref, dkk_p_ref, dka_ref, dlora_ref):
        accs = (dmu_ref, dw0_ref, da0_ref, dkk_p_ref, dka_ref, dlora_ref)

        @pl.when(pl.program_id(0) == 0)
        def _():
            for a_ref in accs:
                a_ref[...] = jnp.zeros_like(a_ref)

        p = p_ref[...]
        prev = _shift_down(p, jnp.where(pl.program_id(0) == 0, 0.0, bnd_ref[0]))
        seg, tri = seg_ref[...], (tri_ref[0], tri_ref[1])
        _, vjp = jax.vjp(lambda *a: _prep_fn(*a, seg, tri), p, prev, mu_ref[...], w0_ref[...], a0_ref[...],
                         kk_ref[...], ka_ref[...], lora_ref[...])
        ct = (*_unpack_heads(dhm_ref), dv1_ref[...] + dv2_ref[...], dg_ref[...], drk_ref[...])
        grads = vjp(ct)
        dp_ref[...] = grads[0]
        dprev_ref[...] = grads[1]
        dfirst_ref[0] = grads[1][0:1, :]
        for a_ref, gval in zip(accs, grads[2:]):
            a_ref[...] += gval

    small = _full((1, W))
    row = pl.BlockSpec((tm, W), lambda i: (i, 0))
    return pl.pallas_call(
        body, name="rwkv_prep_bwd", grid=(T // tm,),
        in_specs=[pl.BlockSpec((tm, RW_COLS), lambda i: (i, 0)),
                  pl.BlockSpec((1, 1, RW_COLS), lambda i: (jnp.maximum(i - 1, 0), 0, 0)),
                  _full((1, RW_COLS)), small, small, small, small, _full((2 * LORA, 2 * W)), _full((W, W)),
                  _full((2, tm, tm)), pl.BlockSpec((N_PAIR, RW_HEADS, tm, 128), lambda i: (0, 0, i, 0))] + [row] * 4,
        out_specs=[pl.BlockSpec((tm, RW_COLS), lambda i: (i, 0)), pl.BlockSpec((tm, RW_COLS), lambda i: (i, 0)),
                   pl.BlockSpec((1, 1, RW_COLS), lambda i: (i, 0, 0)),
                   _full((1, RW_COLS)), small, small, small, small, _full((2 * LORA, 2 * W))],
        out_shape=[jax.ShapeDtypeStruct((T, RW_COLS), F32), jax.ShapeDtypeStruct((T, RW_COLS), F32),
                   jax.ShapeDtypeStruct((T // tm, 1, RW_COLS), F32),
                   jax.ShapeDtypeStruct((1, RW_COLS), F32)] + [jax.ShapeDtypeStruct((1, W), F32)] * 4
                  + [jax.ShapeDtypeStruct((2 * LORA, 2 * W), F32)],
        compiler_params=_cparams(dimension_semantics=("arbitrary",)),
    )(p_rw, bnd, mu, w0, a0, k_k, k_a, lora, seg64, tri, *cts)


def _post_fwd(o, rk, v, g, gn_g, gn_b, r_k, seg64):
    T = o.shape[0]
    tm = _row_tile(T)
    W = RW_WIDTH

    def body(o_ref, rk_ref, v_ref, g_ref, gg_ref, gb_ref, rkp_ref, seg_ref, y_ref):
        y_ref[...] = _post_fn(o_ref[...], rk_ref[...], v_ref[...], g_ref[...], gg_ref[...], gb_ref[...],
                              rkp_ref[...], seg_ref[...]).astype(BF16)

    row = pl.BlockSpec((tm, W), lambda i: (i, 0))
    small = _full((1, W))
    return pl.pallas_call(
        body, name="rwkv_post_fwd", grid=(T // tm,),
        in_specs=[row] * 4 + [small] * 3 + [_full((W, W))],
        out_specs=row, out_shape=jax.ShapeDtypeStruct((T, W), BF16),
        compiler_params=_cparams(dimension_semantics=("arbitrary",)),
    )(o, rk, v, g, gn_g, gn_b, r_k, seg64)


def _post_bwd(o, rk, v, g, gn_g, gn_b, r_k, seg64, dy):
    T = o.shape[0]
    tm = _row_tile(T)
    W = RW_WIDTH

    def body(o_ref, rk_ref, v_ref, g_ref, gg_ref, gb_ref, rkp_ref, seg_ref, dy_ref,
             do_ref, drk_ref, dv_ref, dg_ref, dgg_ref, dgb_ref, drkp_ref):
        accs = (dgg_ref, dgb_ref, drkp_ref)

        @pl.when(pl.program_id(0) == 0)
        def _():
            for a_ref in accs:
                a_ref[...] = jnp.zeros_like(a_ref)

        seg = seg_ref[...]
        _, vjp = jax.vjp(lambda *a: _post_fn(*a, seg), o_ref[...], rk_ref[...], v_ref[...], g_ref[...],
                         gg_ref[...], gb_ref[...], rkp_ref[...])
        grads = vjp(dy_ref[...])
        for o_, gval in zip((do_ref, drk_ref, dv_ref, dg_ref), grads[:4]):
            o_[...] = gval
        for a_ref, gval in zip(accs, grads[4:]):
            a_ref[...] += gval

    row = pl.BlockSpec((tm, W), lambda i: (i, 0))
    small = _full((1, W))
    return pl.pallas_call(
        body, name="rwkv_post_bwd", grid=(T // tm,),
        in_specs=[row] * 4 + [small] * 3 + [_full((W, W)), pl.BlockSpec((tm, W), lambda i: (i, 1))],
        out_specs=[row] * 4 + [small] * 3,
        out_shape=[jax.ShapeDtypeStruct((T, W), F32)] * 4 + [jax.ShapeDtypeStruct((1, W), F32)] * 3,
        compiler_params=_cparams(dimension_semantics=("arbitrary",)),
    )(o, rk, v, g, gn_g, gn_b, r_k, seg64, dy)


def _wkv_lhs(hm_ref, rows):
    tiles = [jnp.transpose(hm_ref[p, :, rows, :].reshape(RW_HEADS * WKV_CHUNK, 128)) for p in range(N_PAIR)]
    hi, lo = _split(jnp.concatenate(tiles, axis=0)[:N_VEC * RW_HEAD])
    return jnp.concatenate([hi, lo], axis=1)


def _wkv_group(nch):
    return min(WKV_GROUP, nch)


N_STEP_VEC = N_VEC - 1
PAD_ROWS = 16


def _wkv_fwd(cols, v, e_tab):
    T = v.shape[0]
    Tc = WKV_CHUNK
    nch = T // Tc
    G = _wkv_group(nch)
    J, W = RW_HEAD, RW_WIDTH
    JS = N_STEP_VEC * J

    def body(cols_ref, v_ref, e_ref, o_ref, states_ref, sa_ref, s_ref):
        @pl.when(pl.program_id(0) == 0)
        def _():
            s_ref[...] = jnp.zeros_like(s_ref)

        st = s_ref[...]
        for c in range(G):
            lhs = _wkv_lhs(cols_ref, slice(c * Tc, (c + 1) * Tc))
            for t in range(Tc):
                row = slice(c * Tc + t, c * Tc + t + 1)
                ex = _dot(lhs[:JS], e_ref[t])
                states_ref[c * Tc + t] = st
                sa = -jnp.sum(st * ex[0:J], axis=0, keepdims=True)
                st = st + ex[J:2 * J] * sa + ex[2 * J:3 * J] * v_ref[row, :]
                sa_ref[row, :] = sa
                o_ref[row, :] = jnp.sum(st * ex[3 * J:4 * J], axis=0, keepdims=True)
            st = st * _dot(lhs[JS:], e_ref[Tc - 1])
        s_ref[...] = st

    GT = G * Tc
    return pl.pallas_call(
        body, name="wkv_fwd", grid=(nch // G,),
        in_specs=[pl.BlockSpec((N_PAIR, RW_HEADS, GT, 128), lambda c: (0, 0, c, 0)),
                  pl.BlockSpec((GT, W), lambda c: (c, 0)), _full((Tc, 2 * 128, W))],
        out_specs=[pl.BlockSpec((GT, W), lambda c: (c, 0)), pl.BlockSpec((GT, J, W), lambda c: (c, 0, 0)),
                   pl.BlockSpec((GT, W), lambda c: (c, 0))],
        out_shape=[jax.ShapeDtypeStruct((T, W), F32), jax.ShapeDtypeStruct((T, J, W), F32),
                   jax.ShapeDtypeStruct((T, W), F32)],
        scratch_shapes=[pltpu.VMEM((J, W), F32)],
        compiler_params=_cparams(dimension_semantics=("arbitrary",)),
    )(cols, v, e_tab)


def _wkv_bwd(cols, v, do, states, sa, e_tab, r_tab):
    T = v.shape[0]
    Tc = WKV_CHUNK
    nch = T // Tc
    G = _wkv_group(nch)
    ngr = nch // G
    J, W = RW_HEAD, RW_WIDTH
    JS = N_STEP_VEC * J
    blocks = [slice(b * 128, (b + 1) * 128) for b in range(W // 128)]

    def body(cols_ref, v_ref, do_ref, states_ref, sa_ref, e_ref, r_ref, dv_ref, dhm_ref, ds_ref):
        @pl.when(pl.program_id(0) == 0)
        def _():
            ds_ref[...] = jnp.zeros_like(ds_ref)

        d_carry = [ds_ref[:, b] for b in blocks]
        last = Tc - 1
        for c in reversed(range(G)):
            at = c * Tc
            rows = slice(at, at + Tc)
            lhs = _wkv_lhs(cols_ref, rows)
            ex = _dot(lhs, e_ref[last])
            dst, ends = [], []
            for i, b in enumerate(blocks):
                s_end = (states_ref[at + last, :, b] + ex[J:2 * J, b] * sa_ref[at + last:at + Tc, b]
                         + ex[2 * J:3 * J, b] * v_ref[at + last:at + Tc, b])
                ends.append((d_carry[i] * s_end).astype(BF16))
                dst.append(d_carry[i] * ex[JS:, b])
            d_decay = _dot(jnp.concatenate(ends, axis=1), r_ref[last])
            acc = jnp.zeros((JS + PAD_ROWS, 128), F32)
            for t in reversed(range(Tc)):
                row = slice(at + t, at + t + 1)
                if t != last:
                    ex = _dot(lhs[:JS], e_ref[t])
                dvs, prods = [], []
                for i, b in enumerate(blocks):
                    kk_e, b_e, k_e, r_e = (ex[n * J:(n + 1) * J, b] for n in range(N_STEP_VEC))
                    do_row, v_row, sa_row = do_ref[row, b], v_ref[row, b], sa_ref[row, b]
                    s_old = states_ref[at + t, :, b]
                    s_new = states_ref[at + t + 1, :, b] if t != last else s_old + b_e * sa_row + k_e * v_row
                    dsn = dst[i] + r_e * do_row
                    dsa = jnp.sum(dsn * b_e, axis=0, keepdims=True)
                    dvs.append(jnp.sum(dsn * k_e, axis=0, keepdims=True))
                    prods.append(jnp.concatenate(
                        [s_old * (-dsa), dsn * sa_row, dsn * v_row, s_new * do_row, jnp.zeros((PAD_ROWS, 128), F32)],
                        axis=0).astype(BF16))
                    dst[i] = dsn - kk_e * dsa
                dv_ref[row, :] = jnp.concatenate(dvs, axis=1)
                acc = acc + _dot(jnp.concatenate(prods, axis=1), r_ref[t])
            d_carry = dst
            tiles = jnp.concatenate([acc[:JS], d_decay, jnp.zeros((2 * N_PAIR * J - N_VEC * J, 128), F32)], axis=0)
            for p in range(N_PAIR):
                dhm_ref[p, :, rows, :] = jnp.transpose(tiles[p * 128:(p + 1) * 128]).reshape(RW_HEADS, Tc, 128)
        for i, b in enumerate(blocks):
            ds_ref[:, b] = d_carry[i]

    GT = G * Tc
    rev2 = lambda c: (ngr - 1 - c, 0)
    rev3 = lambda c: (ngr - 1 - c, 0, 0)
    rev_hm = lambda c: (0, 0, ngr - 1 - c, 0)
    hm_spec = pl.BlockSpec((N_PAIR, RW_HEADS, GT, 128), rev_hm)
    return pl.pallas_call(
        body, name="wkv_bwd", grid=(ngr,),
        in_specs=[hm_spec, pl.BlockSpec((GT, W), rev2), pl.BlockSpec((GT, W), rev2),
                  pl.BlockSpec((GT, J, W), rev3), pl.BlockSpec((GT, W), rev2),
                  _full((Tc, 2 * 128, W)), _full((Tc, W, 128))],
        out_specs=[pl.BlockSpec((GT, W), rev2), hm_spec],
        out_shape=[jax.ShapeDtypeStruct((T, W), F32), jax.ShapeDtypeStruct((N_PAIR, RW_HEADS, T, 128), F32)],
        scratch_shapes=[pltpu.VMEM((J, W), F32)],
        compiler_params=_cparams(dimension_semantics=("arbitrary",)),
    )(cols, v, do, states, sa, e_tab, r_tab)


def _outproj(x, y_ret, y_rw, w_out_b, target, gf):
    T = x.shape[0]
    tm = _row_tile(T)
    W = RW_WIDTH

    def body(x_ref, yr_ref, yw_ref, w_ref, t_ref, gf_ref, loss_ref, dh_ref, dy_ref, dw_ref, dgf_ref):
        @pl.when(pl.program_id(0) == 0)
        def _():
            loss_ref[...] = jnp.zeros_like(loss_ref)
            dw_ref[...] = jnp.zeros_like(dw_ref)
            dgf_ref[...] = jnp.zeros_like(dgf_ref)

        y = jnp.concatenate([yr_ref[...], yw_ref[...]], axis=1)
        w = w_ref[...]
        h = x_ref[...] + _dot(y, w)
        rstd = lax.rsqrt(jnp.mean(h * h, axis=-1, keepdims=True) + RMS_EPS)
        hn = h * rstd
        gfv = gf_ref[...]
        err = hn * gfv - t_ref[...]
        loss_ref[...] += 0.5 * jnp.sum(jnp.mean(err * err, axis=-1))
        dout = err * (1.0 / D_MODEL)
        dgf_ref[...] += jnp.sum(dout * hn, axis=0, keepdims=True)
        dhn = dout * gfv
        dh = rstd * (dhn - hn * jnp.mean(dhn * hn, axis=-1, keepdims=True))
        dh_ref[...] = dh
        dhb = dh.astype(BF16)
        dy_ref[...] = _dot_nt(dhb, w)
        dw_ref[...] += _dot_tn(y, dhb)

    return pl.pallas_call(
        body, name="outproj_loss", grid=(T // tm,),
        in_specs=[pl.BlockSpec((tm, D_MODEL), lambda i: (i, 0)), pl.BlockSpec((tm, W), lambda i: (i, 0)),
                  pl.BlockSpec((tm, W), lambda i: (i, 0)), _full((D_MODEL, D_MODEL)),
                  pl.BlockSpec((tm, D_MODEL), lambda i: (i, 0)), _full((1, D_MODEL))],
        out_specs=[_full((1, PACK_W)), pl.BlockSpec((tm, D_MODEL), lambda i: (i, 0)),
                   pl.BlockSpec((tm, D_MODEL), lambda i: (i, 0)), _full((D_MODEL, D_MODEL)), _full((1, D_MODEL))],
        out_shape=[jax.ShapeDtypeStruct((1, PACK_W), F32), jax.ShapeDtypeStruct((T, D_MODEL), F32),
                   jax.ShapeDtypeStruct((T, D_MODEL), F32), jax.ShapeDtypeStruct((D_MODEL, D_MODEL), F32),
                   jax.ShapeDtypeStruct((1, D_MODEL), F32)],
        compiler_params=_cparams(dimension_semantics=("arbitrary",)),
    )(x, y_ret, y_rw, w_out_b, target, gf)


def _inproj_bwd_x(dp_ret, dp_rw, dprev, dfirst, w_t, x, norm_g, dh):
    T = x.shape[0]
    tm = _row_tile(T)
    nt = T // tm

    def body(dpr_ref, dpw_ref, dprev_ref, dnext_ref, w_ref, x_ref, g_ref, dh_ref, gx_ref, dg_ref, dpt_ref):
        @pl.when(pl.program_id(0) == 0)
        def _():
            dg_ref[...] = jnp.zeros_like(dg_ref)

        next_row = jnp.where(pl.program_id(0) == nt - 1, 0.0, dnext_ref[0])
        dp = jnp.concatenate([dpr_ref[...], dpw_ref[...] + _shift_up(dprev_ref[...], next_row)], axis=1)
        dpt_ref[...] = jnp.transpose(dp).astype(BF16)
        du = _dot(dp.astype(BF16), w_ref[...])
        xf = x_ref[...]
        rstd = lax.rsqrt(jnp.mean(xf * xf, axis=-1, keepdims=True) + RMS_EPS)
        xn = xf * rstd
        dg_ref[...] += jnp.sum(du * xn, axis=0, keepdims=True)
        dxn = du * g_ref[...]
        gx_ref[...] = dh_ref[...] + rstd * (dxn - xn * jnp.mean(dxn * xn, axis=-1, keepdims=True))

    return pl.pallas_call(
        body, name="inproj_bwd_x", grid=(T // tm,),
        in_specs=[pl.BlockSpec((tm, RET_COLS), lambda i: (i, 0)), pl.BlockSpec((tm, RW_COLS), lambda i: (i, 0)),
                  pl.BlockSpec((tm, RW_COLS), lambda i: (i, 0)),
                  pl.BlockSpec((1, 1, RW_COLS), lambda i: (jnp.minimum(i + 1, nt - 1), 0, 0)),
                  _full((IN_COLS, D_MODEL)), pl.BlockSpec((tm, D_MODEL), lambda i: (i, 0)), _full((1, D_MODEL)),
                  pl.BlockSpec((tm, D_MODEL), lambda i: (i, 0))],
        out_specs=[pl.BlockSpec((tm, D_MODEL), lambda i: (i, 0)), _full((1, D_MODEL)),
                   pl.BlockSpec((IN_COLS, tm), lambda i: (0, i))],
        out_shape=[jax.ShapeDtypeStruct((T, D_MODEL), F32), jax.ShapeDtypeStruct((1, D_MODEL), F32),
                   jax.ShapeDtypeStruct((IN_COLS, T), BF16)],
        compiler_params=_cparams(dimension_semantics=("arbitrary",)),
    )(dp_ret, dp_rw, dprev, dfirst, w_t, x, norm_g, dh)


def _inproj_bwd_w(dp_t, u):
    T = u.shape[0]

    def body(d_ref, u_ref, o_ref):
        o_ref[...] = _dot(d_ref[...], u_ref[...])

    return pl.pallas_call(
        body, name="inproj_bwd_w", grid=(N_CHIPS,),
        in_specs=[pl.BlockSpec((IN_SHARD, T), lambda i: (i, 0)), _full((T, D_MODEL))],
        out_specs=pl.BlockSpec((IN_SHARD, D_MODEL), lambda i: (i, 0)),
        out_shape=jax.ShapeDtypeStruct((IN_COLS, D_MODEL), F32),
        compiler_params=_cparams(dimension_semantics=("arbitrary",)),
    )(dp_t, u)


def _with_own(gathered, own, chip):
    return lax.dynamic_update_slice(gathered, own[None], (chip, 0, 0, 0))


def _local_step(x, target, w_in_t, late, chip, small):
    T = x.shape[0]
    tm = _row_tile(T)
    W = RW_WIDTH
    cos, sin = _rope_tables(T)
    tabs = _ret_tables()
    seg64 = _seg_matrix(RW_WIDTH, RW_HEAD)
    e_tab = _wkv_expand_table()
    r_tab = _wkv_reduce_table()

    p_ret, p_rw, u, bnd, *gathered = _inproj_fwd(x, small["norm_g"], w_in_t, late)
    g_out, g_lw, g_la = (_with_own(g, own, chip) for g, own in zip(gathered, late))
    w_out_b = g_out.reshape(D_MODEL, D_MODEL)
    lw = jnp.transpose(g_lw.reshape(N_CHIPS, LORA, LORA_SHARD), (1, 0, 2)).reshape(LORA, W)
    la = jnp.transpose(g_la.reshape(N_CHIPS, LORA, LORA_SHARD), (1, 0, 2)).reshape(LORA, W)
    zero = jnp.zeros((LORA, W), F32)
    lora = jnp.concatenate([jnp.concatenate([lw, zero], axis=1), jnp.concatenate([zero, la], axis=1)], axis=0)
    prep_w = (small["rwkv_mu"], small["w0"], small["a0"], small["k_k"], small["k_a"], lora, seg64, _chunk_tables(tm))
    post_w = (small["rwkv_gn_g"], small["rwkv_gn_b"], small["r_k"], seg64)

    y_ret, ret, s_in_all = _ret_fwd(p_ret, cos, sin, tabs, small["ret_gn_g"])
    hm, v, g, rk = _prep_fwd(p_rw, bnd, *prep_w)
    o, states, sa = _wkv_fwd(hm, v, e_tab)
    y_rw = _post_fwd(o, rk, v, g, *post_w)
    loss, dh, dy, d_w_out, d_gf = _outproj(x, y_ret, y_rw, w_out_b, target, small["final_norm_g"])

    do, d_rk, dv2, dg, d_gn_g, d_gn_b, d_r_k = _post_bwd(o, rk, v, g, *post_w, dy)
    dv1, d_hm = _wkv_bwd(hm, v, do, states, sa, e_tab, r_tab)
    dp_rw, dprev, dfirst, d_mu, d_w0, d_a0, d_k_k, d_k_a, d_lora = _prep_bwd(
        p_rw, bnd, *prep_w, (d_hm, d_rk, dv1, dv2, dg))
    dp_ret, d_ret_gn = _ret_bwd(p_ret, cos, sin, tabs, small["ret_gn_g"], ret, s_in_all, dy)
    grad_x, d_norm_g, dp_t = _inproj_bwd_x(dp_ret, dp_rw, dprev, dfirst, w_in_t, x, small["norm_g"], dh)
    d_w_in = _inproj_bwd_w(dp_t, u)

    d_small = {"norm_g": d_norm_g, "ret_gn_g": d_ret_gn, "rwkv_mu": d_mu, "w0": d_w0, "a0": d_a0, "k_k": d_k_k,
               "k_a": d_k_a, "r_k": d_r_k, "rwkv_gn_g": d_gn_g, "rwkv_gn_b": d_gn_b, "final_norm_g": d_gf}
    return loss, grad_x, d_w_in, d_w_out, d_lora, d_small


ANY = pl.BlockSpec(memory_space=pl.ANY)
CHIP_FLIPS = ((0, 1), (1, 0), (1, 1))
N_FLIPS = len(CHIP_FLIPS)
LORA_SHARD = RW_WIDTH // N_CHIPS
HALF_IN = IN_SHARD // 2
HALF_OUT = OUT_SHARD // 2


def _position():
    return lax.axis_index("x"), lax.axis_index("y"), lax.axis_index("c")


def _flip(v, f):
    return 1 - v if f else v


def _finish(local, remote, landed):
    for cp in landed:
        cp.wait_recv()
    for cp in remote:
        cp.wait_send()
    for cp in local:
        cp.wait()


def _gather_copies(ins, outs, sems):
    send, recv, pass_send, pass_recv = sems
    x, y, c = _position()
    s = 2 * x + y
    sibling = (x, y, 1 - c)
    first, second = (1 - c, c), (c, 1 - c)
    (x1, y1), (x2, y2) = ((x + fx - 2 * x * fx, y + fy - 2 * y * fy) for fx, fy in (first, second))
    s1, s2, sd = 2 * x1 + y1, 2 * x2 + y2, 2 * (1 - x) + (1 - y)

    def copy(src, dst, pair, k, to):
        return pltpu.make_async_remote_copy(src_ref=src, dst_ref=dst, send_sem=pair[0].at[k], recv_sem=pair[1].at[k],
                                            device_id=to, device_id_type=MESH)

    at_once, direct, relayed, passed_in, sends = [], [], [], [], []
    for a in range(len(ins)):
        k = a * N_FLIPS
        own, out = ins[a].at[c], outs[a]
        ici = (send, recv)
        to_first = copy(own, out.at[s, c], ici, k, (x1, y1, c))
        to_second = copy(own, out.at[s, c], ici, k + 1, (x2, y2, c))
        relay = copy(out.at[s1, c], out.at[s1, c], ici, k + 2, (x2, y2, c))
        passes = [copy(out.at[slot, c], out.at[slot, c], (pass_send, pass_recv), k + j, sibling)
                  for j, slot in enumerate((s1, s2, sd))]
        at_once += [to_first, to_second]
        direct += [(copy(own, out.at[s1, c], ici, k, (x1, y1, c)), [relay, passes[0]]),
                   (copy(own, out.at[s2, c], ici, k + 1, (x2, y2, c)), [passes[1]])]
        relayed += [(copy(own, out.at[sd, c], ici, k + 2, (x2, y2, c)), [passes[2]])]
        passed_in += [copy(out.at[slot, 1 - c], out.at[slot, 1 - c], (pass_send, pass_recv), k + j, sibling)
                      for j, slot in enumerate((s2, s1, sd))]
        sends += [to_first, to_second, relay] + passes
    return at_once, direct, relayed, passed_in, sends


def _then(steps):
    for arrived, then in steps:
        arrived.wait_recv()
        for cp in then:
            cp.start()


def _gather_start(copies):
    for cp in copies[0]:
        cp.start()


def _gather_relay(copies):
    _then(copies[1])


def _gather_finish(copies):
    _, _, relayed, passed_in, sends = copies
    _then(relayed)
    _finish([], sends, passed_in)


def _gather_sems(n):
    return [pltpu.SemaphoreType.DMA((n * N_FLIPS,))] * 4


def _gather_shapes(arrs):
    return [jax.ShapeDtypeStruct((N_CHIPS,) + a.shape, a.dtype) for a in arrs]


def _gather_chips(arrs):
    n = len(arrs)

    def body(*refs):
        copies = _gather_copies(refs[:n], refs[n:2 * n], refs[2 * n:])
        _gather_start(copies)
        _gather_relay(copies)
        _gather_finish(copies)

    return pl.pallas_call(
        body, name="gather_weights",
        in_specs=[ANY] * n, out_specs=[ANY] * n,
        out_shape=_gather_shapes(arrs), scratch_shapes=_gather_sems(n),
    )(*arrs)


def _pair_exchange(g_in, g_out, g_small):
    def body(gi_ref, go_ref, gs_ref, li_ref, lo_ref, ls_ref, send, recv):
        x, y, c = _position()
        peer = (x, y, 1 - c)
        srcs = (gi_ref.at[:, pl.ds((1 - c) * HALF_IN, HALF_IN), :], go_ref.at[:, pl.ds((1 - c) * HALF_OUT, HALF_OUT), :],
                gs_ref)
        remote = [pltpu.make_async_remote_copy(src_ref=src, dst_ref=dst, send_sem=send.at[k], recv_sem=recv.at[k],
                                               device_id=peer, device_id_type=MESH)
                  for k, (src, dst) in enumerate(zip(srcs, (li_ref, lo_ref, ls_ref)))]
        for cp in remote:
            cp.start()
        _finish([], remote, remote)

    return pl.pallas_call(
        body, name="pair_exchange",
        in_specs=[ANY] * 3, out_specs=[ANY] * 3,
        out_shape=[jax.ShapeDtypeStruct((N_CHIPS, HALF_IN, D_MODEL), F32),
                   jax.ShapeDtypeStruct((N_CHIPS, HALF_OUT, D_MODEL), F32),
                   jax.ShapeDtypeStruct(g_small.shape, F32)],
        scratch_shapes=[pltpu.SemaphoreType.DMA((3,)), pltpu.SemaphoreType.DMA((3,))],
    )(g_in, g_out, g_small)


def _pair_sum(g_in, g_out, g_small, l_in, l_out, l_small, c_arr):
    def body(c_ref, gi_ref, go_ref, gs_ref, li_ref, lo_ref, ls_ref, ci_ref, co_ref, cs_ref):
        ci_ref[...] = (gi_ref[...] + li_ref[...]).astype(BF16)
        co_ref[...] = (go_ref[...] + lo_ref[...]).astype(BF16)

        @pl.when(pl.program_id(0) == 0)
        def _():
            cs_ref[...] = gs_ref[...] + ls_ref[...]

    nd = g_small.shape
    return pl.pallas_call(
        body, name="pair_sum",
        grid_spec=pltpu.PrefetchScalarGridSpec(
            num_scalar_prefetch=1, grid=(N_CHIPS,),
            in_specs=[pl.BlockSpec((1, HALF_IN, D_MODEL), lambda s, c: (s, c[0], 0)),
                      pl.BlockSpec((1, HALF_OUT, D_MODEL), lambda s, c: (s, c[0], 0)),
                      pl.BlockSpec(nd, lambda s, c: (0, 0)),
                      pl.BlockSpec((1, HALF_IN, D_MODEL), lambda s, c: (s, 0, 0)),
                      pl.BlockSpec((1, HALF_OUT, D_MODEL), lambda s, c: (s, 0, 0)),
                      pl.BlockSpec(nd, lambda s, c: (0, 0))],
            out_specs=[pl.BlockSpec((1, HALF_IN, D_MODEL), lambda s, c: (s, 0, 0)),
                       pl.BlockSpec((1, HALF_OUT, D_MODEL), lambda s, c: (s, 0, 0)),
                       pl.BlockSpec(nd, lambda s, c: (0, 0))]),
        out_shape=[jax.ShapeDtypeStruct((N_CHIPS, HALF_IN, D_MODEL), BF16),
                   jax.ShapeDtypeStruct((N_CHIPS, HALF_OUT, D_MODEL), BF16), jax.ShapeDtypeStruct(nd, F32)],
        compiler_params=_cparams(dimension_semantics=("arbitrary",)),
    )(c_arr, g_in, g_out, g_small, l_in, l_out, l_small)


def _chip_exchange(c_in, c_out, c_small):
    def body(ci_ref, co_ref, cs_ref, li_ref, lo_ref, ls_ref, send, recv):
        x, y, c = _position()
        s = 2 * x + y
        remote = []
        for j, (fx, fy) in enumerate(CHIP_FLIPS):
            px, py = _flip(x, fx), _flip(y, fy)
            ps = 2 * px + py
            for a, (src, dst) in enumerate(((ci_ref.at[ps], li_ref.at[j]), (co_ref.at[ps], lo_ref.at[j]),
                                            (cs_ref, ls_ref.at[j]))):
                k = 3 * j + a
                remote.append(pltpu.make_async_remote_copy(src_ref=src, dst_ref=dst, send_sem=send.at[k],
                                                           recv_sem=recv.at[k], device_id=(px, py, c),
                                                           device_id_type=MESH))
        for cp in remote:
            cp.start()
        _finish([], remote, remote)

    return pl.pallas_call(
        body, name="chip_exchange",
        in_specs=[ANY] * 3, out_specs=[ANY] * 3,
        out_shape=[jax.ShapeDtypeStruct((N_FLIPS, HALF_IN, D_MODEL), c_in.dtype),
                   jax.ShapeDtypeStruct((N_FLIPS, HALF_OUT, D_MODEL), c_out.dtype),
                   jax.ShapeDtypeStruct((N_FLIPS,) + c_small.shape, F32)],
        scratch_shapes=[pltpu.SemaphoreType.DMA((3 * N_FLIPS,)), pltpu.SemaphoreType.DMA((3 * N_FLIPS,))],
    )(c_in, c_out, c_small)


def _chip_sum(g_in, g_out, p_in, p_out, c_small, l_in, l_out, l_small, sc_arr):
    nd = c_small.shape

    def body(s_ref, gi_ref, go_ref, pi_ref, po_ref, cs_ref, li0, li1, li2, lo0, lo1, lo2, ls_ref,
             ri_ref, ro_ref, rs_ref):
        ri_ref[...] = (((gi_ref[0] + pi_ref[0]) + li0[0].astype(F32)) + li1[0].astype(F32)) + li2[0].astype(F32)
        ro_ref[...] = (((go_ref[0] + po_ref[0]) + lo0[0].astype(F32)) + lo1[0].astype(F32)) + lo2[0].astype(F32)
        me = s_ref[0]
        parts = (cs_ref[...], ls_ref[0], ls_ref[1], ls_ref[2])

        def of_chip(s):
            m = jnp.bitwise_xor(me, s)
            return jnp.where(m == 0, parts[0], jnp.where(m == 1, parts[1], jnp.where(m == 2, parts[2], parts[3])))

        rs_ref[...] = ((of_chip(0) + of_chip(1)) + of_chip(2)) + of_chip(3)

    def flip_in(j):
        return pl.BlockSpec((1, HALF_IN, D_MODEL), lambda i, s: (j, 0, 0))

    def flip_out(j):
        return pl.BlockSpec((1, HALF_OUT, D_MODEL), lambda i, s: (j, 0, 0))

    return pl.pallas_call(
        body, name="chip_sum",
        grid_spec=pltpu.PrefetchScalarGridSpec(
            num_scalar_prefetch=1, grid=(1,),
            in_specs=[pl.BlockSpec((1, HALF_IN, D_MODEL), lambda i, s: (s[0], s[1], 0)),
                      pl.BlockSpec((1, HALF_OUT, D_MODEL), lambda i, s: (s[0], s[1], 0)),
                      pl.BlockSpec((1, HALF_IN, D_MODEL), lambda i, s: (s[0], 0, 0)),
                      pl.BlockSpec((1, HALF_OUT, D_MODEL), lambda i, s: (s[0], 0, 0)),
                      pl.BlockSpec(nd, lambda i, s: (0, 0)),
                      flip_in(0), flip_in(1), flip_in(2), flip_out(0), flip_out(1), flip_out(2),
                      pl.BlockSpec((N_FLIPS,) + nd, lambda i, s: (0, 0, 0))],
            out_specs=[pl.BlockSpec((HALF_IN, D_MODEL), lambda i, s: (0, 0)),
                       pl.BlockSpec((HALF_OUT, D_MODEL), lambda i, s: (0, 0)),
                       pl.BlockSpec(nd, lambda i, s: (0, 0))]),
        out_shape=[jax.ShapeDtypeStruct((HALF_IN, D_MODEL), F32), jax.ShapeDtypeStruct((HALF_OUT, D_MODEL), F32),
                   jax.ShapeDtypeStruct(nd, F32)],
        compiler_params=_cparams(dimension_semantics=("arbitrary",)),
    )(sc_arr, g_in, g_out, p_in, p_out, c_small, l_in, l_in, l_in, l_out, l_out, l_out, l_small)


def _pair_share(r_in, r_out):
    def body(ri_ref, ro_ref, li_ref, lo_ref, send, recv):
        x, y, c = _position()
        remote = [pltpu.make_async_remote_copy(src_ref=src, dst_ref=dst, send_sem=send.at[k], recv_sem=recv.at[k],
                                               device_id=(x, y, 1 - c), device_id_type=MESH)
                  for k, (src, dst) in enumerate(((ri_ref, li_ref), (ro_ref, lo_ref)))]
        for cp in remote:
            cp.start()
        _finish([], remote, remote)

    return pl.pallas_call(
        body, name="pair_share",
        in_specs=[ANY] * 2, out_specs=[ANY] * 2,
        out_shape=[jax.ShapeDtypeStruct(r_in.shape, F32), jax.ShapeDtypeStruct(r_out.shape, F32)],
        scratch_shapes=[pltpu.SemaphoreType.DMA((2,)), pltpu.SemaphoreType.DMA((2,))],
    )(r_in, r_out)


def _adam_update(w, g, m, v):
    mn = ADAM_B1 * m + (1.0 - ADAM_B1) * g
    vn = ADAM_B2 * v + (1.0 - ADAM_B2) * jnp.square(g)
    m_hat = mn / (1.0 - ADAM_B1 ** ADAM_STEP)
    v_hat = vn / (1.0 - ADAM_B2 ** ADAM_STEP)
    return -ADAM_LR * (m_hat / (jnp.sqrt(v_hat) + ADAM_EPS) + ADAM_WD * w), mn, vn


def _adamw(name, w, g_mine, g_theirs, m, v, core_arr, tr):
    rows, cols = w.shape
    per_half = rows // 2 // tr

    def body(c_ref, w_ref, gm_ref, gt_ref, m_ref, v_ref, g_ref, d_ref, nm_ref, nv_ref):
        mine = (pl.program_id(0) // per_half) == c_ref[0]
        g = jnp.where(mine, gm_ref[...], gt_ref[...])
        d, mn, vn = _adam_update(w_ref[...], g, m_ref[...], v_ref[...])
        g_ref[...] = g
        d_ref[...] = d
        nm_ref[...] = mn
        nv_ref[...] = vn

    spec = pl.BlockSpec((tr, cols), lambda i, c: (i, 0))
    half = pl.BlockSpec((tr, cols), lambda i, c: (i % per_half, 0))
    return pl.pallas_call(
        body, name=name,
        grid_spec=pltpu.PrefetchScalarGridSpec(
            num_scalar_prefetch=1, grid=(rows // tr,),
            in_specs=[spec, half, half, spec, spec], out_specs=[spec] * 4),
        out_shape=[jax.ShapeDtypeStruct((rows, cols), F32)] * 4,
        compiler_params=_cparams(dimension_semantics=("arbitrary",)),
    )(core_arr, w, g_mine, g_theirs, m, v)


def _row_pieces(n):
    return [(k, k * PACK_W, min(PACK_W, n - k * PACK_W)) for k in range(-(-n // PACK_W))]


def _pack_small(d_small, loss, d_lora):
    ns = len(SMALL_NAMES)

    def body(*refs):
        small_refs, (loss_ref, lora_ref, out_ref) = refs[:ns], refs[ns:]
        out_ref[...] = jnp.zeros_like(out_ref)
        out_ref[PACK_LORA_W:PACK_LORA_W + LORA, :] = lora_ref[:LORA, :RW_WIDTH]
        out_ref[PACK_LORA_A:PACK_LORA_A + LORA, :] = lora_ref[LORA:, RW_WIDTH:]
        for name, n, ref in zip(SMALL_NAMES, SMALL_SIZES, small_refs):
            for k, at, w in _row_pieces(n):
                out_ref[PACK_AT[name] + k:PACK_AT[name] + k + 1, 0:w] = ref[:, at:at + w]
        out_ref[PACK_LOSS:PACK_LOSS + 1, :] = loss_ref[...]

    return pl.pallas_call(body, name="pack_small", out_shape=jax.ShapeDtypeStruct((PACK_ROWS, PACK_W), F32),
                          compiler_params=_cparams())(*d_small, loss, d_lora)


def _adamw_small(tot, chip_arr, ws, ms, vs):
    ns = len(SMALL_NAMES)
    n_par = ns + 2

    def body(s_ref, tot_ref, glw_ref, gla_ref, *refs):
        w_refs, m_refs, v_refs = refs[:n_par], refs[n_par:2 * n_par], refs[2 * n_par:3 * n_par]
        outs = refs[3 * n_par:]
        g_refs, d_refs, nm_refs, nv_refs = (outs[i * n_par:(i + 1) * n_par] for i in range(4))
        grads = [jnp.concatenate([tot_ref[PACK_AT[name] + k:PACK_AT[name] + k + 1, 0:w] for k, _, w in _row_pieces(n)],
                                 axis=1) for name, n in zip(SMALL_NAMES, SMALL_SIZES)]
        grads += [glw_ref[...], gla_ref[...]]
        for i, g in enumerate(grads):
            d, mn, vn = _adam_update(w_refs[i][...], g, m_refs[i][...], v_refs[i][...])
            g_refs[i][...] = g
            d_refs[i][...] = d
            nm_refs[i][...] = mn
            nv_refs[i][...] = vn

    def whole(a):
        nd = a.ndim
        return pl.BlockSpec(a.shape, lambda i, s: (0,) * nd)

    shard = (LORA, LORA_SHARD)
    par_specs = [whole(a) for a in ws]
    res = pl.pallas_call(
        body, name="adamw_small",
        grid_spec=pltpu.PrefetchScalarGridSpec(
            num_scalar_prefetch=1, grid=(1,),
            in_specs=[whole(tot), pl.BlockSpec(shard, lambda i, s: (PACK_LORA_W // LORA, s[0])),
                      pl.BlockSpec(shard, lambda i, s: (PACK_LORA_A // LORA, s[0]))] + par_specs * 3,
            out_specs=par_specs * 4),
        out_shape=[jax.ShapeDtypeStruct(a.shape, F32) for a in ws] * 4,
        compiler_params=_cparams(dimension_semantics=("arbitrary",)),
    )(chip_arr, tot, tot, tot, *ws, *ms, *vs)
    return [res[i * n_par:(i + 1) * n_par] for i in range(4)]


def kernel(x, norm_g, w_in, ret_gn_g, rwkv_mu, w_lora_up, w0, a_lora_up, a0, k_k, k_a, r_k, rwkv_gn_g, rwkv_gn_b, w_out, final_norm_g, loss_target, m_norm_g, m_w_in, m_ret_gn_g, m_rwkv_mu, m_w_lora_up, m_w0, m_a_lora_up, m_a0, m_k_k, m_k_a, m_r_k, m_rwkv_gn_g, m_rwkv_gn_b, m_w_out, m_final_norm_g, v_norm_g, v_w_in, v_ret_gn_g, v_rwkv_mu, v_w_lora_up, v_w0, v_a_lora_up, v_a0, v_k_k, v_k_a, v_r_k, v_rwkv_gn_g, v_rwkv_gn_b, v_w_out, v_final_norm_g):
    W = RW_WIDTH
    params = dict(norm_g=norm_g, ret_gn_g=ret_gn_g, rwkv_mu=rwkv_mu, w0=w0, a0=a0, k_k=k_k, k_a=k_a, r_k=r_k,
                  rwkv_gn_g=rwkv_gn_g, rwkv_gn_b=rwkv_gn_b, final_norm_g=final_norm_g)
    moments_m = dict(norm_g=m_norm_g, ret_gn_g=m_ret_gn_g, rwkv_mu=m_rwkv_mu, w0=m_w0, a0=m_a0, k_k=m_k_k, k_a=m_k_a,
                     r_k=m_r_k, rwkv_gn_g=m_rwkv_gn_g, rwkv_gn_b=m_rwkv_gn_b, final_norm_g=m_final_norm_g)
    moments_v = dict(norm_g=v_norm_g, ret_gn_g=v_ret_gn_g, rwkv_mu=v_rwkv_mu, w0=v_w0, a0=v_a0, k_k=v_k_k, k_a=v_k_a,
                     r_k=v_r_k, rwkv_gn_g=v_rwkv_gn_g, rwkv_gn_b=v_rwkv_gn_b, final_norm_g=v_final_norm_g)
    xi, yi, ci = _position()
    chip = (2 * xi + yi).astype(jnp.int32)

    def halves(a):
        return a.reshape(2, a.shape[0] // 2, a.shape[1])

    w_t, m_t, v_t = (jnp.transpose(a[0]) for a in (w_in, m_w_in, v_w_in))
    own_in = halves(w_t.astype(BF16))
    w_in_t = _with_own(_gather_chips([own_in])[0], own_in, chip).reshape(IN_COLS, D_MODEL)
    late = [halves(w_out[0].astype(BF16)), halves(w_lora_up[0]), halves(a_lora_up[0])]
    small = {n: params[n].reshape(1, -1) for n in SMALL_NAMES}

    loss, grad_x, d_w_in, d_w_out, d_lora, d_small = _local_step(x[0], loss_target[0], w_in_t, late, chip, small)

    core = ci.astype(jnp.int32)
    gi = d_w_in.reshape(N_CHIPS, IN_SHARD, D_MODEL)
    go = d_w_out.reshape(N_CHIPS, OUT_SHARD, D_MODEL)
    gs = _pack_small([d_small[n] for n in SMALL_NAMES], loss, d_lora)
    p_in, p_out, p_small = _pair_exchange(gi, go, gs)
    c_in, c_out, c_small = _pair_sum(gi, go, gs, p_in, p_out, p_small, core.reshape(1))
    l_in, l_out, l_small = _chip_exchange(c_in, c_out, c_small)
    r_in, r_out, tot = _chip_sum(gi, go, p_in, p_out, c_small, l_in, l_out, l_small, jnp.stack([chip, core]))
    t_in, t_out = _pair_share(r_in, r_out)

    grad_w_in, d_in, nm_in, nv_in = (jnp.transpose(a) for a in _adamw(
        "adamw_w_in", w_t, r_in, t_in, m_t, v_t, core.reshape(1), HALF_IN // 2))
    grad_w_out, d_out, nm_out, nv_out = _adamw("adamw_w_out", w_out[0], r_out, t_out, m_w_out[0], v_w_out[0],
                                               core.reshape(1), HALF_OUT)
    par_names = SMALL_NAMES + ("w_lora_up", "a_lora_up")

    def operands(tree, lw_, la_):
        return [tree[n].reshape(1, -1) for n in SMALL_NAMES] + [lw_[0], la_[0]]

    res = _adamw_small(tot, chip.reshape(1), operands(params, w_lora_up, a_lora_up),
                       operands(moments_m, m_w_lora_up, m_a_lora_up), operands(moments_v, v_w_lora_up, v_a_lora_up))

    names = ("norm_g", "w_in", "ret_gn_g", "rwkv_mu", "w_lora_up", "w0", "a_lora_up", "a0", "k_k", "k_a", "r_k",
             "rwkv_gn_g", "rwkv_gn_b", "w_out", "final_norm_g")
    shapes = dict(w_in=w_in.shape, w_out=w_out.shape, w_lora_up=w_lora_up.shape, a_lora_up=a_lora_up.shape,
                  **{n: params[n].shape for n in SMALL_NAMES})

    def leaves(pars, big_in, big_out):
        tree = dict(zip(par_names, pars), w_in=big_in, w_out=big_out)
        return [tree[n].reshape(shapes[n]) for n in names]

    grads = leaves(res[0], grad_w_in, grad_w_out)
    deltas = leaves(res[1], d_in, d_out)
    new_m = leaves(res[2], nm_in, nm_out)
    new_v = leaves(res[3], nv_in, nv_out)
    return (tot[PACK_LOSS, 0], grad_x.reshape(x.shape), *grads, *deltas, *new_m, *new_v)
```

```python
import functools

import numpy as np
import jax
import jax.numpy as jnp
from jax import lax
from jax.experimental import pallas as pl
from jax.experimental.pallas import tpu as pltpu

F32 = jnp.float32
BF16 = jnp.bfloat16
X3 = "bf16x3"
B1 = "bf16"
MESH = pl.DeviceIdType.MESH

D_MODEL = 1024
N_CHIPS = 4
RET_HEADS = 4
RET_DK = 64
RET_DV = 128
RET_QK = RET_HEADS * RET_DK
RET_WIDTH = RET_HEADS * RET_DV
RET_COLS = 2 * RET_QK + 2 * RET_WIDTH
RET_CHUNK = 64
RET_GROUP = 8
RW_WIDTH = 512
RW_HEAD = 64
RW_HEADS = 8
LORA = 64
RW_COLS = 4 * RW_WIDTH + 2 * LORA
IN_COLS = RET_COLS + RW_COLS
IN_SHARD = IN_COLS // N_CHIPS
OUT_SHARD = D_MODEL // N_CHIPS
ROPE_BASE = 10000.0
RMS_EPS = 1e-6
RET_GN_EPS = 1e-5
RW_GN_EPS = 64e-5
WKV_CHUNK = 16
WKV_GROUP = 8
N_VEC = 5

ADAM_LR = 0.001
ADAM_B1 = 0.9
ADAM_B2 = 0.999
ADAM_EPS = 1e-08
ADAM_WD = 0.01
ADAM_STEP = 10

VMEM_LIMIT = 56 * 1024 * 1024

PACK_W = 512
SMALL_NAMES = ("norm_g", "ret_gn_g", "rwkv_mu", "w0", "a0", "k_k", "k_a", "r_k", "rwkv_gn_g", "rwkv_gn_b",
               "final_norm_g")
SMALL_SIZES = (1024, 512, 2176, 512, 512, 512, 512, 512, 512, 512, 1024)
PACK_LORA_W = 0
PACK_LORA_A = LORA
PACK_SMALL = 2 * LORA


def _pack_layout():
    rows, at = {}, PACK_SMALL
    for name, n in zip(SMALL_NAMES, SMALL_SIZES):
        rows[name] = at
        at += -(-n // PACK_W)
    return rows, at


PACK_AT, PACK_LOSS = _pack_layout()
PACK_ROWS = -(-(PACK_LOSS + 1) // 8) * 8


def _cparams(**kw):
    return pltpu.CompilerParams(vmem_limit_bytes=VMEM_LIMIT, **kw)


def _split(x):
    hi = x.astype(BF16)
    lo = (x - hi.astype(F32)).astype(BF16)
    return hi, lo


def _dot_dims(a, b, dims, precision):
    if precision == B1:
        a, b = a.astype(BF16), b.astype(BF16)
    if precision != X3:
        return lax.dot_general(a, b, dims, preferred_element_type=F32)
    (ah, al), (bh, bl) = _split(a), _split(b)
    dot = lambda u, w: lax.dot_general(u, w, dims, preferred_element_type=F32)
    return dot(ah, bh) + dot(ah, bl) + dot(al, bh)


def _dot(a, b, precision=None):
    return _dot_dims(a, b, (((1,), (0,)), ((), ())), precision)


def _dot_nt(a, b, precision=None):
    return _dot_dims(a, b, (((1,), (1,)), ((), ())), precision)


def _dot_tn(a, b, precision=None):
    return _dot_dims(a, b, (((0,), (0,)), ((), ())), precision)


@jax.custom_vjp
def _segsum(x, seg):
    hi, lo = _split(x)
    return _dot(hi, seg) + _dot(lo, seg)


def _segsum_fwd(x, seg):
    return _segsum(x, seg), seg


def _segsum_bwd(seg, ct):
    return _segsum(ct, seg), jnp.zeros_like(seg)


_segsum.defvjp(_segsum_fwd, _segsum_bwd)


def _softplus(z):
    return jnp.maximum(z, 0.0) + jnp.log(1.0 + jnp.exp(-jnp.abs(z)))


def _full(shape):
    nd = len(shape)
    return pl.BlockSpec(shape, lambda *_: (0,) * nd)


def _rope_tables(T):
    half = RET_DK // 2
    expo = -jnp.arange(half, dtype=F32) / jnp.float32(half)
    freqs = jnp.exp(expo * jnp.float32(np.log(ROPE_BASE)))
    ang = jnp.arange(T, dtype=jnp.int32).astype(F32)[:, None] * freqs[None, :]
    cos = jnp.tile(jnp.cos(ang), (1, 2 * RET_HEADS))
    sin = jnp.tile(jnp.sin(ang), (1, 2 * RET_HEADS))
    return cos, sin


def _ret_tables():
    H, C = RET_HEADS, RET_CHUNK
    hidx = jnp.arange(H, dtype=F32)
    lg = jnp.log(1.0 - jnp.exp2(-5.0 - hidx))
    idx = jnp.arange(C, dtype=F32)
    intra = jnp.exp(lg[:, None, None] * jnp.abs(idx[:, None] - idx[None, :]))
    q_dec = jnp.transpose(jnp.exp(lg[:, None] * (idx[None, :] + 1.0)))
    k_dec = jnp.transpose(jnp.exp(lg[:, None] * (C - 1.0 - idx[None, :])))
    chunk_dec = jnp.exp(lg * C)
    qd = jnp.repeat(q_dec, RET_DK, axis=1)
    kd = jnp.repeat(k_dec, RET_DK, axis=1)
    row_h = np.arange(RET_QK) // RET_DK
    col_h = np.arange(RET_WIDTH) // RET_DV
    bm = jnp.asarray((row_h[:, None] == col_h[None, :]).astype(np.float32))
    cd = bm * jnp.repeat(chunk_dec, RET_DK)[:, None]
    return intra, qd, kd, cd, bm


def _seg_matrix(width, head):
    h = np.arange(width) // head
    return jnp.asarray((h[:, None] == h[None, :]).astype(np.float32), dtype=BF16)


def _wkv_expand_table():
    Tc = WKV_CHUNK
    k = np.arange(2 * RW_HEADS * Tc)
    kh, kt = (k % (RW_HEADS * Tc)) // Tc, k % Tc
    nh = np.arange(RW_WIDTH) // RW_HEAD
    e = (kh[None, :, None] == nh[None, None, :]) & (kt[None, :, None] == np.arange(Tc)[:, None, None])
    return jnp.asarray(e.astype(np.float32), dtype=BF16)


def _wkv_reduce_table():
    Tc = WKV_CHUNK
    kh = np.arange(RW_WIDTH) // RW_HEAD
    n = np.arange(RW_HEADS * Tc)
    nh, nt = n // Tc, n % Tc
    r = (kh[None, :, None] == nh[None, None, :]) & (nt[None, None, :] == np.arange(Tc)[:, None, None])
    return jnp.asarray(r.astype(np.float32), dtype=BF16)


def _inproj_fwd(x, norm_g, w_t, late):
    T = x.shape[0]
    tm = _row_tile(T)
    nt = T // tm
    n = len(late)

    def body(x_ref, g_ref, w_ref, *refs):
        late_in, (pret_ref, prw_ref, u_ref, last_ref) = refs[:n], refs[n:n + 4]
        late_out, sems = refs[n + 4:2 * n + 4], refs[2 * n + 4:]

        @pl.when(pl.program_id(0) == 0)
        def _():
            _gather_start(_gather_copies(late_in, late_out, sems))

        xf = x_ref[...]
        rstd = lax.rsqrt(jnp.mean(xf * xf, axis=-1, keepdims=True) + RMS_EPS)
        ub = ((xf * rstd) * g_ref[...]).astype(BF16)
        u_ref[...] = ub
        pret_ref[...] = _dot_nt(ub, w_ref[:RET_COLS, :])
        p_rw = _dot_nt(ub, w_ref[RET_COLS:, :])
        prw_ref[...] = p_rw
        last_ref[0] = p_rw[tm - 1:tm, :]

        @pl.when(pl.program_id(0) == nt // 4)
        def _():
            _gather_relay(_gather_copies(late_in, late_out, sems))

        @pl.when(pl.program_id(0) == nt - 1)
        def _():
            _gather_finish(_gather_copies(late_in, late_out, sems))

    return pl.pallas_call(
        body, name="inproj_fwd", grid=(nt,),
        in_specs=[pl.BlockSpec((tm, D_MODEL), lambda i: (i, 0)), _full((1, D_MODEL)), _full((IN_COLS, D_MODEL))]
                 + [ANY] * n,
        out_specs=[pl.BlockSpec((tm, RET_COLS), lambda i: (i, 0)), pl.BlockSpec((tm, RW_COLS), lambda i: (i, 0)),
                   pl.BlockSpec((tm, D_MODEL), lambda i: (i, 0)), pl.BlockSpec((1, 1, RW_COLS), lambda i: (i, 0, 0))]
                  + [ANY] * n,
        out_shape=[jax.ShapeDtypeStruct((T, RET_COLS), F32), jax.ShapeDtypeStruct((T, RW_COLS), F32),
                   jax.ShapeDtypeStruct((T, D_MODEL), BF16), jax.ShapeDtypeStruct((nt, 1, RW_COLS), F32)]
                  + _gather_shapes(late),
        scratch_shapes=_gather_sems(n),
        compiler_params=_cparams(dimension_semantics=("arbitrary",)),
    )(x, norm_g, w_t, *late)


def _rot_half(x):
    n = x.shape[1]
    lane = lax.broadcasted_iota(jnp.int32, x.shape, 1)
    first = (lane % RET_DK) < (RET_DK // 2)
    return jnp.where(first, -pltpu.roll(x, n - RET_DK // 2, 1), pltpu.roll(x, RET_DK // 2, 1))


def _rope(x, cos, sin):
    return x * cos + _rot_half(x) * sin


def _rope_bwd(d, cos, sin):
    return d * cos - _rot_half(d * sin)


def _ret_post(ret, g, gn_g):
    heads = []
    for h in range(RET_HEADS):
        xh = ret[:, h * RET_DV:(h + 1) * RET_DV]
        xc = xh - jnp.mean(xh, axis=-1, keepdims=True)
        heads.append(xc * lax.rsqrt(jnp.mean(xc * xc, axis=-1, keepdims=True) + RET_GN_EPS))
    return (g * jax.nn.sigmoid(g)) * (jnp.concatenate(heads, axis=1) * gn_g)


def _ret_scores(qt, kt, d_ref, h):
    lane = lax.broadcasted_iota(jnp.int32, qt.shape, 1)
    qh = jnp.where(lane // RET_DK == h, qt, 0.0)
    return qh, _dot_nt(qh, kt, B1) * d_ref[h]


def _ret_group(nch):
    return min(RET_GROUP, nch)


def _ret_fwd(p_ret, cos, sin, tabs, gn_g):
    T = p_ret.shape[0]
    C = RET_CHUNK
    nch = T // C
    G = _ret_group(nch)
    intra_d, qd, kd, cd, bm = tabs

    def body(q_ref, k_ref, v_ref, g_ref, cos_ref, sin_ref, qd_ref, kd_ref, d_ref, cd_ref, bm_ref, gn_ref,
             y_ref, ret_ref, sin_out_ref, s_ref, qt_buf, kv_buf):
        @pl.when(pl.program_id(0) == 0)
        def _():
            s_ref[...] = jnp.zeros_like(s_ref)

        cosv, sinv = cos_ref[...], sin_ref[...]
        qt_all = _rope(q_ref[...], cosv, sinv)
        kt_all = _rope(k_ref[...], cosv, sinv) * (RET_DK ** -0.5)
        for i in range(G):
            rows = slice(i * C, (i + 1) * C)
            qt, kt, v = qt_all[rows], kt_all[rows], v_ref[rows, :]
            intra = []
            for h in range(RET_HEADS):
                _, a = _ret_scores(qt, kt, d_ref, h)
                intra.append(_dot(a, v[:, h * RET_DV:(h + 1) * RET_DV], B1))
            ret_ref[rows, :] = jnp.concatenate(intra, axis=1)
            qt_buf[rows, :] = qt * qd_ref[...]
            kv_buf[i] = _dot_tn(kt * kd_ref[...], v, B1) * bm_ref[...]
        s_in = s_ref[...]
        for i in range(G):
            rows = slice(i * C, (i + 1) * C)
            sin_out_ref[i] = s_in
            ret_ref[rows, :] += _dot(qt_buf[rows, :], s_in, B1)
            s_in = s_in * cd_ref[...] + kv_buf[i]
        s_ref[...] = s_in
        y_ref[...] = _ret_post(ret_ref[...], g_ref[...], gn_ref[...]).astype(BF16)

    GC = G * C
    return pl.pallas_call(
        body, name="ret_fwd", grid=(nch // G,),
        in_specs=[pl.BlockSpec((GC, RET_QK), lambda c: (c, 0)), pl.BlockSpec((GC, RET_QK), lambda c: (c, 1)),
                  pl.BlockSpec((GC, RET_WIDTH), lambda c: (c, 1)), pl.BlockSpec((GC, RET_WIDTH), lambda c: (c, 2)),
                  pl.BlockSpec((GC, RET_QK), lambda c: (c, 0)), pl.BlockSpec((GC, RET_QK), lambda c: (c, 0)),
                  _full((C, RET_QK)), _full((C, RET_QK)), _full((RET_HEADS, C, C)),
                  _full((RET_QK, RET_WIDTH)), _full((RET_QK, RET_WIDTH)), _full((1, RET_WIDTH))],
        out_specs=[pl.BlockSpec((GC, RET_WIDTH), lambda c: (c, 0)), pl.BlockSpec((GC, RET_WIDTH), lambda c: (c, 0)),
                   pl.BlockSpec((G, RET_QK, RET_WIDTH), lambda c: (c, 0, 0))],
        out_shape=[jax.ShapeDtypeStruct((T, RET_WIDTH), BF16), jax.ShapeDtypeStruct((T, RET_WIDTH), F32),
                   jax.ShapeDtypeStruct((nch, RET_QK, RET_WIDTH), F32)],
        scratch_shapes=[pltpu.VMEM((RET_QK, RET_WIDTH), F32), pltpu.VMEM((GC, RET_QK), F32),
                        pltpu.VMEM((G, RET_QK, RET_WIDTH), F32)],
        compiler_params=_cparams(dimension_semantics=("arbitrary",)),
    )(p_ret, p_ret, p_ret, p_ret, cos, sin, qd, kd, intra_d, cd, bm, gn_g)


def _ret_bwd(p_ret, cos, sin, tabs, gn_g, ret, s_in_all, dy):
    T = p_ret.shape[0]
    C = RET_CHUNK
    nch = T // C
    G = _ret_group(nch)
    ngr = nch // G
    intra_d, qd, kd, cd, bm = tabs

    def rev(j):
        return lambda c: (ngr - 1 - c, j)

    def body(q_ref, k_ref, v_ref, g_ref, cos_ref, sin_ref, qd_ref, kd_ref, d_ref, cd_ref, bm_ref, gn_ref,
             ret_ref, sin_ref_, dy_ref, dp_ref, dgn_ref, ds_ref, dkt_buf, ktk_buf, g_buf):
        @pl.when(pl.program_id(0) == 0)
        def _():
            ds_ref[...] = jnp.zeros_like(ds_ref)
            dgn_ref[...] = jnp.zeros_like(dgn_ref)

        _, post_vjp = jax.vjp(_ret_post, ret_ref[...], g_ref[...], gn_ref[...])
        dret_all, dg_all, dgn = post_vjp(dy_ref[...])
        dgn_ref[...] += dgn
        v_cols = slice(2 * RET_QK, 2 * RET_QK + RET_WIDTH)
        dp_ref[:, 2 * RET_QK + RET_WIDTH:] = dg_all

        qdv, kdv = qd_ref[...], kd_ref[...]
        cosv, sinv = cos_ref[...], sin_ref[...]
        qt_all = _rope(q_ref[...], cosv, sinv)
        kt_all = _rope(k_ref[...], cosv, sinv) * (RET_DK ** -0.5)
        for i in range(G):
            rows = slice(i * C, (i + 1) * C)
            qt, kt, v, dret = qt_all[rows], kt_all[rows], v_ref[rows, :], dret_all[rows, :]
            dqt = qdv * _dot_nt(dret, sin_ref_[i], B1)
            dkt = jnp.zeros_like(kt)
            dvs = []
            for h in range(RET_HEADS):
                sl = slice(h * RET_DV, (h + 1) * RET_DV)
                qh, a = _ret_scores(qt, kt, d_ref, h)
                lane = lax.broadcasted_iota(jnp.int32, kt.shape, 1)
                kh = jnp.where(lane // RET_DK == h, kt, 0.0)
                da = _dot_nt(dret[:, sl], v[:, sl], B1) * d_ref[h]
                dvs.append(_dot_tn(a, dret[:, sl], B1))
                dqt = dqt + _dot(da, kh, B1)
                dkt = dkt + _dot_tn(da, qh, B1)
            dp_ref[rows, :RET_QK] = _rope_bwd(dqt, cosv[rows], sinv[rows])
            dp_ref[rows, v_cols] = jnp.concatenate(dvs, axis=1)
            dkt_buf[rows, :] = dkt
            ktk_buf[rows, :] = kt * kdv
            g_buf[i] = _dot_tn(qt * qdv, dret, B1) * bm_ref[...]
        ds_out = ds_ref[...]
        for i in reversed(range(G)):
            rows = slice(i * C, (i + 1) * C)
            dkt = dkt_buf[rows, :] + kdv * _dot_nt(v_ref[rows, :], ds_out, B1)
            dp_ref[rows, RET_QK:2 * RET_QK] = _rope_bwd(dkt * (RET_DK ** -0.5), cosv[rows], sinv[rows])
            dp_ref[rows, v_cols] += _dot(ktk_buf[rows, :], ds_out, B1)
            ds_out = ds_out * cd_ref[...] + g_buf[i]
        ds_ref[...] = ds_out

    GC = G * C
    return pl.pallas_call(
        body, name="ret_bwd", grid=(ngr,),
        in_specs=[pl.BlockSpec((GC, RET_QK), rev(0)), pl.BlockSpec((GC, RET_QK), rev(1)),
                  pl.BlockSpec((GC, RET_WIDTH), rev(1)), pl.BlockSpec((GC, RET_WIDTH), rev(2)),
                  pl.BlockSpec((GC, RET_QK), rev(0)), pl.BlockSpec((GC, RET_QK), rev(0)),
                  _full((C, RET_QK)), _full((C, RET_QK)), _full((RET_HEADS, C, C)),
                  _full((RET_QK, RET_WIDTH)), _full((RET_QK, RET_WIDTH)), _full((1, RET_WIDTH)),
                  pl.BlockSpec((GC, RET_WIDTH), rev(0)),
                  pl.BlockSpec((G, RET_QK, RET_WIDTH), lambda c: (ngr - 1 - c, 0, 0)),
                  pl.BlockSpec((GC, RET_WIDTH), rev(0))],
        out_specs=[pl.BlockSpec((GC, RET_COLS), rev(0)), _full((1, RET_WIDTH))],
        out_shape=[jax.ShapeDtypeStruct((T, RET_COLS), F32), jax.ShapeDtypeStruct((1, RET_WIDTH), F32)],
        scratch_shapes=[pltpu.VMEM((RET_QK, RET_WIDTH), F32), pltpu.VMEM((GC, RET_QK), F32),
                        pltpu.VMEM((GC, RET_QK), F32), pltpu.VMEM((G, RET_QK, RET_WIDTH), F32)],
        compiler_params=_cparams(dimension_semantics=("arbitrary",)),
    )(p_ret, p_ret, p_ret, p_ret, cos, sin, qd, kd, intra_d, cd, bm, gn_g, ret, s_in_all, dy)


@jax.custom_vjp
def _chunk_sums(x, tri):
    hi, lo = _split(x)
    return _dot(tri, hi) + _dot(tri, lo)


def _chunk_sums_fwd(x, tri):
    return _chunk_sums(x, tri), tri


def _chunk_sums_bwd(tri, ct):
    hi, lo = _split(ct)
    return _dot_tn(tri, hi) + _dot_tn(tri, lo), jnp.zeros_like(tri)


_chunk_sums.defvjp(_chunk_sums_fwd, _chunk_sums_bwd)


@jax.custom_vjp
def _lora_dot(z, lora):
    return _dot(z, lora, X3)


def _lora_dot_fwd(z, lora):
    return _lora_dot(z, lora), (z, lora)


def _lora_dot_bwd(res, ct):
    z, lora = res
    return _dot_nt(ct, lora, X3), _dot_tn(z, ct, X3)


_lora_dot.defvjp(_lora_dot_fwd, _lora_dot_bwd)


def _chunk_tables(tm):
    t = np.arange(tm)
    same = (t[:, None] // WKV_CHUNK) == (t[None, :] // WKV_CHUNK)
    return jnp.asarray(np.stack([same & (t[None, :] <= t[:, None]), same]).astype(np.float32), dtype=BF16)


def _prep_fn(p, prev, mu, w0, a0, k_k, k_a, lora, seg, tri):
    W = RW_WIDTH
    ps = p + mu * (prev - p)
    r, kr, vr, g = ps[:, 0:W], ps[:, W:2 * W], ps[:, 2 * W:3 * W], ps[:, 3 * W:4 * W]
    z = ps[:, 4 * W:]
    lane = lax.broadcasted_iota(jnp.int32, z.shape, 1)
    z = jnp.where(lane < LORA, jnp.tanh(z), z)
    lo = _lora_dot(z, lora)
    w_log = -_softplus(-(w0 + lo[:, :W])) - 0.5
    log_decay = -jnp.exp(w_log)
    cum = _chunk_sums(log_decay, tri[0])
    total = _chunk_sums(log_decay, tri[1])
    a = jax.nn.sigmoid(a0 + lo[:, W:])
    kk = kr * k_k
    kk = kk / jnp.maximum(jnp.sqrt(_segsum(kk * kk, seg)), 1e-12)
    k = kr * (1.0 + (a - 1.0) * k_a)
    grow = jnp.exp(-cum)
    return kk * jnp.exp(cum - log_decay), (kk * a) * grow, k * grow, r * jnp.exp(cum), jnp.exp(total), vr, g, r * k


def _post_fn(o, rk, v, g, gn_g, gn_b, r_k, seg):
    mu = _segsum(o, seg) * (1.0 / RW_HEAD)
    oc = o - mu
    var = _segsum(oc * oc, seg) * (1.0 / RW_HEAD)
    on = oc * lax.rsqrt(var + RW_GN_EPS) * gn_g + gn_b
    bonus = _segsum(rk * r_k, seg) * v
    return (g * jax.nn.sigmoid(g)) * (on + bonus)


N_PAIR = (N_VEC + 1) // 2
HALF_LANES = 64


def _swap_halves(x):
    return pltpu.roll(x, HALF_LANES, 1)


def _pack_heads(vecs):
    tm = vecs[0].shape[0]
    low = lax.broadcasted_iota(jnp.int32, (tm, 128), 1) < HALF_LANES
    out = []
    for p in range(N_PAIR):
        a = vecs[2 * p]
        b = vecs[2 * p + 1] if 2 * p + 1 < len(vecs) else None
        heads = []
        for m in range(RW_WIDTH // 128):
            am = a[:, m * 128:(m + 1) * 128]
            bm = jnp.zeros_like(am) if b is None else b[:, m * 128:(m + 1) * 128]
            heads.append(jnp.where(low, am, _swap_halves(bm)))
            heads.append(jnp.where(low, _swap_halves(am), bm))
        out.append(heads)
    return out


def _unpack_heads(hm_ref):
    tm = hm_ref.shape[2]
    low = lax.broadcasted_iota(jnp.int32, (tm, 128), 1) < HALF_LANES
    vecs = []
    for p in range(N_PAIR):
        a, b = [], []
        for m in range(RW_WIDTH // 128):
            even, odd = hm_ref[p, 2 * m], hm_ref[p, 2 * m + 1]
            a.append(jnp.where(low, even, _swap_halves(odd)))
            b.append(jnp.where(low, _swap_halves(even), odd))
        vecs += [jnp.concatenate(a, axis=1), jnp.concatenate(b, axis=1)]
    return vecs[:N_VEC]


def _shift_down(p, first_row):
    row = lax.broadcasted_iota(jnp.int32, p.shape, 0)
    return jnp.where(row == 0, first_row, pltpu.roll(p, 1, 0))


def _shift_up(p, last_row):
    n = p.shape[0]
    row = lax.broadcasted_iota(jnp.int32, p.shape, 0)
    return jnp.where(row == n - 1, last_row, pltpu.roll(p, n - 1, 0))


def _row_tile(T):
    return min(T, 256)


def _prep_fwd(p_rw, bnd, mu, w0, a0, k_k, k_a, lora, seg64, tri):
    T = p_rw.shape[0]
    tm = _row_tile(T)
    W = RW_WIDTH

    def body(p_ref, bnd_ref, mu_ref, w0_ref, a0_ref, kk_ref, ka_ref, lora_ref, seg_ref, tri_ref,
             hm_ref, v_ref, g_ref, rk_ref):
        p = p_ref[...]
        prev = _shift_down(p, jnp.where(pl.program_id(0) == 0, 0.0, bnd_ref[0]))
        res = _prep_fn(p, prev, mu_ref[...], w0_ref[...], a0_ref[...], kk_ref[...], ka_ref[...], lora_ref[...],
                       seg_ref[...], (tri_ref[0], tri_ref[1]))
        for pair, heads in enumerate(_pack_heads(res[:N_VEC])):
            for h, val in enumerate(heads):
                hm_ref[pair, h] = val
        v_ref[...] = res[N_VEC]
        g_ref[...] = res[N_VEC + 1]
        rk_ref[...] = res[N_VEC + 2]

    small = _full((1, W))
    row = pl.BlockSpec((tm, W), lambda i: (i, 0))
    return pl.pallas_call(
        body, name="rwkv_prep_fwd", grid=(T // tm,),
        in_specs=[pl.BlockSpec((tm, RW_COLS), lambda i: (i, 0)),
                  pl.BlockSpec((1, 1, RW_COLS), lambda i: (jnp.maximum(i - 1, 0), 0, 0)),
                  _full((1, RW_COLS)), small, small, small, small, _full((2 * LORA, 2 * W)), _full((W, W)),
                  _full((2, tm, tm))],
        out_specs=[pl.BlockSpec((N_PAIR, RW_HEADS, tm, 128), lambda i: (0, 0, i, 0)), row, row, row],
        out_shape=[jax.ShapeDtypeStruct((N_PAIR, RW_HEADS, T, 128), F32)] + [jax.ShapeDtypeStruct((T, W), F32)] * 3,
        compiler_params=_cparams(dimension_semantics=("arbitrary",)),
    )(p_rw, bnd, mu, w0, a0, k_k, k_a, lora, seg64, tri)


def _prep_bwd(p_rw, bnd, mu, w0, a0, k_k, k_a, lora, seg64, tri, cts):
    T = p_rw.shape[0]
    tm = _row_tile(T)
    W = RW_WIDTH

    def body(p_ref, bnd_ref, mu_ref, w0_ref, a0_ref, kk_ref, ka_ref, lora_ref, seg_ref, tri_ref,
             dhm_ref, drk_ref, dv1_ref, dv2_ref, dg_ref,
             dp_ref, dprev_ref, dfirst_ref, dmu_ref, dw0_ref, da0_ref, dkk_p_ref, dka_ref, dlora_ref):
        accs = (dmu_ref, dw0_ref, da0_ref, dkk_p_ref, dka_ref, dlora_ref)

        @pl.when(pl.program_id(0) == 0)
        def _():
            for a_ref in accs:
                a_ref[...] = jnp.zeros_like(a_ref)

        p = p_ref[...]
        prev = _shift_down(p, jnp.where(pl.program_id(0) == 0, 0.0, bnd_ref[0]))
        seg, tri = seg_ref[...], (tri_ref[0], tri_ref[1])
        _, vjp = jax.vjp(lambda *a: _prep_fn(*a, seg, tri), p, prev, mu_ref[...], w0_ref[...], a0_ref[...],
                         kk_ref[...], ka_ref[...], lora_ref[...])
        ct = (*_unpack_heads(dhm_ref), dv1_ref[...] + dv2_ref[...], dg_ref[...], drk_ref[...])
        grads = vjp(ct)
        dp_ref[...] = grads[0]
        dprev_ref[...] = grads[1]
        dfirst_ref[0] = grads[1][0:1, :]
        for a_ref, gval in zip(accs, grads[2:]):
            a_ref[...] += gval

    small = _full((1, W))
    row = pl.BlockSpec((tm, W), lambda i: (i, 0))
    return pl.pallas_call(
        body, name="rwkv_prep_bwd", grid=(T // tm,),
        in_specs=[pl.BlockSpec((tm, RW_COLS), lambda i: (i, 0)),
                  pl.BlockSpec((1, 1, RW_COLS), lambda i: (jnp.maximum(i - 1, 0), 0, 0)),
                  _full((1, RW_COLS)), small, small, small, small, _full((2 * LORA, 2 * W)), _full((W, W)),
                  _full((2, tm, tm)), pl.BlockSpec((N_PAIR, RW_HEADS, tm, 128), lambda i: (0, 0, i, 0))] + [row] * 4,
        out_specs=[pl.BlockSpec((tm, RW_COLS), lambda i: (i, 0)), pl.BlockSpec((tm, RW_COLS), lambda i: (i, 0)),
                   pl.BlockSpec((1, 1, RW_COLS), lambda i: (i, 0, 0)),
                   _full((1, RW_COLS)), small, small, small, small, _full((2 * LORA, 2 * W))],
        out_shape=[jax.ShapeDtypeStruct((T, RW_COLS), F32), jax.ShapeDtypeStruct((T, RW_COLS), F32),
                   jax.ShapeDtypeStruct((T // tm, 1, RW_COLS), F32),
                   jax.ShapeDtypeStruct((1, RW_COLS), F32)] + [jax.ShapeDtypeStruct((1, W), F32)] * 4
                  + [jax.ShapeDtypeStruct((2 * LORA, 2 * W), F32)],
        compiler_params=_cparams(dimension_semantics=("arbitrary",)),
    )(p_rw, bnd, mu, w0, a0, k_k, k_a, lora, seg64, tri, *cts)


def _post_fwd(o, rk, v, g, gn_g, gn_b, r_k, seg64):
    T = o.shape[0]
    tm = _row_tile(T)
    W = RW_WIDTH

    def body(o_ref, rk_ref, v_ref, g_ref, gg_ref, gb_ref, rkp_ref, seg_ref, y_ref):
        y_ref[...] = _post_fn(o_ref[...], rk_ref[...], v_ref[...], g_ref[...], gg_ref[...], gb_ref[...],
                              rkp_ref[...], seg_ref[...]).astype(BF16)

    row = pl.BlockSpec((tm, W), lambda i: (i, 0))
    small = _full((1, W))
    return pl.pallas_call(
        body, name="rwkv_post_fwd", grid=(T // tm,),
        in_specs=[row] * 4 + [small] * 3 + [_full((W, W))],
        out_specs=row, out_shape=jax.ShapeDtypeStruct((T, W), BF16),
        compiler_params=_cparams(dimension_semantics=("arbitrary",)),
    )(o, rk, v, g, gn_g, gn_b, r_k, seg64)


def _post_bwd(o, rk, v, g, gn_g, gn_b, r_k, seg64, dy):
    T = o.shape[0]
    tm = _row_tile(T)
    W = RW_WIDTH

    def body(o_ref, rk_ref, v_ref, g_ref, gg_ref, gb_ref, rkp_ref, seg_ref, dy_ref,
             do_ref, drk_ref, dv_ref, dg_ref, dgg_ref, dgb_ref, drkp_ref):
        accs = (dgg_ref, dgb_ref, drkp_ref)

        @pl.when(pl.program_id(0) == 0)
        def _():
            for a_ref in accs:
                a_ref[...] = jnp.zeros_like(a_ref)

        seg = seg_ref[...]
        _, vjp = jax.vjp(lambda *a: _post_fn(*a, seg), o_ref[...], rk_ref[...], v_ref[...], g_ref[...],
                         gg_ref[...], gb_ref[...], rkp_ref[...])
        grads = vjp(dy_ref[...])
        for o_, gval in zip((do_ref, drk_ref, dv_ref, dg_ref), grads[:4]):
            o_[...] = gval
        for a_ref, gval in zip(accs, grads[4:]):
            a_ref[...] += gval

    row = pl.BlockSpec((tm, W), lambda i: (i, 0))
    small = _full((1, W))
    return pl.pallas_call(
        body, name="rwkv_post_bwd", grid=(T // tm,),
        in_specs=[row] * 4 + [small] * 3 + [_full((W, W)), pl.BlockSpec((tm, W), lambda i: (i, 1))],
        out_specs=[row] * 4 + [small] * 3,
        out_shape=[jax.ShapeDtypeStruct((T, W), F32)] * 4 + [jax.ShapeDtypeStruct((1, W), F32)] * 3,
        compiler_params=_cparams(dimension_semantics=("arbitrary",)),
    )(o, rk, v, g, gn_g, gn_b, r_k, seg64, dy)


def _wkv_lhs(hm_ref, rows):
    tiles = [jnp.transpose(hm_ref[p, :, rows, :].reshape(RW_HEADS * WKV_CHUNK, 128)) for p in range(N_PAIR)]
    hi, lo = _split(jnp.concatenate(tiles, axis=0)[:N_VEC * RW_HEAD])
    return jnp.concatenate([hi, lo], axis=1)


def _wkv_group(nch):
    return min(WKV_GROUP, nch)


N_STEP_VEC = N_VEC - 1
PAD_ROWS = 16


def _wkv_fwd(cols, v, e_tab):
    T = v.shape[0]
    Tc = WKV_CHUNK
    nch = T // Tc
    G = _wkv_group(nch)
    J, W = RW_HEAD, RW_WIDTH
    JS = N_STEP_VEC * J

    def body(cols_ref, v_ref, e_ref, o_ref, states_ref, sa_ref, s_ref):
        @pl.when(pl.program_id(0) == 0)
        def _():
            s_ref[...] = jnp.zeros_like(s_ref)

        st = s_ref[...]
        for c in range(G):
            lhs = _wkv_lhs(cols_ref, slice(c * Tc, (c + 1) * Tc))
            for t in range(Tc):
                row = slice(c * Tc + t, c * Tc + t + 1)
                ex = _dot(lhs[:JS], e_ref[t])
                states_ref[c * Tc + t] = st
                sa = -jnp.sum(st * ex[0:J], axis=0, keepdims=True)
                st = st + ex[J:2 * J] * sa + ex[2 * J:3 * J] * v_ref[row, :]
                sa_ref[row, :] = sa
                o_ref[row, :] = jnp.sum(st * ex[3 * J:4 * J], axis=0, keepdims=True)
            st = st * _dot(lhs[JS:], e_ref[Tc - 1])
        s_ref[...] = st

    GT = G * Tc
    return pl.pallas_call(
        body, name="wkv_fwd", grid=(nch // G,),
        in_specs=[pl.BlockSpec((N_PAIR, RW_HEADS, GT, 128), lambda c: (0, 0, c, 0)),
                  pl.BlockSpec((GT, W), lambda c: (c, 0)), _full((Tc, 2 * 128, W))],
        out_specs=[pl.BlockSpec((GT, W), lambda c: (c, 0)), pl.BlockSpec((GT, J, W), lambda c: (c, 0, 0)),
                   pl.BlockSpec((GT, W), lambda c: (c, 0))],
        out_shape=[jax.ShapeDtypeStruct((T, W), F32), jax.ShapeDtypeStruct((T, J, W), F32),
                   jax.ShapeDtypeStruct((T, W), F32)],
        scratch_shapes=[pltpu.VMEM((J, W), F32)],
        compiler_params=_cparams(dimension_semantics=("arbitrary",)),
    )(cols, v, e_tab)


def _wkv_bwd(cols, v, do, states, sa, e_tab, r_tab):
    T = v.shape[0]
    Tc = WKV_CHUNK
    nch = T // Tc
    G = _wkv_group(nch)
    ngr = nch // G
    J, W = RW_HEAD, RW_WIDTH
    JS = N_STEP_VEC * J
    blocks = [slice(b * 128, (b + 1) * 128) for b in range(W // 128)]

    def body(cols_ref, v_ref, do_ref, states_ref, sa_ref, e_ref, r_ref, dv_ref, dhm_ref, ds_ref):
        @pl.when(pl.program_id(0) == 0)
        def _():
            ds_ref[...] = jnp.zeros_like(ds_ref)

        d_carry = [ds_ref[:, b] for b in blocks]
        last = Tc - 1
        for c in reversed(range(G)):
            at = c * Tc
            rows = slice(at, at + Tc)
            lhs = _wkv_lhs(cols_ref, rows)
            ex = _dot(lhs, e_ref[last])
            dst, ends = [], []
            for i, b in enumerate(blocks):
                s_end = (states_ref[at + last, :, b] + ex[J:2 * J, b] * sa_ref[at + last:at + Tc, b]
                         + ex[2 * J:3 * J, b] * v_ref[at + last:at + Tc, b])
                ends.append((d_carry[i] * s_end).astype(BF16))
                dst.append(d_carry[i] * ex[JS:, b])
            d_decay = _dot(jnp.concatenate(ends, axis=1), r_ref[last])
            acc = jnp.zeros((JS + PAD_ROWS, 128), F32)
            for t in reversed(range(Tc)):
                row = slice(at + t, at + t + 1)
                if t != last:
                    ex = _dot(lhs[:JS], e_ref[t])
                dvs, prods = [], []
                for i, b in enumerate(blocks):
                    kk_e, b_e, k_e, r_e = (ex[n * J:(n + 1) * J, b] for n in range(N_STEP_VEC))
                    do_row, v_row, sa_row = do_ref[row, b], v_ref[row, b], sa_ref[row, b]
                    s_old = states_ref[at + t, :, b]
                    s_new = states_ref[at + t + 1, :, b] if t != last else s_old + b_e * sa_row + k_e * v_row
                    dsn = dst[i] + r_e * do_row
                    dsa = jnp.sum(dsn * b_e, axis=0, keepdims=True)
                    dvs.append(jnp.sum(dsn * k_e, axis=0, keepdims=True))
                    prods.append(jnp.concatenate(
                        [s_old * (-dsa), dsn * sa_row, dsn * v_row, s_new * do_row, jnp.zeros((PAD_ROWS, 128), F32)],
                        axis=0).astype(BF16))
                    dst[i] = dsn - kk_e * dsa
                dv_ref[row, :] = jnp.concatenate(dvs, axis=1)
                acc = acc + _dot(jnp.concatenate(prods, axis=1), r_ref[t])
            d_carry = dst
            tiles = jnp.concatenate([acc[:JS], d_decay, jnp.zeros((2 * N_PAIR * J - N_VEC * J, 128), F32)], axis=0)
            for p in range(N_PAIR):
                dhm_ref[p, :, rows, :] = jnp.transpose(tiles[p * 128:(p + 1) * 128]).reshape(RW_HEADS, Tc, 128)
        for i, b in enumerate(blocks):
            ds_ref[:, b] = d_carry[i]

    GT = G * Tc
    rev2 = lambda c: (ngr - 1 - c, 0)
    rev3 = lambda c: (ngr - 1 - c, 0, 0)
    rev_hm = lambda c: (0, 0, ngr - 1 - c, 0)
    hm_spec = pl.BlockSpec((N_PAIR, RW_HEADS, GT, 128), rev_hm)
    return pl.pallas_call(
        body, name="wkv_bwd", grid=(ngr,),
        in_specs=[hm_spec, pl.BlockSpec((GT, W), rev2), pl.BlockSpec((GT, W), rev2),
                  pl.BlockSpec((GT, J, W), rev3), pl.BlockSpec((GT, W), rev2),
                  _full((Tc, 2 * 128, W)), _full((Tc, W, 128))],
        out_specs=[pl.BlockSpec((GT, W), rev2), hm_spec],
        out_shape=[jax.ShapeDtypeStruct((T, W), F32), jax.ShapeDtypeStruct((N_PAIR, RW_HEADS, T, 128), F32)],
        scratch_shapes=[pltpu.VMEM((J, W), F32)],
        compiler_params=_cparams(dimension_semantics=("arbitrary",)),
    )(cols, v, do, states, sa, e_tab, r_tab)


def _outproj(x, y_ret, y_rw, w_out_b, target, gf):
    T = x.shape[0]
    tm = _row_tile(T)
    W = RW_WIDTH

    def body(x_ref, yr_ref, yw_ref, w_ref, t_ref, gf_ref, loss_ref, dh_ref, dy_ref, dw_ref, dgf_ref):
        @pl.when(pl.program_id(0) == 0)
        def _():
            loss_ref[...] = jnp.zeros_like(loss_ref)
            dw_ref[...] = jnp.zeros_like(dw_ref)
            dgf_ref[...] = jnp.zeros_like(dgf_ref)

        y = jnp.concatenate([yr_ref[...], yw_ref[...]], axis=1)
        w = w_ref[...]
        h = x_ref[...] + _dot(y, w)
        rstd = lax.rsqrt(jnp.mean(h * h, axis=-1, keepdims=True) + RMS_EPS)
        hn = h * rstd
        gfv = gf_ref[...]
        err = hn * gfv - t_ref[...]
        loss_ref[...] += 0.5 * jnp.sum(jnp.mean(err * err, axis=-1))
        dout = err * (1.0 / D_MODEL)
        dgf_ref[...] += jnp.sum(dout * hn, axis=0, keepdims=True)
        dhn = dout * gfv
        dh = rstd * (dhn - hn * jnp.mean(dhn * hn, axis=-1, keepdims=True))
        dh_ref[...] = dh
        dhb = dh.astype(BF16)
        dy_ref[...] = _dot_nt(dhb, w)
        dw_ref[...] += _dot_tn(y, dhb)

    return pl.pallas_call(
        body, name="outproj_loss", grid=(T // tm,),
        in_specs=[pl.BlockSpec((tm, D_MODEL), lambda i: (i, 0)), pl.BlockSpec((tm, W), lambda i: (i, 0)),
                  pl.BlockSpec((tm, W), lambda i: (i, 0)), _full((D_MODEL, D_MODEL)),
                  pl.BlockSpec((tm, D_MODEL), lambda i: (i, 0)), _full((1, D_MODEL))],
        out_specs=[_full((1, PACK_W)), pl.BlockSpec((tm, D_MODEL), lambda i: (i, 0)),
                   pl.BlockSpec((tm, D_MODEL), lambda i: (i, 0)), _full((D_MODEL, D_MODEL)), _full((1, D_MODEL))],
        out_shape=[jax.ShapeDtypeStruct((1, PACK_W), F32), jax.ShapeDtypeStruct((T, D_MODEL), F32),
                   jax.ShapeDtypeStruct((T, D_MODEL), F32), jax.ShapeDtypeStruct((D_MODEL, D_MODEL), F32),
                   jax.ShapeDtypeStruct((1, D_MODEL), F32)],
        compiler_params=_cparams(dimension_semantics=("arbitrary",)),
    )(x, y_ret, y_rw, w_out_b, target, gf)


def _inproj_bwd_x(dp_ret, dp_rw, dprev, dfirst, w_t, x, norm_g, dh):
    T = x.shape[0]
    tm = _row_tile(T)
    nt = T // tm

    def body(dpr_ref, dpw_ref, dprev_ref, dnext_ref, w_ref, x_ref, g_ref, dh_ref, gx_ref, dg_ref, dpt_ref):
        @pl.when(pl.program_id(0) == 0)
        def _():
            dg_ref[...] = jnp.zeros_like(dg_ref)

        next_row = jnp.where(pl.program_id(0) == nt - 1, 0.0, dnext_ref[0])
        dp = jnp.concatenate([dpr_ref[...], dpw_ref[...] + _shift_up(dprev_ref[...], next_row)], axis=1)
        dpt_ref[...] = jnp.transpose(dp).astype(BF16)
        du = _dot(dp.astype(BF16), w_ref[...])
        xf = x_ref[...]
        rstd = lax.rsqrt(jnp.mean(xf * xf, axis=-1, keepdims=True) + RMS_EPS)
        xn = xf * rstd
        dg_ref[...] += jnp.sum(du * xn, axis=0, keepdims=True)
        dxn = du * g_ref[...]
        gx_ref[...] = dh_ref[...] + rstd * (dxn - xn * jnp.mean(dxn * xn, axis=-1, keepdims=True))

    return pl.pallas_call(
        body, name="inproj_bwd_x", grid=(T // tm,),
        in_specs=[pl.BlockSpec((tm, RET_COLS), lambda i: (i, 0)), pl.BlockSpec((tm, RW_COLS), lambda i: (i, 0)),
                  pl.BlockSpec((tm, RW_COLS), lambda i: (i, 0)),
                  pl.BlockSpec((1, 1, RW_COLS), lambda i: (jnp.minimum(i + 1, nt - 1), 0, 0)),
                  _full((IN_COLS, D_MODEL)), pl.BlockSpec((tm, D_MODEL), lambda i: (i, 0)), _full((1, D_MODEL)),
                  pl.BlockSpec((tm, D_MODEL), lambda i: (i, 0))],
        out_specs=[pl.BlockSpec((tm, D_MODEL), lambda i: (i, 0)), _full((1, D_MODEL)),
                   pl.BlockSpec((IN_COLS, tm), lambda i: (0, i))],
        out_shape=[jax.ShapeDtypeStruct((T, D_MODEL), F32), jax.ShapeDtypeStruct((1, D_MODEL), F32),
                   jax.ShapeDtypeStruct((IN_COLS, T), BF16)],
        compiler_params=_cparams(dimension_semantics=("arbitrary",)),
    )(dp_ret, dp_rw, dprev, dfirst, w_t, x, norm_g, dh)


def _inproj_bwd_w(dp_t, u):
    T = u.shape[0]

    def body(d_ref, u_ref, o_ref):
        o_ref[...] = _dot(d_ref[...], u_ref[...])

    return pl.pallas_call(
        body, name="inproj_bwd_w", grid=(N_CHIPS,),
        in_specs=[pl.BlockSpec((IN_SHARD, T), lambda i: (i, 0)), _full((T, D_MODEL))],
        out_specs=pl.BlockSpec((IN_SHARD, D_MODEL), lambda i: (i, 0)),
        out_shape=jax.ShapeDtypeStruct((IN_COLS, D_MODEL), F32),
        compiler_params=_cparams(dimension_semantics=("arbitrary",)),
    )(dp_t, u)


def _with_own(gathered, own, chip):
    return lax.dynamic_update_slice(gathered, own[None], (chip, 0, 0, 0))


def _local_step(x, target, w_in_t, late, chip, small):
    T = x.shape[0]
    tm = _row_tile(T)
    W = RW_WIDTH
    cos, sin = _rope_tables(T)
    tabs = _ret_tables()
    seg64 = _seg_matrix(RW_WIDTH, RW_HEAD)
    e_tab = _wkv_expand_table()
    r_tab = _wkv_reduce_table()

    p_ret, p_rw, u, bnd, *gathered = _inproj_fwd(x, small["norm_g"], w_in_t, late)
    g_out, g_lw, g_la = (_with_own(g, own, chip) for g, own in zip(gathered, late))
    w_out_b = g_out.reshape(D_MODEL, D_MODEL)
    lw = jnp.transpose(g_lw.reshape(N_CHIPS, LORA, LORA_SHARD), (1, 0, 2)).reshape(LORA, W)
    la = jnp.transpose(g_la.reshape(N_CHIPS, LORA, LORA_SHARD), (1, 0, 2)).reshape(LORA, W)
    zero = jnp.zeros((LORA, W), F32)
    lora = jnp.concatenate([jnp.concatenate([lw, zero], axis=1), jnp.concatenate([zero, la], axis=1)], axis=0)
    prep_w = (small["rwkv_mu"], small["w0"], small["a0"], small["k_k"], small["k_a"], lora, seg64, _chunk_tables(tm))
    post_w = (small["rwkv_gn_g"], small["rwkv_gn_b"], small["r_k"], seg64)

    y_ret, ret, s_in_all = _ret_fwd(p_ret, cos, sin, tabs, small["ret_gn_g"])
    hm, v, g, rk = _prep_fwd(p_rw, bnd, *prep_w)
    o, states, sa = _wkv_fwd(hm, v, e_tab)
    y_rw = _post_fwd(o, rk, v, g, *post_w)
    loss, dh, dy, d_w_out, d_gf = _outproj(x, y_ret, y_rw, w_out_b, target, small["final_norm_g"])

    do, d_rk, dv2, dg, d_gn_g, d_gn_b, d_r_k = _post_bwd(o, rk, v, g, *post_w, dy)
    dv1, d_hm = _wkv_bwd(hm, v, do, states, sa, e_tab, r_tab)
    dp_rw, dprev, dfirst, d_mu, d_w0, d_a0, d_k_k, d_k_a, d_lora = _prep_bwd(
        p_rw, bnd, *prep_w, (d_hm, d_rk, dv1, dv2, dg))
    dp_ret, d_ret_gn = _ret_bwd(p_ret, cos, sin, tabs, small["ret_gn_g"], ret, s_in_all, dy)
    grad_x, d_norm_g, dp_t = _inproj_bwd_x(dp_ret, dp_rw, dprev, dfirst, w_in_t, x, small["norm_g"], dh)
    d_w_in = _inproj_bwd_w(dp_t, u)

    d_small = {"norm_g": d_norm_g, "ret_gn_g": d_ret_gn, "rwkv_mu": d_mu, "w0": d_w0, "a0": d_a0, "k_k": d_k_k,
               "k_a": d_k_a, "r_k": d_r_k, "rwkv_gn_g": d_gn_g, "rwkv_gn_b": d_gn_b, "final_norm_g": d_gf}
    return loss, grad_x, d_w_in, d_w_out, d_lora, d_small


ANY = pl.BlockSpec(memory_space=pl.ANY)
CHIP_FLIPS = ((0, 1), (1, 0), (1, 1))
N_FLIPS = len(CHIP_FLIPS)
LORA_SHARD = RW_WIDTH // N_CHIPS
HALF_IN = IN_SHARD // 2
HALF_OUT = OUT_SHARD // 2


def _position():
    return lax.axis_index("x"), lax.axis_index("y"), lax.axis_index("c")


def _flip(v, f):
    return 1 - v if f else v


def _finish(local, remote, landed):
    for cp in landed:
        cp.wait_recv()
    for cp in remote:
        cp.wait_send()
    for cp in local:
        cp.wait()


def _gather_copies(ins, outs, sems):
    send, recv, pass_send, pass_recv = sems
    x, y, c = _position()
    s = 2 * x + y
    sibling = (x, y, 1 - c)
    first, second = (1 - c, c), (c, 1 - c)
    (x1, y1), (x2, y2) = ((x + fx - 2 * x * fx, y + fy - 2 * y * fy) for fx, fy in (first, second))
    s1, s2, sd = 2 * x1 + y1, 2 * x2 + y2, 2 * (1 - x) + (1 - y)

    def copy(src, dst, pair, k, to):
        return pltpu.make_async_remote_copy(src_ref=src, dst_ref=dst, send_sem=pair[0].at[k], recv_sem=pair[1].at[k],
                                            device_id=to, device_id_type=MESH)

    at_once, direct, relayed, passed_in, sends = [], [], [], [], []
    for a in range(len(ins)):
        k = a * N_FLIPS
        own, out = ins[a].at[c], outs[a]
        ici = (send, recv)
        to_first = copy(own, out.at[s, c], ici, k, (x1, y1, c))
        to_second = copy(own, out.at[s, c], ici, k + 1, (x2, y2, c))
        relay = copy(out.at[s1, c], out.at[s1, c], ici, k + 2, (x2, y2, c))
        passes = [copy(out.at[slot, c], out.at[slot, c], (pass_send, pass_recv), k + j, sibling)
                  for j, slot in enumerate((s1, s2, sd))]
        at_once += [to_first, to_second]
        direct += [(copy(own, out.at[s1, c], ici, k, (x1, y1, c)), [relay, passes[0]]),
                   (copy(own, out.at[s2, c], ici, k + 1, (x2, y2, c)), [passes[1]])]
        relayed += [(copy(own, out.at[sd, c], ici, k + 2, (x2, y2, c)), [passes[2]])]
        passed_in += [copy(out.at[slot, 1 - c], out.at[slot, 1 - c], (pass_send, pass_recv), k + j, sibling)
                      for j, slot in enumerate((s2, s1, sd))]
        sends += [to_first, to_second, relay] + passes
    return at_once, direct, relayed, passed_in, sends


def _then(steps):
    for arrived, then in steps:
        arrived.wait_recv()
        for cp in then:
            cp.start()


def _gather_start(copies):
    for cp in copies[0]:
        cp.start()


def _gather_relay(copies):
    _then(copies[1])


def _gather_finish(copies):
    _, _, relayed, passed_in, sends = copies
    _then(relayed)
    _finish([], sends, passed_in)


def _gather_sems(n):
    return [pltpu.SemaphoreType.DMA((n * N_FLIPS,))] * 4


def _gather_shapes(arrs):
    return [jax.ShapeDtypeStruct((N_CHIPS,) + a.shape, a.dtype) for a in arrs]


def _gather_chips(arrs):
    n = len(arrs)

    def body(*refs):
        copies = _gather_copies(refs[:n], refs[n:2 * n], refs[2 * n:])
        _gather_start(copies)
        _gather_relay(copies)
        _gather_finish(copies)

    return pl.pallas_call(
        body, name="gather_weights",
        in_specs=[ANY] * n, out_specs=[ANY] * n,
        out_shape=_gather_shapes(arrs), scratch_shapes=_gather_sems(n),
    )(*arrs)


def _pair_exchange(g_in, g_out, g_small):
    def body(gi_ref, go_ref, gs_ref, li_ref, lo_ref, ls_ref, send, recv):
        x, y, c = _position()
        peer = (x, y, 1 - c)
        srcs = (gi_ref.at[:, pl.ds((1 - c) * HALF_IN, HALF_IN), :], go_ref.at[:, pl.ds((1 - c) * HALF_OUT, HALF_OUT), :],
                gs_ref)
        remote = [pltpu.make_async_remote_copy(src_ref=src, dst_ref=dst, send_sem=send.at[k], recv_sem=recv.at[k],
                                               device_id=peer, device_id_type=MESH)
                  for k, (src, dst) in enumerate(zip(srcs, (li_ref, lo_ref, ls_ref)))]
        for cp in remote:
            cp.start()
        _finish([], remote, remote)

    return pl.pallas_call(
        body, name="pair_exchange",
        in_specs=[ANY] * 3, out_specs=[ANY] * 3,
        out_shape=[jax.ShapeDtypeStruct((N_CHIPS, HALF_IN, D_MODEL), F32),
                   jax.ShapeDtypeStruct((N_CHIPS, HALF_OUT, D_MODEL), F32),
                   jax.ShapeDtypeStruct(g_small.shape, F32)],
        scratch_shapes=[pltpu.SemaphoreType.DMA((3,)), pltpu.SemaphoreType.DMA((3,))],
    )(g_in, g_out, g_small)


def _pair_sum(g_in, g_out, g_small, l_in, l_out, l_small, c_arr):
    def body(c_ref, gi_ref, go_ref, gs_ref, li_ref, lo_ref, ls_ref, ci_ref, co_ref, cs_ref):
        ci_ref[...] = (gi_ref[...] + li_ref[...]).astype(BF16)
        co_ref[...] = (go_ref[...] + lo_ref[...]).astype(BF16)

        @pl.when(pl.program_id(0) == 0)
        def _():
            cs_ref[...] = gs_ref[...] + ls_ref[...]

    nd = g_small.shape
    return pl.pallas_call(
        body, name="pair_sum",
        grid_spec=pltpu.PrefetchScalarGridSpec(
            num_scalar_prefetch=1, grid=(N_CHIPS,),
            in_specs=[pl.BlockSpec((1, HALF_IN, D_MODEL), lambda s, c: (s, c[0], 0)),
                      pl.BlockSpec((1, HALF_OUT, D_MODEL), lambda s, c: (s, c[0], 0)),
                      pl.BlockSpec(nd, lambda s, c: (0, 0)),
                      pl.BlockSpec((1, HALF_IN, D_MODEL), lambda s, c: (s, 0, 0)),
                      pl.BlockSpec((1, HALF_OUT, D_MODEL), lambda s, c: (s, 0, 0)),
                      pl.BlockSpec(nd, lambda s, c: (0, 0))],
            out_specs=[pl.BlockSpec((1, HALF_IN, D_MODEL), lambda s, c: (s, 0, 0)),
                       pl.BlockSpec((1, HALF_OUT, D_MODEL), lambda s, c: (s, 0, 0)),
                       pl.BlockSpec(nd, lambda s, c: (0, 0))]),
        out_shape=[jax.ShapeDtypeStruct((N_CHIPS, HALF_IN, D_MODEL), BF16),
                   jax.ShapeDtypeStruct((N_CHIPS, HALF_OUT, D_MODEL), BF16), jax.ShapeDtypeStruct(nd, F32)],
        compiler_params=_cparams(dimension_semantics=("arbitrary",)),
    )(c_arr, g_in, g_out, g_small, l_in, l_out, l_small)


def _chip_exchange(c_in, c_out, c_small):
    def body(ci_ref, co_ref, cs_ref, li_ref, lo_ref, ls_ref, send, recv):
        x, y, c = _position()
        s = 2 * x + y
        remote = []
        for j, (fx, fy) in enumerate(CHIP_FLIPS):
            px, py = _flip(x, fx), _flip(y, fy)
            ps = 2 * px + py
            for a, (src, dst) in enumerate(((ci_ref.at[ps], li_ref.at[j]), (co_ref.at[ps], lo_ref.at[j]),
                                            (cs_ref, ls_ref.at[j]))):
                k = 3 * j + a
                remote.append(pltpu.make_async_remote_copy(src_ref=src, dst_ref=dst, send_sem=send.at[k],
                                                           recv_sem=recv.at[k], device_id=(px, py, c),
                                                           device_id_type=MESH))
        for cp in remote:
            cp.start()
        _finish([], remote, remote)

    return pl.pallas_call(
        body, name="chip_exchange",
        in_specs=[ANY] * 3, out_specs=[ANY] * 3,
        out_shape=[jax.ShapeDtypeStruct((N_FLIPS, HALF_IN, D_MODEL), c_in.dtype),
                   jax.ShapeDtypeStruct((N_FLIPS, HALF_OUT, D_MODEL), c_out.dtype),
                   jax.ShapeDtypeStruct((N_FLIPS,) + c_small.shape, F32)],
        scratch_shapes=[pltpu.SemaphoreType.DMA((3 * N_FLIPS,)), pltpu.SemaphoreType.DMA((3 * N_FLIPS,))],
    )(c_in, c_out, c_small)


def _chip_sum(g_in, g_out, p_in, p_out, c_small, l_in, l_out, l_small, sc_arr):
    nd = c_small.shape

    def body(s_ref, gi_ref, go_ref, pi_ref, po_ref, cs_ref, li0, li1, li2, lo0, lo1, lo2, ls_ref,
             ri_ref, ro_ref, rs_ref):
        ri_ref[...] = (((gi_ref[0] + pi_ref[0]) + li0[0].astype(F32)) + li1[0].astype(F32)) + li2[0].astype(F32)
        ro_ref[...] = (((go_ref[0] + po_ref[0]) + lo0[0].astype(F32)) + lo1[0].astype(F32)) + lo2[0].astype(F32)
        me = s_ref[0]
        parts = (cs_ref[...], ls_ref[0], ls_ref[1], ls_ref[2])

        def of_chip(s):
            m = jnp.bitwise_xor(me, s)
            return jnp.where(m == 0, parts[0], jnp.where(m == 1, parts[1], jnp.where(m == 2, parts[2], parts[3])))

        rs_ref[...] = ((of_chip(0) + of_chip(1)) + of_chip(2)) + of_chip(3)

    def flip_in(j):
        return pl.BlockSpec((1, HALF_IN, D_MODEL), lambda i, s: (j, 0, 0))

    def flip_out(j):
        return pl.BlockSpec((1, HALF_OUT, D_MODEL), lambda i, s: (j, 0, 0))

    return pl.pallas_call(
        body, name="chip_sum",
        grid_spec=pltpu.PrefetchScalarGridSpec(
            num_scalar_prefetch=1, grid=(1,),
            in_specs=[pl.BlockSpec((1, HALF_IN, D_MODEL), lambda i, s: (s[0], s[1], 0)),
                      pl.BlockSpec((1, HALF_OUT, D_MODEL), lambda i, s: (s[0], s[1], 0)),
                      pl.BlockSpec((1, HALF_IN, D_MODEL), lambda i, s: (s[0], 0, 0)),
                      pl.BlockSpec((1, HALF_OUT, D_MODEL), lambda i, s: (s[0], 0, 0)),
                      pl.BlockSpec(nd, lambda i, s: (0, 0)),
                      flip_in(0), flip_in(1), flip_in(2), flip_out(0), flip_out(1), flip_out(2),
                      pl.BlockSpec((N_FLIPS,) + nd, lambda i, s: (0, 0, 0))],
            out_specs=[pl.BlockSpec((HALF_IN, D_MODEL), lambda i, s: (0, 0)),
                       pl.BlockSpec((HALF_OUT, D_MODEL), lambda i, s: (0, 0)),
                       pl.BlockSpec(nd, lambda i, s: (0, 0))]),
        out_shape=[jax.ShapeDtypeStruct((HALF_IN, D_MODEL), F32), jax.ShapeDtypeStruct((HALF_OUT, D_MODEL), F32),
                   jax.ShapeDtypeStruct(nd, F32)],
        compiler_params=_cparams(dimension_semantics=("arbitrary",)),
    )(sc_arr, g_in, g_out, p_in, p_out, c_small, l_in, l_in, l_in, l_out, l_out, l_out, l_small)


def _pair_share(r_in, r_out):
    def body(ri_ref, ro_ref, li_ref, lo_ref, send, recv):
        x, y, c = _position()
        remote = [pltpu.make_async_remote_copy(src_ref=src, dst_ref=dst, send_sem=send.at[k], recv_sem=recv.at[k],
                                               device_id=(x, y, 1 - c), device_id_type=MESH)
                  for k, (src, dst) in enumerate(((ri_ref, li_ref), (ro_ref, lo_ref)))]
        for cp in remote:
            cp.start()
        _finish([], remote, remote)

    return pl.pallas_call(
        body, name="pair_share",
        in_specs=[ANY] * 2, out_specs=[ANY] * 2,
        out_shape=[jax.ShapeDtypeStruct(r_in.shape, F32), jax.ShapeDtypeStruct(r_out.shape, F32)],
        scratch_shapes=[pltpu.SemaphoreType.DMA((2,)), pltpu.SemaphoreType.DMA((2,))],
    )(r_in, r_out)


def _adam_update(w, g, m, v):
    mn = ADAM_B1 * m + (1.0 - ADAM_B1) * g
    vn = ADAM_B2 * v + (1.0 - ADAM_B2) * jnp.square(g)
    m_hat = mn / (1.0 - ADAM_B1 ** ADAM_STEP)
    v_hat = vn / (1.0 - ADAM_B2 ** ADAM_STEP)
    return -ADAM_LR * (m_hat / (jnp.sqrt(v_hat) + ADAM_EPS) + ADAM_WD * w), mn, vn


def _adamw(name, w, g_mine, g_theirs, m, v, core_arr, tr):
    rows, cols = w.shape
    per_half = rows // 2 // tr

    def body(c_ref, w_ref, gm_ref, gt_ref, m_ref, v_ref, g_ref, d_ref, nm_ref, nv_ref):
        mine = (pl.program_id(0) // per_half) == c_ref[0]
        g = jnp.where(mine, gm_ref[...], gt_ref[...])
        d, mn, vn = _adam_update(w_ref[...], g, m_ref[...], v_ref[...])
        g_ref[...] = g
        d_ref[...] = d
        nm_ref[...] = mn
        nv_ref[...] = vn

    spec = pl.BlockSpec((tr, cols), lambda i, c: (i, 0))
    half = pl.BlockSpec((tr, cols), lambda i, c: (i % per_half, 0))
    return pl.pallas_call(
        body, name=name,
        grid_spec=pltpu.PrefetchScalarGridSpec(
            num_scalar_prefetch=1, grid=(rows // tr,),
            in_specs=[spec, half, half, spec, spec], out_specs=[spec] * 4),
        out_shape=[jax.ShapeDtypeStruct((rows, cols), F32)] * 4,
        compiler_params=_cparams(dimension_semantics=("arbitrary",)),
    )(core_arr, w, g_mine, g_theirs, m, v)


def _row_pieces(n):
    return [(k, k * PACK_W, min(PACK_W, n - k * PACK_W)) for k in range(-(-n // PACK_W))]


def _pack_small(d_small, loss, d_lora):
    ns = len(SMALL_NAMES)

    def body(*refs):
        small_refs, (loss_ref, lora_ref, out_ref) = refs[:ns], refs[ns:]
        out_ref[...] = jnp.zeros_like(out_ref)
        out_ref[PACK_LORA_W:PACK_LORA_W + LORA, :] = lora_ref[:LORA, :RW_WIDTH]
        out_ref[PACK_LORA_A:PACK_LORA_A + LORA, :] = lora_ref[LORA:, RW_WIDTH:]
        for name, n, ref in zip(SMALL_NAMES, SMALL_SIZES, small_refs):
            for k, at, w in _row_pieces(n):
                out_ref[PACK_AT[name] + k:PACK_AT[name] + k + 1, 0:w] = ref[:, at:at + w]
        out_ref[PACK_LOSS:PACK_LOSS + 1, :] = loss_ref[...]

    return pl.pallas_call(body, name="pack_small", out_shape=jax.ShapeDtypeStruct((PACK_ROWS, PACK_W), F32),
                          compiler_params=_cparams())(*d_small, loss, d_lora)


def _adamw_small(tot, chip_arr, ws, ms, vs):
    ns = len(SMALL_NAMES)
    n_par = ns + 2

    def body(s_ref, tot_ref, glw_ref, gla_ref, *refs):
        w_refs, m_refs, v_refs = refs[:n_par], refs[n_par:2 * n_par], refs[2 * n_par:3 * n_par]
        outs = refs[3 * n_par:]
        g_refs, d_refs, nm_refs, nv_refs = (outs[i * n_par:(i + 1) * n_par] for i in range(4))
        grads = [jnp.concatenate([tot_ref[PACK_AT[name] + k:PACK_AT[name] + k + 1, 0:w] for k, _, w in _row_pieces(n)],
                                 axis=1) for name, n in zip(SMALL_NAMES, SMALL_SIZES)]
        grads += [glw_ref[...], gla_ref[...]]
        for i, g in enumerate(grads):
            d, mn, vn = _adam_update(w_refs[i][...], g, m_refs[i][...], v_refs[i][...])
            g_refs[i][...] = g
            d_refs[i][...] = d
            nm_refs[i][...] = mn
            nv_refs[i][...] = vn

    def whole(a):
        nd = a.ndim
        return pl.BlockSpec(a.shape, lambda i, s: (0,) * nd)

    shard = (LORA, LORA_SHARD)
    par_specs = [whole(a) for a in ws]
    res = pl.pallas_call(
        body, name="adamw_small",
        grid_spec=pltpu.PrefetchScalarGridSpec(
            num_scalar_prefetch=1, grid=(1,),
            in_specs=[whole(tot), pl.BlockSpec(shard, lambda i, s: (PACK_LORA_W // LORA, s[0])),
                      pl.BlockSpec(shard, lambda i, s: (PACK_LORA_A // LORA, s[0]))] + par_specs * 3,
            out_specs=par_specs * 4),
        out_shape=[jax.ShapeDtypeStruct(a.shape, F32) for a in ws] * 4,
        compiler_params=_cparams(dimension_semantics=("arbitrary",)),
    )(chip_arr, tot, tot, tot, *ws, *ms, *vs)
    return [res[i * n_par:(i + 1) * n_par] for i in range(4)]


def kernel(x, norm_g, w_in, ret_gn_g, rwkv_mu, w_lora_up, w0, a_lora_up, a0, k_k, k_a, r_k, rwkv_gn_g, rwkv_gn_b, w_out, final_norm_g, loss_target, m_norm_g, m_w_in, m_ret_gn_g, m_rwkv_mu, m_w_lora_up, m_w0, m_a_lora_up, m_a0, m_k_k, m_k_a, m_r_k, m_rwkv_gn_g, m_rwkv_gn_b, m_w_out, m_final_norm_g, v_norm_g, v_w_in, v_ret_gn_g, v_rwkv_mu, v_w_lora_up, v_w0, v_a_lora_up, v_a0, v_k_k, v_k_a, v_r_k, v_rwkv_gn_g, v_rwkv_gn_b, v_w_out, v_final_norm_g):
    W = RW_WIDTH
    params = dict(norm_g=norm_g, ret_gn_g=ret_gn_g, rwkv_mu=rwkv_mu, w0=w0, a0=a0, k_k=k_k, k_a=k_a, r_k=r_k,
                  rwkv_gn_g=rwkv_gn_g, rwkv_gn_b=rwkv_gn_b, final_norm_g=final_norm_g)
    moments_m = dict(norm_g=m_norm_g, ret_gn_g=m_ret_gn_g, rwkv_mu=m_rwkv_mu, w0=m_w0, a0=m_a0, k_k=m_k_k, k_a=m_k_a,
                     r_k=m_r_k, rwkv_gn_g=m_rwkv_gn_g, rwkv_gn_b=m_rwkv_gn_b, final_norm_g=m_final_norm_g)
    moments_v = dict(norm_g=v_norm_g, ret_gn_g=v_ret_gn_g, rwkv_mu=v_rwkv_mu, w0=v_w0, a0=v_a0, k_k=v_k_k, k_a=v_k_a,
                     r_k=v_r_k, rwkv_gn_g=v_rwkv_gn_g, rwkv_gn_b=v_rwkv_gn_b, final_norm_g=v_final_norm_g)
    xi, yi, ci = _position()
    chip = (2 * xi + yi).astype(jnp.int32)

    def halves(a):
        return a.reshape(2, a.shape[0] // 2, a.shape[1])

    w_t, m_t, v_t = (jnp.transpose(a[0]) for a in (w_in, m_w_in, v_w_in))
    own_in = halves(w_t.astype(BF16))
    w_in_t = _with_own(_gather_chips([own_in])[0], own_in, chip).reshape(IN_COLS, D_MODEL)
    late = [halves(w_out[0].astype(BF16)), halves(w_lora_up[0]), halves(a_lora_up[0])]
    small = {n: params[n].reshape(1, -1) for n in SMALL_NAMES}

    loss, grad_x, d_w_in, d_w_out, d_lora, d_small = _local_step(x[0], loss_target[0], w_in_t, late, chip, small)

    core = ci.astype(jnp.int32)
    gi = d_w_in.reshape(N_CHIPS, IN_SHARD, D_MODEL)
    go = d_w_out.reshape(N_CHIPS, OUT_SHARD, D_MODEL)
    gs = _pack_small([d_small[n] for n in SMALL_NAMES], loss, d_lora)
    p_in, p_out, p_small = _pair_exchange(gi, go, gs)
    c_in, c_out, c_small = _pair_sum(gi, go, gs, p_in, p_out, p_small, core.reshape(1))
    l_in, l_out, l_small = _chip_exchange(c_in, c_out, c_small)
    r_in, r_out, tot = _chip_sum(gi, go, p_in, p_out, c_small, l_in, l_out, l_small, jnp.stack([chip, core]))
    t_in, t_out = _pair_share(r_in, r_out)

    grad_w_in, d_in, nm_in, nv_in = (jnp.transpose(a) for a in _adamw(
        "adamw_w_in", w_t, r_in, t_in, m_t, v_t, core.reshape(1), HALF_IN // 2))
    grad_w_out, d_out, nm_out, nv_out = _adamw("adamw_w_out", w_out[0], r_out, t_out, m_w_out[0], v_w_out[0],
                                               core.reshape(1), HALF_OUT)
    par_names = SMALL_NAMES + ("w_lora_up", "a_lora_up")

    def operands(tree, lw_, la_):
        return [tree[n].reshape(1, -1) for n in SMALL_NAMES] + [lw_[0], la_[0]]

    res = _adamw_small(tot, chip.reshape(1), operands(params, w_lora_up, a_lora_up),
                       operands(moments_m, m_w_lora_up, m_a_lora_up), operands(moments_v, v_w_lora_up, v_a_lora_up))

    names = ("norm_g", "w_in", "ret_gn_g", "rwkv_mu", "w_lora_up", "w0", "a_lora_up", "a0", "k_k", "k_a", "r_k",
             "rwkv_gn_g", "rwkv_gn_b", "w_out", "final_norm_g")
    shapes = dict(w_in=w_in.shape, w_out=w_out.shape, w_lora_up=w_lora_up.shape, a_lora_up=a_lora_up.shape,
                  **{n: params[n].shape for n in SMALL_NAMES})

    def leaves(pars, big_in, big_out):
        tree = dict(zip(par_names, pars), w_in=big_in, w_out=big_out)
        return [tree[n].reshape(shapes[n]) for n in names]

    grads = leaves(res[0], grad_w_in, grad_w_out)
    deltas = leaves(res[1], d_in, d_out)
    new_m = leaves(res[2], nm_in, nm_out)
    new_v = leaves(res[3], nv_in, nv_out)
    return (tot[PACK_LOSS, 0], grad_x.reshape(x.shape), *grads, *deltas, *new_m, *new_v)
```

```python
import functools

import numpy as np
import jax
import jax.numpy as jnp
from jax import lax
from jax.experimental import pallas as pl
from jax.experimental.pallas import tpu as pltpu

F32 = jnp.float32
BF16 = jnp.bfloat16
X3 = "bf16x3"
B1 = "bf16"
MESH = pl.DeviceIdType.MESH

D_MODEL = 1024
N_CHIPS = 4
RET_HEADS = 4
RET_DK = 64
RET_DV = 128
RET_QK = RET_HEADS * RET_DK
RET_WIDTH = RET_HEADS * RET_DV
RET_COLS = 2 * RET_QK + 2 * RET_WIDTH
RET_CHUNK = 64
RET_GROUP = 8
RW_WIDTH = 512
RW_HEAD = 64
RW_HEADS = 8
LORA = 64
RW_COLS = 4 * RW_WIDTH + 2 * LORA
IN_COLS = RET_COLS + RW_COLS
IN_SHARD = IN_COLS // N_CHIPS
OUT_SHARD = D_MODEL // N_CHIPS
ROPE_BASE = 10000.0
RMS_EPS = 1e-6
RET_GN_EPS = 1e-5
RW_GN_EPS = 64e-5
WKV_CHUNK = 16
WKV_GROUP = 8
N_VEC = 5

ADAM_LR = 0.001
ADAM_B1 = 0.9
ADAM_B2 = 0.999
ADAM_EPS = 1e-08
ADAM_WD = 0.01
ADAM_STEP = 10

VMEM_LIMIT = 56 * 1024 * 1024

PACK_W = 512
SMALL_NAMES = ("norm_g", "ret_gn_g", "rwkv_mu", "w0", "a0", "k_k", "k_a", "r_k", "rwkv_gn_g", "rwkv_gn_b",
               "final_norm_g")
SMALL_SIZES = (1024, 512, 2176, 512, 512, 512, 512, 512, 512, 512, 1024)
PACK_LORA_W = 0
PACK_LORA_A = LORA
PACK_SMALL = 2 * LORA


def _pack_layout():
    rows, at = {}, PACK_SMALL
    for name, n in zip(SMALL_NAMES, SMALL_SIZES):
        rows[name] = at
        at += -(-n // PACK_W)
    return rows, at


PACK_AT, PACK_LOSS = _pack_layout()
PACK_ROWS = -(-(PACK_LOSS + 1) // 8) * 8


def _cparams(**kw):
    return pltpu.CompilerParams(vmem_limit_bytes=VMEM_LIMIT, **kw)


def _split(x):
    hi = x.astype(BF16)
    lo = (x - hi.astype(F32)).astype(BF16)
    return hi, lo


def _dot_dims(a, b, dims, precision):
    if precision == B1:
        a, b = a.astype(BF16), b.astype(BF16)
    if precision != X3:
        return lax.dot_general(a, b, dims, preferred_element_type=F32)
    (ah, al), (bh, bl) = _split(a), _split(b)
    dot = lambda u, w: lax.dot_general(u, w, dims, preferred_element_type=F32)
    return dot(ah, bh) + dot(ah, bl) + dot(al, bh)


def _dot(a, b, precision=None):
    return _dot_dims(a, b, (((1,), (0,)), ((), ())), precision)


def _dot_nt(a, b, precision=None):
    return _dot_dims(a, b, (((1,), (1,)), ((), ())), precision)


def _dot_tn(a, b, precision=None):
    return _dot_dims(a, b, (((0,), (0,)), ((), ())), precision)


@jax.custom_vjp
def _segsum(x, seg):
    hi, lo = _split(x)
    return _dot(hi, seg) + _dot(lo, seg)


def _segsum_fwd(x, seg):
    return _segsum(x, seg), seg


def _segsum_bwd(seg, ct):
    return _segsum(ct, seg), jnp.zeros_like(seg)


_segsum.defvjp(_segsum_fwd, _segsum_bwd)


def _softplus(z):
    return jnp.maximum(z, 0.0) + jnp.log(1.0 + jnp.exp(-jnp.abs(z)))


def _full(shape):
    nd = len(shape)
    return pl.BlockSpec(shape, lambda *_: (0,) * nd)


def _rope_tables(T):
    half = RET_DK // 2
    expo = -jnp.arange(half, dtype=F32) / jnp.float32(half)
    freqs = jnp.exp(expo * jnp.float32(np.log(ROPE_BASE)))
    ang = jnp.arange(T, dtype=jnp.int32).astype(F32)[:, None] * freqs[None, :]
    cos = jnp.tile(jnp.cos(ang), (1, 2 * RET_HEADS))
    sin = jnp.tile(jnp.sin(ang), (1, 2 * RET_HEADS))
    return cos, sin


def _ret_tables():
    H, C = RET_HEADS, RET_CHUNK
    hidx = jnp.arange(H, dtype=F32)
    lg = jnp.log(1.0 - jnp.exp2(-5.0 - hidx))
    idx = jnp.arange(C, dtype=F32)
    intra = jnp.exp(lg[:, None, None] * jnp.abs(idx[:, None] - idx[None, :]))
    q_dec = jnp.transpose(jnp.exp(lg[:, None] * (idx[None, :] + 1.0)))
    k_dec = jnp.transpose(jnp.exp(lg[:, None] * (C - 1.0 - idx[None, :])))
    chunk_dec = jnp.exp(lg * C)
    qd = jnp.repeat(q_dec, RET_DK, axis=1)
    kd = jnp.repeat(k_dec, RET_DK, axis=1)
    row_h = np.arange(RET_QK) // RET_DK
    col_h = np.arange(RET_WIDTH) // RET_DV
    bm = jnp.asarray((row_h[:, None] == col_h[None, :]).astype(np.float32))
    cd = bm * jnp.repeat(chunk_dec, RET_DK)[:, None]
    return intra, qd, kd, cd, bm


def _seg_matrix(width, head):
    h = np.arange(width) // head
    return jnp.asarray((h[:, None] == h[None, :]).astype(np.float32), dtype=BF16)


def _wkv_expand_table():
    Tc = WKV_CHUNK
    k = np.arange(2 * RW_HEADS * Tc)
    kh, kt = (k % (RW_HEADS * Tc)) // Tc, k % Tc
    nh = np.arange(RW_WIDTH) // RW_HEAD
    e = (kh[None, :, None] == nh[None, None, :]) & (kt[None, :, None] == np.arange(Tc)[:, None, None])
    return jnp.asarray(e.astype(np.float32), dtype=BF16)


def _wkv_reduce_table():
    Tc = WKV_CHUNK
    kh = np.arange(RW_WIDTH) // RW_HEAD
    n = np.arange(RW_HEADS * Tc)
    nh, nt = n // Tc, n % Tc
    r = (kh[None, :, None] == nh[None, None, :]) & (nt[None, None, :] == np.arange(Tc)[:, None, None])
    return jnp.asarray(r.astype(np.float32), dtype=BF16)


def _inproj_fwd(x, norm_g, w_t, late):
    T = x.shape[0]
    tm = _row_tile(T)
    nt = T // tm
    n = len(late)

    def body(x_ref, g_ref, w_ref, *refs):
        late_in, (pret_ref, prw_ref, u_ref, last_ref) = refs[:n], refs[n:n + 4]
        late_out, sems = refs[n + 4:2 * n + 4], refs[2 * n + 4:]

        @pl.when(pl.program_id(0) == 0)
        def _():
            _gather_start(_gather_copies(late_in, late_out, sems))

        xf = x_ref[...]
        rstd = lax.rsqrt(jnp.mean(xf * xf, axis=-1, keepdims=True) + RMS_EPS)
        ub = ((xf * rstd) * g_ref[...]).astype(BF16)
        u_ref[...] = ub
        pret_ref[...] = _dot_nt(ub, w_ref[:RET_COLS, :])
        p_rw = _dot_nt(ub, w_ref[RET_COLS:, :])
        prw_ref[...] = p_rw
        last_ref[0] = p_rw[tm - 1:tm, :]

        @pl.when(pl.program_id(0) == nt // 4)
        def _():
            _gather_relay(_gather_copies(late_in, late_out, sems))

        @pl.when(pl.program_id(0) == nt - 1)
        def _():
            _gather_finish(_gather_copies(late_in, late_out, sems))

    return pl.pallas_call(
        body, name="inproj_fwd", grid=(nt,),
        in_specs=[pl.BlockSpec((tm, D_MODEL), lambda i: (i, 0)), _full((1, D_MODEL)), _full((IN_COLS, D_MODEL))]
                 + [ANY] * n,
        out_specs=[pl.BlockSpec((tm, RET_COLS), lambda i: (i, 0)), pl.BlockSpec((tm, RW_COLS), lambda i: (i, 0)),
                   pl.BlockSpec((tm, D_MODEL), lambda i: (i, 0)), pl.BlockSpec((1, 1, RW_COLS), lambda i: (i, 0, 0))]
                  + [ANY] * n,
        out_shape=[jax.ShapeDtypeStruct((T, RET_COLS), F32), jax.ShapeDtypeStruct((T, RW_COLS), F32),
                   jax.ShapeDtypeStruct((T, D_MODEL), BF16), jax.ShapeDtypeStruct((nt, 1, RW_COLS), F32)]
                  + _gather_shapes(late),
        scratch_shapes=_gather_sems(n),
        compiler_params=_cparams(dimension_semantics=("arbitrary",)),
    )(x, norm_g, w_t, *late)


def _rot_half(x):
    n = x.shape[1]
    lane = lax.broadcasted_iota(jnp.int32, x.shape, 1)
    first = (lane % RET_DK) < (RET_DK // 2)
    return jnp.where(first, -pltpu.roll(x, n - RET_DK // 2, 1), pltpu.roll(x, RET_DK // 2, 1))


def _rope(x, cos, sin):
    return x * cos + _rot_half(x) * sin


def _rope_bwd(d, cos, sin):
    return d * cos - _rot_half(d * sin)


def _ret_post(ret, g, gn_g):
    heads = []
    for h in range(RET_HEADS):
        xh = ret[:, h * RET_DV:(h + 1) * RET_DV]
        xc = xh - jnp.mean(xh, axis=-1, keepdims=True)
        heads.append(xc * lax.rsqrt(jnp.mean(xc * xc, axis=-1, keepdims=True) + RET_GN_EPS))
    return (g * jax.nn.sigmoid(g)) * (jnp.concatenate(heads, axis=1) * gn_g)


def _ret_scores(qt, kt, d_ref, h):
    lane = lax.broadcasted_iota(jnp.int32, qt.shape, 1)
    qh = jnp.where(lane // RET_DK == h, qt, 0.0)
    return qh, _dot_nt(qh, kt, B1) * d_ref[h]


def _ret_group(nch):
    return min(RET_GROUP, nch)


def _ret_fwd(p_ret, cos, sin, tabs, gn_g):
    T = p_ret.shape[0]
    C = RET_CHUNK
    nch = T // C
    G = _ret_group(nch)
    intra_d, qd, kd, cd, bm = tabs

    def body(q_ref, k_ref, v_ref, g_ref, cos_ref, sin_ref, qd_ref, kd_ref, d_ref, cd_ref, bm_ref, gn_ref,
             y_ref, ret_ref, sin_out_ref, s_ref, qt_buf, kv_buf):
        @pl.when(pl.program_id(0) == 0)
        def _():
            s_ref[...] = jnp.zeros_like(s_ref)

        cosv, sinv = cos_ref[...], sin_ref[...]
        qt_all = _rope(q_ref[...], cosv, sinv)
        kt_all = _rope(k_ref[...], cosv, sinv) * (RET_DK ** -0.5)
        for i in range(G):
            rows = slice(i * C, (i + 1) * C)
            qt, kt, v = qt_all[rows], kt_all[rows], v_ref[rows, :]
            intra = []
            for h in range(RET_HEADS):
                _, a = _ret_scores(qt, kt, d_ref, h)
                intra.append(_dot(a, v[:, h * RET_DV:(h + 1) * RET_DV], B1))
            ret_ref[rows, :] = jnp.concatenate(intra, axis=1)
            qt_buf[rows, :] = qt * qd_ref[...]
            kv_buf[i] = _dot_tn(kt * kd_ref[...], v, B1) * bm_ref[...]
        s_in = s_ref[...]
        for i in range(G):
            rows = slice(i * C, (i + 1) * C)
            sin_out_ref[i] = s_in
            ret_ref[rows, :] += _dot(qt_buf[rows, :], s_in, B1)
            s_in = s_in * cd_ref[...] + kv_buf[i]
        s_ref[...] = s_in
        y_ref[...] = _ret_post(ret_ref[...], g_ref[...], gn_ref[...]).astype(BF16)

    GC = G * C
    return pl.pallas_call(
        body, name="ret_fwd", grid=(nch // G,),
        in_specs=[pl.BlockSpec((GC, RET_QK), lambda c: (c, 0)), pl.BlockSpec((GC, RET_QK), lambda c: (c, 1)),
                  pl.BlockSpec((GC, RET_WIDTH), lambda c: (c, 1)), pl.BlockSpec((GC, RET_WIDTH), lambda c: (c, 2)),
                  pl.BlockSpec((GC, RET_QK), lambda c: (c, 0)), pl.BlockSpec((GC, RET_QK), lambda c: (c, 0)),
                  _full((C, RET_QK)), _full((C, RET_QK)), _full((RET_HEADS, C, C)),
                  _full((RET_QK, RET_WIDTH)), _full((RET_QK, RET_WIDTH)), _full((1, RET_WIDTH))],
        out_specs=[pl.BlockSpec((GC, RET_WIDTH), lambda c: (c, 0)), pl.BlockSpec((GC, RET_WIDTH), lambda c: (c, 0)),
                   pl.BlockSpec((G, RET_QK, RET_WIDTH), lambda c: (c, 0, 0))],
        out_shape=[jax.ShapeDtypeStruct((T, RET_WIDTH), BF16), jax.ShapeDtypeStruct((T, RET_WIDTH), F32),
                   jax.ShapeDtypeStruct((nch, RET_QK, RET_WIDTH), F32)],
        scratch_shapes=[pltpu.VMEM((RET_QK, RET_WIDTH), F32), pltpu.VMEM((GC, RET_QK), F32),
                        pltpu.VMEM((G, RET_QK, RET_WIDTH), F32)],
        compiler_params=_cparams(dimension_semantics=("arbitrary",)),
    )(p_ret, p_ret, p_ret, p_ret, cos, sin, qd, kd, intra_d, cd, bm, gn_g)


def _ret_bwd(p_ret, cos, sin, tabs, gn_g, ret, s_in_all, dy):
    T = p_ret.shape[0]
    C = RET_CHUNK
    nch = T // C
    G = _ret_group(nch)
    ngr = nch // G
    intra_d, qd, kd, cd, bm = tabs

    def rev(j):
        return lambda c: (ngr - 1 - c, j)

    def body(q_ref, k_ref, v_ref, g_ref, cos_ref, sin_ref, qd_ref, kd_ref, d_ref, cd_ref, bm_ref, gn_ref,
             ret_ref, sin_ref_, dy_ref, dp_ref, dgn_ref, ds_ref, dkt_buf, ktk_buf, g_buf):
        @pl.when(pl.program_id(0) == 0)
        def _():
            ds_ref[...] = jnp.zeros_like(ds_ref)
            dgn_ref[...] = jnp.zeros_like(dgn_ref)

        _, post_vjp = jax.vjp(_ret_post, ret_ref[...], g_ref[...], gn_ref[...])
        dret_all, dg_all, dgn = post_vjp(dy_ref[...])
        dgn_ref[...] += dgn
        v_cols = slice(2 * RET_QK, 2 * RET_QK + RET_WIDTH)
        dp_ref[:, 2 * RET_QK + RET_WIDTH:] = dg_all

        qdv, kdv = qd_ref[...], kd_ref[...]
        cosv, sinv = cos_ref[...], sin_ref[...]
        qt_all = _rope(q_ref[...], cosv, sinv)
        kt_all = _rope(k_ref[...], cosv, sinv) * (RET_DK ** -0.5)
        for i in range(G):
            rows = slice(i * C, (i + 1) * C)
            qt, kt, v, dret = qt_all[rows], kt_all[rows], v_ref[rows, :], dret_all[rows, :]
            dqt = qdv * _dot_nt(dret, sin_ref_[i], B1)
            dkt = jnp.zeros_like(kt)
            dvs = []
            for h in range(RET_HEADS):
                sl = slice(h * RET_DV, (h + 1) * RET_DV)
                qh, a = _ret_scores(qt, kt, d_ref, h)
                lane = lax.broadcasted_iota(jnp.int32, kt.shape, 1)
                kh = jnp.where(lane // RET_DK == h, kt, 0.0)
                da = _dot_nt(dret[:, sl], v[:, sl], B1) * d_ref[h]
                dvs.append(_dot_tn(a, dret[:, sl], B1))
                dqt = dqt + _dot(da, kh, B1)
                dkt = dkt + _dot_tn(da, qh, B1)
            dp_ref[rows, :RET_QK] = _rope_bwd(dqt, cosv[rows], sinv[rows])
            dp_ref[rows, v_cols] = jnp.concatenate(dvs, axis=1)
            dkt_buf[rows, :] = dkt
            ktk_buf[rows, :] = kt * kdv
            g_buf[i] = _dot_tn(qt * qdv, dret, B1) * bm_ref[...]
        ds_out = ds_ref[...]
        for i in reversed(range(G)):
            rows = slice(i * C, (i + 1) * C)
            dkt = dkt_buf[rows, :] + kdv * _dot_nt(v_ref[rows, :], ds_out, B1)
            dp_ref[rows, RET_QK:2 * RET_QK] = _rope_bwd(dkt * (RET_DK ** -0.5), cosv[rows], sinv[rows])
            dp_ref[rows, v_cols] += _dot(ktk_buf[rows, :], ds_out, B1)
            ds_out = ds_out * cd_ref[...] + g_buf[i]
        ds_ref[...] = ds_out

    GC = G * C
    return pl.pallas_call(
        body, name="ret_bwd", grid=(ngr,),
        in_specs=[pl.BlockSpec((GC, RET_QK), rev(0)), pl.BlockSpec((GC, RET_QK), rev(1)),
                  pl.BlockSpec((GC, RET_WIDTH), rev(1)), pl.BlockSpec((GC, RET_WIDTH), rev(2)),
                  pl.BlockSpec((GC, RET_QK), rev(0)), pl.BlockSpec((GC, RET_QK), rev(0)),
                  _full((C, RET_QK)), _full((C, RET_QK)), _full((RET_HEADS, C, C)),
                  _full((RET_QK, RET_WIDTH)), _full((RET_QK, RET_WIDTH)), _full((1, RET_WIDTH)),
                  pl.BlockSpec((GC, RET_WIDTH), rev(0)),
                  pl.BlockSpec((G, RET_QK, RET_WIDTH), lambda c: (ngr - 1 - c, 0, 0)),
                  pl.BlockSpec((GC, RET_WIDTH), rev(0))],
        out_specs=[pl.BlockSpec((GC, RET_COLS), rev(0)), _full((1, RET_WIDTH))],
        out_shape=[jax.ShapeDtypeStruct((T, RET_COLS), F32), jax.ShapeDtypeStruct((1, RET_WIDTH), F32)],
        scratch_shapes=[pltpu.VMEM((RET_QK, RET_WIDTH), F32), pltpu.VMEM((GC, RET_QK), F32),
                        pltpu.VMEM((GC, RET_QK), F32), pltpu.VMEM((G, RET_QK, RET_WIDTH), F32)],
        compiler_params=_cparams(dimension_semantics=("arbitrary",)),
    )(p_ret, p_ret, p_ret, p_ret, cos, sin, qd, kd, intra_d, cd, bm, gn_g, ret, s_in_all, dy)


@jax.custom_vjp
def _chunk_sums(x, tri):
    hi, lo = _split(x)
    return _dot(tri, hi) + _dot(tri, lo)


def _chunk_sums_fwd(x, tri):
    return _chunk_sums(x, tri), tri


def _chunk_sums_bwd(tri, ct):
    hi, lo = _split(ct)
    return _dot_tn(tri, hi) + _dot_tn(tri, lo), jnp.zeros_like(tri)


_chunk_sums.defvjp(_chunk_sums_fwd, _chunk_sums_bwd)


@jax.custom_vjp
def _lora_dot(z, lora):
    return _dot(z, lora, X3)


def _lora_dot_fwd(z, lora):
    return _lora_dot(z, lora), (z, lora)


def _lora_dot_bwd(res, ct):
    z, lora = res
    return _dot_nt(ct, lora, X3), _dot_tn(z, ct, X3)


_lora_dot.defvjp(_lora_dot_fwd, _lora_dot_bwd)


def _chunk_tables(tm):
    t = np.arange(tm)
    same = (t[:, None] // WKV_CHUNK) == (t[None, :] // WKV_CHUNK)
    return jnp.asarray(np.stack([same & (t[None, :] <= t[:, None]), same]).astype(np.float32), dtype=BF16)


def _prep_fn(p, prev, mu, w0, a0, k_k, k_a, lora, seg, tri):
    W = RW_WIDTH
    ps = p + mu * (prev - p)
    r, kr, vr, g = ps[:, 0:W], ps[:, W:2 * W], ps[:, 2 * W:3 * W], ps[:, 3 * W:4 * W]
    z = ps[:, 4 * W:]
    lane = lax.broadcasted_iota(jnp.int32, z.shape, 1)
    z = jnp.where(lane < LORA, jnp.tanh(z), z)
    lo = _lora_dot(z, lora)
    w_log = -_softplus(-(w0 + lo[:, :W])) - 0.5
    log_decay = -jnp.exp(w_log)
    cum = _chunk_sums(log_decay, tri[0])
    total = _chunk_sums(log_decay, tri[1])
    a = jax.nn.sigmoid(a0 + lo[:, W:])
    kk = kr * k_k
    kk = kk / jnp.maximum(jnp.sqrt(_segsum(kk * kk, seg)), 1e-12)
    k = kr * (1.0 + (a - 1.0) * k_a)
    grow = jnp.exp(-cum)
    return kk * jnp.exp(cum - log_decay), (kk * a) * grow, k * grow, r * jnp.exp(cum), jnp.exp(total), vr, g, r * k


def _post_fn(o, rk, v, g, gn_g, gn_b, r_k, seg):
    mu = _segsum(o, seg) * (1.0 / RW_HEAD)
    oc = o - mu
    var = _segsum(oc * oc, seg) * (1.0 / RW_HEAD)
    on = oc * lax.rsqrt(var + RW_GN_EPS) * gn_g + gn_b
    bonus = _segsum(rk * r_k, seg) * v
    return (g * jax.nn.sigmoid(g)) * (on + bonus)


N_PAIR = (N_VEC + 1) // 2
HALF_LANES = 64


def _swap_halves(x):
    return pltpu.roll(x, HALF_LANES, 1)


def _pack_heads(vecs):
    tm = vecs[0].shape[0]
    low = lax.broadcasted_iota(jnp.int32, (tm, 128), 1) < HALF_LANES
    out = []
    for p in range(N_PAIR):
        a = vecs[2 * p]
        b = vecs[2 * p + 1] if 2 * p + 1 < len(vecs) else None
        heads = []
        for m in range(RW_WIDTH // 128):
            am = a[:, m * 128:(m + 1) * 128]
            bm = jnp.zeros_like(am) if b is None else b[:, m * 128:(m + 1) * 128]
            heads.append(jnp.where(low, am, _swap_halves(bm)))
            heads.append(jnp.where(low, _swap_halves(am), bm))
        out.append(heads)
    return out


def _unpack_heads(hm_ref):
    tm = hm_ref.shape[2]
    low = lax.broadcasted_iota(jnp.int32, (tm, 128), 1) < HALF_LANES
    vecs = []
    for p in range(N_PAIR):
        a, b = [], []
        for m in range(RW_WIDTH // 128):
            even, odd = hm_ref[p, 2 * m], hm_ref[p, 2 * m + 1]
            a.append(jnp.where(low, even, _swap_halves(odd)))
            b.append(jnp.where(low, _swap_halves(even), odd))
        vecs += [jnp.concatenate(a, axis=1), jnp.concatenate(b, axis=1)]
    return vecs[:N_VEC]


def _shift_down(p, first_row):
    row = lax.broadcasted_iota(jnp.int32, p.shape, 0)
    return jnp.where(row == 0, first_row, pltpu.roll(p, 1, 0))


def _shift_up(p, last_row):
    n = p.shape[0]
    row = lax.broadcasted_iota(jnp.int32, p.shape, 0)
    return jnp.where(row == n - 1, last_row, pltpu.roll(p, n - 1, 0))


def _row_tile(T):
    return min(T, 256)


def _prep_fwd(p_rw, bnd, mu, w0, a0, k_k, k_a, lora, seg64, tri):
    T = p_rw.shape[0]
    tm = _row_tile(T)
    W = RW_WIDTH

    def body(p_ref, bnd_ref, mu_ref, w0_ref, a0_ref, kk_ref, ka_ref, lora_ref, seg_ref, tri_ref,
             hm_ref, v_ref, g_ref, rk_ref):
        p = p_ref[...]
        prev = _shift_down(p, jnp.where(pl.program_id(0) == 0, 0.0, bnd_ref[0]))
        res = _prep_fn(p, prev, mu_ref[...], w0_ref[...], a0_ref[...], kk_ref[...], ka_ref[...], lora_ref[...],
                       seg_ref[...], (tri_ref[0], tri_ref[1]))
        for pair, heads in enumerate(_pack_heads(res[:N_VEC])):
            for h, val in enumerate(heads):
                hm_ref[pair, h] = val
        v_ref[...] = res[N_VEC]
        g_ref[...] = res[N_VEC + 1]
        rk_ref[...] = res[N_VEC + 2]

    small = _full((1, W))
    row = pl.BlockSpec((tm, W), lambda i: (i, 0))
    return pl.pallas_call(
        body, name="rwkv_prep_fwd", grid=(T // tm,),
        in_specs=[pl.BlockSpec((tm, RW_COLS), lambda i: (i, 0)),
                  pl.BlockSpec((1, 1, RW_COLS), lambda i: (jnp.maximum(i - 1, 0), 0, 0)),
                  _full((1, RW_COLS)), small, small, small, small, _full((2 * LORA, 2 * W)), _full((W, W)),
                  _full((2, tm, tm))],
        out_specs=[pl.BlockSpec((N_PAIR, RW_HEADS, tm, 128), lambda i: (0, 0, i, 0)), row, row, row],
        out_shape=[jax.ShapeDtypeStruct((N_PAIR, RW_HEADS, T, 128), F32)] + [jax.ShapeDtypeStruct((T, W), F32)] * 3,
        compiler_params=_cparams(dimension_semantics=("arbitrary",)),
    )(p_rw, bnd, mu, w0, a0, k_k, k_a, lora, seg64, tri)


def _prep_bwd(p_rw, bnd, mu, w0, a0, k_k, k_a, lora, seg64, tri, cts):
    T = p_rw.shape[0]
    tm = _row_tile(T)
    W = RW_WIDTH

    def body(p_ref, bnd_ref, mu_ref, w0_ref, a0_ref, kk_ref, ka_ref, lora_ref, seg_ref, tri_ref,
             dhm_ref, drk_ref, dv1_ref, dv2_ref, dg_ref,
             dp_ref, dprev_ref, dfirst_ref, dmu_ref, dw0_ref, da0_ref, dkk_p_ref, dka_ref, dlora_ref):
        accs = (dmu_ref, dw0_ref, da0_ref, dkk_p_ref, dka_ref, dlora_ref)

        @pl.when(pl.program_id(0) == 0)
        def _():
            for a_ref in accs:
                a_ref[...] = jnp.zeros_like(a_ref)

        p = p_ref[...]
        prev = _shift_down(p, jnp.where(pl.program_id(0) == 0, 0.0, bnd_ref[0]))
        seg, tri = seg_ref[...], (tri_ref[0], tri_ref[1])
        _, vjp = jax.vjp(lambda *a: _prep_fn(*a, seg, tri), p, prev, mu_ref[...], w0_ref[...], a0_ref[...],
                         kk_ref[...], ka_ref[...], lora_ref[...])
        ct = (*_unpack_heads(dhm_ref), dv1_ref[...] + dv2_ref[...], dg_ref[...], drk_ref[...])
        grads = vjp(ct)
        dp_ref[...] = grads[0]
        dprev_ref[...] = grads[1]
        dfirst_ref[0] = grads[1][0:1, :]
        for a_ref, gval in zip(accs, grads[2:]):
            a_ref[...] += gval

    small = _full((1, W))
    row = pl.BlockSpec((tm, W), lambda i: (i, 0))
    return pl.pallas_call(
        body, name="rwkv_prep_bwd", grid=(T // tm,),
        in_specs=[pl.BlockSpec((tm, RW_COLS), lambda i: (i, 0)),
                  pl.BlockSpec((1, 1, RW_COLS), lambda i: (jnp.maximum(i - 1, 0), 0, 0)),
                  _full((1, RW_COLS)), small, small, small, small, _full((2 * LORA, 2 * W)), _full((W, W)),
                  _full((2, tm, tm)), pl.BlockSpec((N_PAIR, RW_HEADS, tm, 128), lambda i: (0, 0, i, 0))] + [row] * 4,
        out_specs=[pl.BlockSpec((tm, RW_COLS), lambda i: (i, 0)), pl.BlockSpec((tm, RW_COLS), lambda i: (i, 0)),
                   pl.BlockSpec((1, 1, RW_COLS), lambda i: (i, 0, 0)),
                   _full((1, RW_COLS)), small, small, small, small, _full((2 * LORA, 2 * W))],
        out_shape=[jax.ShapeDtypeStruct((T, RW_COLS), F32), jax.ShapeDtypeStruct((T, RW_COLS), F32),
                   jax.ShapeDtypeStruct((T // tm, 1, RW_COLS), F32),
                   jax.ShapeDtypeStruct((1, RW_COLS), F32)] + [jax.ShapeDtypeStruct((1, W), F32)] * 4
                  + [jax.ShapeDtypeStruct((2 * LORA, 2 * W), F32)],
        compiler_params=_cparams(dimension_semantics=("arbitrary",)),
    )(p_rw, bnd, mu, w0, a0, k_k, k_a, lora, seg64, tri, *cts)


def _post_fwd(o, rk, v, g, gn_g, gn_b, r_k, seg64):
    T = o.shape[0]
    tm = _row_tile(T)
    W = RW_WIDTH

    def body(o_ref, rk_ref, v_ref, g_ref, gg_ref, gb_ref, rkp_ref, seg_ref, y_ref):
        y_ref[...] = _post_fn(o_ref[...], rk_ref[...], v_ref[...], g_ref[...], gg_ref[...], gb_ref[...],
                              rkp_ref[...], seg_ref[...]).astype(BF16)

    row = pl.BlockSpec((tm, W), lambda i: (i, 0))
    small = _full((1, W))
    return pl.pallas_call(
        body, name="rwkv_post_fwd", grid=(T // tm,),
        in_specs=[row] * 4 + [small] * 3 + [_full((W, W))],
        out_specs=row, out_shape=jax.ShapeDtypeStruct((T, W), BF16),
        compiler_params=_cparams(dimension_semantics=("arbitrary",)),
    )(o, rk, v, g, gn_g, gn_b, r_k, seg64)


def _post_bwd(o, rk, v, g, gn_g, gn_b, r_k, seg64, dy):
    T = o.shape[0]
    tm = _row_tile(T)
    W = RW_WIDTH

    def body(o_ref, rk_ref, v_ref, g_ref, gg_ref, gb_ref, rkp_ref, seg_ref, dy_ref,
             do_ref, drk_ref, dv_ref, dg_ref, dgg_ref, dgb_ref, drkp_ref):
        accs = (dgg_ref, dgb_ref, drkp_ref)

        @pl.when(pl.program_id(0) == 0)
        def _():
            for a_ref in accs:
                a_ref[...] = jnp.zeros_like(a_ref)

        seg = seg_ref[...]
        _, vjp = jax.vjp(lambda *a: _post_fn(*a, seg), o_ref[...], rk_ref[...], v_ref[...], g_ref[...],
                         gg_ref[...], gb_ref[...], rkp_ref[...])
        grads = vjp(dy_ref[...])
        for o_, gval in zip((do_ref, drk_ref, dv_ref, dg_ref), grads[:4]):
            o_[...] = gval
        for a_ref, gval in zip(accs, grads[4:]):
            a_ref[...] += gval

    row = pl.BlockSpec((tm, W), lambda i: (i, 0))
    small = _full((1, W))
    return pl.pallas_call(
        body, name="rwkv_post_bwd", grid=(T // tm,),
        in_specs=[row] * 4 + [small] * 3 + [_full((W, W)), pl.BlockSpec((tm, W), lambda i: (i, 1))],
        out_specs=[row] * 4 + [small] * 3,
        out_shape=[jax.ShapeDtypeStruct((T, W), F32)] * 4 + [jax.ShapeDtypeStruct((1, W), F32)] * 3,
        compiler_params=_cparams(dimension_semantics=("arbitrary",)),
    )(o, rk, v, g, gn_g, gn_b, r_k, seg64, dy)


def _wkv_lhs(hm_ref, rows):
    tiles = [jnp.transpose(hm_ref[p, :, rows, :].reshape(RW_HEADS * WKV_CHUNK, 128)) for p in range(N_PAIR)]
    hi, lo = _split(jnp.concatenate(tiles, axis=0)[:N_VEC * RW_HEAD])
    return jnp.concatenate([hi, lo], axis=1)


def _wkv_group(nch):
    return min(WKV_GROUP, nch)


N_STEP_VEC = N_VEC - 1
PAD_ROWS = 16


def _wkv_fwd(cols, v, e_tab):
    T = v.shape[0]
    Tc = WKV_CHUNK
    nch = T // Tc
    G = _wkv_group(nch)
    J, W = RW_HEAD, RW_WIDTH
    JS = N_STEP_VEC * J

    def body(cols_ref, v_ref, e_ref, o_ref, states_ref, sa_ref, s_ref):
        @pl.when(pl.program_id(0) == 0)
        def _():
            s_ref[...] = jnp.zeros_like(s_ref)

        st = s_ref[...]
        for c in range(G):
            lhs = _wkv_lhs(cols_ref, slice(c * Tc, (c + 1) * Tc))
            for t in range(Tc):
                row = slice(c * Tc + t, c * Tc + t + 1)
                ex = _dot(lhs[:JS], e_ref[t])
                states_ref[c * Tc + t] = st
                sa = -jnp.sum(st * ex[0:J], axis=0, keepdims=True)
                st = st + ex[J:2 * J] * sa + ex[2 * J:3 * J] * v_ref[row, :]
                sa_ref[row, :] = sa
                o_ref[row, :] = jnp.sum(st * ex[3 * J:4 * J], axis=0, keepdims=True)
            st = st * _dot(lhs[JS:], e_ref[Tc - 1])
        s_ref[...] = st

    GT = G * Tc
    return pl.pallas_call(
        body, name="wkv_fwd", grid=(nch // G,),
        in_specs=[pl.BlockSpec((N_PAIR, RW_HEADS, GT, 128), lambda c: (0, 0, c, 0)),
                  pl.BlockSpec((GT, W), lambda c: (c, 0)), _full((Tc, 2 * 128, W))],
        out_specs=[pl.BlockSpec((GT, W), lambda c: (c, 0)), pl.BlockSpec((GT, J, W), lambda c: (c, 0, 0)),
                   pl.BlockSpec((GT, W), lambda c: (c, 0))],
        out_shape=[jax.ShapeDtypeStruct((T, W), F32), jax.ShapeDtypeStruct((T, J, W), F32),
                   jax.ShapeDtypeStruct((T, W), F32)],
        scratch_shapes=[pltpu.VMEM((J, W), F32)],
        compiler_params=_cparams(dimension_semantics=("arbitrary",)),
    )(cols, v, e_tab)


def _wkv_bwd(cols, v, do, states, sa, e_tab, r_tab):
    T = v.shape[0]
    Tc = WKV_CHUNK
    nch = T // Tc
    G = _wkv_group(nch)
    ngr = nch // G
    J, W = RW_HEAD, RW_WIDTH
    JS = N_STEP_VEC * J
    blocks = [slice(b * 128, (b + 1) * 128) for b in range(W // 128)]

    def body(cols_ref, v_ref, do_ref, states_ref, sa_ref, e_ref, r_ref, dv_ref, dhm_ref, ds_ref):
        @pl.when(pl.program_id(0) == 0)
        def _():
            ds_ref[...] = jnp.zeros_like(ds_ref)

        d_carry = [ds_ref[:, b] for b in blocks]
        last = Tc - 1
        for c in reversed(range(G)):
            at = c * Tc
            rows = slice(at, at + Tc)
            lhs = _wkv_lhs(cols_ref, rows)
            ex = _dot(lhs, e_ref[last])
            dst, ends = [], []
            for i, b in enumerate(blocks):
                s_end = (states_ref[at + last, :, b] + ex[J:2 * J, b] * sa_ref[at + last:at + Tc, b]
                         + ex[2 * J:3 * J, b] * v_ref[at + last:at + Tc, b])
                ends.append((d_carry[i] * s_end).astype(BF16))
                dst.append(d_carry[i] * ex[JS:, b])
            d_decay = _dot(jnp.concatenate(ends, axis=1), r_ref[last])
            acc = jnp.zeros((JS + PAD_ROWS, 128), F32)
            for t in reversed(range(Tc)):
                row = slice(at + t, at + t + 1)
                if t != last:
                    ex = _dot(lhs[:JS], e_ref[t])
                dvs, prods = [], []
                for i, b in enumerate(blocks):
                    kk_e, b_e, k_e, r_e = (ex[n * J:(n + 1) * J, b] for n in range(N_STEP_VEC))
                    do_row, v_row, sa_row = do_ref[row, b], v_ref[row, b], sa_ref[row, b]
                    s_old = states_ref[at + t, :, b]
                    s_new = states_ref[at + t + 1, :, b] if t != last else s_old + b_e * sa_row + k_e * v_row
                    dsn = dst[i] + r_e * do_row
                    dsa = jnp.sum(dsn * b_e, axis=0, keepdims=True)
                    dvs.append(jnp.sum(dsn * k_e, axis=0, keepdims=True))
                    prods.append(jnp.concatenate(
                        [s_old * (-dsa), dsn * sa_row, dsn * v_row, s_new * do_row, jnp.zeros((PAD_ROWS, 128), F32)],
                        axis=0).astype(BF16))
                    dst[i] = dsn - kk_e * dsa
                dv_ref[row, :] = jnp.concatenate(dvs, axis=1)
                acc = acc + _dot(jnp.concatenate(prods, axis=1), r_ref[t])
            d_carry = dst
            tiles = jnp.concatenate([acc[:JS], d_decay, jnp.zeros((2 * N_PAIR * J - N_VEC * J, 128), F32)], axis=0)
            for p in range(N_PAIR):
                dhm_ref[p, :, rows, :] = jnp.transpose(tiles[p * 128:(p + 1) * 128]).reshape(RW_HEADS, Tc, 128)
        for i, b in enumerate(blocks):
            ds_ref[:, b] = d_carry[i]

    GT = G * Tc
    rev2 = lambda c: (ngr - 1 - c, 0)
    rev3 = lambda c: (ngr - 1 - c, 0, 0)
    rev_hm = lambda c: (0, 0, ngr - 1 - c, 0)
    hm_spec = pl.BlockSpec((N_PAIR, RW_HEADS, GT, 128), rev_hm)
    return pl.pallas_call(
        body, name="wkv_bwd", grid=(ngr,),
        in_specs=[hm_spec, pl.BlockSpec((GT, W), rev2), pl.BlockSpec((GT, W), rev2),
                  pl.BlockSpec((GT, J, W), rev3), pl.BlockSpec((GT, W), rev2),
                  _full((Tc, 2 * 128, W)), _full((Tc, W, 128))],
        out_specs=[pl.BlockSpec((GT, W), rev2), hm_spec],
        out_shape=[jax.ShapeDtypeStruct((T, W), F32), jax.ShapeDtypeStruct((N_PAIR, RW_HEADS, T, 128), F32)],
        scratch_shapes=[pltpu.VMEM((J, W), F32)],
        compiler_params=_cparams(dimension_semantics=("arbitrary",)),
    )(cols, v, do, states, sa, e_tab, r_tab)


def _outproj(x, y_ret, y_rw, w_out_b, target, gf):
    T = x.shape[0]
    tm = _row_tile(T)
    W = RW_WIDTH

    def body(x_ref, yr_ref, yw_ref, w_ref, t_ref, gf_ref, loss_ref, dh_ref, dy_ref, dw_ref, dgf_ref):
        @pl.when(pl.program_id(0) == 0)
        def _():
            loss_ref[...] = jnp.zeros_like(loss_ref)
            dw_ref[...] = jnp.zeros_like(dw_ref)
            dgf_ref[...] = jnp.zeros_like(dgf_ref)

        y = jnp.concatenate([yr_ref[...], yw_ref[...]], axis=1)
        w = w_ref[...]
        h = x_ref[...] + _dot(y, w)
        rstd = lax.rsqrt(jnp.mean(h * h, axis=-1, keepdims=True) + RMS_EPS)
        hn = h * rstd
        gfv = gf_ref[...]
        err = hn * gfv - t_ref[...]
        loss_ref[...] += 0.5 * jnp.sum(jnp.mean(err * err, axis=-1))
        dout = err * (1.0 / D_MODEL)
        dgf_ref[...] += jnp.sum(dout * hn, axis=0, keepdims=True)
        dhn = dout * gfv
        dh = rstd * (dhn - hn * jnp.mean(dhn * hn, axis=-1, keepdims=True))
        dh_ref[...] = dh
        dhb = dh.astype(BF16)
        dy_ref[...] = _dot_nt(dhb, w)
        dw_ref[...] += _dot_tn(y, dhb)

    return pl.pallas_call(
        body, name="outproj_loss", grid=(T // tm,),
        in_specs=[pl.BlockSpec((tm, D_MODEL), lambda i: (i, 0)), pl.BlockSpec((tm, W), lambda i: (i, 0)),
                  pl.BlockSpec((tm, W), lambda i: (i, 0)), _full((D_MODEL, D_MODEL)),
                  pl.BlockSpec((tm, D_MODEL), lambda i: (i, 0)), _full((1, D_MODEL))],
        out_specs=[_full((1, PACK_W)), pl.BlockSpec((tm, D_MODEL), lambda i: (i, 0)),
                   pl.BlockSpec((tm, D_MODEL), lambda i: (i, 0)), _full((D_MODEL, D_MODEL)), _full((1, D_MODEL))],
        out_shape=[jax.ShapeDtypeStruct((1, PACK_W), F32), jax.ShapeDtypeStruct((T, D_MODEL), F32),
                   jax.ShapeDtypeStruct((T, D_MODEL), F32), jax.ShapeDtypeStruct((D_MODEL, D_MODEL), F32),
                   jax.ShapeDtypeStruct((1, D_MODEL), F32)],
        compiler_params=_cparams(dimension_semantics=("arbitrary",)),
    )(x, y_ret, y_rw, w_out_b, target, gf)


def _inproj_bwd_x(dp_ret, dp_rw, dprev, dfirst, w_t, x, norm_g, dh):
    T = x.shape[0]
    tm = _row_tile(T)
    nt = T // tm

    def body(dpr_ref, dpw_ref, dprev_ref, dnext_ref, w_ref, x_ref, g_ref, dh_ref, gx_ref, dg_ref, dpt_ref):
        @pl.when(pl.program_id(0) == 0)
        def _():
            dg_ref[...] = jnp.zeros_like(dg_ref)

        next_row = jnp.where(pl.program_id(0) == nt - 1, 0.0, dnext_ref[0])
        dp = jnp.concatenate([dpr_ref[...], dpw_ref[...] + _shift_up(dprev_ref[...], next_row)], axis=1)
        dpt_ref[...] = jnp.transpose(dp).astype(BF16)
        du = _dot(dp.astype(BF16), w_ref[...])
        xf = x_ref[...]
        rstd = lax.rsqrt(jnp.mean(xf * xf, axis=-1, keepdims=True) + RMS_EPS)
        xn = xf * rstd
        dg_ref[...] += jnp.sum(du * xn, axis=0, keepdims=True)
        dxn = du * g_ref[...]
        gx_ref[...] = dh_ref[...] + rstd * (dxn - xn * jnp.mean(dxn * xn, axis=-1, keepdims=True))

    return pl.pallas_call(
        body, name="inproj_bwd_x", grid=(T // tm,),
        in_specs=[pl.BlockSpec((tm, RET_COLS), lambda i: (i, 0)), pl.BlockSpec((tm, RW_COLS), lambda i: (i, 0)),
                  pl.BlockSpec((tm, RW_COLS), lambda i: (i, 0)),
                  pl.BlockSpec((1, 1, RW_COLS), lambda i: (jnp.minimum(i + 1, nt - 1), 0, 0)),
                  _full((IN_COLS, D_MODEL)), pl.BlockSpec((tm, D_MODEL), lambda i: (i, 0)), _full((1, D_MODEL)),
                  pl.BlockSpec((tm, D_MODEL), lambda i: (i, 0))],
        out_specs=[pl.BlockSpec((tm, D_MODEL), lambda i: (i, 0)), _full((1, D_MODEL)),
                   pl.BlockSpec((IN_COLS, tm), lambda i: (0, i))],
        out_shape=[jax.ShapeDtypeStruct((T, D_MODEL), F32), jax.ShapeDtypeStruct((1, D_MODEL), F32),
                   jax.ShapeDtypeStruct((IN_COLS, T), BF16)],
        compiler_params=_cparams(dimension_semantics=("arbitrary",)),
    )(dp_ret, dp_rw, dprev, dfirst, w_t, x, norm_g, dh)


def _inproj_bwd_w(dp_t, u):
    T = u.shape[0]

    def body(d_ref, u_ref, o_ref):
        o_ref[...] = _dot(d_ref[...], u_ref[...])

    return pl.pallas_call(
        body, name="inproj_bwd_w", grid=(N_CHIPS,),
        in_specs=[pl.BlockSpec((IN_SHARD, T), lambda i: (i, 0)), _full((T, D_MODEL))],
        out_specs=pl.BlockSpec((IN_SHARD, D_MODEL), lambda i: (i, 0)),
        out_shape=jax.ShapeDtypeStruct((IN_COLS, D_MODEL), F32),
        compiler_params=_cparams(dimension_semantics=("arbitrary",)),
    )(dp_t, u)


def _with_own(gathered, own, chip):
    return lax.dynamic_update_slice(gathered, own[None], (chip, 0, 0, 0))


def _local_step(x, target, w_in_t, late, chip, small):
    T = x.shape[0]
    tm = _row_tile(T)
    W = RW_WIDTH
    cos, sin = _rope_tables(T)
    tabs = _ret_tables()
    seg64 = _seg_matrix(RW_WIDTH, RW_HEAD)
    e_tab = _wkv_expand_table()
    r_tab = _wkv_reduce_table()

    p_ret, p_rw, u, bnd, *gathered = _inproj_fwd(x, small["norm_g"], w_in_t, late)
    g_out, g_lw, g_la = (_with_own(g, own, chip) for g, own in zip(gathered, late))
    w_out_b = g_out.reshape(D_MODEL, D_MODEL)
    lw = jnp.transpose(g_lw.reshape(N_CHIPS, LORA, LORA_SHARD), (1, 0, 2)).reshape(LORA, W)
    la = jnp.transpose(g_la.reshape(N_CHIPS, LORA, LORA_SHARD), (1, 0, 2)).reshape(LORA, W)
    zero = jnp.zeros((LORA, W), F32)
    lora = jnp.concatenate([jnp.concatenate([lw, zero], axis=1), jnp.concatenate([zero, la], axis=1)], axis=0)
    prep_w = (small["rwkv_mu"], small["w0"], small["a0"], small["k_k"], small["k_a"], lora, seg64, _chunk_tables(tm))
    post_w = (small["rwkv_gn_g"], small["rwkv_gn_b"], small["r_k"], seg64)

    y_ret, ret, s_in_all = _ret_fwd(p_ret, cos, sin, tabs, small["ret_gn_g"])
    hm, v, g, rk = _prep_fwd(p_rw, bnd, *prep_w)
    o, states, sa = _wkv_fwd(hm, v, e_tab)
    y_rw = _post_fwd(o, rk, v, g, *post_w)
    loss, dh, dy, d_w_out, d_gf = _outproj(x, y_ret, y_rw, w_out_b, target, small["final_norm_g"])

    do, d_rk, dv2, dg, d_gn_g, d_gn_b, d_r_k = _post_bwd(o, rk, v, g, *post_w, dy)
    dv1, d_hm = _wkv_bwd(hm, v, do, states, sa, e_tab, r_tab)
    dp_rw, dprev, dfirst, d_mu, d_w0, d_a0, d_k_k, d_k_a, d_lora = _prep_bwd(
        p_rw, bnd, *prep_w, (d_hm, d_rk, dv1, dv2, dg))
    dp_ret, d_ret_gn = _ret_bwd(p_ret, cos, sin, tabs, small["ret_gn_g"], ret, s_in_all, dy)
    grad_x, d_norm_g, dp_t = _inproj_bwd_x(dp_ret, dp_rw, dprev, dfirst, w_in_t, x, small["norm_g"], dh)
    d_w_in = _inproj_bwd_w(dp_t, u)

    d_small = {"norm_g": d_norm_g, "ret_gn_g": d_ret_gn, "rwkv_mu": d_mu, "w0": d_w0, "a0": d_a0, "k_k": d_k_k,
               "k_a": d_k_a, "r_k": d_r_k, "rwkv_gn_g": d_gn_g, "rwkv_gn_b": d_gn_b, "final_norm_g": d_gf}
    return loss, grad_x, d_w_in, d_w_out, d_lora, d_small


ANY = pl.BlockSpec(memory_space=pl.ANY)
CHIP_FLIPS = ((0, 1), (1, 0), (1, 1))
N_FLIPS = len(CHIP_FLIPS)
LORA_SHARD = RW_WIDTH // N_CHIPS
HALF_IN = IN_SHARD // 2
HALF_OUT = OUT_SHARD // 2


def _position():
    return lax.axis_index("x"), lax.axis_index("y"), lax.axis_index("c")


def _flip(v, f):
    return 1 - v if f else v


def _finish(local, remote, landed):
    for cp in landed:
        cp.wait_recv()
    for cp in remote:
        cp.wait_send()
    for cp in local:
        cp.wait()


def _gather_copies(ins, outs, sems):
    send, recv, pass_send, pass_recv = sems
    x, y, c = _position()
    s = 2 * x + y
    sibling = (x, y, 1 - c)
    first, second = (1 - c, c), (c, 1 - c)
    (x1, y1), (x2, y2) = ((x + fx - 2 * x * fx, y + fy - 2 * y * fy) for fx, fy in (first, second))
    s1, s2, sd = 2 * x1 + y1, 2 * x2 + y2, 2 * (1 - x) + (1 - y)

    def copy(src, dst, pair, k, to):
        return pltpu.make_async_remote_copy(src_ref=src, dst_ref=dst, send_sem=pair[0].at[k], recv_sem=pair[1].at[k],
                                            device_id=to, device_id_type=MESH)

    at_once, direct, relayed, passed_in, sends = [], [], [], [], []
    for a in range(len(ins)):
        k = a * N_FLIPS
        own, out = ins[a].at[c], outs[a]
        ici = (send, recv)
        to_first = copy(own, out.at[s, c], ici, k, (x1, y1, c))
        to_second = copy(own, out.at[s, c], ici, k + 1, (x2, y2, c))
        relay = copy(out.at[s1, c], out.at[s1, c], ici, k + 2, (x2, y2, c))
        passes = [copy(out.at[slot, c], out.at[slot, c], (pass_send, pass_recv), k + j, sibling)
                  for j, slot in enumerate((s1, s2, sd))]
        at_once += [to_first, to_second]
        direct += [(copy(own, out.at[s1, c], ici, k, (x1, y1, c)), [relay, passes[0]]),
                   (copy(own, out.at[s2, c], ici, k + 1, (x2, y2, c)), [passes[1]])]
        relayed += [(copy(own, out.at[sd, c], ici, k + 2, (x2, y2, c)), [passes[2]])]
        passed_in += [copy(out.at[slot, 1 - c], out.at[slot, 1 - c], (pass_send, pass_recv), k + j, sibling)
                      for j, slot in enumerate((s2, s1, sd))]
        sends += [to_first, to_second, relay] + passes
    return at_once, direct, relayed, passed_in, sends


def _then(steps):
    for arrived, then in steps:
        arrived.wait_recv()
        for cp in then:
            cp.start()


def _gather_start(copies):
    for cp in copies[0]:
        cp.start()


def _gather_relay(copies):
    _then(copies[1])


def _gather_finish(copies):
    _, _, relayed, passed_in, sends = copies
    _then(relayed)
    _finish([], sends, passed_in)


def _gather_sems(n):
    return [pltpu.SemaphoreType.DMA((n * N_FLIPS,))] * 4


def _gather_shapes(arrs):
    return [jax.ShapeDtypeStruct((N_CHIPS,) + a.shape, a.dtype) for a in arrs]


def _gather_chips(arrs):
    n = len(arrs)

    def body(*refs):
        copies = _gather_copies(refs[:n], refs[n:2 * n], refs[2 * n:])
        _gather_start(copies)
        _gather_relay(copies)
        _gather_finish(copies)

    return pl.pallas_call(
        body, name="gather_weights",
        in_specs=[ANY] * n, out_specs=[ANY] * n,
        out_shape=_gather_shapes(arrs), scratch_shapes=_gather_sems(n),
    )(*arrs)


def _pair_exchange(g_in, g_out, g_small):
    def body(gi_ref, go_ref, gs_ref, li_ref, lo_ref, ls_ref, send, recv):
        x, y, c = _position()
        peer = (x, y, 1 - c)
        srcs = (gi_ref.at[:, pl.ds((1 - c) * HALF_IN, HALF_IN), :], go_ref.at[:, pl.ds((1 - c) * HALF_OUT, HALF_OUT), :],
                gs_ref)
        remote = [pltpu.make_async_remote_copy(src_ref=src, dst_ref=dst, send_sem=send.at[k], recv_sem=recv.at[k],
                                               device_id=peer, device_id_type=MESH)
                  for k, (src, dst) in enumerate(zip(srcs, (li_ref, lo_ref, ls_ref)))]
        for cp in remote:
            cp.start()
        _finish([], remote, remote)

    return pl.pallas_call(
        body, name="pair_exchange",
        in_specs=[ANY] * 3, out_specs=[ANY] * 3,
        out_shape=[jax.ShapeDtypeStruct((N_CHIPS, HALF_IN, D_MODEL), F32),
                   jax.ShapeDtypeStruct((N_CHIPS, HALF_OUT, D_MODEL), F32),
                   jax.ShapeDtypeStruct(g_small.shape, F32)],
        scratch_shapes=[pltpu.SemaphoreType.DMA((3,)), pltpu.SemaphoreType.DMA((3,))],
    )(g_in, g_out, g_small)


def _pair_sum(g_in, g_out, g_small, l_in, l_out, l_small, c_arr):
    def body(c_ref, gi_ref, go_ref, gs_ref, li_ref, lo_ref, ls_ref, ci_ref, co_ref, cs_ref):
        ci_ref[...] = (gi_ref[...] + li_ref[...]).astype(BF16)
        co_ref[...] = (go_ref[...] + lo_ref[...]).astype(BF16)

        @pl.when(pl.program_id(0) == 0)
        def _():
            cs_ref[...] = gs_ref[...] + ls_ref[...]

    nd = g_small.shape
    return pl.pallas_call(
        body, name="pair_sum",
        grid_spec=pltpu.PrefetchScalarGridSpec(
            num_scalar_prefetch=1, grid=(N_CHIPS,),
            in_specs=[pl.BlockSpec((1, HALF_IN, D_MODEL), lambda s, c: (s, c[0], 0)),
                      pl.BlockSpec((1, HALF_OUT, D_MODEL), lambda s, c: (s, c[0], 0)),
                      pl.BlockSpec(nd, lambda s, c: (0, 0)),
                      pl.BlockSpec((1, HALF_IN, D_MODEL), lambda s, c: (s, 0, 0)),
                      pl.BlockSpec((1, HALF_OUT, D_MODEL), lambda s, c: (s, 0, 0)),
                      pl.BlockSpec(nd, lambda s, c: (0, 0))],
            out_specs=[pl.BlockSpec((1, HALF_IN, D_MODEL), lambda s, c: (s, 0, 0)),
                       pl.BlockSpec((1, HALF_OUT, D_MODEL), lambda s, c: (s, 0, 0)),
                       pl.BlockSpec(nd, lambda s, c: (0, 0))]),
        out_shape=[jax.ShapeDtypeStruct((N_CHIPS, HALF_IN, D_MODEL), BF16),
                   jax.ShapeDtypeStruct((N_CHIPS, HALF_OUT, D_MODEL), BF16), jax.ShapeDtypeStruct(nd, F32)],
        compiler_params=_cparams(dimension_semantics=("arbitrary",)),
    )(c_arr, g_in, g_out, g_small, l_in, l_out, l_small)


def _chip_exchange(c_in, c_out, c_small):
    def body(ci_ref, co_ref, cs_ref, li_ref, lo_ref, ls_ref, send, recv):
        x, y, c = _position()
        s = 2 * x + y
        remote = []
        for j, (fx, fy) in enumerate(CHIP_FLIPS):
            px, py = _flip(x, fx), _flip(y, fy)
            ps = 2 * px + py
            for a, (src, dst) in enumerate(((ci_ref.at[ps], li_ref.at[j]), (co_ref.at[ps], lo_ref.at[j]),
                                            (cs_ref, ls_ref.at[j]))):
                k = 3 * j + a
                remote.append(pltpu.make_async_remote_copy(src_ref=src, dst_ref=dst, send_sem=send.at[k],
                                                           recv_sem=recv.at[k], device_id=(px, py, c),
                                                           device_id_type=MESH))
        diagonal = 3 * (N_FLIPS - 1)
        for cp in remote[diagonal:] + remote[:diagonal]:
            cp.start()
        _finish([], remote, remote)

    return pl.pallas_call(
        body, name="chip_exchange",
        in_specs=[ANY] * 3, out_specs=[ANY] * 3,
        out_shape=[jax.ShapeDtypeStruct((N_FLIPS, HALF_IN, D_MODEL), c_in.dtype),
                   jax.ShapeDtypeStruct((N_FLIPS, HALF_OUT, D_MODEL), c_out.dtype),
                   jax.ShapeDtypeStruct((N_FLIPS,) + c_small.shape, F32)],
        scratch_shapes=[pltpu.SemaphoreType.DMA((3 * N_FLIPS,)), pltpu.SemaphoreType.DMA((3 * N_FLIPS,))],
    )(c_in, c_out, c_small)


def _chip_sum(g_in, g_out, p_in, p_out, c_small, l_in, l_out, l_small, sc_arr):
    nd = c_small.shape

    def body(s_ref, gi_ref, go_ref, pi_ref, po_ref, cs_ref, li0, li1, li2, lo0, lo1, lo2, ls_ref,
             ri_ref, ro_ref, rs_ref):
        ri_ref[...] = (((gi_ref[0] + pi_ref[0]) + li0[0].astype(F32)) + li1[0].astype(F32)) + li2[0].astype(F32)
        ro_ref[...] = (((go_ref[0] + po_ref[0]) + lo0[0].astype(F32)) + lo1[0].astype(F32)) + lo2[0].astype(F32)
        me = s_ref[0]
        parts = (cs_ref[...], ls_ref[0], ls_ref[1], ls_ref[2])

        def of_chip(s):
            m = jnp.bitwise_xor(me, s)
            return jnp.where(m == 0, parts[0], jnp.where(m == 1, parts[1], jnp.where(m == 2, parts[2], parts[3])))

        rs_ref[...] = ((of_chip(0) + of_chip(1)) + of_chip(2)) + of_chip(3)

    def flip_in(j):
        return pl.BlockSpec((1, HALF_IN, D_MODEL), lambda i, s: (j, 0, 0))

    def flip_out(j):
        return pl.BlockSpec((1, HALF_OUT, D_MODEL), lambda i, s: (j, 0, 0))

    return pl.pallas_call(
        body, name="chip_sum",
        grid_spec=pltpu.PrefetchScalarGridSpec(
            num_scalar_prefetch=1, grid=(1,),
            in_specs=[pl.BlockSpec((1, HALF_IN, D_MODEL), lambda i, s: (s[0], s[1], 0)),
                      pl.BlockSpec((1, HALF_OUT, D_MODEL), lambda i, s: (s[0], s[1], 0)),
                      pl.BlockSpec((1, HALF_IN, D_MODEL), lambda i, s: (s[0], 0, 0)),
                      pl.BlockSpec((1, HALF_OUT, D_MODEL), lambda i, s: (s[0], 0, 0)),
                      pl.BlockSpec(nd, lambda i, s: (0, 0)),
                      flip_in(0), flip_in(1), flip_in(2), flip_out(0), flip_out(1), flip_out(2),
                      pl.BlockSpec((N_FLIPS,) + nd, lambda i, s: (0, 0, 0))],
            out_specs=[pl.BlockSpec((HALF_IN, D_MODEL), lambda i, s: (0, 0)),
                       pl.BlockSpec((HALF_OUT, D_MODEL), lambda i, s: (0, 0)),
                       pl.BlockSpec(nd, lambda i, s: (0, 0))]),
        out_shape=[jax.ShapeDtypeStruct((HALF_IN, D_MODEL), F32), jax.ShapeDtypeStruct((HALF_OUT, D_MODEL), F32),
                   jax.ShapeDtypeStruct(nd, F32)],
        compiler_params=_cparams(dimension_semantics=("arbitrary",)),
    )(sc_arr, g_in, g_out, p_in, p_out, c_small, l_in, l_in, l_in, l_out, l_out, l_out, l_small)


def _pair_share(r_in, r_out):
    def body(ri_ref, ro_ref, li_ref, lo_ref, send, recv):
        x, y, c = _position()
        remote = [pltpu.make_async_remote_copy(src_ref=src, dst_ref=dst, send_sem=send.at[k], recv_sem=recv.at[k],
                                               device_id=(x, y, 1 - c), device_id_type=MESH)
                  for k, (src, dst) in enumerate(((ri_ref, li_ref), (ro_ref, lo_ref)))]
        for cp in remote:
            cp.start()
        _finish([], remote, remote)

    return pl.pallas_call(
        body, name="pair_share",
        in_specs=[ANY] * 2, out_specs=[ANY] * 2,
        out_shape=[jax.ShapeDtypeStruct(r_in.shape, F32), jax.ShapeDtypeStruct(r_out.shape, F32)],
        scratch_shapes=[pltpu.SemaphoreType.DMA((2,)), pltpu.SemaphoreType.DMA((2,))],
    )(r_in, r_out)


def _adam_update(w, g, m, v):
    mn = ADAM_B1 * m + (1.0 - ADAM_B1) * g
    vn = ADAM_B2 * v + (1.0 - ADAM_B2) * jnp.square(g)
    m_hat = mn / (1.0 - ADAM_B1 ** ADAM_STEP)
    v_hat = vn / (1.0 - ADAM_B2 ** ADAM_STEP)
    return -ADAM_LR * (m_hat / (jnp.sqrt(v_hat) + ADAM_EPS) + ADAM_WD * w), mn, vn


def _adamw(name, w, g_mine, g_theirs, m, v, core_arr, tr):
    rows, cols = w.shape
    per_half = rows // 2 // tr

    def body(c_ref, w_ref, gm_ref, gt_ref, m_ref, v_ref, g_ref, d_ref, nm_ref, nv_ref):
        mine = (pl.program_id(0) // per_half) == c_ref[0]
        g = jnp.where(mine, gm_ref[...], gt_ref[...])
        d, mn, vn = _adam_update(w_ref[...], g, m_ref[...], v_ref[...])
        g_ref[...] = g
        d_ref[...] = d
        nm_ref[...] = mn
        nv_ref[...] = vn

    spec = pl.BlockSpec((tr, cols), lambda i, c: (i, 0))
    half = pl.BlockSpec((tr, cols), lambda i, c: (i % per_half, 0))
    return pl.pallas_call(
        body, name=name,
        grid_spec=pltpu.PrefetchScalarGridSpec(
            num_scalar_prefetch=1, grid=(rows // tr,),
            in_specs=[spec, half, half, spec, spec], out_specs=[spec] * 4),
        out_shape=[jax.ShapeDtypeStruct((rows, cols), F32)] * 4,
        compiler_params=_cparams(dimension_semantics=("arbitrary",)),
    )(core_arr, w, g_mine, g_theirs, m, v)


def _row_pieces(n):
    return [(k, k * PACK_W, min(PACK_W, n - k * PACK_W)) for k in range(-(-n // PACK_W))]


def _pack_small(d_small, loss, d_lora):
    ns = len(SMALL_NAMES)

    def body(*refs):
        small_refs, (loss_ref, lora_ref, out_ref) = refs[:ns], refs[ns:]
        out_ref[...] = jnp.zeros_like(out_ref)
        out_ref[PACK_LORA_W:PACK_LORA_W + LORA, :] = lora_ref[:LORA, :RW_WIDTH]
        out_ref[PACK_LORA_A:PACK_LORA_A + LORA, :] = lora_ref[LORA:, RW_WIDTH:]
        for name, n, ref in zip(SMALL_NAMES, SMALL_SIZES, small_refs):
            for k, at, w in _row_pieces(n):
                out_ref[PACK_AT[name] + k:PACK_AT[name] + k + 1, 0:w] = ref[:, at:at + w]
        out_ref[PACK_LOSS:PACK_LOSS + 1, :] = loss_ref[...]

    return pl.pallas_call(body, name="pack_small", out_shape=jax.ShapeDtypeStruct((PACK_ROWS, PACK_W), F32),
                          compiler_params=_cparams())(*d_small, loss, d_lora)


def _adamw_small(tot, chip_arr, ws, ms, vs):
    ns = len(SMALL_NAMES)
    n_par = ns + 2

    def body(s_ref, tot_ref, glw_ref, gla_ref, *refs):
        w_refs, m_refs, v_refs = refs[:n_par], refs[n_par:2 * n_par], refs[2 * n_par:3 * n_par]
        outs = refs[3 * n_par:]
        g_refs, d_refs, nm_refs, nv_refs = (outs[i * n_par:(i + 1) * n_par] for i in range(4))
        grads = [jnp.concatenate([tot_ref[PACK_AT[name] + k:PACK_AT[name] + k + 1, 0:w] for k, _, w in _row_pieces(n)],
                                 axis=1) for name, n in zip(SMALL_NAMES, SMALL_SIZES)]
        grads += [glw_ref[...], gla_ref[...]]
        for i, g in enumerate(grads):
            d, mn, vn = _adam_update(w_refs[i][...], g, m_refs[i][...], v_refs[i][...])
            g_refs[i][...] = g
            d_refs[i][...] = d
            nm_refs[i][...] = mn
            nv_refs[i][...] = vn

    def whole(a):
        nd = a.ndim
        return pl.BlockSpec(a.shape, lambda i, s: (0,) * nd)

    shard = (LORA, LORA_SHARD)
    par_specs = [whole(a) for a in ws]
    res = pl.pallas_call(
        body, name="adamw_small",
        grid_spec=pltpu.PrefetchScalarGridSpec(
            num_scalar_prefetch=1, grid=(1,),
            in_specs=[whole(tot), pl.BlockSpec(shard, lambda i, s: (PACK_LORA_W // LORA, s[0])),
                      pl.BlockSpec(shard, lambda i, s: (PACK_LORA_A // LORA, s[0]))] + par_specs * 3,
            out_specs=par_specs * 4),
        out_shape=[jax.ShapeDtypeStruct(a.shape, F32) for a in ws] * 4,
        compiler_params=_cparams(dimension_semantics=("arbitrary",)),
    )(chip_arr, tot, tot, tot, *ws, *ms, *vs)
    return [res[i * n_par:(i + 1) * n_par] for i in range(4)]


def kernel(x, norm_g, w_in, ret_gn_g, rwkv_mu, w_lora_up, w0, a_lora_up, a0, k_k, k_a, r_k, rwkv_gn_g, rwkv_gn_b, w_out, final_norm_g, loss_target, m_norm_g, m_w_in, m_ret_gn_g, m_rwkv_mu, m_w_lora_up, m_w0, m_a_lora_up, m_a0, m_k_k, m_k_a, m_r_k, m_rwkv_gn_g, m_rwkv_gn_b, m_w_out, m_final_norm_g, v_norm_g, v_w_in, v_ret_gn_g, v_rwkv_mu, v_w_lora_up, v_w0, v_a_lora_up, v_a0, v_k_k, v_k_a, v_r_k, v_rwkv_gn_g, v_rwkv_gn_b, v_w_out, v_final_norm_g):
    W = RW_WIDTH
    params = dict(norm_g=norm_g, ret_gn_g=ret_gn_g, rwkv_mu=rwkv_mu, w0=w0, a0=a0, k_k=k_k, k_a=k_a, r_k=r_k,
                  rwkv_gn_g=rwkv_gn_g, rwkv_gn_b=rwkv_gn_b, final_norm_g=final_norm_g)
    moments_m = dict(norm_g=m_norm_g, ret_gn_g=m_ret_gn_g, rwkv_mu=m_rwkv_mu, w0=m_w0, a0=m_a0, k_k=m_k_k, k_a=m_k_a,
                     r_k=m_r_k, rwkv_gn_g=m_rwkv_gn_g, rwkv_gn_b=m_rwkv_gn_b, final_norm_g=m_final_norm_g)
    moments_v = dict(norm_g=v_norm_g, ret_gn_g=v_ret_gn_g, rwkv_mu=v_rwkv_mu, w0=v_w0, a0=v_a0, k_k=v_k_k, k_a=v_k_a,
                     r_k=v_r_k, rwkv_gn_g=v_rwkv_gn_g, rwkv_gn_b=v_rwkv_gn_b, final_norm_g=v_final_norm_g)
    xi, yi, ci = _position()
    chip = (2 * xi + yi).astype(jnp.int32)

    def halves(a):
        return a.reshape(2, a.shape[0] // 2, a.shape[1])

    w_t, m_t, v_t = (jnp.transpose(a[0]) for a in (w_in, m_w_in, v_w_in))
    own_in = halves(w_t.astype(BF16))
    w_in_t = _with_own(_gather_chips([own_in])[0], own_in, chip).reshape(IN_COLS, D_MODEL)
    late = [halves(w_out[0].astype(BF16)), halves(w_lora_up[0]), halves(a_lora_up[0])]
    small = {n: params[n].reshape(1, -1) for n in SMALL_NAMES}

    loss, grad_x, d_w_in, d_w_out, d_lora, d_small = _local_step(x[0], loss_target[0], w_in_t, late, chip, small)

    core = ci.astype(jnp.int32)
    gi = d_w_in.reshape(N_CHIPS, IN_SHARD, D_MODEL)
    go = d_w_out.reshape(N_CHIPS, OUT_SHARD, D_MODEL)
    gs = _pack_small([d_small[n] for n in SMALL_NAMES], loss, d_lora)
    p_in, p_out, p_small = _pair_exchange(gi, go, gs)
    c_in, c_out, c_small = _pair_sum(gi, go, gs, p_in, p_out, p_small, core.reshape(1))
    l_in, l_out, l_small = _chip_exchange(c_in, c_out, c_small)
    r_in, r_out, tot = _chip_sum(gi, go, p_in, p_out, c_small, l_in, l_out, l_small, jnp.stack([chip, core]))
    t_in, t_out = _pair_share(r_in, r_out)

    grad_w_in, d_in, nm_in, nv_in = (jnp.transpose(a) for a in _adamw(
        "adamw_w_in", w_t, r_in, t_in, m_t, v_t, core.reshape(1), HALF_IN // 2))
    grad_w_out, d_out, nm_out, nv_out = _adamw("adamw_w_out", w_out[0], r_out, t_out, m_w_out[0], v_w_out[0],
                                               core.reshape(1), HALF_OUT)
    par_names = SMALL_NAMES + ("w_lora_up", "a_lora_up")

    def operands(tree, lw_, la_):
        return [tree[n].reshape(1, -1) for n in SMALL_NAMES] + [lw_[0], la_[0]]

    res = _adamw_small(tot, chip.reshape(1), operands(params, w_lora_up, a_lora_up),
                       operands(moments_m, m_w_lora_up, m_a_lora_up), operands(moments_v, v_w_lora_up, v_a_lora_up))

    names = ("norm_g", "w_in", "ret_gn_g", "rwkv_mu", "w_lora_up", "w0", "a_lora_up", "a0", "k_k", "k_a", "r_k",
             "rwkv_gn_g", "rwkv_gn_b", "w_out", "final_norm_g")
    shapes = dict(w_in=w_in.shape, w_out=w_out.shape, w_lora_up=w_lora_up.shape, a_lora_up=a_lora_up.shape,
                  **{n: params[n].shape for n in SMALL_NAMES})

    def leaves(pars, big_in, big_out):
        tree = dict(zip(par_names, pars), w_in=big_in, w_out=big_out)
        return [tree[n].reshape(shapes[n]) for n in names]

    grads = leaves(res[0], grad_w_in, grad_w_out)
    deltas = leaves(res[1], d_in, d_out)
    new_m = leaves(res[2], nm_in, nm_out)
    new_v = leaves(res[3], nv_in, nv_out)
    return (tot[PACK_LOSS, 0], grad_x.reshape(x.shape), *grads, *deltas, *new_m, *new_v)
```

```python
import functools

import numpy as np
import jax
import jax.numpy as jnp
from jax import lax
from jax.experimental import pallas as pl
from jax.experimental.pallas import tpu as pltpu

F32 = jnp.float32
BF16 = jnp.bfloat16
X3 = "bf16x3"
B1 = "bf16"
MESH = pl.DeviceIdType.MESH

D_MODEL = 1024
N_CHIPS = 4
RET_HEADS = 4
RET_DK = 64
RET_DV = 128
RET_QK = RET_HEADS * RET_DK
RET_WIDTH = RET_HEADS * RET_DV
RET_COLS = 2 * RET_QK + 2 * RET_WIDTH
RET_CHUNK = 64
RET_GROUP = 8
RW_WIDTH = 512
RW_HEAD = 64
RW_HEADS = 8
LORA = 64
RW_COLS = 4 * RW_WIDTH + 2 * LORA
IN_COLS = RET_COLS + RW_COLS
IN_SHARD = IN_COLS // N_CHIPS
OUT_SHARD = D_MODEL // N_CHIPS
ROPE_BASE = 10000.0
RMS_EPS = 1e-6
RET_GN_EPS = 1e-5
RW_GN_EPS = 64e-5
WKV_CHUNK = 16
WKV_GROUP = 8
N_VEC = 5

ADAM_LR = 0.001
ADAM_B1 = 0.9
ADAM_B2 = 0.999
ADAM_EPS = 1e-08
ADAM_WD = 0.01
ADAM_STEP = 10

VMEM_LIMIT = 56 * 1024 * 1024

PACK_W = 512
SMALL_NAMES = ("norm_g", "ret_gn_g", "rwkv_mu", "w0", "a0", "k_k", "k_a", "r_k", "rwkv_gn_g", "rwkv_gn_b",
               "final_norm_g")
SMALL_SIZES = (1024, 512, 2176, 512, 512, 512, 512, 512, 512, 512, 1024)
PACK_LORA_W = 0
PACK_LORA_A = LORA
PACK_SMALL = 2 * LORA


def _pack_layout():
    rows, at = {}, PACK_SMALL
    for name, n in zip(SMALL_NAMES, SMALL_SIZES):
        rows[name] = at
        at += -(-n // PACK_W)
    return rows, at


PACK_AT, PACK_LOSS = _pack_layout()
PACK_ROWS = -(-(PACK_LOSS + 1) // 8) * 8


def _cparams(**kw):
    return pltpu.CompilerParams(vmem_limit_bytes=VMEM_LIMIT, **kw)


def _split(x):
    hi = x.astype(BF16)
    lo = (x - hi.astype(F32)).astype(BF16)
    return hi, lo


def _dot_dims(a, b, dims, precision):
    if precision == B1:
        a, b = a.astype(BF16), b.astype(BF16)
    if precision != X3:
        return lax.dot_general(a, b, dims, preferred_element_type=F32)
    (ah, al), (bh, bl) = _split(a), _split(b)
    dot = lambda u, w: lax.dot_general(u, w, dims, preferred_element_type=F32)
    return dot(ah, bh) + dot(ah, bl) + dot(al, bh)


def _dot(a, b, precision=None):
    return _dot_dims(a, b, (((1,), (0,)), ((), ())), precision)


def _dot_nt(a, b, precision=None):
    return _dot_dims(a, b, (((1,), (1,)), ((), ())), precision)


def _dot_tn(a, b, precision=None):
    return _dot_dims(a, b, (((0,), (0,)), ((), ())), precision)


@jax.custom_vjp
def _segsum(x, seg):
    hi, lo = _split(x)
    return _dot(hi, seg) + _dot(lo, seg)


def _segsum_fwd(x, seg):
    return _segsum(x, seg), seg


def _segsum_bwd(seg, ct):
    return _segsum(ct, seg), jnp.zeros_like(seg)


_segsum.defvjp(_segsum_fwd, _segsum_bwd)


def _softplus(z):
    return jnp.maximum(z, 0.0) + jnp.log(1.0 + jnp.exp(-jnp.abs(z)))


def _full(shape):
    nd = len(shape)
    return pl.BlockSpec(shape, lambda *_: (0,) * nd)


def _rope_tables(T):
    half = RET_DK // 2
    expo = -jnp.arange(half, dtype=F32) / jnp.float32(half)
    freqs = jnp.exp(expo * jnp.float32(np.log(ROPE_BASE)))
    ang = jnp.arange(T, dtype=jnp.int32).astype(F32)[:, None] * freqs[None, :]
    cos = jnp.tile(jnp.cos(ang), (1, 2 * RET_HEADS))
    sin = jnp.tile(jnp.sin(ang), (1, 2 * RET_HEADS))
    return cos, sin


def _ret_tables():
    H, C = RET_HEADS, RET_CHUNK
    hidx = jnp.arange(H, dtype=F32)
    lg = jnp.log(1.0 - jnp.exp2(-5.0 - hidx))
    idx = jnp.arange(C, dtype=F32)
    intra = jnp.exp(lg[:, None, None] * jnp.abs(idx[:, None] - idx[None, :]))
    q_dec = jnp.transpose(jnp.exp(lg[:, None] * (idx[None, :] + 1.0)))
    k_dec = jnp.transpose(jnp.exp(lg[:, None] * (C - 1.0 - idx[None, :])))
    chunk_dec = jnp.exp(lg * C)
    qd = jnp.repeat(q_dec, RET_DK, axis=1)
    kd = jnp.repeat(k_dec, RET_DK, axis=1)
    row_h = np.arange(RET_QK) // RET_DK
    col_h = np.arange(RET_WIDTH) // RET_DV
    bm = jnp.asarray((row_h[:, None] == col_h[None, :]).astype(np.float32))
    cd = bm * jnp.repeat(chunk_dec, RET_DK)[:, None]
    return intra, qd, kd, cd, bm


def _seg_matrix(width, head):
    h = np.arange(width) // head
    return jnp.asarray((h[:, None] == h[None, :]).astype(np.float32), dtype=BF16)


def _wkv_expand_table():
    Tc = WKV_CHUNK
    k = np.arange(2 * RW_HEADS * Tc)
    kh, kt = (k % (RW_HEADS * Tc)) // Tc, k % Tc
    nh = np.arange(RW_WIDTH) // RW_HEAD
    e = (kh[None, :, None] == nh[None, None, :]) & (kt[None, :, None] == np.arange(Tc)[:, None, None])
    return jnp.asarray(e.astype(np.float32), dtype=BF16)


def _wkv_reduce_table():
    Tc = WKV_CHUNK
    kh = np.arange(RW_WIDTH) // RW_HEAD
    n = np.arange(RW_HEADS * Tc)
    nh, nt = n // Tc, n % Tc
    r = (kh[None, :, None] == nh[None, None, :]) & (nt[None, None, :] == np.arange(Tc)[:, None, None])
    return jnp.asarray(r.astype(np.float32), dtype=BF16)


def _inproj_fwd(x, norm_g, w_t, late):
    T = x.shape[0]
    tm = _row_tile(T)
    nt = T // tm
    n = len(late)

    def body(x_ref, g_ref, w_ref, *refs):
        late_in, (pret_ref, prw_ref, u_ref, last_ref) = refs[:n], refs[n:n + 4]
        late_out, sems = refs[n + 4:2 * n + 4], refs[2 * n + 4:]

        @pl.when(pl.program_id(0) == 0)
        def _():
            _gather_start(_gather_copies(late_in, late_out, sems))

        xf = x_ref[...]
        rstd = lax.rsqrt(jnp.mean(xf * xf, axis=-1, keepdims=True) + RMS_EPS)
        ub = ((xf * rstd) * g_ref[...]).astype(BF16)
        u_ref[...] = ub
        pret_ref[...] = _dot_nt(ub, w_ref[:RET_COLS, :])
        p_rw = _dot_nt(ub, w_ref[RET_COLS:, :])
        prw_ref[...] = p_rw
        last_ref[0] = p_rw[tm - 1:tm, :]

        @pl.when(pl.program_id(0) == nt // 4)
        def _():
            _gather_relay(_gather_copies(late_in, late_out, sems))

        @pl.when(pl.program_id(0) == nt - 1)
        def _():
            _gather_finish(_gather_copies(late_in, late_out, sems))

    return pl.pallas_call(
        body, name="inproj_fwd", grid=(nt,),
        in_specs=[pl.BlockSpec((tm, D_MODEL), lambda i: (i, 0)), _full((1, D_MODEL)), _full((IN_COLS, D_MODEL))]
                 + [ANY] * n,
        out_specs=[pl.BlockSpec((tm, RET_COLS), lambda i: (i, 0)), pl.BlockSpec((tm, RW_COLS), lambda i: (i, 0)),
                   pl.BlockSpec((tm, D_MODEL), lambda i: (i, 0)), pl.BlockSpec((1, 1, RW_COLS), lambda i: (i, 0, 0))]
                  + [ANY] * n,
        out_shape=[jax.ShapeDtypeStruct((T, RET_COLS), F32), jax.ShapeDtypeStruct((T, RW_COLS), F32),
                   jax.ShapeDtypeStruct((T, D_MODEL), BF16), jax.ShapeDtypeStruct((nt, 1, RW_COLS), F32)]
                  + _gather_shapes(late),
        scratch_shapes=_gather_sems(n),
        compiler_params=_cparams(dimension_semantics=("arbitrary",)),
    )(x, norm_g, w_t, *late)


def _rot_half(x):
    n = x.shape[1]
    lane = lax.broadcasted_iota(jnp.int32, x.shape, 1)
    first = (lane % RET_DK) < (RET_DK // 2)
    return jnp.where(first, -pltpu.roll(x, n - RET_DK // 2, 1), pltpu.roll(x, RET_DK // 2, 1))


def _rope(x, cos, sin):
    return x * cos + _rot_half(x) * sin


def _rope_bwd(d, cos, sin):
    return d * cos - _rot_half(d * sin)


def _ret_post(ret, g, gn_g):
    heads = []
    for h in range(RET_HEADS):
        xh = ret[:, h * RET_DV:(h + 1) * RET_DV]
        xc = xh - jnp.mean(xh, axis=-1, keepdims=True)
        heads.append(xc * lax.rsqrt(jnp.mean(xc * xc, axis=-1, keepdims=True) + RET_GN_EPS))
    return (g * jax.nn.sigmoid(g)) * (jnp.concatenate(heads, axis=1) * gn_g)


def _ret_scores(qt, kt, d_ref, h):
    lane = lax.broadcasted_iota(jnp.int32, qt.shape, 1)
    qh = jnp.where(lane // RET_DK == h, qt, 0.0)
    return qh, _dot_nt(qh, kt, B1) * d_ref[h]


def _ret_group(nch):
    return min(RET_GROUP, nch)


def _ret_fwd(p_ret, cos, sin, tabs, gn_g):
    T = p_ret.shape[0]
    C = RET_CHUNK
    nch = T // C
    G = _ret_group(nch)
    intra_d, qd, kd, cd, bm = tabs

    def body(q_ref, k_ref, v_ref, g_ref, cos_ref, sin_ref, qd_ref, kd_ref, d_ref, cd_ref, bm_ref, gn_ref,
             y_ref, ret_ref, sin_out_ref, s_ref, qt_buf, kv_buf):
        @pl.when(pl.program_id(0) == 0)
        def _():
            s_ref[...] = jnp.zeros_like(s_ref)

        cosv, sinv = cos_ref[...], sin_ref[...]
        qt_all = _rope(q_ref[...], cosv, sinv)
        kt_all = _rope(k_ref[...], cosv, sinv) * (RET_DK ** -0.5)
        for i in range(G):
            rows = slice(i * C, (i + 1) * C)
            qt, kt, v = qt_all[rows], kt_all[rows], v_ref[rows, :]
            intra = []
            for h in range(RET_HEADS):
                _, a = _ret_scores(qt, kt, d_ref, h)
                intra.append(_dot(a, v[:, h * RET_DV:(h + 1) * RET_DV], B1))
            ret_ref[rows, :] = jnp.concatenate(intra, axis=1)
            qt_buf[rows, :] = qt * qd_ref[...]
            kv_buf[i] = _dot_tn(kt * kd_ref[...], v, B1) * bm_ref[...]
        s_in = s_ref[...]
        for i in range(G):
            rows = slice(i * C, (i + 1) * C)
            sin_out_ref[i] = s_in
            ret_ref[rows, :] += _dot(qt_buf[rows, :], s_in, B1)
            s_in = s_in * cd_ref[...] + kv_buf[i]
        s_ref[...] = s_in
        y_ref[...] = _ret_post(ret_ref[...], g_ref[...], gn_ref[...]).astype(BF16)

    GC = G * C
    return pl.pallas_call(
        body, name="ret_fwd", grid=(nch // G,),
        in_specs=[pl.BlockSpec((GC, RET_QK), lambda c: (c, 0)), pl.BlockSpec((GC, RET_QK), lambda c: (c, 1)),
                  pl.BlockSpec((GC, RET_WIDTH), lambda c: (c, 1)), pl.BlockSpec((GC, RET_WIDTH), lambda c: (c, 2)),
                  pl.BlockSpec((GC, RET_QK), lambda c: (c, 0)), pl.BlockSpec((GC, RET_QK), lambda c: (c, 0)),
                  _full((C, RET_QK)), _full((C, RET_QK)), _full((RET_HEADS, C, C)),
                  _full((RET_QK, RET_WIDTH)), _full((RET_QK, RET_WIDTH)), _full((1, RET_WIDTH))],
        out_specs=[pl.BlockSpec((GC, RET_WIDTH), lambda c: (c, 0)), pl.BlockSpec((GC, RET_WIDTH), lambda c: (c, 0)),
                   pl.BlockSpec((G, RET_QK, RET_WIDTH), lambda c: (c, 0, 0))],
        out_shape=[jax.ShapeDtypeStruct((T, RET_WIDTH), BF16), jax.ShapeDtypeStruct((T, RET_WIDTH), F32),
                   jax.ShapeDtypeStruct((nch, RET_QK, RET_WIDTH), F32)],
        scratch_shapes=[pltpu.VMEM((RET_QK, RET_WIDTH), F32), pltpu.VMEM((GC, RET_QK), F32),
                        pltpu.VMEM((G, RET_QK, RET_WIDTH), F32)],
        compiler_params=_cparams(dimension_semantics=("arbitrary",)),
    )(p_ret, p_ret, p_ret, p_ret, cos, sin, qd, kd, intra_d, cd, bm, gn_g)


def _ret_bwd(p_ret, cos, sin, tabs, gn_g, ret, s_in_all, dy):
    T = p_ret.shape[0]
    C = RET_CHUNK
    nch = T // C
    G = _ret_group(nch)
    ngr = nch // G
    intra_d, qd, kd, cd, bm = tabs

    def rev(j):
        return lambda c: (ngr - 1 - c, j)

    def body(q_ref, k_ref, v_ref, g_ref, cos_ref, sin_ref, qd_ref, kd_ref, d_ref, cd_ref, bm_ref, gn_ref,
             ret_ref, sin_ref_, dy_ref, dp_ref, dgn_ref, ds_ref, dkt_buf, ktk_buf, g_buf):
        @pl.when(pl.program_id(0) == 0)
        def _():
            ds_ref[...] = jnp.zeros_like(ds_ref)
            dgn_ref[...] = jnp.zeros_like(dgn_ref)

        _, post_vjp = jax.vjp(_ret_post, ret_ref[...], g_ref[...], gn_ref[...])
        dret_all, dg_all, dgn = post_vjp(dy_ref[...])
        dgn_ref[...] += dgn
        v_cols = slice(2 * RET_QK, 2 * RET_QK + RET_WIDTH)
        dp_ref[:, 2 * RET_QK + RET_WIDTH:] = dg_all

        qdv, kdv = qd_ref[...], kd_ref[...]
        cosv, sinv = cos_ref[...], sin_ref[...]
        qt_all = _rope(q_ref[...], cosv, sinv)
        kt_all = _rope(k_ref[...], cosv, sinv) * (RET_DK ** -0.5)
        for i in range(G):
            rows = slice(i * C, (i + 1) * C)
            qt, kt, v, dret = qt_all[rows], kt_all[rows], v_ref[rows, :], dret_all[rows, :]
            dqt = qdv * _dot_nt(dret, sin_ref_[i], B1)
            dkt = jnp.zeros_like(kt)
            dvs = []
            for h in range(RET_HEADS):
                sl = slice(h * RET_DV, (h + 1) * RET_DV)
                qh, a = _ret_scores(qt, kt, d_ref, h)
                lane = lax.broadcasted_iota(jnp.int32, kt.shape, 1)
                kh = jnp.where(lane // RET_DK == h, kt, 0.0)
                da = _dot_nt(dret[:, sl], v[:, sl], B1) * d_ref[h]
                dvs.append(_dot_tn(a, dret[:, sl], B1))
                dqt = dqt + _dot(da, kh, B1)
                dkt = dkt + _dot_tn(da, qh, B1)
            dp_ref[rows, :RET_QK] = _rope_bwd(dqt, cosv[rows], sinv[rows])
            dp_ref[rows, v_cols] = jnp.concatenate(dvs, axis=1)
            dkt_buf[rows, :] = dkt
            ktk_buf[rows, :] = kt * kdv
            g_buf[i] = _dot_tn(qt * qdv, dret, B1) * bm_ref[...]
        ds_out = ds_ref[...]
        for i in reversed(range(G)):
            rows = slice(i * C, (i + 1) * C)
            dkt = dkt_buf[rows, :] + kdv * _dot_nt(v_ref[rows, :], ds_out, B1)
            dp_ref[rows, RET_QK:2 * RET_QK] = _rope_bwd(dkt * (RET_DK ** -0.5), cosv[rows], sinv[rows])
            dp_ref[rows, v_cols] += _dot(ktk_buf[rows, :], ds_out, B1)
            ds_out = ds_out * cd_ref[...] + g_buf[i]
        ds_ref[...] = ds_out

    GC = G * C
    return pl.pallas_call(
        body, name="ret_bwd", grid=(ngr,),
        in_specs=[pl.BlockSpec((GC, RET_QK), rev(0)), pl.BlockSpec((GC, RET_QK), rev(1)),
                  pl.BlockSpec((GC, RET_WIDTH), rev(1)), pl.BlockSpec((GC, RET_WIDTH), rev(2)),
                  pl.BlockSpec((GC, RET_QK), rev(0)), pl.BlockSpec((GC, RET_QK), rev(0)),
                  _full((C, RET_QK)), _full((C, RET_QK)), _full((RET_HEADS, C, C)),
                  _full((RET_QK, RET_WIDTH)), _full((RET_QK, RET_WIDTH)), _full((1, RET_WIDTH)),
                  pl.BlockSpec((GC, RET_WIDTH), rev(0)),
                  pl.BlockSpec((G, RET_QK, RET_WIDTH), lambda c: (ngr - 1 - c, 0, 0)),
                  pl.BlockSpec((GC, RET_WIDTH), rev(0))],
        out_specs=[pl.BlockSpec((GC, RET_COLS), rev(0)), _full((1, RET_WIDTH))],
        out_shape=[jax.ShapeDtypeStruct((T, RET_COLS), F32), jax.ShapeDtypeStruct((1, RET_WIDTH), F32)],
        scratch_shapes=[pltpu.VMEM((RET_QK, RET_WIDTH), F32), pltpu.VMEM((GC, RET_QK), F32),
                        pltpu.VMEM((GC, RET_QK), F32), pltpu.VMEM((G, RET_QK, RET_WIDTH), F32)],
        compiler_params=_cparams(dimension_semantics=("arbitrary",)),
    )(p_ret, p_ret, p_ret, p_ret, cos, sin, qd, kd, intra_d, cd, bm, gn_g, ret, s_in_all, dy)


@jax.custom_vjp
def _chunk_sums(x, tri):
    hi, lo = _split(x)
    return _dot(tri, hi) + _dot(tri, lo)


def _chunk_sums_fwd(x, tri):
    return _chunk_sums(x, tri), tri


def _chunk_sums_bwd(tri, ct):
    hi, lo = _split(ct)
    return _dot_tn(tri, hi) + _dot_tn(tri, lo), jnp.zeros_like(tri)


_chunk_sums.defvjp(_chunk_sums_fwd, _chunk_sums_bwd)


@jax.custom_vjp
def _lora_dot(z, lora):
    return _dot(z, lora, B1)


def _lora_dot_fwd(z, lora):
    return _lora_dot(z, lora), (z, lora)


def _lora_dot_bwd(res, ct):
    z, lora = res
    return _dot_nt(ct, lora, B1), _dot_tn(z, ct, B1)


_lora_dot.defvjp(_lora_dot_fwd, _lora_dot_bwd)


def _chunk_tables(tm):
    t = np.arange(tm)
    same = (t[:, None] // WKV_CHUNK) == (t[None, :] // WKV_CHUNK)
    return jnp.asarray(np.stack([same & (t[None, :] <= t[:, None]), same]).astype(np.float32), dtype=BF16)


def _prep_fn(p, prev, mu, w0, a0, k_k, k_a, lora, seg, tri):
    W = RW_WIDTH
    ps = p + mu * (prev - p)
    r, kr, vr, g = ps[:, 0:W], ps[:, W:2 * W], ps[:, 2 * W:3 * W], ps[:, 3 * W:4 * W]
    z = ps[:, 4 * W:]
    lane = lax.broadcasted_iota(jnp.int32, z.shape, 1)
    z = jnp.where(lane < LORA, jnp.tanh(z), z)
    lo = _lora_dot(z, lora)
    w_log = -_softplus(-(w0 + lo[:, :W])) - 0.5
    log_decay = -jnp.exp(w_log)
    cum = _chunk_sums(log_decay, tri[0])
    total = _chunk_sums(log_decay, tri[1])
    a = jax.nn.sigmoid(a0 + lo[:, W:])
    kk = kr * k_k
    kk = kk / jnp.maximum(jnp.sqrt(_segsum(kk * kk, seg)), 1e-12)
    k = kr * (1.0 + (a - 1.0) * k_a)
    grow = jnp.exp(-cum)
    return kk * jnp.exp(cum - log_decay), (kk * a) * grow, k * grow, r * jnp.exp(cum), jnp.exp(total), vr, g, r * k


def _post_fn(o, rk, v, g, gn_g, gn_b, r_k, seg):
    mu = _segsum(o, seg) * (1.0 / RW_HEAD)
    oc = o - mu
    var = _segsum(oc * oc, seg) * (1.0 / RW_HEAD)
    on = oc * lax.rsqrt(var + RW_GN_EPS) * gn_g + gn_b
    bonus = _segsum(rk * r_k, seg) * v
    return (g * jax.nn.sigmoid(g)) * (on + bonus)


N_PAIR = (N_VEC + 1) // 2
HALF_LANES = 64


def _swap_halves(x):
    return pltpu.roll(x, HALF_LANES, 1)


def _pack_heads(vecs):
    tm = vecs[0].shape[0]
    low = lax.broadcasted_iota(jnp.int32, (tm, 128), 1) < HALF_LANES
    out = []
    for p in range(N_PAIR):
        a = vecs[2 * p]
        b = vecs[2 * p + 1] if 2 * p + 1 < len(vecs) else None
        heads = []
        for m in range(RW_WIDTH // 128):
            am = a[:, m * 128:(m + 1) * 128]
            bm = jnp.zeros_like(am) if b is None else b[:, m * 128:(m + 1) * 128]
            heads.append(jnp.where(low, am, _swap_halves(bm)))
            heads.append(jnp.where(low, _swap_halves(am), bm))
        out.append(heads)
    return out


def _unpack_heads(hm_ref):
    tm = hm_ref.shape[2]
    low = lax.broadcasted_iota(jnp.int32, (tm, 128), 1) < HALF_LANES
    vecs = []
    for p in range(N_PAIR):
        a, b = [], []
        for m in range(RW_WIDTH // 128):
            even, odd = hm_ref[p, 2 * m], hm_ref[p, 2 * m + 1]
            a.append(jnp.where(low, even, _swap_halves(odd)))
            b.append(jnp.where(low, _swap_halves(even), odd))
        vecs += [jnp.concatenate(a, axis=1), jnp.concatenate(b, axis=1)]
    return vecs[:N_VEC]


def _shift_down(p, first_row):
    row = lax.broadcasted_iota(jnp.int32, p.shape, 0)
    return jnp.where(row == 0, first_row, pltpu.roll(p, 1, 0))


def _shift_up(p, last_row):
    n = p.shape[0]
    row = lax.broadcasted_iota(jnp.int32, p.shape, 0)
    return jnp.where(row == n - 1, last_row, pltpu.roll(p, n - 1, 0))


def _row_tile(T):
    return min(T, 256)


def _prep_fwd(p_rw, bnd, mu, w0, a0, k_k, k_a, lora, seg64, tri):
    T = p_rw.shape[0]
    tm = _row_tile(T)
    W = RW_WIDTH

    def body(p_ref, bnd_ref, mu_ref, w0_ref, a0_ref, kk_ref, ka_ref, lora_ref, seg_ref, tri_ref,
             hm_ref, v_ref, g_ref, rk_ref):
        p = p_ref[...]
        prev = _shift_down(p, jnp.where(pl.program_id(0) == 0, 0.0, bnd_ref[0]))
        res = _prep_fn(p, prev, mu_ref[...], w0_ref[...], a0_ref[...], kk_ref[...], ka_ref[...], lora_ref[...],
                       seg_ref[...], (tri_ref[0], tri_ref[1]))
        for pair, heads in enumerate(_pack_heads(res[:N_VEC])):
            for h, val in enumerate(heads):
                hm_ref[pair, h] = val
        v_ref[...] = res[N_VEC]
        g_ref[...] = res[N_VEC + 1]
        rk_ref[...] = res[N_VEC + 2]

    small = _full((1, W))
    row = pl.BlockSpec((tm, W), lambda i: (i, 0))
    return pl.pallas_call(
        body, name="rwkv_prep_fwd", grid=(T // tm,),
        in_specs=[pl.BlockSpec((tm, RW_COLS), lambda i: (i, 0)),
                  pl.BlockSpec((1, 1, RW_COLS), lambda i: (jnp.maximum(i - 1, 0), 0, 0)),
                  _full((1, RW_COLS)), small, small, small, small, _full((2 * LORA, 2 * W)), _full((W, W)),
                  _full((2, tm, tm))],
        out_specs=[pl.BlockSpec((N_PAIR, RW_HEADS, tm, 128), lambda i: (0, 0, i, 0)), row, row, row],
        out_shape=[jax.ShapeDtypeStruct((N_PAIR, RW_HEADS, T, 128), F32)] + [jax.ShapeDtypeStruct((T, W), F32)] * 3,
        compiler_params=_cparams(dimension_semantics=("arbitrary",)),
    )(p_rw, bnd, mu, w0, a0, k_k, k_a, lora, seg64, tri)


def _prep_bwd(p_rw, bnd, mu, w0, a0, k_k, k_a, lora, seg64, tri, cts):
    T = p_rw.shape[0]
    tm = _row_tile(T)
    W = RW_WIDTH

    def body(p_ref, bnd_ref, mu_ref, w0_ref, a0_ref, kk_ref, ka_ref, lora_ref, seg_ref, tri_ref,
             dhm_ref, drk_ref, dv1_ref, dv2_ref, dg_ref,
             dp_ref, dprev_ref, dfirst_ref, dmu_ref, dw0_ref, da0_ref, dkk_p_ref, dka_ref, dlora_ref):
        accs = (dmu_ref, dw0_ref, da0_ref, dkk_p_ref, dka_ref, dlora_ref)

        @pl.when(pl.program_id(0) == 0)
        def _():
            for a_ref in accs:
                a_ref[...] = jnp.zeros_like(a_ref)

        p = p_ref[...]
        prev = _shift_down(p, jnp.where(pl.program_id(0) == 0, 0.0, bnd_ref[0]))
        seg, tri = seg_ref[...], (tri_ref[0], tri_ref[1])
        _, vjp = jax.vjp(lambda *a: _prep_fn(*a, seg, tri), p, prev, mu_ref[...], w0_ref[...], a0_ref[...],
                         kk_ref[...], ka_ref[...], lora_ref[...])
        ct = (*_unpack_heads(dhm_ref), dv1_ref[...] + dv2_ref[...], dg_ref[...], drk_ref[...])
        grads = vjp(ct)
        dp_ref[...] = grads[0]
        dprev_ref[...] = grads[1]
        dfirst_ref[0] = grads[1][0:1, :]
        for a_ref, gval in zip(accs, grads[2:]):
            a_ref[...] += gval

    small = _full((1, W))
    row = pl.BlockSpec((tm, W), lambda i: (i, 0))
    return pl.pallas_call(
        body, name="rwkv_prep_bwd", grid=(T // tm,),
        in_specs=[pl.BlockSpec((tm, RW_COLS), lambda i: (i, 0)),
                  pl.BlockSpec((1, 1, RW_COLS), lambda i: (jnp.maximum(i - 1, 0), 0, 0)),
                  _full((1, RW_COLS)), small, small, small, small, _full((2 * LORA, 2 * W)), _full((W, W)),
                  _full((2, tm, tm)), pl.BlockSpec((N_PAIR, RW_HEADS, tm, 128), lambda i: (0, 0, i, 0))] + [row] * 4,
        out_specs=[pl.BlockSpec((tm, RW_COLS), lambda i: (i, 0)), pl.BlockSpec((tm, RW_COLS), lambda i: (i, 0)),
                   pl.BlockSpec((1, 1, RW_COLS), lambda i: (i, 0, 0)),
                   _full((1, RW_COLS)), small, small, small, small, _full((2 * LORA, 2 * W))],
        out_shape=[jax.ShapeDtypeStruct((T, RW_COLS), F32), jax.ShapeDtypeStruct((T, RW_COLS), F32),
                   jax.ShapeDtypeStruct((T // tm, 1, RW_COLS), F32),
                   jax.ShapeDtypeStruct((1, RW_COLS), F32)] + [jax.ShapeDtypeStruct((1, W), F32)] * 4
                  + [jax.ShapeDtypeStruct((2 * LORA, 2 * W), F32)],
        compiler_params=_cparams(dimension_semantics=("arbitrary",)),
    )(p_rw, bnd, mu, w0, a0, k_k, k_a, lora, seg64, tri, *cts)


def _post_fwd(o, rk, v, g, gn_g, gn_b, r_k, seg64):
    T = o.shape[0]
    tm = _row_tile(T)
    W = RW_WIDTH

    def body(o_ref, rk_ref, v_ref, g_ref, gg_ref, gb_ref, rkp_ref, seg_ref, y_ref):
        y_ref[...] = _post_fn(o_ref[...], rk_ref[...], v_ref[...], g_ref[...], gg_ref[...], gb_ref[...],
                              rkp_ref[...], seg_ref[...]).astype(BF16)

    row = pl.BlockSpec((tm, W), lambda i: (i, 0))
    small = _full((1, W))
    return pl.pallas_call(
        body, name="rwkv_post_fwd", grid=(T // tm,),
        in_specs=[row] * 4 + [small] * 3 + [_full((W, W))],
        out_specs=row, out_shape=jax.ShapeDtypeStruct((T, W), BF16),
        compiler_params=_cparams(dimension_semantics=("arbitrary",)),
    )(o, rk, v, g, gn_g, gn_b, r_k, seg64)


def _post_bwd(o, rk, v, g, gn_g, gn_b, r_k, seg64, dy):
    T = o.shape[0]
    tm = _row_tile(T)
    W = RW_WIDTH

    def body(o_ref, rk_ref, v_ref, g_ref, gg_ref, gb_ref, rkp_ref, seg_ref, dy_ref,
             do_ref, drk_ref, dv_ref, dg_ref, dgg_ref, dgb_ref, drkp_ref):
        accs = (dgg_ref, dgb_ref, drkp_ref)

        @pl.when(pl.program_id(0) == 0)
        def _():
            for a_ref in accs:
                a_ref[...] = jnp.zeros_like(a_ref)

        seg = seg_ref[...]
        _, vjp = jax.vjp(lambda *a: _post_fn(*a, seg), o_ref[...], rk_ref[...], v_ref[...], g_ref[...],
                         gg_ref[...], gb_ref[...], rkp_ref[...])
        grads = vjp(dy_ref[...])
        for o_, gval in zip((do_ref, drk_ref, dv_ref, dg_ref), grads[:4]):
            o_[...] = gval
        for a_ref, gval in zip(accs, grads[4:]):
            a_ref[...] += gval

    row = pl.BlockSpec((tm, W), lambda i: (i, 0))
    small = _full((1, W))
    return pl.pallas_call(
        body, name="rwkv_post_bwd", grid=(T // tm,),
        in_specs=[row] * 4 + [small] * 3 + [_full((W, W)), pl.BlockSpec((tm, W), lambda i: (i, 1))],
        out_specs=[row] * 4 + [small] * 3,
        out_shape=[jax.ShapeDtypeStruct((T, W), F32)] * 4 + [jax.ShapeDtypeStruct((1, W), F32)] * 3,
        compiler_params=_cparams(dimension_semantics=("arbitrary",)),
    )(o, rk, v, g, gn_g, gn_b, r_k, seg64, dy)


def _wkv_lhs(hm_ref, rows):
    tiles = [jnp.transpose(hm_ref[p, :, rows, :].reshape(RW_HEADS * WKV_CHUNK, 128)) for p in range(N_PAIR)]
    hi, lo = _split(jnp.concatenate(tiles, axis=0)[:N_VEC * RW_HEAD])
    return jnp.concatenate([hi, lo], axis=1)


def _wkv_group(nch):
    return min(WKV_GROUP, nch)


N_STEP_VEC = N_VEC - 1
PAD_ROWS = 16


def _wkv_fwd(cols, v, e_tab):
    T = v.shape[0]
    Tc = WKV_CHUNK
    nch = T // Tc
    G = _wkv_group(nch)
    J, W = RW_HEAD, RW_WIDTH
    JS = N_STEP_VEC * J

    def body(cols_ref, v_ref, e_ref, o_ref, states_ref, sa_ref, s_ref):
        @pl.when(pl.program_id(0) == 0)
        def _():
            s_ref[...] = jnp.zeros_like(s_ref)

        st = s_ref[...]
        for c in range(G):
            lhs = _wkv_lhs(cols_ref, slice(c * Tc, (c + 1) * Tc))
            for t in range(Tc):
                row = slice(c * Tc + t, c * Tc + t + 1)
                ex = _dot(lhs[:JS], e_ref[t])
                states_ref[c * Tc + t] = st
                sa = -jnp.sum(st * ex[0:J], axis=0, keepdims=True)
                st = st + ex[J:2 * J] * sa + ex[2 * J:3 * J] * v_ref[row, :]
                sa_ref[row, :] = sa
                o_ref[row, :] = jnp.sum(st * ex[3 * J:4 * J], axis=0, keepdims=True)
            st = st * _dot(lhs[JS:], e_ref[Tc - 1])
        s_ref[...] = st

    GT = G * Tc
    return pl.pallas_call(
        body, name="wkv_fwd", grid=(nch // G,),
        in_specs=[pl.BlockSpec((N_PAIR, RW_HEADS, GT, 128), lambda c: (0, 0, c, 0)),
                  pl.BlockSpec((GT, W), lambda c: (c, 0)), _full((Tc, 2 * 128, W))],
        out_specs=[pl.BlockSpec((GT, W), lambda c: (c, 0)), pl.BlockSpec((GT, J, W), lambda c: (c, 0, 0)),
                   pl.BlockSpec((GT, W), lambda c: (c, 0))],
        out_shape=[jax.ShapeDtypeStruct((T, W), F32), jax.ShapeDtypeStruct((T, J, W), F32),
                   jax.ShapeDtypeStruct((T, W), F32)],
        scratch_shapes=[pltpu.VMEM((J, W), F32)],
        compiler_params=_cparams(dimension_semantics=("arbitrary",)),
    )(cols, v, e_tab)


def _wkv_bwd(cols, v, do, states, sa, e_tab, r_tab):
    T = v.shape[0]
    Tc = WKV_CHUNK
    nch = T // Tc
    G = _wkv_group(nch)
    ngr = nch // G
    J, W = RW_HEAD, RW_WIDTH
    JS = N_STEP_VEC * J
    blocks = [slice(b * 128, (b + 1) * 128) for b in range(W // 128)]

    def body(cols_ref, v_ref, do_ref, states_ref, sa_ref, e_ref, r_ref, dv_ref, dhm_ref, ds_ref):
        @pl.when(pl.program_id(0) == 0)
        def _():
            ds_ref[...] = jnp.zeros_like(ds_ref)

        d_carry = [ds_ref[:, b] for b in blocks]
        last = Tc - 1
        for c in reversed(range(G)):
            at = c * Tc
            rows = slice(at, at + Tc)
            lhs = _wkv_lhs(cols_ref, rows)
            ex = _dot(lhs, e_ref[last])
            dst, ends = [], []
            for i, b in enumerate(blocks):
                s_end = (states_ref[at + last, :, b] + ex[J:2 * J, b] * sa_ref[at + last:at + Tc, b]
                         + ex[2 * J:3 * J, b] * v_ref[at + last:at + Tc, b])
                ends.append((d_carry[i] * s_end).astype(BF16))
                dst.append(d_carry[i] * ex[JS:, b])
            d_decay = _dot(jnp.concatenate(ends, axis=1), r_ref[last])
            acc = jnp.zeros((JS + PAD_ROWS, 128), F32)
            for t in reversed(range(Tc)):
                row = slice(at + t, at + t + 1)
                if t != last:
                    ex = _dot(lhs[:JS], e_ref[t])
                dvs, prods = [], []
                for i, b in enumerate(blocks):
                    kk_e, b_e, k_e, r_e = (ex[n * J:(n + 1) * J, b] for n in range(N_STEP_VEC))
                    do_row, v_row, sa_row = do_ref[row, b], v_ref[row, b], sa_ref[row, b]
                    s_old = states_ref[at + t, :, b]
                    s_new = states_ref[at + t + 1, :, b] if t != last else s_old + b_e * sa_row + k_e * v_row
                    dsn = dst[i] + r_e * do_row
                    dsa = jnp.sum(dsn * b_e, axis=0, keepdims=True)
                    dvs.append(jnp.sum(dsn * k_e, axis=0, keepdims=True))
                    prods.append(jnp.concatenate(
                        [s_old * (-dsa), dsn * sa_row, dsn * v_row, s_new * do_row, jnp.zeros((PAD_ROWS, 128), F32)],
                        axis=0).astype(BF16))
                    dst[i] = dsn - kk_e * dsa
                dv_ref[row, :] = jnp.concatenate(dvs, axis=1)
                acc = acc + _dot(jnp.concatenate(prods, axis=1), r_ref[t])
            d_carry = dst
            tiles = jnp.concatenate([acc[:JS], d_decay, jnp.zeros((2 * N_PAIR * J - N_VEC * J, 128), F32)], axis=0)
            for p in range(N_PAIR):
                dhm_ref[p, :, rows, :] = jnp.transpose(tiles[p * 128:(p + 1) * 128]).reshape(RW_HEADS, Tc, 128)
        for i, b in enumerate(blocks):
            ds_ref[:, b] = d_carry[i]

    GT = G * Tc
    rev2 = lambda c: (ngr - 1 - c, 0)
    rev3 = lambda c: (ngr - 1 - c, 0, 0)
    rev_hm = lambda c: (0, 0, ngr - 1 - c, 0)
    hm_spec = pl.BlockSpec((N_PAIR, RW_HEADS, GT, 128), rev_hm)
    return pl.pallas_call(
        body, name="wkv_bwd", grid=(ngr,),
        in_specs=[hm_spec, pl.BlockSpec((GT, W), rev2), pl.BlockSpec((GT, W), rev2),
                  pl.BlockSpec((GT, J, W), rev3), pl.BlockSpec((GT, W), rev2),
                  _full((Tc, 2 * 128, W)), _full((Tc, W, 128))],
        out_specs=[pl.BlockSpec((GT, W), rev2), hm_spec],
        out_shape=[jax.ShapeDtypeStruct((T, W), F32), jax.ShapeDtypeStruct((N_PAIR, RW_HEADS, T, 128), F32)],
        scratch_shapes=[pltpu.VMEM((J, W), F32)],
        compiler_params=_cparams(dimension_semantics=("arbitrary",)),
    )(cols, v, do, states, sa, e_tab, r_tab)


def _outproj(x, y_ret, y_rw, w_out_b, target, gf):
    T = x.shape[0]
    tm = _row_tile(T)
    W = RW_WIDTH

    def body(x_ref, yr_ref, yw_ref, w_ref, t_ref, gf_ref, loss_ref, dh_ref, dy_ref, dw_ref, dgf_ref):
        @pl.when(pl.program_id(0) == 0)
        def _():
            loss_ref[...] = jnp.zeros_like(loss_ref)
            dw_ref[...] = jnp.zeros_like(dw_ref)
            dgf_ref[...] = jnp.zeros_like(dgf_ref)

        y = jnp.concatenate([yr_ref[...], yw_ref[...]], axis=1)
        w = w_ref[...]
        h = x_ref[...] + _dot(y, w)
        rstd = lax.rsqrt(jnp.mean(h * h, axis=-1, keepdims=True) + RMS_EPS)
        hn = h * rstd
        gfv = gf_ref[...]
        err = hn * gfv - t_ref[...]
        loss_ref[...] += 0.5 * jnp.sum(jnp.mean(err * err, axis=-1))
        dout = err * (1.0 / D_MODEL)
        dgf_ref[...] += jnp.sum(dout * hn, axis=0, keepdims=True)
        dhn = dout * gfv
        dh = rstd * (dhn - hn * jnp.mean(dhn * hn, axis=-1, keepdims=True))
        dh_ref[...] = dh
        dhb = dh.astype(BF16)
        dy_ref[...] = _dot_nt(dhb, w)
        dw_ref[...] += _dot_tn(y, dhb)

    return pl.pallas_call(
        body, name="outproj_loss", grid=(T // tm,),
        in_specs=[pl.BlockSpec((tm, D_MODEL), lambda i: (i, 0)), pl.BlockSpec((tm, W), lambda i: (i, 0)),
                  pl.BlockSpec((tm, W), lambda i: (i, 0)), _full((D_MODEL, D_MODEL)),
                  pl.BlockSpec((tm, D_MODEL), lambda i: (i, 0)), _full((1, D_MODEL))],
        out_specs=[_full((1, PACK_W)), pl.BlockSpec((tm, D_MODEL), lambda i: (i, 0)),
                   pl.BlockSpec((tm, D_MODEL), lambda i: (i, 0)), _full((D_MODEL, D_MODEL)), _full((1, D_MODEL))],
        out_shape=[jax.ShapeDtypeStruct((1, PACK_W), F32), jax.ShapeDtypeStruct((T, D_MODEL), F32),
                   jax.ShapeDtypeStruct((T, D_MODEL), F32), jax.ShapeDtypeStruct((D_MODEL, D_MODEL), F32),
                   jax.ShapeDtypeStruct((1, D_MODEL), F32)],
        compiler_params=_cparams(dimension_semantics=("arbitrary",)),
    )(x, y_ret, y_rw, w_out_b, target, gf)


def _inproj_bwd_x(dp_ret, dp_rw, dprev, dfirst, w_t, x, norm_g, dh):
    T = x.shape[0]
    tm = _row_tile(T)
    nt = T // tm

    def body(dpr_ref, dpw_ref, dprev_ref, dnext_ref, w_ref, x_ref, g_ref, dh_ref, gx_ref, dg_ref, dpt_ref):
        @pl.when(pl.program_id(0) == 0)
        def _():
            dg_ref[...] = jnp.zeros_like(dg_ref)

        next_row = jnp.where(pl.program_id(0) == nt - 1, 0.0, dnext_ref[0])
        dp = jnp.concatenate([dpr_ref[...], dpw_ref[...] + _shift_up(dprev_ref[...], next_row)], axis=1)
        dpt_ref[...] = jnp.transpose(dp).astype(BF16)
        du = _dot(dp.astype(BF16), w_ref[...])
        xf = x_ref[...]
        rstd = lax.rsqrt(jnp.mean(xf * xf, axis=-1, keepdims=True) + RMS_EPS)
        xn = xf * rstd
        dg_ref[...] += jnp.sum(du * xn, axis=0, keepdims=True)
        dxn = du * g_ref[...]
        gx_ref[...] = dh_ref[...] + rstd * (dxn - xn * jnp.mean(dxn * xn, axis=-1, keepdims=True))

    return pl.pallas_call(
        body, name="inproj_bwd_x", grid=(T // tm,),
        in_specs=[pl.BlockSpec((tm, RET_COLS), lambda i: (i, 0)), pl.BlockSpec((tm, RW_COLS), lambda i: (i, 0)),
                  pl.BlockSpec((tm, RW_COLS), lambda i: (i, 0)),
                  pl.BlockSpec((1, 1, RW_COLS), lambda i: (jnp.minimum(i + 1, nt - 1), 0, 0)),
                  _full((IN_COLS, D_MODEL)), pl.BlockSpec((tm, D_MODEL), lambda i: (i, 0)), _full((1, D_MODEL)),
                  pl.BlockSpec((tm, D_MODEL), lambda i: (i, 0))],
        out_specs=[pl.BlockSpec((tm, D_MODEL), lambda i: (i, 0)), _full((1, D_MODEL)),
                   pl.BlockSpec((IN_COLS, tm), lambda i: (0, i))],
        out_shape=[jax.ShapeDtypeStruct((T, D_MODEL), F32), jax.ShapeDtypeStruct((1, D_MODEL), F32),
                   jax.ShapeDtypeStruct((IN_COLS, T), BF16)],
        compiler_params=_cparams(dimension_semantics=("arbitrary",)),
    )(dp_ret, dp_rw, dprev, dfirst, w_t, x, norm_g, dh)


def _inproj_bwd_w(dp_t, u):
    T = u.shape[0]

    def body(d_ref, u_ref, o_ref):
        o_ref[...] = _dot(d_ref[...], u_ref[...])

    return pl.pallas_call(
        body, name="inproj_bwd_w", grid=(N_CHIPS,),
        in_specs=[pl.BlockSpec((IN_SHARD, T), lambda i: (i, 0)), _full((T, D_MODEL))],
        out_specs=pl.BlockSpec((IN_SHARD, D_MODEL), lambda i: (i, 0)),
        out_shape=jax.ShapeDtypeStruct((IN_COLS, D_MODEL), F32),
        compiler_params=_cparams(dimension_semantics=("arbitrary",)),
    )(dp_t, u)


def _with_own(gathered, own, chip):
    return lax.dynamic_update_slice(gathered, own[None], (chip, 0, 0, 0))


def _local_step(x, target, w_in_t, late, chip, small):
    T = x.shape[0]
    tm = _row_tile(T)
    W = RW_WIDTH
    cos, sin = _rope_tables(T)
    tabs = _ret_tables()
    seg64 = _seg_matrix(RW_WIDTH, RW_HEAD)
    e_tab = _wkv_expand_table()
    r_tab = _wkv_reduce_table()

    p_ret, p_rw, u, bnd, *gathered = _inproj_fwd(x, small["norm_g"], w_in_t, late)
    g_out, g_lw, g_la = (_with_own(g, own, chip) for g, own in zip(gathered, late))
    w_out_b = g_out.reshape(D_MODEL, D_MODEL)
    lw = jnp.transpose(g_lw.reshape(N_CHIPS, LORA, LORA_SHARD), (1, 0, 2)).reshape(LORA, W)
    la = jnp.transpose(g_la.reshape(N_CHIPS, LORA, LORA_SHARD), (1, 0, 2)).reshape(LORA, W)
    zero = jnp.zeros((LORA, W), F32)
    lora = jnp.concatenate([jnp.concatenate([lw, zero], axis=1), jnp.concatenate([zero, la], axis=1)], axis=0)
    prep_w = (small["rwkv_mu"], small["w0"], small["a0"], small["k_k"], small["k_a"], lora, seg64, _chunk_tables(tm))
    post_w = (small["rwkv_gn_g"], small["rwkv_gn_b"], small["r_k"], seg64)

    y_ret, ret, s_in_all = _ret_fwd(p_ret, cos, sin, tabs, small["ret_gn_g"])
    hm, v, g, rk = _prep_fwd(p_rw, bnd, *prep_w)
    o, states, sa = _wkv_fwd(hm, v, e_tab)
    y_rw = _post_fwd(o, rk, v, g, *post_w)
    loss, dh, dy, d_w_out, d_gf = _outproj(x, y_ret, y_rw, w_out_b, target, small["final_norm_g"])

    do, d_rk, dv2, dg, d_gn_g, d_gn_b, d_r_k = _post_bwd(o, rk, v, g, *post_w, dy)
    dv1, d_hm = _wkv_bwd(hm, v, do, states, sa, e_tab, r_tab)
    dp_rw, dprev, dfirst, d_mu, d_w0, d_a0, d_k_k, d_k_a, d_lora = _prep_bwd(
        p_rw, bnd, *prep_w, (d_hm, d_rk, dv1, dv2, dg))
    dp_ret, d_ret_gn = _ret_bwd(p_ret, cos, sin, tabs, small["ret_gn_g"], ret, s_in_all, dy)
    grad_x, d_norm_g, dp_t = _inproj_bwd_x(dp_ret, dp_rw, dprev, dfirst, w_in_t, x, small["norm_g"], dh)
    d_w_in = _inproj_bwd_w(dp_t, u)

    d_small = {"norm_g": d_norm_g, "ret_gn_g": d_ret_gn, "rwkv_mu": d_mu, "w0": d_w0, "a0": d_a0, "k_k": d_k_k,
               "k_a": d_k_a, "r_k": d_r_k, "rwkv_gn_g": d_gn_g, "rwkv_gn_b": d_gn_b, "final_norm_g": d_gf}
    return loss, grad_x, d_w_in, d_w_out, d_lora, d_small


ANY = pl.BlockSpec(memory_space=pl.ANY)
CHIP_FLIPS = ((0, 1), (1, 0), (1, 1))
N_FLIPS = len(CHIP_FLIPS)
LORA_SHARD = RW_WIDTH // N_CHIPS
HALF_IN = IN_SHARD // 2
HALF_OUT = OUT_SHARD // 2


def _position():
    return lax.axis_index("x"), lax.axis_index("y"), lax.axis_index("c")


def _flip(v, f):
    return 1 - v if f else v


def _finish(local, remote, landed):
    for cp in landed:
        cp.wait_recv()
    for cp in remote:
        cp.wait_send()
    for cp in local:
        cp.wait()


def _gather_copies(ins, outs, sems):
    send, recv, pass_send, pass_recv = sems
    x, y, c = _position()
    s = 2 * x + y
    sibling = (x, y, 1 - c)
    first, second = (1 - c, c), (c, 1 - c)
    (x1, y1), (x2, y2) = ((x + fx - 2 * x * fx, y + fy - 2 * y * fy) for fx, fy in (first, second))
    s1, s2, sd = 2 * x1 + y1, 2 * x2 + y2, 2 * (1 - x) + (1 - y)

    def copy(src, dst, pair, k, to):
        return pltpu.make_async_remote_copy(src_ref=src, dst_ref=dst, send_sem=pair[0].at[k], recv_sem=pair[1].at[k],
                                            device_id=to, device_id_type=MESH)

    at_once, direct, relayed, passed_in, sends = [], [], [], [], []
    for a in range(len(ins)):
        k = a * N_FLIPS
        own, out = ins[a].at[c], outs[a]
        ici = (send, recv)
        to_first = copy(own, out.at[s, c], ici, k, (x1, y1, c))
        to_second = copy(own, out.at[s, c], ici, k + 1, (x2, y2, c))
        relay = copy(out.at[s1, c], out.at[s1, c], ici, k + 2, (x2, y2, c))
        passes = [copy(out.at[slot, c], out.at[slot, c], (pass_send, pass_recv), k + j, sibling)
                  for j, slot in enumerate((s1, s2, sd))]
        at_once += [to_first, to_second]
        direct += [(copy(own, out.at[s1, c], ici, k, (x1, y1, c)), [relay, passes[0]]),
                   (copy(own, out.at[s2, c], ici, k + 1, (x2, y2, c)), [passes[1]])]
        relayed += [(copy(own, out.at[sd, c], ici, k + 2, (x2, y2, c)), [passes[2]])]
        passed_in += [copy(out.at[slot, 1 - c], out.at[slot, 1 - c], (pass_send, pass_recv), k + j, sibling)
                      for j, slot in enumerate((s2, s1, sd))]
        sends += [to_first, to_second, relay] + passes
    return at_once, direct, relayed, passed_in, sends


def _then(steps):
    for arrived, then in steps:
        arrived.wait_recv()
        for cp in then:
            cp.start()


def _gather_start(copies):
    for cp in copies[0]:
        cp.start()


def _gather_relay(copies):
    _then(copies[1])


def _gather_finish(copies):
    _, _, relayed, passed_in, sends = copies
    _then(relayed)
    _finish([], sends, passed_in)


def _gather_sems(n):
    return [pltpu.SemaphoreType.DMA((n * N_FLIPS,))] * 4


def _gather_shapes(arrs):
    return [jax.ShapeDtypeStruct((N_CHIPS,) + a.shape, a.dtype) for a in arrs]


def _gather_chips(arrs):
    n = len(arrs)

    def body(*refs):
        copies = _gather_copies(refs[:n], refs[n:2 * n], refs[2 * n:])
        _gather_start(copies)
        _gather_relay(copies)
        _gather_finish(copies)

    return pl.pallas_call(
        body, name="gather_weights",
        in_specs=[ANY] * n, out_specs=[ANY] * n,
        out_shape=_gather_shapes(arrs), scratch_shapes=_gather_sems(n),
    )(*arrs)


def _pair_exchange(g_in, g_out, g_small):
    def body(gi_ref, go_ref, gs_ref, li_ref, lo_ref, ls_ref, send, recv):
        x, y, c = _position()
        peer = (x, y, 1 - c)
        srcs = (gi_ref.at[:, pl.ds((1 - c) * HALF_IN, HALF_IN), :], go_ref.at[:, pl.ds((1 - c) * HALF_OUT, HALF_OUT), :],
                gs_ref)
        remote = [pltpu.make_async_remote_copy(src_ref=src, dst_ref=dst, send_sem=send.at[k], recv_sem=recv.at[k],
                                               device_id=peer, device_id_type=MESH)
                  for k, (src, dst) in enumerate(zip(srcs, (li_ref, lo_ref, ls_ref)))]
        for cp in remote:
            cp.start()
        _finish([], remote, remote)

    return pl.pallas_call(
        body, name="pair_exchange",
        in_specs=[ANY] * 3, out_specs=[ANY] * 3,
        out_shape=[jax.ShapeDtypeStruct((N_CHIPS, HALF_IN, D_MODEL), F32),
                   jax.ShapeDtypeStruct((N_CHIPS, HALF_OUT, D_MODEL), F32),
                   jax.ShapeDtypeStruct(g_small.shape, F32)],
        scratch_shapes=[pltpu.SemaphoreType.DMA((3,)), pltpu.SemaphoreType.DMA((3,))],
    )(g_in, g_out, g_small)


def _pair_sum(g_in, g_out, g_small, l_in, l_out, l_small, c_arr):
    def body(c_ref, gi_ref, go_ref, gs_ref, li_ref, lo_ref, ls_ref, ci_ref, co_ref, cs_ref):
        ci_ref[...] = (gi_ref[...] + li_ref[...]).astype(BF16)
        co_ref[...] = (go_ref[...] + lo_ref[...]).astype(BF16)

        @pl.when(pl.program_id(0) == 0)
        def _():
            cs_ref[...] = gs_ref[...] + ls_ref[...]

    nd = g_small.shape
    return pl.pallas_call(
        body, name="pair_sum",
        grid_spec=pltpu.PrefetchScalarGridSpec(
            num_scalar_prefetch=1, grid=(N_CHIPS,),
            in_specs=[pl.BlockSpec((1, HALF_IN, D_MODEL), lambda s, c: (s, c[0], 0)),
                      pl.BlockSpec((1, HALF_OUT, D_MODEL), lambda s, c: (s, c[0], 0)),
                      pl.BlockSpec(nd, lambda s, c: (0, 0)),
                      pl.BlockSpec((1, HALF_IN, D_MODEL), lambda s, c: (s, 0, 0)),
                      pl.BlockSpec((1, HALF_OUT, D_MODEL), lambda s, c: (s, 0, 0)),
                      pl.BlockSpec(nd, lambda s, c: (0, 0))],
            out_specs=[pl.BlockSpec((1, HALF_IN, D_MODEL), lambda s, c: (s, 0, 0)),
                       pl.BlockSpec((1, HALF_OUT, D_MODEL), lambda s, c: (s, 0, 0)),
                       pl.BlockSpec(nd, lambda s, c: (0, 0))]),
        out_shape=[jax.ShapeDtypeStruct((N_CHIPS, HALF_IN, D_MODEL), BF16),
                   jax.ShapeDtypeStruct((N_CHIPS, HALF_OUT, D_MODEL), BF16), jax.ShapeDtypeStruct(nd, F32)],
        compiler_params=_cparams(dimension_semantics=("arbitrary",)),
    )(c_arr, g_in, g_out, g_small, l_in, l_out, l_small)


def _chip_exchange(c_in, c_out, c_small):
    def body(ci_ref, co_ref, cs_ref, li_ref, lo_ref, ls_ref, send, recv):
        x, y, c = _position()
        s = 2 * x + y
        remote = []
        for j, (fx, fy) in enumerate(CHIP_FLIPS):
            px, py = _flip(x, fx), _flip(y, fy)
            ps = 2 * px + py
            for a, (src, dst) in enumerate(((ci_ref.at[ps], li_ref.at[j]), (co_ref.at[ps], lo_ref.at[j]),
                                            (cs_ref, ls_ref.at[j]))):
                k = 3 * j + a
                remote.append(pltpu.make_async_remote_copy(src_ref=src, dst_ref=dst, send_sem=send.at[k],
                                                           recv_sem=recv.at[k], device_id=(px, py, c),
                                                           device_id_type=MESH))
        for cp in remote:
            cp.start()
        _finish([], remote, remote)

    return pl.pallas_call(
        body, name="chip_exchange",
        in_specs=[ANY] * 3, out_specs=[ANY] * 3,
        out_shape=[jax.ShapeDtypeStruct((N_FLIPS, HALF_IN, D_MODEL), c_in.dtype),
                   jax.ShapeDtypeStruct((N_FLIPS, HALF_OUT, D_MODEL), c_out.dtype),
                   jax.ShapeDtypeStruct((N_FLIPS,) + c_small.shape, F32)],
        scratch_shapes=[pltpu.SemaphoreType.DMA((3 * N_FLIPS,)), pltpu.SemaphoreType.DMA((3 * N_FLIPS,))],
    )(c_in, c_out, c_small)


def _chip_sum(g_in, g_out, p_in, p_out, c_small, l_in, l_out, l_small, sc_arr):
    nd = c_small.shape

    def body(s_ref, gi_ref, go_ref, pi_ref, po_ref, cs_ref, li0, li1, li2, lo0, lo1, lo2, ls_ref,
             ri_ref, ro_ref, rs_ref):
        ri_ref[...] = (((gi_ref[0] + pi_ref[0]) + li0[0].astype(F32)) + li1[0].astype(F32)) + li2[0].astype(F32)
        ro_ref[...] = (((go_ref[0] + po_ref[0]) + lo0[0].astype(F32)) + lo1[0].astype(F32)) + lo2[0].astype(F32)
        me = s_ref[0]
        parts = (cs_ref[...], ls_ref[0], ls_ref[1], ls_ref[2])

        def of_chip(s):
            m = jnp.bitwise_xor(me, s)
            return jnp.where(m == 0, parts[0], jnp.where(m == 1, parts[1], jnp.where(m == 2, parts[2], parts[3])))

        rs_ref[...] = ((of_chip(0) + of_chip(1)) + of_chip(2)) + of_chip(3)

    def flip_in(j):
        return pl.BlockSpec((1, HALF_IN, D_MODEL), lambda i, s: (j, 0, 0))

    def flip_out(j):
        return pl.BlockSpec((1, HALF_OUT, D_MODEL), lambda i, s: (j, 0, 0))

    return pl.pallas_call(
        body, name="chip_sum",
        grid_spec=pltpu.PrefetchScalarGridSpec(
            num_scalar_prefetch=1, grid=(1,),
            in_specs=[pl.BlockSpec((1, HALF_IN, D_MODEL), lambda i, s: (s[0], s[1], 0)),
                      pl.BlockSpec((1, HALF_OUT, D_MODEL), lambda i, s: (s[0], s[1], 0)),
                      pl.BlockSpec((1, HALF_IN, D_MODEL), lambda i, s: (s[0], 0, 0)),
                      pl.BlockSpec((1, HALF_OUT, D_MODEL), lambda i, s: (s[0], 0, 0)),
                      pl.BlockSpec(nd, lambda i, s: (0, 0)),
                      flip_in(0), flip_in(1), flip_in(2), flip_out(0), flip_out(1), flip_out(2),
                      pl.BlockSpec((N_FLIPS,) + nd, lambda i, s: (0, 0, 0))],
            out_specs=[pl.BlockSpec((HALF_IN, D_MODEL), lambda i, s: (0, 0)),
                       pl.BlockSpec((HALF_OUT, D_MODEL), lambda i, s: (0, 0)),
                       pl.BlockSpec(nd, lambda i, s: (0, 0))]),
        out_shape=[jax.ShapeDtypeStruct((HALF_IN, D_MODEL), F32), jax.ShapeDtypeStruct((HALF_OUT, D_MODEL), F32),
                   jax.ShapeDtypeStruct(nd, F32)],
        compiler_params=_cparams(dimension_semantics=("arbitrary",)),
    )(sc_arr, g_in, g_out, p_in, p_out, c_small, l_in, l_in, l_in, l_out, l_out, l_out, l_small)


def _pair_share(r_in, r_out):
    def body(ri_ref, ro_ref, li_ref, lo_ref, send, recv):
        x, y, c = _position()
        remote = [pltpu.make_async_remote_copy(src_ref=src, dst_ref=dst, send_sem=send.at[k], recv_sem=recv.at[k],
                                               device_id=(x, y, 1 - c), device_id_type=MESH)
                  for k, (src, dst) in enumerate(((ri_ref, li_ref), (ro_ref, lo_ref)))]
        for cp in remote:
            cp.start()
        _finish([], remote, remote)

    return pl.pallas_call(
        body, name="pair_share",
        in_specs=[ANY] * 2, out_specs=[ANY] * 2,
        out_shape=[jax.ShapeDtypeStruct(r_in.shape, F32), jax.ShapeDtypeStruct(r_out.shape, F32)],
        scratch_shapes=[pltpu.SemaphoreType.DMA((2,)), pltpu.SemaphoreType.DMA((2,))],
    )(r_in, r_out)


def _adam_update(w, g, m, v):
    mn = ADAM_B1 * m + (1.0 - ADAM_B1) * g
    vn = ADAM_B2 * v + (1.0 - ADAM_B2) * jnp.square(g)
    m_hat = mn / (1.0 - ADAM_B1 ** ADAM_STEP)
    v_hat = vn / (1.0 - ADAM_B2 ** ADAM_STEP)
    return -ADAM_LR * (m_hat / (jnp.sqrt(v_hat) + ADAM_EPS) + ADAM_WD * w), mn, vn


def _adamw(name, w, g_mine, g_theirs, m, v, core_arr, tr):
    rows, cols = w.shape
    per_half = rows // 2 // tr

    def body(c_ref, w_ref, gm_ref, gt_ref, m_ref, v_ref, g_ref, d_ref, nm_ref, nv_ref):
        mine = (pl.program_id(0) // per_half) == c_ref[0]
        g = jnp.where(mine, gm_ref[...], gt_ref[...])
        d, mn, vn = _adam_update(w_ref[...], g, m_ref[...], v_ref[...])
        g_ref[...] = g
        d_ref[...] = d
        nm_ref[...] = mn
        nv_ref[...] = vn

    spec = pl.BlockSpec((tr, cols), lambda i, c: (i, 0))
    half = pl.BlockSpec((tr, cols), lambda i, c: (i % per_half, 0))
    return pl.pallas_call(
        body, name=name,
        grid_spec=pltpu.PrefetchScalarGridSpec(
            num_scalar_prefetch=1, grid=(rows // tr,),
            in_specs=[spec, half, half, spec, spec], out_specs=[spec] * 4),
        out_shape=[jax.ShapeDtypeStruct((rows, cols), F32)] * 4,
        compiler_params=_cparams(dimension_semantics=("arbitrary",)),
    )(core_arr, w, g_mine, g_theirs, m, v)


def _row_pieces(n):
    return [(k, k * PACK_W, min(PACK_W, n - k * PACK_W)) for k in range(-(-n // PACK_W))]


def _pack_small(d_small, loss, d_lora):
    ns = len(SMALL_NAMES)

    def body(*refs):
        small_refs, (loss_ref, lora_ref, out_ref) = refs[:ns], refs[ns:]
        out_ref[...] = jnp.zeros_like(out_ref)
        out_ref[PACK_LORA_W:PACK_LORA_W + LORA, :] = lora_ref[:LORA, :RW_WIDTH]
        out_ref[PACK_LORA_A:PACK_LORA_A + LORA, :] = lora_ref[LORA:, RW_WIDTH:]
        for name, n, ref in zip(SMALL_NAMES, SMALL_SIZES, small_refs):
            for k, at, w in _row_pieces(n):
                out_ref[PACK_AT[name] + k:PACK_AT[name] + k + 1, 0:w] = ref[:, at:at + w]
        out_ref[PACK_LOSS:PACK_LOSS + 1, :] = loss_ref[...]

    return pl.pallas_call(body, name="pack_small", out_shape=jax.ShapeDtypeStruct((PACK_ROWS, PACK_W), F32),
                          compiler_params=_cparams())(*d_small, loss, d_lora)


def _adamw_small(tot, chip_arr, ws, ms, vs):
    ns = len(SMALL_NAMES)
    n_par = ns + 2

    def body(s_ref, tot_ref, glw_ref, gla_ref, *refs):
        w_refs, m_refs, v_refs = refs[:n_par], refs[n_par:2 * n_par], refs[2 * n_par:3 * n_par]
        outs = refs[3 * n_par:]
        g_refs, d_refs, nm_refs, nv_refs = (outs[i * n_par:(i + 1) * n_par] for i in range(4))
        grads = [jnp.concatenate([tot_ref[PACK_AT[name] + k:PACK_AT[name] + k + 1, 0:w] for k, _, w in _row_pieces(n)],
                                 axis=1) for name, n in zip(SMALL_NAMES, SMALL_SIZES)]
        grads += [glw_ref[...], gla_ref[...]]
        for i, g in enumerate(grads):
            d, mn, vn = _adam_update(w_refs[i][...], g, m_refs[i][...], v_refs[i][...])
            g_refs[i][...] = g
            d_refs[i][...] = d
            nm_refs[i][...] = mn
            nv_refs[i][...] = vn

    def whole(a):
        nd = a.ndim
        return pl.BlockSpec(a.shape, lambda i, s: (0,) * nd)

    shard = (LORA, LORA_SHARD)
    par_specs = [whole(a) for a in ws]
    res = pl.pallas_call(
        body, name="adamw_small",
        grid_spec=pltpu.PrefetchScalarGridSpec(
            num_scalar_prefetch=1, grid=(1,),
            in_specs=[whole(tot), pl.BlockSpec(shard, lambda i, s: (PACK_LORA_W // LORA, s[0])),
                      pl.BlockSpec(shard, lambda i, s: (PACK_LORA_A // LORA, s[0]))] + par_specs * 3,
            out_specs=par_specs * 4),
        out_shape=[jax.ShapeDtypeStruct(a.shape, F32) for a in ws] * 4,
        compiler_params=_cparams(dimension_semantics=("arbitrary",)),
    )(chip_arr, tot, tot, tot, *ws, *ms, *vs)
    return [res[i * n_par:(i + 1) * n_par] for i in range(4)]


def kernel(x, norm_g, w_in, ret_gn_g, rwkv_mu, w_lora_up, w0, a_lora_up, a0, k_k, k_a, r_k, rwkv_gn_g, rwkv_gn_b, w_out, final_norm_g, loss_target, m_norm_g, m_w_in, m_ret_gn_g, m_rwkv_mu, m_w_lora_up, m_w0, m_a_lora_up, m_a0, m_k_k, m_k_a, m_r_k, m_rwkv_gn_g, m_rwkv_gn_b, m_w_out, m_final_norm_g, v_norm_g, v_w_in, v_ret_gn_g, v_rwkv_mu, v_w_lora_up, v_w0, v_a_lora_up, v_a0, v_k_k, v_k_a, v_r_k, v_rwkv_gn_g, v_rwkv_gn_b, v_w_out, v_final_norm_g):
    W = RW_WIDTH
    params = dict(norm_g=norm_g, ret_gn_g=ret_gn_g, rwkv_mu=rwkv_mu, w0=w0, a0=a0, k_k=k_k, k_a=k_a, r_k=r_k,
                  rwkv_gn_g=rwkv_gn_g, rwkv_gn_b=rwkv_gn_b, final_norm_g=final_norm_g)
    moments_m = dict(norm_g=m_norm_g, ret_gn_g=m_ret_gn_g, rwkv_mu=m_rwkv_mu, w0=m_w0, a0=m_a0, k_k=m_k_k, k_a=m_k_a,
                     r_k=m_r_k, rwkv_gn_g=m_rwkv_gn_g, rwkv_gn_b=m_rwkv_gn_b, final_norm_g=m_final_norm_g)
    moments_v = dict(norm_g=v_norm_g, ret_gn_g=v_ret_gn_g, rwkv_mu=v_rwkv_mu, w0=v_w0, a0=v_a0, k_k=v_k_k, k_a=v_k_a,
                     r_k=v_r_k, rwkv_gn_g=v_rwkv_gn_g, rwkv_gn_b=v_rwkv_gn_b, final_norm_g=v_final_norm_g)
    xi, yi, ci = _position()
    chip = (2 * xi + yi).astype(jnp.int32)

    def halves(a):
        return a.reshape(2, a.shape[0] // 2, a.shape[1])

    w_t, m_t, v_t = (jnp.transpose(a[0]) for a in (w_in, m_w_in, v_w_in))
    own_in = halves(w_t.astype(BF16))
    w_in_t = _with_own(_gather_chips([own_in])[0], own_in, chip).reshape(IN_COLS, D_MODEL)
    late = [halves(w_out[0].astype(BF16)), halves(w_lora_up[0]), halves(a_lora_up[0])]
    small = {n: params[n].reshape(1, -1) for n in SMALL_NAMES}

    loss, grad_x, d_w_in, d_w_out, d_lora, d_small = _local_step(x[0], loss_target[0], w_in_t, late, chip, small)

    core = ci.astype(jnp.int32)
    gi = d_w_in.reshape(N_CHIPS, IN_SHARD, D_MODEL)
    go = d_w_out.reshape(N_CHIPS, OUT_SHARD, D_MODEL)
    gs = _pack_small([d_small[n] for n in SMALL_NAMES], loss, d_lora)
    p_in, p_out, p_small = _pair_exchange(gi, go, gs)
    c_in, c_out, c_small = _pair_sum(gi, go, gs, p_in, p_out, p_small, core.reshape(1))
    l_in, l_out, l_small = _chip_exchange(c_in, c_out, c_small)
    r_in, r_out, tot = _chip_sum(gi, go, p_in, p_out, c_small, l_in, l_out, l_small, jnp.stack([chip, core]))
    t_in, t_out = _pair_share(r_in, r_out)

    grad_w_in, d_in, nm_in, nv_in = (jnp.transpose(a) for a in _adamw(
        "adamw_w_in", w_t, r_in, t_in, m_t, v_t, core.reshape(1), HALF_IN // 2))
    grad_w_out, d_out, nm_out, nv_out = _adamw("adamw_w_out", w_out[0], r_out, t_out, m_w_out[0], v_w_out[0],
                                               core.reshape(1), HALF_OUT)
    par_names = SMALL_NAMES + ("w_lora_up", "a_lora_up")

    def operands(tree, lw_, la_):
        return [tree[n].reshape(1, -1) for n in SMALL_NAMES] + [lw_[0], la_[0]]

    res = _adamw_small(tot, chip.reshape(1), operands(params, w_lora_up, a_lora_up),
                       operands(moments_m, m_w_lora_up, m_a_lora_up), operands(moments_v, v_w_lora_up, v_a_lora_up))

    names = ("norm_g", "w_in", "ret_gn_g", "rwkv_mu", "w_lora_up", "w0", "a_lora_up", "a0", "k_k", "k_a", "r_k",
             "rwkv_gn_g", "rwkv_gn_b", "w_out", "final_norm_g")
    shapes = dict(w_in=w_in.shape, w_out=w_out.shape, w_lora_up=w_lora_up.shape, a_lora_up=a_lora_up.shape,
                  **{n: params[n].shape for n in SMALL_NAMES})

    def leaves(pars, big_in, big_out):
        tree = dict(zip(par_names, pars), w_in=big_in, w_out=big_out)
        return [tree[n].reshape(shapes[n]) for n in names]

    grads = leaves(res[0], grad_w_in, grad_w_out)
    deltas = leaves(res[1], d_in, d_out)
    new_m = leaves(res[2], nm_in, nm_out)
    new_v = leaves(res[3], nv_in, nv_out)
    return (tot[PACK_LOSS, 0], grad_x.reshape(x.shape), *grads, *deltas, *new_m, *new_v)
```

```python
import functools

import numpy as np
import jax
import jax.numpy as jnp
from jax import lax
from jax.experimental import pallas as pl
from jax.experimental.pallas import tpu as pltpu

F32 = jnp.float32
BF16 = jnp.bfloat16
X3 = "bf16x3"
B1 = "bf16"
MESH = pl.DeviceIdType.MESH

D_MODEL = 1024
N_CHIPS = 4
RET_HEADS = 4
RET_DK = 64
RET_DV = 128
RET_QK = RET_HEADS * RET_DK
RET_WIDTH = RET_HEADS * RET_DV
RET_COLS = 2 * RET_QK + 2 * RET_WIDTH
RET_CHUNK = 64
RET_GROUP = 8
RW_WIDTH = 512
RW_HEAD = 64
RW_HEADS = 8
LORA = 64
RW_COLS = 4 * RW_WIDTH + 2 * LORA
IN_COLS = RET_COLS + RW_COLS
IN_SHARD = IN_COLS // N_CHIPS
OUT_SHARD = D_MODEL // N_CHIPS
ROPE_BASE = 10000.0
RMS_EPS = 1e-6
RET_GN_EPS = 1e-5
RW_GN_EPS = 64e-5
WKV_CHUNK = 16
WKV_GROUP = 8
N_VEC = 5

ADAM_LR = 0.001
ADAM_B1 = 0.9
ADAM_B2 = 0.999
ADAM_EPS = 1e-08
ADAM_WD = 0.01
ADAM_STEP = 10

VMEM_LIMIT = 56 * 1024 * 1024

PACK_W = 512
SMALL_NAMES = ("norm_g", "ret_gn_g", "rwkv_mu", "w0", "a0", "k_k", "k_a", "r_k", "rwkv_gn_g", "rwkv_gn_b",
               "final_norm_g")
SMALL_SIZES = (1024, 512, 2176, 512, 512, 512, 512, 512, 512, 512, 1024)
PACK_LORA_W = 0
PACK_LORA_A = LORA
PACK_SMALL = 2 * LORA


def _pack_layout():
    rows, at = {}, PACK_SMALL
    for name, n in zip(SMALL_NAMES, SMALL_SIZES):
        rows[name] = at
        at += -(-n // PACK_W)
    return rows, at


PACK_AT, PACK_LOSS = _pack_layout()
PACK_ROWS = -(-(PACK_LOSS + 1) // 8) * 8


def _cparams(**kw):
    return pltpu.CompilerParams(vmem_limit_bytes=VMEM_LIMIT, **kw)


def _split(x):
    hi = x.astype(BF16)
    lo = (x - hi.astype(F32)).astype(BF16)
    return hi, lo


def _dot_dims(a, b, dims, precision):
    if precision == B1:
        a, b = a.astype(BF16), b.astype(BF16)
    if precision != X3:
        return lax.dot_general(a, b, dims, preferred_element_type=F32)
    (ah, al), (bh, bl) = _split(a), _split(b)
    dot = lambda u, w: lax.dot_general(u, w, dims, preferred_element_type=F32)
    return dot(ah, bh) + dot(ah, bl) + dot(al, bh)


def _dot(a, b, precision=None):
    return _dot_dims(a, b, (((1,), (0,)), ((), ())), precision)


def _dot_nt(a, b, precision=None):
    return _dot_dims(a, b, (((1,), (1,)), ((), ())), precision)


def _dot_tn(a, b, precision=None):
    return _dot_dims(a, b, (((0,), (0,)), ((), ())), precision)


@jax.custom_vjp
def _segsum(x, seg):
    hi, lo = _split(x)
    return _dot(hi, seg) + _dot(lo, seg)


def _segsum_fwd(x, seg):
    return _segsum(x, seg), seg


def _segsum_bwd(seg, ct):
    return _segsum(ct, seg), jnp.zeros_like(seg)


_segsum.defvjp(_segsum_fwd, _segsum_bwd)


def _softplus(z):
    return jnp.maximum(z, 0.0) + jnp.log(1.0 + jnp.exp(-jnp.abs(z)))


def _full(shape):
    nd = len(shape)
    return pl.BlockSpec(shape, lambda *_: (0,) * nd)


def _rope_tables(T):
    half = RET_DK // 2
    expo = -jnp.arange(half, dtype=F32) / jnp.float32(half)
    freqs = jnp.exp(expo * jnp.float32(np.log(ROPE_BASE)))
    ang = jnp.arange(T, dtype=jnp.int32).astype(F32)[:, None] * freqs[None, :]
    cos = jnp.tile(jnp.cos(ang), (1, 2 * RET_HEADS))
    sin = jnp.tile(jnp.sin(ang), (1, 2 * RET_HEADS))
    return cos, sin


def _ret_tables():
    H, C = RET_HEADS, RET_CHUNK
    hidx = jnp.arange(H, dtype=F32)
    lg = jnp.log(1.0 - jnp.exp2(-5.0 - hidx))
    idx = jnp.arange(C, dtype=F32)
    intra = jnp.exp(lg[:, None, None] * jnp.abs(idx[:, None] - idx[None, :]))
    q_dec = jnp.transpose(jnp.exp(lg[:, None] * (idx[None, :] + 1.0)))
    k_dec = jnp.transpose(jnp.exp(lg[:, None] * (C - 1.0 - idx[None, :])))
    chunk_dec = jnp.exp(lg * C)
    qd = jnp.repeat(q_dec, RET_DK, axis=1)
    kd = jnp.repeat(k_dec, RET_DK, axis=1)
    row_h = np.arange(RET_QK) // RET_DK
    col_h = np.arange(RET_WIDTH) // RET_DV
    bm = jnp.asarray((row_h[:, None] == col_h[None, :]).astype(np.float32))
    cd = bm * jnp.repeat(chunk_dec, RET_DK)[:, None]
    return intra, qd, kd, cd, bm


def _seg_matrix(width, head):
    h = np.arange(width) // head
    return jnp.asarray((h[:, None] == h[None, :]).astype(np.float32), dtype=BF16)


def _wkv_expand_table():
    Tc = WKV_CHUNK
    k = np.arange(2 * RW_HEADS * Tc)
    kh, kt = (k % (RW_HEADS * Tc)) // Tc, k % Tc
    nh = np.arange(RW_WIDTH) // RW_HEAD
    e = (kh[None, :, None] == nh[None, None, :]) & (kt[None, :, None] == np.arange(Tc)[:, None, None])
    return jnp.asarray(e.astype(np.float32), dtype=BF16)


def _wkv_reduce_table():
    Tc = WKV_CHUNK
    kh = np.arange(RW_WIDTH) // RW_HEAD
    n = np.arange(RW_HEADS * Tc)
    nh, nt = n // Tc, n % Tc
    r = (kh[None, :, None] == nh[None, None, :]) & (nt[None, None, :] == np.arange(Tc)[:, None, None])
    return jnp.asarray(r.astype(np.float32), dtype=BF16)


def _inproj_fwd(x, norm_g, w_t, late):
    T = x.shape[0]
    tm = _row_tile(T)
    nt = T // tm
    n = len(late)

    def body(x_ref, g_ref, w_ref, *refs):
        late_in, (pret_ref, prw_ref, u_ref, last_ref) = refs[:n], refs[n:n + 4]
        late_out, sems = refs[n + 4:2 * n + 4], refs[2 * n + 4:]

        @pl.when(pl.program_id(0) == 0)
        def _():
            _gather_start(_gather_copies(late_in, late_out, sems))

        xf = x_ref[...]
        rstd = lax.rsqrt(jnp.mean(xf * xf, axis=-1, keepdims=True) + RMS_EPS)
        ub = ((xf * rstd) * g_ref[...]).astype(BF16)
        u_ref[...] = ub
        pret_ref[...] = _dot_nt(ub, w_ref[:RET_COLS, :])
        p_rw = _dot_nt(ub, w_ref[RET_COLS:, :])
        prw_ref[...] = p_rw
        last_ref[0] = p_rw[tm - 1:tm, :]

        @pl.when(pl.program_id(0) == nt // 4)
        def _():
            _gather_relay(_gather_copies(late_in, late_out, sems))

        @pl.when(pl.program_id(0) == nt - 1)
        def _():
            _gather_finish(_gather_copies(late_in, late_out, sems))

    return pl.pallas_call(
        body, name="inproj_fwd", grid=(nt,),
        in_specs=[pl.BlockSpec((tm, D_MODEL), lambda i: (i, 0)), _full((1, D_MODEL)), _full((IN_COLS, D_MODEL))]
                 + [ANY] * n,
        out_specs=[pl.BlockSpec((tm, RET_COLS), lambda i: (i, 0)), pl.BlockSpec((tm, RW_COLS), lambda i: (i, 0)),
                   pl.BlockSpec((tm, D_MODEL), lambda i: (i, 0)), pl.BlockSpec((1, 1, RW_COLS), lambda i: (i, 0, 0))]
                  + [ANY] * n,
        out_shape=[jax.ShapeDtypeStruct((T, RET_COLS), F32), jax.ShapeDtypeStruct((T, RW_COLS), F32),
                   jax.ShapeDtypeStruct((T, D_MODEL), BF16), jax.ShapeDtypeStruct((nt, 1, RW_COLS), F32)]
                  + _gather_shapes(late),
        scratch_shapes=_gather_sems(n),
        compiler_params=_cparams(dimension_semantics=("arbitrary",)),
    )(x, norm_g, w_t, *late)


def _rot_half(x):
    n = x.shape[1]
    lane = lax.broadcasted_iota(jnp.int32, x.shape, 1)
    first = (lane % RET_DK) < (RET_DK // 2)
    return jnp.where(first, -pltpu.roll(x, n - RET_DK // 2, 1), pltpu.roll(x, RET_DK // 2, 1))


def _rope(x, cos, sin):
    return x * cos + _rot_half(x) * sin


def _rope_bwd(d, cos, sin):
    return d * cos - _rot_half(d * sin)


def _ret_post(ret, g, gn_g):
    heads = []
    for h in range(RET_HEADS):
        xh = ret[:, h * RET_DV:(h + 1) * RET_DV]
        xc = xh - jnp.mean(xh, axis=-1, keepdims=True)
        heads.append(xc * lax.rsqrt(jnp.mean(xc * xc, axis=-1, keepdims=True) + RET_GN_EPS))
    return (g * jax.nn.sigmoid(g)) * (jnp.concatenate(heads, axis=1) * gn_g)


def _ret_scores(qt, kt, d_ref, h):
    lane = lax.broadcasted_iota(jnp.int32, qt.shape, 1)
    qh = jnp.where(lane // RET_DK == h, qt, 0.0)
    return qh, _dot_nt(qh, kt, B1) * d_ref[h]


def _ret_group(nch):
    return min(RET_GROUP, nch)


def _ret_fwd(p_ret, cos, sin, tabs, gn_g):
    T = p_ret.shape[0]
    C = RET_CHUNK
    nch = T // C
    G = _ret_group(nch)
    intra_d, qd, kd, cd, bm = tabs

    def body(q_ref, k_ref, v_ref, g_ref, cos_ref, sin_ref, qd_ref, kd_ref, d_ref, cd_ref, bm_ref, gn_ref,
             y_ref, ret_ref, sin_out_ref, s_ref, qt_buf, kv_buf):
        @pl.when(pl.program_id(0) == 0)
        def _():
            s_ref[...] = jnp.zeros_like(s_ref)

        cosv, sinv = cos_ref[...], sin_ref[...]
        qt_all = _rope(q_ref[...], cosv, sinv)
        kt_all = _rope(k_ref[...], cosv, sinv) * (RET_DK ** -0.5)
        for i in range(G):
            rows = slice(i * C, (i + 1) * C)
            qt, kt, v = qt_all[rows], kt_all[rows], v_ref[rows, :]
            intra = []
            for h in range(RET_HEADS):
                _, a = _ret_scores(qt, kt, d_ref, h)
                intra.append(_dot(a, v[:, h * RET_DV:(h + 1) * RET_DV], B1))
            ret_ref[rows, :] = jnp.concatenate(intra, axis=1)
            qt_buf[rows, :] = qt * qd_ref[...]
            kv_buf[i] = _dot_tn(kt * kd_ref[...], v, B1) * bm_ref[...]
        s_in = s_ref[...]
        for i in range(G):
            rows = slice(i * C, (i + 1) * C)
            sin_out_ref[i] = s_in
            ret_ref[rows, :] += _dot(qt_buf[rows, :], s_in, B1)
            s_in = s_in * cd_ref[...] + kv_buf[i]
        s_ref[...] = s_in
        y_ref[...] = _ret_post(ret_ref[...], g_ref[...], gn_ref[...]).astype(BF16)

    GC = G * C
    return pl.pallas_call(
        body, name="ret_fwd", grid=(nch // G,),
        in_specs=[pl.BlockSpec((GC, RET_QK), lambda c: (c, 0)), pl.BlockSpec((GC, RET_QK), lambda c: (c, 1)),
                  pl.BlockSpec((GC, RET_WIDTH), lambda c: (c, 1)), pl.BlockSpec((GC, RET_WIDTH), lambda c: (c, 2)),
                  pl.BlockSpec((GC, RET_QK), lambda c: (c, 0)), pl.BlockSpec((GC, RET_QK), lambda c: (c, 0)),
                  _full((C, RET_QK)), _full((C, RET_QK)), _full((RET_HEADS, C, C)),
                  _full((RET_QK, RET_WIDTH)), _full((RET_QK, RET_WIDTH)), _full((1, RET_WIDTH))],
        out_specs=[pl.BlockSpec((GC, RET_WIDTH), lambda c: (c, 0)), pl.BlockSpec((GC, RET_WIDTH), lambda c: (c, 0)),
                   pl.BlockSpec((G, RET_QK, RET_WIDTH), lambda c: (c, 0, 0))],
        out_shape=[jax.ShapeDtypeStruct((T, RET_WIDTH), BF16), jax.ShapeDtypeStruct((T, RET_WIDTH), F32),
                   jax.ShapeDtypeStruct((nch, RET_QK, RET_WIDTH), F32)],
        scratch_shapes=[pltpu.VMEM((RET_QK, RET_WIDTH), F32), pltpu.VMEM((GC, RET_QK), F32),
                        pltpu.VMEM((G, RET_QK, RET_WIDTH), F32)],
        compiler_params=_cparams(dimension_semantics=("arbitrary",)),
    )(p_ret, p_ret, p_ret, p_ret, cos, sin, qd, kd, intra_d, cd, bm, gn_g)


def _ret_bwd(p_ret, cos, sin, tabs, gn_g, ret, s_in_all, dy):
    T = p_ret.shape[0]
    C = RET_CHUNK
    nch = T // C
    G = _ret_group(nch)
    ngr = nch // G
    intra_d, qd, kd, cd, bm = tabs

    def rev(j):
        return lambda c: (ngr - 1 - c, j)

    def body(q_ref, k_ref, v_ref, g_ref, cos_ref, sin_ref, qd_ref, kd_ref, d_ref, cd_ref, bm_ref, gn_ref,
             ret_ref, sin_ref_, dy_ref, dp_ref, dgn_ref, ds_ref, dkt_buf, ktk_buf, g_buf):
        @pl.when(pl.program_id(0) == 0)
        def _():
            ds_ref[...] = jnp.zeros_like(ds_ref)
            dgn_ref[...] = jnp.zeros_like(dgn_ref)

        _, post_vjp = jax.vjp(_ret_post, ret_ref[...], g_ref[...], gn_ref[...])
        dret_all, dg_all, dgn = post_vjp(dy_ref[...])
        dgn_ref[...] += dgn
        v_cols = slice(2 * RET_QK, 2 * RET_QK + RET_WIDTH)
        dp_ref[:, 2 * RET_QK + RET_WIDTH:] = dg_all

        qdv, kdv = qd_ref[...], kd_ref[...]
        cosv, sinv = cos_ref[...], sin_ref[...]
        qt_all = _rope(q_ref[...], cosv, sinv)
        kt_all = _rope(k_ref[...], cosv, sinv) * (RET_DK ** -0.5)
        for i in range(G):
            rows = slice(i * C, (i + 1) * C)
            qt, kt, v, dret = qt_all[rows], kt_all[rows], v_ref[rows, :], dret_all[rows, :]
            dqt = qdv * _dot_nt(dret, sin_ref_[i], B1)
            dkt = jnp.zeros_like(kt)
            dvs = []
            for h in range(RET_HEADS):
                sl = slice(h * RET_DV, (h + 1) * RET_DV)
                qh, a = _ret_scores(qt, kt, d_ref, h)
                lane = lax.broadcasted_iota(jnp.int32, kt.shape, 1)
                kh = jnp.where(lane // RET_DK == h, kt, 0.0)
                da = _dot_nt(dret[:, sl], v[:, sl], B1) * d_ref[h]
                dvs.append(_dot_tn(a, dret[:, sl], B1))
                dqt = dqt + _dot(da, kh, B1)
                dkt = dkt + _dot_tn(da, qh, B1)
            dp_ref[rows, :RET_QK] = _rope_bwd(dqt, cosv[rows], sinv[rows])
            dp_ref[rows, v_cols] = jnp.concatenate(dvs, axis=1)
            dkt_buf[rows, :] = dkt
            ktk_buf[rows, :] = kt * kdv
            g_buf[i] = _dot_tn(qt * qdv, dret, B1) * bm_ref[...]
        ds_out = ds_ref[...]
        for i in reversed(range(G)):
            rows = slice(i * C, (i + 1) * C)
            dkt = dkt_buf[rows, :] + kdv * _dot_nt(v_ref[rows, :], ds_out, B1)
            dp_ref[rows, RET_QK:2 * RET_QK] = _rope_bwd(dkt * (RET_DK ** -0.5), cosv[rows], sinv[rows])
            dp_ref[rows, v_cols] += _dot(ktk_buf[rows, :], ds_out, B1)
            ds_out = ds_out * cd_ref[...] + g_buf[i]
        ds_ref[...] = ds_out

    GC = G * C
    return pl.pallas_call(
        body, name="ret_bwd", grid=(ngr,),
        in_specs=[pl.BlockSpec((GC, RET_QK), rev(0)), pl.BlockSpec((GC, RET_QK), rev(1)),
                  pl.BlockSpec((GC, RET_WIDTH), rev(1)), pl.BlockSpec((GC, RET_WIDTH), rev(2)),
                  pl.BlockSpec((GC, RET_QK), rev(0)), pl.BlockSpec((GC, RET_QK), rev(0)),
                  _full((C, RET_QK)), _full((C, RET_QK)), _full((RET_HEADS, C, C)),
                  _full((RET_QK, RET_WIDTH)), _full((RET_QK, RET_WIDTH)), _full((1, RET_WIDTH)),
                  pl.BlockSpec((GC, RET_WIDTH), rev(0)),
                  pl.BlockSpec((G, RET_QK, RET_WIDTH), lambda c: (ngr - 1 - c, 0, 0)),
                  pl.BlockSpec((GC, RET_WIDTH), rev(0))],
        out_specs=[pl.BlockSpec((GC, RET_COLS), rev(0)), _full((1, RET_WIDTH))],
        out_shape=[jax.ShapeDtypeStruct((T, RET_COLS), F32), jax.ShapeDtypeStruct((1, RET_WIDTH), F32)],
        scratch_shapes=[pltpu.VMEM((RET_QK, RET_WIDTH), F32), pltpu.VMEM((GC, RET_QK), F32),
                        pltpu.VMEM((GC, RET_QK), F32), pltpu.VMEM((G, RET_QK, RET_WIDTH), F32)],
        compiler_params=_cparams(dimension_semantics=("arbitrary",)),
    )(p_ret, p_ret, p_ret, p_ret, cos, sin, qd, kd, intra_d, cd, bm, gn_g, ret, s_in_all, dy)


@jax.custom_vjp
def _chunk_sums(x, tri):
    hi, lo = _split(x)
    return _dot(tri, hi) + _dot(tri, lo)


def _chunk_sums_fwd(x, tri):
    return _chunk_sums(x, tri), tri


def _chunk_sums_bwd(tri, ct):
    hi, lo = _split(ct)
    return _dot_tn(tri, hi) + _dot_tn(tri, lo), jnp.zeros_like(tri)


_chunk_sums.defvjp(_chunk_sums_fwd, _chunk_sums_bwd)


@jax.custom_vjp
def _lora_dot(z, lora):
    return _dot(z, lora, B1)


def _lora_dot_fwd(z, lora):
    return _lora_dot(z, lora), (z, lora)


def _lora_dot_bwd(res, ct):
    z, lora = res
    return _dot_nt(ct, lora, B1), _dot_tn(z, ct, B1)


_lora_dot.defvjp(_lora_dot_fwd, _lora_dot_bwd)


def _chunk_tables(tm):
    t = np.arange(tm)
    same = (t[:, None] // WKV_CHUNK) == (t[None, :] // WKV_CHUNK)
    return jnp.asarray(np.stack([same & (t[None, :] <= t[:, None]), same]).astype(np.float32), dtype=BF16)


def _prep_fn(p, prev, mu, w0, a0, k_k, k_a, lora, seg, tri):
    W = RW_WIDTH
    ps = p + mu * (prev - p)
    r, kr, vr, g = ps[:, 0:W], ps[:, W:2 * W], ps[:, 2 * W:3 * W], ps[:, 3 * W:4 * W]
    z = ps[:, 4 * W:]
    lane = lax.broadcasted_iota(jnp.int32, z.shape, 1)
    z = jnp.where(lane < LORA, jnp.tanh(z), z)
    lo = _lora_dot(z, lora)
    w_log = -_softplus(-(w0 + lo[:, :W])) - 0.5
    log_decay = -jnp.exp(w_log)
    rows = p.shape[0]
    sums = _chunk_sums(log_decay, jnp.concatenate(tri, axis=0))
    cum, total = sums[:rows], sums[rows:]
    a = jax.nn.sigmoid(a0 + lo[:, W:])
    kk = kr * k_k
    kk = kk / jnp.maximum(jnp.sqrt(_segsum(kk * kk, seg)), 1e-12)
    k = kr * (1.0 + (a - 1.0) * k_a)
    grow = jnp.exp(-cum)
    return kk * jnp.exp(cum - log_decay), (kk * a) * grow, k * grow, r * jnp.exp(cum), jnp.exp(total), vr, g, r * k


def _post_fn(o, rk, v, g, gn_g, gn_b, r_k, seg):
    mu = _segsum(o, seg) * (1.0 / RW_HEAD)
    oc = o - mu
    var = _segsum(oc * oc, seg) * (1.0 / RW_HEAD)
    on = oc * lax.rsqrt(var + RW_GN_EPS) * gn_g + gn_b
    bonus = _segsum(rk * r_k, seg) * v
    return (g * jax.nn.sigmoid(g)) * (on + bonus)


N_PAIR = (N_VEC + 1) // 2
HALF_LANES = 64


def _swap_halves(x):
    return pltpu.roll(x, HALF_LANES, 1)


def _pack_heads(vecs):
    tm = vecs[0].shape[0]
    low = lax.broadcasted_iota(jnp.int32, (tm, 128), 1) < HALF_LANES
    out = []
    for p in range(N_PAIR):
        a = vecs[2 * p]
        b = vecs[2 * p + 1] if 2 * p + 1 < len(vecs) else None
        heads = []
        for m in range(RW_WIDTH // 128):
            am = a[:, m * 128:(m + 1) * 128]
            bm = jnp.zeros_like(am) if b is None else b[:, m * 128:(m + 1) * 128]
            heads.append(jnp.where(low, am, _swap_halves(bm)))
            heads.append(jnp.where(low, _swap_halves(am), bm))
        out.append(heads)
    return out


def _unpack_heads(hm_ref):
    tm = hm_ref.shape[2]
    low = lax.broadcasted_iota(jnp.int32, (tm, 128), 1) < HALF_LANES
    vecs = []
    for p in range(N_PAIR):
        a, b = [], []
        for m in range(RW_WIDTH // 128):
            even, odd = hm_ref[p, 2 * m], hm_ref[p, 2 * m + 1]
            a.append(jnp.where(low, even, _swap_halves(odd)))
            b.append(jnp.where(low, _swap_halves(even), odd))
        vecs += [jnp.concatenate(a, axis=1), jnp.concatenate(b, axis=1)]
    return vecs[:N_VEC]


def _shift_down(p, first_row):
    row = lax.broadcasted_iota(jnp.int32, p.shape, 0)
    return jnp.where(row == 0, first_row, pltpu.roll(p, 1, 0))


def _shift_up(p, last_row):
    n = p.shape[0]
    row = lax.broadcasted_iota(jnp.int32, p.shape, 0)
    return jnp.where(row == n - 1, last_row, pltpu.roll(p, n - 1, 0))


def _row_tile(T):
    return min(T, 256)


def _prep_fwd(p_rw, bnd, mu, w0, a0, k_k, k_a, lora, seg64, tri):
    T = p_rw.shape[0]
    tm = _row_tile(T)
    W = RW_WIDTH

    def body(p_ref, bnd_ref, mu_ref, w0_ref, a0_ref, kk_ref, ka_ref, lora_ref, seg_ref, tri_ref,
             hm_ref, v_ref, g_ref, rk_ref):
        p = p_ref[...]
        prev = _shift_down(p, jnp.where(pl.program_id(0) == 0, 0.0, bnd_ref[0]))
        res = _prep_fn(p, prev, mu_ref[...], w0_ref[...], a0_ref[...], kk_ref[...], ka_ref[...], lora_ref[...],
                       seg_ref[...], (tri_ref[0], tri_ref[1]))
        for pair, heads in enumerate(_pack_heads(res[:N_VEC])):
            for h, val in enumerate(heads):
                hm_ref[pair, h] = val
        v_ref[...] = res[N_VEC]
        g_ref[...] = res[N_VEC + 1]
        rk_ref[...] = res[N_VEC + 2]

    small = _full((1, W))
    row = pl.BlockSpec((tm, W), lambda i: (i, 0))
    return pl.pallas_call(
        body, name="rwkv_prep_fwd", grid=(T // tm,),
        in_specs=[pl.BlockSpec((tm, RW_COLS), lambda i: (i, 0)),
                  pl.BlockSpec((1, 1, RW_COLS), lambda i: (jnp.maximum(i - 1, 0), 0, 0)),
                  _full((1, RW_COLS)), small, small, small, small, _full((2 * LORA, 2 * W)), _full((W, W)),
                  _full((2, tm, tm))],
        out_specs=[pl.BlockSpec((N_PAIR, RW_HEADS, tm, 128), lambda i: (0, 0, i, 0)), row, row, row],
        out_shape=[jax.ShapeDtypeStruct((N_PAIR, RW_HEADS, T, 128), F32)] + [jax.ShapeDtypeStruct((T, W), F32)] * 3,
        compiler_params=_cparams(dimension_semantics=("arbitrary",)),
    )(p_rw, bnd, mu, w0, a0, k_k, k_a, lora, seg64, tri)


def _prep_bwd(p_rw, bnd, mu, w0, a0, k_k, k_a, lora, seg64, tri, cts):
    T = p_rw.shape[0]
    tm = _row_tile(T)
    W = RW_WIDTH

    def body(p_ref, bnd_ref, mu_ref, w0_ref, a0_ref, kk_ref, ka_ref, lora_ref, seg_ref, tri_ref,
             dhm_ref, drk_ref, dv1_ref, dv2_ref, dg_ref,
             dp_ref, dprev_ref, dfirst_ref, dmu_ref, dw0_ref, da0_ref, dkk_p_ref, dka_ref, dlora_ref):
        accs = (dmu_ref, dw0_ref, da0_ref, dkk_p_ref, dka_ref, dlora_ref)

        @pl.when(pl.program_id(0) == 0)
        def _():
            for a_ref in accs:
                a_ref[...] = jnp.zeros_like(a_ref)

        p = p_ref[...]
        prev = _shift_down(p, jnp.where(pl.program_id(0) == 0, 0.0, bnd_ref[0]))
        seg, tri = seg_ref[...], (tri_ref[0], tri_ref[1])
        _, vjp = jax.vjp(lambda *a: _prep_fn(*a, seg, tri), p, prev, mu_ref[...], w0_ref[...], a0_ref[...],
                         kk_ref[...], ka_ref[...], lora_ref[...])
        ct = (*_unpack_heads(dhm_ref), dv1_ref[...] + dv2_ref[...], dg_ref[...], drk_ref[...])
        grads = vjp(ct)
        dp_ref[...] = grads[0]
        dprev_ref[...] = grads[1]
        dfirst_ref[0] = grads[1][0:1, :]
        for a_ref, gval in zip(accs, grads[2:]):
            a_ref[...] += gval

    small = _full((1, W))
    row = pl.BlockSpec((tm, W), lambda i: (i, 0))
    return pl.pallas_call(
        body, name="rwkv_prep_bwd", grid=(T // tm,),
        in_specs=[pl.BlockSpec((tm, RW_COLS), lambda i: (i, 0)),
                  pl.BlockSpec((1, 1, RW_COLS), lambda i: (jnp.maximum(i - 1, 0), 0, 0)),
                  _full((1, RW_COLS)), small, small, small, small, _full((2 * LORA, 2 * W)), _full((W, W)),
                  _full((2, tm, tm)), pl.BlockSpec((N_PAIR, RW_HEADS, tm, 128), lambda i: (0, 0, i, 0))] + [row] * 4,
        out_specs=[pl.BlockSpec((tm, RW_COLS), lambda i: (i, 0)), pl.BlockSpec((tm, RW_COLS), lambda i: (i, 0)),
                   pl.BlockSpec((1, 1, RW_COLS), lambda i: (i, 0, 0)),
                   _full((1, RW_COLS)), small, small, small, small, _full((2 * LORA, 2 * W))],
        out_shape=[jax.ShapeDtypeStruct((T, RW_COLS), F32), jax.ShapeDtypeStruct((T, RW_COLS), F32),
                   jax.ShapeDtypeStruct((T // tm, 1, RW_COLS), F32),
                   jax.ShapeDtypeStruct((1, RW_COLS), F32)] + [jax.ShapeDtypeStruct((1, W), F32)] * 4
                  + [jax.ShapeDtypeStruct((2 * LORA, 2 * W), F32)],
        compiler_params=_cparams(dimension_semantics=("arbitrary",)),
    )(p_rw, bnd, mu, w0, a0, k_k, k_a, lora, seg64, tri, *cts)


def _post_fwd(o, rk, v, g, gn_g, gn_b, r_k, seg64):
    T = o.shape[0]
    tm = _row_tile(T)
    W = RW_WIDTH

    def body(o_ref, rk_ref, v_ref, g_ref, gg_ref, gb_ref, rkp_ref, seg_ref, y_ref):
        y_ref[...] = _post_fn(o_ref[...], rk_ref[...], v_ref[...], g_ref[...], gg_ref[...], gb_ref[...],
                              rkp_ref[...], seg_ref[...]).astype(BF16)

    row = pl.BlockSpec((tm, W), lambda i: (i, 0))
    small = _full((1, W))
    return pl.pallas_call(
        body, name="rwkv_post_fwd", grid=(T // tm,),
        in_specs=[row] * 4 + [small] * 3 + [_full((W, W))],
        out_specs=row, out_shape=jax.ShapeDtypeStruct((T, W), BF16),
        compiler_params=_cparams(dimension_semantics=("arbitrary",)),
    )(o, rk, v, g, gn_g, gn_b, r_k, seg64)


def _post_bwd(o, rk, v, g, gn_g, gn_b, r_k, seg64, dy):
    T = o.shape[0]
    tm = _row_tile(T)
    W = RW_WIDTH

    def body(o_ref, rk_ref, v_ref, g_ref, gg_ref, gb_ref, rkp_ref, seg_ref, dy_ref,
             do_ref, drk_ref, dv_ref, dg_ref, dgg_ref, dgb_ref, drkp_ref):
        accs = (dgg_ref, dgb_ref, drkp_ref)

        @pl.when(pl.program_id(0) == 0)
        def _():
            for a_ref in accs:
                a_ref[...] = jnp.zeros_like(a_ref)

        seg = seg_ref[...]
        _, vjp = jax.vjp(lambda *a: _post_fn(*a, seg), o_ref[...], rk_ref[...], v_ref[...], g_ref[...],
                         gg_ref[...], gb_ref[...], rkp_ref[...])
        grads = vjp(dy_ref[...])
        for o_, gval in zip((do_ref, drk_ref, dv_ref, dg_ref), grads[:4]):
            o_[...] = gval
        for a_ref, gval in zip(accs, grads[4:]):
            a_ref[...] += gval

    row = pl.BlockSpec((tm, W), lambda i: (i, 0))
    small = _full((1, W))
    return pl.pallas_call(
        body, name="rwkv_post_bwd", grid=(T // tm,),
        in_specs=[row] * 4 + [small] * 3 + [_full((W, W)), pl.BlockSpec((tm, W), lambda i: (i, 1))],
        out_specs=[row] * 4 + [small] * 3,
        out_shape=[jax.ShapeDtypeStruct((T, W), F32)] * 4 + [jax.ShapeDtypeStruct((1, W), F32)] * 3,
        compiler_params=_cparams(dimension_semantics=("arbitrary",)),
    )(o, rk, v, g, gn_g, gn_b, r_k, seg64, dy)


def _wkv_lhs(hm_ref, rows):
    tiles = [jnp.transpose(hm_ref[p, :, rows, :].reshape(RW_HEADS * WKV_CHUNK, 128)) for p in range(N_PAIR)]
    hi, lo = _split(jnp.concatenate(tiles, axis=0)[:N_VEC * RW_HEAD])
    return jnp.concatenate([hi, lo], axis=1)


def _wkv_group(nch):
    return min(WKV_GROUP, nch)


N_STEP_VEC = N_VEC - 1
PAD_ROWS = 16


def _wkv_fwd(cols, v, e_tab):
    T = v.shape[0]
    Tc = WKV_CHUNK
    nch = T // Tc
    G = _wkv_group(nch)
    J, W = RW_HEAD, RW_WIDTH
    JS = N_STEP_VEC * J

    def body(cols_ref, v_ref, e_ref, o_ref, states_ref, sa_ref, s_ref):
        @pl.when(pl.program_id(0) == 0)
        def _():
            s_ref[...] = jnp.zeros_like(s_ref)

        st = s_ref[...]
        for c in range(G):
            lhs = _wkv_lhs(cols_ref, slice(c * Tc, (c + 1) * Tc))
            for t in range(Tc):
                row = slice(c * Tc + t, c * Tc + t + 1)
                ex = _dot(lhs[:JS], e_ref[t])
                states_ref[c * Tc + t] = st
                sa = -jnp.sum(st * ex[0:J], axis=0, keepdims=True)
                st = st + ex[J:2 * J] * sa + ex[2 * J:3 * J] * v_ref[row, :]
                sa_ref[row, :] = sa
                o_ref[row, :] = jnp.sum(st * ex[3 * J:4 * J], axis=0, keepdims=True)
            st = st * _dot(lhs[JS:], e_ref[Tc - 1])
        s_ref[...] = st

    GT = G * Tc
    return pl.pallas_call(
        body, name="wkv_fwd", grid=(nch // G,),
        in_specs=[pl.BlockSpec((N_PAIR, RW_HEADS, GT, 128), lambda c: (0, 0, c, 0)),
                  pl.BlockSpec((GT, W), lambda c: (c, 0)), _full((Tc, 2 * 128, W))],
        out_specs=[pl.BlockSpec((GT, W), lambda c: (c, 0)), pl.BlockSpec((GT, J, W), lambda c: (c, 0, 0)),
                   pl.BlockSpec((GT, W), lambda c: (c, 0))],
        out_shape=[jax.ShapeDtypeStruct((T, W), F32), jax.ShapeDtypeStruct((T, J, W), F32),
                   jax.ShapeDtypeStruct((T, W), F32)],
        scratch_shapes=[pltpu.VMEM((J, W), F32)],
        compiler_params=_cparams(dimension_semantics=("arbitrary",)),
    )(cols, v, e_tab)


def _wkv_bwd(cols, v, do, states, sa, e_tab, r_tab):
    T = v.shape[0]
    Tc = WKV_CHUNK
    nch = T // Tc
    G = _wkv_group(nch)
    ngr = nch // G
    J, W = RW_HEAD, RW_WIDTH
    JS = N_STEP_VEC * J
    blocks = [slice(b * 128, (b + 1) * 128) for b in range(W // 128)]

    def body(cols_ref, v_ref, do_ref, states_ref, sa_ref, e_ref, r_ref, dv_ref, dhm_ref, ds_ref):
        @pl.when(pl.program_id(0) == 0)
        def _():
            ds_ref[...] = jnp.zeros_like(ds_ref)

        d_carry = [ds_ref[:, b] for b in blocks]
        last = Tc - 1
        for c in reversed(range(G)):
            at = c * Tc
            rows = slice(at, at + Tc)
            lhs = _wkv_lhs(cols_ref, rows)
            ex = _dot(lhs, e_ref[last])
            dst, ends = [], []
            for i, b in enumerate(blocks):
                s_end = (states_ref[at + last, :, b] + ex[J:2 * J, b] * sa_ref[at + last:at + Tc, b]
                         + ex[2 * J:3 * J, b] * v_ref[at + last:at + Tc, b])
                ends.append((d_carry[i] * s_end).astype(BF16))
                dst.append(d_carry[i] * ex[JS:, b])
            d_decay = _dot(jnp.concatenate(ends, axis=1), r_ref[last])
            acc = jnp.zeros((JS + PAD_ROWS, 128), F32)
            for t in reversed(range(Tc)):
                row = slice(at + t, at + t + 1)
                if t != last:
                    ex = _dot(lhs[:JS], e_ref[t])
                dvs, prods = [], []
                for i, b in enumerate(blocks):
                    kk_e, b_e, k_e, r_e = (ex[n * J:(n + 1) * J, b] for n in range(N_STEP_VEC))
                    do_row, v_row, sa_row = do_ref[row, b], v_ref[row, b], sa_ref[row, b]
                    s_old = states_ref[at + t, :, b]
                    s_new = states_ref[at + t + 1, :, b] if t != last else s_old + b_e * sa_row + k_e * v_row
                    dsn = dst[i] + r_e * do_row
                    dsa = jnp.sum(dsn * b_e, axis=0, keepdims=True)
                    dvs.append(jnp.sum(dsn * k_e, axis=0, keepdims=True))
                    prods.append(jnp.concatenate(
                        [s_old * (-dsa), dsn * sa_row, dsn * v_row, s_new * do_row, jnp.zeros((PAD_ROWS, 128), F32)],
                        axis=0).astype(BF16))
                    dst[i] = dsn - kk_e * dsa
                dv_ref[row, :] = jnp.concatenate(dvs, axis=1)
                acc = acc + _dot(jnp.concatenate(prods, axis=1), r_ref[t])
            d_carry = dst
            tiles = jnp.concatenate([acc[:JS], d_decay, jnp.zeros((2 * N_PAIR * J - N_VEC * J, 128), F32)], axis=0)
            for p in range(N_PAIR):
                dhm_ref[p, :, rows, :] = jnp.transpose(tiles[p * 128:(p + 1) * 128]).reshape(RW_HEADS, Tc, 128)
        for i, b in enumerate(blocks):
            ds_ref[:, b] = d_carry[i]

    GT = G * Tc
    rev2 = lambda c: (ngr - 1 - c, 0)
    rev3 = lambda c: (ngr - 1 - c, 0, 0)
    rev_hm = lambda c: (0, 0, ngr - 1 - c, 0)
    hm_spec = pl.BlockSpec((N_PAIR, RW_HEADS, GT, 128), rev_hm)
    return pl.pallas_call(
        body, name="wkv_bwd", grid=(ngr,),
        in_specs=[hm_spec, pl.BlockSpec((GT, W), rev2), pl.BlockSpec((GT, W), rev2),
                  pl.BlockSpec((GT, J, W), rev3), pl.BlockSpec((GT, W), rev2),
                  _full((Tc, 2 * 128, W)), _full((Tc, W, 128))],
        out_specs=[pl.BlockSpec((GT, W), rev2), hm_spec],
        out_shape=[jax.ShapeDtypeStruct((T, W), F32), jax.ShapeDtypeStruct((N_PAIR, RW_HEADS, T, 128), F32)],
        scratch_shapes=[pltpu.VMEM((J, W), F32)],
        compiler_params=_cparams(dimension_semantics=("arbitrary",)),
    )(cols, v, do, states, sa, e_tab, r_tab)


def _outproj(x, y_ret, y_rw, w_out_b, target, gf):
    T = x.shape[0]
    tm = _row_tile(T)
    W = RW_WIDTH

    def body(x_ref, yr_ref, yw_ref, w_ref, t_ref, gf_ref, loss_ref, dh_ref, dy_ref, dw_ref, dgf_ref):
        @pl.when(pl.program_id(0) == 0)
        def _():
            loss_ref[...] = jnp.zeros_like(loss_ref)
            dw_ref[...] = jnp.zeros_like(dw_ref)
            dgf_ref[...] = jnp.zeros_like(dgf_ref)

        y = jnp.concatenate([yr_ref[...], yw_ref[...]], axis=1)
        w = w_ref[...]
        h = x_ref[...] + _dot(y, w)
        rstd = lax.rsqrt(jnp.mean(h * h, axis=-1, keepdims=True) + RMS_EPS)
        hn = h * rstd
        gfv = gf_ref[...]
        err = hn * gfv - t_ref[...]
        loss_ref[...] += 0.5 * jnp.sum(jnp.mean(err * err, axis=-1))
        dout = err * (1.0 / D_MODEL)
        dgf_ref[...] += jnp.sum(dout * hn, axis=0, keepdims=True)
        dhn = dout * gfv
        dh = rstd * (dhn - hn * jnp.mean(dhn * hn, axis=-1, keepdims=True))
        dh_ref[...] = dh
        dhb = dh.astype(BF16)
        dy_ref[...] = _dot_nt(dhb, w)
        dw_ref[...] += _dot_tn(y, dhb)

    return pl.pallas_call(
        body, name="outproj_loss", grid=(T // tm,),
        in_specs=[pl.BlockSpec((tm, D_MODEL), lambda i: (i, 0)), pl.BlockSpec((tm, W), lambda i: (i, 0)),
                  pl.BlockSpec((tm, W), lambda i: (i, 0)), _full((D_MODEL, D_MODEL)),
                  pl.BlockSpec((tm, D_MODEL), lambda i: (i, 0)), _full((1, D_MODEL))],
        out_specs=[_full((1, PACK_W)), pl.BlockSpec((tm, D_MODEL), lambda i: (i, 0)),
                   pl.BlockSpec((tm, D_MODEL), lambda i: (i, 0)), _full((D_MODEL, D_MODEL)), _full((1, D_MODEL))],
        out_shape=[jax.ShapeDtypeStruct((1, PACK_W), F32), jax.ShapeDtypeStruct((T, D_MODEL), F32),
                   jax.ShapeDtypeStruct((T, D_MODEL), F32), jax.ShapeDtypeStruct((D_MODEL, D_MODEL), F32),
                   jax.ShapeDtypeStruct((1, D_MODEL), F32)],
        compiler_params=_cparams(dimension_semantics=("arbitrary",)),
    )(x, y_ret, y_rw, w_out_b, target, gf)


def _inproj_bwd_x(dp_ret, dp_rw, dprev, dfirst, w_t, x, norm_g, dh):
    T = x.shape[0]
    tm = _row_tile(T)
    nt = T // tm

    def body(dpr_ref, dpw_ref, dprev_ref, dnext_ref, w_ref, x_ref, g_ref, dh_ref, gx_ref, dg_ref, dpt_ref):
        @pl.when(pl.program_id(0) == 0)
        def _():
            dg_ref[...] = jnp.zeros_like(dg_ref)

        next_row = jnp.where(pl.program_id(0) == nt - 1, 0.0, dnext_ref[0])
        dp = jnp.concatenate([dpr_ref[...], dpw_ref[...] + _shift_up(dprev_ref[...], next_row)], axis=1)
        dpt_ref[...] = jnp.transpose(dp).astype(BF16)
        du = _dot(dp.astype(BF16), w_ref[...])
        xf = x_ref[...]
        rstd = lax.rsqrt(jnp.mean(xf * xf, axis=-1, keepdims=True) + RMS_EPS)
        xn = xf * rstd
        dg_ref[...] += jnp.sum(du * xn, axis=0, keepdims=True)
        dxn = du * g_ref[...]
        gx_ref[...] = dh_ref[...] + rstd * (dxn - xn * jnp.mean(dxn * xn, axis=-1, keepdims=True))

    return pl.pallas_call(
        body, name="inproj_bwd_x", grid=(T // tm,),
        in_specs=[pl.BlockSpec((tm, RET_COLS), lambda i: (i, 0)), pl.BlockSpec((tm, RW_COLS), lambda i: (i, 0)),
                  pl.BlockSpec((tm, RW_COLS), lambda i: (i, 0)),
                  pl.BlockSpec((1, 1, RW_COLS), lambda i: (jnp.minimum(i + 1, nt - 1), 0, 0)),
                  _full((IN_COLS, D_MODEL)), pl.BlockSpec((tm, D_MODEL), lambda i: (i, 0)), _full((1, D_MODEL)),
                  pl.BlockSpec((tm, D_MODEL), lambda i: (i, 0))],
        out_specs=[pl.BlockSpec((tm, D_MODEL), lambda i: (i, 0)), _full((1, D_MODEL)),
                   pl.BlockSpec((IN_COLS, tm), lambda i: (0, i))],
        out_shape=[jax.ShapeDtypeStruct((T, D_MODEL), F32), jax.ShapeDtypeStruct((1, D_MODEL), F32),
                   jax.ShapeDtypeStruct((IN_COLS, T), BF16)],
        compiler_params=_cparams(dimension_semantics=("arbitrary",)),
    )(dp_ret, dp_rw, dprev, dfirst, w_t, x, norm_g, dh)


def _inproj_bwd_w(dp_t, u):
    T = u.shape[0]

    def body(d_ref, u_ref, o_ref):
        o_ref[...] = _dot(d_ref[...], u_ref[...])

    return pl.pallas_call(
        body, name="inproj_bwd_w", grid=(N_CHIPS,),
        in_specs=[pl.BlockSpec((IN_SHARD, T), lambda i: (i, 0)), _full((T, D_MODEL))],
        out_specs=pl.BlockSpec((IN_SHARD, D_MODEL), lambda i: (i, 0)),
        out_shape=jax.ShapeDtypeStruct((IN_COLS, D_MODEL), F32),
        compiler_params=_cparams(dimension_semantics=("arbitrary",)),
    )(dp_t, u)


def _with_own(gathered, own, chip):
    return lax.dynamic_update_slice(gathered, own[None], (chip, 0, 0, 0))


def _local_step(x, target, w_in_t, late, chip, small):
    T = x.shape[0]
    tm = _row_tile(T)
    W = RW_WIDTH
    cos, sin = _rope_tables(T)
    tabs = _ret_tables()
    seg64 = _seg_matrix(RW_WIDTH, RW_HEAD)
    e_tab = _wkv_expand_table()
    r_tab = _wkv_reduce_table()

    p_ret, p_rw, u, bnd, *gathered = _inproj_fwd(x, small["norm_g"], w_in_t, late)
    g_out, g_lw, g_la = (_with_own(g, own, chip) for g, own in zip(gathered, late))
    w_out_b = g_out.reshape(D_MODEL, D_MODEL)
    lw = jnp.transpose(g_lw.reshape(N_CHIPS, LORA, LORA_SHARD), (1, 0, 2)).reshape(LORA, W)
    la = jnp.transpose(g_la.reshape(N_CHIPS, LORA, LORA_SHARD), (1, 0, 2)).reshape(LORA, W)
    zero = jnp.zeros((LORA, W), F32)
    lora = jnp.concatenate([jnp.concatenate([lw, zero], axis=1), jnp.concatenate([zero, la], axis=1)], axis=0)
    prep_w = (small["rwkv_mu"], small["w0"], small["a0"], small["k_k"], small["k_a"], lora, seg64, _chunk_tables(tm))
    post_w = (small["rwkv_gn_g"], small["rwkv_gn_b"], small["r_k"], seg64)

    y_ret, ret, s_in_all = _ret_fwd(p_ret, cos, sin, tabs, small["ret_gn_g"])
    hm, v, g, rk = _prep_fwd(p_rw, bnd, *prep_w)
    o, states, sa = _wkv_fwd(hm, v, e_tab)
    y_rw = _post_fwd(o, rk, v, g, *post_w)
    loss, dh, dy, d_w_out, d_gf = _outproj(x, y_ret, y_rw, w_out_b, target, small["final_norm_g"])

    do, d_rk, dv2, dg, d_gn_g, d_gn_b, d_r_k = _post_bwd(o, rk, v, g, *post_w, dy)
    dv1, d_hm = _wkv_bwd(hm, v, do, states, sa, e_tab, r_tab)
    dp_rw, dprev, dfirst, d_mu, d_w0, d_a0, d_k_k, d_k_a, d_lora = _prep_bwd(
        p_rw, bnd, *prep_w, (d_hm, d_rk, dv1, dv2, dg))
    dp_ret, d_ret_gn = _ret_bwd(p_ret, cos, sin, tabs, small["ret_gn_g"], ret, s_in_all, dy)
    grad_x, d_norm_g, dp_t = _inproj_bwd_x(dp_ret, dp_rw, dprev, dfirst, w_in_t, x, small["norm_g"], dh)
    d_w_in = _inproj_bwd_w(dp_t, u)

    d_small = {"norm_g": d_norm_g, "ret_gn_g": d_ret_gn, "rwkv_mu": d_mu, "w0": d_w0, "a0": d_a0, "k_k": d_k_k,
               "k_a": d_k_a, "r_k": d_r_k, "rwkv_gn_g": d_gn_g, "rwkv_gn_b": d_gn_b, "final_norm_g": d_gf}
    return loss, grad_x, d_w_in, d_w_out, d_lora, d_small


ANY = pl.BlockSpec(memory_space=pl.ANY)
CHIP_FLIPS = ((0, 1), (1, 0), (1, 1))
N_FLIPS = len(CHIP_FLIPS)
LORA_SHARD = RW_WIDTH // N_CHIPS
HALF_IN = IN_SHARD // 2
HALF_OUT = OUT_SHARD // 2


def _position():
    return lax.axis_index("x"), lax.axis_index("y"), lax.axis_index("c")


def _flip(v, f):
    return 1 - v if f else v


def _finish(local, remote, landed):
    for cp in landed:
        cp.wait_recv()
    for cp in remote:
        cp.wait_send()
    for cp in local:
        cp.wait()


def _gather_copies(ins, outs, sems):
    send, recv, pass_send, pass_recv = sems
    x, y, c = _position()
    s = 2 * x + y
    sibling = (x, y, 1 - c)
    first, second = (1 - c, c), (c, 1 - c)
    (x1, y1), (x2, y2) = ((x + fx - 2 * x * fx, y + fy - 2 * y * fy) for fx, fy in (first, second))
    s1, s2, sd = 2 * x1 + y1, 2 * x2 + y2, 2 * (1 - x) + (1 - y)

    def copy(src, dst, pair, k, to):
        return pltpu.make_async_remote_copy(src_ref=src, dst_ref=dst, send_sem=pair[0].at[k], recv_sem=pair[1].at[k],
                                            device_id=to, device_id_type=MESH)

    at_once, direct, relayed, passed_in, sends = [], [], [], [], []
    for a in range(len(ins)):
        k = a * N_FLIPS
        own, out = ins[a].at[c], outs[a]
        ici = (send, recv)
        to_first = copy(own, out.at[s, c], ici, k, (x1, y1, c))
        to_second = copy(own, out.at[s, c], ici, k + 1, (x2, y2, c))
        relay = copy(out.at[s1, c], out.at[s1, c], ici, k + 2, (x2, y2, c))
        passes = [copy(out.at[slot, c], out.at[slot, c], (pass_send, pass_recv), k + j, sibling)
                  for j, slot in enumerate((s1, s2, sd))]
        at_once += [to_first, to_second]
        direct += [(copy(own, out.at[s1, c], ici, k, (x1, y1, c)), [relay, passes[0]]),
                   (copy(own, out.at[s2, c], ici, k + 1, (x2, y2, c)), [passes[1]])]
        relayed += [(copy(own, out.at[sd, c], ici, k + 2, (x2, y2, c)), [passes[2]])]
        passed_in += [copy(out.at[slot, 1 - c], out.at[slot, 1 - c], (pass_send, pass_recv), k + j, sibling)
                      for j, slot in enumerate((s2, s1, sd))]
        sends += [to_first, to_second, relay] + passes
    return at_once, direct, relayed, passed_in, sends


def _then(steps):
    for arrived, then in steps:
        arrived.wait_recv()
        for cp in then:
            cp.start()


def _gather_start(copies):
    for cp in copies[0]:
        cp.start()


def _gather_relay(copies):
    _then(copies[1])


def _gather_finish(copies):
    _, _, relayed, passed_in, sends = copies
    _then(relayed)
    _finish([], sends, passed_in)


def _gather_sems(n):
    return [pltpu.SemaphoreType.DMA((n * N_FLIPS,))] * 4


def _gather_shapes(arrs):
    return [jax.ShapeDtypeStruct((N_CHIPS,) + a.shape, a.dtype) for a in arrs]


def _gather_chips(arrs):
    n = len(arrs)

    def body(*refs):
        copies = _gather_copies(refs[:n], refs[n:2 * n], refs[2 * n:])
        _gather_start(copies)
        _gather_relay(copies)
        _gather_finish(copies)

    return pl.pallas_call(
        body, name="gather_weights",
        in_specs=[ANY] * n, out_specs=[ANY] * n,
        out_shape=_gather_shapes(arrs), scratch_shapes=_gather_sems(n),
    )(*arrs)


def _pair_exchange(g_in, g_out, g_small):
    def body(gi_ref, go_ref, gs_ref, li_ref, lo_ref, ls_ref, send, recv):
        x, y, c = _position()
        peer = (x, y, 1 - c)
        srcs = (gi_ref.at[:, pl.ds((1 - c) * HALF_IN, HALF_IN), :], go_ref.at[:, pl.ds((1 - c) * HALF_OUT, HALF_OUT), :],
                gs_ref)
        remote = [pltpu.make_async_remote_copy(src_ref=src, dst_ref=dst, send_sem=send.at[k], recv_sem=recv.at[k],
                                               device_id=peer, device_id_type=MESH)
                  for k, (src, dst) in enumerate(zip(srcs, (li_ref, lo_ref, ls_ref)))]
        for cp in remote:
            cp.start()
        _finish([], remote, remote)

    return pl.pallas_call(
        body, name="pair_exchange",
        in_specs=[ANY] * 3, out_specs=[ANY] * 3,
        out_shape=[jax.ShapeDtypeStruct((N_CHIPS, HALF_IN, D_MODEL), F32),
                   jax.ShapeDtypeStruct((N_CHIPS, HALF_OUT, D_MODEL), F32),
                   jax.ShapeDtypeStruct(g_small.shape, F32)],
        scratch_shapes=[pltpu.SemaphoreType.DMA((3,)), pltpu.SemaphoreType.DMA((3,))],
    )(g_in, g_out, g_small)


def _pair_sum(g_in, g_out, g_small, l_in, l_out, l_small, c_arr):
    def body(c_ref, gi_ref, go_ref, gs_ref, li_ref, lo_ref, ls_ref, ci_ref, co_ref, cs_ref):
        ci_ref[...] = (gi_ref[...] + li_ref[...]).astype(BF16)
        co_ref[...] = (go_ref[...] + lo_ref[...]).astype(BF16)

        @pl.when(pl.program_id(0) == 0)
        def _():
            cs_ref[...] = gs_ref[...] + ls_ref[...]

    nd = g_small.shape
    return pl.pallas_call(
        body, name="pair_sum",
        grid_spec=pltpu.PrefetchScalarGridSpec(
            num_scalar_prefetch=1, grid=(N_CHIPS,),
            in_specs=[pl.BlockSpec((1, HALF_IN, D_MODEL), lambda s, c: (s, c[0], 0)),
                      pl.BlockSpec((1, HALF_OUT, D_MODEL), lambda s, c: (s, c[0], 0)),
                      pl.BlockSpec(nd, lambda s, c: (0, 0)),
                      pl.BlockSpec((1, HALF_IN, D_MODEL), lambda s, c: (s, 0, 0)),
                      pl.BlockSpec((1, HALF_OUT, D_MODEL), lambda s, c: (s, 0, 0)),
                      pl.BlockSpec(nd, lambda s, c: (0, 0))],
            out_specs=[pl.BlockSpec((1, HALF_IN, D_MODEL), lambda s, c: (s, 0, 0)),
                       pl.BlockSpec((1, HALF_OUT, D_MODEL), lambda s, c: (s, 0, 0)),
                       pl.BlockSpec(nd, lambda s, c: (0, 0))]),
        out_shape=[jax.ShapeDtypeStruct((N_CHIPS, HALF_IN, D_MODEL), BF16),
                   jax.ShapeDtypeStruct((N_CHIPS, HALF_OUT, D_MODEL), BF16), jax.ShapeDtypeStruct(nd, F32)],
        compiler_params=_cparams(dimension_semantics=("arbitrary",)),
    )(c_arr, g_in, g_out, g_small, l_in, l_out, l_small)


def _chip_exchange(c_in, c_out, c_small):
    def body(ci_ref, co_ref, cs_ref, li_ref, lo_ref, ls_ref, send, recv):
        x, y, c = _position()
        s = 2 * x + y
        remote = []
        for j, (fx, fy) in enumerate(CHIP_FLIPS):
            px, py = _flip(x, fx), _flip(y, fy)
            ps = 2 * px + py
            for a, (src, dst) in enumerate(((ci_ref.at[ps], li_ref.at[j]), (co_ref.at[ps], lo_ref.at[j]),
                                            (cs_ref, ls_ref.at[j]))):
                k = 3 * j + a
                remote.append(pltpu.make_async_remote_copy(src_ref=src, dst_ref=dst, send_sem=send.at[k],
                                                           recv_sem=recv.at[k], device_id=(px, py, c),
                                                           device_id_type=MESH))
        for cp in remote:
            cp.start()
        _finish([], remote, remote)

    return pl.pallas_call(
        body, name="chip_exchange",
        in_specs=[ANY] * 3, out_specs=[ANY] * 3,
        out_shape=[jax.ShapeDtypeStruct((N_FLIPS, HALF_IN, D_MODEL), c_in.dtype),
                   jax.ShapeDtypeStruct((N_FLIPS, HALF_OUT, D_MODEL), c_out.dtype),
                   jax.ShapeDtypeStruct((N_FLIPS,) + c_small.shape, F32)],
        scratch_shapes=[pltpu.SemaphoreType.DMA((3 * N_FLIPS,)), pltpu.SemaphoreType.DMA((3 * N_FLIPS,))],
    )(c_in, c_out, c_small)


def _chip_sum(g_in, g_out, p_in, p_out, c_small, l_in, l_out, l_small, sc_arr):
    nd = c_small.shape

    def body(s_ref, gi_ref, go_ref, pi_ref, po_ref, cs_ref, li0, li1, li2, lo0, lo1, lo2, ls_ref,
             ri_ref, ro_ref, rs_ref):
        ri_ref[...] = (((gi_ref[0] + pi_ref[0]) + li0[0].astype(F32)) + li1[0].astype(F32)) + li2[0].astype(F32)
        ro_ref[...] = (((go_ref[0] + po_ref[0]) + lo0[0].astype(F32)) + lo1[0].astype(F32)) + lo2[0].astype(F32)
        me = s_ref[0]
        parts = (cs_ref[...], ls_ref[0], ls_ref[1], ls_ref[2])

        def of_chip(s):
            m = jnp.bitwise_xor(me, s)
            return jnp.where(m == 0, parts[0], jnp.where(m == 1, parts[1], jnp.where(m == 2, parts[2], parts[3])))

        rs_ref[...] = ((of_chip(0) + of_chip(1)) + of_chip(2)) + of_chip(3)

    def flip_in(j):
        return pl.BlockSpec((1, HALF_IN, D_MODEL), lambda i, s: (j, 0, 0))

    def flip_out(j):
        return pl.BlockSpec((1, HALF_OUT, D_MODEL), lambda i, s: (j, 0, 0))

    return pl.pallas_call(
        body, name="chip_sum",
        grid_spec=pltpu.PrefetchScalarGridSpec(
            num_scalar_prefetch=1, grid=(1,),
            in_specs=[pl.BlockSpec((1, HALF_IN, D_MODEL), lambda i, s: (s[0], s[1], 0)),
                      pl.BlockSpec((1, HALF_OUT, D_MODEL), lambda i, s: (s[0], s[1], 0)),
                      pl.BlockSpec((1, HALF_IN, D_MODEL), lambda i, s: (s[0], 0, 0)),
                      pl.BlockSpec((1, HALF_OUT, D_MODEL), lambda i, s: (s[0], 0, 0)),
                      pl.BlockSpec(nd, lambda i, s: (0, 0)),
                      flip_in(0), flip_in(1), flip_in(2), flip_out(0), flip_out(1), flip_out(2),
                      pl.BlockSpec((N_FLIPS,) + nd, lambda i, s: (0, 0, 0))],
            out_specs=[pl.BlockSpec((HALF_IN, D_MODEL), lambda i, s: (0, 0)),
                       pl.BlockSpec((HALF_OUT, D_MODEL), lambda i, s: (0, 0)),
                       pl.BlockSpec(nd, lambda i, s: (0, 0))]),
        out_shape=[jax.ShapeDtypeStruct((HALF_IN, D_MODEL), F32), jax.ShapeDtypeStruct((HALF_OUT, D_MODEL), F32),
                   jax.ShapeDtypeStruct(nd, F32)],
        compiler_params=_cparams(dimension_semantics=("arbitrary",)),
    )(sc_arr, g_in, g_out, p_in, p_out, c_small, l_in, l_in, l_in, l_out, l_out, l_out, l_small)


def _pair_share(r_in, r_out):
    def body(ri_ref, ro_ref, li_ref, lo_ref, send, recv):
        x, y, c = _position()
        remote = [pltpu.make_async_remote_copy(src_ref=src, dst_ref=dst, send_sem=send.at[k], recv_sem=recv.at[k],
                                               device_id=(x, y, 1 - c), device_id_type=MESH)
                  for k, (src, dst) in enumerate(((ri_ref, li_ref), (ro_ref, lo_ref)))]
        for cp in remote:
            cp.start()
        _finish([], remote, remote)

    return pl.pallas_call(
        body, name="pair_share",
        in_specs=[ANY] * 2, out_specs=[ANY] * 2,
        out_shape=[jax.ShapeDtypeStruct(r_in.shape, F32), jax.ShapeDtypeStruct(r_out.shape, F32)],
        scratch_shapes=[pltpu.SemaphoreType.DMA((2,)), pltpu.SemaphoreType.DMA((2,))],
    )(r_in, r_out)


def _adam_update(w, g, m, v):
    mn = ADAM_B1 * m + (1.0 - ADAM_B1) * g
    vn = ADAM_B2 * v + (1.0 - ADAM_B2) * jnp.square(g)
    m_hat = mn / (1.0 - ADAM_B1 ** ADAM_STEP)
    v_hat = vn / (1.0 - ADAM_B2 ** ADAM_STEP)
    return -ADAM_LR * (m_hat / (jnp.sqrt(v_hat) + ADAM_EPS) + ADAM_WD * w), mn, vn


def _adamw(name, w, g_mine, g_theirs, m, v, core_arr, tr):
    rows, cols = w.shape
    per_half = rows // 2 // tr

    def body(c_ref, w_ref, gm_ref, gt_ref, m_ref, v_ref, g_ref, d_ref, nm_ref, nv_ref):
        mine = (pl.program_id(0) // per_half) == c_ref[0]
        g = jnp.where(mine, gm_ref[...], gt_ref[...])
        d, mn, vn = _adam_update(w_ref[...], g, m_ref[...], v_ref[...])
        g_ref[...] = g
        d_ref[...] = d
        nm_ref[...] = mn
        nv_ref[...] = vn

    spec = pl.BlockSpec((tr, cols), lambda i, c: (i, 0))
    half = pl.BlockSpec((tr, cols), lambda i, c: (i % per_half, 0))
    return pl.pallas_call(
        body, name=name,
        grid_spec=pltpu.PrefetchScalarGridSpec(
            num_scalar_prefetch=1, grid=(rows // tr,),
            in_specs=[spec, half, half, spec, spec], out_specs=[spec] * 4),
        out_shape=[jax.ShapeDtypeStruct((rows, cols), F32)] * 4,
        compiler_params=_cparams(dimension_semantics=("arbitrary",)),
    )(core_arr, w, g_mine, g_theirs, m, v)


def _row_pieces(n):
    return [(k, k * PACK_W, min(PACK_W, n - k * PACK_W)) for k in range(-(-n // PACK_W))]


def _pack_small(d_small, loss, d_lora):
    ns = len(SMALL_NAMES)

    def body(*refs):
        small_refs, (loss_ref, lora_ref, out_ref) = refs[:ns], refs[ns:]
        out_ref[...] = jnp.zeros_like(out_ref)
        out_ref[PACK_LORA_W:PACK_LORA_W + LORA, :] = lora_ref[:LORA, :RW_WIDTH]
        out_ref[PACK_LORA_A:PACK_LORA_A + LORA, :] = lora_ref[LORA:, RW_WIDTH:]
        for name, n, ref in zip(SMALL_NAMES, SMALL_SIZES, small_refs):
            for k, at, w in _row_pieces(n):
                out_ref[PACK_AT[name] + k:PACK_AT[name] + k + 1, 0:w] = ref[:, at:at + w]
        out_ref[PACK_LOSS:PACK_LOSS + 1, :] = loss_ref[...]

    return pl.pallas_call(body, name="pack_small", out_shape=jax.ShapeDtypeStruct((PACK_ROWS, PACK_W), F32),
                          compiler_params=_cparams())(*d_small, loss, d_lora)


def _adamw_small(tot, chip_arr, ws, ms, vs):
    ns = len(SMALL_NAMES)
    n_par = ns + 2

    def body(s_ref, tot_ref, glw_ref, gla_ref, *refs):
        w_refs, m_refs, v_refs = refs[:n_par], refs[n_par:2 * n_par], refs[2 * n_par:3 * n_par]
        outs = refs[3 * n_par:]
        g_refs, d_refs, nm_refs, nv_refs = (outs[i * n_par:(i + 1) * n_par] for i in range(4))
        grads = [jnp.concatenate([tot_ref[PACK_AT[name] + k:PACK_AT[name] + k + 1, 0:w] for k, _, w in _row_pieces(n)],
                                 axis=1) for name, n in zip(SMALL_NAMES, SMALL_SIZES)]
        grads += [glw_ref[...], gla_ref[...]]
        for i, g in enumerate(grads):
            d, mn, vn = _adam_update(w_refs[i][...], g, m_refs[i][...], v_refs[i][...])
            g_refs[i][...] = g
            d_refs[i][...] = d
            nm_refs[i][...] = mn
            nv_refs[i][...] = vn

    def whole(a):
        nd = a.ndim
        return pl.BlockSpec(a.shape, lambda i, s: (0,) * nd)

    shard = (LORA, LORA_SHARD)
    par_specs = [whole(a) for a in ws]
    res = pl.pallas_call(
        body, name="adamw_small",
        grid_spec=pltpu.PrefetchScalarGridSpec(
            num_scalar_prefetch=1, grid=(1,),
            in_specs=[whole(tot), pl.BlockSpec(shard, lambda i, s: (PACK_LORA_W // LORA, s[0])),
                      pl.BlockSpec(shard, lambda i, s: (PACK_LORA_A // LORA, s[0]))] + par_specs * 3,
            out_specs=par_specs * 4),
        out_shape=[jax.ShapeDtypeStruct(a.shape, F32) for a in ws] * 4,
        compiler_params=_cparams(dimension_semantics=("arbitrary",)),
    )(chip_arr, tot, tot, tot, *ws, *ms, *vs)
    return [res[i * n_par:(i + 1) * n_par] for i in range(4)]


def kernel(x, norm_g, w_in, ret_gn_g, rwkv_mu, w_lora_up, w0, a_lora_up, a0, k_k, k_a, r_k, rwkv_gn_g, rwkv_gn_b, w_out, final_norm_g, loss_target, m_norm_g, m_w_in, m_ret_gn_g, m_rwkv_mu, m_w_lora_up, m_w0, m_a_lora_up, m_a0, m_k_k, m_k_a, m_r_k, m_rwkv_gn_g, m_rwkv_gn_b, m_w_out, m_final_norm_g, v_norm_g, v_w_in, v_ret_gn_g, v_rwkv_mu, v_w_lora_up, v_w0, v_a_lora_up, v_a0, v_k_k, v_k_a, v_r_k, v_rwkv_gn_g, v_rwkv_gn_b, v_w_out, v_final_norm_g):
    W = RW_WIDTH
    params = dict(norm_g=norm_g, ret_gn_g=ret_gn_g, rwkv_mu=rwkv_mu, w0=w0, a0=a0, k_k=k_k, k_a=k_a, r_k=r_k,
                  rwkv_gn_g=rwkv_gn_g, rwkv_gn_b=rwkv_gn_b, final_norm_g=final_norm_g)
    moments_m = dict(norm_g=m_norm_g, ret_gn_g=m_ret_gn_g, rwkv_mu=m_rwkv_mu, w0=m_w0, a0=m_a0, k_k=m_k_k, k_a=m_k_a,
                     r_k=m_r_k, rwkv_gn_g=m_rwkv_gn_g, rwkv_gn_b=m_rwkv_gn_b, final_norm_g=m_final_norm_g)
    moments_v = dict(norm_g=v_norm_g, ret_gn_g=v_ret_gn_g, rwkv_mu=v_rwkv_mu, w0=v_w0, a0=v_a0, k_k=v_k_k, k_a=v_k_a,
                     r_k=v_r_k, rwkv_gn_g=v_rwkv_gn_g, rwkv_gn_b=v_rwkv_gn_b, final_norm_g=v_final_norm_g)
    xi, yi, ci = _position()
    chip = (2 * xi + yi).astype(jnp.int32)

    def halves(a):
        return a.reshape(2, a.shape[0] // 2, a.shape[1])

    w_t, m_t, v_t = (jnp.transpose(a[0]) for a in (w_in, m_w_in, v_w_in))
    own_in = halves(w_t.astype(BF16))
    w_in_t = _with_own(_gather_chips([own_in])[0], own_in, chip).reshape(IN_COLS, D_MODEL)
    late = [halves(w_out[0].astype(BF16)), halves(w_lora_up[0]), halves(a_lora_up[0])]
    small = {n: params[n].reshape(1, -1) for n in SMALL_NAMES}

    loss, grad_x, d_w_in, d_w_out, d_lora, d_small = _local_step(x[0], loss_target[0], w_in_t, late, chip, small)

    core = ci.astype(jnp.int32)
    gi = d_w_in.reshape(N_CHIPS, IN_SHARD, D_MODEL)
    go = d_w_out.reshape(N_CHIPS, OUT_SHARD, D_MODEL)
    gs = _pack_small([d_small[n] for n in SMALL_NAMES], loss, d_lora)
    p_in, p_out, p_small = _pair_exchange(gi, go, gs)
    c_in, c_out, c_small = _pair_sum(gi, go, gs, p_in, p_out, p_small, core.reshape(1))
    l_in, l_out, l_small = _chip_exchange(c_in, c_out, c_small)
    r_in, r_out, tot = _chip_sum(gi, go, p_in, p_out, c_small, l_in, l_out, l_small, jnp.stack([chip, core]))
    t_in, t_out = _pair_share(r_in, r_out)

    grad_w_in, d_in, nm_in, nv_in = (jnp.transpose(a) for a in _adamw(
        "adamw_w_in", w_t, r_in, t_in, m_t, v_t, core.reshape(1), HALF_IN // 2))
    grad_w_out, d_out, nm_out, nv_out = _adamw("adamw_w_out", w_out[0], r_out, t_out, m_w_out[0], v_w_out[0],
                                               core.reshape(1), HALF_OUT)
    par_names = SMALL_NAMES + ("w_lora_up", "a_lora_up")

    def operands(tree, lw_, la_):
        return [tree[n].reshape(1, -1) for n in SMALL_NAMES] + [lw_[0], la_[0]]

    res = _adamw_small(tot, chip.reshape(1), operands(params, w_lora_up, a_lora_up),
                       operands(moments_m, m_w_lora_up, m_a_lora_up), operands(moments_v, v_w_lora_up, v_a_lora_up))

    names = ("norm_g", "w_in", "ret_gn_g", "rwkv_mu", "w_lora_up", "w0", "a_lora_up", "a0", "k_k", "k_a", "r_k",
             "rwkv_gn_g", "rwkv_gn_b", "w_out", "final_norm_g")
    shapes = dict(w_in=w_in.shape, w_out=w_out.shape, w_lora_up=w_lora_up.shape, a_lora_up=a_lora_up.shape,
                  **{n: params[n].shape for n in SMALL_NAMES})

    def leaves(pars, big_in, big_out):
        tree = dict(zip(par_names, pars), w_in=big_in, w_out=big_out)
        return [tree[n].reshape(shapes[n]) for n in names]

    grads = leaves(res[0], grad_w_in, grad_w_out)
    deltas = leaves(res[1], d_in, d_out)
    new_m = leaves(res[2], nm_in, nm_out)
    new_v = leaves(res[3], nv_in, nv_out)
    return (tot[PACK_LOSS, 0], grad_x.reshape(x.shape), *grads, *deltas, *new_m, *new_v)
```
